```python
import math
import jax
import jax.numpy as jnp
from jax import lax
import numpy as np

D_MODEL = 2048
BATCH = 32
SEQ = 256
DEPTH = 2
DEC_BATCH = 4
DEC_SEQ = 4096
PAST_LEN = 512

GRID_W = 64
BLOCK = 128
SSD_HEADS = 12
SSD_HEAD_DIM = 64
SSD_WIDTH = SSD_HEADS * SSD_HEAD_DIM
SSD_GROUPS = 2
SSD_D_STATE = 64
SSD_CONV_W = 3
SSD_CHUNK = 128
SSD_CONV_CH = SSD_WIDTH + 2 * SSD_GROUPS * SSD_D_STATE
ATT_HEADS = 12
ATT_KV_HEADS = 4
ATT_GROUP = ATT_HEADS // ATT_KV_HEADS
HEAD_DIM = 64
ATT_WIDTH = ATT_HEADS * HEAD_DIM
ATT_KV_WIDTH = ATT_KV_HEADS * HEAD_DIM
WINDOW = 128
ROPE_PER_AXIS = HEAD_DIM // 4
ROPE_BASE = 10000.0
S5_GROUPS = 32
S5_GROUP_CH = 16
S5_WIDTH = S5_GROUPS * S5_GROUP_CH
S5_P = 64
MIX_WIDTH = SSD_WIDTH + ATT_WIDTH + S5_WIDTH
D_FF = -(-(8 * D_MODEL) // (3 * 256)) * 256
OFF_Z = SSD_WIDTH
OFF_XBC = OFF_Z + SSD_CONV_CH
OFF_DT = OFF_XBC + 2 * SSD_HEADS
OFF_Q = OFF_DT + ATT_WIDTH
OFF_K = OFF_Q + ATT_KV_WIDTH
OFF_V = OFF_K + ATT_KV_WIDTH
D_IN = OFF_V + S5_WIDTH
IN_SPLITS = (OFF_Z, OFF_XBC, OFF_DT, OFF_Q, OFF_K, OFF_V)

kernel_name = 'hybrid_ssd_swa_s5_diffusion_step'


def rmsnorm(x, g, eps=1e-6):
    xf = x.astype(jnp.float32)
    y = xf * lax.rsqrt(jnp.mean(xf * xf, axis=-1, keepdims=True) + eps)
    return (y * g.astype(jnp.float32)).astype(x.dtype)


def dwconv_centred(x, w, b):
    k = w.shape[0]
    y = lax.conv_general_dilated(x, w[:, None, :].astype(x.dtype), window_strides=(1,),
                                 padding=[(k // 2, k // 2)],
                                 dimension_numbers=('NWC', 'WIO', 'NWC'),
                                 feature_group_count=w.shape[1])
    return y + b.astype(x.dtype)


def ssd_scan(x, dt, a_neg, bm, cm, h0):
    b, L, nh, hp = x.shape
    n = bm.shape[-1]
    q = SSD_CHUNK
    nc = L // q
    xc = (x * dt[..., None]).reshape(b, nc, q, nh, hp)
    bc = bm.reshape(b, nc, q, nh, n)
    cc = cm.reshape(b, nc, q, nh, n)
    cum = jnp.cumsum((dt * a_neg).reshape(b, nc, q, nh), axis=2)
    tri = jnp.tril(jnp.ones((q, q), dtype=bool))[None, None, :, :, None]
    seg = cum[:, :, :, None, :] - cum[:, :, None, :, :]
    decay = jnp.exp(jnp.where(tri, seg, -jnp.inf))
    scores = jnp.einsum('bclhn,bcshn->bclsh', cc, bc) * decay
    y_diag = jnp.einsum('bclsh,bcshp->bclhp', scores, xc)
    to_end = jnp.exp(cum[:, :, -1:, :] - cum)
    chunk_states = jnp.einsum('bcshn,bcsh,bcshp->bchpn', bc, to_end, xc)
    chunk_decay = jnp.exp(cum[:, :, -1, :])

    def step(h, inp):
        st, dec = inp
        return h * dec[:, :, None, None] + st, h

    h_final, h_start = lax.scan(step, h0, (jnp.moveaxis(chunk_states, 1, 0),
                                           jnp.moveaxis(chunk_decay, 1, 0)))
    h_start = jnp.moveaxis(h_start, 0, 1)
    y_off = jnp.einsum('bclhn,bchpn->bclhp', cc, h_start) * jnp.exp(cum)[..., None]
    return (y_diag + y_off).reshape(b, L, nh, hp), h_final


def ssd_mixer(z, xbc, dt_raw, P, l, h0_f, h0_b):
    b, L, _ = z.shape
    xbc = jax.nn.silu(dwconv_centred(xbc, P['ssd_conv_w'][l], P['ssd_conv_b'][l])).astype(jnp.float32)
    xs, bm, cm = jnp.split(xbc, [SSD_WIDTH, SSD_WIDTH + SSD_GROUPS * SSD_D_STATE], axis=-1)
    rep = SSD_HEADS // SSD_GROUPS
    xs = xs.reshape(b, L, SSD_HEADS, SSD_HEAD_DIM)
    bm = jnp.repeat(bm.reshape(b, L, SSD_GROUPS, SSD_D_STATE), rep, axis=2)
    cm = jnp.repeat(cm.reshape(b, L, SSD_GROUPS, SSD_D_STATE), rep, axis=2)
    dt = jax.nn.softplus(dt_raw.astype(jnp.float32).reshape(b, L, 2, SSD_HEADS)
                         + P['ssd_dt_bias'][l].astype(jnp.float32))
    a_neg = -jnp.exp(P['ssd_A_log'][l].astype(jnp.float32))
    y_f, h_f = ssd_scan(xs, dt[:, :, 0], a_neg[0], bm, cm, h0_f)
    y_b, h_b = ssd_scan(jnp.flip(xs, axis=1), jnp.flip(dt[:, :, 1], axis=1), a_neg[1],
                        jnp.flip(bm, axis=1), jnp.flip(cm, axis=1), h0_b)
    y = y_f + jnp.flip(y_b, axis=1) + P['ssd_D'][l].astype(jnp.float32)[:, None] * xs
    y = y.reshape(b, L, SSD_WIDTH) * jax.nn.silu(z.astype(jnp.float32))
    return rmsnorm(y, P['ssd_norm'][l]).astype(z.dtype), h_f, h_b


def axial_rope_tables(L):
    rows = L // GRID_W
    row = jnp.repeat(jnp.arange(rows, dtype=jnp.float32), GRID_W)
    col = jnp.tile(jnp.arange(GRID_W, dtype=jnp.float32), rows)
    inv = ROPE_BASE ** (-jnp.arange(ROPE_PER_AXIS, dtype=jnp.float32) / ROPE_PER_AXIS)
    ang = jnp.concatenate([row[:, None] * inv, col[:, None] * inv], axis=-1)
    return jnp.cos(ang), jnp.sin(ang)


def apply_rope(x, cos, sin):
    half = HEAD_DIM // 2
    xf = x.astype(jnp.float32)
    x1, x2 = xf[..., :half], xf[..., half:]
    c = cos[None, :, None, :]
    s = sin[None, :, None, :]
    return jnp.concatenate([x1 * c - x2 * s, x2 * c + x1 * s], axis=-1).astype(x.dtype)


def context_attention(q, k, v, sink):
    b, C, _, d = q.shape
    nq = C // BLOCK
    scale = HEAD_DIM ** -0.5
    qb = jnp.moveaxis(q.astype(jnp.float32).reshape(b, nq, BLOCK, ATT_KV_HEADS, ATT_GROUP, d), 1, 0)
    kf = k.astype(jnp.float32)
    vf = v.astype(jnp.float32)
    sk = jnp.broadcast_to(sink.astype(jnp.float32).reshape(ATT_KV_HEADS, ATT_GROUP)[None, :, :, None, None],
                          (b, ATT_KV_HEADS, ATT_GROUP, BLOCK, 1))

    def one(qblk):
        s = jnp.einsum('bqkgd,bckd->bkgqc', qblk, kf) * scale
        p = jax.nn.softmax(jnp.concatenate([sk, s], axis=-1), axis=-1)[..., 1:]
        return jnp.einsum('bkgqc,bckd->bqkgd', p, vf)

    o = lax.map(one, qb)
    return jnp.moveaxis(o, 0, 1).reshape(b, C, ATT_WIDTH).astype(q.dtype)


def latent_attention(q, k, v, ck, cv, sink):
    b, L, _, d = q.shape
    nb = L // BLOCK
    n_side = WINDOW // BLOCK
    span = BLOCK * (2 * n_side + 1)
    scale = HEAD_DIM ** -0.5
    qb = q.astype(jnp.float32).reshape(b, nb, BLOCK, ATT_KV_HEADS, ATT_GROUP, d)
    kp = jnp.pad(k.astype(jnp.float32), ((0, 0), (WINDOW, WINDOW), (0, 0), (0, 0)))
    vp = jnp.pad(v.astype(jnp.float32), ((0, 0), (WINDOW, WINDOW), (0, 0), (0, 0)))
    kwin = jnp.concatenate([kp[:, i * BLOCK:i * BLOCK + L].reshape(b, nb, BLOCK, ATT_KV_HEADS, d)
                            for i in range(2 * n_side + 1)], axis=2)
    vwin = jnp.concatenate([vp[:, i * BLOCK:i * BLOCK + L].reshape(b, nb, BLOCK, ATT_KV_HEADS, d)
                            for i in range(2 * n_side + 1)], axis=2)
    ckf = ck.astype(jnp.float32)
    cvf = cv.astype(jnp.float32)
    n_ctx = ckf.shape[1]
    sk = jnp.broadcast_to(sink.astype(jnp.float32).reshape(ATT_KV_HEADS, ATT_GROUP)[None, :, :, None, None],
                          (b, ATT_KV_HEADS, ATT_GROUP, BLOCK, 1))

    def one(args):
        qblk, kblk, vblk, n = args
        pos_q = n * BLOCK + jnp.arange(BLOCK)
        pos_k = n * BLOCK - WINDOW + jnp.arange(span)
        valid = (jnp.abs(pos_q[:, None] - pos_k[None, :]) <= WINDOW) & (pos_k >= 0) & (pos_k < L)
        s_band = jnp.where(valid, jnp.einsum('bqkgd,bskd->bkgqs', qblk, kblk) * scale, -jnp.inf)
        s_ctx = jnp.einsum('bqkgd,bckd->bkgqc', qblk, ckf) * scale
        p = jax.nn.softmax(jnp.concatenate([sk, s_ctx, s_band], axis=-1), axis=-1)
        return (jnp.einsum('bkgqc,bckd->bqkgd', p[..., 1:1 + n_ctx], cvf)
                + jnp.einsum('bkgqs,bskd->bqkgd', p[..., 1 + n_ctx:], vblk))

    o = lax.map(one, (jnp.moveaxis(qb, 1, 0), jnp.moveaxis(kwin, 1, 0),
                      jnp.moveaxis(vwin, 1, 0), jnp.arange(nb)))
    return jnp.moveaxis(o, 0, 1).reshape(b, L, ATT_WIDTH).astype(q.dtype)


def _linear_combine(e1, e2):
    a1, b1 = e1
    a2, b2 = e2
    return a1 * a2, a2 * b1 + b2


def s5_scan(bu, lam_bar, h0):
    bu = bu.at[:, 0].add(lam_bar[None] * h0)
    a = jnp.broadcast_to(lam_bar, bu.shape)
    _, hs = lax.associative_scan(_linear_combine, (a, bu), axis=1)
    return hs, hs[:, -1]


def s5_mixer(u, P, l, h0_f, h0_b):
    b, L, _ = u.shape
    uf = u.astype(jnp.float32)
    uc = uf.reshape(b, L, S5_GROUPS, S5_GROUP_CH).astype(jnp.complex64)
    b_in = lax.complex(P['s5_B_re'][l].astype(jnp.float32), P['s5_B_im'][l].astype(jnp.float32))
    y = P['s5_D'][l].astype(jnp.float32) * uf
    finals = []
    for d, h0 in enumerate((h0_f, h0_b)):
        lam = lax.complex(P['s5_A_re'][l, d].astype(jnp.float32), P['s5_A_im'][l, d].astype(jnp.float32))
        step = jnp.exp(P['s5_log_dt'][l, d].astype(jnp.float32))[:, None]
        lam_bar = jnp.exp(lam * step)
        b_bar = ((lam_bar - 1.0) / lam)[..., None] * b_in
        bu = jnp.einsum('blgc,gpc->blgp', uc, b_bar)
        if d == 1:
            bu = jnp.flip(bu, axis=1)
        hs, h_last = s5_scan(bu, lam_bar, h0)
        if d == 1:
            hs = jnp.flip(hs, axis=1)
        c_out = lax.complex(P['s5_C_re'][l, d].astype(jnp.float32), P['s5_C_im'][l, d].astype(jnp.float32))
        y = y + jnp.real(jnp.einsum('blgp,gcp->blgc', hs, c_out)).reshape(b, L, S5_WIDTH)
        finals.append(h_last)
    g = jax.nn.gelu(y)
    out = g * jax.nn.sigmoid(g @ P['s5_w_glu'][l].astype(jnp.float32) + P['s5_b_glu'][l].astype(jnp.float32))
    return out.astype(u.dtype), finals[0], finals[1]


def mixer_block(h, P, l, ctx):
    b, L, _ = h.shape
    proj = jnp.einsum('bld,de->ble', h, P['w_in'][l])
    z, xbc, dt_raw, q, k, v, u = jnp.split(proj, IN_SPLITS, axis=-1)
    q = q.reshape(b, L, ATT_HEADS, HEAD_DIM)
    k = k.reshape(b, L, ATT_KV_HEADS, HEAD_DIM)
    v = v.reshape(b, L, ATT_KV_HEADS, HEAD_DIM)
    sink = P['attn_sink'][l]
    if ctx is None:
        h0 = jnp.zeros((b, SSD_HEADS, SSD_HEAD_DIM, SSD_D_STATE), jnp.float32)
        s0 = jnp.zeros((b, S5_GROUPS, S5_P), jnp.complex64)
        o_att = context_attention(q, k, v, sink)
        y_ssd, h_f, h_b = ssd_mixer(z, xbc, dt_raw, P, l, h0, h0)
        y_s5, s_f, s_b = s5_mixer(u, P, l, s0, s0)
        new_ctx = (k, v, h_f, h_b, s_f, s_b)
    else:
        ck, cv, h0_f, h0_b, s0_f, s0_b = ctx
        cos, sin = axial_rope_tables(L)
        o_att = latent_attention(apply_rope(q, cos, sin), apply_rope(k, cos, sin), v, ck, cv, sink)
        y_ssd, _, _ = ssd_mixer(z, xbc, dt_raw, P, l, h0_f, h0_b)
        y_s5, _, _ = s5_mixer(u, P, l, s0_f, s0_b)
        new_ctx = None
    mix = jnp.concatenate([y_ssd, o_att, y_s5], axis=-1)
    return jnp.einsum('ble,ed->bld', mix, P['w_out'][l]), new_ctx


def trunk_layer(x, mod, P, l, ctx):
    sh1, sc1, g1, sh2, sc2, g2 = jnp.split(mod.astype(x.dtype), 6, axis=-1)
    h = rmsnorm(x, P['norm_mix_pre'][l]) * (1 + sc1) + sh1
    mix, new_ctx = mixer_block(h, P, l, ctx)
    x = x + g1 * rmsnorm(mix, P['norm_mix_post'][l])
    h = rmsnorm(x, P['norm_ffn_pre'][l]) * (1 + sc2) + sh2
    gate, up = jnp.split(jnp.einsum('bld,df->blf', h, P['w_ffn_in'][l]), 2, axis=-1)
    f = jnp.einsum('blf,fd->bld', jax.nn.silu(gate) * up, P['w_ffn_out'][l])
    x = x + g2 * rmsnorm(f, P['norm_ffn_post'][l])
    return x, new_ctx


def setup_inputs(seed: int = 0) -> dict:
    key = jax.random.key(seed)
    ks = iter(jax.random.split(key, 40))

    def nrm(shape, scale):
        return jax.random.normal(next(ks), shape, jnp.float32) * scale

    def unif(shape, lo, hi):
        return jax.random.uniform(next(ks), shape, jnp.float32, lo, hi)

    x_prompt = nrm((BATCH, SEQ, D_MODEL), 1.0)
    x_sample = nrm((DEC_BATCH, DEC_SEQ, D_MODEL), 1.0)
    c = nrm((DEC_BATCH, D_MODEL), 1.0)
    cache_k = nrm((DEC_BATCH, DEPTH, PAST_LEN, ATT_KV_HEADS, HEAD_DIM), 1.0)
    cache_v = nrm((DEC_BATCH, DEPTH, PAST_LEN, ATT_KV_HEADS, HEAD_DIM), 1.0)
    state_ssd = nrm((DEC_BATCH, DEPTH, 2, SSD_HEADS, SSD_HEAD_DIM, SSD_D_STATE), 0.5)
    state_s5_re = nrm((DEC_BATCH, DEPTH, 2, S5_GROUPS, S5_P), 0.5)
    state_s5_im = nrm((DEC_BATCH, DEPTH, 2, S5_GROUPS, S5_P), 0.5)
    c_ctx = nrm((D_MODEL,), 1.0)
    w_ada = nrm((DEPTH, D_MODEL, 6 * D_MODEL), 0.5 * D_MODEL ** -0.5)
    b_ada = nrm((DEPTH, 6 * D_MODEL), 0.02)
    norm_mix_pre = 1.0 + nrm((DEPTH, D_MODEL), 0.02)
    norm_mix_post = 1.0 + nrm((DEPTH, D_MODEL), 0.02)
    norm_ffn_pre = 1.0 + nrm((DEPTH, D_MODEL), 0.02)
    norm_ffn_post = 1.0 + nrm((DEPTH, D_MODEL), 0.02)
    w_in = nrm((DEPTH, D_MODEL, D_IN), D_MODEL ** -0.5)
    w_out = nrm((DEPTH, MIX_WIDTH, D_MODEL), MIX_WIDTH ** -0.5)
    ssd_conv_w = nrm((DEPTH, SSD_CONV_W, SSD_CONV_CH), SSD_CONV_W ** -0.5)
    ssd_conv_b = nrm((DEPTH, SSD_CONV_CH), 0.02)
    dt0 = jnp.exp(unif((DEPTH, 2, SSD_HEADS), math.log(1e-3), math.log(1e-1)))
    ssd_dt_bias = dt0 + jnp.log(-jnp.expm1(-dt0))
    ssd_A_log = jnp.log(unif((DEPTH, 2, SSD_HEADS), 1.0, 16.0))
    ssd_D = 1.0 + nrm((DEPTH, SSD_HEADS), 0.1)
    ssd_norm = 1.0 + nrm((DEPTH, SSD_WIDTH), 0.02)
    attn_sink = nrm((DEPTH, ATT_HEADS), 1.0)
    n_idx = jnp.arange(S5_P, dtype=jnp.float32)
    s5_A_re = -0.5 + nrm((DEPTH, 2, S5_GROUPS, S5_P), 0.01)
    s5_A_im = math.pi * n_idx + nrm((DEPTH, 2, S5_GROUPS, S5_P), 0.01)
    s5_log_dt = unif((DEPTH, 2, S5_GROUPS), math.log(1e-3), math.log(1e-1))
    s5_B_re = nrm((DEPTH, S5_GROUPS, S5_P, S5_GROUP_CH), (2 * S5_GROUP_CH) ** -0.5)
    s5_B_im = nrm((DEPTH, S5_GROUPS, S5_P, S5_GROUP_CH), (2 * S5_GROUP_CH) ** -0.5)
    s5_C_re = nrm((DEPTH, 2, S5_GROUPS, S5_GROUP_CH, S5_P), (2 * S5_P) ** -0.5)
    s5_C_im = nrm((DEPTH, 2, S5_GROUPS, S5_GROUP_CH, S5_P), (2 * S5_P) ** -0.5)
    s5_D = nrm((DEPTH, S5_WIDTH), 1.0)
    s5_w_glu = nrm((DEPTH, S5_WIDTH, S5_WIDTH), S5_WIDTH ** -0.5)
    s5_b_glu = nrm((DEPTH, S5_WIDTH), 0.02)
    w_ffn_in = nrm((DEPTH, D_MODEL, 2 * D_FF), D_MODEL ** -0.5)
    w_ffn_out = nrm((DEPTH, D_FF, D_MODEL), D_FF ** -0.5)
    return {'x_prompt': x_prompt, 'x_sample': x_sample, 'c': c,
            'cache_k': cache_k, 'cache_v': cache_v, 'state_ssd': state_ssd,
            'state_s5_re': state_s5_re, 'state_s5_im': state_s5_im, 'c_ctx': c_ctx,
            'w_ada': w_ada, 'b_ada': b_ada,
            'norm_mix_pre': norm_mix_pre, 'norm_mix_post': norm_mix_post,
            'norm_ffn_pre': norm_ffn_pre, 'norm_ffn_post': norm_ffn_post,
            'w_in': w_in, 'w_out': w_out,
            'ssd_conv_w': ssd_conv_w, 'ssd_conv_b': ssd_conv_b, 'ssd_dt_bias': ssd_dt_bias,
            'ssd_A_log': ssd_A_log, 'ssd_D': ssd_D, 'ssd_norm': ssd_norm,
            'attn_sink': attn_sink,
            's5_A_re': s5_A_re, 's5_A_im': s5_A_im, 's5_log_dt': s5_log_dt,
            's5_B_re': s5_B_re, 's5_B_im': s5_B_im, 's5_C_re': s5_C_re, 's5_C_im': s5_C_im,
            's5_D': s5_D, 's5_w_glu': s5_w_glu, 's5_b_glu': s5_b_glu,
            'w_ffn_in': w_ffn_in, 'w_ffn_out': w_ffn_out}


def reference(x_prompt, x_sample, c, cache_k, cache_v, state_ssd, state_s5_re, state_s5_im, c_ctx,
              w_ada, b_ada, norm_mix_pre, norm_mix_post, norm_ffn_pre, norm_ffn_post,
              w_in, w_out, ssd_conv_w, ssd_conv_b, ssd_dt_bias, ssd_A_log, ssd_D, ssd_norm,
              attn_sink, s5_A_re, s5_A_im, s5_log_dt, s5_B_re, s5_B_im, s5_C_re, s5_C_im,
              s5_D, s5_w_glu, s5_b_glu, w_ffn_in, w_ffn_out):
    P = {'norm_mix_pre': norm_mix_pre, 'norm_mix_post': norm_mix_post,
         'norm_ffn_pre': norm_ffn_pre, 'norm_ffn_post': norm_ffn_post,
         'w_in': w_in, 'w_out': w_out,
         'ssd_conv_w': ssd_conv_w, 'ssd_conv_b': ssd_conv_b, 'ssd_dt_bias': ssd_dt_bias,
         'ssd_A_log': ssd_A_log, 'ssd_D': ssd_D, 'ssd_norm': ssd_norm,
         'attn_sink': attn_sink,
         's5_A_re': s5_A_re, 's5_A_im': s5_A_im, 's5_log_dt': s5_log_dt,
         's5_B_re': s5_B_re, 's5_B_im': s5_B_im, 's5_C_re': s5_C_re, 's5_C_im': s5_C_im,
         's5_D': s5_D, 's5_w_glu': s5_w_glu, 's5_b_glu': s5_b_glu,
         'w_ffn_in': w_ffn_in, 'w_ffn_out': w_ffn_out}
    out_dtype = x_prompt.dtype

    xp = x_prompt
    ks, vs, ssd_st, s5_re, s5_im = [], [], [], [], []
    for l in range(DEPTH):
        mod = (jax.nn.silu(c_ctx) @ w_ada[l] + b_ada[l])[None, None, :]
        xp, (k, v, h_f, h_b, s_f, s_b) = trunk_layer(xp, mod, P, l, None)
        ks.append(k)
        vs.append(v)
        ssd_st.append(jnp.stack([h_f, h_b], axis=1))
        s5_re.append(jnp.stack([jnp.real(s_f), jnp.real(s_b)], axis=1))
        s5_im.append(jnp.stack([jnp.imag(s_f), jnp.imag(s_b)], axis=1))
    new_cache_k = jnp.stack(ks, axis=1)
    new_cache_v = jnp.stack(vs, axis=1)
    new_state_ssd = jnp.stack(ssd_st, axis=1).astype(out_dtype)
    new_state_s5_re = jnp.stack(s5_re, axis=1).astype(out_dtype)
    new_state_s5_im = jnp.stack(s5_im, axis=1).astype(out_dtype)

    xs = x_sample
    for l in range(DEPTH):
        mod = (jax.nn.silu(c) @ w_ada[l] + b_ada[l])[:, None, :]
        ctx = (cache_k[:, l], cache_v[:, l],
               state_ssd[:, l, 0].astype(jnp.float32), state_ssd[:, l, 1].astype(jnp.float32),
               lax.complex(state_s5_re[:, l, 0].astype(jnp.float32), state_s5_im[:, l, 0].astype(jnp.float32)),
               lax.complex(state_s5_re[:, l, 1].astype(jnp.float32), state_s5_im[:, l, 1].astype(jnp.float32)))
        xs, _ = trunk_layer(xs, mod, P, l, ctx)

    return (xp, xs, new_cache_k, new_cache_v, new_state_ssd, new_state_s5_re, new_state_s5_im)
```

```python
import functools
import math

import jax
import jax.numpy as jnp
from jax import lax
from jax.experimental import pallas as pl
from jax.experimental.pallas import tpu as pltpu

F32 = jnp.float32
BF16 = jnp.bfloat16

D_MODEL = 2048
N_LAYERS = 2
D_FF = 5632
GRID_W = 64
EPS = 1e-6

SSD_HEADS = 12
SSD_HEAD_DIM = 64
SSD_WIDTH = SSD_HEADS * SSD_HEAD_DIM
SSD_GROUPS = 2
SSD_STATE = 64
SSD_BC = 2 * SSD_GROUPS * SSD_STATE
SSD_CHUNK = 128
SSD_HALO = 8
ATT_HEADS = 12
ATT_KV = 4
ATT_GROUP = ATT_HEADS // ATT_KV
HEAD_DIM = 64
ATT_WIDTH = ATT_HEADS * HEAD_DIM
KV_WIDTH = ATT_KV * HEAD_DIM
ATT_BLOCK = 128
ROPE_PER_AXIS = HEAD_DIM // 4
ROPE_BASE = 10000.0
S5_GROUPS = 32
S5_CH = 16
S5_WIDTH = S5_GROUPS * S5_CH
S5_P = 64
S5_T = 32
S5_TW = S5_T * S5_CH
S5_ST = 4 * S5_P
MIX_WIDTH = SSD_WIDTH + ATT_WIDTH + S5_WIDTH

O_Z = 0
O_XS = SSD_WIDTH
O_BC = O_XS + SSD_WIDTH
O_DT = O_BC + SSD_BC
O_Q = O_DT + 2 * SSD_HEADS
O_K = O_Q + ATT_WIDTH
O_V = O_K + KV_WIDTH
O_U = O_V + KV_WIDTH
O_END = O_U + S5_WIDTH
C_Q = 0
C_Z = 768
C_XS = 1536
C_BC = 2304
C_K = 2560
C_V = 2816
C_U = 3072
C_DT = 3584
DT_PAD = 128
D_INP = C_DT + DT_PAD

NEG = -1e30
VMEM_LIMIT = 56 * 1024 * 1024


def _cparams(*sem):
    return pltpu.CompilerParams(dimension_semantics=sem, vmem_limit_bytes=VMEM_LIMIT)


def _resident(shape):
    nd = len(shape)
    return pl.BlockSpec(shape, lambda *_: (0,) * nd, pipeline_mode=pl.Buffered(1))


def _rms(x, g):
    return x * lax.rsqrt(jnp.mean(x * x, axis=-1, keepdims=True) + EPS) * g


def _silu(x):
    return x * jax.nn.sigmoid(x)


def _dot(a, b):
    return jnp.dot(a, b, preferred_element_type=F32)


def _dot_nt(a, b):
    return lax.dot_general(a, b, (((1,), (1,)), ((), ())), preferred_element_type=F32)


def _dot_tn(a, b):
    return lax.dot_general(a, b, (((0,), (0,)), ((), ())), preferred_element_type=F32)


def _split_bf16(x, parts):
    out = []
    for _ in range(parts):
        p = x.astype(BF16)
        out.append(p)
        x = x - p.astype(F32)
    return out


def _dot_sel_rhs(x, sel, parts):
    acc = None
    for p in _split_bf16(x, parts):
        t = _dot(p, sel)
        acc = t if acc is None else acc + t
    return acc


def _dot_sel_lhs(sel, x, parts):
    acc = None
    for p in _split_bf16(x, parts):
        t = _dot(sel, p)
        acc = t if acc is None else acc + t
    return acc


def _mod_row(i, tm, n_ctx, lat_len):
    start = i * tm
    return jnp.where(start < n_ctx, 0, 1 + (start - n_ctx) // lat_len)


def _ada_kernel(c_ref, w_ref, b_ref, o_ref):
    c = c_ref[...]
    o_ref[0] = _dot(_silu(c).astype(BF16), w_ref[0].astype(BF16)) + b_ref[0]


def _ada(cvecs, w_ada, b_ada):
    tn = 1024
    rows = cvecs.shape[0]
    return pl.pallas_call(
        _ada_kernel,
        grid=(N_LAYERS, 6 * D_MODEL // tn),
        in_specs=[pl.BlockSpec((rows, D_MODEL), lambda l, j: (0, 0)),
                  pl.BlockSpec((1, D_MODEL, tn), lambda l, j: (l, 0, j)),
                  pl.BlockSpec((1, 1, tn), lambda l, j: (l, 0, j))],
        out_specs=pl.BlockSpec((1, rows, tn), lambda l, j: (l, 0, j)),
        out_shape=jax.ShapeDtypeStruct((N_LAYERS, rows, 6 * D_MODEL), F32),
        compiler_params=_cparams("parallel", "parallel"),
        name="ada_mod",
    )(cvecs, w_ada, b_ada.reshape(N_LAYERS, 1, 6 * D_MODEL))


def _in_proj_kernel(x_ref, mod_ref, g_ref, w_ref, o_ref):
    mod = mod_ref[0]
    sh = mod[:, 0:D_MODEL]
    sc = mod[:, D_MODEL:2 * D_MODEL]
    h = _rms(x_ref[...], g_ref[...]) * (1.0 + sc) + sh
    o_ref[...] = _dot(h.astype(BF16), w_ref[...])


def _in_proj(x, mod3, gamma, w, *, tm, n_ctx, lat_len):
    n = x.shape[0]
    return pl.pallas_call(
        _in_proj_kernel,
        grid=(n // tm,),
        in_specs=[pl.BlockSpec((tm, D_MODEL), lambda i: (i, 0)),
                  pl.BlockSpec((1, 1, 6 * D_MODEL), lambda i: (_mod_row(i, tm, n_ctx, lat_len), 0, 0)),
                  _resident((1, D_MODEL)),
                  _resident((D_MODEL, D_INP))],
        out_specs=pl.BlockSpec((tm, D_INP), lambda i: (i, 0)),
        out_shape=jax.ShapeDtypeStruct((n, D_INP), F32),
        compiler_params=_cparams("parallel"),
        name="in_proj",
    )(x, mod3, gamma, w)


def _gelu_tanh(x):
    return 0.5 * x * (1.0 + jnp.tanh(math.sqrt(2.0 / math.pi) * (x + 0.044715 * (x * x * x))))


def _out_proj_kernel(x_ref, mod_ref, g_ref, yssd_ref, oatt_ref, ys5_ref, u_ref, d_ref, wglu_ref, bglu_ref,
                     w1_ref, w2_ref, w3_ref, o_ref):
    y5 = ys5_ref[...] + d_ref[...] * u_ref[...]
    g = _gelu_tanh(y5)
    s5 = g * jax.nn.sigmoid(_dot(g.astype(BF16), wglu_ref[...]) + bglu_ref[...])
    mix = (_dot(yssd_ref[...], w1_ref[...]) + _dot(oatt_ref[...], w2_ref[...])
           + _dot(s5.astype(BF16), w3_ref[...]))
    gate = mod_ref[0][:, 2 * D_MODEL:3 * D_MODEL]
    o_ref[...] = x_ref[...] + gate * _rms(mix, g_ref[...])


def _out_proj(x, mod3, gamma, y_ssd, o_att, y_s5, proj, s5_d, w_glu, b_glu, w1, w2, w3, *, tm, n_ctx, lat_len):
    n = x.shape[0]
    row = lambda i: (i, 0)
    return pl.pallas_call(
        _out_proj_kernel,
        grid=(n // tm,),
        in_specs=[pl.BlockSpec((tm, D_MODEL), row),
                  pl.BlockSpec((1, 1, 6 * D_MODEL), lambda i: (_mod_row(i, tm, n_ctx, lat_len), 0, 0)),
                  _resident((1, D_MODEL)),
                  pl.BlockSpec((tm, SSD_WIDTH), row),
                  pl.BlockSpec((tm, ATT_WIDTH), row),
                  pl.BlockSpec((tm, S5_WIDTH), row),
                  pl.BlockSpec((tm, S5_WIDTH), lambda i: (i, C_U // S5_WIDTH)),
                  _resident((1, S5_WIDTH)),
                  _resident((S5_WIDTH, S5_WIDTH)),
                  _resident((1, S5_WIDTH)),
                  _resident((SSD_WIDTH, D_MODEL)),
                  _resident((ATT_WIDTH, D_MODEL)),
                  _resident((S5_WIDTH, D_MODEL))],
        out_specs=pl.BlockSpec((tm, D_MODEL), row),
        out_shape=jax.ShapeDtypeStruct((n, D_MODEL), F32),
        compiler_params=_cparams("parallel"),
        name="out_proj",
    )(x, mod3, gamma, y_ssd, o_att, y_s5, proj, s5_d, w_glu, b_glu, w1, w2, w3)


def _ffn_kernel(x_ref, mod_ref, gpre_ref, gpost_ref, wg_ref, wu_ref, wo_ref, o_ref, h_scr, acc_scr):
    j = pl.program_id(1)

    @pl.when(j == 0)
    def _():
        mod = mod_ref[0]
        sh = mod[:, 3 * D_MODEL:4 * D_MODEL]
        sc = mod[:, 4 * D_MODEL:5 * D_MODEL]
        h = _rms(x_ref[...], gpre_ref[...]) * (1.0 + sc) + sh
        h_scr[...] = h.astype(BF16)
        acc_scr[...] = jnp.zeros_like(acc_scr)

    h = h_scr[...]
    act = _silu(_dot(h, wg_ref[...])) * _dot(h, wu_ref[...])
    acc_scr[...] += _dot(act.astype(BF16), wo_ref[...])

    @pl.when(j == pl.num_programs(1) - 1)
    def _():
        gate = mod_ref[0][:, 5 * D_MODEL:6 * D_MODEL]
        o_ref[...] = x_ref[...] + gate * _rms(acc_scr[...], gpost_ref[...])


def _ffn(x, mod3, g_pre, g_post, w_in, w_out, *, tm, tf, n_ctx, lat_len):
    n = x.shape[0]
    nf = D_FF // tf
    return pl.pallas_call(
        _ffn_kernel,
        grid=(n // tm, nf),
        in_specs=[pl.BlockSpec((tm, D_MODEL), lambda i, j: (i, 0)),
                  pl.BlockSpec((1, 1, 6 * D_MODEL), lambda i, j: (_mod_row(i, tm, n_ctx, lat_len), 0, 0)),
                  _resident((1, D_MODEL)),
                  _resident((1, D_MODEL)),
                  pl.BlockSpec((D_MODEL, tf), lambda i, j: (0, j)),
                  pl.BlockSpec((D_MODEL, tf), lambda i, j: (0, j + nf)),
                  pl.BlockSpec((tf, D_MODEL), lambda i, j: (j, 0))],
        out_specs=pl.BlockSpec((tm, D_MODEL), lambda i, j: (i, 0)),
        out_shape=jax.ShapeDtypeStruct((n, D_MODEL), F32),
        scratch_shapes=[pltpu.VMEM((tm, D_MODEL), BF16), pltpu.VMEM((tm, D_MODEL), F32)],
        compiler_params=_cparams("parallel", "arbitrary"),
        name="ffn",
    )(x, mod3, g_pre, g_post, w_in, w_in, w_out)


def _attend(q, parts, sink_ref):
    outs = []
    for h in range(ATT_HEADS):
        kv = h // ATT_GROUP
        lo, hi = kv * HEAD_DIM, (kv + 1) * HEAD_DIM
        qh = q[:, h * HEAD_DIM:(h + 1) * HEAD_DIM]
        sink = sink_ref[h]
        scores = []
        m = None
        for k, _, mask in parts:
            s = _dot_nt(qh, k[:, lo:hi])
            if mask is not None:
                s = jnp.where(mask, s, NEG)
            scores.append(s)
            sm = jnp.max(s, axis=-1, keepdims=True)
            m = sm if m is None else jnp.maximum(m, sm)
        m = jnp.maximum(m, sink)
        den = jnp.exp(sink - m)
        acc = None
        for s, (_, v, _) in zip(scores, parts):
            p = jnp.exp(s - m)
            den = den + jnp.sum(p, axis=-1, keepdims=True)
            t = _dot(p.astype(BF16), v[:, lo:hi])
            acc = t if acc is None else acc + t
        outs.append(acc / den)
    return jnp.concatenate(outs, axis=-1)


def _ctx_attn_kernel(sink_ref, q_ref, k_ref, v_ref, o_ref):
    q = (q_ref[...] * (HEAD_DIM ** -0.5)).astype(BF16)
    parts = [(k_ref[...].astype(BF16), v_ref[...].astype(BF16), None)]
    o_ref[...] = _attend(q, parts, sink_ref).astype(o_ref.dtype)


def _ctx_attention(proj, sink, *, n_seq, seq_len):
    return pl.pallas_call(
        _ctx_attn_kernel,
        grid=(n_seq,),
        in_specs=[pl.BlockSpec(memory_space=pltpu.SMEM),
                  pl.BlockSpec((seq_len, ATT_WIDTH), lambda b: (b, C_Q // ATT_WIDTH)),
                  pl.BlockSpec((seq_len, KV_WIDTH), lambda b: (b, C_K // KV_WIDTH)),
                  pl.BlockSpec((seq_len, KV_WIDTH), lambda b: (b, C_V // KV_WIDTH))],
        out_specs=pl.BlockSpec((seq_len, ATT_WIDTH), lambda b: (b, 0)),
        out_shape=jax.ShapeDtypeStruct((n_seq * seq_len, ATT_WIDTH), BF16),
        compiler_params=_cparams("parallel"),
        name="ctx_attention",
    )(sink, proj, proj, proj)


def _rope(x, cos, sin_signed, first_half):
    outs = []
    for j in range(x.shape[1] // 128):
        xj = x[:, j * 128:(j + 1) * 128]
        partner = jnp.where(first_half, pltpu.roll(xj, 128 - HEAD_DIM // 2, axis=1),
                            pltpu.roll(xj, HEAD_DIM // 2, axis=1))
        outs.append(xj * cos + partner * sin_signed)
    return jnp.concatenate(outs, axis=-1)


def _lat_attn_kernel(sink_ref, q_ref, kp_ref, kc_ref, kn_ref, vp_ref, vc_ref, vn_ref, ck_ref, cv_ref,
                     cosq_ref, sinq_ref, cosp_ref, sinp_ref, cosn_ref, sinn_ref, o_ref):
    n = pl.program_id(1)
    nb = pl.num_programs(1)
    blk = ATT_BLOCK
    lane = lax.broadcasted_iota(jnp.int32, (blk, 128), 1)
    first_half = (lane % HEAD_DIM) < (HEAD_DIM // 2)
    q = _rope(q_ref[...], cosq_ref[...], sinq_ref[...], first_half)
    q = (q * (HEAD_DIM ** -0.5)).astype(BF16)
    kp = _rope(kp_ref[...], cosp_ref[...], sinp_ref[...], first_half).astype(BF16)
    kc = _rope(kc_ref[...], cosq_ref[...], sinq_ref[...], first_half).astype(BF16)
    kn = _rope(kn_ref[...], cosn_ref[...], sinn_ref[...], first_half).astype(BF16)
    row = lax.broadcasted_iota(jnp.int32, (blk, blk), 0)
    col = lax.broadcasted_iota(jnp.int32, (blk, blk), 1)
    mask_prev = jnp.logical_and(col >= row, n > 0)
    mask_next = jnp.logical_and(col <= row, n < nb - 1)
    parts = [(ck_ref[0], cv_ref[0], None),
             (kp, vp_ref[...].astype(BF16), mask_prev),
             (kc, vc_ref[...].astype(BF16), None),
             (kn, vn_ref[...].astype(BF16), mask_next)]
    o_ref[...] = _attend(q, parts, sink_ref).astype(o_ref.dtype)


def _lat_attention(proj, sink, ck, cv, cos, sin_signed, *, row0, n_seq, seq_len):
    blk = ATT_BLOCK
    nb = seq_len // blk
    rb0 = row0 // blk

    def cur(b, n):
        return rb0 + b * nb + n

    def prev(b, n):
        return rb0 + b * nb + jnp.maximum(n - 1, 0)

    def nxt(b, n):
        return rb0 + b * nb + jnp.minimum(n + 1, nb - 1)

    kcol, vcol = C_K // KV_WIDTH, C_V // KV_WIDTH
    tab = lambda f: pl.BlockSpec((blk, 128), f)
    return pl.pallas_call(
        _lat_attn_kernel,
        grid=(n_seq, nb),
        in_specs=[pl.BlockSpec(memory_space=pltpu.SMEM),
                  pl.BlockSpec((blk, ATT_WIDTH), lambda b, n: (cur(b, n), C_Q // ATT_WIDTH)),
                  pl.BlockSpec((blk, KV_WIDTH), lambda b, n: (prev(b, n), kcol)),
                  pl.BlockSpec((blk, KV_WIDTH), lambda b, n: (cur(b, n), kcol)),
                  pl.BlockSpec((blk, KV_WIDTH), lambda b, n: (nxt(b, n), kcol)),
                  pl.BlockSpec((blk, KV_WIDTH), lambda b, n: (prev(b, n), vcol)),
                  pl.BlockSpec((blk, KV_WIDTH), lambda b, n: (cur(b, n), vcol)),
                  pl.BlockSpec((blk, KV_WIDTH), lambda b, n: (nxt(b, n), vcol)),
                  pl.BlockSpec((1,) + ck.shape[1:], lambda b, n: (b, 0, 0)),
                  pl.BlockSpec((1,) + cv.shape[1:], lambda b, n: (b, 0, 0)),
                  tab(lambda b, n: (n, 0)), tab(lambda b, n: (n, 0)),
                  tab(lambda b, n: (jnp.maximum(n - 1, 0), 0)), tab(lambda b, n: (jnp.maximum(n - 1, 0), 0)),
                  tab(lambda b, n: (jnp.minimum(n + 1, nb - 1), 0)),
                  tab(lambda b, n: (jnp.minimum(n + 1, nb - 1), 0))],
        out_specs=pl.BlockSpec((blk, ATT_WIDTH), lambda b, n: (b * nb + n, 0)),
        out_shape=jax.ShapeDtypeStruct((n_seq * seq_len, ATT_WIDTH), BF16),
        compiler_params=_cparams("parallel", "parallel"),
        name="lat_attention",
    )(sink, proj, proj, proj, proj, proj, proj, proj, ck, cv, cos, sin_signed, cos, sin_signed, cos, sin_signed)


def _rope_tables(seq_len):
    rows = seq_len // GRID_W
    row = jnp.repeat(jnp.arange(rows, dtype=F32), GRID_W)
    col = jnp.tile(jnp.arange(GRID_W, dtype=F32), rows)
    inv = ROPE_BASE ** (-jnp.arange(ROPE_PER_AXIS, dtype=F32) / ROPE_PER_AXIS)
    ang = jnp.concatenate([row[:, None] * inv, col[:, None] * inv], axis=-1)
    cos, sin = jnp.cos(ang), jnp.sin(ang)
    cos128 = jnp.tile(cos, (1, 4))
    sin128 = jnp.tile(jnp.concatenate([-sin, sin], axis=-1), (1, 2))
    return cos128, sin128


def _conv_silu(scr, prev_ref, cur_ref, next_ref, w_ref, b_ref, is_first, is_last):
    q = SSD_CHUNK
    h = SSD_HALO
    scr[0:h, :] = jnp.where(is_first, 0.0, prev_ref[...])
    scr[h:h + q, :] = cur_ref[...]
    scr[h + q:h + q + h, :] = jnp.where(is_last, 0.0, next_ref[...])
    y = (scr[h - 1:h - 1 + q, :] * w_ref[0:1, :] + scr[h:h + q, :] * w_ref[1:2, :]
         + scr[h + 1:h + 1 + q, :] * w_ref[2:3, :] + b_ref[...])
    return _silu(y)


def _softplus(x):
    return jnp.maximum(x, 0.0) + jnp.log1p(jnp.exp(-jnp.abs(x)))


def _ssd_bwd_kernel(xsp_ref, xsc_ref, xsn_ref, bcp_ref, bcc_ref, bcn_ref, dt_ref, h0_ref,
                    cwx_ref, cbx_ref, cwb_ref, cbb_ref, dtb_ref, alog_ref, tri_ref, exp_ref,
                    hstart_ref, hfin_ref, xs_scr, bc_scr, h_scr):
    i = pl.program_id(1)
    nc = pl.num_programs(1)

    @pl.when(i == 0)
    def _():
        h_scr[...] = h0_ref[0]

    hb = h_scr[...]
    hstart_ref[0, 0] = hb
    is_first = i == nc - 1
    is_last = i == 0
    xs = _conv_silu(xs_scr, xsp_ref, xsc_ref, xsn_ref, cwx_ref, cbx_ref, is_first, is_last)
    bc = _conv_silu(bc_scr, bcp_ref, bcc_ref, bcn_ref, cwb_ref, cbb_ref, is_first, is_last)
    bm = bc[:, 0:SSD_GROUPS * SSD_STATE].astype(BF16)
    dtv = _softplus(dt_ref[...] + dtb_ref[...])
    dta = dtv * (-jnp.exp(alog_ref[...]))
    rev = _dot_sel_lhs(tri_ref[1], dta, 3)
    sel_b = exp_ref[1]
    dt_w = _dot_sel_rhs(dtv, sel_b, 2)
    to_end = _dot_sel_rhs(jnp.exp(rev[0:1, :] - rev), sel_b, 2)
    decay = _dot_sel_rhs(jnp.exp(rev[0:SSD_HALO, :]), sel_b, 2)[0:1, :]
    xw = (xs * dt_w * to_end).astype(BF16)
    half = SSD_WIDTH // SSD_GROUPS
    upd = jnp.concatenate(
        [_dot_tn(bm[:, g * SSD_STATE:(g + 1) * SSD_STATE], xw[:, g * half:(g + 1) * half])
         for g in range(SSD_GROUPS)], axis=-1)
    h_new = hb * decay + upd
    h_scr[...] = h_new

    @pl.when(i == nc - 1)
    def _():
        hfin_ref[0] = h_new


def _ssd_fwd_kernel(z_ref, xsp_ref, xsc_ref, xsn_ref, bcp_ref, bcc_ref, bcn_ref, dt_ref, hb_ref, h0_ref,
                    cwx_ref, cbx_ref, cwb_ref, cbb_ref, dtb_ref, alog_ref, tri_ref, exp_ref, dvec_ref, nw_ref,
                    y_ref, hfin_ref, xs_scr, bc_scr, h_scr):
    c = pl.program_id(1)
    nc = pl.num_programs(1)
    q = SSD_CHUNK
    nh = SSD_HEADS

    @pl.when(c == 0)
    def _():
        h_scr[...] = h0_ref[0]

    hf = h_scr[...]
    xs = _conv_silu(xs_scr, xsp_ref, xsc_ref, xsn_ref, cwx_ref, cbx_ref, c == 0, c == nc - 1)
    bc = _conv_silu(bc_scr, bcp_ref, bcc_ref, bcn_ref, cwb_ref, cbb_ref, c == 0, c == nc - 1)
    gs = SSD_GROUPS * SSD_STATE
    bm = bc[:, 0:gs].astype(BF16)
    cm = bc[:, gs:2 * gs].astype(BF16)
    xs_bf = xs.astype(BF16)

    dtv = _softplus(dt_ref[...] + dtb_ref[...])
    a_neg = -jnp.exp(alog_ref[...])
    dta = dtv * a_neg
    lower, upper = tri_ref[0], tri_ref[1]
    cum = _dot_sel_lhs(lower, dta, 3)
    rev = _dot_sel_lhs(upper, dta, 3)
    dt_t = dtv.T
    dta_t = dta.T
    cum_t = _dot_sel_rhs(dta_t, upper, 3)
    rev_t = _dot_sel_rhs(dta_t, lower, 3)

    sel_f, sel_b = exp_ref[0], exp_ref[1]
    dtf_w = _dot_sel_rhs(dtv, sel_f, 2)
    dtb_w = _dot_sel_rhs(dtv, sel_b, 2)
    ecum_w = _dot_sel_rhs(jnp.exp(cum), sel_f, 2)
    erev_w = _dot_sel_rhs(jnp.exp(rev), sel_b, 2)
    toend_w = _dot_sel_rhs(jnp.exp(cum[q - 1:q, :] - cum), sel_f, 2)

    row = lax.broadcasted_iota(jnp.int32, (q, q), 0)
    col = lax.broadcasted_iota(jnp.int32, (q, q), 1)
    causal = row >= col
    anti = row <= col
    gmat = [_dot_nt(cm[:, g * SSD_STATE:(g + 1) * SSD_STATE], bm[:, g * SSD_STATE:(g + 1) * SSD_STATE])
            for g in range(SSD_GROUPS)]
    per_group = nh // SSD_GROUPS
    ys = []
    for h in range(nh):
        seg_f = cum[:, h:h + 1] - cum_t[h:h + 1, :]
        dec_f = jnp.exp(jnp.where(causal, seg_f, NEG)) * dt_t[h:h + 1, :]
        seg_b = rev[:, nh + h:nh + h + 1] - rev_t[nh + h:nh + h + 1, :]
        dec_b = jnp.exp(jnp.where(anti, seg_b, NEG)) * dt_t[nh + h:nh + h + 1, :]
        m = (gmat[h // per_group] * (dec_f + dec_b)).astype(BF16)
        ys.append(_dot(m, xs_bf[:, h * SSD_HEAD_DIM:(h + 1) * SSD_HEAD_DIM]))
    y = jnp.concatenate(ys, axis=-1)

    half = SSD_WIDTH // SSD_GROUPS
    hf_bf = hf.astype(BF16)
    hb_bf = hb_ref[0, 0].astype(BF16)
    off_f = jnp.concatenate([_dot(cm[:, g * SSD_STATE:(g + 1) * SSD_STATE], hf_bf[:, g * half:(g + 1) * half])
                             for g in range(SSD_GROUPS)], axis=-1)
    off_b = jnp.concatenate([_dot(cm[:, g * SSD_STATE:(g + 1) * SSD_STATE], hb_bf[:, g * half:(g + 1) * half])
                             for g in range(SSD_GROUPS)], axis=-1)
    y = y + off_f * ecum_w + off_b * erev_w + dvec_ref[...] * xs
    y = y * _silu(z_ref[...])
    y_ref[...] = _rms(y, nw_ref[...]).astype(y_ref.dtype)

    xw = (xs * dtf_w * toend_w).astype(BF16)
    upd = jnp.concatenate(
        [_dot_tn(bm[:, g * SSD_STATE:(g + 1) * SSD_STATE], xw[:, g * half:(g + 1) * half])
         for g in range(SSD_GROUPS)], axis=-1)
    h_new = hf * ecum_w[q - 1:q, :] + upd
    h_scr[...] = h_new

    @pl.when(c == nc - 1)
    def _():
        hfin_ref[0] = h_new


def _ssd_mixer(proj, h0_f, h0_b, consts, *, row0, n_seq, seq_len):
    q = SSD_CHUNK
    nc = seq_len // q
    rb0 = row0 // q
    per8 = q // SSD_HALO
    n_rows8 = proj.shape[0] // SSD_HALO
    (cwx, cbx, cwb, cbb, dt_bias, a_log, tri, expand, dvec, nw) = consts

    def specs(chunk_of):
        def cur(b, i):
            return rb0 + b * nc + chunk_of(i)

        def prev8(b, i):
            return jnp.maximum(cur(b, i) * per8 - 1, 0)

        def next8(b, i):
            return jnp.minimum((cur(b, i) + 1) * per8, n_rows8 - 1)

        xcol, bcol, dcol = C_XS // SSD_WIDTH, C_BC // SSD_BC, C_DT // DT_PAD
        return [pl.BlockSpec((SSD_HALO, SSD_WIDTH), lambda b, i: (prev8(b, i), xcol)),
                pl.BlockSpec((q, SSD_WIDTH), lambda b, i: (cur(b, i), xcol)),
                pl.BlockSpec((SSD_HALO, SSD_WIDTH), lambda b, i: (next8(b, i), xcol)),
                pl.BlockSpec((SSD_HALO, SSD_BC), lambda b, i: (prev8(b, i), bcol)),
                pl.BlockSpec((q, SSD_BC), lambda b, i: (cur(b, i), bcol)),
                pl.BlockSpec((SSD_HALO, SSD_BC), lambda b, i: (next8(b, i), bcol)),
                pl.BlockSpec((q, DT_PAD), lambda b, i: (cur(b, i), dcol))], cur

    state_spec = pl.BlockSpec((1, SSD_STATE, SSD_WIDTH), lambda b, i: (b, 0, 0))
    const_specs = [_resident(cwx.shape), _resident(cbx.shape), _resident(cwb.shape), _resident(cbb.shape),
                   _resident(dt_bias.shape), _resident(a_log.shape), _resident(tri.shape),
                   _resident(expand.shape)]
    scratch = [pltpu.VMEM((q + 2 * SSD_HALO, SSD_WIDTH), F32), pltpu.VMEM((q + 2 * SSD_HALO, SSD_BC), F32),
               pltpu.VMEM((SSD_STATE, SSD_WIDTH), F32)]

    data_specs, _ = specs(lambda i: nc - 1 - i)
    hb_start, hb_fin = pl.pallas_call(
        _ssd_bwd_kernel,
        grid=(n_seq, nc),
        in_specs=data_specs + [state_spec] + const_specs,
        out_specs=[pl.BlockSpec((1, 1, SSD_STATE, SSD_WIDTH), lambda b, i: (b, nc - 1 - i, 0, 0)), state_spec],
        out_shape=[jax.ShapeDtypeStruct((n_seq, nc, SSD_STATE, SSD_WIDTH), F32),
                   jax.ShapeDtypeStruct((n_seq, SSD_STATE, SSD_WIDTH), F32)],
        scratch_shapes=scratch,
        compiler_params=_cparams("parallel", "arbitrary"),
        name="ssd_backward_states",
    )(proj, proj, proj, proj, proj, proj, proj, h0_b, cwx, cbx, cwb, cbb, dt_bias, a_log, tri, expand)

    data_specs, cur = specs(lambda i: i)
    y, hf_fin = pl.pallas_call(
        _ssd_fwd_kernel,
        grid=(n_seq, nc),
        in_specs=([pl.BlockSpec((q, SSD_WIDTH), lambda b, i: (cur(b, i), C_Z // SSD_WIDTH))] + data_specs
                  + [pl.BlockSpec((1, 1, SSD_STATE, SSD_WIDTH), lambda b, i: (b, i, 0, 0)), state_spec]
                  + const_specs + [_resident(dvec.shape), _resident(nw.shape)]),
        out_specs=[pl.BlockSpec((q, SSD_WIDTH), lambda b, i: (b * nc + i, 0)), state_spec],
        out_shape=[jax.ShapeDtypeStruct((n_seq * seq_len, SSD_WIDTH), BF16),
                   jax.ShapeDtypeStruct((n_seq, SSD_STATE, SSD_WIDTH), F32)],
        scratch_shapes=scratch,
        compiler_params=_cparams("parallel", "arbitrary"),
        name="ssd_forward",
    )(proj, proj, proj, proj, proj, proj, proj, proj, hb_start, h0_f,
      cwx, cbx, cwb, cbb, dt_bias, a_log, tri, expand, dvec, nw)
    return y, hf_fin, hb_fin


def _ssd_consts(l, ssd_conv_w, ssd_conv_b, ssd_dt_bias, ssd_A_log, ssd_D, ssd_norm):
    q = SSD_CHUNK
    cw, cb = ssd_conv_w[l], ssd_conv_b[l]
    pad = DT_PAD - 2 * SSD_HEADS
    dt_bias = jnp.pad(ssd_dt_bias[l].reshape(1, -1), ((0, 0), (0, pad)))
    a_log = jnp.pad(ssd_A_log[l].reshape(1, -1), ((0, 0), (0, pad)))
    r = jnp.arange(q)
    tri = jnp.stack([r[None, :] <= r[:, None], r[None, :] >= r[:, None]]).astype(BF16)
    head_of_col = jnp.arange(SSD_WIDTH) // SSD_HEAD_DIM
    j = jnp.arange(DT_PAD)
    expand = jnp.stack([j[:, None] == head_of_col[None, :],
                        j[:, None] == head_of_col[None, :] + SSD_HEADS]).astype(BF16)
    dvec = jnp.repeat(ssd_D[l], SSD_HEAD_DIM).reshape(1, SSD_WIDTH)
    return (cw[:, :SSD_WIDTH], cb[:SSD_WIDTH].reshape(1, -1), cw[:, SSD_WIDTH:], cb[SSD_WIDTH:].reshape(1, -1),
            dt_bias, a_log, tri, expand, dvec, ssd_norm[l].reshape(1, -1))


def _ssd_state_in(s):
    b = s.shape[0]
    return jnp.transpose(s, (0, 3, 1, 2)).reshape(b, SSD_STATE, SSD_WIDTH)


def _ssd_state_out(s):
    b = s.shape[0]
    return jnp.transpose(s.reshape(b, SSD_STATE, SSD_HEADS, SSD_HEAD_DIM), (0, 2, 3, 1))


def _s5_kernel_table_kernel(a_ref, g_ref, o_ref):
    o_ref[0] = jnp.dot(a_ref[0], g_ref[0], preferred_element_type=F32, precision=lax.Precision.HIGHEST)


def _s5_tables(l, s5_A_re, s5_A_im, s5_log_dt, s5_B_re, s5_B_im, s5_C_re, s5_C_im):
    t = S5_T
    a_re, a_im = s5_A_re[l], s5_A_im[l]
    step = jnp.exp(s5_log_dt[l])[:, :, None]
    k = jnp.arange(t + 1, dtype=F32)[None, None, :, None]
    mag = jnp.exp(k * (a_re * step)[:, :, None, :])
    ang = k * (a_im * step)[:, :, None, :]
    pw_re, pw_im = mag * jnp.cos(ang), mag * jnp.sin(ang)
    lb_re, lb_im = pw_re[:, :, 1], pw_im[:, :, 1]
    den = a_re * a_re + a_im * a_im
    r_re = ((lb_re - 1.0) * a_re + lb_im * a_im) / den
    r_im = (lb_im * a_re - (lb_re - 1.0) * a_im) / den
    b_re, b_im = s5_B_re[l][None], s5_B_im[l][None]
    bb_re = r_re[..., None] * b_re - r_im[..., None] * b_im
    bb_im = r_re[..., None] * b_im + r_im[..., None] * b_re
    c_re = jnp.swapaxes(s5_C_re[l], -1, -2)
    c_im = jnp.swapaxes(s5_C_im[l], -1, -2)

    pk_re = jnp.swapaxes(pw_re[:, :, :t], -1, -2)[..., None]
    pk_im = jnp.swapaxes(pw_im[:, :, :t], -1, -2)[..., None]
    cc_re, cc_im = c_re[:, :, :, None, :], c_im[:, :, :, None, :]
    g_re = pk_re * cc_re - pk_im * cc_im
    g_im = pk_re * cc_im + pk_im * cc_re
    gmat = jnp.concatenate([g_re, g_im], axis=2).reshape(2 * S5_GROUPS, 2 * S5_P, S5_TW)
    amat = jnp.concatenate([jnp.swapaxes(bb_re, -1, -2), -jnp.swapaxes(bb_im, -1, -2)], axis=-1)
    amat = amat.reshape(2 * S5_GROUPS, S5_CH, 2 * S5_P)
    krow = pl.pallas_call(
        _s5_kernel_table_kernel,
        grid=(2 * S5_GROUPS,),
        in_specs=[pl.BlockSpec((1, S5_CH, 2 * S5_P), lambda i: (i, 0, 0)),
                  pl.BlockSpec((1, 2 * S5_P, S5_TW), lambda i: (i, 0, 0))],
        out_specs=pl.BlockSpec((1, S5_CH, S5_TW), lambda i: (i, 0, 0)),
        out_shape=jax.ShapeDtypeStruct((2 * S5_GROUPS, S5_CH, S5_TW), F32),
        compiler_params=_cparams("parallel"),
        name="s5_toeplitz_kernels",
    )(amat, gmat).reshape(2, S5_GROUPS, S5_CH, t, S5_CH)

    s_idx = jnp.arange(t)[:, None]
    t_idx = jnp.arange(t)[None, :]
    lag_f = t_idx - s_idx
    lag_b = s_idx - t_idx
    kf = jnp.where((lag_f >= 0)[None, None, :, :, None], krow[0][:, :, jnp.clip(lag_f, 0, t - 1), :], 0.0)
    kb = jnp.where((lag_b >= 0)[None, None, :, :, None], krow[1][:, :, jnp.clip(lag_b, 0, t - 1), :], 0.0)
    toep = jnp.transpose(kf + kb, (0, 2, 1, 3, 4)).reshape(S5_GROUPS, S5_TW, S5_TW)

    def state_cols(p_re, p_im, bre, bim):
        p_re, p_im = p_re[:, :, None, :], p_im[:, :, None, :]
        bre, bim = jnp.swapaxes(bre, -1, -2)[:, None], jnp.swapaxes(bim, -1, -2)[:, None]
        re = p_re * bre - p_im * bim
        im = p_re * bim + p_im * bre
        return re.reshape(S5_GROUPS, S5_TW, S5_P), im.reshape(S5_GROUPS, S5_TW, S5_P)

    sf_re, sf_im = state_cols(pw_re[0, :, t - 1::-1][:, :t], pw_im[0, :, t - 1::-1][:, :t], bb_re[0], bb_im[0])
    sb_re, sb_im = state_cols(pw_re[1, :, :t], pw_im[1, :, :t], bb_re[1], bb_im[1])
    rhs = jnp.concatenate([toep, sf_re, sb_re, sf_im, sb_im], axis=-1).astype(BF16)

    def readout(p_re, p_im, cre, cim):
        p_re = jnp.swapaxes(p_re, -1, -2)[..., None]
        p_im = jnp.swapaxes(p_im, -1, -2)[..., None]
        cre, cim = cre[:, :, None, :], cim[:, :, None, :]
        w_re = p_re * cre - p_im * cim
        w_im = -(p_re * cim + p_im * cre)
        return w_re.reshape(S5_GROUPS, S5_P, S5_TW), w_im.reshape(S5_GROUPS, S5_P, S5_TW)

    of_re, of_im = readout(pw_re[0, :, 1:], pw_im[0, :, 1:], c_re[0], c_im[0])
    ob_re, ob_im = readout(pw_re[1, :, :0:-1], pw_im[1, :, :0:-1], c_re[1], c_im[1])
    woff = jnp.concatenate([of_re, ob_re, of_im, ob_im], axis=1).astype(BF16)
    lam_t = jnp.concatenate([pw_re[0, :, t], pw_re[1, :, t], pw_im[0, :, t], pw_im[1, :, t]],
                            axis=-1).reshape(S5_GROUPS, 1, S5_ST)
    return rhs, woff, lam_t


def _s5_kernel(u_ref, rhs_ref, woff_ref, lam_ref, h0_ref, y_ref, hfin_ref, s_scr, hin_scr):
    nc, bp, _ = s_scr.shape
    z = _dot(u_ref[0], rhs_ref[0])
    s_scr[...] = z[:, S5_TW:].reshape(nc, bp, S5_ST)
    lam = lam_ref[0]
    a_re = lam[:, 0:2 * S5_P]
    a_im = lam[:, 2 * S5_P:]
    fwd_lane = lax.broadcasted_iota(jnp.int32, (bp, 2 * S5_P), 1) < S5_P
    h0 = h0_ref[0]

    def body(i, carry):
        h_re, h_im = carry
        j = nc - 1 - i
        s_f = s_scr[i]
        s_b = s_scr[j]
        hin_scr[i, :, 0:S5_P] = h_re[:, 0:S5_P]
        hin_scr[i, :, 2 * S5_P:3 * S5_P] = h_im[:, 0:S5_P]
        hin_scr[j, :, S5_P:2 * S5_P] = h_re[:, S5_P:]
        hin_scr[j, :, 3 * S5_P:] = h_im[:, S5_P:]
        s_re = jnp.where(fwd_lane, s_f[:, 0:2 * S5_P], s_b[:, 0:2 * S5_P])
        s_im = jnp.where(fwd_lane, s_f[:, 2 * S5_P:], s_b[:, 2 * S5_P:])
        return (a_re * h_re - a_im * h_im + s_re, a_re * h_im + a_im * h_re + s_im)

    h_re, h_im = lax.fori_loop(0, nc, body, (h0[:, 0:2 * S5_P], h0[:, 2 * S5_P:]))
    hfin_ref[0] = jnp.concatenate([h_re, h_im], axis=-1)
    hin = hin_scr[...].reshape(nc * bp, S5_ST).astype(BF16)
    y_ref[0] = z[:, 0:S5_TW] + _dot(hin, woff_ref[0])


def _s5_mixer(u, tables, h0, *, n_seq, seq_len):
    rhs, woff, lam_t = tables
    t = S5_T
    nc = seq_len // t
    bp = -(-n_seq // 8) * 8
    ut = u.astype(BF16).reshape(n_seq, nc, t, S5_GROUPS, S5_CH)
    ut = jnp.transpose(ut, (3, 1, 0, 2, 4))
    ut = jnp.pad(ut, ((0, 0), (0, 0), (0, bp - n_seq), (0, 0), (0, 0))).reshape(S5_GROUPS, nc * bp, S5_TW)
    if h0 is None:
        h0 = jnp.zeros((S5_GROUPS, bp, S5_ST), F32)
    grp = lambda shape: pl.BlockSpec((1,) + shape, lambda g: (g, 0, 0))
    y, hfin = pl.pallas_call(
        _s5_kernel,
        grid=(S5_GROUPS,),
        in_specs=[grp((nc * bp, S5_TW)), grp(rhs.shape[1:]), grp(woff.shape[1:]), grp((1, S5_ST)),
                  grp((bp, S5_ST))],
        out_specs=[grp((nc * bp, S5_TW)), grp((bp, S5_ST))],
        out_shape=[jax.ShapeDtypeStruct((S5_GROUPS, nc * bp, S5_TW), F32),
                   jax.ShapeDtypeStruct((S5_GROUPS, bp, S5_ST), F32)],
        scratch_shapes=[pltpu.VMEM((nc, bp, S5_ST), F32), pltpu.VMEM((nc, bp, S5_ST), F32)],
        compiler_params=_cparams("parallel"),
        name="s5_mixer",
    )(ut, rhs, woff, lam_t, h0)
    y = y.reshape(S5_GROUPS, nc, bp, t, S5_CH)[:, :, :n_seq]
    y = jnp.transpose(y, (2, 1, 3, 0, 4)).reshape(n_seq * seq_len, S5_WIDTH)
    return y, hfin


def _s5_state_in(re, im, bp):
    b = re.shape[0]
    x = jnp.stack([re, im], axis=1)
    x = jnp.transpose(x, (3, 0, 1, 2, 4)).reshape(S5_GROUPS, b, S5_ST)
    return jnp.pad(x, ((0, 0), (0, bp - b), (0, 0)))


def _s5_state_out(h, n_seq):
    x = h[:, :n_seq].reshape(S5_GROUPS, n_seq, 2, 2, S5_P)
    x = jnp.transpose(x, (2, 1, 3, 0, 4))
    return x[0], x[1]


def kernel(x_prompt, x_sample, c, cache_k, cache_v, state_ssd, state_s5_re, state_s5_im, c_ctx, w_ada, b_ada, norm_mix_pre, norm_mix_post, norm_ffn_pre, norm_ffn_post, w_in, w_out, ssd_conv_w, ssd_conv_b, ssd_dt_bias, ssd_A_log, ssd_D, ssd_norm, attn_sink, s5_A_re, s5_A_im, s5_log_dt, s5_B_re, s5_B_im, s5_C_re, s5_C_im, s5_D, s5_w_glu, s5_b_glu, w_ffn_in, w_ffn_out):
    nb_ctx, len_ctx, _ = x_prompt.shape
    nb_lat, len_lat, _ = x_sample.shape
    n_ctx = nb_ctx * len_ctx
    n_lat = nb_lat * len_lat
    geom = dict(n_ctx=n_ctx, lat_len=len_lat)

    cvecs = jnp.concatenate([c_ctx[None, :], c, jnp.zeros((7 - nb_lat, D_MODEL), F32)], axis=0)
    mods = _ada(cvecs, w_ada, b_ada)
    cos128, sin128 = _rope_tables(len_lat)

    x = jnp.concatenate([x_prompt.reshape(n_ctx, D_MODEL), x_sample.reshape(n_lat, D_MODEL)], axis=0)
    new_k, new_v, new_ssd, new_s5_re, new_s5_im = [], [], [], [], []
    bp_lat = -(-nb_lat // 8) * 8
    for l in range(N_LAYERS):
        mod3 = mods[l].reshape(mods.shape[1], 1, 6 * D_MODEL)
        wl = w_in[l]
        w_in_p = jnp.concatenate(
            [wl[:, O_Q:O_K], wl[:, O_Z:O_XS], wl[:, O_XS:O_BC], wl[:, O_BC:O_DT], wl[:, O_K:O_V], wl[:, O_V:O_U],
             wl[:, O_U:O_END], wl[:, O_DT:O_Q], jnp.zeros((D_MODEL, DT_PAD - 2 * SSD_HEADS), F32)],
            axis=1).astype(BF16)
        proj = _in_proj(x, mod3, norm_mix_pre[l].reshape(1, -1), w_in_p, tm=256, **geom)

        sink = attn_sink[l]
        o_ctx = _ctx_attention(proj, sink, n_seq=nb_ctx, seq_len=len_ctx)
        ck = cache_k[:, l].reshape(nb_lat, -1, KV_WIDTH).astype(BF16)
        cv = cache_v[:, l].reshape(nb_lat, -1, KV_WIDTH).astype(BF16)
        o_lat = _lat_attention(proj, sink, ck, cv, cos128, sin128, row0=n_ctx, n_seq=nb_lat, seq_len=len_lat)
        o_att = jnp.concatenate([o_ctx, o_lat], axis=0)

        consts = _ssd_consts(l, ssd_conv_w, ssd_conv_b, ssd_dt_bias, ssd_A_log, ssd_D, ssd_norm)
        zeros_ssd = jnp.zeros((nb_ctx, SSD_STATE, SSD_WIDTH), F32)
        y_ctx, hf_ctx, hb_ctx = _ssd_mixer(proj, zeros_ssd, zeros_ssd, consts, row0=0, n_seq=nb_ctx,
                                           seq_len=len_ctx)
        y_lat, _, _ = _ssd_mixer(proj, _ssd_state_in(state_ssd[:, l, 0]), _ssd_state_in(state_ssd[:, l, 1]),
                                 consts, row0=n_ctx, n_seq=nb_lat, seq_len=len_lat)
        y_ssd = jnp.concatenate([y_ctx, y_lat], axis=0)

        tables = _s5_tables(l, s5_A_re, s5_A_im, s5_log_dt, s5_B_re, s5_B_im, s5_C_re, s5_C_im)
        u = proj[:, C_U:C_U + S5_WIDTH]
        s5_ctx, s5_fin = _s5_mixer(u[:n_ctx], tables, None, n_seq=nb_ctx, seq_len=len_ctx)
        h0_s5 = _s5_state_in(state_s5_re[:, l], state_s5_im[:, l], bp_lat)
        s5_lat, _ = _s5_mixer(u[n_ctx:], tables, h0_s5, n_seq=nb_lat, seq_len=len_lat)
        y_s5 = jnp.concatenate([s5_ctx, s5_lat], axis=0)

        wo = w_out[l].astype(BF16)
        x = _out_proj(x, mod3, norm_mix_post[l].reshape(1, -1), y_ssd, o_att, y_s5, proj,
                      s5_D[l].reshape(1, -1), s5_w_glu[l].astype(BF16), s5_b_glu[l].reshape(1, -1),
                      wo[:SSD_WIDTH], wo[SSD_WIDTH:SSD_WIDTH + ATT_WIDTH], wo[SSD_WIDTH + ATT_WIDTH:],
                      tm=256, **geom)
        x = _ffn(x, mod3, norm_ffn_pre[l].reshape(1, -1), norm_ffn_post[l].reshape(1, -1),
                 w_ffn_in[l].astype(BF16), w_ffn_out[l].astype(BF16), tm=512, tf=512, **geom)

        new_k.append(proj[:n_ctx, C_K:C_K + KV_WIDTH].reshape(nb_ctx, len_ctx, ATT_KV, HEAD_DIM))
        new_v.append(proj[:n_ctx, C_V:C_V + KV_WIDTH].reshape(nb_ctx, len_ctx, ATT_KV, HEAD_DIM))
        new_ssd.append(jnp.stack([_ssd_state_out(hf_ctx), _ssd_state_out(hb_ctx)], axis=1))
        re, im = _s5_state_out(s5_fin, nb_ctx)
        new_s5_re.append(re)
        new_s5_im.append(im)

    y_prompt = x[:n_ctx].reshape(nb_ctx, len_ctx, D_MODEL)
    y_sample = x[n_ctx:].reshape(nb_lat, len_lat, D_MODEL)
    return (y_prompt, y_sample, jnp.stack(new_k, axis=1), jnp.stack(new_v, axis=1), jnp.stack(new_ssd, axis=1),
            jnp.stack(new_s5_re, axis=1), jnp.stack(new_s5_im, axis=1))
```

```python
import functools
import math

import jax
import jax.numpy as jnp
from jax import lax
from jax.experimental import pallas as pl
from jax.experimental.pallas import tpu as pltpu

F32 = jnp.float32
BF16 = jnp.bfloat16
HIGHEST = lax.Precision.HIGHEST

D_MODEL = 2048
N_LAYERS = 2
D_FF = 5632
GRID_W = 64
EPS = 1e-6
LANES = 128

SSD_HEADS = 12
SSD_HEAD_DIM = 64
SSD_WIDTH = SSD_HEADS * SSD_HEAD_DIM
SSD_GROUPS = 2
SSD_STATE = 64
SSD_BC = 2 * SSD_GROUPS * SSD_STATE
SSD_CHUNK = 128
SSD_HALO = 8
ATT_HEADS = 12
ATT_KV = 4
ATT_GROUP = ATT_HEADS // ATT_KV
HEAD_DIM = 64
ATT_WIDTH = ATT_HEADS * HEAD_DIM
KV_WIDTH = ATT_KV * HEAD_DIM
ATT_BLOCK = 128
ROPE_PER_AXIS = HEAD_DIM // 4
ROPE_BASE = 10000.0
S5_GROUPS = 32
S5_CH = 16
S5_WIDTH = S5_GROUPS * S5_CH
S5_P = 64
S5_T = 32
S5_TW = S5_T * S5_CH
S5_ST = 4 * S5_P
S5_OCT = LANES // S5_CH
S5_MAX_ROWS = 256
MIX_WIDTH = SSD_WIDTH + ATT_WIDTH + S5_WIDTH

O_Z = 0
O_XS = SSD_WIDTH
O_BC = O_XS + SSD_WIDTH
O_DT = O_BC + SSD_BC
O_Q = O_DT + 2 * SSD_HEADS
O_K = O_Q + ATT_WIDTH
O_V = O_K + KV_WIDTH
O_U = O_V + KV_WIDTH
O_END = O_U + S5_WIDTH
C_Q = 0
C_Z = 768
C_XS = 1536
C_BC = 2304
C_K = 2560
C_V = 2816
C_U = 3072
C_DT = 3584
DT_PAD = 128
D_INP = C_DT + DT_PAD

NEG = -1e30
VMEM_LIMIT = 56 * 1024 * 1024


def _cparams(*sem):
    return pltpu.CompilerParams(dimension_semantics=sem, vmem_limit_bytes=VMEM_LIMIT)


def _resident(shape):
    nd = len(shape)
    return pl.BlockSpec(shape, lambda *_: (0,) * nd, pipeline_mode=pl.Buffered(1))


def _rms(x, g):
    return x * lax.rsqrt(jnp.mean(x * x, axis=-1, keepdims=True) + EPS) * g


def _silu(x):
    return x * jax.nn.sigmoid(x)


def _dot(a, b):
    return jnp.dot(a, b, preferred_element_type=F32)


def _dot_hi(a, b):
    return jnp.dot(a, b, preferred_element_type=F32, precision=HIGHEST)


def _dot_nt(a, b):
    return lax.dot_general(a, b, (((1,), (1,)), ((), ())), preferred_element_type=F32)


def _dot_tn(a, b):
    return lax.dot_general(a, b, (((0,), (0,)), ((), ())), preferred_element_type=F32)


def _split_bf16(x, parts):
    out = []
    for _ in range(parts):
        p = x.astype(BF16)
        out.append(p)
        x = x - p.astype(F32)
    return out


def _dot_sel_rhs(x, sel, parts):
    acc = None
    for p in _split_bf16(x, parts):
        t = _dot(p, sel)
        acc = t if acc is None else acc + t
    return acc


def _dot_sel_lhs(sel, x, parts):
    acc = None
    for p in _split_bf16(x, parts):
        t = _dot(sel, p)
        acc = t if acc is None else acc + t
    return acc


def _mod_spec(tm, rows_per_mod, first_row):
    return pl.BlockSpec((1, 1, 6 * D_MODEL), lambda i, *_: (first_row + (i * tm) // rows_per_mod, 0, 0))


def _ada_kernel(c_ref, w_ref, b_ref, o_ref):
    c = c_ref[...]
    o_ref[0] = _dot(_silu(c).astype(BF16), w_ref[0].astype(BF16)) + b_ref[0]


def _ada(cvecs, w_ada, b_ada):
    tn = 1024
    rows = cvecs.shape[0]
    return pl.pallas_call(
        _ada_kernel,
        grid=(N_LAYERS, 6 * D_MODEL // tn),
        in_specs=[pl.BlockSpec((rows, D_MODEL), lambda l, j: (0, 0)),
                  pl.BlockSpec((1, D_MODEL, tn), lambda l, j: (l, 0, j)),
                  pl.BlockSpec((1, 1, tn), lambda l, j: (l, 0, j))],
        out_specs=pl.BlockSpec((1, rows, tn), lambda l, j: (l, 0, j)),
        out_shape=jax.ShapeDtypeStruct((N_LAYERS, rows, 6 * D_MODEL), F32),
        compiler_params=_cparams("parallel", "parallel"),
        name="ada_mod",
    )(cvecs, w_ada, b_ada.reshape(N_LAYERS, 1, 6 * D_MODEL))


def _in_proj_kernel(x_ref, mod_ref, g_ref, w_ref, o_ref):
    mod = mod_ref[0]
    sh = mod[:, 0:D_MODEL]
    sc = mod[:, D_MODEL:2 * D_MODEL]
    h = _rms(x_ref[...], g_ref[...]) * (1.0 + sc) + sh
    o_ref[...] = _dot(h.astype(BF16), w_ref[...])


def _in_proj(x, mod3, gamma, w, *, tm, mod_spec):
    n = x.shape[0]
    return pl.pallas_call(
        _in_proj_kernel,
        grid=(n // tm,),
        in_specs=[pl.BlockSpec((tm, D_MODEL), lambda i: (i, 0)),
                  mod_spec,
                  _resident((1, D_MODEL)),
                  _resident((D_MODEL, D_INP))],
        out_specs=pl.BlockSpec((tm, D_INP), lambda i: (i, 0)),
        out_shape=jax.ShapeDtypeStruct((n, D_INP), F32),
        compiler_params=_cparams("parallel"),
        name="in_proj",
    )(x, mod3, gamma, w)


def _gelu_tanh(x):
    return 0.5 * x * (1.0 + jnp.tanh(math.sqrt(2.0 / math.pi) * (x + 0.044715 * (x * x * x))))


def _out_proj_kernel(x_ref, mod_ref, g_ref, yssd_ref, oatt_ref, ys5_ref, u_ref, d_ref, wglu_ref, bglu_ref,
                     w1_ref, w2_ref, w3_ref, o_ref):
    y5 = ys5_ref[...] + d_ref[...] * u_ref[...]
    g = _gelu_tanh(y5)
    s5 = g * jax.nn.sigmoid(_dot(g.astype(BF16), wglu_ref[...]) + bglu_ref[...])
    mix = (_dot(yssd_ref[...], w1_ref[...]) + _dot(oatt_ref[...], w2_ref[...])
           + _dot(s5.astype(BF16), w3_ref[...]))
    gate = mod_ref[0][:, 2 * D_MODEL:3 * D_MODEL]
    o_ref[...] = x_ref[...] + gate * _rms(mix, g_ref[...])


def _out_proj(x, mod3, gamma, y_ssd, o_att, y_s5, proj, s5_d, w_glu, b_glu, w1, w2, w3, *, tm, mod_spec):
    n = x.shape[0]
    row = lambda i: (i, 0)
    return pl.pallas_call(
        _out_proj_kernel,
        grid=(n // tm,),
        in_specs=[pl.BlockSpec((tm, D_MODEL), row),
                  mod_spec,
                  _resident((1, D_MODEL)),
                  pl.BlockSpec((tm, SSD_WIDTH), row),
                  pl.BlockSpec((tm, ATT_WIDTH), row),
                  pl.BlockSpec((tm, S5_WIDTH), row),
                  pl.BlockSpec((tm, S5_WIDTH), lambda i: (i, C_U // S5_WIDTH)),
                  _resident((1, S5_WIDTH)),
                  _resident((S5_WIDTH, S5_WIDTH)),
                  _resident((1, S5_WIDTH)),
                  _resident((SSD_WIDTH, D_MODEL)),
                  _resident((ATT_WIDTH, D_MODEL)),
                  _resident((S5_WIDTH, D_MODEL))],
        out_specs=pl.BlockSpec((tm, D_MODEL), row),
        out_shape=jax.ShapeDtypeStruct((n, D_MODEL), F32),
        compiler_params=_cparams("parallel"),
        name="out_proj",
    )(x, mod3, gamma, y_ssd, o_att, y_s5, proj, s5_d, w_glu, b_glu, w1, w2, w3)


def _ffn_kernel(x_ref, mod_ref, gpre_ref, gpost_ref, wg_ref, wu_ref, wo_ref, o_ref, h_scr, act_scr, *, nf):
    j = pl.program_id(1)
    tf = wg_ref.shape[1]
    tn = wo_ref.shape[1]

    @pl.when(j == 0)
    def _():
        mod = mod_ref[0]
        sh = mod[:, 3 * D_MODEL:4 * D_MODEL]
        sc = mod[:, 4 * D_MODEL:5 * D_MODEL]
        h = _rms(x_ref[...], gpre_ref[...]) * (1.0 + sc) + sh
        h_scr[...] = h.astype(BF16)

    @pl.when(j < nf)
    def _():
        h = h_scr[...]
        act = _silu(_dot(h, wg_ref[...])) * _dot(h, wu_ref[...])
        act_scr[:, pl.ds(pl.multiple_of(j * tf, tf), tf)] = act.astype(BF16)

    @pl.when(j >= nf)
    def _():
        o_ref[:, pl.ds(pl.multiple_of((j - nf) * tn, tn), tn)] = _dot(act_scr[...], wo_ref[...])

    @pl.when(j == pl.num_programs(1) - 1)
    def _():
        gate = mod_ref[0][:, 5 * D_MODEL:6 * D_MODEL]
        o_ref[...] = x_ref[...] + gate * _rms(o_ref[...], gpost_ref[...])


def _ffn(x, mod3, g_pre, g_post, w_in, w_out, *, tm, tf, tn, mod_spec):
    n = x.shape[0]
    nf = D_FF // tf
    nn = D_MODEL // tn
    return pl.pallas_call(
        functools.partial(_ffn_kernel, nf=nf),
        grid=(n // tm, nf + nn),
        in_specs=[pl.BlockSpec((tm, D_MODEL), lambda i, j: (i, 0)),
                  mod_spec,
                  _resident((1, D_MODEL)),
                  _resident((1, D_MODEL)),
                  pl.BlockSpec((D_MODEL, tf), lambda i, j: (0, jnp.minimum(j, nf - 1))),
                  pl.BlockSpec((D_MODEL, tf), lambda i, j: (0, jnp.minimum(j, nf - 1) + nf)),
                  pl.BlockSpec((D_FF, tn), lambda i, j: (0, jnp.maximum(j - nf, 0)))],
        out_specs=pl.BlockSpec((tm, D_MODEL), lambda i, j: (i, 0)),
        out_shape=jax.ShapeDtypeStruct((n, D_MODEL), F32),
        scratch_shapes=[pltpu.VMEM((tm, D_MODEL), BF16), pltpu.VMEM((tm, D_FF), BF16)],
        compiler_params=_cparams("parallel", "arbitrary"),
        name="ffn",
    )(x, mod3, g_pre, g_post, w_in, w_in, w_out)


def _attend(q, parts, sink_ref):
    outs = []
    for h in range(ATT_HEADS):
        kv = h // ATT_GROUP
        lo, hi = kv * HEAD_DIM, (kv + 1) * HEAD_DIM
        qh = q[:, h * HEAD_DIM:(h + 1) * HEAD_DIM]
        sink = sink_ref[h]
        scores = []
        m = None
        for k, _, mask in parts:
            s = _dot_nt(qh, k[:, lo:hi])
            if mask is not None:
                s = jnp.where(mask, s, NEG)
            scores.append(s)
            sm = jnp.max(s, axis=-1, keepdims=True)
            m = sm if m is None else jnp.maximum(m, sm)
        m = jnp.maximum(m, sink)
        den = jnp.exp(sink - m)
        acc = None
        for s, (_, v, _) in zip(scores, parts):
            p = jnp.exp(s - m)
            den = den + jnp.sum(p, axis=-1, keepdims=True)
            t = _dot(p.astype(BF16), v[:, lo:hi])
            acc = t if acc is None else acc + t
        outs.append(acc / den)
    return jnp.concatenate(outs, axis=-1)


def _ctx_attn_kernel(sink_ref, q_ref, k_ref, v_ref, o_ref):
    q = (q_ref[...] * (HEAD_DIM ** -0.5)).astype(BF16)
    parts = [(k_ref[...].astype(BF16), v_ref[...].astype(BF16), None)]
    o_ref[...] = _attend(q, parts, sink_ref).astype(o_ref.dtype)


def _ctx_attention(proj, sink, *, n_seq, seq_len):
    return pl.pallas_call(
        _ctx_attn_kernel,
        grid=(n_seq,),
        in_specs=[pl.BlockSpec(memory_space=pltpu.SMEM),
                  pl.BlockSpec((seq_len, ATT_WIDTH), lambda b: (b, C_Q // ATT_WIDTH)),
                  pl.BlockSpec((seq_len, KV_WIDTH), lambda b: (b, C_K // KV_WIDTH)),
                  pl.BlockSpec((seq_len, KV_WIDTH), lambda b: (b, C_V // KV_WIDTH))],
        out_specs=pl.BlockSpec((seq_len, ATT_WIDTH), lambda b: (b, 0)),
        out_shape=jax.ShapeDtypeStruct((n_seq * seq_len, ATT_WIDTH), BF16),
        compiler_params=_cparams("parallel"),
        name="ctx_attention",
    )(sink, proj, proj, proj)


def _rope(x, cos, sin_signed, first_half):
    outs = []
    for j in range(x.shape[1] // LANES):
        xj = x[:, j * LANES:(j + 1) * LANES]
        partner = jnp.where(first_half, pltpu.roll(xj, LANES - HEAD_DIM // 2, axis=1),
                            pltpu.roll(xj, HEAD_DIM // 2, axis=1))
        outs.append(xj * cos + partner * sin_signed)
    return jnp.concatenate(outs, axis=-1)


def _lat_attn_kernel(sink_ref, q_ref, kp_ref, kc_ref, kn_ref, vp_ref, vc_ref, vn_ref, ck_ref, cv_ref,
                     cosq_ref, sinq_ref, cosp_ref, sinp_ref, cosn_ref, sinn_ref, o_ref):
    n = pl.program_id(1)
    nb = pl.num_programs(1)
    blk = ATT_BLOCK
    lane = lax.broadcasted_iota(jnp.int32, (blk, LANES), 1)
    first_half = (lane % HEAD_DIM) < (HEAD_DIM // 2)
    q = _rope(q_ref[...], cosq_ref[...], sinq_ref[...], first_half)
    q = (q * (HEAD_DIM ** -0.5)).astype(BF16)
    kp = _rope(kp_ref[...], cosp_ref[...], sinp_ref[...], first_half).astype(BF16)
    kc = _rope(kc_ref[...], cosq_ref[...], sinq_ref[...], first_half).astype(BF16)
    kn = _rope(kn_ref[...], cosn_ref[...], sinn_ref[...], first_half).astype(BF16)
    row = lax.broadcasted_iota(jnp.int32, (blk, blk), 0)
    col = lax.broadcasted_iota(jnp.int32, (blk, blk), 1)
    mask_prev = jnp.logical_and(col >= row, n > 0)
    mask_next = jnp.logical_and(col <= row, n < nb - 1)
    parts = [(ck_ref[0], cv_ref[0], None),
             (kp, vp_ref[...].astype(BF16), mask_prev),
             (kc, vc_ref[...].astype(BF16), None),
             (kn, vn_ref[...].astype(BF16), mask_next)]
    o_ref[...] = _attend(q, parts, sink_ref).astype(o_ref.dtype)


def _lat_attention(proj, sink, ck, cv, cos, sin_signed, *, n_seq, seq_len):
    blk = ATT_BLOCK
    nb = seq_len // blk

    def cur(b, n):
        return b * nb + n

    def prev(b, n):
        return b * nb + jnp.maximum(n - 1, 0)

    def nxt(b, n):
        return b * nb + jnp.minimum(n + 1, nb - 1)

    kcol, vcol = C_K // KV_WIDTH, C_V // KV_WIDTH
    tab = lambda f: pl.BlockSpec((blk, LANES), f)
    return pl.pallas_call(
        _lat_attn_kernel,
        grid=(n_seq, nb),
        in_specs=[pl.BlockSpec(memory_space=pltpu.SMEM),
                  pl.BlockSpec((blk, ATT_WIDTH), lambda b, n: (cur(b, n), C_Q // ATT_WIDTH)),
                  pl.BlockSpec((blk, KV_WIDTH), lambda b, n: (prev(b, n), kcol)),
                  pl.BlockSpec((blk, KV_WIDTH), lambda b, n: (cur(b, n), kcol)),
                  pl.BlockSpec((blk, KV_WIDTH), lambda b, n: (nxt(b, n), kcol)),
                  pl.BlockSpec((blk, KV_WIDTH), lambda b, n: (prev(b, n), vcol)),
                  pl.BlockSpec((blk, KV_WIDTH), lambda b, n: (cur(b, n), vcol)),
                  pl.BlockSpec((blk, KV_WIDTH), lambda b, n: (nxt(b, n), vcol)),
                  pl.BlockSpec((1,) + ck.shape[1:], lambda b, n: (b, 0, 0)),
                  pl.BlockSpec((1,) + cv.shape[1:], lambda b, n: (b, 0, 0)),
                  tab(lambda b, n: (n, 0)), tab(lambda b, n: (n, 0)),
                  tab(lambda b, n: (jnp.maximum(n - 1, 0), 0)), tab(lambda b, n: (jnp.maximum(n - 1, 0), 0)),
                  tab(lambda b, n: (jnp.minimum(n + 1, nb - 1), 0)),
                  tab(lambda b, n: (jnp.minimum(n + 1, nb - 1), 0))],
        out_specs=pl.BlockSpec((blk, ATT_WIDTH), lambda b, n: (cur(b, n), 0)),
        out_shape=jax.ShapeDtypeStruct((n_seq * seq_len, ATT_WIDTH), BF16),
        compiler_params=_cparams("parallel", "parallel"),
        name="lat_attention",
    )(sink, proj, proj, proj, proj, proj, proj, proj, ck, cv, cos, sin_signed, cos, sin_signed, cos, sin_signed)


def _rope_tables(seq_len):
    rows = seq_len // GRID_W
    row = jnp.repeat(jnp.arange(rows, dtype=F32), GRID_W)
    col = jnp.tile(jnp.arange(GRID_W, dtype=F32), rows)
    inv = ROPE_BASE ** (-jnp.arange(ROPE_PER_AXIS, dtype=F32) / ROPE_PER_AXIS)
    ang = jnp.concatenate([row[:, None] * inv, col[:, None] * inv], axis=-1)
    cos, sin = jnp.cos(ang), jnp.sin(ang)
    cos128 = jnp.tile(cos, (1, 4))
    sin128 = jnp.tile(jnp.concatenate([-sin, sin], axis=-1), (1, 2))
    return cos128, sin128


def _conv_silu(scr, prev_ref, cur_ref, next_ref, w_ref, b_ref, is_first, is_last):
    q = SSD_CHUNK
    h = SSD_HALO
    scr[0:h, :] = jnp.where(is_first, 0.0, prev_ref[...])
    scr[h:h + q, :] = cur_ref[...]
    scr[h + q:h + q + h, :] = jnp.where(is_last, 0.0, next_ref[...])
    y = (scr[h - 1:h - 1 + q, :] * w_ref[0:1, :] + scr[h:h + q, :] * w_ref[1:2, :]
         + scr[h + 1:h + 1 + q, :] * w_ref[2:3, :] + b_ref[...])
    return _silu(y)


def _softplus(x):
    return jnp.maximum(x, 0.0) + jnp.log1p(jnp.exp(-jnp.abs(x)))


def _ssd_bwd_kernel(xsp_ref, xsc_ref, xsn_ref, bcp_ref, bcc_ref, bcn_ref, dt_ref, h0_ref,
                    cwx_ref, cbx_ref, cwb_ref, cbb_ref, dtb_ref, alog_ref, tri_ref, exp_ref,
                    hstart_ref, hfin_ref, xs_scr, bc_scr, h_scr):
    i = pl.program_id(1)
    nc = pl.num_programs(1)

    @pl.when(i == 0)
    def _():
        h_scr[...] = h0_ref[0]

    hb = h_scr[...]
    hstart_ref[0, 0] = hb
    is_first = i == nc - 1
    is_last = i == 0
    xs = _conv_silu(xs_scr, xsp_ref, xsc_ref, xsn_ref, cwx_ref, cbx_ref, is_first, is_last)
    bc = _conv_silu(bc_scr, bcp_ref, bcc_ref, bcn_ref, cwb_ref, cbb_ref, is_first, is_last)
    bm = bc[:, 0:SSD_GROUPS * SSD_STATE].astype(BF16)
    dtv = _softplus(dt_ref[...] + dtb_ref[...])
    dta = dtv * (-jnp.exp(alog_ref[...]))
    rev = _dot_sel_lhs(tri_ref[1], dta, 3)
    sel_b = exp_ref[1]
    dt_w = _dot_sel_rhs(dtv, sel_b, 2)
    to_end = _dot_sel_rhs(jnp.exp(rev[0:1, :] - rev), sel_b, 2)
    decay = _dot_sel_rhs(jnp.exp(rev[0:SSD_HALO, :]), sel_b, 2)[0:1, :]
    xw = (xs * dt_w * to_end).astype(BF16)
    half = SSD_WIDTH // SSD_GROUPS
    upd = jnp.concatenate(
        [_dot_tn(bm[:, g * SSD_STATE:(g + 1) * SSD_STATE], xw[:, g * half:(g + 1) * half])
         for g in range(SSD_GROUPS)], axis=-1)
    h_new = hb * decay + upd
    h_scr[...] = h_new

    @pl.when(i == nc - 1)
    def _():
        hfin_ref[0] = h_new


def _ssd_fwd_kernel(z_ref, xsp_ref, xsc_ref, xsn_ref, bcp_ref, bcc_ref, bcn_ref, dt_ref, hb_ref, h0_ref,
                    cwx_ref, cbx_ref, cwb_ref, cbb_ref, dtb_ref, alog_ref, tri_ref, exp_ref, dvec_ref, nw_ref,
                    y_ref, hfin_ref, xs_scr, bc_scr, h_scr):
    c = pl.program_id(1)
    nc = pl.num_programs(1)
    q = SSD_CHUNK
    nh = SSD_HEADS

    @pl.when(c == 0)
    def _():
        h_scr[...] = h0_ref[0]

    hf = h_scr[...]
    xs = _conv_silu(xs_scr, xsp_ref, xsc_ref, xsn_ref, cwx_ref, cbx_ref, c == 0, c == nc - 1)
    bc = _conv_silu(bc_scr, bcp_ref, bcc_ref, bcn_ref, cwb_ref, cbb_ref, c == 0, c == nc - 1)
    gs = SSD_GROUPS * SSD_STATE
    bm = bc[:, 0:gs].astype(BF16)
    cm = bc[:, gs:2 * gs].astype(BF16)
    xs_bf = xs.astype(BF16)

    dtv = _softplus(dt_ref[...] + dtb_ref[...])
    a_neg = -jnp.exp(alog_ref[...])
    dta = dtv * a_neg
    lower, upper = tri_ref[0], tri_ref[1]
    cum = _dot_sel_lhs(lower, dta, 3)
    rev = _dot_sel_lhs(upper, dta, 3)
    dt_t = dtv.T
    dta_t = dta.T
    cum_t = _dot_sel_rhs(dta_t, upper, 3)
    rev_t = _dot_sel_rhs(dta_t, lower, 3)

    sel_f, sel_b = exp_ref[0], exp_ref[1]
    dtf_w = _dot_sel_rhs(dtv, sel_f, 2)
    dtb_w = _dot_sel_rhs(dtv, sel_b, 2)
    ecum_w = _dot_sel_rhs(jnp.exp(cum), sel_f, 2)
    erev_w = _dot_sel_rhs(jnp.exp(rev), sel_b, 2)
    toend_w = _dot_sel_rhs(jnp.exp(cum[q - 1:q, :] - cum), sel_f, 2)

    row = lax.broadcasted_iota(jnp.int32, (q, q), 0)
    col = lax.broadcasted_iota(jnp.int32, (q, q), 1)
    causal = row >= col
    anti = row <= col
    gmat = [_dot_nt(cm[:, g * SSD_STATE:(g + 1) * SSD_STATE], bm[:, g * SSD_STATE:(g + 1) * SSD_STATE])
            for g in range(SSD_GROUPS)]
    per_group = nh // SSD_GROUPS
    ys = []
    for h in range(nh):
        seg_f = cum[:, h:h + 1] - cum_t[h:h + 1, :]
        dec_f = jnp.exp(jnp.where(causal, seg_f, NEG)) * dt_t[h:h + 1, :]
        seg_b = rev[:, nh + h:nh + h + 1] - rev_t[nh + h:nh + h + 1, :]
        dec_b = jnp.exp(jnp.where(anti, seg_b, NEG)) * dt_t[nh + h:nh + h + 1, :]
        m = (gmat[h // per_group] * (dec_f + dec_b)).astype(BF16)
        ys.append(_dot(m, xs_bf[:, h * SSD_HEAD_DIM:(h + 1) * SSD_HEAD_DIM]))
    y = jnp.concatenate(ys, axis=-1)

    half = SSD_WIDTH // SSD_GROUPS
    hf_bf = hf.astype(BF16)
    hb_bf = hb_ref[0, 0].astype(BF16)
    off_f = jnp.concatenate([_dot(cm[:, g * SSD_STATE:(g + 1) * SSD_STATE], hf_bf[:, g * half:(g + 1) * half])
                             for g in range(SSD_GROUPS)], axis=-1)
    off_b = jnp.concatenate([_dot(cm[:, g * SSD_STATE:(g + 1) * SSD_STATE], hb_bf[:, g * half:(g + 1) * half])
                             for g in range(SSD_GROUPS)], axis=-1)
    y = y + off_f * ecum_w + off_b * erev_w + dvec_ref[...] * xs
    y = y * _silu(z_ref[...])
    y_ref[...] = _rms(y, nw_ref[...]).astype(y_ref.dtype)

    xw = (xs * dtf_w * toend_w).astype(BF16)
    upd = jnp.concatenate(
        [_dot_tn(bm[:, g * SSD_STATE:(g + 1) * SSD_STATE], xw[:, g * half:(g + 1) * half])
         for g in range(SSD_GROUPS)], axis=-1)
    h_new = hf * ecum_w[q - 1:q, :] + upd
    h_scr[...] = h_new

    @pl.when(c == nc - 1)
    def _():
        hfin_ref[0] = h_new


def _ssd_mixer(proj, h0_f, h0_b, consts, *, n_seq, seq_len):
    q = SSD_CHUNK
    nc = seq_len // q
    per8 = q // SSD_HALO
    n_rows8 = proj.shape[0] // SSD_HALO
    (cwx, cbx, cwb, cbb, dt_bias, a_log, tri, expand, dvec, nw) = consts

    def specs(chunk_of):
        def cur(b, i):
            return b * nc + chunk_of(i)

        def prev8(b, i):
            return jnp.maximum(cur(b, i) * per8 - 1, 0)

        def next8(b, i):
            return jnp.minimum((cur(b, i) + 1) * per8, n_rows8 - 1)

        xcol, bcol, dcol = C_XS // SSD_WIDTH, C_BC // SSD_BC, C_DT // DT_PAD
        return [pl.BlockSpec((SSD_HALO, SSD_WIDTH), lambda b, i: (prev8(b, i), xcol)),
                pl.BlockSpec((q, SSD_WIDTH), lambda b, i: (cur(b, i), xcol)),
                pl.BlockSpec((SSD_HALO, SSD_WIDTH), lambda b, i: (next8(b, i), xcol)),
                pl.BlockSpec((SSD_HALO, SSD_BC), lambda b, i: (prev8(b, i), bcol)),
                pl.BlockSpec((q, SSD_BC), lambda b, i: (cur(b, i), bcol)),
                pl.BlockSpec((SSD_HALO, SSD_BC), lambda b, i: (next8(b, i), bcol)),
                pl.BlockSpec((q, DT_PAD), lambda b, i: (cur(b, i), dcol))], cur

    state_spec = pl.BlockSpec((1, SSD_STATE, SSD_WIDTH), lambda b, i: (b, 0, 0))
    const_specs = [_resident(cwx.shape), _resident(cbx.shape), _resident(cwb.shape), _resident(cbb.shape),
                   _resident(dt_bias.shape), _resident(a_log.shape), _resident(tri.shape),
                   _resident(expand.shape)]
    scratch = [pltpu.VMEM((q + 2 * SSD_HALO, SSD_WIDTH), F32), pltpu.VMEM((q + 2 * SSD_HALO, SSD_BC), F32),
               pltpu.VMEM((SSD_STATE, SSD_WIDTH), F32)]

    data_specs, _ = specs(lambda i: nc - 1 - i)
    hb_start, hb_fin = pl.pallas_call(
        _ssd_bwd_kernel,
        grid=(n_seq, nc),
        in_specs=data_specs + [state_spec] + const_specs,
        out_specs=[pl.BlockSpec((1, 1, SSD_STATE, SSD_WIDTH), lambda b, i: (b, nc - 1 - i, 0, 0)), state_spec],
        out_shape=[jax.ShapeDtypeStruct((n_seq, nc, SSD_STATE, SSD_WIDTH), F32),
                   jax.ShapeDtypeStruct((n_seq, SSD_STATE, SSD_WIDTH), F32)],
        scratch_shapes=scratch,
        compiler_params=_cparams("parallel", "arbitrary"),
        name="ssd_backward_states",
    )(proj, proj, proj, proj, proj, proj, proj, h0_b, cwx, cbx, cwb, cbb, dt_bias, a_log, tri, expand)

    data_specs, cur = specs(lambda i: i)
    y, hf_fin = pl.pallas_call(
        _ssd_fwd_kernel,
        grid=(n_seq, nc),
        in_specs=([pl.BlockSpec((q, SSD_WIDTH), lambda b, i: (cur(b, i), C_Z // SSD_WIDTH))] + data_specs
                  + [pl.BlockSpec((1, 1, SSD_STATE, SSD_WIDTH), lambda b, i: (b, i, 0, 0)), state_spec]
                  + const_specs + [_resident(dvec.shape), _resident(nw.shape)]),
        out_specs=[pl.BlockSpec((q, SSD_WIDTH), lambda b, i: (b * nc + i, 0)), state_spec],
        out_shape=[jax.ShapeDtypeStruct((n_seq * seq_len, SSD_WIDTH), BF16),
                   jax.ShapeDtypeStruct((n_seq, SSD_STATE, SSD_WIDTH), F32)],
        scratch_shapes=scratch,
        compiler_params=_cparams("parallel", "arbitrary"),
        name="ssd_forward",
    )(proj, proj, proj, proj, proj, proj, proj, proj, hb_start, h0_f,
      cwx, cbx, cwb, cbb, dt_bias, a_log, tri, expand, dvec, nw)
    return y, hf_fin, hb_fin


def _ssd_consts(l, ssd_conv_w, ssd_conv_b, ssd_dt_bias, ssd_A_log, ssd_D, ssd_norm):
    q = SSD_CHUNK
    cw, cb = ssd_conv_w[l], ssd_conv_b[l]
    pad = DT_PAD - 2 * SSD_HEADS
    dt_bias = jnp.pad(ssd_dt_bias[l].reshape(1, -1), ((0, 0), (0, pad)))
    a_log = jnp.pad(ssd_A_log[l].reshape(1, -1), ((0, 0), (0, pad)))
    r = jnp.arange(q)
    tri = jnp.stack([r[None, :] <= r[:, None], r[None, :] >= r[:, None]]).astype(BF16)
    head_of_col = jnp.arange(SSD_WIDTH) // SSD_HEAD_DIM
    j = jnp.arange(DT_PAD)
    expand = jnp.stack([j[:, None] == head_of_col[None, :],
                        j[:, None] == head_of_col[None, :] + SSD_HEADS]).astype(BF16)
    dvec = jnp.repeat(ssd_D[l], SSD_HEAD_DIM).reshape(1, SSD_WIDTH)
    return (cw[:, :SSD_WIDTH], cb[:SSD_WIDTH].reshape(1, -1), cw[:, SSD_WIDTH:], cb[SSD_WIDTH:].reshape(1, -1),
            dt_bias, a_log, tri, expand, dvec, ssd_norm[l].reshape(1, -1))


def _ssd_state_in(s):
    b = s.shape[0]
    return jnp.transpose(s, (0, 3, 1, 2)).reshape(b, SSD_STATE, SSD_WIDTH)


def _ssd_state_out(s):
    b = s.shape[0]
    return jnp.transpose(s.reshape(b, SSD_STATE, SSD_HEADS, SSD_HEAD_DIM), (0, 2, 3, 1))


def _s5_table_kernel(pw_ref, ct_ref, bb_ref, bbt_ref, sel_ref, tile_ref, rhs_ref, woff_ref):
    t = S5_T
    c = S5_CH
    tile = tile_ref[...]
    krows, offs, cols = [], [], []
    for d in range(2):
        pw_re, pw_im = pw_ref[0, 2 * d], pw_ref[0, 2 * d + 1]
        c_re = _dot_hi(ct_ref[0, :, (2 * d) * c:(2 * d + 1) * c], tile)
        c_im = _dot_hi(ct_ref[0, :, (2 * d + 1) * c:(2 * d + 2) * c], tile)
        b_re = _dot_hi(bb_ref[0, :, (2 * d) * c:(2 * d + 1) * c], tile)
        b_im = _dot_hi(bb_ref[0, :, (2 * d + 1) * c:(2 * d + 2) * c], tile)
        bt_re = bbt_ref[0, (2 * d) * c:(2 * d + 1) * c, :]
        bt_im = bbt_ref[0, (2 * d + 1) * c:(2 * d + 2) * c, :]
        lag, read, state = ((0, 1, 2), (2, 3, 0))[d]
        q_re, q_im = _dot_hi(pw_re, sel_ref[lag]), _dot_hi(pw_im, sel_ref[lag])
        g_re = q_re * c_re - q_im * c_im
        g_im = q_re * c_im + q_im * c_re
        krows.append(_dot_hi(bt_re, g_re) - _dot_hi(bt_im, g_im))
        q_re, q_im = _dot_hi(pw_re, sel_ref[read]), _dot_hi(pw_im, sel_ref[read])
        offs.append((q_re * c_re - q_im * c_im, -(q_re * c_im + q_im * c_re)))
        q_re, q_im = _dot_hi(pw_re, sel_ref[state]), _dot_hi(pw_im, sel_ref[state])
        cols.append(((q_re * b_re - q_im * b_im).T, (q_re * b_im + q_im * b_re).T))
    k_f, k_b = krows
    for s in range(t):
        r = t - 1 - s
        f = k_f if s == 0 else jnp.concatenate([jnp.zeros((c, c * s), F32), k_f[:, :S5_TW - c * s]], axis=1)
        b = k_b if r == 0 else jnp.concatenate([k_b[:, c * r:], jnp.zeros((c, c * r), F32)], axis=1)
        rhs_ref[0, s * c:(s + 1) * c, 0:S5_TW] = (f + b).astype(BF16)
    rhs_ref[0, :, S5_TW:] = jnp.concatenate([cols[0][0], cols[1][0], cols[0][1], cols[1][1]], axis=1).astype(BF16)
    woff_ref[0] = jnp.concatenate([offs[0][0], offs[1][0], offs[0][1], offs[1][1]], axis=0).astype(BF16)


def _s5_tables(s5_A_re, s5_A_im, s5_log_dt, s5_B_re, s5_B_im, s5_C_re, s5_C_im):
    t = S5_T
    n = N_LAYERS * S5_GROUPS
    step = jnp.exp(s5_log_dt)[..., None]
    k = jnp.arange(LANES, dtype=F32)
    keep = k <= t
    kk = jnp.where(keep, k, 0.0)
    mag = jnp.exp(kk * (s5_A_re * step)[..., None])
    ang = kk * (s5_A_im * step)[..., None]
    pw_re = jnp.where(keep, mag * jnp.cos(ang), 0.0)
    pw_im = jnp.where(keep, mag * jnp.sin(ang), 0.0)
    lb_re, lb_im = pw_re[..., 1], pw_im[..., 1]
    den = s5_A_re * s5_A_re + s5_A_im * s5_A_im
    r_re = ((lb_re - 1.0) * s5_A_re + lb_im * s5_A_im) / den
    r_im = (lb_im * s5_A_re - (lb_re - 1.0) * s5_A_im) / den
    b_re, b_im = s5_B_re[:, None], s5_B_im[:, None]
    bb_re = r_re[..., None] * b_re - r_im[..., None] * b_im
    bb_im = r_re[..., None] * b_im + r_im[..., None] * b_re
    c_re = jnp.swapaxes(s5_C_re, -1, -2)
    c_im = jnp.swapaxes(s5_C_im, -1, -2)

    def pack(re, im, axis):
        return jnp.concatenate([re[:, 0], im[:, 0], re[:, 1], im[:, 1]], axis=axis)

    pw = jnp.stack([pw_re[:, 0], pw_im[:, 0], pw_re[:, 1], pw_im[:, 1]], axis=2).reshape(n, 4, S5_P, LANES)
    ct = pack(c_re, c_im, -1).reshape(n, S5_P, 4 * S5_CH)
    bb = pack(bb_re, bb_im, -1)
    bbt = jnp.swapaxes(bb, -1, -2).reshape(n, 4 * S5_CH, S5_P)
    bb = bb.reshape(n, S5_P, 4 * S5_CH)
    lam_t = jnp.concatenate([pw_re[:, 0, :, :, t], pw_re[:, 1, :, :, t], pw_im[:, 0, :, :, t],
                             pw_im[:, 1, :, :, t]], axis=-1).reshape(n, 1, S5_ST)
    jt = jnp.arange(S5_TW) // S5_CH
    lane_k = jnp.arange(LANES)[:, None]
    sel = jnp.stack([lane_k == f[None, :] for f in (jt, jt + 1, t - 1 - jt, t - jt)]).astype(F32)
    tile = (jnp.arange(S5_CH)[:, None] == (jnp.arange(S5_TW) % S5_CH)[None, :]).astype(F32)
    grp = lambda shape: pl.BlockSpec((1,) + shape, lambda i: (i,) + (0,) * len(shape))
    rhs, woff = pl.pallas_call(
        _s5_table_kernel,
        grid=(n,),
        in_specs=[grp((4, S5_P, LANES)), grp((S5_P, 4 * S5_CH)), grp((S5_P, 4 * S5_CH)), grp((4 * S5_CH, S5_P)),
                  _resident(sel.shape), _resident(tile.shape)],
        out_specs=[grp((S5_TW, S5_TW + S5_ST)), grp((S5_ST, S5_TW))],
        out_shape=[jax.ShapeDtypeStruct((n, S5_TW, S5_TW + S5_ST), BF16),
                   jax.ShapeDtypeStruct((n, S5_ST, S5_TW), BF16)],
        compiler_params=_cparams("parallel"),
        name="s5_tables",
    )(pw, ct, bb, bbt, sel, tile)
    return rhs, woff, lam_t


def _seg_transpose(vs, masks):
    cur = list(vs)
    for d in (1, 2, 4):
        nxt = list(cur)
        for a in range(S5_OCT):
            if a & d:
                continue
            b = a | d
            nxt[a] = jnp.where(masks[d], cur[a], pltpu.roll(cur[b], S5_CH * d, axis=1))
            nxt[b] = jnp.where(masks[d], pltpu.roll(cur[a], LANES - S5_CH * d, axis=1), cur[b])
        cur = nxt
    return cur


def _s5_kernel(u_ref, rhs_ref, woff_ref, lam_ref, h0_ref, y_ref, hfin_ref,
               dst_scr, y_scr, sre_scr, sim_scr, hfre_scr, hfim_scr, hbre_scr, hbim_scr, *, nc, rb):
    t = S5_T
    rows = dst_scr.shape[1]
    sb = rows // nc
    seg = lax.broadcasted_iota(jnp.int32, (rb, LANES), 1) // S5_CH
    masks = {d: (seg & d) == 0 for d in (1, 2, 4)}

    def gather(i, carry):
        r0 = pl.multiple_of(i * rb, rb)
        for tq in range(t // S5_OCT):
            pieces = [u_ref[pl.ds(r0 * t + tq * S5_OCT + k, rb, stride=t), :] for k in range(S5_OCT)]
            for gl, v in enumerate(_seg_transpose(pieces, masks)):
                dst_scr[gl, pl.ds(r0, rb), tq * LANES:(tq + 1) * LANES] = v.astype(BF16)
        return carry

    lax.fori_loop(0, rows // rb, gather, 0)

    for gl in range(S5_OCT):
        z = _dot(dst_scr[gl], rhs_ref[gl])
        y_scr[gl] = z[:, 0:S5_TW]
        sre_scr[gl] = z[:, S5_TW:S5_TW + 2 * S5_P]
        sim_scr[gl] = z[:, S5_TW + 2 * S5_P:]

    fwd_lane = lax.broadcasted_iota(jnp.int32, (sb, 2 * S5_P), 1) < S5_P
    a_re = [lam_ref[gl][:, 0:2 * S5_P] for gl in range(S5_OCT)]
    a_im = [lam_ref[gl][:, 2 * S5_P:] for gl in range(S5_OCT)]

    def step(i, carry):
        fi = pl.ds(i, sb, stride=nc)
        bj = pl.ds(nc - 1 - i, sb, stride=nc)
        out = []
        for gl in range(S5_OCT):
            h_re, h_im = carry[2 * gl], carry[2 * gl + 1]
            hfre_scr[gl, fi, :] = h_re
            hfim_scr[gl, fi, :] = h_im
            hbre_scr[gl, bj, :] = h_re
            hbim_scr[gl, bj, :] = h_im
            s_re = jnp.where(fwd_lane, sre_scr[gl, fi, :], sre_scr[gl, bj, :])
            s_im = jnp.where(fwd_lane, sim_scr[gl, fi, :], sim_scr[gl, bj, :])
            out.append(a_re[gl] * h_re - a_im[gl] * h_im + s_re)
            out.append(a_re[gl] * h_im + a_im[gl] * h_re + s_im)
        return tuple(out)

    init = []
    for gl in range(S5_OCT):
        h0 = h0_ref[gl, 0]
        init += [h0[:, 0:2 * S5_P], h0[:, 2 * S5_P:]]
    fin = lax.fori_loop(0, nc, step, tuple(init))

    all_fwd = lax.broadcasted_iota(jnp.int32, (rows, 2 * S5_P), 1) < S5_P
    for gl in range(S5_OCT):
        hfin_ref[gl, 0] = jnp.concatenate([fin[2 * gl], fin[2 * gl + 1]], axis=-1)
        hin = jnp.concatenate([jnp.where(all_fwd, hfre_scr[gl], hbre_scr[gl]),
                               jnp.where(all_fwd, hfim_scr[gl], hbim_scr[gl])], axis=-1).astype(BF16)
        y_scr[gl] += _dot(hin, woff_ref[gl])

    def scatter(i, carry):
        r0 = pl.multiple_of(i * rb, rb)
        for tq in range(t // S5_OCT):
            ins = [y_scr[gl, pl.ds(r0, rb), tq * LANES:(tq + 1) * LANES] for gl in range(S5_OCT)]
            for k, v in enumerate(_seg_transpose(ins, masks)):
                y_ref[pl.ds(r0 * t + tq * S5_OCT + k, rb, stride=t), :] = v
        return carry

    lax.fori_loop(0, rows // rb, scatter, 0)


def _s5_mixer(proj, tables, layer, h0, *, n_seq, seq_len):
    rhs, woff, lam_t = tables
    t = S5_T
    nc = seq_len // t
    sb = max(1, min(n_seq, S5_MAX_ROWS // nc))
    assert n_seq % sb == 0
    nbs = n_seq // sb
    rows = sb * nc
    rb = min(rows, 32)
    noct = S5_GROUPS // S5_OCT
    ucol = C_U // LANES
    tab = lambda shape: pl.BlockSpec((S5_OCT,) + shape, lambda j, s: (layer * noct + j, 0, 0))
    state = pl.BlockSpec((S5_OCT, 1, sb, S5_ST), lambda j, s: (j, s, 0, 0))
    rows_scr = lambda dt: pltpu.VMEM((S5_OCT, rows, 2 * S5_P), dt)
    y, hfin = pl.pallas_call(
        functools.partial(_s5_kernel, nc=nc, rb=rb),
        grid=(noct, nbs),
        in_specs=[pl.BlockSpec((sb * seq_len, LANES), lambda j, s: (s, ucol + j)),
                  tab(rhs.shape[1:]), tab(woff.shape[1:]), tab(lam_t.shape[1:]), state],
        out_specs=[pl.BlockSpec((sb * seq_len, LANES), lambda j, s: (s, j)), state],
        out_shape=[jax.ShapeDtypeStruct((n_seq * seq_len, S5_WIDTH), F32),
                   jax.ShapeDtypeStruct((S5_GROUPS, nbs, sb, S5_ST), F32)],
        scratch_shapes=[pltpu.VMEM((S5_OCT, rows, S5_TW), BF16), pltpu.VMEM((S5_OCT, rows, S5_TW), F32),
                        rows_scr(F32), rows_scr(F32), rows_scr(F32), rows_scr(F32), rows_scr(F32), rows_scr(F32)],
        compiler_params=_cparams("parallel", "parallel"),
        name="s5_mixer",
    )(proj, rhs, woff, lam_t, h0.reshape(S5_GROUPS, nbs, sb, S5_ST))
    return y, hfin.reshape(S5_GROUPS, n_seq, S5_ST)


def _s5_state_in(re, im):
    b = re.shape[0]
    x = jnp.stack([re, im], axis=1)
    return jnp.transpose(x, (3, 0, 1, 2, 4)).reshape(S5_GROUPS, b, S5_ST)


def _s5_state_out(h):
    n_seq = h.shape[1]
    x = h.reshape(S5_GROUPS, n_seq, 2, 2, S5_P)
    x = jnp.transpose(x, (2, 1, 3, 0, 4))
    return x[0], x[1]


def kernel(x_prompt, x_sample, c, cache_k, cache_v, state_ssd, state_s5_re, state_s5_im, c_ctx, w_ada, b_ada, norm_mix_pre, norm_mix_post, norm_ffn_pre, norm_ffn_post, w_in, w_out, ssd_conv_w, ssd_conv_b, ssd_dt_bias, ssd_A_log, ssd_D, ssd_norm, attn_sink, s5_A_re, s5_A_im, s5_log_dt, s5_B_re, s5_B_im, s5_C_re, s5_C_im, s5_D, s5_w_glu, s5_b_glu, w_ffn_in, w_ffn_out):
    nb_ctx, len_ctx, _ = x_prompt.shape
    nb_lat, len_lat, _ = x_sample.shape
    n_ctx = nb_ctx * len_ctx
    n_lat = nb_lat * len_lat
    tm = 512
    mod_rows = -(-(1 + nb_lat) // 8) * 8
    cvecs = jnp.concatenate([c_ctx[None, :], c, jnp.zeros((mod_rows - 1 - nb_lat, D_MODEL), F32)], axis=0)
    mods = _ada(cvecs, w_ada, b_ada)
    token_sets = (dict(mod_spec=_mod_spec(tm, n_ctx, 0)), dict(mod_spec=_mod_spec(tm, len_lat, 1)))

    cos128, sin128 = _rope_tables(len_lat)
    s5_tab = _s5_tables(s5_A_re, s5_A_im, s5_log_dt, s5_B_re, s5_B_im, s5_C_re, s5_C_im)
    w_in_p = jnp.concatenate(
        [w_in[:, :, O_Q:O_K], w_in[:, :, O_Z:O_XS], w_in[:, :, O_XS:O_BC], w_in[:, :, O_BC:O_DT],
         w_in[:, :, O_K:O_V], w_in[:, :, O_V:O_U], w_in[:, :, O_U:O_END], w_in[:, :, O_DT:O_Q],
         jnp.zeros((N_LAYERS, D_MODEL, DT_PAD - 2 * SSD_HEADS), F32)], axis=2).astype(BF16)
    w_out_b = w_out.astype(BF16)
    w_glu_b = s5_w_glu.astype(BF16)
    w_ffn_in_b = w_ffn_in.astype(BF16)
    w_ffn_out_b = w_ffn_out.astype(BF16)
    zeros_ssd = jnp.zeros((nb_ctx, SSD_STATE, SSD_WIDTH), F32)
    zeros_s5 = jnp.zeros((S5_GROUPS, nb_ctx, S5_ST), F32)

    xs = [x_prompt.reshape(n_ctx, D_MODEL), x_sample.reshape(n_lat, D_MODEL)]
    new_k, new_v, new_ssd, new_s5_re, new_s5_im = [], [], [], [], []
    for l in range(N_LAYERS):
        mod3 = mods[l].reshape(mod_rows, 1, 6 * D_MODEL)
        proj = [_in_proj(x, mod3, norm_mix_pre[l].reshape(1, -1), w_in_p[l], tm=tm, **ts)
                for x, ts in zip(xs, token_sets)]

        sink = attn_sink[l]
        ck = cache_k[:, l].reshape(nb_lat, -1, KV_WIDTH).astype(BF16)
        cv = cache_v[:, l].reshape(nb_lat, -1, KV_WIDTH).astype(BF16)
        o_att = [_ctx_attention(proj[0], sink, n_seq=nb_ctx, seq_len=len_ctx),
                 _lat_attention(proj[1], sink, ck, cv, cos128, sin128, n_seq=nb_lat, seq_len=len_lat)]

        consts = _ssd_consts(l, ssd_conv_w, ssd_conv_b, ssd_dt_bias, ssd_A_log, ssd_D, ssd_norm)
        y_ctx, hf_ctx, hb_ctx = _ssd_mixer(proj[0], zeros_ssd, zeros_ssd, consts, n_seq=nb_ctx, seq_len=len_ctx)
        y_lat, _, _ = _ssd_mixer(proj[1], _ssd_state_in(state_ssd[:, l, 0]), _ssd_state_in(state_ssd[:, l, 1]),
                                 consts, n_seq=nb_lat, seq_len=len_lat)
        y_ssd = [y_ctx, y_lat]

        s5_ctx, s5_fin = _s5_mixer(proj[0], s5_tab, l, zeros_s5, n_seq=nb_ctx, seq_len=len_ctx)
        s5_lat, _ = _s5_mixer(proj[1], s5_tab, l, _s5_state_in(state_s5_re[:, l], state_s5_im[:, l]),
                              n_seq=nb_lat, seq_len=len_lat)
        y_s5 = [s5_ctx, s5_lat]

        wo = w_out_b[l]
        xs = [_out_proj(x, mod3, norm_mix_post[l].reshape(1, -1), y_ssd[i], o_att[i], y_s5[i], proj[i],
                        s5_D[l].reshape(1, -1), w_glu_b[l], s5_b_glu[l].reshape(1, -1),
                        wo[:SSD_WIDTH], wo[SSD_WIDTH:SSD_WIDTH + ATT_WIDTH], wo[SSD_WIDTH + ATT_WIDTH:],
                        tm=tm, **ts)
              for i, (x, ts) in enumerate(zip(xs, token_sets))]
        xs = [_ffn(x, mod3, norm_ffn_pre[l].reshape(1, -1), norm_ffn_post[l].reshape(1, -1),
                   w_ffn_in_b[l], w_ffn_out_b[l], tm=tm, tf=512, tn=512, **ts)
              for x, ts in zip(xs, token_sets)]

        new_k.append(proj[0][:, C_K:C_K + KV_WIDTH].reshape(nb_ctx, len_ctx, ATT_KV, HEAD_DIM))
        new_v.append(proj[0][:, C_V:C_V + KV_WIDTH].reshape(nb_ctx, len_ctx, ATT_KV, HEAD_DIM))
        new_ssd.append(jnp.stack([_ssd_state_out(hf_ctx), _ssd_state_out(hb_ctx)], axis=1))
        re, im = _s5_state_out(s5_fin)
        new_s5_re.append(re)
        new_s5_im.append(im)

    y_prompt = xs[0].reshape(nb_ctx, len_ctx, D_MODEL)
    y_sample = xs[1].reshape(nb_lat, len_lat, D_MODEL)
    return (y_prompt, y_sample, jnp.stack(new_k, axis=1), jnp.stack(new_v, axis=1), jnp.stack(new_ssd, axis=1),
            jnp.stack(new_s5_re, axis=1), jnp.stack(new_s5_im, axis=1))
```

```python
import functools
import math

import jax
import jax.numpy as jnp
from jax import lax
from jax.experimental import pallas as pl
from jax.experimental.pallas import tpu as pltpu

F32 = jnp.float32
BF16 = jnp.bfloat16
HIGHEST = lax.Precision.HIGHEST

D_MODEL = 2048
N_LAYERS = 2
D_FF = 5632
GRID_W = 64
EPS = 1e-6
LANES = 128

SSD_HEADS = 12
SSD_HEAD_DIM = 64
SSD_WIDTH = SSD_HEADS * SSD_HEAD_DIM
SSD_GROUPS = 2
SSD_STATE = 64
SSD_BC = 2 * SSD_GROUPS * SSD_STATE
SSD_CHUNK = 128
SSD_HALO = 8
ATT_HEADS = 12
ATT_KV = 4
ATT_GROUP = ATT_HEADS // ATT_KV
HEAD_DIM = 64
ATT_WIDTH = ATT_HEADS * HEAD_DIM
KV_WIDTH = ATT_KV * HEAD_DIM
ATT_BLOCK = 128
QKV_WIDTH = ATT_WIDTH + 2 * KV_WIDTH
LOG2E = math.log2(math.e)
Q_SCALE = HEAD_DIM ** -0.5 * LOG2E
ROPE_PER_AXIS = HEAD_DIM // 4
ROPE_BASE = 10000.0
S5_GROUPS = 32
S5_CH = 16
S5_WIDTH = S5_GROUPS * S5_CH
S5_P = 64
S5_T = 32
S5_TW = S5_T * S5_CH
S5_ST = 4 * S5_P
S5_OCT = LANES // S5_CH
S5_MAX_ROWS = 256
MIX_WIDTH = SSD_WIDTH + ATT_WIDTH + S5_WIDTH

O_Z = 0
O_XS = SSD_WIDTH
O_BC = O_XS + SSD_WIDTH
O_DT = O_BC + SSD_BC
O_Q = O_DT + 2 * SSD_HEADS
O_K = O_Q + ATT_WIDTH
O_V = O_K + KV_WIDTH
O_U = O_V + KV_WIDTH
O_END = O_U + S5_WIDTH
C_Q = 0
C_Z = 768
C_XS = 1536
C_BC = 2304
C_K = 2560
C_V = 2816
C_U = 3072
C_DT = 3584
DT_PAD = 128
D_INP = C_DT + DT_PAD

NEG = -1e30
VMEM_LIMIT = 56 * 1024 * 1024


def _cparams(*sem):
    return pltpu.CompilerParams(dimension_semantics=sem, vmem_limit_bytes=VMEM_LIMIT)


def _resident(shape):
    nd = len(shape)
    return pl.BlockSpec(shape, lambda *_: (0,) * nd, pipeline_mode=pl.Buffered(1))


def _rms(x, g):
    return x * lax.rsqrt(jnp.mean(x * x, axis=-1, keepdims=True) + EPS) * g


def _silu(x):
    return x * jax.nn.sigmoid(x)


def _dot(a, b):
    return jnp.dot(a, b, preferred_element_type=F32)


def _dot_hi(a, b):
    return jnp.dot(a, b, preferred_element_type=F32, precision=HIGHEST)


def _dot_nt(a, b):
    return lax.dot_general(a, b, (((1,), (1,)), ((), ())), preferred_element_type=F32)


def _dot_tn(a, b):
    return lax.dot_general(a, b, (((0,), (0,)), ((), ())), preferred_element_type=F32)


def _split_bf16(x, parts):
    out = []
    for _ in range(parts):
        p = x.astype(BF16)
        out.append(p)
        x = x - p.astype(F32)
    return out


def _dot_sel_rhs(x, sel, parts):
    acc = None
    for p in _split_bf16(x, parts):
        t = _dot(p, sel)
        acc = t if acc is None else acc + t
    return acc


def _dot_sel_lhs(sel, x, parts):
    acc = None
    for p in _split_bf16(x, parts):
        t = _dot(sel, p)
        acc = t if acc is None else acc + t
    return acc


def _mod_spec(tm, rows_per_mod, first_row):
    return pl.BlockSpec((1, 1, 6 * D_MODEL), lambda i, *_: (first_row + (i * tm) // rows_per_mod, 0, 0))


def _ada_kernel(c_ref, w_ref, b_ref, o_ref):
    c = c_ref[...]
    o_ref[0] = _dot(_silu(c).astype(BF16), w_ref[0].astype(BF16)) + b_ref[0]


def _ada(cvecs, w_ada, b_ada):
    tn = 1024
    rows = cvecs.shape[0]
    return pl.pallas_call(
        _ada_kernel,
        grid=(N_LAYERS, 6 * D_MODEL // tn),
        in_specs=[pl.BlockSpec((rows, D_MODEL), lambda l, j: (0, 0)),
                  pl.BlockSpec((1, D_MODEL, tn), lambda l, j: (l, 0, j)),
                  pl.BlockSpec((1, 1, tn), lambda l, j: (l, 0, j))],
        out_specs=pl.BlockSpec((1, rows, tn), lambda l, j: (l, 0, j)),
        out_shape=jax.ShapeDtypeStruct((N_LAYERS, rows, 6 * D_MODEL), F32),
        compiler_params=_cparams("parallel", "parallel"),
        name="ada_mod",
    )(cvecs, w_ada, b_ada.reshape(N_LAYERS, 1, 6 * D_MODEL))


def _rope(x, cos, sin_signed, first_half):
    outs = []
    for j in range(x.shape[1] // LANES):
        xj = x[:, j * LANES:(j + 1) * LANES]
        partner = jnp.where(first_half, pltpu.roll(xj, LANES - HEAD_DIM // 2, axis=1),
                            pltpu.roll(xj, HEAD_DIM // 2, axis=1))
        outs.append(xj * cos + partner * sin_signed)
    return jnp.concatenate(outs, axis=-1)


def _in_proj_kernel(*refs, rope):
    if rope:
        x_ref, mod_ref, g_ref, w_ref, cos_ref, sin_ref, o_ref, qkv_ref = refs
    else:
        x_ref, mod_ref, g_ref, w_ref, o_ref, qkv_ref = refs
    mod = mod_ref[0]
    sh = mod[:, 0:D_MODEL]
    sc = mod[:, D_MODEL:2 * D_MODEL]
    h = _rms(x_ref[...], g_ref[...]) * (1.0 + sc) + sh
    o_ref[...] = _dot(h.astype(BF16), w_ref[...])
    q = o_ref[:, C_Q:C_Q + ATT_WIDTH]
    k = o_ref[:, C_K:C_K + KV_WIDTH]
    if rope:
        lane = lax.broadcasted_iota(jnp.int32, (x_ref.shape[0], LANES), 1)
        first_half = (lane % HEAD_DIM) < (HEAD_DIM // 2)
        q = _rope(q, cos_ref[...], sin_ref[...], first_half)
        k = _rope(k, cos_ref[...], sin_ref[...], first_half)
    qkv_ref[:, 0:ATT_WIDTH] = (q * Q_SCALE).astype(BF16)
    qkv_ref[:, ATT_WIDTH:ATT_WIDTH + KV_WIDTH] = k.astype(BF16)
    qkv_ref[:, ATT_WIDTH + KV_WIDTH:] = o_ref[:, C_V:C_V + KV_WIDTH].astype(BF16)


def _in_proj(x, mod3, gamma, w, *, tm, mod_spec, rope_tables=None, seq_len=None):
    n = x.shape[0]
    rope = rope_tables is not None
    tables, table_specs = (), []
    if rope:
        per_seq = seq_len // tm
        tables = tuple(rope_tables)
        table_specs = [pl.BlockSpec((tm, LANES), lambda i: (i % per_seq, 0))] * 2
    return pl.pallas_call(
        functools.partial(_in_proj_kernel, rope=rope),
        grid=(n // tm,),
        in_specs=[pl.BlockSpec((tm, D_MODEL), lambda i: (i, 0)),
                  mod_spec,
                  _resident((1, D_MODEL)),
                  _resident((D_MODEL, D_INP))] + table_specs,
        out_specs=[pl.BlockSpec((tm, D_INP), lambda i: (i, 0)), pl.BlockSpec((tm, QKV_WIDTH), lambda i: (i, 0))],
        out_shape=[jax.ShapeDtypeStruct((n, D_INP), F32), jax.ShapeDtypeStruct((n, QKV_WIDTH), BF16)],
        compiler_params=_cparams("parallel"),
        name="in_proj",
    )(x, mod3, gamma, w, *tables)


def _gelu_tanh(x):
    return 0.5 * x * (1.0 + jnp.tanh(math.sqrt(2.0 / math.pi) * (x + 0.044715 * (x * x * x))))


def _out_proj_kernel(x_ref, mod_ref, g_ref, yssd_ref, oatt_ref, ys5_ref, u_ref, d_ref, wglu_ref, bglu_ref,
                     w1_ref, w2_ref, w3_ref, o_ref):
    y5 = ys5_ref[...] + d_ref[...] * u_ref[...]
    g = _gelu_tanh(y5)
    s5 = g * jax.nn.sigmoid(_dot(g.astype(BF16), wglu_ref[...]) + bglu_ref[...])
    mix = (_dot(yssd_ref[...], w1_ref[...]) + _dot(oatt_ref[...], w2_ref[...])
           + _dot(s5.astype(BF16), w3_ref[...]))
    gate = mod_ref[0][:, 2 * D_MODEL:3 * D_MODEL]
    o_ref[...] = x_ref[...] + gate * _rms(mix, g_ref[...])


def _out_proj(x, mod3, gamma, y_ssd, o_att, y_s5, proj, s5_d, w_glu, b_glu, w1, w2, w3, *, tm, mod_spec):
    n = x.shape[0]
    row = lambda i: (i, 0)
    return pl.pallas_call(
        _out_proj_kernel,
        grid=(n // tm,),
        in_specs=[pl.BlockSpec((tm, D_MODEL), row),
                  mod_spec,
                  _resident((1, D_MODEL)),
                  pl.BlockSpec((tm, SSD_WIDTH), row),
                  pl.BlockSpec((tm, ATT_WIDTH), row),
                  pl.BlockSpec((tm, S5_WIDTH), row),
                  pl.BlockSpec((tm, S5_WIDTH), lambda i: (i, C_U // S5_WIDTH)),
                  _resident((1, S5_WIDTH)),
                  _resident((S5_WIDTH, S5_WIDTH)),
                  _resident((1, S5_WIDTH)),
                  _resident((SSD_WIDTH, D_MODEL)),
                  _resident((ATT_WIDTH, D_MODEL)),
                  _resident((S5_WIDTH, D_MODEL))],
        out_specs=pl.BlockSpec((tm, D_MODEL), row),
        out_shape=jax.ShapeDtypeStruct((n, D_MODEL), F32),
        compiler_params=_cparams("parallel"),
        name="out_proj",
    )(x, mod3, gamma, y_ssd, o_att, y_s5, proj, s5_d, w_glu, b_glu, w1, w2, w3)


def _ffn_kernel(x_ref, mod_ref, gpre_ref, gpost_ref, wg_ref, wu_ref, wo_ref, o_ref, h_scr, act_scr, *, nf):
    j = pl.program_id(1)
    tf = wg_ref.shape[1]
    tn = wo_ref.shape[1]

    @pl.when(j == 0)
    def _():
        mod = mod_ref[0]
        sh = mod[:, 3 * D_MODEL:4 * D_MODEL]
        sc = mod[:, 4 * D_MODEL:5 * D_MODEL]
        h = _rms(x_ref[...], gpre_ref[...]) * (1.0 + sc) + sh
        h_scr[...] = h.astype(BF16)

    @pl.when(j < nf)
    def _():
        h = h_scr[...]
        act = _silu(_dot(h, wg_ref[...])) * _dot(h, wu_ref[...])
        act_scr[:, pl.ds(pl.multiple_of(j * tf, tf), tf)] = act.astype(BF16)

    @pl.when(j >= nf)
    def _():
        o_ref[:, pl.ds(pl.multiple_of((j - nf) * tn, tn), tn)] = _dot(act_scr[...], wo_ref[...])

    @pl.when(j == pl.num_programs(1) - 1)
    def _():
        gate = mod_ref[0][:, 5 * D_MODEL:6 * D_MODEL]
        o_ref[...] = x_ref[...] + gate * _rms(o_ref[...], gpost_ref[...])


def _ffn(x, mod3, g_pre, g_post, w_in, w_out, *, tm, tf, tn, mod_spec):
    n = x.shape[0]
    nf = D_FF // tf
    nn = D_MODEL // tn
    return pl.pallas_call(
        functools.partial(_ffn_kernel, nf=nf),
        grid=(n // tm, nf + nn),
        in_specs=[pl.BlockSpec((tm, D_MODEL), lambda i, j: (i, 0)),
                  mod_spec,
                  _resident((1, D_MODEL)),
                  _resident((1, D_MODEL)),
                  pl.BlockSpec((D_MODEL, tf), lambda i, j: (0, jnp.minimum(j, nf - 1))),
                  pl.BlockSpec((D_MODEL, tf), lambda i, j: (0, jnp.minimum(j, nf - 1) + nf)),
                  pl.BlockSpec((D_FF, tn), lambda i, j: (0, jnp.maximum(j - nf, 0)))],
        out_specs=pl.BlockSpec((tm, D_MODEL), lambda i, j: (i, 0)),
        out_shape=jax.ShapeDtypeStruct((n, D_MODEL), F32),
        scratch_shapes=[pltpu.VMEM((tm, D_MODEL), BF16), pltpu.VMEM((tm, D_FF), BF16)],
        compiler_params=_cparams("parallel", "arbitrary"),
        name="ffn",
    )(x, mod3, g_pre, g_post, w_in, w_in, w_out)


def _attend(q, parts, sink_ref):
    rows = q.shape[0]
    srow = lax.broadcasted_iota(jnp.int32, (ATT_GROUP * rows, 1), 0)
    outs = [None] * ATT_HEADS
    for kv in range(ATT_KV):
        heads = range(kv * ATT_GROUP, (kv + 1) * ATT_GROUP)
        q3 = jnp.concatenate([q[:, h * HEAD_DIM:(h + 1) * HEAD_DIM] for h in heads], axis=0)
        sink = jnp.full((ATT_GROUP * rows, 1), sink_ref[heads[-1]] * LOG2E, F32)
        for i in range(ATT_GROUP - 2, -1, -1):
            sink = jnp.where(srow < (i + 1) * rows, sink_ref[heads[i]] * LOG2E, sink)
        lo = kv * HEAD_DIM
        scores = []
        m = sink
        for k, _, mask in parts:
            s = _dot_nt(q3, k[:, lo:lo + HEAD_DIM])
            if mask is not None:
                s = jnp.where(mask, s, NEG)
            scores.append(s)
            m = jnp.maximum(m, jnp.max(s, axis=-1, keepdims=True))
        vlo = (lo // LANES) * LANES
        v_first = lo == vlo
        acc = None
        for s, (_, v, _) in zip(scores, parts):
            v128 = v[:, vlo:vlo + LANES]
            lane = lax.broadcasted_iota(jnp.int32, v128.shape, 1)
            keep = (lane < HEAD_DIM) if v_first else (lane >= HEAD_DIM)
            t = _dot(jnp.exp2(s - m).astype(BF16), jnp.where(keep, v128, jnp.ones_like(v128)))
            acc = t if acc is None else acc + t
        ocol, dcol = (0, HEAD_DIM) if v_first else (HEAD_DIM, 0)
        o3 = acc[:, ocol:ocol + HEAD_DIM] / (acc[:, dcol:dcol + 1] + jnp.exp2(sink - m))
        for i, h in enumerate(heads):
            outs[h] = o3[i * rows:(i + 1) * rows]
    return jnp.concatenate(outs, axis=-1)


def _ctx_attn_kernel(sink_ref, q_ref, k_ref, v_ref, o_ref):
    o_ref[...] = _attend(q_ref[...], [(k_ref[...], v_ref[...], None)], sink_ref).astype(o_ref.dtype)


def _ctx_attention(qkv, sink, *, n_seq, seq_len):
    kcol = ATT_WIDTH // KV_WIDTH
    return pl.pallas_call(
        _ctx_attn_kernel,
        grid=(n_seq,),
        in_specs=[pl.BlockSpec(memory_space=pltpu.SMEM),
                  pl.BlockSpec((seq_len, ATT_WIDTH), lambda b: (b, 0)),
                  pl.BlockSpec((seq_len, KV_WIDTH), lambda b: (b, kcol)),
                  pl.BlockSpec((seq_len, KV_WIDTH), lambda b: (b, kcol + 1))],
        out_specs=pl.BlockSpec((seq_len, ATT_WIDTH), lambda b: (b, 0)),
        out_shape=jax.ShapeDtypeStruct((n_seq * seq_len, ATT_WIDTH), BF16),
        compiler_params=_cparams("parallel"),
        name="ctx_attention",
    )(sink, qkv, qkv, qkv)


def _lat_attn_kernel(sink_ref, q_ref, kp_ref, kc_ref, kn_ref, vp_ref, vc_ref, vn_ref, ck_ref, cv_ref, o_ref):
    n = pl.program_id(1)
    nb = pl.num_programs(1)
    blk = ATT_BLOCK
    row = lax.broadcasted_iota(jnp.int32, (ATT_GROUP * blk, blk), 0) % blk
    col = lax.broadcasted_iota(jnp.int32, (ATT_GROUP * blk, blk), 1)
    mask_prev = jnp.logical_and(col >= row, n > 0)
    mask_next = jnp.logical_and(col <= row, n < nb - 1)
    parts = [(ck_ref[0], cv_ref[0], None),
             (kp_ref[...], vp_ref[...], mask_prev),
             (kc_ref[...], vc_ref[...], None),
             (kn_ref[...], vn_ref[...], mask_next)]
    o_ref[...] = _attend(q_ref[...], parts, sink_ref).astype(o_ref.dtype)


def _lat_attention(qkv, sink, ck, cv, *, n_seq, seq_len):
    blk = ATT_BLOCK
    nb = seq_len // blk

    def cur(b, n):
        return b * nb + n

    def prev(b, n):
        return b * nb + jnp.maximum(n - 1, 0)

    def nxt(b, n):
        return b * nb + jnp.minimum(n + 1, nb - 1)

    kcol = ATT_WIDTH // KV_WIDTH
    vcol = kcol + 1
    return pl.pallas_call(
        _lat_attn_kernel,
        grid=(n_seq, nb),
        in_specs=[pl.BlockSpec(memory_space=pltpu.SMEM),
                  pl.BlockSpec((blk, ATT_WIDTH), lambda b, n: (cur(b, n), 0)),
                  pl.BlockSpec((blk, KV_WIDTH), lambda b, n: (prev(b, n), kcol)),
                  pl.BlockSpec((blk, KV_WIDTH), lambda b, n: (cur(b, n), kcol)),
                  pl.BlockSpec((blk, KV_WIDTH), lambda b, n: (nxt(b, n), kcol)),
                  pl.BlockSpec((blk, KV_WIDTH), lambda b, n: (prev(b, n), vcol)),
                  pl.BlockSpec((blk, KV_WIDTH), lambda b, n: (cur(b, n), vcol)),
                  pl.BlockSpec((blk, KV_WIDTH), lambda b, n: (nxt(b, n), vcol)),
                  pl.BlockSpec((1,) + ck.shape[1:], lambda b, n: (b, 0, 0)),
                  pl.BlockSpec((1,) + cv.shape[1:], lambda b, n: (b, 0, 0))],
        out_specs=pl.BlockSpec((blk, ATT_WIDTH), lambda b, n: (cur(b, n), 0)),
        out_shape=jax.ShapeDtypeStruct((n_seq * seq_len, ATT_WIDTH), BF16),
        compiler_params=_cparams("parallel", "parallel"),
        name="lat_attention",
    )(sink, qkv, qkv, qkv, qkv, qkv, qkv, qkv, ck, cv)


def _rope_tables(seq_len):
    rows = seq_len // GRID_W
    row = jnp.repeat(jnp.arange(rows, dtype=F32), GRID_W)
    col = jnp.tile(jnp.arange(GRID_W, dtype=F32), rows)
    inv = ROPE_BASE ** (-jnp.arange(ROPE_PER_AXIS, dtype=F32) / ROPE_PER_AXIS)
    ang = jnp.concatenate([row[:, None] * inv, col[:, None] * inv], axis=-1)
    cos, sin = jnp.cos(ang), jnp.sin(ang)
    cos128 = jnp.tile(cos, (1, 4))
    sin128 = jnp.tile(jnp.concatenate([-sin, sin], axis=-1), (1, 2))
    return cos128, sin128


def _conv_silu(scr, prev_ref, cur_ref, next_ref, w_ref, b_ref, is_first, is_last):
    q = SSD_CHUNK
    h = SSD_HALO
    scr[0:h, :] = jnp.where(is_first, 0.0, prev_ref[...])
    scr[h:h + q, :] = cur_ref[...]
    scr[h + q:h + q + h, :] = jnp.where(is_last, 0.0, next_ref[...])
    y = (scr[h - 1:h - 1 + q, :] * w_ref[0:1, :] + scr[h:h + q, :] * w_ref[1:2, :]
         + scr[h + 1:h + 1 + q, :] * w_ref[2:3, :] + b_ref[...])
    return _silu(y)


def _softplus(x):
    return jnp.maximum(x, 0.0) + jnp.log1p(jnp.exp(-jnp.abs(x)))


def _ssd_bwd_kernel(xsp_ref, xsc_ref, xsn_ref, bcp_ref, bcc_ref, bcn_ref, dt_ref, h0_ref,
                    cwx_ref, cbx_ref, cwb_ref, cbb_ref, dtb_ref, alog_ref, tri_ref, exp_ref,
                    hstart_ref, hfin_ref, xs_scr, bc_scr, h_scr):
    i = pl.program_id(1)
    nc = pl.num_programs(1)

    @pl.when(i == 0)
    def _():
        h_scr[...] = h0_ref[0]

    hb = h_scr[...]
    hstart_ref[0, 0] = hb
    is_first = i == nc - 1
    is_last = i == 0
    xs = _conv_silu(xs_scr, xsp_ref, xsc_ref, xsn_ref, cwx_ref, cbx_ref, is_first, is_last)
    bc = _conv_silu(bc_scr, bcp_ref, bcc_ref, bcn_ref, cwb_ref, cbb_ref, is_first, is_last)
    bm = bc[:, 0:SSD_GROUPS * SSD_STATE].astype(BF16)
    dtv = _softplus(dt_ref[...] + dtb_ref[...])
    dta = dtv * (-LOG2E * jnp.exp(alog_ref[...]))
    rev = _dot_sel_lhs(tri_ref[1], dta, 3)
    sel_b = exp_ref[1]
    dt_w = _dot_sel_rhs(dtv, sel_b, 2)
    to_end = _dot_sel_rhs(jnp.exp2(rev[0:1, :] - rev), sel_b, 2)
    decay = _dot_sel_rhs(jnp.exp2(rev[0:SSD_HALO, :]), sel_b, 2)[0:1, :]
    xw = (xs * dt_w * to_end).astype(BF16)
    half = SSD_WIDTH // SSD_GROUPS
    upd = jnp.concatenate(
        [_dot_tn(bm[:, g * SSD_STATE:(g + 1) * SSD_STATE], xw[:, g * half:(g + 1) * half])
         for g in range(SSD_GROUPS)], axis=-1)
    h_new = hb * decay + upd
    h_scr[...] = h_new

    @pl.when(i == nc - 1)
    def _():
        hfin_ref[0] = h_new


def _ssd_fwd_kernel(z_ref, xsp_ref, xsc_ref, xsn_ref, bcp_ref, bcc_ref, bcn_ref, dt_ref, hb_ref, h0_ref,
                    cwx_ref, cbx_ref, cwb_ref, cbb_ref, dtb_ref, alog_ref, tri_ref, exp_ref, dvec_ref, nw_ref,
                    y_ref, hfin_ref, xs_scr, bc_scr, h_scr):
    c = pl.program_id(1)
    nc = pl.num_programs(1)
    q = SSD_CHUNK
    nh = SSD_HEADS

    @pl.when(c == 0)
    def _():
        h_scr[...] = h0_ref[0]

    hf = h_scr[...]
    xs = _conv_silu(xs_scr, xsp_ref, xsc_ref, xsn_ref, cwx_ref, cbx_ref, c == 0, c == nc - 1)
    bc = _conv_silu(bc_scr, bcp_ref, bcc_ref, bcn_ref, cwb_ref, cbb_ref, c == 0, c == nc - 1)
    gs = SSD_GROUPS * SSD_STATE
    bm = bc[:, 0:gs].astype(BF16)
    cm = bc[:, gs:2 * gs].astype(BF16)
    xs_bf = xs.astype(BF16)

    dtv = _softplus(dt_ref[...] + dtb_ref[...])
    dta = dtv * (-LOG2E * jnp.exp(alog_ref[...]))
    lower, upper = tri_ref[0], tri_ref[1]
    cum = _dot_sel_lhs(lower, dta, 3)
    rev = _dot_sel_lhs(upper, dta, 3)
    dt_t = dtv.T
    dta_t = dta.T
    cum_t = _dot_sel_rhs(dta_t, upper, 3)
    rev_t = _dot_sel_rhs(dta_t, lower, 3)

    sel_f, sel_b = exp_ref[0], exp_ref[1]
    dtf_w = _dot_sel_rhs(dtv, sel_f, 2)
    dtb_w = _dot_sel_rhs(dtv, sel_b, 2)
    ecum_w = _dot_sel_rhs(jnp.exp2(cum), sel_f, 2)
    erev_w = _dot_sel_rhs(jnp.exp2(rev), sel_b, 2)
    toend_w = _dot_sel_rhs(jnp.exp2(cum[q - 1:q, :] - cum), sel_f, 2)

    row = lax.broadcasted_iota(jnp.int32, (q, q), 0)
    col = lax.broadcasted_iota(jnp.int32, (q, q), 1)
    causal = row >= col
    anti = row <= col
    gmat = [_dot_nt(cm[:, g * SSD_STATE:(g + 1) * SSD_STATE], bm[:, g * SSD_STATE:(g + 1) * SSD_STATE])
            for g in range(SSD_GROUPS)]
    per_group = nh // SSD_GROUPS
    ys = []
    for h in range(nh):
        seg_f = cum[:, h:h + 1] - cum_t[h:h + 1, :]
        dec_f = jnp.exp2(jnp.where(causal, seg_f, NEG)) * dt_t[h:h + 1, :]
        seg_b = rev[:, nh + h:nh + h + 1] - rev_t[nh + h:nh + h + 1, :]
        dec_b = jnp.exp2(jnp.where(anti, seg_b, NEG)) * dt_t[nh + h:nh + h + 1, :]
        m = (gmat[h // per_group] * (dec_f + dec_b)).astype(BF16)
        ys.append(_dot(m, xs_bf[:, h * SSD_HEAD_DIM:(h + 1) * SSD_HEAD_DIM]))
    y = jnp.concatenate(ys, axis=-1)

    half = SSD_WIDTH // SSD_GROUPS
    hf_bf = hf.astype(BF16)
    hb_bf = hb_ref[0, 0].astype(BF16)
    off_f = jnp.concatenate([_dot(cm[:, g * SSD_STATE:(g + 1) * SSD_STATE], hf_bf[:, g * half:(g + 1) * half])
                             for g in range(SSD_GROUPS)], axis=-1)
    off_b = jnp.concatenate([_dot(cm[:, g * SSD_STATE:(g + 1) * SSD_STATE], hb_bf[:, g * half:(g + 1) * half])
                             for g in range(SSD_GROUPS)], axis=-1)
    y = y + off_f * ecum_w + off_b * erev_w + dvec_ref[...] * xs
    y = y * _silu(z_ref[...])
    y_ref[...] = _rms(y, nw_ref[...]).astype(y_ref.dtype)

    xw = (xs * dtf_w * toend_w).astype(BF16)
    upd = jnp.concatenate(
        [_dot_tn(bm[:, g * SSD_STATE:(g + 1) * SSD_STATE], xw[:, g * half:(g + 1) * half])
         for g in range(SSD_GROUPS)], axis=-1)
    h_new = hf * ecum_w[q - 1:q, :] + upd
    h_scr[...] = h_new

    @pl.when(c == nc - 1)
    def _():
        hfin_ref[0] = h_new


def _ssd_mixer(proj, h0_f, h0_b, consts, *, n_seq, seq_len):
    q = SSD_CHUNK
    nc = seq_len // q
    per8 = q // SSD_HALO
    n_rows8 = proj.shape[0] // SSD_HALO
    (cwx, cbx, cwb, cbb, dt_bias, a_log, tri, expand, dvec, nw) = consts

    def specs(chunk_of):
        def cur(b, i):
            return b * nc + chunk_of(i)

        def prev8(b, i):
            return jnp.maximum(cur(b, i) * per8 - 1, 0)

        def next8(b, i):
            return jnp.minimum((cur(b, i) + 1) * per8, n_rows8 - 1)

        xcol, bcol, dcol = C_XS // SSD_WIDTH, C_BC // SSD_BC, C_DT // DT_PAD
        return [pl.BlockSpec((SSD_HALO, SSD_WIDTH), lambda b, i: (prev8(b, i), xcol)),
                pl.BlockSpec((q, SSD_WIDTH), lambda b, i: (cur(b, i), xcol)),
                pl.BlockSpec((SSD_HALO, SSD_WIDTH), lambda b, i: (next8(b, i), xcol)),
                pl.BlockSpec((SSD_HALO, SSD_BC), lambda b, i: (prev8(b, i), bcol)),
                pl.BlockSpec((q, SSD_BC), lambda b, i: (cur(b, i), bcol)),
                pl.BlockSpec((SSD_HALO, SSD_BC), lambda b, i: (next8(b, i), bcol)),
                pl.BlockSpec((q, DT_PAD), lambda b, i: (cur(b, i), dcol))], cur

    state_spec = pl.BlockSpec((1, SSD_STATE, SSD_WIDTH), lambda b, i: (b, 0, 0))
    const_specs = [_resident(cwx.shape), _resident(cbx.shape), _resident(cwb.shape), _resident(cbb.shape),
                   _resident(dt_bias.shape), _resident(a_log.shape), _resident(tri.shape),
                   _resident(expand.shape)]
    scratch = [pltpu.VMEM((q + 2 * SSD_HALO, SSD_WIDTH), F32), pltpu.VMEM((q + 2 * SSD_HALO, SSD_BC), F32),
               pltpu.VMEM((SSD_STATE, SSD_WIDTH), F32)]

    data_specs, _ = specs(lambda i: nc - 1 - i)
    hb_start, hb_fin = pl.pallas_call(
        _ssd_bwd_kernel,
        grid=(n_seq, nc),
        in_specs=data_specs + [state_spec] + const_specs,
        out_specs=[pl.BlockSpec((1, 1, SSD_STATE, SSD_WIDTH), lambda b, i: (b, nc - 1 - i, 0, 0)), state_spec],
        out_shape=[jax.ShapeDtypeStruct((n_seq, nc, SSD_STATE, SSD_WIDTH), F32),
                   jax.ShapeDtypeStruct((n_seq, SSD_STATE, SSD_WIDTH), F32)],
        scratch_shapes=scratch,
        compiler_params=_cparams("parallel", "arbitrary"),
        name="ssd_backward_states",
    )(proj, proj, proj, proj, proj, proj, proj, h0_b, cwx, cbx, cwb, cbb, dt_bias, a_log, tri, expand)

    data_specs, cur = specs(lambda i: i)
    y, hf_fin = pl.pallas_call(
        _ssd_fwd_kernel,
        grid=(n_seq, nc),
        in_specs=([pl.BlockSpec((q, SSD_WIDTH), lambda b, i: (cur(b, i), C_Z // SSD_WIDTH))] + data_specs
                  + [pl.BlockSpec((1, 1, SSD_STATE, SSD_WIDTH), lambda b, i: (b, i, 0, 0)), state_spec]
                  + const_specs + [_resident(dvec.shape), _resident(nw.shape)]),
        out_specs=[pl.BlockSpec((q, SSD_WIDTH), lambda b, i: (b * nc + i, 0)), state_spec],
        out_shape=[jax.ShapeDtypeStruct((n_seq * seq_len, SSD_WIDTH), BF16),
                   jax.ShapeDtypeStruct((n_seq, SSD_STATE, SSD_WIDTH), F32)],
        scratch_shapes=scratch,
        compiler_params=_cparams("parallel", "arbitrary"),
        name="ssd_forward",
    )(proj, proj, proj, proj, proj, proj, proj, proj, hb_start, h0_f,
      cwx, cbx, cwb, cbb, dt_bias, a_log, tri, expand, dvec, nw)
    return y, hf_fin, hb_fin


def _ssd_consts(l, ssd_conv_w, ssd_conv_b, ssd_dt_bias, ssd_A_log, ssd_D, ssd_norm):
    q = SSD_CHUNK
    cw, cb = ssd_conv_w[l], ssd_conv_b[l]
    pad = DT_PAD - 2 * SSD_HEADS
    dt_bias = jnp.pad(ssd_dt_bias[l].reshape(1, -1), ((0, 0), (0, pad)))
    a_log = jnp.pad(ssd_A_log[l].reshape(1, -1), ((0, 0), (0, pad)))
    r = jnp.arange(q)
    tri = jnp.stack([r[None, :] <= r[:, None], r[None, :] >= r[:, None]]).astype(BF16)
    head_of_col = jnp.arange(SSD_WIDTH) // SSD_HEAD_DIM
    j = jnp.arange(DT_PAD)
    expand = jnp.stack([j[:, None] == head_of_col[None, :],
                        j[:, None] == head_of_col[None, :] + SSD_HEADS]).astype(BF16)
    dvec = jnp.repeat(ssd_D[l], SSD_HEAD_DIM).reshape(1, SSD_WIDTH)
    return (cw[:, :SSD_WIDTH], cb[:SSD_WIDTH].reshape(1, -1), cw[:, SSD_WIDTH:], cb[SSD_WIDTH:].reshape(1, -1),
            dt_bias, a_log, tri, expand, dvec, ssd_norm[l].reshape(1, -1))


def _ssd_state_in(s):
    b = s.shape[0]
    return jnp.transpose(s, (0, 3, 1, 2)).reshape(b, SSD_STATE, SSD_WIDTH)


def _ssd_state_out(s):
    b = s.shape[0]
    return jnp.transpose(s.reshape(b, SSD_STATE, SSD_HEADS, SSD_HEAD_DIM), (0, 2, 3, 1))


def _s5_table_kernel(pw_ref, ct_ref, bb_ref, bbt_ref, sel_ref, tile_ref, rhs_ref, woff_ref):
    t = S5_T
    c = S5_CH
    tile = tile_ref[...]
    spread = lambda x, sel: _dot_sel_rhs(x, sel, 3)
    krows, offs, cols = [], [], []
    for d in range(2):
        pw_re, pw_im = pw_ref[0, 2 * d], pw_ref[0, 2 * d + 1]
        c_re = spread(ct_ref[0, :, (2 * d) * c:(2 * d + 1) * c], tile)
        c_im = spread(ct_ref[0, :, (2 * d + 1) * c:(2 * d + 2) * c], tile)
        b_re = spread(bb_ref[0, :, (2 * d) * c:(2 * d + 1) * c], tile)
        b_im = spread(bb_ref[0, :, (2 * d + 1) * c:(2 * d + 2) * c], tile)
        bt_re = bbt_ref[0, (2 * d) * c:(2 * d + 1) * c, :]
        bt_im = bbt_ref[0, (2 * d + 1) * c:(2 * d + 2) * c, :]
        up = (spread(pw_re, sel_ref[0]), spread(pw_im, sel_ref[0]))
        down = (spread(pw_re, sel_ref[1]), spread(pw_im, sel_ref[1]))
        (q_re, q_im), state = ((up, down), (down, up))[d]
        g_re = q_re * c_re - q_im * c_im
        g_im = q_re * c_im + q_im * c_re
        krows.append(_dot_hi(bt_re, g_re) - _dot_hi(bt_im, g_im))
        lb_re, lb_im = pw_re[:, 1:2], pw_im[:, 1:2]
        r_re = q_re * lb_re - q_im * lb_im
        r_im = q_re * lb_im + q_im * lb_re
        offs.append((r_re * c_re - r_im * c_im, -(r_re * c_im + r_im * c_re)))
        s_re, s_im = state
        cols.append(((s_re * b_re - s_im * b_im).T, (s_re * b_im + s_im * b_re).T))
    k_f, k_b = krows
    for s in range(t):
        r = t - 1 - s
        f = k_f if s == 0 else jnp.concatenate([jnp.zeros((c, c * s), F32), k_f[:, :S5_TW - c * s]], axis=1)
        b = k_b if r == 0 else jnp.concatenate([k_b[:, c * r:], jnp.zeros((c, c * r), F32)], axis=1)
        rhs_ref[0, s * c:(s + 1) * c, 0:S5_TW] = (f + b).astype(BF16)
    rhs_ref[0, :, S5_TW:] = jnp.concatenate([cols[0][0], cols[1][0], cols[0][1], cols[1][1]], axis=1).astype(BF16)
    woff_ref[0] = jnp.concatenate([offs[0][0], offs[1][0], offs[0][1], offs[1][1]], axis=0).astype(BF16)


def _s5_tables(s5_A_re, s5_A_im, s5_log_dt, s5_B_re, s5_B_im, s5_C_re, s5_C_im):
    t = S5_T
    n = N_LAYERS * S5_GROUPS
    step = jnp.exp(s5_log_dt)[..., None]
    k = jnp.arange(LANES, dtype=F32)
    keep = k <= t
    kk = jnp.where(keep, k, 0.0)
    mag = jnp.exp(kk * (s5_A_re * step)[..., None])
    ang = kk * (s5_A_im * step)[..., None]
    pw_re = jnp.where(keep, mag * jnp.cos(ang), 0.0)
    pw_im = jnp.where(keep, mag * jnp.sin(ang), 0.0)
    lb_re, lb_im = pw_re[..., 1], pw_im[..., 1]
    den = s5_A_re * s5_A_re + s5_A_im * s5_A_im
    r_re = ((lb_re - 1.0) * s5_A_re + lb_im * s5_A_im) / den
    r_im = (lb_im * s5_A_re - (lb_re - 1.0) * s5_A_im) / den
    b_re, b_im = s5_B_re[:, None], s5_B_im[:, None]
    bb_re = r_re[..., None] * b_re - r_im[..., None] * b_im
    bb_im = r_re[..., None] * b_im + r_im[..., None] * b_re
    c_re = jnp.swapaxes(s5_C_re, -1, -2)
    c_im = jnp.swapaxes(s5_C_im, -1, -2)

    def pack(re, im, axis):
        return jnp.concatenate([re[:, 0], im[:, 0], re[:, 1], im[:, 1]], axis=axis)

    pw = jnp.stack([pw_re[:, 0], pw_im[:, 0], pw_re[:, 1], pw_im[:, 1]], axis=2).reshape(n, 4, S5_P, LANES)
    ct = pack(c_re, c_im, -1).reshape(n, S5_P, 4 * S5_CH)
    bb = pack(bb_re, bb_im, -1)
    bbt = jnp.swapaxes(bb, -1, -2).reshape(n, 4 * S5_CH, S5_P)
    bb = bb.reshape(n, S5_P, 4 * S5_CH)
    lam_t = jnp.concatenate([pw_re[:, 0, :, :, t], pw_re[:, 1, :, :, t], pw_im[:, 0, :, :, t],
                             pw_im[:, 1, :, :, t]], axis=-1).reshape(n, 1, S5_ST)
    jt = jnp.arange(S5_TW) // S5_CH
    lane_k = jnp.arange(LANES)[:, None]
    sel = jnp.stack([lane_k == f[None, :] for f in (jt, t - 1 - jt)]).astype(BF16)
    tile = (jnp.arange(S5_CH)[:, None] == (jnp.arange(S5_TW) % S5_CH)[None, :]).astype(BF16)
    grp = lambda shape: pl.BlockSpec((1,) + shape, lambda i: (i,) + (0,) * len(shape))
    rhs, woff = pl.pallas_call(
        _s5_table_kernel,
        grid=(n,),
        in_specs=[grp((4, S5_P, LANES)), grp((S5_P, 4 * S5_CH)), grp((S5_P, 4 * S5_CH)), grp((4 * S5_CH, S5_P)),
                  _resident(sel.shape), _resident(tile.shape)],
        out_specs=[grp((S5_TW, S5_TW + S5_ST)), grp((S5_ST, S5_TW))],
        out_shape=[jax.ShapeDtypeStruct((n, S5_TW, S5_TW + S5_ST), BF16),
                   jax.ShapeDtypeStruct((n, S5_ST, S5_TW), BF16)],
        compiler_params=_cparams("parallel"),
        name="s5_tables",
    )(pw, ct, bb, bbt, sel, tile)
    return rhs, woff, lam_t


def _unit_transpose(vs, masks, axis, unit):
    size = vs[0].shape[axis]
    cur = list(vs)
    for d in (1, 2, 4):
        nxt = list(cur)
        for a in range(S5_OCT):
            if a & d:
                continue
            b = a | d
            nxt[a] = jnp.where(masks[d], cur[a], pltpu.roll(cur[b], unit * d, axis=axis))
            nxt[b] = jnp.where(masks[d], pltpu.roll(cur[a], size - unit * d, axis=axis), cur[b])
        cur = nxt
    return cur


def _s5_kernel(u_ref, rhs_ref, woff_ref, lam_ref, h0_ref, y_ref, hfin_ref,
               dst_scr, y_scr, sre_scr, sim_scr, hfre_scr, hfim_scr, hbre_scr, hbim_scr, *, nc, rb):
    t = S5_T
    rows = dst_scr.shape[1]
    sb = rows // nc
    tiles = t // S5_OCT
    seg = lax.broadcasted_iota(jnp.int32, (rb, LANES), 1) // S5_CH
    sub = lax.broadcasted_iota(jnp.int32, (rb, LANES), 0)
    lane_masks = {d: (seg & d) == 0 for d in (1, 2, 4)}
    sub_masks = {d: (sub & d) == 0 for d in (1, 2, 4)}
    nr = rb // S5_OCT

    def tile_rows(i, r, tq):
        return pl.ds((i * rb + r) * tiles + tq, nr, stride=S5_OCT * tiles)

    def gather(i, carry):
        r0 = pl.multiple_of(i * rb, rb)
        for tq in range(tiles):
            by_chunk = [u_ref[tile_rows(i, r, tq)].reshape(rb, LANES) for r in range(S5_OCT)]
            by_time = _unit_transpose(by_chunk, sub_masks, 0, 1)
            for gl, v in enumerate(_unit_transpose(by_time, lane_masks, 1, S5_CH)):
                dst_scr[gl, pl.ds(r0, rb), tq * LANES:(tq + 1) * LANES] = v.astype(BF16)
        return carry

    lax.fori_loop(0, rows // rb, gather, 0)

    for gl in range(S5_OCT):
        z = _dot(dst_scr[gl], rhs_ref[gl])
        y_scr[gl] = z[:, 0:S5_TW]
        sre_scr[gl] = z[:, S5_TW:S5_TW + 2 * S5_P]
        sim_scr[gl] = z[:, S5_TW + 2 * S5_P:]

    fwd_lane = lax.broadcasted_iota(jnp.int32, (sb, 2 * S5_P), 1) < S5_P
    a_re = [lam_ref[gl][:, 0:2 * S5_P] for gl in range(S5_OCT)]
    a_im = [lam_ref[gl][:, 2 * S5_P:] for gl in range(S5_OCT)]

    def step(i, carry):
        fi = pl.ds(i, sb, stride=nc)
        bj = pl.ds(nc - 1 - i, sb, stride=nc)
        out = []
        for gl in range(S5_OCT):
            h_re, h_im = carry[2 * gl], carry[2 * gl + 1]
            hfre_scr[gl, fi, :] = h_re
            hfim_scr[gl, fi, :] = h_im
            hbre_scr[gl, bj, :] = h_re
            hbim_scr[gl, bj, :] = h_im
            s_re = jnp.where(fwd_lane, sre_scr[gl, fi, :], sre_scr[gl, bj, :])
            s_im = jnp.where(fwd_lane, sim_scr[gl, fi, :], sim_scr[gl, bj, :])
            out.append(a_re[gl] * h_re - a_im[gl] * h_im + s_re)
            out.append(a_re[gl] * h_im + a_im[gl] * h_re + s_im)
        return tuple(out)

    init = []
    for gl in range(S5_OCT):
        h0 = h0_ref[gl, 0]
        init += [h0[:, 0:2 * S5_P], h0[:, 2 * S5_P:]]
    fin = lax.fori_loop(0, nc, step, tuple(init))

    all_fwd = lax.broadcasted_iota(jnp.int32, (rows, 2 * S5_P), 1) < S5_P
    for gl in range(S5_OCT):
        hfin_ref[gl, 0] = jnp.concatenate([fin[2 * gl], fin[2 * gl + 1]], axis=-1)
        hin = jnp.concatenate([jnp.where(all_fwd, hfre_scr[gl], hbre_scr[gl]),
                               jnp.where(all_fwd, hfim_scr[gl], hbim_scr[gl])], axis=-1).astype(BF16)
        y_scr[gl] += _dot(hin, woff_ref[gl])

    def scatter(i, carry):
        r0 = pl.multiple_of(i * rb, rb)
        for tq in range(tiles):
            by_group = [y_scr[gl, pl.ds(r0, rb), tq * LANES:(tq + 1) * LANES] for gl in range(S5_OCT)]
            by_time = _unit_transpose(by_group, lane_masks, 1, S5_CH)
            for r, v in enumerate(_unit_transpose(by_time, sub_masks, 0, 1)):
                y_ref[tile_rows(i, r, tq)] = v.reshape(nr, S5_OCT, LANES)
        return carry

    lax.fori_loop(0, rows // rb, scatter, 0)


def _s5_mixer(proj, tables, layer, h0, *, n_seq, seq_len):
    rhs, woff, lam_t = tables
    t = S5_T
    nc = seq_len // t
    sb = max(1, min(n_seq, S5_MAX_ROWS // nc))
    assert n_seq % sb == 0
    nbs = n_seq // sb
    rows = sb * nc
    rb = min(rows, 32)
    noct = S5_GROUPS // S5_OCT
    ucol = C_U // LANES
    tab = lambda shape: pl.BlockSpec((S5_OCT,) + shape, lambda j, s: (layer * noct + j, 0, 0))
    state = pl.BlockSpec((S5_OCT, 1, sb, S5_ST), lambda j, s: (j, s, 0, 0))
    rows_scr = lambda dt: pltpu.VMEM((S5_OCT, rows, 2 * S5_P), dt)
    n_tok = n_seq * seq_len
    tok_block = (sb * seq_len // S5_OCT, S5_OCT, LANES)
    y, hfin = pl.pallas_call(
        functools.partial(_s5_kernel, nc=nc, rb=rb),
        grid=(noct, nbs),
        in_specs=[pl.BlockSpec(tok_block, lambda j, s: (s, 0, ucol + j)),
                  tab(rhs.shape[1:]), tab(woff.shape[1:]), tab(lam_t.shape[1:]), state],
        out_specs=[pl.BlockSpec(tok_block, lambda j, s: (s, 0, j)), state],
        out_shape=[jax.ShapeDtypeStruct((n_tok // S5_OCT, S5_OCT, S5_WIDTH), F32),
                   jax.ShapeDtypeStruct((S5_GROUPS, nbs, sb, S5_ST), F32)],
        scratch_shapes=[pltpu.VMEM((S5_OCT, rows, S5_TW), BF16), pltpu.VMEM((S5_OCT, rows, S5_TW), F32),
                        rows_scr(F32), rows_scr(F32), rows_scr(F32), rows_scr(F32), rows_scr(F32), rows_scr(F32)],
        compiler_params=_cparams("parallel", "parallel"),
        name="s5_mixer",
    )(proj.reshape(n_tok // S5_OCT, S5_OCT, D_INP), rhs, woff, lam_t, h0.reshape(S5_GROUPS, nbs, sb, S5_ST))
    return y.reshape(n_tok, S5_WIDTH), hfin.reshape(S5_GROUPS, n_seq, S5_ST)


def _s5_state_in(re, im):
    b = re.shape[0]
    x = jnp.stack([re, im], axis=1)
    return jnp.transpose(x, (3, 0, 1, 2, 4)).reshape(S5_GROUPS, b, S5_ST)


def _s5_state_out(h):
    n_seq = h.shape[1]
    x = h.reshape(S5_GROUPS, n_seq, 2, 2, S5_P)
    x = jnp.transpose(x, (2, 1, 3, 0, 4))
    return x[0], x[1]


def kernel(x_prompt, x_sample, c, cache_k, cache_v, state_ssd, state_s5_re, state_s5_im, c_ctx, w_ada, b_ada, norm_mix_pre, norm_mix_post, norm_ffn_pre, norm_ffn_post, w_in, w_out, ssd_conv_w, ssd_conv_b, ssd_dt_bias, ssd_A_log, ssd_D, ssd_norm, attn_sink, s5_A_re, s5_A_im, s5_log_dt, s5_B_re, s5_B_im, s5_C_re, s5_C_im, s5_D, s5_w_glu, s5_b_glu, w_ffn_in, w_ffn_out):
    nb_ctx, len_ctx, _ = x_prompt.shape
    nb_lat, len_lat, _ = x_sample.shape
    n_ctx = nb_ctx * len_ctx
    n_lat = nb_lat * len_lat
    tm = 512
    mod_rows = -(-(1 + nb_lat) // 8) * 8
    cvecs = jnp.concatenate([c_ctx[None, :], c, jnp.zeros((mod_rows - 1 - nb_lat, D_MODEL), F32)], axis=0)
    mods = _ada(cvecs, w_ada, b_ada)
    token_sets = (dict(mod_spec=_mod_spec(tm, n_ctx, 0)), dict(mod_spec=_mod_spec(tm, len_lat, 1)))
    rope_args = (dict(), dict(rope_tables=_rope_tables(len_lat), seq_len=len_lat))

    s5_tab = _s5_tables(s5_A_re, s5_A_im, s5_log_dt, s5_B_re, s5_B_im, s5_C_re, s5_C_im)
    w_in_p = jnp.concatenate(
        [w_in[:, :, O_Q:O_K], w_in[:, :, O_Z:O_XS], w_in[:, :, O_XS:O_BC], w_in[:, :, O_BC:O_DT],
         w_in[:, :, O_K:O_V], w_in[:, :, O_V:O_U], w_in[:, :, O_U:O_END], w_in[:, :, O_DT:O_Q],
         jnp.zeros((N_LAYERS, D_MODEL, DT_PAD - 2 * SSD_HEADS), F32)], axis=2).astype(BF16)
    w_out_b = w_out.astype(BF16)
    w_glu_b = s5_w_glu.astype(BF16)
    w_ffn_in_b = w_ffn_in.astype(BF16)
    w_ffn_out_b = w_ffn_out.astype(BF16)
    zeros_ssd = jnp.zeros((nb_ctx, SSD_STATE, SSD_WIDTH), F32)
    zeros_s5 = jnp.zeros((S5_GROUPS, nb_ctx, S5_ST), F32)

    xs = [x_prompt.reshape(n_ctx, D_MODEL), x_sample.reshape(n_lat, D_MODEL)]
    new_k, new_v, new_ssd, new_s5_re, new_s5_im = [], [], [], [], []
    for l in range(N_LAYERS):
        mod3 = mods[l].reshape(mod_rows, 1, 6 * D_MODEL)
        proj, qkv = zip(*[_in_proj(x, mod3, norm_mix_pre[l].reshape(1, -1), w_in_p[l], tm=tm, **ts, **ra)
                          for x, ts, ra in zip(xs, token_sets, rope_args)])

        sink = attn_sink[l]
        ck = cache_k[:, l].reshape(nb_lat, -1, KV_WIDTH).astype(BF16)
        cv = cache_v[:, l].reshape(nb_lat, -1, KV_WIDTH).astype(BF16)
        o_att = [_ctx_attention(qkv[0], sink, n_seq=nb_ctx, seq_len=len_ctx),
                 _lat_attention(qkv[1], sink, ck, cv, n_seq=nb_lat, seq_len=len_lat)]

        consts = _ssd_consts(l, ssd_conv_w, ssd_conv_b, ssd_dt_bias, ssd_A_log, ssd_D, ssd_norm)
        y_ctx, hf_ctx, hb_ctx = _ssd_mixer(proj[0], zeros_ssd, zeros_ssd, consts, n_seq=nb_ctx, seq_len=len_ctx)
        y_lat, _, _ = _ssd_mixer(proj[1], _ssd_state_in(state_ssd[:, l, 0]), _ssd_state_in(state_ssd[:, l, 1]),
                                 consts, n_seq=nb_lat, seq_len=len_lat)
        y_ssd = [y_ctx, y_lat]

        s5_ctx, s5_fin = _s5_mixer(proj[0], s5_tab, l, zeros_s5, n_seq=nb_ctx, seq_len=len_ctx)
        s5_lat, _ = _s5_mixer(proj[1], s5_tab, l, _s5_state_in(state_s5_re[:, l], state_s5_im[:, l]),
                              n_seq=nb_lat, seq_len=len_lat)
        y_s5 = [s5_ctx, s5_lat]

        wo = w_out_b[l]
        xs = [_out_proj(x, mod3, norm_mix_post[l].reshape(1, -1), y_ssd[i], o_att[i], y_s5[i], proj[i],
                        s5_D[l].reshape(1, -1), w_glu_b[l], s5_b_glu[l].reshape(1, -1),
                        wo[:SSD_WIDTH], wo[SSD_WIDTH:SSD_WIDTH + ATT_WIDTH], wo[SSD_WIDTH + ATT_WIDTH:],
                        tm=tm, **ts)
              for i, (x, ts) in enumerate(zip(xs, token_sets))]
        xs = [_ffn(x, mod3, norm_ffn_pre[l].reshape(1, -1), norm_ffn_post[l].reshape(1, -1),
                   w_ffn_in_b[l], w_ffn_out_b[l], tm=tm, tf=512, tn=512, **ts)
              for x, ts in zip(xs, token_sets)]

        new_k.append(proj[0][:, C_K:C_K + KV_WIDTH].reshape(nb_ctx, len_ctx, ATT_KV, HEAD_DIM))
        new_v.append(proj[0][:, C_V:C_V + KV_WIDTH].reshape(nb_ctx, len_ctx, ATT_KV, HEAD_DIM))
        new_ssd.append(jnp.stack([_ssd_state_out(hf_ctx), _ssd_state_out(hb_ctx)], axis=1))
        re, im = _s5_state_out(s5_fin)
        new_s5_re.append(re)
        new_s5_im.append(im)

    y_prompt = xs[0].reshape(nb_ctx, len_ctx, D_MODEL)
    y_sample = xs[1].reshape(nb_lat, len_lat, D_MODEL)
    return (y_prompt, y_sample, jnp.stack(new_k, axis=1), jnp.stack(new_v, axis=1), jnp.stack(new_ssd, axis=1),
            jnp.stack(new_s5_re, axis=1), jnp.stack(new_s5_im, axis=1))
```

```python
import functools
import math

import jax
import jax.numpy as jnp
from jax import lax
from jax.experimental import pallas as pl
from jax.experimental.pallas import tpu as pltpu

F32 = jnp.float32
BF16 = jnp.bfloat16
HIGHEST = lax.Precision.HIGHEST

D_MODEL = 2048
N_LAYERS = 2
D_FF = 5632
GRID_W = 64
EPS = 1e-6
LANES = 128

SSD_HEADS = 12
SSD_HEAD_DIM = 64
SSD_WIDTH = SSD_HEADS * SSD_HEAD_DIM
SSD_GROUPS = 2
SSD_STATE = 64
SSD_BC = 2 * SSD_GROUPS * SSD_STATE
SSD_CHUNK = 128
SSD_HALO = 8
ATT_HEADS = 12
ATT_KV = 4
ATT_GROUP = ATT_HEADS // ATT_KV
HEAD_DIM = 64
ATT_WIDTH = ATT_HEADS * HEAD_DIM
KV_WIDTH = ATT_KV * HEAD_DIM
ATT_BLOCK = 128
QKV_WIDTH = ATT_WIDTH + 2 * KV_WIDTH
LOG2E = math.log2(math.e)
Q_SCALE = HEAD_DIM ** -0.5 * LOG2E
ROPE_PER_AXIS = HEAD_DIM // 4
ROPE_BASE = 10000.0
S5_GROUPS = 32
S5_CH = 16
S5_WIDTH = S5_GROUPS * S5_CH
S5_P = 64
S5_T = 32
S5_TW = S5_T * S5_CH
S5_ST = 4 * S5_P
S5_OCT = LANES // S5_CH
S5_MAX_ROWS = 256
MIX_WIDTH = SSD_WIDTH + ATT_WIDTH + S5_WIDTH

O_Z = 0
O_XS = SSD_WIDTH
O_BC = O_XS + SSD_WIDTH
O_DT = O_BC + SSD_BC
O_Q = O_DT + 2 * SSD_HEADS
O_K = O_Q + ATT_WIDTH
O_V = O_K + KV_WIDTH
O_U = O_V + KV_WIDTH
O_END = O_U + S5_WIDTH
C_Q = 0
C_Z = 768
C_XS = 1536
C_BC = 2304
C_K = 2560
C_V = 2816
C_U = 3072
C_DT = 3584
DT_PAD = 128
D_INP = C_DT + DT_PAD

NEG = -1e30
VMEM_LIMIT = 56 * 1024 * 1024


def _cparams(*sem):
    return pltpu.CompilerParams(dimension_semantics=sem, vmem_limit_bytes=VMEM_LIMIT)


def _resident(shape):
    nd = len(shape)
    return pl.BlockSpec(shape, lambda *_: (0,) * nd, pipeline_mode=pl.Buffered(1))


def _layer_resident(shape, layer, block=0):
    rest = (0,) * (len(shape) - 1)
    return pl.BlockSpec((None,) + shape, lambda *_: (layer, block) + rest, pipeline_mode=pl.Buffered(1))


def _rms(x, g):
    return x * lax.rsqrt(jnp.mean(x * x, axis=-1, keepdims=True) + EPS) * g


def _silu(x):
    return x * jax.nn.sigmoid(x)


def _dot(a, b):
    return jnp.dot(a, b, preferred_element_type=F32)


def _dot_hi(a, b):
    return jnp.dot(a, b, preferred_element_type=F32, precision=HIGHEST)


def _dot_nt(a, b):
    return lax.dot_general(a, b, (((1,), (1,)), ((), ())), preferred_element_type=F32)


def _dot_tn(a, b):
    return lax.dot_general(a, b, (((0,), (0,)), ((), ())), preferred_element_type=F32)


def _split_bf16(x, parts):
    out = []
    for _ in range(parts):
        p = x.astype(BF16)
        out.append(p)
        x = x - p.astype(F32)
    return out


def _dot_sel_rhs(x, sel, parts):
    acc = None
    for p in _split_bf16(x, parts):
        t = _dot(p, sel)
        acc = t if acc is None else acc + t
    return acc


def _dot_sel_lhs(sel, x, parts):
    acc = None
    for p in _split_bf16(x, parts):
        t = _dot(sel, p)
        acc = t if acc is None else acc + t
    return acc


def _mod_spec(tm, rows_per_mod, first_row):
    return pl.BlockSpec((1, 1, 6 * D_MODEL), lambda i, *_: (first_row + (i * tm) // rows_per_mod, 0, 0))


def _ada_kernel(c_ref, w_ref, b_ref, o_ref):
    c = c_ref[...]
    o_ref[0] = _dot(_silu(c).astype(BF16), w_ref[0].astype(BF16)) + b_ref[0]


def _ada(cvecs, w_ada, b_ada):
    tn = 1024
    rows = cvecs.shape[0]
    return pl.pallas_call(
        _ada_kernel,
        grid=(N_LAYERS, 6 * D_MODEL // tn),
        in_specs=[pl.BlockSpec((rows, D_MODEL), lambda l, j: (0, 0)),
                  pl.BlockSpec((1, D_MODEL, tn), lambda l, j: (l, 0, j)),
                  pl.BlockSpec((1, 1, tn), lambda l, j: (l, 0, j))],
        out_specs=pl.BlockSpec((1, rows, tn), lambda l, j: (l, 0, j)),
        out_shape=jax.ShapeDtypeStruct((N_LAYERS, rows, 6 * D_MODEL), F32),
        compiler_params=_cparams("parallel", "parallel"),
        name="ada_mod",
    )(cvecs, w_ada, b_ada.reshape(N_LAYERS, 1, 6 * D_MODEL))


def _rope(x, cos, sin_signed, first_half):
    outs = []
    for j in range(x.shape[1] // LANES):
        xj = x[:, j * LANES:(j + 1) * LANES]
        partner = jnp.where(first_half, pltpu.roll(xj, LANES - HEAD_DIM // 2, axis=1),
                            pltpu.roll(xj, HEAD_DIM // 2, axis=1))
        outs.append(xj * cos + partner * sin_signed)
    return jnp.concatenate(outs, axis=-1)


def _in_proj_kernel(*refs, rope):
    if rope:
        x_ref, mod_ref, g_ref, w_ref, cos_ref, sin_ref, o_ref, qkv_ref = refs
    else:
        x_ref, mod_ref, g_ref, w_ref, o_ref, qkv_ref = refs
    mod = mod_ref[0]
    sh = mod[:, 0:D_MODEL]
    sc = mod[:, D_MODEL:2 * D_MODEL]
    h = _rms(x_ref[...], g_ref[...]) * (1.0 + sc) + sh
    o_ref[...] = _dot(h.astype(BF16), w_ref[...])
    q = o_ref[:, C_Q:C_Q + ATT_WIDTH]
    k = o_ref[:, C_K:C_K + KV_WIDTH]
    if rope:
        lane = lax.broadcasted_iota(jnp.int32, (x_ref.shape[0], LANES), 1)
        first_half = (lane % HEAD_DIM) < (HEAD_DIM // 2)
        q = _rope(q, cos_ref[...], sin_ref[...], first_half)
        k = _rope(k, cos_ref[...], sin_ref[...], first_half)
    qkv_ref[:, 0:ATT_WIDTH] = (q * Q_SCALE).astype(BF16)
    qkv_ref[:, ATT_WIDTH:ATT_WIDTH + KV_WIDTH] = k.astype(BF16)
    qkv_ref[:, ATT_WIDTH + KV_WIDTH:] = o_ref[:, C_V:C_V + KV_WIDTH].astype(BF16)


def _in_proj(x, mod3, gamma, w, *, layer, tm, mod_spec, rope_tables=None, seq_len=None):
    n = x.shape[0]
    rope = rope_tables is not None
    tables, table_specs = (), []
    if rope:
        per_seq = seq_len // tm
        tables = tuple(rope_tables)
        table_specs = [pl.BlockSpec((tm, LANES), lambda i: (i % per_seq, 0))] * 2
    return pl.pallas_call(
        functools.partial(_in_proj_kernel, rope=rope),
        grid=(n // tm,),
        in_specs=[pl.BlockSpec((tm, D_MODEL), lambda i: (i, 0)),
                  mod_spec,
                  _resident((1, D_MODEL)),
                  _layer_resident((D_MODEL, D_INP), layer)] + table_specs,
        out_specs=[pl.BlockSpec((tm, D_INP), lambda i: (i, 0)), pl.BlockSpec((tm, QKV_WIDTH), lambda i: (i, 0))],
        out_shape=[jax.ShapeDtypeStruct((n, D_INP), F32), jax.ShapeDtypeStruct((n, QKV_WIDTH), BF16)],
        compiler_params=_cparams("parallel"),
        name="in_proj",
    )(x, mod3, gamma, w, *tables)


def _gelu_tanh(x):
    return 0.5 * x * (1.0 + jnp.tanh(math.sqrt(2.0 / math.pi) * (x + 0.044715 * (x * x * x))))


def _out_proj_kernel(x_ref, mod_ref, g_ref, yssd_ref, oatt_ref, ys5_ref, u_ref, d_ref, wglu_ref, bglu_ref,
                     w1_ref, w2_ref, w3_ref, o_ref):
    y5 = ys5_ref[...] + d_ref[...] * u_ref[...]
    g = _gelu_tanh(y5)
    s5 = g * jax.nn.sigmoid(_dot(g.astype(BF16), wglu_ref[...]) + bglu_ref[...])
    mix = (_dot(yssd_ref[...], w1_ref[...]) + _dot(oatt_ref[...], w2_ref[...])
           + _dot(s5.astype(BF16), w3_ref[...]))
    gate = mod_ref[0][:, 2 * D_MODEL:3 * D_MODEL]
    o_ref[...] = x_ref[...] + gate * _rms(mix, g_ref[...])


def _out_proj(x, mod3, gamma, y_ssd, o_att, y_s5, proj, s5_d, w_glu, b_glu, w_out, *, layer, tm, mod_spec):
    n = x.shape[0]
    row = lambda i: (i, 0)
    return pl.pallas_call(
        _out_proj_kernel,
        grid=(n // tm,),
        in_specs=[pl.BlockSpec((tm, D_MODEL), row),
                  mod_spec,
                  _resident((1, D_MODEL)),
                  pl.BlockSpec((tm, SSD_WIDTH), row),
                  pl.BlockSpec((tm, ATT_WIDTH), row),
                  pl.BlockSpec((tm, S5_WIDTH), row),
                  pl.BlockSpec((tm, S5_WIDTH), lambda i: (i, C_U // S5_WIDTH)),
                  _resident((1, S5_WIDTH)),
                  _layer_resident((S5_WIDTH, S5_WIDTH), layer),
                  _resident((1, S5_WIDTH)),
                  _layer_resident((SSD_WIDTH, D_MODEL), layer, 0),
                  _layer_resident((ATT_WIDTH, D_MODEL), layer, SSD_WIDTH // ATT_WIDTH),
                  _layer_resident((S5_WIDTH, D_MODEL), layer, (SSD_WIDTH + ATT_WIDTH) // S5_WIDTH)],
        out_specs=pl.BlockSpec((tm, D_MODEL), row),
        out_shape=jax.ShapeDtypeStruct((n, D_MODEL), F32),
        compiler_params=_cparams("parallel"),
        name="out_proj",
    )(x, mod3, gamma, y_ssd, o_att, y_s5, proj, s5_d, w_glu, b_glu, w_out, w_out, w_out)


def _ffn_kernel(x_ref, mod_ref, gpre_ref, gpost_ref, wg_ref, wu_ref, wo_ref, o_ref, h_scr, act_scr, *, nf):
    j = pl.program_id(1)
    last = pl.num_programs(1) - 1
    tf = wg_ref.shape[1]
    tn = wo_ref.shape[1]

    def gated(h):
        return (_silu(_dot(h, wg_ref[...])) * _dot(h, wu_ref[...])).astype(BF16)

    @pl.when(j == 0)
    def _():
        mod = mod_ref[0]
        sh = mod[:, 3 * D_MODEL:4 * D_MODEL]
        sc = mod[:, 4 * D_MODEL:5 * D_MODEL]
        h = (_rms(x_ref[...], gpre_ref[...]) * (1.0 + sc) + sh).astype(BF16)
        h_scr[...] = h
        act_scr[:, 0:tf] = gated(h)

    @pl.when(jnp.logical_and(j > 0, j < nf))
    def _():
        act_scr[:, pl.ds(pl.multiple_of(j * tf, tf), tf)] = gated(h_scr[...])

    @pl.when(jnp.logical_and(j >= nf, j < last))
    def _():
        o_ref[:, pl.ds(pl.multiple_of((j - nf) * tn, tn), tn)] = _dot(act_scr[...], wo_ref[...])

    @pl.when(j == last)
    def _():
        f = jnp.concatenate([o_ref[:, 0:D_MODEL - tn], _dot(act_scr[...], wo_ref[...])], axis=-1)
        gate = mod_ref[0][:, 5 * D_MODEL:6 * D_MODEL]
        o_ref[...] = x_ref[...] + gate * _rms(f, gpost_ref[...])


def _ffn(x, mod3, g_pre, g_post, w_in, w_out, *, layer, tm, tf, tn, mod_spec):
    n = x.shape[0]
    nf = D_FF // tf
    nn = D_MODEL // tn
    return pl.pallas_call(
        functools.partial(_ffn_kernel, nf=nf),
        grid=(n // tm, nf + nn),
        in_specs=[pl.BlockSpec((tm, D_MODEL), lambda i, j: (i, 0)),
                  mod_spec,
                  _resident((1, D_MODEL)),
                  _resident((1, D_MODEL)),
                  pl.BlockSpec((None, D_MODEL, tf), lambda i, j: (layer, 0, jnp.minimum(j, nf - 1))),
                  pl.BlockSpec((None, D_MODEL, tf), lambda i, j: (layer, 0, jnp.minimum(j, nf - 1) + nf)),
                  pl.BlockSpec((None, D_FF, tn), lambda i, j: (layer, 0, jnp.maximum(j - nf, 0)))],
        out_specs=pl.BlockSpec((tm, D_MODEL), lambda i, j: (i, 0)),
        out_shape=jax.ShapeDtypeStruct((n, D_MODEL), F32),
        scratch_shapes=[pltpu.VMEM((tm, D_MODEL), BF16), pltpu.VMEM((tm, D_FF), BF16)],
        compiler_params=_cparams("parallel", "arbitrary"),
        name="ffn",
    )(x, mod3, g_pre, g_post, w_in, w_in, w_out)


def _attend(q, parts, sink_ref):
    rows = q.shape[0]
    srow = lax.broadcasted_iota(jnp.int32, (ATT_GROUP * rows, 1), 0)
    outs = [None] * ATT_HEADS
    for kv in range(ATT_KV):
        heads = range(kv * ATT_GROUP, (kv + 1) * ATT_GROUP)
        q3 = jnp.concatenate([q[:, h * HEAD_DIM:(h + 1) * HEAD_DIM] for h in heads], axis=0)
        sink = jnp.full((ATT_GROUP * rows, 1), sink_ref[heads[-1]] * LOG2E, F32)
        for i in range(ATT_GROUP - 2, -1, -1):
            sink = jnp.where(srow < (i + 1) * rows, sink_ref[heads[i]] * LOG2E, sink)
        lo = kv * HEAD_DIM
        scores = []
        m = sink
        for k, _, mask in parts:
            s = _dot_nt(q3, k[:, lo:lo + HEAD_DIM])
            if mask is not None:
                s = jnp.where(mask, s, NEG)
            scores.append(s)
            m = jnp.maximum(m, jnp.max(s, axis=-1, keepdims=True))
        vlo = (lo // LANES) * LANES
        v_first = lo == vlo
        acc = None
        for s, (_, v, _) in zip(scores, parts):
            v128 = v[:, vlo:vlo + LANES]
            lane = lax.broadcasted_iota(jnp.int32, v128.shape, 1)
            keep = (lane < HEAD_DIM) if v_first else (lane >= HEAD_DIM)
            t = _dot(jnp.exp2(s - m).astype(BF16), jnp.where(keep, v128, jnp.ones_like(v128)))
            acc = t if acc is None else acc + t
        ocol, dcol = (0, HEAD_DIM) if v_first else (HEAD_DIM, 0)
        o3 = acc[:, ocol:ocol + HEAD_DIM] / (acc[:, dcol:dcol + 1] + jnp.exp2(sink - m))
        for i, h in enumerate(heads):
            outs[h] = o3[i * rows:(i + 1) * rows]
    return jnp.concatenate(outs, axis=-1)


def _ctx_attn_kernel(sink_ref, q_ref, k_ref, v_ref, o_ref):
    o_ref[...] = _attend(q_ref[...], [(k_ref[...], v_ref[...], None)], sink_ref).astype(o_ref.dtype)


def _ctx_attention(qkv, sink, *, n_seq, seq_len):
    kcol = ATT_WIDTH // KV_WIDTH
    blk = ATT_BLOCK
    nb = seq_len // blk
    return pl.pallas_call(
        _ctx_attn_kernel,
        grid=(n_seq, nb),
        in_specs=[pl.BlockSpec(memory_space=pltpu.SMEM),
                  pl.BlockSpec((blk, ATT_WIDTH), lambda b, n: (b * nb + n, 0)),
                  pl.BlockSpec((seq_len, KV_WIDTH), lambda b, n: (b, kcol)),
                  pl.BlockSpec((seq_len, KV_WIDTH), lambda b, n: (b, kcol + 1))],
        out_specs=pl.BlockSpec((blk, ATT_WIDTH), lambda b, n: (b * nb + n, 0)),
        out_shape=jax.ShapeDtypeStruct((n_seq * seq_len, ATT_WIDTH), BF16),
        compiler_params=_cparams("parallel", "parallel"),
        name="ctx_attention",
    )(sink, qkv, qkv, qkv)


def _lat_attn_kernel(sink_ref, q_ref, kp_ref, kc_ref, kn_ref, vp_ref, vc_ref, vn_ref, ck_ref, cv_ref, o_ref):
    n = pl.program_id(1)
    nb = pl.num_programs(1)
    blk = ATT_BLOCK
    row = lax.broadcasted_iota(jnp.int32, (ATT_GROUP * blk, blk), 0) % blk
    col = lax.broadcasted_iota(jnp.int32, (ATT_GROUP * blk, blk), 1)
    mask_prev = jnp.logical_and(col >= row, n > 0)
    mask_next = jnp.logical_and(col <= row, n < nb - 1)
    parts = [(ck_ref[0], cv_ref[0], None),
             (kp_ref[...], vp_ref[...], mask_prev),
             (kc_ref[...], vc_ref[...], None),
             (kn_ref[...], vn_ref[...], mask_next)]
    o_ref[...] = _attend(q_ref[...], parts, sink_ref).astype(o_ref.dtype)


def _lat_attention(qkv, sink, ck, cv, *, n_seq, seq_len):
    blk = ATT_BLOCK
    nb = seq_len // blk

    def cur(b, n):
        return b * nb + n

    def prev(b, n):
        return b * nb + jnp.maximum(n - 1, 0)

    def nxt(b, n):
        return b * nb + jnp.minimum(n + 1, nb - 1)

    kcol = ATT_WIDTH // KV_WIDTH
    vcol = kcol + 1
    return pl.pallas_call(
        _lat_attn_kernel,
        grid=(n_seq, nb),
        in_specs=[pl.BlockSpec(memory_space=pltpu.SMEM),
                  pl.BlockSpec((blk, ATT_WIDTH), lambda b, n: (cur(b, n), 0)),
                  pl.BlockSpec((blk, KV_WIDTH), lambda b, n: (prev(b, n), kcol)),
                  pl.BlockSpec((blk, KV_WIDTH), lambda b, n: (cur(b, n), kcol)),
                  pl.BlockSpec((blk, KV_WIDTH), lambda b, n: (nxt(b, n), kcol)),
                  pl.BlockSpec((blk, KV_WIDTH), lambda b, n: (prev(b, n), vcol)),
                  pl.BlockSpec((blk, KV_WIDTH), lambda b, n: (cur(b, n), vcol)),
                  pl.BlockSpec((blk, KV_WIDTH), lambda b, n: (nxt(b, n), vcol)),
                  pl.BlockSpec((1,) + ck.shape[1:], lambda b, n: (b, 0, 0)),
                  pl.BlockSpec((1,) + cv.shape[1:], lambda b, n: (b, 0, 0))],
        out_specs=pl.BlockSpec((blk, ATT_WIDTH), lambda b, n: (cur(b, n), 0)),
        out_shape=jax.ShapeDtypeStruct((n_seq * seq_len, ATT_WIDTH), BF16),
        compiler_params=_cparams("parallel", "parallel"),
        name="lat_attention",
    )(sink, qkv, qkv, qkv, qkv, qkv, qkv, qkv, ck, cv)


def _rope_tables(seq_len):
    rows = seq_len // GRID_W
    row = jnp.repeat(jnp.arange(rows, dtype=F32), GRID_W)
    col = jnp.tile(jnp.arange(GRID_W, dtype=F32), rows)
    inv = ROPE_BASE ** (-jnp.arange(ROPE_PER_AXIS, dtype=F32) / ROPE_PER_AXIS)
    ang = jnp.concatenate([row[:, None] * inv, col[:, None] * inv], axis=-1)
    cos, sin = jnp.cos(ang), jnp.sin(ang)
    cos128 = jnp.tile(cos, (1, 4))
    sin128 = jnp.tile(jnp.concatenate([-sin, sin], axis=-1), (1, 2))
    return cos128, sin128


def _conv_silu(scr, prev_ref, cur_ref, next_ref, w_ref, b_ref, is_first, is_last):
    q = SSD_CHUNK
    h = SSD_HALO
    scr[0:h, :] = jnp.where(is_first, 0.0, prev_ref[...])
    scr[h:h + q, :] = cur_ref[...]
    scr[h + q:h + q + h, :] = jnp.where(is_last, 0.0, next_ref[...])
    y = (scr[h - 1:h - 1 + q, :] * w_ref[0:1, :] + scr[h:h + q, :] * w_ref[1:2, :]
         + scr[h + 1:h + 1 + q, :] * w_ref[2:3, :] + b_ref[...])
    return _silu(y)


def _softplus(x):
    return jnp.maximum(x, 0.0) + jnp.log1p(jnp.exp(-jnp.abs(x)))


def _ssd_bwd_kernel(xsp_ref, xsc_ref, xsn_ref, bcp_ref, bcc_ref, bcn_ref, dt_ref, h0_ref,
                    cwx_ref, cbx_ref, cwb_ref, cbb_ref, dtb_ref, alog_ref, tri_ref, exp_ref,
                    hstart_ref, hfin_ref, xs_scr, bc_scr, h_scr):
    i = pl.program_id(1)
    nc = pl.num_programs(1)

    @pl.when(i == 0)
    def _():
        h_scr[...] = h0_ref[0]

    hb = h_scr[...]
    hstart_ref[0, 0] = hb
    is_first = i == nc - 1
    is_last = i == 0
    xs = _conv_silu(xs_scr, xsp_ref, xsc_ref, xsn_ref, cwx_ref, cbx_ref, is_first, is_last)
    bc = _conv_silu(bc_scr, bcp_ref, bcc_ref, bcn_ref, cwb_ref, cbb_ref, is_first, is_last)
    bm = bc[:, 0:SSD_GROUPS * SSD_STATE].astype(BF16)
    dtv = _softplus(dt_ref[...] + dtb_ref[...])
    dta = dtv * (-LOG2E * jnp.exp(alog_ref[...]))
    rev = _dot_sel_lhs(tri_ref[1], dta, 3)
    sel_b = exp_ref[1]
    dt_w = _dot_sel_rhs(dtv, sel_b, 2)
    to_end = _dot_sel_rhs(jnp.exp2(rev[0:1, :] - rev), sel_b, 2)
    decay = _dot_sel_rhs(jnp.exp2(rev[0:SSD_HALO, :]), sel_b, 2)[0:1, :]
    xw = (xs * dt_w * to_end).astype(BF16)
    half = SSD_WIDTH // SSD_GROUPS
    upd = jnp.concatenate(
        [_dot_tn(bm[:, g * SSD_STATE:(g + 1) * SSD_STATE], xw[:, g * half:(g + 1) * half])
         for g in range(SSD_GROUPS)], axis=-1)
    h_new = hb * decay + upd
    h_scr[...] = h_new

    @pl.when(i == nc - 1)
    def _():
        hfin_ref[0] = h_new.T


def _ssd_fwd_kernel(z_ref, xsp_ref, xsc_ref, xsn_ref, bcp_ref, bcc_ref, bcn_ref, dt_ref, hb_ref, h0_ref,
                    cwx_ref, cbx_ref, cwb_ref, cbb_ref, dtb_ref, alog_ref, tri_ref, exp_ref, dvec_ref, nw_ref,
                    y_ref, hfin_ref, xs_scr, bc_scr, h_scr):
    c = pl.program_id(1)
    nc = pl.num_programs(1)
    q = SSD_CHUNK
    nh = SSD_HEADS

    @pl.when(c == 0)
    def _():
        h_scr[...] = h0_ref[0]

    hf = h_scr[...]
    xs = _conv_silu(xs_scr, xsp_ref, xsc_ref, xsn_ref, cwx_ref, cbx_ref, c == 0, c == nc - 1)
    bc = _conv_silu(bc_scr, bcp_ref, bcc_ref, bcn_ref, cwb_ref, cbb_ref, c == 0, c == nc - 1)
    gs = SSD_GROUPS * SSD_STATE
    bm = bc[:, 0:gs].astype(BF16)
    cm = bc[:, gs:2 * gs].astype(BF16)
    xs_bf = xs.astype(BF16)

    dtv = _softplus(dt_ref[...] + dtb_ref[...])
    dta = dtv * (-LOG2E * jnp.exp(alog_ref[...]))
    lower, upper = tri_ref[0], tri_ref[1]
    cum = _dot_sel_lhs(lower, dta, 3)
    rev = _dot_sel_lhs(upper, dta, 3)
    dt_t = dtv.T
    dta_t = dta.T
    cum_t = _dot_sel_rhs(dta_t, upper, 3)
    rev_t = _dot_sel_rhs(dta_t, lower, 3)

    sel_f, sel_b = exp_ref[0], exp_ref[1]
    dtf_w = _dot_sel_rhs(dtv, sel_f, 2)
    dtb_w = _dot_sel_rhs(dtv, sel_b, 2)
    ecum_w = _dot_sel_rhs(jnp.exp2(cum), sel_f, 2)
    erev_w = _dot_sel_rhs(jnp.exp2(rev), sel_b, 2)
    toend_w = _dot_sel_rhs(jnp.exp2(cum[q - 1:q, :] - cum), sel_f, 2)

    row = lax.broadcasted_iota(jnp.int32, (q, q), 0)
    col = lax.broadcasted_iota(jnp.int32, (q, q), 1)
    causal = row >= col
    anti = row <= col
    gmat = [_dot_nt(cm[:, g * SSD_STATE:(g + 1) * SSD_STATE], bm[:, g * SSD_STATE:(g + 1) * SSD_STATE])
            for g in range(SSD_GROUPS)]
    per_group = nh // SSD_GROUPS
    ys = []
    for h in range(nh):
        seg_f = cum[:, h:h + 1] - cum_t[h:h + 1, :]
        dec_f = jnp.exp2(jnp.where(causal, seg_f, NEG)) * dt_t[h:h + 1, :]
        seg_b = rev[:, nh + h:nh + h + 1] - rev_t[nh + h:nh + h + 1, :]
        dec_b = jnp.exp2(jnp.where(anti, seg_b, NEG)) * dt_t[nh + h:nh + h + 1, :]
        m = (gmat[h // per_group] * (dec_f + dec_b)).astype(BF16)
        ys.append(_dot(m, xs_bf[:, h * SSD_HEAD_DIM:(h + 1) * SSD_HEAD_DIM]))
    y = jnp.concatenate(ys, axis=-1)

    half = SSD_WIDTH // SSD_GROUPS
    hf_bf = hf.astype(BF16)
    hb_bf = hb_ref[0, 0].astype(BF16)
    off_f = jnp.concatenate([_dot(cm[:, g * SSD_STATE:(g + 1) * SSD_STATE], hf_bf[:, g * half:(g + 1) * half])
                             for g in range(SSD_GROUPS)], axis=-1)
    off_b = jnp.concatenate([_dot(cm[:, g * SSD_STATE:(g + 1) * SSD_STATE], hb_bf[:, g * half:(g + 1) * half])
                             for g in range(SSD_GROUPS)], axis=-1)
    y = y + off_f * ecum_w + off_b * erev_w + dvec_ref[...] * xs
    y = y * _silu(z_ref[...])
    y_ref[...] = _rms(y, nw_ref[...]).astype(y_ref.dtype)

    xw = (xs * dtf_w * toend_w).astype(BF16)
    upd = jnp.concatenate(
        [_dot_tn(bm[:, g * SSD_STATE:(g + 1) * SSD_STATE], xw[:, g * half:(g + 1) * half])
         for g in range(SSD_GROUPS)], axis=-1)
    h_new = hf * ecum_w[q - 1:q, :] + upd
    h_scr[...] = h_new

    @pl.when(c == nc - 1)
    def _():
        hfin_ref[0] = h_new.T


def _ssd_mixer(proj, h0_f, h0_b, consts, *, n_seq, seq_len):
    q = SSD_CHUNK
    nc = seq_len // q
    per8 = q // SSD_HALO
    n_rows8 = proj.shape[0] // SSD_HALO
    (cwx, cbx, cwb, cbb, dt_bias, a_log, tri, expand, dvec, nw) = consts

    def specs(chunk_of):
        def cur(b, i):
            return b * nc + chunk_of(i)

        def prev8(b, i):
            return jnp.maximum(cur(b, i) * per8 - 1, 0)

        def next8(b, i):
            return jnp.minimum((cur(b, i) + 1) * per8, n_rows8 - 1)

        xcol, bcol, dcol = C_XS // SSD_WIDTH, C_BC // SSD_BC, C_DT // DT_PAD
        return [pl.BlockSpec((SSD_HALO, SSD_WIDTH), lambda b, i: (prev8(b, i), xcol)),
                pl.BlockSpec((q, SSD_WIDTH), lambda b, i: (cur(b, i), xcol)),
                pl.BlockSpec((SSD_HALO, SSD_WIDTH), lambda b, i: (next8(b, i), xcol)),
                pl.BlockSpec((SSD_HALO, SSD_BC), lambda b, i: (prev8(b, i), bcol)),
                pl.BlockSpec((q, SSD_BC), lambda b, i: (cur(b, i), bcol)),
                pl.BlockSpec((SSD_HALO, SSD_BC), lambda b, i: (next8(b, i), bcol)),
                pl.BlockSpec((q, DT_PAD), lambda b, i: (cur(b, i), dcol))], cur

    state_spec = pl.BlockSpec((1, SSD_STATE, SSD_WIDTH), lambda b, i: (b, 0, 0))
    final_spec = pl.BlockSpec((1, SSD_WIDTH, SSD_STATE), lambda b, i: (b, 0, 0))
    const_specs = [_resident(cwx.shape), _resident(cbx.shape), _resident(cwb.shape), _resident(cbb.shape),
                   _resident(dt_bias.shape), _resident(a_log.shape), _resident(tri.shape),
                   _resident(expand.shape)]
    scratch = [pltpu.VMEM((q + 2 * SSD_HALO, SSD_WIDTH), F32), pltpu.VMEM((q + 2 * SSD_HALO, SSD_BC), F32),
               pltpu.VMEM((SSD_STATE, SSD_WIDTH), F32)]

    data_specs, _ = specs(lambda i: nc - 1 - i)
    hb_start, hb_fin = pl.pallas_call(
        _ssd_bwd_kernel,
        grid=(n_seq, nc),
        in_specs=data_specs + [state_spec] + const_specs,
        out_specs=[pl.BlockSpec((1, 1, SSD_STATE, SSD_WIDTH), lambda b, i: (b, nc - 1 - i, 0, 0)), final_spec],
        out_shape=[jax.ShapeDtypeStruct((n_seq, nc, SSD_STATE, SSD_WIDTH), F32),
                   jax.ShapeDtypeStruct((n_seq, SSD_WIDTH, SSD_STATE), F32)],
        scratch_shapes=scratch,
        compiler_params=_cparams("parallel", "arbitrary"),
        name="ssd_backward_states",
    )(proj, proj, proj, proj, proj, proj, proj, h0_b, cwx, cbx, cwb, cbb, dt_bias, a_log, tri, expand)

    data_specs, cur = specs(lambda i: i)
    y, hf_fin = pl.pallas_call(
        _ssd_fwd_kernel,
        grid=(n_seq, nc),
        in_specs=([pl.BlockSpec((q, SSD_WIDTH), lambda b, i: (cur(b, i), C_Z // SSD_WIDTH))] + data_specs
                  + [pl.BlockSpec((1, 1, SSD_STATE, SSD_WIDTH), lambda b, i: (b, i, 0, 0)), state_spec]
                  + const_specs + [_resident(dvec.shape), _resident(nw.shape)]),
        out_specs=[pl.BlockSpec((q, SSD_WIDTH), lambda b, i: (b * nc + i, 0)), final_spec],
        out_shape=[jax.ShapeDtypeStruct((n_seq * seq_len, SSD_WIDTH), BF16),
                   jax.ShapeDtypeStruct((n_seq, SSD_WIDTH, SSD_STATE), F32)],
        scratch_shapes=scratch,
        compiler_params=_cparams("parallel", "arbitrary"),
        name="ssd_forward",
    )(proj, proj, proj, proj, proj, proj, proj, proj, hb_start, h0_f,
      cwx, cbx, cwb, cbb, dt_bias, a_log, tri, expand, dvec, nw)
    return y, hf_fin, hb_fin


def _ssd_consts(l, ssd_conv_w, ssd_conv_b, ssd_dt_bias, ssd_A_log, ssd_D, ssd_norm):
    q = SSD_CHUNK
    cw, cb = ssd_conv_w[l], ssd_conv_b[l]
    pad = DT_PAD - 2 * SSD_HEADS
    dt_bias = jnp.pad(ssd_dt_bias[l].reshape(1, -1), ((0, 0), (0, pad)))
    a_log = jnp.pad(ssd_A_log[l].reshape(1, -1), ((0, 0), (0, pad)))
    r = jnp.arange(q)
    tri = jnp.stack([r[None, :] <= r[:, None], r[None, :] >= r[:, None]]).astype(BF16)
    head_of_col = jnp.arange(SSD_WIDTH) // SSD_HEAD_DIM
    j = jnp.arange(DT_PAD)
    expand = jnp.stack([j[:, None] == head_of_col[None, :],
                        j[:, None] == head_of_col[None, :] + SSD_HEADS]).astype(BF16)
    dvec = jnp.repeat(ssd_D[l], SSD_HEAD_DIM).reshape(1, SSD_WIDTH)
    return (cw[:, :SSD_WIDTH], cb[:SSD_WIDTH].reshape(1, -1), cw[:, SSD_WIDTH:], cb[SSD_WIDTH:].reshape(1, -1),
            dt_bias, a_log, tri, expand, dvec, ssd_norm[l].reshape(1, -1))


def _ssd_state_in(s):
    b = s.shape[0]
    return jnp.transpose(s, (0, 3, 1, 2)).reshape(b, SSD_STATE, SSD_WIDTH)


def _ssd_state_out(s):
    return s.reshape(s.shape[0], SSD_HEADS, SSD_HEAD_DIM, SSD_STATE)


def _s5_table_kernel(pw_ref, ct_ref, bb_ref, bbt_ref, sel_ref, tile_ref, rhs_ref, woff_ref):
    t = S5_T
    c = S5_CH
    tile = tile_ref[...]
    spread = lambda x, sel: _dot_sel_rhs(x, sel, 3)
    krows, offs, cols = [], [], []
    for d in range(2):
        pw_re, pw_im = pw_ref[0, 2 * d], pw_ref[0, 2 * d + 1]
        c_re = spread(ct_ref[0, :, (2 * d) * c:(2 * d + 1) * c], tile)
        c_im = spread(ct_ref[0, :, (2 * d + 1) * c:(2 * d + 2) * c], tile)
        b_re = spread(bb_ref[0, :, (2 * d) * c:(2 * d + 1) * c], tile)
        b_im = spread(bb_ref[0, :, (2 * d + 1) * c:(2 * d + 2) * c], tile)
        bt_re = bbt_ref[0, (2 * d) * c:(2 * d + 1) * c, :]
        bt_im = bbt_ref[0, (2 * d + 1) * c:(2 * d + 2) * c, :]
        up = (spread(pw_re, sel_ref[0]), spread(pw_im, sel_ref[0]))
        down = (spread(pw_re, sel_ref[1]), spread(pw_im, sel_ref[1]))
        (q_re, q_im), state = ((up, down), (down, up))[d]
        g_re = q_re * c_re - q_im * c_im
        g_im = q_re * c_im + q_im * c_re
        krows.append(_dot_hi(bt_re, g_re) - _dot_hi(bt_im, g_im))
        lb_re, lb_im = pw_re[:, 1:2], pw_im[:, 1:2]
        r_re = q_re * lb_re - q_im * lb_im
        r_im = q_re * lb_im + q_im * lb_re
        offs.append((r_re * c_re - r_im * c_im, -(r_re * c_im + r_im * c_re)))
        s_re, s_im = state
        cols.append(((s_re * b_re - s_im * b_im).T, (s_re * b_im + s_im * b_re).T))
    k_f, k_b = krows
    for s in range(t):
        r = t - 1 - s
        f = k_f if s == 0 else jnp.concatenate([jnp.zeros((c, c * s), F32), k_f[:, :S5_TW - c * s]], axis=1)
        b = k_b if r == 0 else jnp.concatenate([k_b[:, c * r:], jnp.zeros((c, c * r), F32)], axis=1)
        rhs_ref[0, s * c:(s + 1) * c, 0:S5_TW] = (f + b).astype(BF16)
    rhs_ref[0, :, S5_TW:] = jnp.concatenate([cols[0][0], cols[1][0], cols[0][1], cols[1][1]], axis=1).astype(BF16)
    woff_ref[0] = jnp.concatenate([offs[0][0], offs[1][0], offs[0][1], offs[1][1]], axis=0).astype(BF16)


def _s5_tables(s5_A_re, s5_A_im, s5_log_dt, s5_B_re, s5_B_im, s5_C_re, s5_C_im):
    t = S5_T
    n = N_LAYERS * S5_GROUPS
    step = jnp.exp(s5_log_dt)[..., None]
    k = jnp.arange(LANES, dtype=F32)
    keep = k <= t
    kk = jnp.where(keep, k, 0.0)
    mag = jnp.exp(kk * (s5_A_re * step)[..., None])
    ang = kk * (s5_A_im * step)[..., None]
    pw_re = jnp.where(keep, mag * jnp.cos(ang), 0.0)
    pw_im = jnp.where(keep, mag * jnp.sin(ang), 0.0)
    lb_re, lb_im = pw_re[..., 1], pw_im[..., 1]
    den = s5_A_re * s5_A_re + s5_A_im * s5_A_im
    r_re = ((lb_re - 1.0) * s5_A_re + lb_im * s5_A_im) / den
    r_im = (lb_im * s5_A_re - (lb_re - 1.0) * s5_A_im) / den
    b_re, b_im = s5_B_re[:, None], s5_B_im[:, None]
    bb_re = r_re[..., None] * b_re - r_im[..., None] * b_im
    bb_im = r_re[..., None] * b_im + r_im[..., None] * b_re
    c_re = jnp.swapaxes(s5_C_re, -1, -2)
    c_im = jnp.swapaxes(s5_C_im, -1, -2)

    def pack(re, im, axis):
        return jnp.concatenate([re[:, 0], im[:, 0], re[:, 1], im[:, 1]], axis=axis)

    pw = jnp.stack([pw_re[:, 0], pw_im[:, 0], pw_re[:, 1], pw_im[:, 1]], axis=2).reshape(n, 4, S5_P, LANES)
    ct = pack(c_re, c_im, -1).reshape(n, S5_P, 4 * S5_CH)
    bb = pack(bb_re, bb_im, -1)
    bbt = jnp.swapaxes(bb, -1, -2).reshape(n, 4 * S5_CH, S5_P)
    bb = bb.reshape(n, S5_P, 4 * S5_CH)
    lam_t = jnp.concatenate([pw_re[:, 0, :, :, t], pw_re[:, 1, :, :, t], pw_im[:, 0, :, :, t],
                             pw_im[:, 1, :, :, t]], axis=-1).reshape(n, 1, S5_ST)
    jt = jnp.arange(S5_TW) // S5_CH
    lane_k = jnp.arange(LANES)[:, None]
    sel = jnp.stack([lane_k == f[None, :] for f in (jt, t - 1 - jt)]).astype(BF16)
    tile = (jnp.arange(S5_CH)[:, None] == (jnp.arange(S5_TW) % S5_CH)[None, :]).astype(BF16)
    grp = lambda shape: pl.BlockSpec((1,) + shape, lambda i: (i,) + (0,) * len(shape))
    rhs, woff = pl.pallas_call(
        _s5_table_kernel,
        grid=(n,),
        in_specs=[grp((4, S5_P, LANES)), grp((S5_P, 4 * S5_CH)), grp((S5_P, 4 * S5_CH)), grp((4 * S5_CH, S5_P)),
                  _resident(sel.shape), _resident(tile.shape)],
        out_specs=[grp((S5_TW, S5_TW + S5_ST)), grp((S5_ST, S5_TW))],
        out_shape=[jax.ShapeDtypeStruct((n, S5_TW, S5_TW + S5_ST), BF16),
                   jax.ShapeDtypeStruct((n, S5_ST, S5_TW), BF16)],
        compiler_params=_cparams("parallel"),
        name="s5_tables",
    )(pw, ct, bb, bbt, sel, tile)
    return rhs, woff, lam_t


def _unit_transpose(vs, masks, axis, unit):
    size = vs[0].shape[axis]
    cur = list(vs)
    for d in (1, 2, 4):
        nxt = list(cur)
        for a in range(S5_OCT):
            if a & d:
                continue
            b = a | d
            nxt[a] = jnp.where(masks[d], cur[a], pltpu.roll(cur[b], unit * d, axis=axis))
            nxt[b] = jnp.where(masks[d], pltpu.roll(cur[a], size - unit * d, axis=axis), cur[b])
        cur = nxt
    return cur


def _s5_kernel(u_ref, rhs_ref, woff_ref, lam_ref, h0_ref, y_ref, hfin_ref,
               dst_scr, y_scr, sre_scr, sim_scr, hfre_scr, hfim_scr, hbre_scr, hbim_scr, *, nc, rb):
    t = S5_T
    rows = dst_scr.shape[1]
    sb = rows // nc
    tiles = t // S5_OCT
    nr = rb // S5_OCT
    tile_shape = (nr, S5_OCT, LANES)
    seg = lax.broadcasted_iota(jnp.int32, tile_shape, 2) // S5_CH
    sub = lax.broadcasted_iota(jnp.int32, tile_shape, 1)
    lane_masks = {d: (seg & d) == 0 for d in (1, 2, 4)}
    sub_masks = {d: (sub & d) == 0 for d in (1, 2, 4)}

    def tile_rows(i, r, tq):
        return pl.ds((i * rb + r) * tiles + tq, nr, stride=S5_OCT * tiles)

    def gather(i, carry):
        r0 = pl.multiple_of(i * rb, rb)
        for tq in range(tiles):
            by_chunk = [u_ref[tile_rows(i, r, tq)] for r in range(S5_OCT)]
            by_time = _unit_transpose(by_chunk, sub_masks, 1, 1)
            for gl, v in enumerate(_unit_transpose(by_time, lane_masks, 2, S5_CH)):
                dst_scr[gl, pl.ds(r0, rb), tq * LANES:(tq + 1) * LANES] = v.reshape(rb, LANES).astype(BF16)
        return carry

    lax.fori_loop(0, rows // rb, gather, 0)

    for gl in range(S5_OCT):
        z = _dot(dst_scr[gl], rhs_ref[gl])
        y_scr[gl] = z[:, 0:S5_TW]
        sre_scr[gl] = z[:, S5_TW:S5_TW + 2 * S5_P]
        sim_scr[gl] = z[:, S5_TW + 2 * S5_P:]

    fwd_lane = lax.broadcasted_iota(jnp.int32, (sb, 2 * S5_P), 1) < S5_P
    a_re = [lam_ref[gl][:, 0:2 * S5_P] for gl in range(S5_OCT)]
    a_im = [lam_ref[gl][:, 2 * S5_P:] for gl in range(S5_OCT)]

    def step(i, carry):
        fi = pl.ds(i, sb, stride=nc)
        bj = pl.ds(nc - 1 - i, sb, stride=nc)
        out = []
        for gl in range(S5_OCT):
            h_re, h_im = carry[2 * gl], carry[2 * gl + 1]
            hfre_scr[gl, fi, :] = h_re
            hfim_scr[gl, fi, :] = h_im
            hbre_scr[gl, bj, :] = h_re
            hbim_scr[gl, bj, :] = h_im
            s_re = jnp.where(fwd_lane, sre_scr[gl, fi, :], sre_scr[gl, bj, :])
            s_im = jnp.where(fwd_lane, sim_scr[gl, fi, :], sim_scr[gl, bj, :])
            out.append(a_re[gl] * h_re - a_im[gl] * h_im + s_re)
            out.append(a_re[gl] * h_im + a_im[gl] * h_re + s_im)
        return tuple(out)

    init = []
    for gl in range(S5_OCT):
        h0 = h0_ref[gl, 0]
        init += [h0[:, 0:2 * S5_P], h0[:, 2 * S5_P:]]
    fin = lax.fori_loop(0, nc, step, tuple(init))

    all_fwd = lax.broadcasted_iota(jnp.int32, (rows, 2 * S5_P), 1) < S5_P
    for gl in range(S5_OCT):
        hfin_ref[gl, 0] = jnp.concatenate([fin[2 * gl], fin[2 * gl + 1]], axis=-1)
        hin = jnp.concatenate([jnp.where(all_fwd, hfre_scr[gl], hbre_scr[gl]),
                               jnp.where(all_fwd, hfim_scr[gl], hbim_scr[gl])], axis=-1).astype(BF16)
        y_scr[gl] += _dot(hin, woff_ref[gl])

    def scatter(i, carry):
        r0 = pl.multiple_of(i * rb, rb)
        for tq in range(tiles):
            by_group = [y_scr[gl, pl.ds(r0, rb), tq * LANES:(tq + 1) * LANES].reshape(tile_shape)
                        for gl in range(S5_OCT)]
            by_time = _unit_transpose(by_group, lane_masks, 2, S5_CH)
            for r, v in enumerate(_unit_transpose(by_time, sub_masks, 1, 1)):
                y_ref[tile_rows(i, r, tq)] = v
        return carry

    lax.fori_loop(0, rows // rb, scatter, 0)


def _s5_mixer(proj, tables, layer, h0, *, n_seq, seq_len):
    rhs, woff, lam_t = tables
    t = S5_T
    nc = seq_len // t
    sb = max(1, min(n_seq, S5_MAX_ROWS // nc))
    assert n_seq % sb == 0
    nbs = n_seq // sb
    rows = sb * nc
    rb = min(rows, 32)
    noct = S5_GROUPS // S5_OCT
    ucol = C_U // LANES
    tab = lambda shape: pl.BlockSpec((S5_OCT,) + shape, lambda j, s: (layer * noct + j, 0, 0))
    state = pl.BlockSpec((S5_OCT, 1, sb, S5_ST), lambda j, s: (j, s, 0, 0))
    rows_scr = lambda dt: pltpu.VMEM((S5_OCT, rows, 2 * S5_P), dt)
    n_tok = n_seq * seq_len
    tok_block = (sb * seq_len // S5_OCT, S5_OCT, LANES)
    y, hfin = pl.pallas_call(
        functools.partial(_s5_kernel, nc=nc, rb=rb),
        grid=(noct, nbs),
        in_specs=[pl.BlockSpec(tok_block, lambda j, s: (s, 0, ucol + j)),
                  tab(rhs.shape[1:]), tab(woff.shape[1:]), tab(lam_t.shape[1:]), state],
        out_specs=[pl.BlockSpec(tok_block, lambda j, s: (s, 0, j)), state],
        out_shape=[jax.ShapeDtypeStruct((n_tok // S5_OCT, S5_OCT, S5_WIDTH), F32),
                   jax.ShapeDtypeStruct((S5_GROUPS, nbs, sb, S5_ST), F32)],
        scratch_shapes=[pltpu.VMEM((S5_OCT, rows, S5_TW), BF16), pltpu.VMEM((S5_OCT, rows, S5_TW), F32),
                        rows_scr(F32), rows_scr(F32), rows_scr(F32), rows_scr(F32), rows_scr(F32), rows_scr(F32)],
        compiler_params=_cparams("parallel", "parallel"),
        name="s5_mixer",
    )(proj.reshape(n_tok // S5_OCT, S5_OCT, D_INP), rhs, woff, lam_t, h0.reshape(S5_GROUPS, nbs, sb, S5_ST))
    return y.reshape(n_tok, S5_WIDTH), hfin.reshape(S5_GROUPS, n_seq, S5_ST)


def _s5_state_in(re, im):
    b = re.shape[0]
    x = jnp.stack([re, im], axis=1)
    return jnp.transpose(x, (3, 0, 1, 2, 4)).reshape(S5_GROUPS, b, S5_ST)


def _s5_state_out(h):
    n_seq = h.shape[1]
    x = h.reshape(S5_GROUPS, n_seq, 2, 2, S5_P)
    x = jnp.transpose(x, (2, 1, 3, 0, 4))
    return x[0], x[1]


def kernel(x_prompt, x_sample, c, cache_k, cache_v, state_ssd, state_s5_re, state_s5_im, c_ctx, w_ada, b_ada, norm_mix_pre, norm_mix_post, norm_ffn_pre, norm_ffn_post, w_in, w_out, ssd_conv_w, ssd_conv_b, ssd_dt_bias, ssd_A_log, ssd_D, ssd_norm, attn_sink, s5_A_re, s5_A_im, s5_log_dt, s5_B_re, s5_B_im, s5_C_re, s5_C_im, s5_D, s5_w_glu, s5_b_glu, w_ffn_in, w_ffn_out):
    nb_ctx, len_ctx, _ = x_prompt.shape
    nb_lat, len_lat, _ = x_sample.shape
    n_ctx = nb_ctx * len_ctx
    n_lat = nb_lat * len_lat
    tm = 512
    mod_rows = -(-(1 + nb_lat) // 8) * 8
    cvecs = jnp.concatenate([c_ctx[None, :], c, jnp.zeros((mod_rows - 1 - nb_lat, D_MODEL), F32)], axis=0)
    mods = _ada(cvecs, w_ada, b_ada)
    token_sets = (dict(mod_spec=_mod_spec(tm, n_ctx, 0)), dict(mod_spec=_mod_spec(tm, len_lat, 1)))
    rope_args = (dict(), dict(rope_tables=_rope_tables(len_lat), seq_len=len_lat))

    s5_tab = _s5_tables(s5_A_re, s5_A_im, s5_log_dt, s5_B_re, s5_B_im, s5_C_re, s5_C_im)
    w_in_p = jnp.concatenate(
        [w_in[:, :, O_Q:O_K], w_in[:, :, O_Z:O_XS], w_in[:, :, O_XS:O_BC], w_in[:, :, O_BC:O_DT],
         w_in[:, :, O_K:O_V], w_in[:, :, O_V:O_U], w_in[:, :, O_U:O_END], w_in[:, :, O_DT:O_Q],
         jnp.zeros((N_LAYERS, D_MODEL, DT_PAD - 2 * SSD_HEADS), F32)], axis=2).astype(BF16)
    w_out_b = w_out.astype(BF16)
    w_glu_b = s5_w_glu.astype(BF16)
    w_ffn_in_b = w_ffn_in.astype(BF16)
    w_ffn_out_b = w_ffn_out.astype(BF16)
    zeros_ssd = jnp.zeros((nb_ctx, SSD_STATE, SSD_WIDTH), F32)
    zeros_s5 = jnp.zeros((S5_GROUPS, nb_ctx, S5_ST), F32)

    xs = [x_prompt.reshape(n_ctx, D_MODEL), x_sample.reshape(n_lat, D_MODEL)]
    new_k, new_v, new_ssd, new_s5_re, new_s5_im = [], [], [], [], []
    for l in range(N_LAYERS):
        mod3 = mods[l].reshape(mod_rows, 1, 6 * D_MODEL)
        proj, qkv = zip(*[_in_proj(x, mod3, norm_mix_pre[l].reshape(1, -1), w_in_p, layer=l, tm=tm, **ts, **ra)
                          for x, ts, ra in zip(xs, token_sets, rope_args)])

        sink = attn_sink[l]
        ck = cache_k[:, l].reshape(nb_lat, -1, KV_WIDTH).astype(BF16)
        cv = cache_v[:, l].reshape(nb_lat, -1, KV_WIDTH).astype(BF16)
        o_att = [_ctx_attention(qkv[0], sink, n_seq=nb_ctx, seq_len=len_ctx),
                 _lat_attention(qkv[1], sink, ck, cv, n_seq=nb_lat, seq_len=len_lat)]

        consts = _ssd_consts(l, ssd_conv_w, ssd_conv_b, ssd_dt_bias, ssd_A_log, ssd_D, ssd_norm)
        y_ctx, hf_ctx, hb_ctx = _ssd_mixer(proj[0], zeros_ssd, zeros_ssd, consts, n_seq=nb_ctx, seq_len=len_ctx)
        y_lat, _, _ = _ssd_mixer(proj[1], _ssd_state_in(state_ssd[:, l, 0]), _ssd_state_in(state_ssd[:, l, 1]),
                                 consts, n_seq=nb_lat, seq_len=len_lat)
        y_ssd = [y_ctx, y_lat]

        s5_ctx, s5_fin = _s5_mixer(proj[0], s5_tab, l, zeros_s5, n_seq=nb_ctx, seq_len=len_ctx)
        s5_lat, _ = _s5_mixer(proj[1], s5_tab, l, _s5_state_in(state_s5_re[:, l], state_s5_im[:, l]),
                              n_seq=nb_lat, seq_len=len_lat)
        y_s5 = [s5_ctx, s5_lat]

        xs = [_out_proj(x, mod3, norm_mix_post[l].reshape(1, -1), y_ssd[i], o_att[i], y_s5[i], proj[i],
                        s5_D[l].reshape(1, -1), w_glu_b, s5_b_glu[l].reshape(1, -1), w_out_b,
                        layer=l, tm=tm, **ts)
              for i, (x, ts) in enumerate(zip(xs, token_sets))]
        xs = [_ffn(x, mod3, norm_ffn_pre[l].reshape(1, -1), norm_ffn_post[l].reshape(1, -1),
                   w_ffn_in_b, w_ffn_out_b, layer=l, tm=tm, tf=512, tn=512, **ts)
              for x, ts in zip(xs, token_sets)]

        new_k.append(proj[0][:, C_K:C_K + KV_WIDTH].reshape(nb_ctx, len_ctx, ATT_KV, HEAD_DIM))
        new_v.append(proj[0][:, C_V:C_V + KV_WIDTH].reshape(nb_ctx, len_ctx, ATT_KV, HEAD_DIM))
        new_ssd.append(jnp.stack([_ssd_state_out(hf_ctx), _ssd_state_out(hb_ctx)], axis=1))
        re, im = _s5_state_out(s5_fin)
        new_s5_re.append(re)
        new_s5_im.append(im)

    y_prompt = xs[0].reshape(nb_ctx, len_ctx, D_MODEL)
    y_sample = xs[1].reshape(nb_lat, len_lat, D_MODEL)
    return (y_prompt, y_sample, jnp.stack(new_k, axis=1), jnp.stack(new_v, axis=1), jnp.stack(new_ssd, axis=1),
            jnp.stack(new_s5_re, axis=1), jnp.stack(new_s5_im, axis=1))
```

```python
import functools
import math

import jax
import jax.numpy as jnp
import numpy as np
from jax import lax
from jax.experimental import pallas as pl
from jax.experimental.pallas import tpu as pltpu

F32 = jnp.float32
BF16 = jnp.bfloat16
HIGHEST = lax.Precision.HIGHEST

D_MODEL = 2048
N_LAYERS = 2
D_FF = 5632
GRID_W = 64
EPS = 1e-6
LANES = 128

SSD_HEADS = 12
SSD_HEAD_DIM = 64
SSD_WIDTH = SSD_HEADS * SSD_HEAD_DIM
SSD_GROUPS = 2
SSD_STATE = 64
SSD_BC = 2 * SSD_GROUPS * SSD_STATE
SSD_CHUNK = 128
SSD_HALO = 8
ATT_HEADS = 12
ATT_KV = 4
ATT_GROUP = ATT_HEADS // ATT_KV
HEAD_DIM = 64
ATT_WIDTH = ATT_HEADS * HEAD_DIM
KV_WIDTH = ATT_KV * HEAD_DIM
ATT_BLOCK = 128
QKV_WIDTH = ATT_WIDTH + 2 * KV_WIDTH
LOG2E = math.log2(math.e)
Q_SCALE = HEAD_DIM ** -0.5 * LOG2E
ROPE_PER_AXIS = HEAD_DIM // 4
ROPE_BASE = 10000.0
S5_GROUPS = 32
S5_CH = 16
S5_WIDTH = S5_GROUPS * S5_CH
S5_P = 64
S5_T = 32
S5_TW = S5_T * S5_CH
S5_ST = 4 * S5_P
S5_OCT = LANES // S5_CH
S5_MAX_ROWS = 256
MIX_WIDTH = SSD_WIDTH + ATT_WIDTH + S5_WIDTH

O_Z = 0
O_XS = SSD_WIDTH
O_BC = O_XS + SSD_WIDTH
O_DT = O_BC + SSD_BC
O_Q = O_DT + 2 * SSD_HEADS
O_K = O_Q + ATT_WIDTH
O_V = O_K + KV_WIDTH
O_U = O_V + KV_WIDTH
O_END = O_U + S5_WIDTH
C_Q = 0
C_Z = 768
C_XS = 1536
C_BC = 2304
C_K = 2560
C_V = 2816
C_U = 3072
C_DT = 3584
DT_PAD = 128
D_INP = C_DT + DT_PAD

FFN_UP_ROWS = 1024
FFN_UP_COLS = 512

NEG = -1e30
VMEM_LIMIT = 56 * 1024 * 1024


def _cparams(*sem):
    return pltpu.CompilerParams(dimension_semantics=sem, vmem_limit_bytes=VMEM_LIMIT)


def _resident(shape):
    nd = len(shape)
    return pl.BlockSpec(shape, lambda *_: (0,) * nd, pipeline_mode=pl.Buffered(1))


def _layer_resident(shape, layer, block=0):
    rest = (0,) * (len(shape) - 1)
    return pl.BlockSpec((None,) + shape, lambda *_: (layer, block) + rest, pipeline_mode=pl.Buffered(1))


def _rms(x, g):
    return x * lax.rsqrt(jnp.mean(x * x, axis=-1, keepdims=True) + EPS) * g


def _silu(x):
    return x * jax.nn.sigmoid(x)


def _dot(a, b):
    return jnp.dot(a, b, preferred_element_type=F32)


def _dot_hi(a, b):
    return jnp.dot(a, b, preferred_element_type=F32, precision=HIGHEST)


def _dot_nt(a, b):
    return lax.dot_general(a, b, (((1,), (1,)), ((), ())), preferred_element_type=F32)


def _dot_tn(a, b):
    return lax.dot_general(a, b, (((0,), (0,)), ((), ())), preferred_element_type=F32)


def _split_bf16(x, parts):
    out = []
    for _ in range(parts):
        p = x.astype(BF16)
        out.append(p)
        x = x - p.astype(F32)
    return out


def _dot_sel_rhs(x, sel, parts):
    acc = None
    for p in _split_bf16(x, parts):
        t = _dot(p, sel)
        acc = t if acc is None else acc + t
    return acc


def _dot_sel_lhs(sel, x, parts):
    acc = None
    for p in _split_bf16(x, parts):
        t = _dot(sel, p)
        acc = t if acc is None else acc + t
    return acc


def _mod_spec(tm, rows_per_mod, first_row):
    return pl.BlockSpec((1, 1, 6 * D_MODEL), lambda i, *_: (first_row + (i * tm) // rows_per_mod, 0, 0))


def _ada_kernel(c_ref, w_ref, b_ref, o_ref):
    c = c_ref[...]
    o_ref[0] = _dot(_silu(c).astype(BF16), w_ref[0].astype(BF16)) + b_ref[0]


def _ada(cvecs, w_ada, b_ada):
    tn = 1024
    rows = cvecs.shape[0]
    return pl.pallas_call(
        _ada_kernel,
        grid=(N_LAYERS, 6 * D_MODEL // tn),
        in_specs=[pl.BlockSpec((rows, D_MODEL), lambda l, j: (0, 0)),
                  pl.BlockSpec((1, D_MODEL, tn), lambda l, j: (l, 0, j)),
                  pl.BlockSpec((1, 1, tn), lambda l, j: (l, 0, j))],
        out_specs=pl.BlockSpec((1, rows, tn), lambda l, j: (l, 0, j)),
        out_shape=jax.ShapeDtypeStruct((N_LAYERS, rows, 6 * D_MODEL), F32),
        compiler_params=_cparams("parallel", "parallel"),
        name="ada_mod",
    )(cvecs, w_ada, b_ada.reshape(N_LAYERS, 1, 6 * D_MODEL))


def _rope(x, cos, sin_signed, first_half):
    outs = []
    for j in range(x.shape[1] // LANES):
        xj = x[:, j * LANES:(j + 1) * LANES]
        partner = jnp.where(first_half, pltpu.roll(xj, LANES - HEAD_DIM // 2, axis=1),
                            pltpu.roll(xj, HEAD_DIM // 2, axis=1))
        outs.append(xj * cos + partner * sin_signed)
    return jnp.concatenate(outs, axis=-1)


def _in_proj_kernel(*refs, rope):
    if rope:
        x_ref, mod_ref, g_ref, w_ref, cos_ref, sin_ref, o_ref, qkv_ref = refs
    else:
        x_ref, mod_ref, g_ref, w_ref, o_ref, qkv_ref = refs
    mod = mod_ref[0]
    sh = mod[:, 0:D_MODEL]
    sc = mod[:, D_MODEL:2 * D_MODEL]
    h = _rms(x_ref[...], g_ref[...]) * (1.0 + sc) + sh
    o_ref[...] = _dot(h.astype(BF16), w_ref[...])
    q = o_ref[:, C_Q:C_Q + ATT_WIDTH]
    k = o_ref[:, C_K:C_K + KV_WIDTH]
    if rope:
        lane = lax.broadcasted_iota(jnp.int32, (x_ref.shape[0], LANES), 1)
        first_half = (lane % HEAD_DIM) < (HEAD_DIM // 2)
        q = _rope(q, cos_ref[...], sin_ref[...], first_half)
        k = _rope(k, cos_ref[...], sin_ref[...], first_half)
    qkv_ref[:, 0:ATT_WIDTH] = (q * Q_SCALE).astype(BF16)
    qkv_ref[:, ATT_WIDTH:ATT_WIDTH + KV_WIDTH] = k.astype(BF16)
    qkv_ref[:, ATT_WIDTH + KV_WIDTH:] = o_ref[:, C_V:C_V + KV_WIDTH].astype(BF16)


def _in_proj(x, mod3, gamma, w, *, layer, tm, mod_spec, rope_tables=None, seq_len=None):
    n = x.shape[0]
    rope = rope_tables is not None
    tables, table_specs = (), []
    if rope:
        per_seq = seq_len // tm
        tables = tuple(rope_tables)
        table_specs = [pl.BlockSpec((tm, LANES), lambda i: (i % per_seq, 0))] * 2
    return pl.pallas_call(
        functools.partial(_in_proj_kernel, rope=rope),
        grid=(n // tm,),
        in_specs=[pl.BlockSpec((tm, D_MODEL), lambda i: (i, 0)),
                  mod_spec,
                  _resident((1, D_MODEL)),
                  _layer_resident((D_MODEL, D_INP), layer)] + table_specs,
        out_specs=[pl.BlockSpec((tm, D_INP), lambda i: (i, 0)), pl.BlockSpec((tm, QKV_WIDTH), lambda i: (i, 0))],
        out_shape=[jax.ShapeDtypeStruct((n, D_INP), F32), jax.ShapeDtypeStruct((n, QKV_WIDTH), BF16)],
        compiler_params=_cparams("parallel"),
        name="in_proj",
    )(x, mod3, gamma, w, *tables)


def _gelu_tanh(x):
    return 0.5 * x * (1.0 + jnp.tanh(math.sqrt(2.0 / math.pi) * (x + 0.044715 * (x * x * x))))


def _out_proj_kernel(x_ref, mod_ref, g_ref, yssd_ref, oatt_ref, ys5_ref, u_ref, d_ref, wglu_ref, bglu_ref,
                     w1_ref, w2_ref, w3_ref, o_ref):
    y5 = ys5_ref[...] + d_ref[...] * u_ref[...]
    g = _gelu_tanh(y5)
    s5 = g * jax.nn.sigmoid(_dot(g.astype(BF16), wglu_ref[...]) + bglu_ref[...])
    mix = (_dot(yssd_ref[...], w1_ref[...]) + _dot(oatt_ref[...], w2_ref[...])
           + _dot(s5.astype(BF16), w3_ref[...]))
    gate = mod_ref[0][:, 2 * D_MODEL:3 * D_MODEL]
    o_ref[...] = x_ref[...] + gate * _rms(mix, g_ref[...])


def _out_proj(x, mod3, gamma, y_ssd, o_att, y_s5, proj, s5_d, w_glu, b_glu, w_out, *, layer, tm, mod_spec):
    n = x.shape[0]
    row = lambda i: (i, 0)
    return pl.pallas_call(
        _out_proj_kernel,
        grid=(n // tm,),
        in_specs=[pl.BlockSpec((tm, D_MODEL), row),
                  mod_spec,
                  _resident((1, D_MODEL)),
                  pl.BlockSpec((tm, SSD_WIDTH), row),
                  pl.BlockSpec((tm, ATT_WIDTH), row),
                  pl.BlockSpec((tm, S5_WIDTH), row),
                  pl.BlockSpec((tm, S5_WIDTH), lambda i: (i, C_U // S5_WIDTH)),
                  _resident((1, S5_WIDTH)),
                  _layer_resident((S5_WIDTH, S5_WIDTH), layer),
                  _resident((1, S5_WIDTH)),
                  _layer_resident((SSD_WIDTH, D_MODEL), layer, 0),
                  _layer_resident((ATT_WIDTH, D_MODEL), layer, SSD_WIDTH // ATT_WIDTH),
                  _layer_resident((S5_WIDTH, D_MODEL), layer, (SSD_WIDTH + ATT_WIDTH) // S5_WIDTH)],
        out_specs=pl.BlockSpec((tm, D_MODEL), row),
        out_shape=jax.ShapeDtypeStruct((n, D_MODEL), F32),
        compiler_params=_cparams("parallel"),
        name="out_proj",
    )(x, mod3, gamma, y_ssd, o_att, y_s5, proj, s5_d, w_glu, b_glu, w_out, w_out, w_out)


def _ffn_up_kernel(x_ref, mod_ref, gpre_ref, wg_ref, wu_ref, act_ref, h_scr):
    j = pl.program_id(1)

    def gated(h):
        return (_silu(_dot(h, wg_ref[...])) * _dot(h, wu_ref[...])).astype(BF16)

    @pl.when(j == 0)
    def _():
        mod = mod_ref[0]
        sh = mod[:, 3 * D_MODEL:4 * D_MODEL]
        sc = mod[:, 4 * D_MODEL:5 * D_MODEL]
        h = (_rms(x_ref[...], gpre_ref[...]) * (1.0 + sc) + sh).astype(BF16)
        h_scr[...] = h
        act_ref[...] = gated(h)

    @pl.when(j > 0)
    def _():
        act_ref[...] = gated(h_scr[...])


def _ffn_down_kernel(x_ref, mod_ref, gpost_ref, act_ref, wo_ref, o_ref):
    gate = mod_ref[0][:, 5 * D_MODEL:6 * D_MODEL]
    o_ref[...] = x_ref[...] + gate * _rms(_dot(act_ref[...], wo_ref[...]), gpost_ref[...])


def _ffn(x, mod3, g_pre, g_post, w_in, w_out, *, layer, tm_up, tf, tm_down, mod_spec_up, mod_spec_down):
    n = x.shape[0]
    nf = D_FF // tf
    act = pl.pallas_call(
        _ffn_up_kernel,
        grid=(n // tm_up, nf),
        in_specs=[pl.BlockSpec((tm_up, D_MODEL), lambda i, j: (i, 0)),
                  mod_spec_up,
                  _resident((1, D_MODEL)),
                  pl.BlockSpec((None, D_MODEL, tf), lambda i, j: (layer, 0, j)),
                  pl.BlockSpec((None, D_MODEL, tf), lambda i, j: (layer, 0, j + nf))],
        out_specs=pl.BlockSpec((tm_up, tf), lambda i, j: (i, j)),
        out_shape=jax.ShapeDtypeStruct((n, D_FF), BF16),
        scratch_shapes=[pltpu.VMEM((tm_up, D_MODEL), BF16)],
        compiler_params=_cparams("parallel", "arbitrary"),
        name="ffn_up",
    )(x, mod3, g_pre, w_in, w_in)
    return pl.pallas_call(
        _ffn_down_kernel,
        grid=(n // tm_down,),
        in_specs=[pl.BlockSpec((tm_down, D_MODEL), lambda i: (i, 0)),
                  mod_spec_down,
                  _resident((1, D_MODEL)),
                  pl.BlockSpec((tm_down, D_FF), lambda i: (i, 0)),
                  _layer_resident((D_FF, D_MODEL), layer)],
        out_specs=pl.BlockSpec((tm_down, D_MODEL), lambda i: (i, 0)),
        out_shape=jax.ShapeDtypeStruct((n, D_MODEL), F32),
        compiler_params=_cparams("parallel"),
        name="ffn_down",
    )(x, mod3, g_post, act, w_out)


def _attend(q, parts, sink_ref):
    rows = q.shape[0]
    srow = lax.broadcasted_iota(jnp.int32, (ATT_GROUP * rows, 1), 0)
    outs = [None] * ATT_HEADS
    for kv in range(ATT_KV):
        heads = range(kv * ATT_GROUP, (kv + 1) * ATT_GROUP)
        q3 = jnp.concatenate([q[:, h * HEAD_DIM:(h + 1) * HEAD_DIM] for h in heads], axis=0)
        sink = jnp.full((ATT_GROUP * rows, 1), sink_ref[heads[-1]] * LOG2E, F32)
        for i in range(ATT_GROUP - 2, -1, -1):
            sink = jnp.where(srow < (i + 1) * rows, sink_ref[heads[i]] * LOG2E, sink)
        lo = kv * HEAD_DIM
        scores = []
        m = sink
        for k, _, mask in parts:
            s = _dot_nt(q3, k[:, lo:lo + HEAD_DIM])
            if mask is not None:
                s = jnp.where(mask, s, NEG)
            scores.append(s)
            m = jnp.maximum(m, jnp.max(s, axis=-1, keepdims=True))
        vlo = (lo // LANES) * LANES
        v_first = lo == vlo
        acc = None
        for s, (_, v, _) in zip(scores, parts):
            v128 = v[:, vlo:vlo + LANES]
            lane = lax.broadcasted_iota(jnp.int32, v128.shape, 1)
            keep = (lane < HEAD_DIM) if v_first else (lane >= HEAD_DIM)
            t = _dot(jnp.exp2(s - m).astype(BF16), jnp.where(keep, v128, jnp.ones_like(v128)))
            acc = t if acc is None else acc + t
        ocol, dcol = (0, HEAD_DIM) if v_first else (HEAD_DIM, 0)
        o3 = acc[:, ocol:ocol + HEAD_DIM] / (acc[:, dcol:dcol + 1] + jnp.exp2(sink - m))
        for i, h in enumerate(heads):
            outs[h] = o3[i * rows:(i + 1) * rows]
    return jnp.concatenate(outs, axis=-1)


def _ctx_attn_kernel(sink_ref, q_ref, k_ref, v_ref, o_ref):
    o_ref[...] = _attend(q_ref[...], [(k_ref[...], v_ref[...], None)], sink_ref).astype(o_ref.dtype)


def _ctx_attention(qkv, sink, *, n_seq, seq_len):
    kcol = ATT_WIDTH // KV_WIDTH
    blk = ATT_BLOCK
    nb = seq_len // blk
    return pl.pallas_call(
        _ctx_attn_kernel,
        grid=(n_seq, nb),
        in_specs=[pl.BlockSpec(memory_space=pltpu.SMEM),
                  pl.BlockSpec((blk, ATT_WIDTH), lambda b, n: (b * nb + n, 0)),
                  pl.BlockSpec((seq_len, KV_WIDTH), lambda b, n: (b, kcol)),
                  pl.BlockSpec((seq_len, KV_WIDTH), lambda b, n: (b, kcol + 1))],
        out_specs=pl.BlockSpec((blk, ATT_WIDTH), lambda b, n: (b * nb + n, 0)),
        out_shape=jax.ShapeDtypeStruct((n_seq * seq_len, ATT_WIDTH), BF16),
        compiler_params=_cparams("parallel", "parallel"),
        name="ctx_attention",
    )(sink, qkv, qkv, qkv)


def _lat_attn_kernel(sink_ref, q_ref, kp_ref, kc_ref, kn_ref, vp_ref, vc_ref, vn_ref, ck_ref, cv_ref, o_ref):
    n = pl.program_id(1)
    nb = pl.num_programs(1)
    blk = ATT_BLOCK
    row = lax.broadcasted_iota(jnp.int32, (ATT_GROUP * blk, blk), 0) % blk
    col = lax.broadcasted_iota(jnp.int32, (ATT_GROUP * blk, blk), 1)
    mask_prev = jnp.logical_and(col >= row, n > 0)
    mask_next = jnp.logical_and(col <= row, n < nb - 1)
    parts = [(ck_ref[0], cv_ref[0], None),
             (kp_ref[...], vp_ref[...], mask_prev),
             (kc_ref[...], vc_ref[...], None),
             (kn_ref[...], vn_ref[...], mask_next)]
    o_ref[...] = _attend(q_ref[...], parts, sink_ref).astype(o_ref.dtype)


def _lat_attention(qkv, sink, ck, cv, *, n_seq, seq_len):
    blk = ATT_BLOCK
    nb = seq_len // blk

    def cur(b, n):
        return b * nb + n

    def prev(b, n):
        return b * nb + jnp.maximum(n - 1, 0)

    def nxt(b, n):
        return b * nb + jnp.minimum(n + 1, nb - 1)

    kcol = ATT_WIDTH // KV_WIDTH
    vcol = kcol + 1
    return pl.pallas_call(
        _lat_attn_kernel,
        grid=(n_seq, nb),
        in_specs=[pl.BlockSpec(memory_space=pltpu.SMEM),
                  pl.BlockSpec((blk, ATT_WIDTH), lambda b, n: (cur(b, n), 0)),
                  pl.BlockSpec((blk, KV_WIDTH), lambda b, n: (prev(b, n), kcol)),
                  pl.BlockSpec((blk, KV_WIDTH), lambda b, n: (cur(b, n), kcol)),
                  pl.BlockSpec((blk, KV_WIDTH), lambda b, n: (nxt(b, n), kcol)),
                  pl.BlockSpec((blk, KV_WIDTH), lambda b, n: (prev(b, n), vcol)),
                  pl.BlockSpec((blk, KV_WIDTH), lambda b, n: (cur(b, n), vcol)),
                  pl.BlockSpec((blk, KV_WIDTH), lambda b, n: (nxt(b, n), vcol)),
                  pl.BlockSpec((1,) + ck.shape[1:], lambda b, n: (b, 0, 0)),
                  pl.BlockSpec((1,) + cv.shape[1:], lambda b, n: (b, 0, 0))],
        out_specs=pl.BlockSpec((blk, ATT_WIDTH), lambda b, n: (cur(b, n), 0)),
        out_shape=jax.ShapeDtypeStruct((n_seq * seq_len, ATT_WIDTH), BF16),
        compiler_params=_cparams("parallel", "parallel"),
        name="lat_attention",
    )(sink, qkv, qkv, qkv, qkv, qkv, qkv, qkv, ck, cv)


def _rope_tables(seq_len):
    rows = seq_len // GRID_W
    row = np.repeat(np.arange(rows, dtype=np.float32), GRID_W)
    col = np.tile(np.arange(GRID_W, dtype=np.float32), rows)
    inv = np.float32(ROPE_BASE) ** (-np.arange(ROPE_PER_AXIS, dtype=np.float32) / np.float32(ROPE_PER_AXIS))
    ang = np.concatenate([row[:, None] * inv, col[:, None] * inv], axis=-1)
    cos, sin = np.cos(ang), np.sin(ang)
    cos128 = np.tile(cos, (1, 4))
    sin128 = np.tile(np.concatenate([-sin, sin], axis=-1), (1, 2))
    return jnp.asarray(cos128, F32), jnp.asarray(sin128, F32)


def _conv_silu(scr, prev_ref, cur_ref, next_ref, w_ref, b_ref, is_first, is_last):
    q = SSD_CHUNK
    h = SSD_HALO
    scr[0:h, :] = jnp.where(is_first, 0.0, prev_ref[...])
    scr[h:h + q, :] = cur_ref[...]
    scr[h + q:h + q + h, :] = jnp.where(is_last, 0.0, next_ref[...])
    y = (scr[h - 1:h - 1 + q, :] * w_ref[0:1, :] + scr[h:h + q, :] * w_ref[1:2, :]
         + scr[h + 1:h + 1 + q, :] * w_ref[2:3, :] + b_ref[...])
    return _silu(y)


def _softplus(x):
    return jnp.maximum(x, 0.0) + jnp.log1p(jnp.exp(-jnp.abs(x)))


def _ssd_bwd_kernel(xsp_ref, xsc_ref, xsn_ref, bcp_ref, bcc_ref, bcn_ref, dt_ref, h0_ref,
                    cwx_ref, cbx_ref, cwb_ref, cbb_ref, dtb_ref, alog_ref, tri_ref, exp_ref,
                    hstart_ref, hfin_ref, xs_scr, bc_scr, h_scr):
    i = pl.program_id(1)
    nc = pl.num_programs(1)

    @pl.when(i == 0)
    def _():
        h_scr[...] = h0_ref[0]

    hb = h_scr[...]
    hstart_ref[0, 0] = hb
    is_first = i == nc - 1
    is_last = i == 0
    xs = _conv_silu(xs_scr, xsp_ref, xsc_ref, xsn_ref, cwx_ref, cbx_ref, is_first, is_last)
    bc = _conv_silu(bc_scr, bcp_ref, bcc_ref, bcn_ref, cwb_ref, cbb_ref, is_first, is_last)
    bm = bc[:, 0:SSD_GROUPS * SSD_STATE].astype(BF16)
    dtv = _softplus(dt_ref[...] + dtb_ref[...])
    dta = dtv * (-LOG2E * jnp.exp(alog_ref[...]))
    rev = _dot_sel_lhs(tri_ref[1], dta, 3)
    sel_b = exp_ref[1]
    dt_w = _dot_sel_rhs(dtv, sel_b, 2)
    to_end = _dot_sel_rhs(jnp.exp2(rev[0:1, :] - rev), sel_b, 2)
    decay = _dot_sel_rhs(jnp.exp2(rev[0:SSD_HALO, :]), sel_b, 2)[0:1, :]
    xw = (xs * dt_w * to_end).astype(BF16)
    half = SSD_WIDTH // SSD_GROUPS
    upd = jnp.concatenate(
        [_dot_tn(bm[:, g * SSD_STATE:(g + 1) * SSD_STATE], xw[:, g * half:(g + 1) * half])
         for g in range(SSD_GROUPS)], axis=-1)
    h_new = hb * decay + upd
    h_scr[...] = h_new

    @pl.when(i == nc - 1)
    def _():
        hfin_ref[0] = h_new.T


def _ssd_fwd_kernel(z_ref, xsp_ref, xsc_ref, xsn_ref, bcp_ref, bcc_ref, bcn_ref, dt_ref, hb_ref, h0_ref,
                    cwx_ref, cbx_ref, cwb_ref, cbb_ref, dtb_ref, alog_ref, tri_ref, exp_ref, dvec_ref, nw_ref,
                    y_ref, hfin_ref, xs_scr, bc_scr, h_scr):
    c = pl.program_id(1)
    nc = pl.num_programs(1)
    q = SSD_CHUNK
    nh = SSD_HEADS

    @pl.when(c == 0)
    def _():
        h_scr[...] = h0_ref[0]

    hf = h_scr[...]
    xs = _conv_silu(xs_scr, xsp_ref, xsc_ref, xsn_ref, cwx_ref, cbx_ref, c == 0, c == nc - 1)
    bc = _conv_silu(bc_scr, bcp_ref, bcc_ref, bcn_ref, cwb_ref, cbb_ref, c == 0, c == nc - 1)
    gs = SSD_GROUPS * SSD_STATE
    bm = bc[:, 0:gs].astype(BF16)
    cm = bc[:, gs:2 * gs].astype(BF16)
    xs_bf = xs.astype(BF16)

    dtv = _softplus(dt_ref[...] + dtb_ref[...])
    dta = dtv * (-LOG2E * jnp.exp(alog_ref[...]))
    lower, upper = tri_ref[0], tri_ref[1]
    cum = _dot_sel_lhs(lower, dta, 3)
    rev = _dot_sel_lhs(upper, dta, 3)
    dt_t = dtv.T
    dta_t = dta.T
    cum_t = _dot_sel_rhs(dta_t, upper, 3)
    rev_t = _dot_sel_rhs(dta_t, lower, 3)

    sel_f, sel_b = exp_ref[0], exp_ref[1]
    dtf_w = _dot_sel_rhs(dtv, sel_f, 2)
    dtb_w = _dot_sel_rhs(dtv, sel_b, 2)
    ecum_w = _dot_sel_rhs(jnp.exp2(cum), sel_f, 2)
    erev_w = _dot_sel_rhs(jnp.exp2(rev), sel_b, 2)
    toend_w = _dot_sel_rhs(jnp.exp2(cum[q - 1:q, :] - cum), sel_f, 2)

    row = lax.broadcasted_iota(jnp.int32, (q, q), 0)
    col = lax.broadcasted_iota(jnp.int32, (q, q), 1)
    causal = row >= col
    anti = row <= col
    gmat = [_dot_nt(cm[:, g * SSD_STATE:(g + 1) * SSD_STATE], bm[:, g * SSD_STATE:(g + 1) * SSD_STATE])
            for g in range(SSD_GROUPS)]
    per_group = nh // SSD_GROUPS
    ys = []
    for h in range(nh):
        seg_f = cum[:, h:h + 1] - cum_t[h:h + 1, :]
        dec_f = jnp.exp2(jnp.where(causal, seg_f, NEG)) * dt_t[h:h + 1, :]
        seg_b = rev[:, nh + h:nh + h + 1] - rev_t[nh + h:nh + h + 1, :]
        dec_b = jnp.exp2(jnp.where(anti, seg_b, NEG)) * dt_t[nh + h:nh + h + 1, :]
        m = (gmat[h // per_group] * (dec_f + dec_b)).astype(BF16)
        ys.append(_dot(m, xs_bf[:, h * SSD_HEAD_DIM:(h + 1) * SSD_HEAD_DIM]))
    y = jnp.concatenate(ys, axis=-1)

    half = SSD_WIDTH // SSD_GROUPS
    hf_bf = hf.astype(BF16)
    hb_bf = hb_ref[0, 0].astype(BF16)
    off_f = jnp.concatenate([_dot(cm[:, g * SSD_STATE:(g + 1) * SSD_STATE], hf_bf[:, g * half:(g + 1) * half])
                             for g in range(SSD_GROUPS)], axis=-1)
    off_b = jnp.concatenate([_dot(cm[:, g * SSD_STATE:(g + 1) * SSD_STATE], hb_bf[:, g * half:(g + 1) * half])
                             for g in range(SSD_GROUPS)], axis=-1)
    y = y + off_f * ecum_w + off_b * erev_w + dvec_ref[...] * xs
    y = y * _silu(z_ref[...])
    y_ref[...] = _rms(y, nw_ref[...]).astype(y_ref.dtype)

    xw = (xs * dtf_w * toend_w).astype(BF16)
    upd = jnp.concatenate(
        [_dot_tn(bm[:, g * SSD_STATE:(g + 1) * SSD_STATE], xw[:, g * half:(g + 1) * half])
         for g in range(SSD_GROUPS)], axis=-1)
    h_new = hf * ecum_w[q - 1:q, :] + upd
    h_scr[...] = h_new

    @pl.when(c == nc - 1)
    def _():
        hfin_ref[0] = h_new.T


def _ssd_mixer(proj, h0_f, h0_b, consts, *, n_seq, seq_len):
    q = SSD_CHUNK
    nc = seq_len // q
    per8 = q // SSD_HALO
    n_rows8 = proj.shape[0] // SSD_HALO
    (cwx, cbx, cwb, cbb, dt_bias, a_log, tri, expand, dvec, nw) = consts

    def specs(chunk_of):
        def cur(b, i):
            return b * nc + chunk_of(i)

        def prev8(b, i):
            return jnp.maximum(cur(b, i) * per8 - 1, 0)

        def next8(b, i):
            return jnp.minimum((cur(b, i) + 1) * per8, n_rows8 - 1)

        xcol, bcol, dcol = C_XS // SSD_WIDTH, C_BC // SSD_BC, C_DT // DT_PAD
        return [pl.BlockSpec((SSD_HALO, SSD_WIDTH), lambda b, i: (prev8(b, i), xcol)),
                pl.BlockSpec((q, SSD_WIDTH), lambda b, i: (cur(b, i), xcol)),
                pl.BlockSpec((SSD_HALO, SSD_WIDTH), lambda b, i: (next8(b, i), xcol)),
                pl.BlockSpec((SSD_HALO, SSD_BC), lambda b, i: (prev8(b, i), bcol)),
                pl.BlockSpec((q, SSD_BC), lambda b, i: (cur(b, i), bcol)),
                pl.BlockSpec((SSD_HALO, SSD_BC), lambda b, i: (next8(b, i), bcol)),
                pl.BlockSpec((q, DT_PAD), lambda b, i: (cur(b, i), dcol))], cur

    state_spec = pl.BlockSpec((1, SSD_STATE, SSD_WIDTH), lambda b, i: (b, 0, 0))
    final_spec = pl.BlockSpec((1, SSD_WIDTH, SSD_STATE), lambda b, i: (b, 0, 0))
    const_specs = [_resident(cwx.shape), _resident(cbx.shape), _resident(cwb.shape), _resident(cbb.shape),
                   _resident(dt_bias.shape), _resident(a_log.shape), _resident(tri.shape),
                   _resident(expand.shape)]
    scratch = [pltpu.VMEM((q + 2 * SSD_HALO, SSD_WIDTH), F32), pltpu.VMEM((q + 2 * SSD_HALO, SSD_BC), F32),
               pltpu.VMEM((SSD_STATE, SSD_WIDTH), F32)]

    data_specs, _ = specs(lambda i: nc - 1 - i)
    hb_start, hb_fin = pl.pallas_call(
        _ssd_bwd_kernel,
        grid=(n_seq, nc),
        in_specs=data_specs + [state_spec] + const_specs,
        out_specs=[pl.BlockSpec((1, 1, SSD_STATE, SSD_WIDTH), lambda b, i: (b, nc - 1 - i, 0, 0)), final_spec],
        out_shape=[jax.ShapeDtypeStruct((n_seq, nc, SSD_STATE, SSD_WIDTH), F32),
                   jax.ShapeDtypeStruct((n_seq, SSD_WIDTH, SSD_STATE), F32)],
        scratch_shapes=scratch,
        compiler_params=_cparams("parallel", "arbitrary"),
        name="ssd_backward_states",
    )(proj, proj, proj, proj, proj, proj, proj, h0_b, cwx, cbx, cwb, cbb, dt_bias, a_log, tri, expand)

    data_specs, cur = specs(lambda i: i)
    y, hf_fin = pl.pallas_call(
        _ssd_fwd_kernel,
        grid=(n_seq, nc),
        in_specs=([pl.BlockSpec((q, SSD_WIDTH), lambda b, i: (cur(b, i), C_Z // SSD_WIDTH))] + data_specs
                  + [pl.BlockSpec((1, 1, SSD_STATE, SSD_WIDTH), lambda b, i: (b, i, 0, 0)), state_spec]
                  + const_specs + [_resident(dvec.shape), _resident(nw.shape)]),
        out_specs=[pl.BlockSpec((q, SSD_WIDTH), lambda b, i: (b * nc + i, 0)), final_spec],
        out_shape=[jax.ShapeDtypeStruct((n_seq * seq_len, SSD_WIDTH), BF16),
                   jax.ShapeDtypeStruct((n_seq, SSD_WIDTH, SSD_STATE), F32)],
        scratch_shapes=scratch,
        compiler_params=_cparams("parallel", "arbitrary"),
        name="ssd_forward",
    )(proj, proj, proj, proj, proj, proj, proj, proj, hb_start, h0_f,
      cwx, cbx, cwb, cbb, dt_bias, a_log, tri, expand, dvec, nw)
    return y, hf_fin, hb_fin


def _ssd_consts(l, ssd_conv_w, ssd_conv_b, ssd_dt_bias, ssd_A_log, ssd_D, ssd_norm):
    q = SSD_CHUNK
    cw, cb = ssd_conv_w[l], ssd_conv_b[l]
    pad = DT_PAD - 2 * SSD_HEADS
    dt_bias = jnp.pad(ssd_dt_bias[l].reshape(1, -1), ((0, 0), (0, pad)))
    a_log = jnp.pad(ssd_A_log[l].reshape(1, -1), ((0, 0), (0, pad)))
    r = np.arange(q)
    tri = jnp.asarray(np.stack([r[None, :] <= r[:, None], r[None, :] >= r[:, None]]), BF16)
    head_of_col = np.arange(SSD_WIDTH) // SSD_HEAD_DIM
    j = np.arange(DT_PAD)
    expand = jnp.asarray(np.stack([j[:, None] == head_of_col[None, :],
                                   j[:, None] == head_of_col[None, :] + SSD_HEADS]), BF16)
    dvec = jnp.repeat(ssd_D[l], SSD_HEAD_DIM).reshape(1, SSD_WIDTH)
    return (cw[:, :SSD_WIDTH], cb[:SSD_WIDTH].reshape(1, -1), cw[:, SSD_WIDTH:], cb[SSD_WIDTH:].reshape(1, -1),
            dt_bias, a_log, tri, expand, dvec, ssd_norm[l].reshape(1, -1))


def _ssd_state_in(s):
    b = s.shape[0]
    return jnp.transpose(s, (0, 3, 1, 2)).reshape(b, SSD_STATE, SSD_WIDTH)


def _ssd_state_out(s):
    return s.reshape(s.shape[0], SSD_HEADS, SSD_HEAD_DIM, SSD_STATE)


def _s5_table_kernel(pw_ref, ct_ref, bb_ref, bbt_ref, sel_ref, tile_ref, rhs_ref, woff_ref):
    t = S5_T
    c = S5_CH
    tile = tile_ref[...]
    spread = lambda x, sel: _dot_sel_rhs(x, sel, 3)
    krows, offs, cols = [], [], []
    for d in range(2):
        pw_re, pw_im = pw_ref[0, 2 * d], pw_ref[0, 2 * d + 1]
        c_re = spread(ct_ref[0, :, (2 * d) * c:(2 * d + 1) * c], tile)
        c_im = spread(ct_ref[0, :, (2 * d + 1) * c:(2 * d + 2) * c], tile)
        b_re = spread(bb_ref[0, :, (2 * d) * c:(2 * d + 1) * c], tile)
        b_im = spread(bb_ref[0, :, (2 * d + 1) * c:(2 * d + 2) * c], tile)
        bt_re = bbt_ref[0, (2 * d) * c:(2 * d + 1) * c, :]
        bt_im = bbt_ref[0, (2 * d + 1) * c:(2 * d + 2) * c, :]
        up = (spread(pw_re, sel_ref[0]), spread(pw_im, sel_ref[0]))
        down = (spread(pw_re, sel_ref[1]), spread(pw_im, sel_ref[1]))
        (q_re, q_im), state = ((up, down), (down, up))[d]
        g_re = q_re * c_re - q_im * c_im
        g_im = q_re * c_im + q_im * c_re
        krows.append(_dot_hi(bt_re, g_re) - _dot_hi(bt_im, g_im))
        lb_re, lb_im = pw_re[:, 1:2], pw_im[:, 1:2]
        r_re = q_re * lb_re - q_im * lb_im
        r_im = q_re * lb_im + q_im * lb_re
        offs.append((r_re * c_re - r_im * c_im, -(r_re * c_im + r_im * c_re)))
        s_re, s_im = state
        cols.append(((s_re * b_re - s_im * b_im).T, (s_re * b_im + s_im * b_re).T))
    k_f, k_b = krows
    for s in range(t):
        r = t - 1 - s
        f = k_f if s == 0 else jnp.concatenate([jnp.zeros((c, c * s), F32), k_f[:, :S5_TW - c * s]], axis=1)
        b = k_b if r == 0 else jnp.concatenate([k_b[:, c * r:], jnp.zeros((c, c * r), F32)], axis=1)
        rhs_ref[0, s * c:(s + 1) * c, 0:S5_TW] = (f + b).astype(BF16)
    rhs_ref[0, :, S5_TW:] = jnp.concatenate([cols[0][0], cols[1][0], cols[0][1], cols[1][1]], axis=1).astype(BF16)
    woff_ref[0] = jnp.concatenate([offs[0][0], offs[1][0], offs[0][1], offs[1][1]], axis=0).astype(BF16)


def _s5_tables(s5_A_re, s5_A_im, s5_log_dt, s5_B_re, s5_B_im, s5_C_re, s5_C_im):
    t = S5_T
    n = N_LAYERS * S5_GROUPS
    step = jnp.exp(s5_log_dt)[..., None]
    k = jnp.arange(LANES, dtype=F32)
    keep = k <= t
    kk = jnp.where(keep, k, 0.0)
    mag = jnp.exp(kk * (s5_A_re * step)[..., None])
    ang = kk * (s5_A_im * step)[..., None]
    pw_re = jnp.where(keep, mag * jnp.cos(ang), 0.0)
    pw_im = jnp.where(keep, mag * jnp.sin(ang), 0.0)
    lb_re, lb_im = pw_re[..., 1], pw_im[..., 1]
    den = s5_A_re * s5_A_re + s5_A_im * s5_A_im
    r_re = ((lb_re - 1.0) * s5_A_re + lb_im * s5_A_im) / den
    r_im = (lb_im * s5_A_re - (lb_re - 1.0) * s5_A_im) / den
    b_re, b_im = s5_B_re[:, None], s5_B_im[:, None]
    bb_re = r_re[..., None] * b_re - r_im[..., None] * b_im
    bb_im = r_re[..., None] * b_im + r_im[..., None] * b_re
    c_re = jnp.swapaxes(s5_C_re, -1, -2)
    c_im = jnp.swapaxes(s5_C_im, -1, -2)

    def pack(re, im, axis):
        return jnp.concatenate([re[:, 0], im[:, 0], re[:, 1], im[:, 1]], axis=axis)

    pw = jnp.stack([pw_re[:, 0], pw_im[:, 0], pw_re[:, 1], pw_im[:, 1]], axis=2).reshape(n, 4, S5_P, LANES)
    ct = pack(c_re, c_im, -1).reshape(n, S5_P, 4 * S5_CH)
    bb = pack(bb_re, bb_im, -1)
    bbt = jnp.swapaxes(bb, -1, -2).reshape(n, 4 * S5_CH, S5_P)
    bb = bb.reshape(n, S5_P, 4 * S5_CH)
    lam_t = jnp.concatenate([pw_re[:, 0, :, :, t], pw_re[:, 1, :, :, t], pw_im[:, 0, :, :, t],
                             pw_im[:, 1, :, :, t]], axis=-1).reshape(n, 1, S5_ST)
    jt = np.arange(S5_TW) // S5_CH
    lane_k = np.arange(LANES)[:, None]
    sel = jnp.asarray(np.stack([lane_k == f[None, :] for f in (jt, t - 1 - jt)]), BF16)
    tile = jnp.asarray(np.arange(S5_CH)[:, None] == (np.arange(S5_TW) % S5_CH)[None, :], BF16)
    grp = lambda shape: pl.BlockSpec((1,) + shape, lambda i: (i,) + (0,) * len(shape))
    rhs, woff = pl.pallas_call(
        _s5_table_kernel,
        grid=(n,),
        in_specs=[grp((4, S5_P, LANES)), grp((S5_P, 4 * S5_CH)), grp((S5_P, 4 * S5_CH)), grp((4 * S5_CH, S5_P)),
                  _resident(sel.shape), _resident(tile.shape)],
        out_specs=[grp((S5_TW, S5_TW + S5_ST)), grp((S5_ST, S5_TW))],
        out_shape=[jax.ShapeDtypeStruct((n, S5_TW, S5_TW + S5_ST), BF16),
                   jax.ShapeDtypeStruct((n, S5_ST, S5_TW), BF16)],
        compiler_params=_cparams("parallel"),
        name="s5_tables",
    )(pw, ct, bb, bbt, sel, tile)
    return rhs, woff, lam_t


def _unit_transpose(vs, masks, axis, unit):
    size = vs[0].shape[axis]
    cur = list(vs)
    for d in (1, 2, 4):
        nxt = list(cur)
        for a in range(S5_OCT):
            if a & d:
                continue
            b = a | d
            nxt[a] = jnp.where(masks[d], cur[a], pltpu.roll(cur[b], unit * d, axis=axis))
            nxt[b] = jnp.where(masks[d], pltpu.roll(cur[a], size - unit * d, axis=axis), cur[b])
        cur = nxt
    return cur


def _s5_kernel(u_ref, rhs_ref, woff_ref, lam_ref, h0_ref, y_ref, hfin_ref,
               dst_scr, y_scr, sre_scr, sim_scr, hfre_scr, hfim_scr, hbre_scr, hbim_scr, *, nc, rb):
    t = S5_T
    rows = dst_scr.shape[1]
    sb = rows // nc
    tiles = t // S5_OCT
    nr = rb // S5_OCT
    tile_shape = (nr, S5_OCT, LANES)
    seg = lax.broadcasted_iota(jnp.int32, tile_shape, 2) // S5_CH
    sub = lax.broadcasted_iota(jnp.int32, tile_shape, 1)
    lane_masks = {d: (seg & d) == 0 for d in (1, 2, 4)}
    sub_masks = {d: (sub & d) == 0 for d in (1, 2, 4)}

    def tile_rows(i, r, tq):
        return pl.ds((i * rb + r) * tiles + tq, nr, stride=S5_OCT * tiles)

    def gather(i, carry):
        r0 = pl.multiple_of(i * rb, rb)
        for tq in range(tiles):
            by_chunk = [u_ref[tile_rows(i, r, tq)] for r in range(S5_OCT)]
            by_time = _unit_transpose(by_chunk, sub_masks, 1, 1)
            for gl, v in enumerate(_unit_transpose(by_time, lane_masks, 2, S5_CH)):
                dst_scr[gl, pl.ds(r0, rb), tq * LANES:(tq + 1) * LANES] = v.reshape(rb, LANES).astype(BF16)
        return carry

    lax.fori_loop(0, rows // rb, gather, 0)

    for gl in range(S5_OCT):
        z = _dot(dst_scr[gl], rhs_ref[gl])
        y_scr[gl] = z[:, 0:S5_TW]
        sre_scr[gl] = z[:, S5_TW:S5_TW + 2 * S5_P]
        sim_scr[gl] = z[:, S5_TW + 2 * S5_P:]

    fwd_lane = lax.broadcasted_iota(jnp.int32, (sb, 2 * S5_P), 1) < S5_P
    a_re = [lam_ref[gl][:, 0:2 * S5_P] for gl in range(S5_OCT)]
    a_im = [lam_ref[gl][:, 2 * S5_P:] for gl in range(S5_OCT)]

    def step(i, carry):
        fi = pl.ds(i, sb, stride=nc)
        bj = pl.ds(nc - 1 - i, sb, stride=nc)
        out = []
        for gl in range(S5_OCT):
            h_re, h_im = carry[2 * gl], carry[2 * gl + 1]
            hfre_scr[gl, fi, :] = h_re
            hfim_scr[gl, fi, :] = h_im
            hbre_scr[gl, bj, :] = h_re
            hbim_scr[gl, bj, :] = h_im
            s_re = jnp.where(fwd_lane, sre_scr[gl, fi, :], sre_scr[gl, bj, :])
            s_im = jnp.where(fwd_lane, sim_scr[gl, fi, :], sim_scr[gl, bj, :])
            out.append(a_re[gl] * h_re - a_im[gl] * h_im + s_re)
            out.append(a_re[gl] * h_im + a_im[gl] * h_re + s_im)
        return tuple(out)

    init = []
    for gl in range(S5_OCT):
        h0 = h0_ref[gl, 0]
        init += [h0[:, 0:2 * S5_P], h0[:, 2 * S5_P:]]
    fin = lax.fori_loop(0, nc, step, tuple(init))

    all_fwd = lax.broadcasted_iota(jnp.int32, (rows, 2 * S5_P), 1) < S5_P
    for gl in range(S5_OCT):
        hfin_ref[gl, 0] = jnp.concatenate([fin[2 * gl], fin[2 * gl + 1]], axis=-1)
        hin = jnp.concatenate([jnp.where(all_fwd, hfre_scr[gl], hbre_scr[gl]),
                               jnp.where(all_fwd, hfim_scr[gl], hbim_scr[gl])], axis=-1).astype(BF16)
        y_scr[gl] += _dot(hin, woff_ref[gl])

    def scatter(i, carry):
        r0 = pl.multiple_of(i * rb, rb)
        for tq in range(tiles):
            by_group = [y_scr[gl, pl.ds(r0, rb), tq * LANES:(tq + 1) * LANES].reshape(tile_shape)
                        for gl in range(S5_OCT)]
            by_time = _unit_transpose(by_group, lane_masks, 2, S5_CH)
            for r, v in enumerate(_unit_transpose(by_time, sub_masks, 1, 1)):
                y_ref[tile_rows(i, r, tq)] = v
        return carry

    lax.fori_loop(0, rows // rb, scatter, 0)


def _s5_mixer(proj, tables, layer, h0, *, n_seq, seq_len):
    rhs, woff, lam_t = tables
    t = S5_T
    nc = seq_len // t
    sb = max(1, min(n_seq, S5_MAX_ROWS // nc))
    assert n_seq % sb == 0
    nbs = n_seq // sb
    rows = sb * nc
    rb = min(rows, 32)
    noct = S5_GROUPS // S5_OCT
    ucol = C_U // LANES
    tab = lambda shape: pl.BlockSpec((S5_OCT,) + shape, lambda j, s: (layer * noct + j, 0, 0))
    state = pl.BlockSpec((S5_OCT, 1, sb, S5_ST), lambda j, s: (j, s, 0, 0))
    rows_scr = lambda dt: pltpu.VMEM((S5_OCT, rows, 2 * S5_P), dt)
    n_tok = n_seq * seq_len
    tok_block = (sb * seq_len // S5_OCT, S5_OCT, LANES)
    y, hfin = pl.pallas_call(
        functools.partial(_s5_kernel, nc=nc, rb=rb),
        grid=(noct, nbs),
        in_specs=[pl.BlockSpec(tok_block, lambda j, s: (s, 0, ucol + j)),
                  tab(rhs.shape[1:]), tab(woff.shape[1:]), tab(lam_t.shape[1:]), state],
        out_specs=[pl.BlockSpec(tok_block, lambda j, s: (s, 0, j)), state],
        out_shape=[jax.ShapeDtypeStruct((n_tok // S5_OCT, S5_OCT, S5_WIDTH), F32),
                   jax.ShapeDtypeStruct((S5_GROUPS, nbs, sb, S5_ST), F32)],
        scratch_shapes=[pltpu.VMEM((S5_OCT, rows, S5_TW), BF16), pltpu.VMEM((S5_OCT, rows, S5_TW), F32),
                        rows_scr(F32), rows_scr(F32), rows_scr(F32), rows_scr(F32), rows_scr(F32), rows_scr(F32)],
        compiler_params=_cparams("parallel", "parallel"),
        name="s5_mixer",
    )(proj.reshape(n_tok // S5_OCT, S5_OCT, D_INP), rhs, woff, lam_t, h0.reshape(S5_GROUPS, nbs, sb, S5_ST))
    return y.reshape(n_tok, S5_WIDTH), hfin.reshape(S5_GROUPS, n_seq, S5_ST)


def _s5_state_in(re, im):
    b = re.shape[0]
    x = jnp.stack([re, im], axis=1)
    return jnp.transpose(x, (3, 0, 1, 2, 4)).reshape(S5_GROUPS, b, S5_ST)


def _s5_state_out(h):
    n_seq = h.shape[1]
    x = h.reshape(S5_GROUPS, n_seq, 2, 2, S5_P)
    x = jnp.transpose(x, (2, 1, 3, 0, 4))
    return x[0], x[1]


def kernel(x_prompt, x_sample, c, cache_k, cache_v, state_ssd, state_s5_re, state_s5_im, c_ctx, w_ada, b_ada, norm_mix_pre, norm_mix_post, norm_ffn_pre, norm_ffn_post, w_in, w_out, ssd_conv_w, ssd_conv_b, ssd_dt_bias, ssd_A_log, ssd_D, ssd_norm, attn_sink, s5_A_re, s5_A_im, s5_log_dt, s5_B_re, s5_B_im, s5_C_re, s5_C_im, s5_D, s5_w_glu, s5_b_glu, w_ffn_in, w_ffn_out):
    nb_ctx, len_ctx, _ = x_prompt.shape
    nb_lat, len_lat, _ = x_sample.shape
    n_ctx = nb_ctx * len_ctx
    n_lat = nb_lat * len_lat
    tm = 512
    mod_rows = -(-(1 + nb_lat) // 8) * 8
    cvecs = jnp.concatenate([c_ctx[None, :], c, jnp.zeros((mod_rows - 1 - nb_lat, D_MODEL), F32)], axis=0)
    mods = _ada(cvecs, w_ada, b_ada)
    mod_geom = ((n_ctx, 0), (len_lat, 1))
    token_sets = tuple(dict(mod_spec=_mod_spec(tm, *g)) for g in mod_geom)
    tm_up = min(FFN_UP_ROWS, n_ctx, len_lat)
    rope_args = (dict(), dict(rope_tables=_rope_tables(len_lat), seq_len=len_lat))

    s5_tab = _s5_tables(s5_A_re, s5_A_im, s5_log_dt, s5_B_re, s5_B_im, s5_C_re, s5_C_im)
    w_in_p = jnp.concatenate(
        [w_in[:, :, O_Q:O_K], w_in[:, :, O_Z:O_XS], w_in[:, :, O_XS:O_BC], w_in[:, :, O_BC:O_DT],
         w_in[:, :, O_K:O_V], w_in[:, :, O_V:O_U], w_in[:, :, O_U:O_END], w_in[:, :, O_DT:O_Q],
         jnp.zeros((N_LAYERS, D_MODEL, DT_PAD - 2 * SSD_HEADS), F32)], axis=2).astype(BF16)
    w_out_b = w_out.astype(BF16)
    w_glu_b = s5_w_glu.astype(BF16)
    w_ffn_in_b = w_ffn_in.astype(BF16)
    w_ffn_out_b = w_ffn_out.astype(BF16)
    zeros_ssd = jnp.zeros((nb_ctx, SSD_STATE, SSD_WIDTH), F32)
    zeros_s5 = jnp.zeros((S5_GROUPS, nb_ctx, S5_ST), F32)

    xs = [x_prompt.reshape(n_ctx, D_MODEL), x_sample.reshape(n_lat, D_MODEL)]
    new_k, new_v, new_ssd, new_s5_re, new_s5_im = [], [], [], [], []
    for l in range(N_LAYERS):
        mod3 = mods[l].reshape(mod_rows, 1, 6 * D_MODEL)
        proj, qkv = zip(*[_in_proj(x, mod3, norm_mix_pre[l].reshape(1, -1), w_in_p, layer=l, tm=tm, **ts, **ra)
                          for x, ts, ra in zip(xs, token_sets, rope_args)])

        sink = attn_sink[l]
        ck = cache_k[:, l].reshape(nb_lat, -1, KV_WIDTH).astype(BF16)
        cv = cache_v[:, l].reshape(nb_lat, -1, KV_WIDTH).astype(BF16)
        o_att = [_ctx_attention(qkv[0], sink, n_seq=nb_ctx, seq_len=len_ctx),
                 _lat_attention(qkv[1], sink, ck, cv, n_seq=nb_lat, seq_len=len_lat)]

        consts = _ssd_consts(l, ssd_conv_w, ssd_conv_b, ssd_dt_bias, ssd_A_log, ssd_D, ssd_norm)
        y_ctx, hf_ctx, hb_ctx = _ssd_mixer(proj[0], zeros_ssd, zeros_ssd, consts, n_seq=nb_ctx, seq_len=len_ctx)
        y_lat, _, _ = _ssd_mixer(proj[1], _ssd_state_in(state_ssd[:, l, 0]), _ssd_state_in(state_ssd[:, l, 1]),
                                 consts, n_seq=nb_lat, seq_len=len_lat)
        y_ssd = [y_ctx, y_lat]

        s5_ctx, s5_fin = _s5_mixer(proj[0], s5_tab, l, zeros_s5, n_seq=nb_ctx, seq_len=len_ctx)
        s5_lat, _ = _s5_mixer(proj[1], s5_tab, l, _s5_state_in(state_s5_re[:, l], state_s5_im[:, l]),
                              n_seq=nb_lat, seq_len=len_lat)
        y_s5 = [s5_ctx, s5_lat]

        xs = [_out_proj(x, mod3, norm_mix_post[l].reshape(1, -1), y_ssd[i], o_att[i], y_s5[i], proj[i],
                        s5_D[l].reshape(1, -1), w_glu_b, s5_b_glu[l].reshape(1, -1), w_out_b,
                        layer=l, tm=tm, **ts)
              for i, (x, ts) in enumerate(zip(xs, token_sets))]
        xs = [_ffn(x, mod3, norm_ffn_pre[l].reshape(1, -1), norm_ffn_post[l].reshape(1, -1),
                   w_ffn_in_b, w_ffn_out_b, layer=l, tm_up=tm_up, tf=FFN_UP_COLS, tm_down=tm,
                   mod_spec_up=_mod_spec(tm_up, *g), mod_spec_down=_mod_spec(tm, *g))
              for x, g in zip(xs, mod_geom)]

        new_k.append(proj[0][:, C_K:C_K + KV_WIDTH].reshape(nb_ctx, len_ctx, ATT_KV, HEAD_DIM))
        new_v.append(proj[0][:, C_V:C_V + KV_WIDTH].reshape(nb_ctx, len_ctx, ATT_KV, HEAD_DIM))
        new_ssd.append(jnp.stack([_ssd_state_out(hf_ctx), _ssd_state_out(hb_ctx)], axis=1))
        re, im = _s5_state_out(s5_fin)
        new_s5_re.append(re)
        new_s5_im.append(im)

    y_prompt = xs[0].reshape(nb_ctx, len_ctx, D_MODEL)
    y_sample = xs[1].reshape(nb_lat, len_lat, D_MODEL)
    return (y_prompt, y_sample, jnp.stack(new_k, axis=1), jnp.stack(new_v, axis=1), jnp.stack(new_ssd, axis=1),
            jnp.stack(new_s5_re, axis=1), jnp.stack(new_s5_im, axis=1))
```

```python
import functools
import math

import jax
import jax.numpy as jnp
import numpy as np
from jax import lax
from jax.experimental import pallas as pl
from jax.experimental.pallas import tpu as pltpu

F32 = jnp.float32
BF16 = jnp.bfloat16
HIGHEST = lax.Precision.HIGHEST

D_MODEL = 2048
N_LAYERS = 2
D_FF = 5632
GRID_W = 64
EPS = 1e-6
LANES = 128

SSD_HEADS = 12
SSD_HEAD_DIM = 64
SSD_WIDTH = SSD_HEADS * SSD_HEAD_DIM
SSD_GROUPS = 2
SSD_STATE = 64
SSD_BC = 2 * SSD_GROUPS * SSD_STATE
SSD_CHUNK = 128
SSD_CHUNKS_PER_STEP = 4
SSD_HALO = 8
ATT_HEADS = 12
ATT_KV = 4
ATT_GROUP = ATT_HEADS // ATT_KV
HEAD_DIM = 64
ATT_WIDTH = ATT_HEADS * HEAD_DIM
KV_WIDTH = ATT_KV * HEAD_DIM
ATT_BLOCK = 128
QKV_WIDTH = ATT_WIDTH + 2 * KV_WIDTH
LOG2E = math.log2(math.e)
Q_SCALE = HEAD_DIM ** -0.5 * LOG2E
ROPE_PER_AXIS = HEAD_DIM // 4
ROPE_BASE = 10000.0
S5_GROUPS = 32
S5_CH = 16
S5_WIDTH = S5_GROUPS * S5_CH
S5_P = 64
S5_T = 32
S5_TW = S5_T * S5_CH
S5_ST = 4 * S5_P
S5_OCT = LANES // S5_CH
S5_MAX_ROWS = 256
MIX_WIDTH = SSD_WIDTH + ATT_WIDTH + S5_WIDTH

O_Z = 0
O_XS = SSD_WIDTH
O_BC = O_XS + SSD_WIDTH
O_DT = O_BC + SSD_BC
O_Q = O_DT + 2 * SSD_HEADS
O_K = O_Q + ATT_WIDTH
O_V = O_K + KV_WIDTH
O_U = O_V + KV_WIDTH
O_END = O_U + S5_WIDTH
C_Q = 0
C_Z = 768
C_XS = 1536
C_BC = 2304
C_K = 2560
C_V = 2816
C_U = 3072
C_DT = 3584
DT_PAD = 128
D_INP = C_DT + DT_PAD

FFN_UP_ROWS = 1024
FFN_UP_COLS = 512

NEG = -1e30
VMEM_LIMIT = 56 * 1024 * 1024


def _cparams(*sem):
    return pltpu.CompilerParams(dimension_semantics=sem, vmem_limit_bytes=VMEM_LIMIT)


def _resident(shape):
    nd = len(shape)
    return pl.BlockSpec(shape, lambda *_: (0,) * nd, pipeline_mode=pl.Buffered(1))


def _layer_resident(shape, layer, block=0):
    rest = (0,) * (len(shape) - 1)
    return pl.BlockSpec((None,) + shape, lambda *_: (layer, block) + rest, pipeline_mode=pl.Buffered(1))


def _rms(x, g):
    return x * lax.rsqrt(jnp.mean(x * x, axis=-1, keepdims=True) + EPS) * g


def _silu(x):
    return x * jax.nn.sigmoid(x)


def _dot(a, b):
    return jnp.dot(a, b, preferred_element_type=F32)


def _dot_hi(a, b):
    return jnp.dot(a, b, preferred_element_type=F32, precision=HIGHEST)


def _dot_nt(a, b):
    return lax.dot_general(a, b, (((1,), (1,)), ((), ())), preferred_element_type=F32)


def _dot_tn(a, b):
    return lax.dot_general(a, b, (((0,), (0,)), ((), ())), preferred_element_type=F32)


def _split_bf16(x, parts):
    out = []
    for _ in range(parts):
        p = x.astype(BF16)
        out.append(p)
        x = x - p.astype(F32)
    return out


def _dot_sel_rhs(x, sel, parts):
    acc = None
    for p in _split_bf16(x, parts):
        t = _dot(p, sel)
        acc = t if acc is None else acc + t
    return acc


def _dot_sel_lhs(sel, x, parts):
    acc = None
    for p in _split_bf16(x, parts):
        t = _dot(sel, p)
        acc = t if acc is None else acc + t
    return acc


def _mod_spec(tm, rows_per_mod, first_row):
    return pl.BlockSpec((1, 1, 6 * D_MODEL), lambda i, *_: (first_row + (i * tm) // rows_per_mod, 0, 0))


def _ada_kernel(c_ref, w_ref, b_ref, o_ref):
    c = c_ref[...]
    o_ref[0] = _dot(_silu(c).astype(BF16), w_ref[0].astype(BF16)) + b_ref[0]


def _ada(cvecs, w_ada, b_ada):
    tn = 1024
    rows = cvecs.shape[0]
    return pl.pallas_call(
        _ada_kernel,
        grid=(N_LAYERS, 6 * D_MODEL // tn),
        in_specs=[pl.BlockSpec((rows, D_MODEL), lambda l, j: (0, 0)),
                  pl.BlockSpec((1, D_MODEL, tn), lambda l, j: (l, 0, j)),
                  pl.BlockSpec((1, 1, tn), lambda l, j: (l, 0, j))],
        out_specs=pl.BlockSpec((1, rows, tn), lambda l, j: (l, 0, j)),
        out_shape=jax.ShapeDtypeStruct((N_LAYERS, rows, 6 * D_MODEL), F32),
        compiler_params=_cparams("parallel", "parallel"),
        name="ada_mod",
    )(cvecs, w_ada, b_ada.reshape(N_LAYERS, 1, 6 * D_MODEL))


def _rope(x, cos, sin_signed, first_half):
    outs = []
    for j in range(x.shape[1] // LANES):
        xj = x[:, j * LANES:(j + 1) * LANES]
        partner = jnp.where(first_half, pltpu.roll(xj, LANES - HEAD_DIM // 2, axis=1),
                            pltpu.roll(xj, HEAD_DIM // 2, axis=1))
        outs.append(xj * cos + partner * sin_signed)
    return jnp.concatenate(outs, axis=-1)


def _in_proj_kernel(*refs, rope):
    if rope:
        x_ref, mod_ref, g_ref, w_ref, cos_ref, sin_ref, o_ref, qkv_ref = refs
    else:
        x_ref, mod_ref, g_ref, w_ref, o_ref, qkv_ref = refs
    mod = mod_ref[0]
    sh = mod[:, 0:D_MODEL]
    sc = mod[:, D_MODEL:2 * D_MODEL]
    h = _rms(x_ref[...], g_ref[...]) * (1.0 + sc) + sh
    o_ref[...] = _dot(h.astype(BF16), w_ref[...])
    o_ref[:, C_Z:C_Z + SSD_WIDTH] = _silu(o_ref[:, C_Z:C_Z + SSD_WIDTH])
    q = o_ref[:, C_Q:C_Q + ATT_WIDTH]
    k = o_ref[:, C_K:C_K + KV_WIDTH]
    if rope:
        lane = lax.broadcasted_iota(jnp.int32, (x_ref.shape[0], LANES), 1)
        first_half = (lane % HEAD_DIM) < (HEAD_DIM // 2)
        q = _rope(q, cos_ref[...], sin_ref[...], first_half)
        k = _rope(k, cos_ref[...], sin_ref[...], first_half)
    qkv_ref[:, 0:ATT_WIDTH] = (q * Q_SCALE).astype(BF16)
    qkv_ref[:, ATT_WIDTH:ATT_WIDTH + KV_WIDTH] = k.astype(BF16)
    qkv_ref[:, ATT_WIDTH + KV_WIDTH:] = o_ref[:, C_V:C_V + KV_WIDTH].astype(BF16)


def _in_proj(x, mod3, gamma, w, *, layer, tm, mod_spec, rope_tables=None, seq_len=None):
    n = x.shape[0]
    rope = rope_tables is not None
    tables, table_specs = (), []
    if rope:
        per_seq = seq_len // tm
        tables = tuple(rope_tables)
        table_specs = [pl.BlockSpec((tm, LANES), lambda i: (i % per_seq, 0))] * 2
    return pl.pallas_call(
        functools.partial(_in_proj_kernel, rope=rope),
        grid=(n // tm,),
        in_specs=[pl.BlockSpec((tm, D_MODEL), lambda i: (i, 0)),
                  mod_spec,
                  _resident((1, D_MODEL)),
                  _layer_resident((D_MODEL, D_INP), layer)] + table_specs,
        out_specs=[pl.BlockSpec((tm, D_INP), lambda i: (i, 0)), pl.BlockSpec((tm, QKV_WIDTH), lambda i: (i, 0))],
        out_shape=[jax.ShapeDtypeStruct((n, D_INP), F32), jax.ShapeDtypeStruct((n, QKV_WIDTH), BF16)],
        compiler_params=_cparams("parallel"),
        name="in_proj",
    )(x, mod3, gamma, w, *tables)


def _gelu_tanh(x):
    return 0.5 * x * (1.0 + jnp.tanh(math.sqrt(2.0 / math.pi) * (x + 0.044715 * (x * x * x))))


def _out_proj_kernel(x_ref, mod_ref, g_ref, yssd_ref, oatt_ref, ys5_ref, u_ref, d_ref, wglu_ref, bglu_ref,
                     w1_ref, w2_ref, w3_ref, o_ref):
    y5 = ys5_ref[...] + d_ref[...] * u_ref[...]
    g = _gelu_tanh(y5)
    s5 = g * jax.nn.sigmoid(_dot(g.astype(BF16), wglu_ref[...]) + bglu_ref[...])
    mix = (_dot(yssd_ref[...], w1_ref[...]) + _dot(oatt_ref[...], w2_ref[...])
           + _dot(s5.astype(BF16), w3_ref[...]))
    gate = mod_ref[0][:, 2 * D_MODEL:3 * D_MODEL]
    o_ref[...] = x_ref[...] + gate * _rms(mix, g_ref[...])


def _out_proj(x, mod3, gamma, y_ssd, o_att, y_s5, proj, s5_d, w_glu, b_glu, w_out, *, layer, tm, mod_spec):
    n = x.shape[0]
    row = lambda i: (i, 0)
    return pl.pallas_call(
        _out_proj_kernel,
        grid=(n // tm,),
        in_specs=[pl.BlockSpec((tm, D_MODEL), row),
                  mod_spec,
                  _resident((1, D_MODEL)),
                  pl.BlockSpec((tm, SSD_WIDTH), row),
                  pl.BlockSpec((tm, ATT_WIDTH), row),
                  pl.BlockSpec((tm, S5_WIDTH), row),
                  pl.BlockSpec((tm, S5_WIDTH), lambda i: (i, C_U // S5_WIDTH)),
                  _resident((1, S5_WIDTH)),
                  _layer_resident((S5_WIDTH, S5_WIDTH), layer),
                  _resident((1, S5_WIDTH)),
                  _layer_resident((SSD_WIDTH, D_MODEL), layer, 0),
                  _layer_resident((ATT_WIDTH, D_MODEL), layer, SSD_WIDTH // ATT_WIDTH),
                  _layer_resident((S5_WIDTH, D_MODEL), layer, (SSD_WIDTH + ATT_WIDTH) // S5_WIDTH)],
        out_specs=pl.BlockSpec((tm, D_MODEL), row),
        out_shape=jax.ShapeDtypeStruct((n, D_MODEL), F32),
        compiler_params=_cparams("parallel"),
        name="out_proj",
    )(x, mod3, gamma, y_ssd, o_att, y_s5, proj, s5_d, w_glu, b_glu, w_out, w_out, w_out)


def _ffn_up_kernel(x_ref, mod_ref, gpre_ref, wg_ref, wu_ref, act_ref, h_scr):
    j = pl.program_id(1)

    def gated(h):
        return (_silu(_dot(h, wg_ref[...])) * _dot(h, wu_ref[...])).astype(BF16)

    @pl.when(j == 0)
    def _():
        mod = mod_ref[0]
        sh = mod[:, 3 * D_MODEL:4 * D_MODEL]
        sc = mod[:, 4 * D_MODEL:5 * D_MODEL]
        h = (_rms(x_ref[...], gpre_ref[...]) * (1.0 + sc) + sh).astype(BF16)
        h_scr[...] = h
        act_ref[...] = gated(h)

    @pl.when(j > 0)
    def _():
        act_ref[...] = gated(h_scr[...])


def _ffn_down_kernel(x_ref, mod_ref, gpost_ref, act_ref, wo_ref, o_ref):
    gate = mod_ref[0][:, 5 * D_MODEL:6 * D_MODEL]
    o_ref[...] = x_ref[...] + gate * _rms(_dot(act_ref[...], wo_ref[...]), gpost_ref[...])


def _ffn(x, mod3, g_pre, g_post, w_in, w_out, *, layer, tm_up, tf, tm_down, mod_spec_up, mod_spec_down):
    n = x.shape[0]
    nf = D_FF // tf
    act = pl.pallas_call(
        _ffn_up_kernel,
        grid=(n // tm_up, nf),
        in_specs=[pl.BlockSpec((tm_up, D_MODEL), lambda i, j: (i, 0)),
                  mod_spec_up,
                  _resident((1, D_MODEL)),
                  pl.BlockSpec((None, D_MODEL, tf), lambda i, j: (layer, 0, j)),
                  pl.BlockSpec((None, D_MODEL, tf), lambda i, j: (layer, 0, j + nf))],
        out_specs=pl.BlockSpec((tm_up, tf), lambda i, j: (i, j)),
        out_shape=jax.ShapeDtypeStruct((n, D_FF), BF16),
        scratch_shapes=[pltpu.VMEM((tm_up, D_MODEL), BF16)],
        compiler_params=_cparams("parallel", "arbitrary"),
        name="ffn_up",
    )(x, mod3, g_pre, w_in, w_in)
    return pl.pallas_call(
        _ffn_down_kernel,
        grid=(n // tm_down,),
        in_specs=[pl.BlockSpec((tm_down, D_MODEL), lambda i: (i, 0)),
                  mod_spec_down,
                  _resident((1, D_MODEL)),
                  pl.BlockSpec((tm_down, D_FF), lambda i: (i, 0)),
                  _layer_resident((D_FF, D_MODEL), layer)],
        out_specs=pl.BlockSpec((tm_down, D_MODEL), lambda i: (i, 0)),
        out_shape=jax.ShapeDtypeStruct((n, D_MODEL), F32),
        compiler_params=_cparams("parallel"),
        name="ffn_down",
    )(x, mod3, g_post, act, w_out)


def _attend(q, parts, sink_ref):
    rows = q.shape[0]
    srow = lax.broadcasted_iota(jnp.int32, (ATT_GROUP * rows, 1), 0)
    outs = [None] * ATT_HEADS
    for kv in range(ATT_KV):
        heads = range(kv * ATT_GROUP, (kv + 1) * ATT_GROUP)
        q3 = jnp.concatenate([q[:, h * HEAD_DIM:(h + 1) * HEAD_DIM] for h in heads], axis=0)
        sink = jnp.full((ATT_GROUP * rows, 1), sink_ref[heads[-1]] * LOG2E, F32)
        for i in range(ATT_GROUP - 2, -1, -1):
            sink = jnp.where(srow < (i + 1) * rows, sink_ref[heads[i]] * LOG2E, sink)
        lo = kv * HEAD_DIM
        scores = []
        m = sink
        for k, _, mask in parts:
            s = _dot_nt(q3, k[:, lo:lo + HEAD_DIM])
            if mask is not None:
                s = jnp.where(mask, s, NEG)
            scores.append(s)
            m = jnp.maximum(m, jnp.max(s, axis=-1, keepdims=True))
        vlo = (lo // LANES) * LANES
        v_first = lo == vlo
        v128 = jnp.concatenate([v[:, vlo:vlo + LANES] for _, v, _ in parts], axis=0)
        lane = lax.broadcasted_iota(jnp.int32, v128.shape, 1)
        keep = (lane < HEAD_DIM) if v_first else (lane >= HEAD_DIM)
        p = jnp.concatenate([jnp.exp2(s - m).astype(BF16) for s in scores], axis=1)
        acc = _dot(p, jnp.where(keep, v128, jnp.ones_like(v128)))
        ocol, dcol = (0, HEAD_DIM) if v_first else (HEAD_DIM, 0)
        o3 = acc[:, ocol:ocol + HEAD_DIM] / (acc[:, dcol:dcol + 1] + jnp.exp2(sink - m))
        for i, h in enumerate(heads):
            outs[h] = o3[i * rows:(i + 1) * rows]
    return jnp.concatenate(outs, axis=-1)


def _ctx_attn_kernel(sink_ref, q_ref, k_ref, v_ref, o_ref):
    o_ref[...] = _attend(q_ref[...], [(k_ref[...], v_ref[...], None)], sink_ref).astype(o_ref.dtype)


def _ctx_attention(qkv, sink, *, n_seq, seq_len):
    kcol = ATT_WIDTH // KV_WIDTH
    blk = ATT_BLOCK
    nb = seq_len // blk
    return pl.pallas_call(
        _ctx_attn_kernel,
        grid=(n_seq, nb),
        in_specs=[pl.BlockSpec(memory_space=pltpu.SMEM),
                  pl.BlockSpec((blk, ATT_WIDTH), lambda b, n: (b * nb + n, 0)),
                  pl.BlockSpec((seq_len, KV_WIDTH), lambda b, n: (b, kcol)),
                  pl.BlockSpec((seq_len, KV_WIDTH), lambda b, n: (b, kcol + 1))],
        out_specs=pl.BlockSpec((blk, ATT_WIDTH), lambda b, n: (b * nb + n, 0)),
        out_shape=jax.ShapeDtypeStruct((n_seq * seq_len, ATT_WIDTH), BF16),
        compiler_params=_cparams("parallel", "parallel"),
        name="ctx_attention",
    )(sink, qkv, qkv, qkv)


def _lat_attn_kernel(sink_ref, q_ref, kp_ref, kc_ref, kn_ref, vp_ref, vc_ref, vn_ref, ck_ref, cv_ref, o_ref):
    n = pl.program_id(1)
    nb = pl.num_programs(1)
    blk = ATT_BLOCK
    row = lax.broadcasted_iota(jnp.int32, (ATT_GROUP * blk, blk), 0) % blk
    col = lax.broadcasted_iota(jnp.int32, (ATT_GROUP * blk, blk), 1)
    mask_prev = jnp.logical_and(col >= row, n > 0)
    mask_next = jnp.logical_and(col <= row, n < nb - 1)
    parts = [(ck_ref[0], cv_ref[0], None),
             (kp_ref[...], vp_ref[...], mask_prev),
             (kc_ref[...], vc_ref[...], None),
             (kn_ref[...], vn_ref[...], mask_next)]
    o_ref[...] = _attend(q_ref[...], parts, sink_ref).astype(o_ref.dtype)


def _lat_attention(qkv, sink, ck, cv, *, n_seq, seq_len):
    blk = ATT_BLOCK
    nb = seq_len // blk

    def cur(b, n):
        return b * nb + n

    def prev(b, n):
        return b * nb + jnp.maximum(n - 1, 0)

    def nxt(b, n):
        return b * nb + jnp.minimum(n + 1, nb - 1)

    kcol = ATT_WIDTH // KV_WIDTH
    vcol = kcol + 1
    return pl.pallas_call(
        _lat_attn_kernel,
        grid=(n_seq, nb),
        in_specs=[pl.BlockSpec(memory_space=pltpu.SMEM),
                  pl.BlockSpec((blk, ATT_WIDTH), lambda b, n: (cur(b, n), 0)),
                  pl.BlockSpec((blk, KV_WIDTH), lambda b, n: (prev(b, n), kcol)),
                  pl.BlockSpec((blk, KV_WIDTH), lambda b, n: (cur(b, n), kcol)),
                  pl.BlockSpec((blk, KV_WIDTH), lambda b, n: (nxt(b, n), kcol)),
                  pl.BlockSpec((blk, KV_WIDTH), lambda b, n: (prev(b, n), vcol)),
                  pl.BlockSpec((blk, KV_WIDTH), lambda b, n: (cur(b, n), vcol)),
                  pl.BlockSpec((blk, KV_WIDTH), lambda b, n: (nxt(b, n), vcol)),
                  pl.BlockSpec((1,) + ck.shape[1:], lambda b, n: (b, 0, 0)),
                  pl.BlockSpec((1,) + cv.shape[1:], lambda b, n: (b, 0, 0))],
        out_specs=pl.BlockSpec((blk, ATT_WIDTH), lambda b, n: (cur(b, n), 0)),
        out_shape=jax.ShapeDtypeStruct((n_seq * seq_len, ATT_WIDTH), BF16),
        compiler_params=_cparams("parallel", "parallel"),
        name="lat_attention",
    )(sink, qkv, qkv, qkv, qkv, qkv, qkv, qkv, ck, cv)


def _rope_tables(seq_len):
    rows = seq_len // GRID_W
    row = np.repeat(np.arange(rows, dtype=np.float32), GRID_W)
    col = np.tile(np.arange(GRID_W, dtype=np.float32), rows)
    inv = np.float32(ROPE_BASE) ** (-np.arange(ROPE_PER_AXIS, dtype=np.float32) / np.float32(ROPE_PER_AXIS))
    ang = np.concatenate([row[:, None] * inv, col[:, None] * inv], axis=-1)
    cos, sin = np.cos(ang), np.sin(ang)
    cos128 = np.tile(cos, (1, 4))
    sin128 = np.tile(np.concatenate([-sin, sin], axis=-1), (1, 2))
    return jnp.asarray(cos128, F32), jnp.asarray(sin128, F32)


def _conv_kernel(xp_ref, xc_ref, xn_ref, bp_ref, bc_ref, bn_ref, cwx_ref, cbx_ref, cwb_ref, cbb_ref, o_ref,
                 xs_scr, bc_scr, *, seq_len):
    tr = xc_ref.shape[0]
    h = SSD_HALO
    pos = (pl.program_id(0) * tr + lax.broadcasted_iota(jnp.int32, (tr, 1), 0)) % seq_len
    first = pos == 0
    last = pos == seq_len - 1

    def conv(scr, prev_ref, cur_ref, next_ref, w_ref, b_ref):
        scr[0:h, :] = prev_ref[...]
        scr[h:h + tr, :] = cur_ref[...]
        scr[h + tr:h + tr + h, :] = next_ref[...]
        before = jnp.where(first, 0.0, scr[h - 1:h - 1 + tr, :])
        after = jnp.where(last, 0.0, scr[h + 1:h + 1 + tr, :])
        return _silu(before * w_ref[0:1, :] + scr[h:h + tr, :] * w_ref[1:2, :] + after * w_ref[2:3, :]
                     + b_ref[...])

    o_ref[:, 0:SSD_WIDTH] = conv(xs_scr, xp_ref, xc_ref, xn_ref, cwx_ref, cbx_ref)
    o_ref[:, SSD_WIDTH:] = conv(bc_scr, bp_ref, bc_ref, bn_ref, cwb_ref, cbb_ref)


def _ssd_conv(proj, consts, *, seq_len, tr):
    n = proj.shape[0]
    per8 = tr // SSD_HALO
    n_rows8 = n // SSD_HALO
    cwx, cbx, cwb, cbb = consts[:4]
    prev8 = lambda i: jnp.maximum(i * per8 - 1, 0)
    next8 = lambda i: jnp.minimum((i + 1) * per8, n_rows8 - 1)
    xcol, bcol = C_XS // SSD_WIDTH, C_BC // SSD_BC
    return pl.pallas_call(
        functools.partial(_conv_kernel, seq_len=seq_len),
        grid=(n // tr,),
        in_specs=[pl.BlockSpec((SSD_HALO, SSD_WIDTH), lambda i: (prev8(i), xcol)),
                  pl.BlockSpec((tr, SSD_WIDTH), lambda i: (i, xcol)),
                  pl.BlockSpec((SSD_HALO, SSD_WIDTH), lambda i: (next8(i), xcol)),
                  pl.BlockSpec((SSD_HALO, SSD_BC), lambda i: (prev8(i), bcol)),
                  pl.BlockSpec((tr, SSD_BC), lambda i: (i, bcol)),
                  pl.BlockSpec((SSD_HALO, SSD_BC), lambda i: (next8(i), bcol)),
                  _resident(cwx.shape), _resident(cbx.shape), _resident(cwb.shape), _resident(cbb.shape)],
        out_specs=pl.BlockSpec((tr, SSD_WIDTH + SSD_BC), lambda i: (i, 0)),
        out_shape=jax.ShapeDtypeStruct((n, SSD_WIDTH + SSD_BC), F32),
        scratch_shapes=[pltpu.VMEM((tr + 2 * SSD_HALO, SSD_WIDTH), F32),
                        pltpu.VMEM((tr + 2 * SSD_HALO, SSD_BC), F32)],
        compiler_params=_cparams("parallel"),
        name="ssd_conv",
    )(proj, proj, proj, proj, proj, proj, cwx, cbx, cwb, cbb)


def _softplus(x):
    return jnp.maximum(x, 0.0) + jnp.log1p(jnp.exp(-jnp.abs(x)))


def _ssd_bwd_kernel(xs_ref, bc_ref, dt_ref, h0_ref, dtb_ref, alog_ref, tri_ref, exp_ref,
                    hstart_ref, hfin_ref, h_scr):
    i = pl.program_id(1)
    q = SSD_CHUNK
    half = SSD_WIDTH // SSD_GROUPS
    sel_b = exp_ref[1]
    neg_a = -LOG2E * jnp.exp(alog_ref[...])

    @pl.when(i == 0)
    def _():
        h_scr[...] = h0_ref[0]

    hb = h_scr[...]
    for j in reversed(range(xs_ref.shape[0] // q)):
        rows = slice(j * q, (j + 1) * q)
        hstart_ref[0, j] = hb
        bm = bc_ref[rows, 0:SSD_GROUPS * SSD_STATE].astype(BF16)
        dtv = _softplus(dt_ref[rows, :] + dtb_ref[...])
        rev = _dot_sel_lhs(tri_ref[1], dtv * neg_a, 3)
        dt_w = _dot_sel_rhs(dtv, sel_b, 2)
        to_end = _dot_sel_rhs(jnp.exp2(rev[0:1, :] - rev), sel_b, 2)
        decay = _dot_sel_rhs(jnp.exp2(rev[0:SSD_HALO, :]), sel_b, 2)[0:1, :]
        xw = (xs_ref[rows, :] * dt_w * to_end).astype(BF16)
        upd = jnp.concatenate(
            [_dot_tn(bm[:, g * SSD_STATE:(g + 1) * SSD_STATE], xw[:, g * half:(g + 1) * half])
             for g in range(SSD_GROUPS)], axis=-1)
        hb = hb * decay + upd
    h_scr[...] = hb

    @pl.when(i == pl.num_programs(1) - 1)
    def _():
        hfin_ref[0] = hb.T


def _ssd_fwd_chunk(hf, hb_start, gz, xs, bc, dt_raw, dtb_ref, alog_ref, tri_ref, exp_ref, dvec_ref, nw_ref):
    q = SSD_CHUNK
    nh = SSD_HEADS
    gs = SSD_GROUPS * SSD_STATE
    bm = bc[:, 0:gs].astype(BF16)
    cm = bc[:, gs:2 * gs].astype(BF16)
    xs_bf = xs.astype(BF16)

    dtv = _softplus(dt_raw + dtb_ref[...])
    dta = dtv * (-LOG2E * jnp.exp(alog_ref[...]))
    lower, upper = tri_ref[0], tri_ref[1]
    cum = _dot_sel_lhs(lower, dta, 3)
    rev = _dot_sel_lhs(upper, dta, 3)
    dt_t = dtv.T
    dta_t = dta.T
    cum_t = _dot_sel_rhs(dta_t, upper, 3)
    rev_t = _dot_sel_rhs(dta_t, lower, 3)

    sel_f, sel_b = exp_ref[0], exp_ref[1]
    dtf_w = _dot_sel_rhs(dtv, sel_f, 2)
    dtb_w = _dot_sel_rhs(dtv, sel_b, 2)
    ecum_w = _dot_sel_rhs(jnp.exp2(cum), sel_f, 2)
    erev_w = _dot_sel_rhs(jnp.exp2(rev), sel_b, 2)
    toend_w = _dot_sel_rhs(jnp.exp2(cum[q - 1:q, :] - cum), sel_f, 2)

    row = lax.broadcasted_iota(jnp.int32, (q, q), 0)
    col = lax.broadcasted_iota(jnp.int32, (q, q), 1)
    below = row > col
    log_dt_t = jnp.log2(dt_t)
    src_f = log_dt_t - cum_t
    src_b = log_dt_t - rev_t
    gmat = [_dot_nt(cm[:, g * SSD_STATE:(g + 1) * SSD_STATE], bm[:, g * SSD_STATE:(g + 1) * SSD_STATE])
            for g in range(SSD_GROUPS)]
    per_group = nh // SSD_GROUPS
    half = SSD_WIDTH // SSD_GROUPS
    ys = []
    for h in range(nh):
        e = jnp.where(below, cum[:, h:h + 1] + src_f[h:h + 1, :],
                      rev[:, nh + h:nh + h + 1] + src_b[nh + h:nh + h + 1, :])
        m = (gmat[h // per_group] * jnp.exp2(e)).astype(BF16)
        ys.append(_dot(m, xs_bf[:, h * SSD_HEAD_DIM:(h + 1) * SSD_HEAD_DIM]))
    cb = cm.astype(F32) * bm.astype(F32)
    g_diag = jnp.concatenate(
        [jnp.broadcast_to(jnp.sum(cb[:, g * SSD_STATE:(g + 1) * SSD_STATE], axis=-1, keepdims=True), (q, half))
         for g in range(SSD_GROUPS)], axis=-1)
    y = jnp.concatenate(ys, axis=-1) + g_diag * dtf_w * xs

    hf_bf = hf.astype(BF16)
    hb_bf = hb_start.astype(BF16)
    off_f = jnp.concatenate([_dot(cm[:, g * SSD_STATE:(g + 1) * SSD_STATE], hf_bf[:, g * half:(g + 1) * half])
                             for g in range(SSD_GROUPS)], axis=-1)
    off_b = jnp.concatenate([_dot(cm[:, g * SSD_STATE:(g + 1) * SSD_STATE], hb_bf[:, g * half:(g + 1) * half])
                             for g in range(SSD_GROUPS)], axis=-1)
    y = y + off_f * ecum_w + off_b * erev_w + dvec_ref[...] * xs
    y = _rms(y * gz, nw_ref[...])

    xw = (xs * dtf_w * toend_w).astype(BF16)
    upd = jnp.concatenate(
        [_dot_tn(bm[:, g * SSD_STATE:(g + 1) * SSD_STATE], xw[:, g * half:(g + 1) * half])
         for g in range(SSD_GROUPS)], axis=-1)
    return y, hf * ecum_w[q - 1:q, :] + upd


def _ssd_fwd_kernel(gz_ref, xs_ref, bc_ref, dt_ref, hb_ref, h0_ref,
                    dtb_ref, alog_ref, tri_ref, exp_ref, dvec_ref, nw_ref, y_ref, hfin_ref, h_scr):
    c = pl.program_id(1)
    q = SSD_CHUNK

    @pl.when(c == 0)
    def _():
        h_scr[...] = h0_ref[0]

    hf = h_scr[...]
    for j in range(xs_ref.shape[0] // q):
        rows = slice(j * q, (j + 1) * q)
        y, hf = _ssd_fwd_chunk(hf, hb_ref[0, j], gz_ref[rows, :], xs_ref[rows, :], bc_ref[rows, :], dt_ref[rows, :],
                               dtb_ref, alog_ref, tri_ref, exp_ref, dvec_ref, nw_ref)
        y_ref[rows, :] = y.astype(y_ref.dtype)
    h_scr[...] = hf

    @pl.when(c == pl.num_programs(1) - 1)
    def _():
        hfin_ref[0] = hf.T


def _ssd_mixer(proj, h0_f, h0_b, consts, *, n_seq, seq_len, conv_rows):
    q = SSD_CHUNK
    nc = seq_len // q
    cps = max(c for c in range(1, SSD_CHUNKS_PER_STEP + 1) if nc % c == 0)
    ns = nc // cps
    rows = cps * q
    (dt_bias, a_log, tri, expand, dvec, nw) = consts[4:]
    xbc = _ssd_conv(proj, consts, seq_len=seq_len, tr=conv_rows)

    def specs(chunk_of):
        def cur(b, i):
            return b * ns + chunk_of(i)

        return [pl.BlockSpec((rows, SSD_WIDTH), lambda b, i: (cur(b, i), 0)),
                pl.BlockSpec((rows, SSD_BC), lambda b, i: (cur(b, i), SSD_WIDTH // SSD_BC)),
                pl.BlockSpec((rows, DT_PAD), lambda b, i: (cur(b, i), C_DT // DT_PAD))], cur

    state_spec = pl.BlockSpec((1, SSD_STATE, SSD_WIDTH), lambda b, i: (b, 0, 0))
    final_spec = pl.BlockSpec((1, SSD_WIDTH, SSD_STATE), lambda b, i: (b, 0, 0))
    const_specs = [_resident(dt_bias.shape), _resident(a_log.shape), _resident(tri.shape),
                   _resident(expand.shape)]
    scratch = [pltpu.VMEM((SSD_STATE, SSD_WIDTH), F32)]

    data_specs, _ = specs(lambda i: ns - 1 - i)
    hb_start, hb_fin = pl.pallas_call(
        _ssd_bwd_kernel,
        grid=(n_seq, ns),
        in_specs=data_specs + [state_spec] + const_specs,
        out_specs=[pl.BlockSpec((1, cps, SSD_STATE, SSD_WIDTH), lambda b, i: (b, ns - 1 - i, 0, 0)), final_spec],
        out_shape=[jax.ShapeDtypeStruct((n_seq, nc, SSD_STATE, SSD_WIDTH), F32),
                   jax.ShapeDtypeStruct((n_seq, SSD_WIDTH, SSD_STATE), F32)],
        scratch_shapes=scratch,
        compiler_params=_cparams("parallel", "arbitrary"),
        name="ssd_backward_states",
    )(xbc, xbc, proj, h0_b, dt_bias, a_log, tri, expand)

    data_specs, cur = specs(lambda i: i)
    y, hf_fin = pl.pallas_call(
        _ssd_fwd_kernel,
        grid=(n_seq, ns),
        in_specs=([pl.BlockSpec((rows, SSD_WIDTH), lambda b, i: (cur(b, i), C_Z // SSD_WIDTH))] + data_specs
                  + [pl.BlockSpec((1, cps, SSD_STATE, SSD_WIDTH), lambda b, i: (b, i, 0, 0)), state_spec]
                  + const_specs + [_resident(dvec.shape), _resident(nw.shape)]),
        out_specs=[pl.BlockSpec((rows, SSD_WIDTH), lambda b, i: (b * ns + i, 0)), final_spec],
        out_shape=[jax.ShapeDtypeStruct((n_seq * seq_len, SSD_WIDTH), BF16),
                   jax.ShapeDtypeStruct((n_seq, SSD_WIDTH, SSD_STATE), F32)],
        scratch_shapes=scratch,
        compiler_params=_cparams("parallel", "arbitrary"),
        name="ssd_forward",
    )(proj, xbc, xbc, proj, hb_start, h0_f, dt_bias, a_log, tri, expand, dvec, nw)
    return y, hf_fin, hb_fin


def _ssd_consts(l, ssd_conv_w, ssd_conv_b, ssd_dt_bias, ssd_A_log, ssd_D, ssd_norm):
    q = SSD_CHUNK
    cw, cb = ssd_conv_w[l], ssd_conv_b[l]
    pad = DT_PAD - 2 * SSD_HEADS
    dt_bias = jnp.pad(ssd_dt_bias[l].reshape(1, -1), ((0, 0), (0, pad)))
    a_log = jnp.pad(ssd_A_log[l].reshape(1, -1), ((0, 0), (0, pad)))
    r = np.arange(q)
    tri = jnp.asarray(np.stack([r[None, :] <= r[:, None], r[None, :] >= r[:, None]]), BF16)
    head_of_col = np.arange(SSD_WIDTH) // SSD_HEAD_DIM
    j = np.arange(DT_PAD)
    expand = jnp.asarray(np.stack([j[:, None] == head_of_col[None, :],
                                   j[:, None] == head_of_col[None, :] + SSD_HEADS]), BF16)
    dvec = jnp.repeat(ssd_D[l], SSD_HEAD_DIM).reshape(1, SSD_WIDTH)
    return (cw[:, :SSD_WIDTH], cb[:SSD_WIDTH].reshape(1, -1), cw[:, SSD_WIDTH:], cb[SSD_WIDTH:].reshape(1, -1),
            dt_bias, a_log, tri, expand, dvec, ssd_norm[l].reshape(1, -1))


def _ssd_state_in(s):
    b = s.shape[0]
    return jnp.transpose(s, (0, 3, 1, 2)).reshape(b, SSD_STATE, SSD_WIDTH)


def _ssd_state_out(s):
    return s.reshape(s.shape[0], SSD_HEADS, SSD_HEAD_DIM, SSD_STATE)


def _s5_table_kernel(pw_ref, ct_ref, bb_ref, bbt_ref, sel_ref, tile_ref, rhs_ref, woff_ref):
    t = S5_T
    c = S5_CH
    tile = tile_ref[...]
    spread = lambda x, sel: _dot_sel_rhs(x, sel, 3)
    krows, offs, cols = [], [], []
    for d in range(2):
        pw_re, pw_im = pw_ref[0, 2 * d], pw_ref[0, 2 * d + 1]
        c_re = spread(ct_ref[0, :, (2 * d) * c:(2 * d + 1) * c], tile)
        c_im = spread(ct_ref[0, :, (2 * d + 1) * c:(2 * d + 2) * c], tile)
        b_re = spread(bb_ref[0, :, (2 * d) * c:(2 * d + 1) * c], tile)
        b_im = spread(bb_ref[0, :, (2 * d + 1) * c:(2 * d + 2) * c], tile)
        bt_re = bbt_ref[0, (2 * d) * c:(2 * d + 1) * c, :]
        bt_im = bbt_ref[0, (2 * d + 1) * c:(2 * d + 2) * c, :]
        up = (spread(pw_re, sel_ref[0]), spread(pw_im, sel_ref[0]))
        down = (spread(pw_re, sel_ref[1]), spread(pw_im, sel_ref[1]))
        (q_re, q_im), state = ((up, down), (down, up))[d]
        g_re = q_re * c_re - q_im * c_im
        g_im = q_re * c_im + q_im * c_re
        krows.append(_dot_hi(bt_re, g_re) - _dot_hi(bt_im, g_im))
        lb_re, lb_im = pw_re[:, 1:2], pw_im[:, 1:2]
        r_re = q_re * lb_re - q_im * lb_im
        r_im = q_re * lb_im + q_im * lb_re
        offs.append((r_re * c_re - r_im * c_im, -(r_re * c_im + r_im * c_re)))
        s_re, s_im = state
        cols.append(((s_re * b_re - s_im * b_im).T, (s_re * b_im + s_im * b_re).T))
    k_f, k_b = krows
    for s in range(t):
        r = t - 1 - s
        f = k_f if s == 0 else jnp.concatenate([jnp.zeros((c, c * s), F32), k_f[:, :S5_TW - c * s]], axis=1)
        b = k_b if r == 0 else jnp.concatenate([k_b[:, c * r:], jnp.zeros((c, c * r), F32)], axis=1)
        rhs_ref[0, s * c:(s + 1) * c, 0:S5_TW] = (f + b).astype(BF16)
    rhs_ref[0, :, S5_TW:] = jnp.concatenate([cols[0][0], cols[1][0], cols[0][1], cols[1][1]], axis=1).astype(BF16)
    woff_ref[0] = jnp.concatenate([offs[0][0], offs[1][0], offs[0][1], offs[1][1]], axis=0).astype(BF16)


def _s5_tables(s5_A_re, s5_A_im, s5_log_dt, s5_B_re, s5_B_im, s5_C_re, s5_C_im):
    t = S5_T
    n = N_LAYERS * S5_GROUPS
    step = jnp.exp(s5_log_dt)[..., None]
    k = jnp.arange(LANES, dtype=F32)
    keep = k <= t
    kk = jnp.where(keep, k, 0.0)
    mag = jnp.exp(kk * (s5_A_re * step)[..., None])
    ang = kk * (s5_A_im * step)[..., None]
    pw_re = jnp.where(keep, mag * jnp.cos(ang), 0.0)
    pw_im = jnp.where(keep, mag * jnp.sin(ang), 0.0)
    lb_re, lb_im = pw_re[..., 1], pw_im[..., 1]
    den = s5_A_re * s5_A_re + s5_A_im * s5_A_im
    r_re = ((lb_re - 1.0) * s5_A_re + lb_im * s5_A_im) / den
    r_im = (lb_im * s5_A_re - (lb_re - 1.0) * s5_A_im) / den
    b_re, b_im = s5_B_re[:, None], s5_B_im[:, None]
    bb_re = r_re[..., None] * b_re - r_im[..., None] * b_im
    bb_im = r_re[..., None] * b_im + r_im[..., None] * b_re
    c_re = jnp.swapaxes(s5_C_re, -1, -2)
    c_im = jnp.swapaxes(s5_C_im, -1, -2)

    def pack(re, im, axis):
        return jnp.concatenate([re[:, 0], im[:, 0], re[:, 1], im[:, 1]], axis=axis)

    pw = jnp.stack([pw_re[:, 0], pw_im[:, 0], pw_re[:, 1], pw_im[:, 1]], axis=2).reshape(n, 4, S5_P, LANES)
    ct = pack(c_re, c_im, -1).reshape(n, S5_P, 4 * S5_CH)
    bb = pack(bb_re, bb_im, -1)
    bbt = jnp.swapaxes(bb, -1, -2).reshape(n, 4 * S5_CH, S5_P)
    bb = bb.reshape(n, S5_P, 4 * S5_CH)
    lam_t = jnp.concatenate([pw_re[:, 0, :, :, t], pw_re[:, 1, :, :, t], pw_im[:, 0, :, :, t],
                             pw_im[:, 1, :, :, t]], axis=-1).reshape(n, 1, S5_ST)
    jt = np.arange(S5_TW) // S5_CH
    lane_k = np.arange(LANES)[:, None]
    sel = jnp.asarray(np.stack([lane_k == f[None, :] for f in (jt, t - 1 - jt)]), BF16)
    tile = jnp.asarray(np.arange(S5_CH)[:, None] == (np.arange(S5_TW) % S5_CH)[None, :], BF16)
    grp = lambda shape: pl.BlockSpec((1,) + shape, lambda i: (i,) + (0,) * len(shape))
    rhs, woff = pl.pallas_call(
        _s5_table_kernel,
        grid=(n,),
        in_specs=[grp((4, S5_P, LANES)), grp((S5_P, 4 * S5_CH)), grp((S5_P, 4 * S5_CH)), grp((4 * S5_CH, S5_P)),
                  _resident(sel.shape), _resident(tile.shape)],
        out_specs=[grp((S5_TW, S5_TW + S5_ST)), grp((S5_ST, S5_TW))],
        out_shape=[jax.ShapeDtypeStruct((n, S5_TW, S5_TW + S5_ST), BF16),
                   jax.ShapeDtypeStruct((n, S5_ST, S5_TW), BF16)],
        compiler_params=_cparams("parallel"),
        name="s5_tables",
    )(pw, ct, bb, bbt, sel, tile)
    return rhs, woff, lam_t


def _unit_transpose(vs, masks, axis, unit):
    size = vs[0].shape[axis]
    cur = list(vs)
    for d in (1, 2, 4):
        nxt = list(cur)
        for a in range(S5_OCT):
            if a & d:
                continue
            b = a | d
            nxt[a] = jnp.where(masks[d], cur[a], pltpu.roll(cur[b], unit * d, axis=axis))
            nxt[b] = jnp.where(masks[d], pltpu.roll(cur[a], size - unit * d, axis=axis), cur[b])
        cur = nxt
    return cur


def _s5_kernel(u_ref, rhs_ref, woff_ref, lam_ref, h0_ref, y_ref, hfin_ref,
               dst_scr, y_scr, sre_scr, sim_scr, hfre_scr, hfim_scr, hbre_scr, hbim_scr, *, nc, rb):
    t = S5_T
    rows = dst_scr.shape[1]
    sb = rows // nc
    tiles = t // S5_OCT
    nr = rb // S5_OCT
    tile_shape = (nr, S5_OCT, LANES)
    seg = lax.broadcasted_iota(jnp.int32, tile_shape, 2) // S5_CH
    sub = lax.broadcasted_iota(jnp.int32, tile_shape, 1)
    lane_masks = {d: (seg & d) == 0 for d in (1, 2, 4)}
    sub_masks = {d: (sub & d) == 0 for d in (1, 2, 4)}

    def tile_rows(i, r, tq):
        return pl.ds((i * rb + r) * tiles + tq, nr, stride=S5_OCT * tiles)

    def gather(i, carry):
        r0 = pl.multiple_of(i * rb, rb)
        for tq in range(tiles):
            by_chunk = [u_ref[tile_rows(i, r, tq)] for r in range(S5_OCT)]
            by_time = _unit_transpose(by_chunk, sub_masks, 1, 1)
            for gl, v in enumerate(_unit_transpose(by_time, lane_masks, 2, S5_CH)):
                dst_scr[gl, pl.ds(r0, rb), tq * LANES:(tq + 1) * LANES] = v.reshape(rb, LANES).astype(BF16)
        return carry

    lax.fori_loop(0, rows // rb, gather, 0)

    for gl in range(S5_OCT):
        z = _dot(dst_scr[gl], rhs_ref[gl])
        y_scr[gl] = z[:, 0:S5_TW]
        sre_scr[gl] = z[:, S5_TW:S5_TW + 2 * S5_P]
        sim_scr[gl] = z[:, S5_TW + 2 * S5_P:]

    fwd_lane = lax.broadcasted_iota(jnp.int32, (sb, 2 * S5_P), 1) < S5_P
    a_re = [lam_ref[gl][:, 0:2 * S5_P] for gl in range(S5_OCT)]
    a_im = [lam_ref[gl][:, 2 * S5_P:] for gl in range(S5_OCT)]

    def step(i, carry):
        fi = pl.ds(i, sb, stride=nc)
        bj = pl.ds(nc - 1 - i, sb, stride=nc)
        out = []
        for gl in range(S5_OCT):
            h_re, h_im = carry[2 * gl], carry[2 * gl + 1]
            hfre_scr[gl, fi, :] = h_re
            hfim_scr[gl, fi, :] = h_im
            hbre_scr[gl, bj, :] = h_re
            hbim_scr[gl, bj, :] = h_im
            s_re = jnp.where(fwd_lane, sre_scr[gl, fi, :], sre_scr[gl, bj, :])
            s_im = jnp.where(fwd_lane, sim_scr[gl, fi, :], sim_scr[gl, bj, :])
            out.append(a_re[gl] * h_re - a_im[gl] * h_im + s_re)
            out.append(a_re[gl] * h_im + a_im[gl] * h_re + s_im)
        return tuple(out)

    init = []
    for gl in range(S5_OCT):
        h0 = h0_ref[gl, 0]
        init += [h0[:, 0:2 * S5_P], h0[:, 2 * S5_P:]]
    fin = lax.fori_loop(0, nc, step, tuple(init))

    all_fwd = lax.broadcasted_iota(jnp.int32, (rows, 2 * S5_P), 1) < S5_P
    for gl in range(S5_OCT):
        hfin_ref[gl, 0] = jnp.concatenate([fin[2 * gl], fin[2 * gl + 1]], axis=-1)
        hin = jnp.concatenate([jnp.where(all_fwd, hfre_scr[gl], hbre_scr[gl]),
                               jnp.where(all_fwd, hfim_scr[gl], hbim_scr[gl])], axis=-1).astype(BF16)
        y_scr[gl] += _dot(hin, woff_ref[gl])

    def scatter(i, carry):
        r0 = pl.multiple_of(i * rb, rb)
        for tq in range(tiles):
            by_group = [y_scr[gl, pl.ds(r0, rb), tq * LANES:(tq + 1) * LANES].reshape(tile_shape)
                        for gl in range(S5_OCT)]
            by_time = _unit_transpose(by_group, lane_masks, 2, S5_CH)
            for r, v in enumerate(_unit_transpose(by_time, sub_masks, 1, 1)):
                y_ref[tile_rows(i, r, tq)] = v
        return carry

    lax.fori_loop(0, rows // rb, scatter, 0)


def _s5_mixer(proj, tables, layer, h0, *, n_seq, seq_len):
    rhs, woff, lam_t = tables
    t = S5_T
    nc = seq_len // t
    sb = max(1, min(n_seq, S5_MAX_ROWS // nc))
    assert n_seq % sb == 0
    nbs = n_seq // sb
    rows = sb * nc
    rb = min(rows, 32)
    noct = S5_GROUPS // S5_OCT
    ucol = C_U // LANES
    tab = lambda shape: pl.BlockSpec((S5_OCT,) + shape, lambda j, s: (layer * noct + j, 0, 0))
    state = pl.BlockSpec((S5_OCT, 1, sb, S5_ST), lambda j, s: (j, s, 0, 0))
    rows_scr = lambda dt: pltpu.VMEM((S5_OCT, rows, 2 * S5_P), dt)
    n_tok = n_seq * seq_len
    tok_block = (sb * seq_len // S5_OCT, S5_OCT, LANES)
    y, hfin = pl.pallas_call(
        functools.partial(_s5_kernel, nc=nc, rb=rb),
        grid=(noct, nbs),
        in_specs=[pl.BlockSpec(tok_block, lambda j, s: (s, 0, ucol + j)),
                  tab(rhs.shape[1:]), tab(woff.shape[1:]), tab(lam_t.shape[1:]), state],
        out_specs=[pl.BlockSpec(tok_block, lambda j, s: (s, 0, j)), state],
        out_shape=[jax.ShapeDtypeStruct((n_tok // S5_OCT, S5_OCT, S5_WIDTH), F32),
                   jax.ShapeDtypeStruct((S5_GROUPS, nbs, sb, S5_ST), F32)],
        scratch_shapes=[pltpu.VMEM((S5_OCT, rows, S5_TW), BF16), pltpu.VMEM((S5_OCT, rows, S5_TW), F32),
                        rows_scr(F32), rows_scr(F32), rows_scr(F32), rows_scr(F32), rows_scr(F32), rows_scr(F32)],
        compiler_params=_cparams("parallel", "parallel"),
        name="s5_mixer",
    )(proj.reshape(n_tok // S5_OCT, S5_OCT, D_INP), rhs, woff, lam_t, h0.reshape(S5_GROUPS, nbs, sb, S5_ST))
    return y.reshape(n_tok, S5_WIDTH), hfin.reshape(S5_GROUPS, n_seq, S5_ST)


def _s5_state_in(re, im):
    b = re.shape[0]
    x = jnp.stack([re, im], axis=1)
    return jnp.transpose(x, (3, 0, 1, 2, 4)).reshape(S5_GROUPS, b, S5_ST)


def _s5_state_out(h):
    n_seq = h.shape[1]
    x = h.reshape(S5_GROUPS, n_seq, 2, 2, S5_P)
    x = jnp.transpose(x, (2, 1, 3, 0, 4))
    return x[0], x[1]


def kernel(x_prompt, x_sample, c, cache_k, cache_v, state_ssd, state_s5_re, state_s5_im, c_ctx, w_ada, b_ada, norm_mix_pre, norm_mix_post, norm_ffn_pre, norm_ffn_post, w_in, w_out, ssd_conv_w, ssd_conv_b, ssd_dt_bias, ssd_A_log, ssd_D, ssd_norm, attn_sink, s5_A_re, s5_A_im, s5_log_dt, s5_B_re, s5_B_im, s5_C_re, s5_C_im, s5_D, s5_w_glu, s5_b_glu, w_ffn_in, w_ffn_out):
    nb_ctx, len_ctx, _ = x_prompt.shape
    nb_lat, len_lat, _ = x_sample.shape
    n_ctx = nb_ctx * len_ctx
    n_lat = nb_lat * len_lat
    tm = 512
    mod_rows = -(-(1 + nb_lat) // 8) * 8
    cvecs = jnp.concatenate([c_ctx[None, :], c, jnp.zeros((mod_rows - 1 - nb_lat, D_MODEL), F32)], axis=0)
    mods = _ada(cvecs, w_ada, b_ada)
    mod_geom = ((n_ctx, 0), (len_lat, 1))
    token_sets = tuple(dict(mod_spec=_mod_spec(tm, *g)) for g in mod_geom)
    tm_up = min(FFN_UP_ROWS, n_ctx, len_lat)
    rope_args = (dict(), dict(rope_tables=_rope_tables(len_lat), seq_len=len_lat))

    s5_tab = _s5_tables(s5_A_re, s5_A_im, s5_log_dt, s5_B_re, s5_B_im, s5_C_re, s5_C_im)
    w_in_p = jnp.concatenate(
        [w_in[:, :, O_Q:O_K], w_in[:, :, O_Z:O_XS], w_in[:, :, O_XS:O_BC], w_in[:, :, O_BC:O_DT],
         w_in[:, :, O_K:O_V], w_in[:, :, O_V:O_U], w_in[:, :, O_U:O_END], w_in[:, :, O_DT:O_Q],
         jnp.zeros((N_LAYERS, D_MODEL, DT_PAD - 2 * SSD_HEADS), F32)], axis=2).astype(BF16)
    w_out_b = w_out.astype(BF16)
    w_glu_b = s5_w_glu.astype(BF16)
    w_ffn_in_b = w_ffn_in.astype(BF16)
    w_ffn_out_b = w_ffn_out.astype(BF16)
    zeros_ssd = jnp.zeros((nb_ctx, SSD_STATE, SSD_WIDTH), F32)
    zeros_s5 = jnp.zeros((S5_GROUPS, nb_ctx, S5_ST), F32)

    xs = [x_prompt.reshape(n_ctx, D_MODEL), x_sample.reshape(n_lat, D_MODEL)]
    new_k, new_v, new_ssd, new_s5_re, new_s5_im = [], [], [], [], []
    for l in range(N_LAYERS):
        mod3 = mods[l].reshape(mod_rows, 1, 6 * D_MODEL)
        proj, qkv = zip(*[_in_proj(x, mod3, norm_mix_pre[l].reshape(1, -1), w_in_p, layer=l, tm=tm, **ts, **ra)
                          for x, ts, ra in zip(xs, token_sets, rope_args)])

        sink = attn_sink[l]
        ck = cache_k[:, l].reshape(nb_lat, -1, KV_WIDTH).astype(BF16)
        cv = cache_v[:, l].reshape(nb_lat, -1, KV_WIDTH).astype(BF16)
        o_att = [_ctx_attention(qkv[0], sink, n_seq=nb_ctx, seq_len=len_ctx),
                 _lat_attention(qkv[1], sink, ck, cv, n_seq=nb_lat, seq_len=len_lat)]

        consts = _ssd_consts(l, ssd_conv_w, ssd_conv_b, ssd_dt_bias, ssd_A_log, ssd_D, ssd_norm)
        y_ctx, hf_ctx, hb_ctx = _ssd_mixer(proj[0], zeros_ssd, zeros_ssd, consts, n_seq=nb_ctx, seq_len=len_ctx,
                                           conv_rows=tm)
        y_lat, _, _ = _ssd_mixer(proj[1], _ssd_state_in(state_ssd[:, l, 0]), _ssd_state_in(state_ssd[:, l, 1]),
                                 consts, n_seq=nb_lat, seq_len=len_lat, conv_rows=tm)
        y_ssd = [y_ctx, y_lat]

        s5_ctx, s5_fin = _s5_mixer(proj[0], s5_tab, l, zeros_s5, n_seq=nb_ctx, seq_len=len_ctx)
        s5_lat, _ = _s5_mixer(proj[1], s5_tab, l, _s5_state_in(state_s5_re[:, l], state_s5_im[:, l]),
                              n_seq=nb_lat, seq_len=len_lat)
        y_s5 = [s5_ctx, s5_lat]

        xs = [_out_proj(x, mod3, norm_mix_post[l].reshape(1, -1), y_ssd[i], o_att[i], y_s5[i], proj[i],
                        s5_D[l].reshape(1, -1), w_glu_b, s5_b_glu[l].reshape(1, -1), w_out_b,
                        layer=l, tm=tm, **ts)
              for i, (x, ts) in enumerate(zip(xs, token_sets))]
        xs = [_ffn(x, mod3, norm_ffn_pre[l].reshape(1, -1), norm_ffn_post[l].reshape(1, -1),
                   w_ffn_in_b, w_ffn_out_b, layer=l, tm_up=tm_up, tf=FFN_UP_COLS, tm_down=tm,
                   mod_spec_up=_mod_spec(tm_up, *g), mod_spec_down=_mod_spec(tm, *g))
              for x, g in zip(xs, mod_geom)]

        new_k.append(proj[0][:, C_K:C_K + KV_WIDTH].reshape(nb_ctx, len_ctx, ATT_KV, HEAD_DIM))
        new_v.append(proj[0][:, C_V:C_V + KV_WIDTH].reshape(nb_ctx, len_ctx, ATT_KV, HEAD_DIM))
        new_ssd.append(jnp.stack([_ssd_state_out(hf_ctx), _ssd_state_out(hb_ctx)], axis=1))
        re, im = _s5_state_out(s5_fin)
        new_s5_re.append(re)
        new_s5_im.append(im)

    y_prompt = xs[0].reshape(nb_ctx, len_ctx, D_MODEL)
    y_sample = xs[1].reshape(nb_lat, len_lat, D_MODEL)
    return (y_prompt, y_sample, jnp.stack(new_k, axis=1), jnp.stack(new_v, axis=1), jnp.stack(new_ssd, axis=1),
            jnp.stack(new_s5_re, axis=1), jnp.stack(new_s5_im, axis=1))
```

```python
import functools
import math

import jax
import jax.numpy as jnp
import numpy as np
from jax import lax
from jax.experimental import pallas as pl
from jax.experimental.pallas import tpu as pltpu

F32 = jnp.float32
BF16 = jnp.bfloat16
HIGHEST = lax.Precision.HIGHEST

D_MODEL = 2048
N_LAYERS = 2
D_FF = 5632
GRID_W = 64
EPS = 1e-6
LANES = 128

SSD_HEADS = 12
SSD_HEAD_DIM = 64
SSD_WIDTH = SSD_HEADS * SSD_HEAD_DIM
SSD_GROUPS = 2
SSD_STATE = 64
SSD_BC = 2 * SSD_GROUPS * SSD_STATE
SSD_CHUNK = 128
SSD_CHUNKS_PER_STEP = 4
SSD_HALO = 8
ATT_HEADS = 12
ATT_KV = 4
ATT_GROUP = ATT_HEADS // ATT_KV
HEAD_DIM = 64
ATT_WIDTH = ATT_HEADS * HEAD_DIM
KV_WIDTH = ATT_KV * HEAD_DIM
ATT_BLOCK = 128
QV_WIDTH = ATT_WIDTH + KV_WIDTH
LOG2E = math.log2(math.e)
Q_SCALE = HEAD_DIM ** -0.5 * LOG2E
ROPE_PER_AXIS = HEAD_DIM // 4
ROPE_BASE = 10000.0
S5_GROUPS = 32
S5_CH = 16
S5_WIDTH = S5_GROUPS * S5_CH
S5_P = 64
S5_T = 32
S5_TW = S5_T * S5_CH
S5_ST = 4 * S5_P
S5_OCT = LANES // S5_CH
S5_MAX_ROWS = 256
MIX_WIDTH = SSD_WIDTH + ATT_WIDTH + S5_WIDTH

O_Z = 0
O_XS = SSD_WIDTH
O_BC = O_XS + SSD_WIDTH
O_DT = O_BC + SSD_BC
O_Q = O_DT + 2 * SSD_HEADS
O_K = O_Q + ATT_WIDTH
O_V = O_K + KV_WIDTH
O_U = O_V + KV_WIDTH
O_END = O_U + S5_WIDTH
C_Q = 0
C_Z = 768
C_XS = 1536
C_BC = 2304
C_K = 2560
C_V = 2816
C_U = 3072
C_DT = 3584
DT_PAD = 128
D_INP = C_DT + DT_PAD

FFN_UP_ROWS = 1024
FFN_UP_COLS = 512

NEG = -1e30
VMEM_LIMIT = 56 * 1024 * 1024


def _cparams(*sem):
    return pltpu.CompilerParams(dimension_semantics=sem, vmem_limit_bytes=VMEM_LIMIT)


def _resident(shape):
    nd = len(shape)
    return pl.BlockSpec(shape, lambda *_: (0,) * nd, pipeline_mode=pl.Buffered(1))


def _layer_resident(shape, layer, block=0):
    rest = (0,) * (len(shape) - 1)
    return pl.BlockSpec((None,) + shape, lambda *_: (layer, block) + rest, pipeline_mode=pl.Buffered(1))


def _rms(x, g):
    return x * lax.rsqrt(jnp.mean(x * x, axis=-1, keepdims=True) + EPS) * g


def _silu(x):
    return x * jax.nn.sigmoid(x)


def _dot(a, b):
    return jnp.dot(a, b, preferred_element_type=F32)


def _dot_hi(a, b):
    return jnp.dot(a, b, preferred_element_type=F32, precision=HIGHEST)


def _dot_nt(a, b):
    return lax.dot_general(a, b, (((1,), (1,)), ((), ())), preferred_element_type=F32)


def _dot_tn(a, b):
    return lax.dot_general(a, b, (((0,), (0,)), ((), ())), preferred_element_type=F32)


def _split_bf16(x, parts):
    out = []
    for _ in range(parts):
        p = x.astype(BF16)
        out.append(p)
        x = x - p.astype(F32)
    return out


def _dot_sel_rhs(x, sel, parts):
    acc = None
    for p in _split_bf16(x, parts):
        t = _dot(p, sel)
        acc = t if acc is None else acc + t
    return acc


def _dot_sel_lhs(sel, x, parts):
    acc = None
    for p in _split_bf16(x, parts):
        t = _dot(sel, p)
        acc = t if acc is None else acc + t
    return acc


def _mod_spec(tm, rows_per_mod, first_row):
    return pl.BlockSpec((1, 1, 6 * D_MODEL), lambda i, *_: (first_row + (i * tm) // rows_per_mod, 0, 0))


def _ada_kernel(c_ref, w_ref, b_ref, o_ref):
    c = c_ref[...]
    o_ref[0] = _dot(_silu(c).astype(BF16), w_ref[0].astype(BF16)) + b_ref[0]


def _ada(cvecs, w_ada, b_ada):
    tn = 1024
    rows = cvecs.shape[0]
    return pl.pallas_call(
        _ada_kernel,
        grid=(N_LAYERS, 6 * D_MODEL // tn),
        in_specs=[pl.BlockSpec((rows, D_MODEL), lambda l, j: (0, 0)),
                  pl.BlockSpec((1, D_MODEL, tn), lambda l, j: (l, 0, j)),
                  pl.BlockSpec((1, 1, tn), lambda l, j: (l, 0, j))],
        out_specs=pl.BlockSpec((1, rows, tn), lambda l, j: (l, 0, j)),
        out_shape=jax.ShapeDtypeStruct((N_LAYERS, rows, 6 * D_MODEL), F32),
        compiler_params=_cparams("parallel", "parallel"),
        name="ada_mod",
    )(cvecs, w_ada, b_ada.reshape(N_LAYERS, 1, 6 * D_MODEL))


def _rope(x, cos, sin_signed, first_half):
    outs = []
    for j in range(x.shape[1] // LANES):
        xj = x[:, j * LANES:(j + 1) * LANES]
        partner = jnp.where(first_half, pltpu.roll(xj, LANES - HEAD_DIM // 2, axis=1),
                            pltpu.roll(xj, HEAD_DIM // 2, axis=1))
        outs.append(xj * cos + partner * sin_signed)
    return jnp.concatenate(outs, axis=-1)


def _in_proj_kernel(*refs, rope):
    if rope:
        x_ref, mod_ref, g_ref, w_ref, cos_ref, sin_ref, o_ref, qv_ref, kt_ref = refs
    else:
        x_ref, mod_ref, g_ref, w_ref, o_ref, qv_ref, kt_ref = refs
    mod = mod_ref[0]
    sh = mod[:, 0:D_MODEL]
    sc = mod[:, D_MODEL:2 * D_MODEL]
    h = _rms(x_ref[...], g_ref[...]) * (1.0 + sc) + sh
    o_ref[...] = _dot(h.astype(BF16), w_ref[...])
    o_ref[:, C_Z:C_Z + SSD_WIDTH] = _silu(o_ref[:, C_Z:C_Z + SSD_WIDTH])
    q = o_ref[:, C_Q:C_Q + ATT_WIDTH]
    k = o_ref[:, C_K:C_K + KV_WIDTH]
    if rope:
        lane = lax.broadcasted_iota(jnp.int32, (x_ref.shape[0], LANES), 1)
        first_half = (lane % HEAD_DIM) < (HEAD_DIM // 2)
        q = _rope(q, cos_ref[...], sin_ref[...], first_half)
        k = _rope(k, cos_ref[...], sin_ref[...], first_half)
    qv_ref[:, 0:ATT_WIDTH] = (q * Q_SCALE).astype(BF16)
    qv_ref[:, ATT_WIDTH:] = o_ref[:, C_V:C_V + KV_WIDTH].astype(BF16)
    kt_ref[...] = k.T.astype(BF16)


def _in_proj(x, mod3, gamma, w, *, layer, tm, mod_spec, rope_tables=None, seq_len=None):
    n = x.shape[0]
    rope = rope_tables is not None
    tables, table_specs = (), []
    if rope:
        per_seq = seq_len // tm
        tables = tuple(rope_tables)
        table_specs = [pl.BlockSpec((tm, LANES), lambda i: (i % per_seq, 0))] * 2
    return pl.pallas_call(
        functools.partial(_in_proj_kernel, rope=rope),
        grid=(n // tm,),
        in_specs=[pl.BlockSpec((tm, D_MODEL), lambda i: (i, 0)),
                  mod_spec,
                  _resident((1, D_MODEL)),
                  _layer_resident((D_MODEL, D_INP), layer)] + table_specs,
        out_specs=[pl.BlockSpec((tm, D_INP), lambda i: (i, 0)), pl.BlockSpec((tm, QV_WIDTH), lambda i: (i, 0)),
                   pl.BlockSpec((KV_WIDTH, tm), lambda i: (0, i))],
        out_shape=[jax.ShapeDtypeStruct((n, D_INP), F32), jax.ShapeDtypeStruct((n, QV_WIDTH), BF16),
                   jax.ShapeDtypeStruct((KV_WIDTH, n), BF16)],
        compiler_params=_cparams("parallel"),
        name="in_proj",
    )(x, mod3, gamma, w, *tables)


def _gelu_tanh(x):
    return 0.5 * x * (1.0 + jnp.tanh(math.sqrt(2.0 / math.pi) * (x + 0.044715 * (x * x * x))))


def _out_proj_kernel(x_ref, mod_ref, g_ref, yssd_ref, oatt_ref, ys5_ref, u_ref, d_ref, wglu_ref, bglu_ref,
                     w_ref, o_ref, mix_scr):
    y5 = ys5_ref[...] + d_ref[...] * u_ref[...]
    g = _gelu_tanh(y5)
    s5 = g * jax.nn.sigmoid(_dot(g.astype(BF16), wglu_ref[...]) + bglu_ref[...])
    mix_scr[:, 0:SSD_WIDTH] = yssd_ref[...]
    mix_scr[:, SSD_WIDTH:SSD_WIDTH + ATT_WIDTH] = oatt_ref[...]
    mix_scr[:, SSD_WIDTH + ATT_WIDTH:] = s5.astype(BF16)
    gate = mod_ref[0][:, 2 * D_MODEL:3 * D_MODEL]
    o_ref[...] = x_ref[...] + gate * _rms(_dot(mix_scr[...], w_ref[...]), g_ref[...])


def _out_proj(x, mod3, gamma, y_ssd, o_att, y_s5, proj, s5_d, w_glu, b_glu, w_out, *, layer, tm, mod_spec):
    n = x.shape[0]
    row = lambda i: (i, 0)
    return pl.pallas_call(
        _out_proj_kernel,
        grid=(n // tm,),
        in_specs=[pl.BlockSpec((tm, D_MODEL), row),
                  mod_spec,
                  _resident((1, D_MODEL)),
                  pl.BlockSpec((tm, SSD_WIDTH), row),
                  pl.BlockSpec((tm, ATT_WIDTH), row),
                  pl.BlockSpec((tm, S5_WIDTH), row),
                  pl.BlockSpec((tm, S5_WIDTH), lambda i: (i, C_U // S5_WIDTH)),
                  _resident((1, S5_WIDTH)),
                  _layer_resident((S5_WIDTH, S5_WIDTH), layer),
                  _resident((1, S5_WIDTH)),
                  _layer_resident((MIX_WIDTH, D_MODEL), layer)],
        out_specs=pl.BlockSpec((tm, D_MODEL), row),
        out_shape=jax.ShapeDtypeStruct((n, D_MODEL), F32),
        scratch_shapes=[pltpu.VMEM((tm, MIX_WIDTH), BF16)],
        compiler_params=_cparams("parallel"),
        name="out_proj",
    )(x, mod3, gamma, y_ssd, o_att, y_s5, proj, s5_d, w_glu, b_glu, w_out)


def _ffn_up_kernel(x_ref, mod_ref, gpre_ref, wg_ref, wu_ref, act_ref, h_scr):
    j = pl.program_id(1)

    def gated(h):
        return (_silu(_dot(h, wg_ref[...])) * _dot(h, wu_ref[...])).astype(BF16)

    @pl.when(j == 0)
    def _():
        mod = mod_ref[0]
        sh = mod[:, 3 * D_MODEL:4 * D_MODEL]
        sc = mod[:, 4 * D_MODEL:5 * D_MODEL]
        h = (_rms(x_ref[...], gpre_ref[...]) * (1.0 + sc) + sh).astype(BF16)
        h_scr[...] = h
        act_ref[...] = gated(h)

    @pl.when(j > 0)
    def _():
        act_ref[...] = gated(h_scr[...])


def _ffn_down_kernel(x_ref, mod_ref, gpost_ref, act_ref, wo_ref, o_ref):
    gate = mod_ref[0][:, 5 * D_MODEL:6 * D_MODEL]
    o_ref[...] = x_ref[...] + gate * _rms(_dot(act_ref[...], wo_ref[...]), gpost_ref[...])


def _ffn(x, mod3, g_pre, g_post, w_in, w_out, *, layer, tm_up, tf, tm_down, mod_spec_up, mod_spec_down):
    n = x.shape[0]
    nf = D_FF // tf
    act = pl.pallas_call(
        _ffn_up_kernel,
        grid=(n // tm_up, nf),
        in_specs=[pl.BlockSpec((tm_up, D_MODEL), lambda i, j: (i, 0)),
                  mod_spec_up,
                  _resident((1, D_MODEL)),
                  pl.BlockSpec((None, D_MODEL, tf), lambda i, j: (layer, 0, j)),
                  pl.BlockSpec((None, D_MODEL, tf), lambda i, j: (layer, 0, j + nf))],
        out_specs=pl.BlockSpec((tm_up, tf), lambda i, j: (i, j)),
        out_shape=jax.ShapeDtypeStruct((n, D_FF), BF16),
        scratch_shapes=[pltpu.VMEM((tm_up, D_MODEL), BF16)],
        compiler_params=_cparams("parallel", "arbitrary"),
        name="ffn_up",
    )(x, mod3, g_pre, w_in, w_in)
    return pl.pallas_call(
        _ffn_down_kernel,
        grid=(n // tm_down,),
        in_specs=[pl.BlockSpec((tm_down, D_MODEL), lambda i: (i, 0)),
                  mod_spec_down,
                  _resident((1, D_MODEL)),
                  pl.BlockSpec((tm_down, D_FF), lambda i: (i, 0)),
                  _layer_resident((D_FF, D_MODEL), layer)],
        out_specs=pl.BlockSpec((tm_down, D_MODEL), lambda i: (i, 0)),
        out_shape=jax.ShapeDtypeStruct((n, D_MODEL), F32),
        compiler_params=_cparams("parallel"),
        name="ffn_down",
    )(x, mod3, g_post, act, w_out)


def _attend(q, parts, sink_ref):
    rows = q.shape[0]
    srow = lax.broadcasted_iota(jnp.int32, (ATT_GROUP * rows, 1), 0)
    outs = [None] * ATT_HEADS

    def masked_scores(kv):
        heads = range(kv * ATT_GROUP, (kv + 1) * ATT_GROUP)
        q3 = jnp.concatenate([q[:, h * HEAD_DIM:(h + 1) * HEAD_DIM] for h in heads], axis=0)
        scores = []
        for kt, _, mask in parts:
            s = _dot(q3, kt[kv * HEAD_DIM:(kv + 1) * HEAD_DIM, :])
            scores.append(s if mask is None else jnp.where(mask, s, NEG))
        return scores

    def softmax_numerator(kv, scores):
        heads = range(kv * ATT_GROUP, (kv + 1) * ATT_GROUP)
        sink = jnp.full((ATT_GROUP * rows, 1), sink_ref[heads[-1]] * LOG2E, F32)
        for i in range(ATT_GROUP - 2, -1, -1):
            sink = jnp.where(srow < (i + 1) * rows, sink_ref[heads[i]] * LOG2E, sink)
        blocks = [s[:, j:j + LANES] for s in scores for j in range(0, s.shape[1], LANES)]
        m = jnp.maximum(sink, jnp.max(functools.reduce(jnp.maximum, blocks), axis=-1, keepdims=True))
        p = jnp.concatenate([jnp.exp2(s - m).astype(BF16) for s in scores], axis=1)
        return p, jnp.exp2(sink - m)

    def weighted_values(kv, p, sink_term):
        lo = kv * HEAD_DIM
        vlo = (lo // LANES) * LANES
        v_first = lo == vlo
        v128 = jnp.concatenate([v[:, vlo:vlo + LANES] for _, v, _ in parts], axis=0)
        lane = lax.broadcasted_iota(jnp.int32, v128.shape, 1)
        keep = (lane < HEAD_DIM) if v_first else (lane >= HEAD_DIM)
        acc = _dot(p, jnp.where(keep, v128, jnp.ones_like(v128)))
        ocol, dcol = (0, HEAD_DIM) if v_first else (HEAD_DIM, 0)
        o3 = acc[:, ocol:ocol + HEAD_DIM] / (acc[:, dcol:dcol + 1] + sink_term)
        for i in range(ATT_GROUP):
            outs[kv * ATT_GROUP + i] = o3[i * rows:(i + 1) * rows]

    scores = {0: masked_scores(0)}
    probs = {}
    for step in range(ATT_KV + 2):
        if step + 1 < ATT_KV:
            scores[step + 1] = masked_scores(step + 1)
        if step - 1 in probs:
            weighted_values(step - 1, *probs.pop(step - 1))
        if step in scores:
            probs[step] = softmax_numerator(step, scores.pop(step))
    return jnp.concatenate(outs, axis=-1)


def _ctx_attn_kernel(sink_ref, q_ref, k_ref, v_ref, o_ref):
    o_ref[...] = _attend(q_ref[...], [(k_ref[...], v_ref[...], None)], sink_ref).astype(o_ref.dtype)


def _ctx_attention(qv, kt, sink, *, n_seq, seq_len):
    vcol = ATT_WIDTH // KV_WIDTH
    blk = ATT_BLOCK
    nb = seq_len // blk
    return pl.pallas_call(
        _ctx_attn_kernel,
        grid=(n_seq, nb),
        in_specs=[pl.BlockSpec(memory_space=pltpu.SMEM),
                  pl.BlockSpec((blk, ATT_WIDTH), lambda b, n: (b * nb + n, 0)),
                  pl.BlockSpec((KV_WIDTH, seq_len), lambda b, n: (0, b)),
                  pl.BlockSpec((seq_len, KV_WIDTH), lambda b, n: (b, vcol))],
        out_specs=pl.BlockSpec((blk, ATT_WIDTH), lambda b, n: (b * nb + n, 0)),
        out_shape=jax.ShapeDtypeStruct((n_seq * seq_len, ATT_WIDTH), BF16),
        compiler_params=_cparams("parallel", "parallel"),
        name="ctx_attention",
    )(sink, qv, kt, qv)


def _lat_attn_kernel(sink_ref, q_ref, kp_ref, kc_ref, kn_ref, vp_ref, vc_ref, vn_ref, ck_ref, cv_ref, o_ref):
    n = pl.program_id(1)
    nb = pl.num_programs(1)
    blk = ATT_BLOCK
    row = lax.broadcasted_iota(jnp.int32, (ATT_GROUP * blk, blk), 0) % blk
    col = lax.broadcasted_iota(jnp.int32, (ATT_GROUP * blk, blk), 1)
    mask_prev = jnp.logical_and(col >= row, n > 0)
    mask_next = jnp.logical_and(col <= row, n < nb - 1)
    parts = [(ck_ref[0], cv_ref[0], None),
             (kp_ref[...], vp_ref[...], mask_prev),
             (kc_ref[...], vc_ref[...], None),
             (kn_ref[...], vn_ref[...], mask_next)]
    o_ref[...] = _attend(q_ref[...], parts, sink_ref).astype(o_ref.dtype)


def _lat_attention(qv, kt, sink, ckt, cv, *, n_seq, seq_len):
    blk = ATT_BLOCK
    nb = seq_len // blk

    def cur(b, n):
        return b * nb + n

    def prev(b, n):
        return b * nb + jnp.maximum(n - 1, 0)

    def nxt(b, n):
        return b * nb + jnp.minimum(n + 1, nb - 1)

    vcol = ATT_WIDTH // KV_WIDTH
    return pl.pallas_call(
        _lat_attn_kernel,
        grid=(n_seq, nb),
        in_specs=[pl.BlockSpec(memory_space=pltpu.SMEM),
                  pl.BlockSpec((blk, ATT_WIDTH), lambda b, n: (cur(b, n), 0)),
                  pl.BlockSpec((KV_WIDTH, blk), lambda b, n: (0, prev(b, n))),
                  pl.BlockSpec((KV_WIDTH, blk), lambda b, n: (0, cur(b, n))),
                  pl.BlockSpec((KV_WIDTH, blk), lambda b, n: (0, nxt(b, n))),
                  pl.BlockSpec((blk, KV_WIDTH), lambda b, n: (prev(b, n), vcol)),
                  pl.BlockSpec((blk, KV_WIDTH), lambda b, n: (cur(b, n), vcol)),
                  pl.BlockSpec((blk, KV_WIDTH), lambda b, n: (nxt(b, n), vcol)),
                  pl.BlockSpec((1,) + ckt.shape[1:], lambda b, n: (b, 0, 0)),
                  pl.BlockSpec((1,) + cv.shape[1:], lambda b, n: (b, 0, 0))],
        out_specs=pl.BlockSpec((blk, ATT_WIDTH), lambda b, n: (cur(b, n), 0)),
        out_shape=jax.ShapeDtypeStruct((n_seq * seq_len, ATT_WIDTH), BF16),
        compiler_params=_cparams("parallel", "parallel"),
        name="lat_attention",
    )(sink, qv, kt, kt, kt, qv, qv, qv, ckt, cv)


def _rope_tables(seq_len):
    rows = seq_len // GRID_W
    row = np.repeat(np.arange(rows, dtype=np.float32), GRID_W)
    col = np.tile(np.arange(GRID_W, dtype=np.float32), rows)
    inv = np.float32(ROPE_BASE) ** (-np.arange(ROPE_PER_AXIS, dtype=np.float32) / np.float32(ROPE_PER_AXIS))
    ang = np.concatenate([row[:, None] * inv, col[:, None] * inv], axis=-1)
    cos, sin = np.cos(ang), np.sin(ang)
    cos128 = np.tile(cos, (1, 4))
    sin128 = np.tile(np.concatenate([-sin, sin], axis=-1), (1, 2))
    return jnp.asarray(cos128, F32), jnp.asarray(sin128, F32)


def _conv_kernel(xp_ref, xc_ref, xn_ref, bp_ref, bc_ref, bn_ref, cwx_ref, cbx_ref, cwb_ref, cbb_ref, o_ref,
                 xs_scr, bc_scr, *, seq_len):
    tr = xc_ref.shape[0]
    h = SSD_HALO
    pos = (pl.program_id(0) * tr + lax.broadcasted_iota(jnp.int32, (tr, 1), 0)) % seq_len
    first = pos == 0
    last = pos == seq_len - 1

    def conv(scr, prev_ref, cur_ref, next_ref, w_ref, b_ref):
        scr[0:h, :] = prev_ref[...]
        scr[h:h + tr, :] = cur_ref[...]
        scr[h + tr:h + tr + h, :] = next_ref[...]
        before = jnp.where(first, 0.0, scr[h - 1:h - 1 + tr, :])
        after = jnp.where(last, 0.0, scr[h + 1:h + 1 + tr, :])
        return _silu(before * w_ref[0:1, :] + scr[h:h + tr, :] * w_ref[1:2, :] + after * w_ref[2:3, :]
                     + b_ref[...])

    o_ref[:, 0:SSD_WIDTH] = conv(xs_scr, xp_ref, xc_ref, xn_ref, cwx_ref, cbx_ref)
    o_ref[:, SSD_WIDTH:] = conv(bc_scr, bp_ref, bc_ref, bn_ref, cwb_ref, cbb_ref)


def _ssd_conv(proj, consts, *, seq_len, tr):
    n = proj.shape[0]
    per8 = tr // SSD_HALO
    n_rows8 = n // SSD_HALO
    cwx, cbx, cwb, cbb = consts[:4]
    prev8 = lambda i: jnp.maximum(i * per8 - 1, 0)
    next8 = lambda i: jnp.minimum((i + 1) * per8, n_rows8 - 1)
    xcol, bcol = C_XS // SSD_WIDTH, C_BC // SSD_BC
    return pl.pallas_call(
        functools.partial(_conv_kernel, seq_len=seq_len),
        grid=(n // tr,),
        in_specs=[pl.BlockSpec((SSD_HALO, SSD_WIDTH), lambda i: (prev8(i), xcol)),
                  pl.BlockSpec((tr, SSD_WIDTH), lambda i: (i, xcol)),
                  pl.BlockSpec((SSD_HALO, SSD_WIDTH), lambda i: (next8(i), xcol)),
                  pl.BlockSpec((SSD_HALO, SSD_BC), lambda i: (prev8(i), bcol)),
                  pl.BlockSpec((tr, SSD_BC), lambda i: (i, bcol)),
                  pl.BlockSpec((SSD_HALO, SSD_BC), lambda i: (next8(i), bcol)),
                  _resident(cwx.shape), _resident(cbx.shape), _resident(cwb.shape), _resident(cbb.shape)],
        out_specs=pl.BlockSpec((tr, SSD_WIDTH + SSD_BC), lambda i: (i, 0)),
        out_shape=jax.ShapeDtypeStruct((n, SSD_WIDTH + SSD_BC), F32),
        scratch_shapes=[pltpu.VMEM((tr + 2 * SSD_HALO, SSD_WIDTH), F32),
                        pltpu.VMEM((tr + 2 * SSD_HALO, SSD_BC), F32)],
        compiler_params=_cparams("parallel"),
        name="ssd_conv",
    )(proj, proj, proj, proj, proj, proj, cwx, cbx, cwb, cbb)


def _softplus(x):
    return jnp.maximum(x, 0.0) + jnp.log1p(jnp.exp(-jnp.abs(x)))


def _ssd_bwd_kernel(xs_ref, bc_ref, dt_ref, h0_ref, dtb_ref, alog_ref, tri_ref, exp_ref,
                    hstart_ref, hfin_ref, h_scr):
    i = pl.program_id(1)
    q = SSD_CHUNK
    half = SSD_WIDTH // SSD_GROUPS
    sel_b = exp_ref[1]
    neg_a = -LOG2E * jnp.exp(alog_ref[...])

    @pl.when(i == 0)
    def _():
        h_scr[...] = h0_ref[0]

    hb = h_scr[...]
    for j in reversed(range(xs_ref.shape[0] // q)):
        rows = slice(j * q, (j + 1) * q)
        hstart_ref[0, j] = hb
        bm = bc_ref[rows, 0:SSD_GROUPS * SSD_STATE].astype(BF16)
        dtv = _softplus(dt_ref[rows, :] + dtb_ref[...])
        rev = _dot_sel_lhs(tri_ref[1], dtv * neg_a, 3)
        dt_w = _dot_sel_rhs(dtv, sel_b, 2)
        to_end = _dot_sel_rhs(jnp.exp2(rev[0:1, :] - rev), sel_b, 2)
        decay = _dot_sel_rhs(jnp.exp2(rev[0:SSD_HALO, :]), sel_b, 2)[0:1, :]
        xw = (xs_ref[rows, :] * dt_w * to_end).astype(BF16)
        upd = jnp.concatenate(
            [_dot_tn(bm[:, g * SSD_STATE:(g + 1) * SSD_STATE], xw[:, g * half:(g + 1) * half])
             for g in range(SSD_GROUPS)], axis=-1)
        hb = hb * decay + upd
    h_scr[...] = hb

    @pl.when(i == pl.num_programs(1) - 1)
    def _():
        hfin_ref[0] = hb.T


def _ssd_fwd_chunk(hf, hb_start, gz, xs, bc, dt_raw, dtb_ref, alog_ref, tri_ref, exp_ref, dvec_ref, nw_ref):
    q = SSD_CHUNK
    nh = SSD_HEADS
    gs = SSD_GROUPS * SSD_STATE
    bm = bc[:, 0:gs].astype(BF16)
    cm = bc[:, gs:2 * gs].astype(BF16)
    xs_bf = xs.astype(BF16)

    dtv = _softplus(dt_raw + dtb_ref[...])
    dta = dtv * (-LOG2E * jnp.exp(alog_ref[...]))
    lower, upper = tri_ref[0], tri_ref[1]
    cum = _dot_sel_lhs(lower, dta, 3)
    rev = _dot_sel_lhs(upper, dta, 3)
    dt_t = dtv.T
    dta_t = dta.T
    cum_t = _dot_sel_rhs(dta_t, upper, 3)
    rev_t = _dot_sel_rhs(dta_t, lower, 3)

    sel_f, sel_b = exp_ref[0], exp_ref[1]
    dtf_w = _dot_sel_rhs(dtv, sel_f, 2)
    dtb_w = _dot_sel_rhs(dtv, sel_b, 2)
    ecum_w = _dot_sel_rhs(jnp.exp2(cum), sel_f, 2)
    erev_w = _dot_sel_rhs(jnp.exp2(rev), sel_b, 2)
    toend_w = _dot_sel_rhs(jnp.exp2(cum[q - 1:q, :] - cum), sel_f, 2)

    row = lax.broadcasted_iota(jnp.int32, (q, q), 0)
    col = lax.broadcasted_iota(jnp.int32, (q, q), 1)
    below = row > col
    log_dt_t = jnp.log2(dt_t)
    src_f = log_dt_t - cum_t
    src_b = log_dt_t - rev_t
    gmat = [_dot_nt(cm[:, g * SSD_STATE:(g + 1) * SSD_STATE], bm[:, g * SSD_STATE:(g + 1) * SSD_STATE])
            for g in range(SSD_GROUPS)]
    per_group = nh // SSD_GROUPS
    half = SSD_WIDTH // SSD_GROUPS
    ys = []
    for h in range(nh):
        e = jnp.where(below, cum[:, h:h + 1] + src_f[h:h + 1, :],
                      rev[:, nh + h:nh + h + 1] + src_b[nh + h:nh + h + 1, :])
        m = (gmat[h // per_group] * jnp.exp2(e)).astype(BF16)
        ys.append(_dot(m, xs_bf[:, h * SSD_HEAD_DIM:(h + 1) * SSD_HEAD_DIM]))
    cb = cm.astype(F32) * bm.astype(F32)
    g_diag = jnp.concatenate(
        [jnp.broadcast_to(jnp.sum(cb[:, g * SSD_STATE:(g + 1) * SSD_STATE], axis=-1, keepdims=True), (q, half))
         for g in range(SSD_GROUPS)], axis=-1)
    y = jnp.concatenate(ys, axis=-1) + g_diag * dtf_w * xs

    hf_bf = hf.astype(BF16)
    hb_bf = hb_start.astype(BF16)
    off_f = jnp.concatenate([_dot(cm[:, g * SSD_STATE:(g + 1) * SSD_STATE], hf_bf[:, g * half:(g + 1) * half])
                             for g in range(SSD_GROUPS)], axis=-1)
    off_b = jnp.concatenate([_dot(cm[:, g * SSD_STATE:(g + 1) * SSD_STATE], hb_bf[:, g * half:(g + 1) * half])
                             for g in range(SSD_GROUPS)], axis=-1)
    y = y + off_f * ecum_w + off_b * erev_w + dvec_ref[...] * xs
    y = _rms(y * gz, nw_ref[...])

    xw = (xs * dtf_w * toend_w).astype(BF16)
    upd = jnp.concatenate(
        [_dot_tn(bm[:, g * SSD_STATE:(g + 1) * SSD_STATE], xw[:, g * half:(g + 1) * half])
         for g in range(SSD_GROUPS)], axis=-1)
    return y, hf * ecum_w[q - 1:q, :] + upd


def _ssd_fwd_kernel(gz_ref, xs_ref, bc_ref, dt_ref, hb_ref, h0_ref,
                    dtb_ref, alog_ref, tri_ref, exp_ref, dvec_ref, nw_ref, y_ref, hfin_ref, h_scr):
    c = pl.program_id(1)
    q = SSD_CHUNK

    @pl.when(c == 0)
    def _():
        h_scr[...] = h0_ref[0]

    hf = h_scr[...]
    for j in range(xs_ref.shape[0] // q):
        rows = slice(j * q, (j + 1) * q)
        y, hf = _ssd_fwd_chunk(hf, hb_ref[0, j], gz_ref[rows, :], xs_ref[rows, :], bc_ref[rows, :], dt_ref[rows, :],
                               dtb_ref, alog_ref, tri_ref, exp_ref, dvec_ref, nw_ref)
        y_ref[rows, :] = y.astype(y_ref.dtype)
    h_scr[...] = hf

    @pl.when(c == pl.num_programs(1) - 1)
    def _():
        hfin_ref[0] = hf.T


def _ssd_mixer(proj, h0_f, h0_b, consts, *, n_seq, seq_len, conv_rows):
    q = SSD_CHUNK
    nc = seq_len // q
    cps = max(c for c in range(1, SSD_CHUNKS_PER_STEP + 1) if nc % c == 0)
    ns = nc // cps
    rows = cps * q
    (dt_bias, a_log, tri, expand, dvec, nw) = consts[4:]
    xbc = _ssd_conv(proj, consts, seq_len=seq_len, tr=conv_rows)

    def specs(chunk_of):
        def cur(b, i):
            return b * ns + chunk_of(i)

        return [pl.BlockSpec((rows, SSD_WIDTH), lambda b, i: (cur(b, i), 0)),
                pl.BlockSpec((rows, SSD_BC), lambda b, i: (cur(b, i), SSD_WIDTH // SSD_BC)),
                pl.BlockSpec((rows, DT_PAD), lambda b, i: (cur(b, i), C_DT // DT_PAD))], cur

    state_spec = pl.BlockSpec((1, SSD_STATE, SSD_WIDTH), lambda b, i: (b, 0, 0))
    final_spec = pl.BlockSpec((1, SSD_WIDTH, SSD_STATE), lambda b, i: (b, 0, 0))
    const_specs = [_resident(dt_bias.shape), _resident(a_log.shape), _resident(tri.shape),
                   _resident(expand.shape)]
    scratch = [pltpu.VMEM((SSD_STATE, SSD_WIDTH), F32)]

    data_specs, _ = specs(lambda i: ns - 1 - i)
    hb_start, hb_fin = pl.pallas_call(
        _ssd_bwd_kernel,
        grid=(n_seq, ns),
        in_specs=data_specs + [state_spec] + const_specs,
        out_specs=[pl.BlockSpec((1, cps, SSD_STATE, SSD_WIDTH), lambda b, i: (b, ns - 1 - i, 0, 0)), final_spec],
        out_shape=[jax.ShapeDtypeStruct((n_seq, nc, SSD_STATE, SSD_WIDTH), F32),
                   jax.ShapeDtypeStruct((n_seq, SSD_WIDTH, SSD_STATE), F32)],
        scratch_shapes=scratch,
        compiler_params=_cparams("parallel", "arbitrary"),
        name="ssd_backward_states",
    )(xbc, xbc, proj, h0_b, dt_bias, a_log, tri, expand)

    data_specs, cur = specs(lambda i: i)
    y, hf_fin = pl.pallas_call(
        _ssd_fwd_kernel,
        grid=(n_seq, ns),
        in_specs=([pl.BlockSpec((rows, SSD_WIDTH), lambda b, i: (cur(b, i), C_Z // SSD_WIDTH))] + data_specs
                  + [pl.BlockSpec((1, cps, SSD_STATE, SSD_WIDTH), lambda b, i: (b, i, 0, 0)), state_spec]
                  + const_specs + [_resident(dvec.shape), _resident(nw.shape)]),
        out_specs=[pl.BlockSpec((rows, SSD_WIDTH), lambda b, i: (b * ns + i, 0)), final_spec],
        out_shape=[jax.ShapeDtypeStruct((n_seq * seq_len, SSD_WIDTH), BF16),
                   jax.ShapeDtypeStruct((n_seq, SSD_WIDTH, SSD_STATE), F32)],
        scratch_shapes=scratch,
        compiler_params=_cparams("parallel", "arbitrary"),
        name="ssd_forward",
    )(proj, xbc, xbc, proj, hb_start, h0_f, dt_bias, a_log, tri, expand, dvec, nw)
    return y, hf_fin, hb_fin


def _ssd_consts(l, ssd_conv_w, ssd_conv_b, ssd_dt_bias, ssd_A_log, ssd_D, ssd_norm):
    q = SSD_CHUNK
    cw, cb = ssd_conv_w[l], ssd_conv_b[l]
    pad = DT_PAD - 2 * SSD_HEADS
    dt_bias = jnp.pad(ssd_dt_bias[l].reshape(1, -1), ((0, 0), (0, pad)))
    a_log = jnp.pad(ssd_A_log[l].reshape(1, -1), ((0, 0), (0, pad)))
    r = np.arange(q)
    tri = jnp.asarray(np.stack([r[None, :] <= r[:, None], r[None, :] >= r[:, None]]), BF16)
    head_of_col = np.arange(SSD_WIDTH) // SSD_HEAD_DIM
    j = np.arange(DT_PAD)
    expand = jnp.asarray(np.stack([j[:, None] == head_of_col[None, :],
                                   j[:, None] == head_of_col[None, :] + SSD_HEADS]), BF16)
    dvec = jnp.repeat(ssd_D[l], SSD_HEAD_DIM).reshape(1, SSD_WIDTH)
    return (cw[:, :SSD_WIDTH], cb[:SSD_WIDTH].reshape(1, -1), cw[:, SSD_WIDTH:], cb[SSD_WIDTH:].reshape(1, -1),
            dt_bias, a_log, tri, expand, dvec, ssd_norm[l].reshape(1, -1))


def _ssd_state_in(s):
    b = s.shape[0]
    return jnp.transpose(s, (0, 3, 1, 2)).reshape(b, SSD_STATE, SSD_WIDTH)


def _ssd_state_out(s):
    return s.reshape(s.shape[0], SSD_HEADS, SSD_HEAD_DIM, SSD_STATE)


def _s5_table_kernel(pw_ref, ct_ref, bb_ref, bbt_ref, sel_ref, tile_ref, rhs_ref, woff_ref):
    t = S5_T
    c = S5_CH
    tile = tile_ref[...]
    spread = lambda x, sel: _dot_sel_rhs(x, sel, 3)
    krows, offs, cols = [], [], []
    for d in range(2):
        pw_re, pw_im = pw_ref[0, 2 * d], pw_ref[0, 2 * d + 1]
        c_re = spread(ct_ref[0, :, (2 * d) * c:(2 * d + 1) * c], tile)
        c_im = spread(ct_ref[0, :, (2 * d + 1) * c:(2 * d + 2) * c], tile)
        b_re = spread(bb_ref[0, :, (2 * d) * c:(2 * d + 1) * c], tile)
        b_im = spread(bb_ref[0, :, (2 * d + 1) * c:(2 * d + 2) * c], tile)
        bt_re = bbt_ref[0, (2 * d) * c:(2 * d + 1) * c, :]
        bt_im = bbt_ref[0, (2 * d + 1) * c:(2 * d + 2) * c, :]
        up = (spread(pw_re, sel_ref[0]), spread(pw_im, sel_ref[0]))
        down = (spread(pw_re, sel_ref[1]), spread(pw_im, sel_ref[1]))
        (q_re, q_im), state = ((up, down), (down, up))[d]
        g_re = q_re * c_re - q_im * c_im
        g_im = q_re * c_im + q_im * c_re
        krows.append(_dot_hi(bt_re, g_re) - _dot_hi(bt_im, g_im))
        lb_re, lb_im = pw_re[:, 1:2], pw_im[:, 1:2]
        r_re = q_re * lb_re - q_im * lb_im
        r_im = q_re * lb_im + q_im * lb_re
        offs.append((r_re * c_re - r_im * c_im, -(r_re * c_im + r_im * c_re)))
        s_re, s_im = state
        cols.append(((s_re * b_re - s_im * b_im).T, (s_re * b_im + s_im * b_re).T))
    k_f, k_b = krows
    for s in range(t):
        r = t - 1 - s
        f = k_f if s == 0 else jnp.concatenate([jnp.zeros((c, c * s), F32), k_f[:, :S5_TW - c * s]], axis=1)
        b = k_b if r == 0 else jnp.concatenate([k_b[:, c * r:], jnp.zeros((c, c * r), F32)], axis=1)
        rhs_ref[0, s * c:(s + 1) * c, 0:S5_TW] = (f + b).astype(BF16)
    rhs_ref[0, :, S5_TW:] = jnp.concatenate([cols[0][0], cols[1][0], cols[0][1], cols[1][1]], axis=1).astype(BF16)
    woff_ref[0] = jnp.concatenate([offs[0][0], offs[1][0], offs[0][1], offs[1][1]], axis=0).astype(BF16)


def _s5_tables(s5_A_re, s5_A_im, s5_log_dt, s5_B_re, s5_B_im, s5_C_re, s5_C_im):
    t = S5_T
    n = N_LAYERS * S5_GROUPS
    step = jnp.exp(s5_log_dt)[..., None]
    k = jnp.arange(LANES, dtype=F32)
    keep = k <= t
    kk = jnp.where(keep, k, 0.0)
    mag = jnp.exp(kk * (s5_A_re * step)[..., None])
    ang = kk * (s5_A_im * step)[..., None]
    pw_re = jnp.where(keep, mag * jnp.cos(ang), 0.0)
    pw_im = jnp.where(keep, mag * jnp.sin(ang), 0.0)
    lb_re, lb_im = pw_re[..., 1], pw_im[..., 1]
    den = s5_A_re * s5_A_re + s5_A_im * s5_A_im
    r_re = ((lb_re - 1.0) * s5_A_re + lb_im * s5_A_im) / den
    r_im = (lb_im * s5_A_re - (lb_re - 1.0) * s5_A_im) / den
    b_re, b_im = s5_B_re[:, None], s5_B_im[:, None]
    bb_re = r_re[..., None] * b_re - r_im[..., None] * b_im
    bb_im = r_re[..., None] * b_im + r_im[..., None] * b_re
    c_re = jnp.swapaxes(s5_C_re, -1, -2)
    c_im = jnp.swapaxes(s5_C_im, -1, -2)

    def pack(re, im, axis):
        return jnp.concatenate([re[:, 0], im[:, 0], re[:, 1], im[:, 1]], axis=axis)

    pw = jnp.stack([pw_re[:, 0], pw_im[:, 0], pw_re[:, 1], pw_im[:, 1]], axis=2).reshape(n, 4, S5_P, LANES)
    ct = pack(c_re, c_im, -1).reshape(n, S5_P, 4 * S5_CH)
    bb = pack(bb_re, bb_im, -1)
    bbt = jnp.swapaxes(bb, -1, -2).reshape(n, 4 * S5_CH, S5_P)
    bb = bb.reshape(n, S5_P, 4 * S5_CH)
    lam_t = jnp.concatenate([pw_re[:, 0, :, :, t], pw_re[:, 1, :, :, t], pw_im[:, 0, :, :, t],
                             pw_im[:, 1, :, :, t]], axis=-1).reshape(n, 1, S5_ST)
    jt = np.arange(S5_TW) // S5_CH
    lane_k = np.arange(LANES)[:, None]
    sel = jnp.asarray(np.stack([lane_k == f[None, :] for f in (jt, t - 1 - jt)]), BF16)
    tile = jnp.asarray(np.arange(S5_CH)[:, None] == (np.arange(S5_TW) % S5_CH)[None, :], BF16)
    grp = lambda shape: pl.BlockSpec((1,) + shape, lambda i: (i,) + (0,) * len(shape))
    rhs, woff = pl.pallas_call(
        _s5_table_kernel,
        grid=(n,),
        in_specs=[grp((4, S5_P, LANES)), grp((S5_P, 4 * S5_CH)), grp((S5_P, 4 * S5_CH)), grp((4 * S5_CH, S5_P)),
                  _resident(sel.shape), _resident(tile.shape)],
        out_specs=[grp((S5_TW, S5_TW + S5_ST)), grp((S5_ST, S5_TW))],
        out_shape=[jax.ShapeDtypeStruct((n, S5_TW, S5_TW + S5_ST), BF16),
                   jax.ShapeDtypeStruct((n, S5_ST, S5_TW), BF16)],
        compiler_params=_cparams("parallel"),
        name="s5_tables",
    )(pw, ct, bb, bbt, sel, tile)
    return rhs, woff, lam_t


def _unit_transpose(vs, masks, axis, unit):
    size = vs[0].shape[axis]
    cur = list(vs)
    for d in (1, 2, 4):
        nxt = list(cur)
        for a in range(S5_OCT):
            if a & d:
                continue
            b = a | d
            nxt[a] = jnp.where(masks[d], cur[a], pltpu.roll(cur[b], unit * d, axis=axis))
            nxt[b] = jnp.where(masks[d], pltpu.roll(cur[a], size - unit * d, axis=axis), cur[b])
        cur = nxt
    return cur


def _s5_kernel(u_ref, rhs_ref, woff_ref, lam_ref, h0_ref, y_ref, hfin_ref,
               dst_scr, y_scr, sre_scr, sim_scr, hfre_scr, hfim_scr, hbre_scr, hbim_scr, *, nc, rb):
    t = S5_T
    rows = dst_scr.shape[1]
    sb = rows // nc
    tiles = t // S5_OCT
    nr = rb // S5_OCT
    tile_shape = (nr, S5_OCT, LANES)
    seg = lax.broadcasted_iota(jnp.int32, tile_shape, 2) // S5_CH
    sub = lax.broadcasted_iota(jnp.int32, tile_shape, 1)
    lane_masks = {d: (seg & d) == 0 for d in (1, 2, 4)}
    sub_masks = {d: (sub & d) == 0 for d in (1, 2, 4)}

    def tile_rows(i, r, tq):
        return pl.ds((i * rb + r) * tiles + tq, nr, stride=S5_OCT * tiles)

    def gather(i, carry):
        r0 = pl.multiple_of(i * rb, rb)
        for tq in range(tiles):
            by_chunk = [u_ref[tile_rows(i, r, tq)] for r in range(S5_OCT)]
            by_time = _unit_transpose(by_chunk, sub_masks, 1, 1)
            for gl, v in enumerate(_unit_transpose(by_time, lane_masks, 2, S5_CH)):
                dst_scr[gl, pl.ds(r0, rb), tq * LANES:(tq + 1) * LANES] = v.reshape(rb, LANES).astype(BF16)
        return carry

    lax.fori_loop(0, rows // rb, gather, 0)

    for gl in range(S5_OCT):
        z = _dot(dst_scr[gl], rhs_ref[gl])
        y_scr[gl] = z[:, 0:S5_TW]
        sre_scr[gl] = z[:, S5_TW:S5_TW + 2 * S5_P]
        sim_scr[gl] = z[:, S5_TW + 2 * S5_P:]

    fwd_lane = lax.broadcasted_iota(jnp.int32, (sb, 2 * S5_P), 1) < S5_P
    a_re = [lam_ref[gl][:, 0:2 * S5_P] for gl in range(S5_OCT)]
    a_im = [lam_ref[gl][:, 2 * S5_P:] for gl in range(S5_OCT)]

    def step(i, carry):
        fi = pl.ds(i, sb, stride=nc)
        bj = pl.ds(nc - 1 - i, sb, stride=nc)
        out = []
        for gl in range(S5_OCT):
            h_re, h_im = carry[2 * gl], carry[2 * gl + 1]
            hfre_scr[gl, fi, :] = h_re
            hfim_scr[gl, fi, :] = h_im
            hbre_scr[gl, bj, :] = h_re
            hbim_scr[gl, bj, :] = h_im
            s_re = jnp.where(fwd_lane, sre_scr[gl, fi, :], sre_scr[gl, bj, :])
            s_im = jnp.where(fwd_lane, sim_scr[gl, fi, :], sim_scr[gl, bj, :])
            out.append(a_re[gl] * h_re - a_im[gl] * h_im + s_re)
            out.append(a_re[gl] * h_im + a_im[gl] * h_re + s_im)
        return tuple(out)

    init = []
    for gl in range(S5_OCT):
        h0 = h0_ref[gl, 0]
        init += [h0[:, 0:2 * S5_P], h0[:, 2 * S5_P:]]
    fin = lax.fori_loop(0, nc, step, tuple(init))

    all_fwd = lax.broadcasted_iota(jnp.int32, (rows, 2 * S5_P), 1) < S5_P
    for gl in range(S5_OCT):
        hfin_ref[gl, 0] = jnp.concatenate([fin[2 * gl], fin[2 * gl + 1]], axis=-1)
        hin = jnp.concatenate([jnp.where(all_fwd, hfre_scr[gl], hbre_scr[gl]),
                               jnp.where(all_fwd, hfim_scr[gl], hbim_scr[gl])], axis=-1).astype(BF16)
        y_scr[gl] += _dot(hin, woff_ref[gl])

    def scatter(i, carry):
        r0 = pl.multiple_of(i * rb, rb)
        for tq in range(tiles):
            by_group = [y_scr[gl, pl.ds(r0, rb), tq * LANES:(tq + 1) * LANES].reshape(tile_shape)
                        for gl in range(S5_OCT)]
            by_time = _unit_transpose(by_group, lane_masks, 2, S5_CH)
            for r, v in enumerate(_unit_transpose(by_time, sub_masks, 1, 1)):
                y_ref[tile_rows(i, r, tq)] = v
        return carry

    lax.fori_loop(0, rows // rb, scatter, 0)


def _s5_mixer(proj, tables, layer, h0, *, n_seq, seq_len):
    rhs, woff, lam_t = tables
    t = S5_T
    nc = seq_len // t
    sb = max(1, min(n_seq, S5_MAX_ROWS // nc))
    assert n_seq % sb == 0
    nbs = n_seq // sb
    rows = sb * nc
    rb = min(rows, 32)
    noct = S5_GROUPS // S5_OCT
    ucol = C_U // LANES
    tab = lambda shape: pl.BlockSpec((S5_OCT,) + shape, lambda j, s: (layer * noct + j, 0, 0))
    state = pl.BlockSpec((S5_OCT, 1, sb, S5_ST), lambda j, s: (j, s, 0, 0))
    rows_scr = lambda dt: pltpu.VMEM((S5_OCT, rows, 2 * S5_P), dt)
    n_tok = n_seq * seq_len
    tok_block = (sb * seq_len // S5_OCT, S5_OCT, LANES)
    y, hfin = pl.pallas_call(
        functools.partial(_s5_kernel, nc=nc, rb=rb),
        grid=(noct, nbs),
        in_specs=[pl.BlockSpec(tok_block, lambda j, s: (s, 0, ucol + j)),
                  tab(rhs.shape[1:]), tab(woff.shape[1:]), tab(lam_t.shape[1:]), state],
        out_specs=[pl.BlockSpec(tok_block, lambda j, s: (s, 0, j)), state],
        out_shape=[jax.ShapeDtypeStruct((n_tok // S5_OCT, S5_OCT, S5_WIDTH), F32),
                   jax.ShapeDtypeStruct((S5_GROUPS, nbs, sb, S5_ST), F32)],
        scratch_shapes=[pltpu.VMEM((S5_OCT, rows, S5_TW), BF16), pltpu.VMEM((S5_OCT, rows, S5_TW), F32),
                        rows_scr(F32), rows_scr(F32), rows_scr(F32), rows_scr(F32), rows_scr(F32), rows_scr(F32)],
        compiler_params=_cparams("parallel", "parallel"),
        name="s5_mixer",
    )(proj.reshape(n_tok // S5_OCT, S5_OCT, D_INP), rhs, woff, lam_t, h0.reshape(S5_GROUPS, nbs, sb, S5_ST))
    return y.reshape(n_tok, S5_WIDTH), hfin.reshape(S5_GROUPS, n_seq, S5_ST)


def _s5_state_in(re, im):
    b = re.shape[0]
    x = jnp.stack([re, im], axis=1)
    return jnp.transpose(x, (3, 0, 1, 2, 4)).reshape(S5_GROUPS, b, S5_ST)


def _s5_state_out(h):
    n_seq = h.shape[1]
    x = h.reshape(S5_GROUPS, n_seq, 2, 2, S5_P)
    x = jnp.transpose(x, (2, 1, 3, 0, 4))
    return x[0], x[1]


def kernel(x_prompt, x_sample, c, cache_k, cache_v, state_ssd, state_s5_re, state_s5_im, c_ctx, w_ada, b_ada, norm_mix_pre, norm_mix_post, norm_ffn_pre, norm_ffn_post, w_in, w_out, ssd_conv_w, ssd_conv_b, ssd_dt_bias, ssd_A_log, ssd_D, ssd_norm, attn_sink, s5_A_re, s5_A_im, s5_log_dt, s5_B_re, s5_B_im, s5_C_re, s5_C_im, s5_D, s5_w_glu, s5_b_glu, w_ffn_in, w_ffn_out):
    nb_ctx, len_ctx, _ = x_prompt.shape
    nb_lat, len_lat, _ = x_sample.shape
    n_ctx = nb_ctx * len_ctx
    n_lat = nb_lat * len_lat
    tm = 512
    mod_rows = -(-(1 + nb_lat) // 8) * 8
    cvecs = jnp.concatenate([c_ctx[None, :], c, jnp.zeros((mod_rows - 1 - nb_lat, D_MODEL), F32)], axis=0)
    mods = _ada(cvecs, w_ada, b_ada)
    mod_geom = ((n_ctx, 0), (len_lat, 1))
    token_sets = tuple(dict(mod_spec=_mod_spec(tm, *g)) for g in mod_geom)
    tm_up = min(FFN_UP_ROWS, n_ctx, len_lat)
    rope_args = (dict(), dict(rope_tables=_rope_tables(len_lat), seq_len=len_lat))

    s5_tab = _s5_tables(s5_A_re, s5_A_im, s5_log_dt, s5_B_re, s5_B_im, s5_C_re, s5_C_im)
    w_in_p = jnp.concatenate(
        [w_in[:, :, O_Q:O_K], w_in[:, :, O_Z:O_XS], w_in[:, :, O_XS:O_BC], w_in[:, :, O_BC:O_DT],
         w_in[:, :, O_K:O_V], w_in[:, :, O_V:O_U], w_in[:, :, O_U:O_END], w_in[:, :, O_DT:O_Q],
         jnp.zeros((N_LAYERS, D_MODEL, DT_PAD - 2 * SSD_HEADS), F32)], axis=2).astype(BF16)
    w_out_b = w_out.astype(BF16)
    w_glu_b = s5_w_glu.astype(BF16)
    w_ffn_in_b = w_ffn_in.astype(BF16)
    w_ffn_out_b = w_ffn_out.astype(BF16)
    zeros_ssd = jnp.zeros((nb_ctx, SSD_STATE, SSD_WIDTH), F32)
    zeros_s5 = jnp.zeros((S5_GROUPS, nb_ctx, S5_ST), F32)

    xs = [x_prompt.reshape(n_ctx, D_MODEL), x_sample.reshape(n_lat, D_MODEL)]
    new_k, new_v, new_ssd, new_s5_re, new_s5_im = [], [], [], [], []
    for l in range(N_LAYERS):
        mod3 = mods[l].reshape(mod_rows, 1, 6 * D_MODEL)
        proj, qv, kt = zip(*[_in_proj(x, mod3, norm_mix_pre[l].reshape(1, -1), w_in_p, layer=l, tm=tm, **ts, **ra)
                             for x, ts, ra in zip(xs, token_sets, rope_args)])

        sink = attn_sink[l]
        ckt = jnp.swapaxes(cache_k[:, l].reshape(nb_lat, -1, KV_WIDTH), 1, 2).astype(BF16)
        cv = cache_v[:, l].reshape(nb_lat, -1, KV_WIDTH).astype(BF16)
        o_att = [_ctx_attention(qv[0], kt[0], sink, n_seq=nb_ctx, seq_len=len_ctx),
                 _lat_attention(qv[1], kt[1], sink, ckt, cv, n_seq=nb_lat, seq_len=len_lat)]

        consts = _ssd_consts(l, ssd_conv_w, ssd_conv_b, ssd_dt_bias, ssd_A_log, ssd_D, ssd_norm)
        y_ctx, hf_ctx, hb_ctx = _ssd_mixer(proj[0], zeros_ssd, zeros_ssd, consts, n_seq=nb_ctx, seq_len=len_ctx,
                                           conv_rows=tm)
        y_lat, _, _ = _ssd_mixer(proj[1], _ssd_state_in(state_ssd[:, l, 0]), _ssd_state_in(state_ssd[:, l, 1]),
                                 consts, n_seq=nb_lat, seq_len=len_lat, conv_rows=tm)
        y_ssd = [y_ctx, y_lat]

        s5_ctx, s5_fin = _s5_mixer(proj[0], s5_tab, l, zeros_s5, n_seq=nb_ctx, seq_len=len_ctx)
        s5_lat, _ = _s5_mixer(proj[1], s5_tab, l, _s5_state_in(state_s5_re[:, l], state_s5_im[:, l]),
                              n_seq=nb_lat, seq_len=len_lat)
        y_s5 = [s5_ctx, s5_lat]

        xs = [_out_proj(x, mod3, norm_mix_post[l].reshape(1, -1), y_ssd[i], o_att[i], y_s5[i], proj[i],
                        s5_D[l].reshape(1, -1), w_glu_b, s5_b_glu[l].reshape(1, -1), w_out_b,
                        layer=l, tm=tm, **ts)
              for i, (x, ts) in enumerate(zip(xs, token_sets))]
        xs = [_ffn(x, mod3, norm_ffn_pre[l].reshape(1, -1), norm_ffn_post[l].reshape(1, -1),
                   w_ffn_in_b, w_ffn_out_b, layer=l, tm_up=tm_up, tf=FFN_UP_COLS, tm_down=tm,
                   mod_spec_up=_mod_spec(tm_up, *g), mod_spec_down=_mod_spec(tm, *g))
              for x, g in zip(xs, mod_geom)]

        new_k.append(proj[0][:, C_K:C_K + KV_WIDTH].reshape(nb_ctx, len_ctx, ATT_KV, HEAD_DIM))
        new_v.append(proj[0][:, C_V:C_V + KV_WIDTH].reshape(nb_ctx, len_ctx, ATT_KV, HEAD_DIM))
        new_ssd.append(jnp.stack([_ssd_state_out(hf_ctx), _ssd_state_out(hb_ctx)], axis=1))
        re, im = _s5_state_out(s5_fin)
        new_s5_re.append(re)
        new_s5_im.append(im)

    y_prompt = xs[0].reshape(nb_ctx, len_ctx, D_MODEL)
    y_sample = xs[1].reshape(nb_lat, len_lat, D_MODEL)
    return (y_prompt, y_sample, jnp.stack(new_k, axis=1), jnp.stack(new_v, axis=1), jnp.stack(new_ssd, axis=1),
            jnp.stack(new_s5_re, axis=1), jnp.stack(new_s5_im, axis=1))
```

```python
import functools
import math

import jax
import jax.numpy as jnp
import numpy as np
from jax import lax
from jax.experimental import pallas as pl
from jax.experimental.pallas import tpu as pltpu

F32 = jnp.float32
BF16 = jnp.bfloat16
HIGHEST = lax.Precision.HIGHEST

D_MODEL = 2048
N_LAYERS = 2
D_FF = 5632
GRID_W = 64
EPS = 1e-6
LANES = 128

SSD_HEADS = 12
SSD_HEAD_DIM = 64
SSD_WIDTH = SSD_HEADS * SSD_HEAD_DIM
SSD_GROUPS = 2
SSD_STATE = 64
SSD_BC = 2 * SSD_GROUPS * SSD_STATE
SSD_CHUNK = 128
SSD_CHUNKS_PER_STEP = 4
SSD_HALO = 8
ATT_HEADS = 12
ATT_KV = 4
ATT_GROUP = ATT_HEADS // ATT_KV
HEAD_DIM = 64
ATT_WIDTH = ATT_HEADS * HEAD_DIM
KV_WIDTH = ATT_KV * HEAD_DIM
ATT_BLOCK = 128
QV_WIDTH = ATT_WIDTH + KV_WIDTH
LOG2E = math.log2(math.e)
Q_SCALE = HEAD_DIM ** -0.5 * LOG2E
ROPE_PER_AXIS = HEAD_DIM // 4
ROPE_BASE = 10000.0
S5_GROUPS = 32
S5_CH = 16
S5_WIDTH = S5_GROUPS * S5_CH
S5_P = 64
S5_T = 32
S5_TW = S5_T * S5_CH
S5_ST = 4 * S5_P
S5_OCT = LANES // S5_CH
S5_MAX_ROWS = 256
MIX_WIDTH = SSD_WIDTH + ATT_WIDTH + S5_WIDTH

O_Z = 0
O_XS = SSD_WIDTH
O_BC = O_XS + SSD_WIDTH
O_DT = O_BC + SSD_BC
O_Q = O_DT + 2 * SSD_HEADS
O_K = O_Q + ATT_WIDTH
O_V = O_K + KV_WIDTH
O_U = O_V + KV_WIDTH
O_END = O_U + S5_WIDTH
C_Q = 0
C_Z = 768
C_XS = 1536
C_BC = 2304
C_K = 2560
C_V = 2816
C_U = 3072
C_DT = 3584
DT_PAD = 128
D_INP = C_DT + DT_PAD

FFN_UP_ROWS = 1024
FFN_UP_COLS = 512

NEG = -1e30
VMEM_LIMIT = 56 * 1024 * 1024


def _cparams(*sem):
    return pltpu.CompilerParams(dimension_semantics=sem, vmem_limit_bytes=VMEM_LIMIT)


def _resident(shape):
    nd = len(shape)
    return pl.BlockSpec(shape, lambda *_: (0,) * nd, pipeline_mode=pl.Buffered(1))


def _layer_resident(shape, layer, block=0):
    rest = (0,) * (len(shape) - 1)
    return pl.BlockSpec((None,) + shape, lambda *_: (layer, block) + rest, pipeline_mode=pl.Buffered(1))


def _rms(x, g):
    return x * lax.rsqrt(jnp.mean(x * x, axis=-1, keepdims=True) + EPS) * g


def _silu(x):
    return x * jax.nn.sigmoid(x)


def _dot(a, b):
    return jnp.dot(a, b, preferred_element_type=F32)


def _dot_hi(a, b):
    return jnp.dot(a, b, preferred_element_type=F32, precision=HIGHEST)


def _dot_nt(a, b):
    return lax.dot_general(a, b, (((1,), (1,)), ((), ())), preferred_element_type=F32)


def _dot_tn(a, b):
    return lax.dot_general(a, b, (((0,), (0,)), ((), ())), preferred_element_type=F32)


def _split_bf16(x, parts):
    out = []
    for _ in range(parts):
        p = x.astype(BF16)
        out.append(p)
        x = x - p.astype(F32)
    return out


def _dot_sel_rhs(x, sel, parts):
    acc = None
    for p in _split_bf16(x, parts):
        t = _dot(p, sel)
        acc = t if acc is None else acc + t
    return acc


def _dot_sel_lhs(sel, x, parts):
    acc = None
    for p in _split_bf16(x, parts):
        t = _dot(sel, p)
        acc = t if acc is None else acc + t
    return acc


def _mod_spec(tm, rows_per_mod, first_row):
    return pl.BlockSpec((1, 1, 6 * D_MODEL), lambda i, *_: (first_row + (i * tm) // rows_per_mod, 0, 0))


def _ada_kernel(c_ref, w_ref, b_ref, o_ref):
    c = c_ref[...]
    o_ref[0] = _dot(_silu(c).astype(BF16), w_ref[0].astype(BF16)) + b_ref[0]


def _ada(cvecs, w_ada, b_ada):
    tn = 1024
    rows = cvecs.shape[0]
    return pl.pallas_call(
        _ada_kernel,
        grid=(N_LAYERS, 6 * D_MODEL // tn),
        in_specs=[pl.BlockSpec((rows, D_MODEL), lambda l, j: (0, 0)),
                  pl.BlockSpec((1, D_MODEL, tn), lambda l, j: (l, 0, j)),
                  pl.BlockSpec((1, 1, tn), lambda l, j: (l, 0, j))],
        out_specs=pl.BlockSpec((1, rows, tn), lambda l, j: (l, 0, j)),
        out_shape=jax.ShapeDtypeStruct((N_LAYERS, rows, 6 * D_MODEL), F32),
        compiler_params=_cparams("parallel", "parallel"),
        name="ada_mod",
    )(cvecs, w_ada, b_ada.reshape(N_LAYERS, 1, 6 * D_MODEL))


def _rope(x, cos, sin_signed, first_half):
    outs = []
    for j in range(x.shape[1] // LANES):
        xj = x[:, j * LANES:(j + 1) * LANES]
        partner = jnp.where(first_half, pltpu.roll(xj, LANES - HEAD_DIM // 2, axis=1),
                            pltpu.roll(xj, HEAD_DIM // 2, axis=1))
        outs.append(xj * cos + partner * sin_signed)
    return jnp.concatenate(outs, axis=-1)


def _in_proj_kernel(*refs, rope):
    if rope:
        x_ref, mod_ref, g_ref, w_ref, cos_ref, sin_ref, o_ref, qv_ref, kt_ref = refs
    else:
        x_ref, mod_ref, g_ref, w_ref, o_ref, qv_ref, kt_ref = refs
    mod = mod_ref[0]
    sh = mod[:, 0:D_MODEL]
    sc = mod[:, D_MODEL:2 * D_MODEL]
    h = _rms(x_ref[...], g_ref[...]) * (1.0 + sc) + sh
    o_ref[...] = _dot(h.astype(BF16), w_ref[...])
    o_ref[:, C_Z:C_Z + SSD_WIDTH] = _silu(o_ref[:, C_Z:C_Z + SSD_WIDTH])
    q = o_ref[:, C_Q:C_Q + ATT_WIDTH]
    k = o_ref[:, C_K:C_K + KV_WIDTH]
    if rope:
        lane = lax.broadcasted_iota(jnp.int32, (x_ref.shape[0], LANES), 1)
        first_half = (lane % HEAD_DIM) < (HEAD_DIM // 2)
        q = _rope(q, cos_ref[...], sin_ref[...], first_half)
        k = _rope(k, cos_ref[...], sin_ref[...], first_half)
    qv_ref[:, 0:ATT_WIDTH] = (q * Q_SCALE).astype(BF16)
    qv_ref[:, ATT_WIDTH:] = o_ref[:, C_V:C_V + KV_WIDTH].astype(BF16)
    kt_ref[...] = k.T.astype(BF16)


def _in_proj(x, mod3, gamma, w, *, layer, tm, mod_spec, rope_tables=None, seq_len=None):
    n = x.shape[0]
    rope = rope_tables is not None
    tables, table_specs = (), []
    if rope:
        per_seq = seq_len // tm
        tables = tuple(rope_tables)
        table_specs = [pl.BlockSpec((tm, LANES), lambda i: (i % per_seq, 0))] * 2
    return pl.pallas_call(
        functools.partial(_in_proj_kernel, rope=rope),
        grid=(n // tm,),
        in_specs=[pl.BlockSpec((tm, D_MODEL), lambda i: (i, 0)),
                  mod_spec,
                  _resident((1, D_MODEL)),
                  _layer_resident((D_MODEL, D_INP), layer)] + table_specs,
        out_specs=[pl.BlockSpec((tm, D_INP), lambda i: (i, 0)), pl.BlockSpec((tm, QV_WIDTH), lambda i: (i, 0)),
                   pl.BlockSpec((KV_WIDTH, tm), lambda i: (0, i))],
        out_shape=[jax.ShapeDtypeStruct((n, D_INP), F32), jax.ShapeDtypeStruct((n, QV_WIDTH), BF16),
                   jax.ShapeDtypeStruct((KV_WIDTH, n), BF16)],
        compiler_params=_cparams("parallel"),
        name="in_proj",
    )(x, mod3, gamma, w, *tables)


def _gelu_tanh(x):
    return 0.5 * x * (1.0 + jnp.tanh(math.sqrt(2.0 / math.pi) * (x + 0.044715 * (x * x * x))))


def _out_proj_kernel(x_ref, mod_ref, g_ref, yssd_ref, oatt_ref, ys5_ref, u_ref, d_ref, wglu_ref, bglu_ref,
                     w_ref, o_ref, mix_scr):
    y5 = ys5_ref[...] + d_ref[...] * u_ref[...]
    g = _gelu_tanh(y5)
    s5 = g * jax.nn.sigmoid(_dot(g.astype(BF16), wglu_ref[...]) + bglu_ref[...])
    mix_scr[:, 0:SSD_WIDTH] = yssd_ref[...]
    mix_scr[:, SSD_WIDTH:SSD_WIDTH + ATT_WIDTH] = oatt_ref[...]
    mix_scr[:, SSD_WIDTH + ATT_WIDTH:] = s5.astype(BF16)
    gate = mod_ref[0][:, 2 * D_MODEL:3 * D_MODEL]
    o_ref[...] = x_ref[...] + gate * _rms(_dot(mix_scr[...], w_ref[...]), g_ref[...])


def _out_proj(x, mod3, gamma, y_ssd, o_att, y_s5, proj, s5_d, w_glu, b_glu, w_out, *, layer, tm, mod_spec):
    n = x.shape[0]
    row = lambda i: (i, 0)
    return pl.pallas_call(
        _out_proj_kernel,
        grid=(n // tm,),
        in_specs=[pl.BlockSpec((tm, D_MODEL), row),
                  mod_spec,
                  _resident((1, D_MODEL)),
                  pl.BlockSpec((tm, SSD_WIDTH), row),
                  pl.BlockSpec((tm, ATT_WIDTH), row),
                  pl.BlockSpec((tm, S5_WIDTH), row),
                  pl.BlockSpec((tm, S5_WIDTH), lambda i: (i, C_U // S5_WIDTH)),
                  _resident((1, S5_WIDTH)),
                  _layer_resident((S5_WIDTH, S5_WIDTH), layer),
                  _resident((1, S5_WIDTH)),
                  _layer_resident((MIX_WIDTH, D_MODEL), layer)],
        out_specs=pl.BlockSpec((tm, D_MODEL), row),
        out_shape=jax.ShapeDtypeStruct((n, D_MODEL), F32),
        scratch_shapes=[pltpu.VMEM((tm, MIX_WIDTH), BF16)],
        compiler_params=_cparams("parallel"),
        name="out_proj",
    )(x, mod3, gamma, y_ssd, o_att, y_s5, proj, s5_d, w_glu, b_glu, w_out)


def _ffn_up_kernel(x_ref, mod_ref, gpre_ref, wg_ref, wu_ref, act_ref, h_scr):
    j = pl.program_id(1)

    def gated(h):
        return (_silu(_dot(h, wg_ref[...])) * _dot(h, wu_ref[...])).astype(BF16)

    @pl.when(j == 0)
    def _():
        mod = mod_ref[0]
        sh = mod[:, 3 * D_MODEL:4 * D_MODEL]
        sc = mod[:, 4 * D_MODEL:5 * D_MODEL]
        h = (_rms(x_ref[...], gpre_ref[...]) * (1.0 + sc) + sh).astype(BF16)
        h_scr[...] = h
        act_ref[...] = gated(h)

    @pl.when(j > 0)
    def _():
        act_ref[...] = gated(h_scr[...])


def _ffn_down_kernel(x_ref, mod_ref, gpost_ref, act_ref, wo_ref, o_ref):
    gate = mod_ref[0][:, 5 * D_MODEL:6 * D_MODEL]
    o_ref[...] = x_ref[...] + gate * _rms(_dot(act_ref[...], wo_ref[...]), gpost_ref[...])


def _ffn(x, mod3, g_pre, g_post, w_in, w_out, *, layer, tm_up, tf, tm_down, mod_spec_up, mod_spec_down):
    n = x.shape[0]
    nf = D_FF // tf
    act = pl.pallas_call(
        _ffn_up_kernel,
        grid=(n // tm_up, nf),
        in_specs=[pl.BlockSpec((tm_up, D_MODEL), lambda i, j: (i, 0)),
                  mod_spec_up,
                  _resident((1, D_MODEL)),
                  pl.BlockSpec((None, D_MODEL, tf), lambda i, j: (layer, 0, j)),
                  pl.BlockSpec((None, D_MODEL, tf), lambda i, j: (layer, 0, j + nf))],
        out_specs=pl.BlockSpec((tm_up, tf), lambda i, j: (i, j)),
        out_shape=jax.ShapeDtypeStruct((n, D_FF), BF16),
        scratch_shapes=[pltpu.VMEM((tm_up, D_MODEL), BF16)],
        compiler_params=_cparams("parallel", "arbitrary"),
        name="ffn_up",
    )(x, mod3, g_pre, w_in, w_in)
    return pl.pallas_call(
        _ffn_down_kernel,
        grid=(n // tm_down,),
        in_specs=[pl.BlockSpec((tm_down, D_MODEL), lambda i: (i, 0)),
                  mod_spec_down,
                  _resident((1, D_MODEL)),
                  pl.BlockSpec((tm_down, D_FF), lambda i: (i, 0)),
                  _layer_resident((D_FF, D_MODEL), layer)],
        out_specs=pl.BlockSpec((tm_down, D_MODEL), lambda i: (i, 0)),
        out_shape=jax.ShapeDtypeStruct((n, D_MODEL), F32),
        compiler_params=_cparams("parallel"),
        name="ffn_down",
    )(x, mod3, g_post, act, w_out)


def _attend(q, parts, sink_ref):
    rows = q.shape[0]
    srow = lax.broadcasted_iota(jnp.int32, (ATT_GROUP * rows, 1), 0)
    outs = [None] * ATT_HEADS

    def masked_scores(kv):
        heads = range(kv * ATT_GROUP, (kv + 1) * ATT_GROUP)
        q3 = jnp.concatenate([q[:, h * HEAD_DIM:(h + 1) * HEAD_DIM] for h in heads], axis=0)
        scores = []
        for kt, _, mask in parts:
            s = _dot(q3, kt[kv * HEAD_DIM:(kv + 1) * HEAD_DIM, :])
            scores.append(s if mask is None else jnp.where(mask, s, NEG))
        return scores

    def softmax_numerator(kv, scores):
        heads = range(kv * ATT_GROUP, (kv + 1) * ATT_GROUP)
        sink = jnp.full((ATT_GROUP * rows, 1), sink_ref[heads[-1]] * LOG2E, F32)
        for i in range(ATT_GROUP - 2, -1, -1):
            sink = jnp.where(srow < (i + 1) * rows, sink_ref[heads[i]] * LOG2E, sink)
        blocks = [s[:, j:j + LANES] for s in scores for j in range(0, s.shape[1], LANES)]
        m = jnp.maximum(sink, jnp.max(functools.reduce(jnp.maximum, blocks), axis=-1, keepdims=True))
        p = jnp.concatenate([jnp.exp2(s - m).astype(BF16) for s in scores], axis=1)
        return p, jnp.exp2(sink - m)

    def weighted_values(kv, p, sink_term):
        lo = kv * HEAD_DIM
        vlo = (lo // LANES) * LANES
        v_first = lo == vlo
        v128 = jnp.concatenate([v[:, vlo:vlo + LANES] for _, v, _ in parts], axis=0)
        lane = lax.broadcasted_iota(jnp.int32, v128.shape, 1)
        keep = (lane < HEAD_DIM) if v_first else (lane >= HEAD_DIM)
        acc = _dot(p, jnp.where(keep, v128, jnp.ones_like(v128)))
        ocol, dcol = (0, HEAD_DIM) if v_first else (HEAD_DIM, 0)
        o3 = acc[:, ocol:ocol + HEAD_DIM] / (acc[:, dcol:dcol + 1] + sink_term)
        for i in range(ATT_GROUP):
            outs[kv * ATT_GROUP + i] = o3[i * rows:(i + 1) * rows]

    scores = {0: masked_scores(0)}
    probs = {}
    for step in range(ATT_KV + 2):
        if step + 1 < ATT_KV:
            scores[step + 1] = masked_scores(step + 1)
        if step - 1 in probs:
            weighted_values(step - 1, *probs.pop(step - 1))
        if step in scores:
            probs[step] = softmax_numerator(step, scores.pop(step))
    return jnp.concatenate(outs, axis=-1)


def _ctx_attn_kernel(sink_ref, q_ref, k_ref, v_ref, o_ref):
    o_ref[...] = _attend(q_ref[...], [(k_ref[...], v_ref[...], None)], sink_ref).astype(o_ref.dtype)


def _ctx_attention(qv, kt, sink, *, n_seq, seq_len):
    vcol = ATT_WIDTH // KV_WIDTH
    blk = ATT_BLOCK
    nb = seq_len // blk
    return pl.pallas_call(
        _ctx_attn_kernel,
        grid=(n_seq, nb),
        in_specs=[pl.BlockSpec(memory_space=pltpu.SMEM),
                  pl.BlockSpec((blk, ATT_WIDTH), lambda b, n: (b * nb + n, 0)),
                  pl.BlockSpec((KV_WIDTH, seq_len), lambda b, n: (0, b)),
                  pl.BlockSpec((seq_len, KV_WIDTH), lambda b, n: (b, vcol))],
        out_specs=pl.BlockSpec((blk, ATT_WIDTH), lambda b, n: (b * nb + n, 0)),
        out_shape=jax.ShapeDtypeStruct((n_seq * seq_len, ATT_WIDTH), BF16),
        compiler_params=_cparams("parallel", "parallel"),
        name="ctx_attention",
    )(sink, qv, kt, qv)


def _lat_attn_kernel(sink_ref, q_ref, kp_ref, kc_ref, kn_ref, vp_ref, vc_ref, vn_ref, ck_ref, cv_ref, o_ref):
    n = pl.program_id(1)
    nb = pl.num_programs(1)
    blk = ATT_BLOCK
    row = lax.broadcasted_iota(jnp.int32, (ATT_GROUP * blk, blk), 0) % blk
    col = lax.broadcasted_iota(jnp.int32, (ATT_GROUP * blk, blk), 1)
    mask_prev = jnp.logical_and(col >= row, n > 0)
    mask_next = jnp.logical_and(col <= row, n < nb - 1)
    parts = [(ck_ref[0], cv_ref[0], None),
             (kp_ref[...], vp_ref[...], mask_prev),
             (kc_ref[...], vc_ref[...], None),
             (kn_ref[...], vn_ref[...], mask_next)]
    o_ref[...] = _attend(q_ref[...], parts, sink_ref).astype(o_ref.dtype)


def _lat_attention(qv, kt, sink, ckt, cv, *, n_seq, seq_len):
    blk = ATT_BLOCK
    nb = seq_len // blk

    def cur(b, n):
        return b * nb + n

    def prev(b, n):
        return b * nb + jnp.maximum(n - 1, 0)

    def nxt(b, n):
        return b * nb + jnp.minimum(n + 1, nb - 1)

    vcol = ATT_WIDTH // KV_WIDTH
    return pl.pallas_call(
        _lat_attn_kernel,
        grid=(n_seq, nb),
        in_specs=[pl.BlockSpec(memory_space=pltpu.SMEM),
                  pl.BlockSpec((blk, ATT_WIDTH), lambda b, n: (cur(b, n), 0)),
                  pl.BlockSpec((KV_WIDTH, blk), lambda b, n: (0, prev(b, n))),
                  pl.BlockSpec((KV_WIDTH, blk), lambda b, n: (0, cur(b, n))),
                  pl.BlockSpec((KV_WIDTH, blk), lambda b, n: (0, nxt(b, n))),
                  pl.BlockSpec((blk, KV_WIDTH), lambda b, n: (prev(b, n), vcol)),
                  pl.BlockSpec((blk, KV_WIDTH), lambda b, n: (cur(b, n), vcol)),
                  pl.BlockSpec((blk, KV_WIDTH), lambda b, n: (nxt(b, n), vcol)),
                  pl.BlockSpec((1,) + ckt.shape[1:], lambda b, n: (b, 0, 0)),
                  pl.BlockSpec((1,) + cv.shape[1:], lambda b, n: (b, 0, 0))],
        out_specs=pl.BlockSpec((blk, ATT_WIDTH), lambda b, n: (cur(b, n), 0)),
        out_shape=jax.ShapeDtypeStruct((n_seq * seq_len, ATT_WIDTH), BF16),
        compiler_params=_cparams("parallel", "parallel"),
        name="lat_attention",
    )(sink, qv, kt, kt, kt, qv, qv, qv, ckt, cv)


def _rope_tables(seq_len):
    rows = seq_len // GRID_W
    row = np.repeat(np.arange(rows, dtype=np.float32), GRID_W)
    col = np.tile(np.arange(GRID_W, dtype=np.float32), rows)
    inv = np.float32(ROPE_BASE) ** (-np.arange(ROPE_PER_AXIS, dtype=np.float32) / np.float32(ROPE_PER_AXIS))
    ang = np.concatenate([row[:, None] * inv, col[:, None] * inv], axis=-1)
    cos, sin = np.cos(ang), np.sin(ang)
    cos128 = np.tile(cos, (1, 4))
    sin128 = np.tile(np.concatenate([-sin, sin], axis=-1), (1, 2))
    return jnp.asarray(cos128, F32), jnp.asarray(sin128, F32)


def _conv_silu(scr, prev_ref, cur_ref, next_ref, w_ref, b_ref, at_start, at_end):
    n = cur_ref.shape[0]
    h = SSD_HALO
    scr[0:h, :] = jnp.where(at_start, 0.0, prev_ref[...])
    scr[h:h + n, :] = cur_ref[...]
    scr[h + n:h + n + h, :] = jnp.where(at_end, 0.0, next_ref[...])
    return _silu(scr[h - 1:h - 1 + n, :] * w_ref[0:1, :] + scr[h:h + n, :] * w_ref[1:2, :]
                 + scr[h + 1:h + 1 + n, :] * w_ref[2:3, :] + b_ref[...])


def _softplus(x):
    return jnp.maximum(x, 0.0) + jnp.log1p(jnp.exp(-jnp.abs(x)))


def _ssd_bwd_kernel(xsp_ref, xsc_ref, xsn_ref, bcp_ref, bcc_ref, bcn_ref, dt_ref, h0_ref,
                    cwx_ref, cbx_ref, cwb_ref, cbb_ref, dtb_ref, alog_ref, tri_ref, exp_ref,
                    hstart_ref, hfin_ref, xbc_ref, xs_scr, bc_scr, h_scr):
    i = pl.program_id(1)
    q = SSD_CHUNK
    half = SSD_WIDTH // SSD_GROUPS
    sel_b = exp_ref[1]
    neg_a = -LOG2E * jnp.exp(alog_ref[...])
    at_start = i == pl.num_programs(1) - 1
    at_end = i == 0
    xbc_ref[:, 0:SSD_WIDTH] = _conv_silu(xs_scr, xsp_ref, xsc_ref, xsn_ref, cwx_ref, cbx_ref, at_start, at_end)
    xbc_ref[:, SSD_WIDTH:] = _conv_silu(bc_scr, bcp_ref, bcc_ref, bcn_ref, cwb_ref, cbb_ref, at_start, at_end)

    @pl.when(i == 0)
    def _():
        h_scr[...] = h0_ref[0]

    hb = h_scr[...]
    for j in reversed(range(xsc_ref.shape[0] // q)):
        rows = slice(j * q, (j + 1) * q)
        hstart_ref[0, j] = hb
        bm = xbc_ref[rows, SSD_WIDTH:SSD_WIDTH + SSD_GROUPS * SSD_STATE].astype(BF16)
        dtv = _softplus(dt_ref[rows, :] + dtb_ref[...])
        rev = _dot_sel_lhs(tri_ref[1], dtv * neg_a, 3)
        dt_w = _dot_sel_rhs(dtv, sel_b, 2)
        to_end = _dot_sel_rhs(jnp.exp2(rev[0:1, :] - rev), sel_b, 2)
        decay = _dot_sel_rhs(jnp.exp2(rev[0:SSD_HALO, :]), sel_b, 2)[0:1, :]
        xw = (xbc_ref[rows, 0:SSD_WIDTH] * dt_w * to_end).astype(BF16)
        upd = jnp.concatenate(
            [_dot_tn(bm[:, g * SSD_STATE:(g + 1) * SSD_STATE], xw[:, g * half:(g + 1) * half])
             for g in range(SSD_GROUPS)], axis=-1)
        hb = hb * decay + upd
    h_scr[...] = hb

    @pl.when(i == pl.num_programs(1) - 1)
    def _():
        hfin_ref[0] = hb.T


def _ssd_intra(xs, bc, dt_raw, dtb_ref, alog_ref, tri_ref, exp_ref, dvec_ref):
    q = SSD_CHUNK
    nh = SSD_HEADS
    gs = SSD_GROUPS * SSD_STATE
    bm = bc[:, 0:gs].astype(BF16)
    cm = bc[:, gs:2 * gs].astype(BF16)
    xs_bf = xs.astype(BF16)

    dtv = _softplus(dt_raw + dtb_ref[...])
    dta = dtv * (-LOG2E * jnp.exp(alog_ref[...]))
    lower, upper = tri_ref[0], tri_ref[1]
    cum = _dot_sel_lhs(lower, dta, 3)
    rev = _dot_sel_lhs(upper, dta, 3)
    dt_t = dtv.T
    dta_t = dta.T
    cum_t = _dot_sel_rhs(dta_t, upper, 3)
    rev_t = _dot_sel_rhs(dta_t, lower, 3)

    sel_f, sel_b = exp_ref[0], exp_ref[1]
    dtf_w = _dot_sel_rhs(dtv, sel_f, 2)
    ecum_w = _dot_sel_rhs(jnp.exp2(cum), sel_f, 2)
    erev_w = _dot_sel_rhs(jnp.exp2(rev), sel_b, 2)
    toend_w = _dot_sel_rhs(jnp.exp2(cum[q - 1:q, :] - cum), sel_f, 2)

    row = lax.broadcasted_iota(jnp.int32, (q, q), 0)
    col = lax.broadcasted_iota(jnp.int32, (q, q), 1)
    below = row > col
    log_dt_t = jnp.log2(dt_t)
    src_f = log_dt_t - cum_t
    src_b = log_dt_t - rev_t
    gmat = [_dot_nt(cm[:, g * SSD_STATE:(g + 1) * SSD_STATE], bm[:, g * SSD_STATE:(g + 1) * SSD_STATE])
            for g in range(SSD_GROUPS)]
    per_group = nh // SSD_GROUPS
    half = SSD_WIDTH // SSD_GROUPS
    ys = []
    for h in range(nh):
        e = jnp.where(below, cum[:, h:h + 1] + src_f[h:h + 1, :],
                      rev[:, nh + h:nh + h + 1] + src_b[nh + h:nh + h + 1, :])
        m = (gmat[h // per_group] * jnp.exp2(e)).astype(BF16)
        ys.append(_dot(m, xs_bf[:, h * SSD_HEAD_DIM:(h + 1) * SSD_HEAD_DIM]))
    cb = cm.astype(F32) * bm.astype(F32)
    g_diag = jnp.concatenate(
        [jnp.broadcast_to(jnp.sum(cb[:, g * SSD_STATE:(g + 1) * SSD_STATE], axis=-1, keepdims=True), (q, half))
         for g in range(SSD_GROUPS)], axis=-1)
    y = jnp.concatenate(ys, axis=-1) + (g_diag * dtf_w + dvec_ref[...]) * xs
    xw = (xs * dtf_w * toend_w).astype(BF16)
    upd = jnp.concatenate(
        [_dot_tn(bm[:, g * SSD_STATE:(g + 1) * SSD_STATE], xw[:, g * half:(g + 1) * half])
         for g in range(SSD_GROUPS)], axis=-1)
    return y, cm, ecum_w, erev_w, upd


def _ssd_carry(hf, hb_start, gz, intra, nw_ref):
    y, cm, ecum_w, erev_w, upd = intra
    half = SSD_WIDTH // SSD_GROUPS
    hf_bf = hf.astype(BF16)
    hb_bf = hb_start.astype(BF16)
    off_f = jnp.concatenate([_dot(cm[:, g * SSD_STATE:(g + 1) * SSD_STATE], hf_bf[:, g * half:(g + 1) * half])
                             for g in range(SSD_GROUPS)], axis=-1)
    off_b = jnp.concatenate([_dot(cm[:, g * SSD_STATE:(g + 1) * SSD_STATE], hb_bf[:, g * half:(g + 1) * half])
                             for g in range(SSD_GROUPS)], axis=-1)
    y = y + off_f * ecum_w + off_b * erev_w
    y = _rms(y * gz, nw_ref[...])
    return y, hf * ecum_w[SSD_CHUNK - 1:SSD_CHUNK, :] + upd


def _ssd_fwd_kernel(gz_ref, xs_ref, bc_ref, dt_ref, hb_ref, h0_ref,
                    dtb_ref, alog_ref, tri_ref, exp_ref, dvec_ref, nw_ref, y_ref, hfin_ref, h_scr):
    c = pl.program_id(1)
    q = SSD_CHUNK
    cps = xs_ref.shape[0] // q
    rows = [slice(j * q, (j + 1) * q) for j in range(cps)]

    def intra(j):
        return _ssd_intra(xs_ref[rows[j], :], bc_ref[rows[j], :], dt_ref[rows[j], :],
                          dtb_ref, alog_ref, tri_ref, exp_ref, dvec_ref)

    @pl.when(c == 0)
    def _():
        h_scr[...] = h0_ref[0]

    hf = h_scr[...]
    ahead = intra(0)
    for j in range(cps):
        cur = ahead
        if j + 1 < cps:
            ahead = intra(j + 1)
        y, hf = _ssd_carry(hf, hb_ref[0, j], gz_ref[rows[j], :], cur, nw_ref)
        y_ref[rows[j], :] = y.astype(y_ref.dtype)
    h_scr[...] = hf

    @pl.when(c == pl.num_programs(1) - 1)
    def _():
        hfin_ref[0] = hf.T


def _ssd_mixer(proj, h0_f, h0_b, consts, *, n_seq, seq_len):
    q = SSD_CHUNK
    nc = seq_len // q
    cps = max(c for c in range(1, SSD_CHUNKS_PER_STEP + 1) if nc % c == 0)
    ns = nc // cps
    rows = cps * q
    (cwx, cbx, cwb, cbb, dt_bias, a_log, tri, expand, dvec, nw) = consts
    per8 = rows // SSD_HALO
    last8 = proj.shape[0] // SSD_HALO - 1

    def raw_specs(width, col):
        step = lambda b, i: b * ns + ns - 1 - i
        return [pl.BlockSpec((SSD_HALO, width), lambda b, i: (jnp.maximum(step(b, i) * per8 - 1, 0), col)),
                pl.BlockSpec((rows, width), lambda b, i: (step(b, i), col)),
                pl.BlockSpec((SSD_HALO, width), lambda b, i: (jnp.minimum((step(b, i) + 1) * per8, last8), col))]

    def specs(chunk_of):
        def cur(b, i):
            return b * ns + chunk_of(i)

        return [pl.BlockSpec((rows, SSD_WIDTH), lambda b, i: (cur(b, i), 0)),
                pl.BlockSpec((rows, SSD_BC), lambda b, i: (cur(b, i), SSD_WIDTH // SSD_BC)),
                pl.BlockSpec((rows, DT_PAD), lambda b, i: (cur(b, i), C_DT // DT_PAD))], cur

    state_spec = pl.BlockSpec((1, SSD_STATE, SSD_WIDTH), lambda b, i: (b, 0, 0))
    final_spec = pl.BlockSpec((1, SSD_WIDTH, SSD_STATE), lambda b, i: (b, 0, 0))
    const_specs = [_resident(dt_bias.shape), _resident(a_log.shape), _resident(tri.shape),
                   _resident(expand.shape)]
    scratch = [pltpu.VMEM((SSD_STATE, SSD_WIDTH), F32)]

    data_specs, rcur = specs(lambda i: ns - 1 - i)
    hb_start, hb_fin, xbc = pl.pallas_call(
        _ssd_bwd_kernel,
        grid=(n_seq, ns),
        in_specs=(raw_specs(SSD_WIDTH, C_XS // SSD_WIDTH) + raw_specs(SSD_BC, C_BC // SSD_BC)
                  + [data_specs[2], state_spec]
                  + [_resident(cwx.shape), _resident(cbx.shape), _resident(cwb.shape), _resident(cbb.shape)]
                  + const_specs),
        out_specs=[pl.BlockSpec((1, cps, SSD_STATE, SSD_WIDTH), lambda b, i: (b, ns - 1 - i, 0, 0)), final_spec,
                   pl.BlockSpec((rows, SSD_WIDTH + SSD_BC), lambda b, i: (rcur(b, i), 0))],
        out_shape=[jax.ShapeDtypeStruct((n_seq, nc, SSD_STATE, SSD_WIDTH), F32),
                   jax.ShapeDtypeStruct((n_seq, SSD_WIDTH, SSD_STATE), F32),
                   jax.ShapeDtypeStruct((n_seq * seq_len, SSD_WIDTH + SSD_BC), F32)],
        scratch_shapes=[pltpu.VMEM((rows + 2 * SSD_HALO, SSD_WIDTH), F32),
                        pltpu.VMEM((rows + 2 * SSD_HALO, SSD_BC), F32)] + scratch,
        compiler_params=_cparams("parallel", "arbitrary"),
        name="ssd_backward_states",
    )(proj, proj, proj, proj, proj, proj, proj, h0_b, cwx, cbx, cwb, cbb, dt_bias, a_log, tri, expand)

    data_specs, cur = specs(lambda i: i)
    y, hf_fin = pl.pallas_call(
        _ssd_fwd_kernel,
        grid=(n_seq, ns),
        in_specs=([pl.BlockSpec((rows, SSD_WIDTH), lambda b, i: (cur(b, i), C_Z // SSD_WIDTH))] + data_specs
                  + [pl.BlockSpec((1, cps, SSD_STATE, SSD_WIDTH), lambda b, i: (b, i, 0, 0)), state_spec]
                  + const_specs + [_resident(dvec.shape), _resident(nw.shape)]),
        out_specs=[pl.BlockSpec((rows, SSD_WIDTH), lambda b, i: (b * ns + i, 0)), final_spec],
        out_shape=[jax.ShapeDtypeStruct((n_seq * seq_len, SSD_WIDTH), BF16),
                   jax.ShapeDtypeStruct((n_seq, SSD_WIDTH, SSD_STATE), F32)],
        scratch_shapes=scratch,
        compiler_params=_cparams("parallel", "arbitrary"),
        name="ssd_forward",
    )(proj, xbc, xbc, proj, hb_start, h0_f, dt_bias, a_log, tri, expand, dvec, nw)
    return y, hf_fin, hb_fin


def _ssd_consts(l, ssd_conv_w, ssd_conv_b, ssd_dt_bias, ssd_A_log, ssd_D, ssd_norm):
    q = SSD_CHUNK
    cw, cb = ssd_conv_w[l], ssd_conv_b[l]
    pad = DT_PAD - 2 * SSD_HEADS
    dt_bias = jnp.pad(ssd_dt_bias[l].reshape(1, -1), ((0, 0), (0, pad)))
    a_log = jnp.pad(ssd_A_log[l].reshape(1, -1), ((0, 0), (0, pad)))
    r = np.arange(q)
    tri = jnp.asarray(np.stack([r[None, :] <= r[:, None], r[None, :] >= r[:, None]]), BF16)
    head_of_col = np.arange(SSD_WIDTH) // SSD_HEAD_DIM
    j = np.arange(DT_PAD)
    expand = jnp.asarray(np.stack([j[:, None] == head_of_col[None, :],
                                   j[:, None] == head_of_col[None, :] + SSD_HEADS]), BF16)
    dvec = jnp.repeat(ssd_D[l], SSD_HEAD_DIM).reshape(1, SSD_WIDTH)
    return (cw[:, :SSD_WIDTH], cb[:SSD_WIDTH].reshape(1, -1), cw[:, SSD_WIDTH:], cb[SSD_WIDTH:].reshape(1, -1),
            dt_bias, a_log, tri, expand, dvec, ssd_norm[l].reshape(1, -1))


def _ssd_state_in(s):
    b = s.shape[0]
    return jnp.transpose(s, (0, 3, 1, 2)).reshape(b, SSD_STATE, SSD_WIDTH)


def _ssd_state_out(s):
    return s.reshape(s.shape[0], SSD_HEADS, SSD_HEAD_DIM, SSD_STATE)


def _s5_table_kernel(pw_ref, ct_ref, bb_ref, bbt_ref, sel_ref, tile_ref, rhs_ref, woff_ref):
    t = S5_T
    c = S5_CH
    tile = tile_ref[...]
    spread = lambda x, sel: _dot_sel_rhs(x, sel, 3)
    krows, offs, cols = [], [], []
    for d in range(2):
        pw_re, pw_im = pw_ref[0, 2 * d], pw_ref[0, 2 * d + 1]
        c_re = spread(ct_ref[0, :, (2 * d) * c:(2 * d + 1) * c], tile)
        c_im = spread(ct_ref[0, :, (2 * d + 1) * c:(2 * d + 2) * c], tile)
        b_re = spread(bb_ref[0, :, (2 * d) * c:(2 * d + 1) * c], tile)
        b_im = spread(bb_ref[0, :, (2 * d + 1) * c:(2 * d + 2) * c], tile)
        bt_re = bbt_ref[0, (2 * d) * c:(2 * d + 1) * c, :]
        bt_im = bbt_ref[0, (2 * d + 1) * c:(2 * d + 2) * c, :]
        up = (spread(pw_re, sel_ref[0]), spread(pw_im, sel_ref[0]))
        down = (spread(pw_re, sel_ref[1]), spread(pw_im, sel_ref[1]))
        (q_re, q_im), state = ((up, down), (down, up))[d]
        g_re = q_re * c_re - q_im * c_im
        g_im = q_re * c_im + q_im * c_re
        krows.append(_dot_hi(bt_re, g_re) - _dot_hi(bt_im, g_im))
        lb_re, lb_im = pw_re[:, 1:2], pw_im[:, 1:2]
        r_re = q_re * lb_re - q_im * lb_im
        r_im = q_re * lb_im + q_im * lb_re
        offs.append((r_re * c_re - r_im * c_im, -(r_re * c_im + r_im * c_re)))
        s_re, s_im = state
        cols.append(((s_re * b_re - s_im * b_im).T, (s_re * b_im + s_im * b_re).T))
    k_f, k_b = krows
    for s in range(t):
        r = t - 1 - s
        f = k_f if s == 0 else jnp.concatenate([jnp.zeros((c, c * s), F32), k_f[:, :S5_TW - c * s]], axis=1)
        b = k_b if r == 0 else jnp.concatenate([k_b[:, c * r:], jnp.zeros((c, c * r), F32)], axis=1)
        rhs_ref[0, s * c:(s + 1) * c, 0:S5_TW] = (f + b).astype(BF16)
    rhs_ref[0, :, S5_TW:] = jnp.concatenate([cols[0][0], cols[1][0], cols[0][1], cols[1][1]], axis=1).astype(BF16)
    woff_ref[0] = jnp.concatenate([offs[0][0], offs[1][0], offs[0][1], offs[1][1]], axis=0).astype(BF16)


def _s5_tables(s5_A_re, s5_A_im, s5_log_dt, s5_B_re, s5_B_im, s5_C_re, s5_C_im):
    t = S5_T
    n = N_LAYERS * S5_GROUPS
    step = jnp.exp(s5_log_dt)[..., None]
    k = jnp.arange(LANES, dtype=F32)
    keep = k <= t
    kk = jnp.where(keep, k, 0.0)
    mag = jnp.exp(kk * (s5_A_re * step)[..., None])
    ang = kk * (s5_A_im * step)[..., None]
    pw_re = jnp.where(keep, mag * jnp.cos(ang), 0.0)
    pw_im = jnp.where(keep, mag * jnp.sin(ang), 0.0)
    lb_re, lb_im = pw_re[..., 1], pw_im[..., 1]
    den = s5_A_re * s5_A_re + s5_A_im * s5_A_im
    r_re = ((lb_re - 1.0) * s5_A_re + lb_im * s5_A_im) / den
    r_im = (lb_im * s5_A_re - (lb_re - 1.0) * s5_A_im) / den
    b_re, b_im = s5_B_re[:, None], s5_B_im[:, None]
    bb_re = r_re[..., None] * b_re - r_im[..., None] * b_im
    bb_im = r_re[..., None] * b_im + r_im[..., None] * b_re
    c_re = jnp.swapaxes(s5_C_re, -1, -2)
    c_im = jnp.swapaxes(s5_C_im, -1, -2)

    def pack(re, im, axis):
        return jnp.concatenate([re[:, 0], im[:, 0], re[:, 1], im[:, 1]], axis=axis)

    pw = jnp.stack([pw_re[:, 0], pw_im[:, 0], pw_re[:, 1], pw_im[:, 1]], axis=2).reshape(n, 4, S5_P, LANES)
    ct = pack(c_re, c_im, -1).reshape(n, S5_P, 4 * S5_CH)
    bb = pack(bb_re, bb_im, -1)
    bbt = jnp.swapaxes(bb, -1, -2).reshape(n, 4 * S5_CH, S5_P)
    bb = bb.reshape(n, S5_P, 4 * S5_CH)
    lam_t = jnp.concatenate([pw_re[:, 0, :, :, t], pw_re[:, 1, :, :, t], pw_im[:, 0, :, :, t],
                             pw_im[:, 1, :, :, t]], axis=-1).reshape(n, 1, S5_ST)
    jt = np.arange(S5_TW) // S5_CH
    lane_k = np.arange(LANES)[:, None]
    sel = jnp.asarray(np.stack([lane_k == f[None, :] for f in (jt, t - 1 - jt)]), BF16)
    tile = jnp.asarray(np.arange(S5_CH)[:, None] == (np.arange(S5_TW) % S5_CH)[None, :], BF16)
    grp = lambda shape: pl.BlockSpec((1,) + shape, lambda i: (i,) + (0,) * len(shape))
    rhs, woff = pl.pallas_call(
        _s5_table_kernel,
        grid=(n,),
        in_specs=[grp((4, S5_P, LANES)), grp((S5_P, 4 * S5_CH)), grp((S5_P, 4 * S5_CH)), grp((4 * S5_CH, S5_P)),
                  _resident(sel.shape), _resident(tile.shape)],
        out_specs=[grp((S5_TW, S5_TW + S5_ST)), grp((S5_ST, S5_TW))],
        out_shape=[jax.ShapeDtypeStruct((n, S5_TW, S5_TW + S5_ST), BF16),
                   jax.ShapeDtypeStruct((n, S5_ST, S5_TW), BF16)],
        compiler_params=_cparams("parallel"),
        name="s5_tables",
    )(pw, ct, bb, bbt, sel, tile)
    return rhs, woff, lam_t


def _unit_transpose(vs, masks, axis, unit):
    size = vs[0].shape[axis]
    cur = list(vs)
    for d in (1, 2, 4):
        nxt = list(cur)
        for a in range(S5_OCT):
            if a & d:
                continue
            b = a | d
            nxt[a] = jnp.where(masks[d], cur[a], pltpu.roll(cur[b], unit * d, axis=axis))
            nxt[b] = jnp.where(masks[d], pltpu.roll(cur[a], size - unit * d, axis=axis), cur[b])
        cur = nxt
    return cur


def _s5_kernel(u_ref, rhs_ref, woff_ref, lam_ref, h0_ref, y_ref, hfin_ref,
               dst_scr, y_scr, sre_scr, sim_scr, hfre_scr, hfim_scr, hbre_scr, hbim_scr, *, nc, rb):
    t = S5_T
    rows = dst_scr.shape[1]
    sb = rows // nc
    tiles = t // S5_OCT
    nr = rb // S5_OCT
    tile_shape = (nr, S5_OCT, LANES)
    seg = lax.broadcasted_iota(jnp.int32, tile_shape, 2) // S5_CH
    sub = lax.broadcasted_iota(jnp.int32, tile_shape, 1)
    lane_masks = {d: (seg & d) == 0 for d in (1, 2, 4)}
    sub_masks = {d: (sub & d) == 0 for d in (1, 2, 4)}

    def tile_rows(i, r, tq):
        return pl.ds((i * rb + r) * tiles + tq, nr, stride=S5_OCT * tiles)

    def gather(i, carry):
        r0 = pl.multiple_of(i * rb, rb)
        for tq in range(tiles):
            by_chunk = [u_ref[tile_rows(i, r, tq)] for r in range(S5_OCT)]
            by_time = _unit_transpose(by_chunk, sub_masks, 1, 1)
            for gl, v in enumerate(_unit_transpose(by_time, lane_masks, 2, S5_CH)):
                dst_scr[gl, pl.ds(r0, rb), tq * LANES:(tq + 1) * LANES] = v.reshape(rb, LANES).astype(BF16)
        return carry

    lax.fori_loop(0, rows // rb, gather, 0)

    for gl in range(S5_OCT):
        z = _dot(dst_scr[gl], rhs_ref[gl])
        y_scr[gl] = z[:, 0:S5_TW]
        sre_scr[gl] = z[:, S5_TW:S5_TW + 2 * S5_P]
        sim_scr[gl] = z[:, S5_TW + 2 * S5_P:]

    fwd_lane = lax.broadcasted_iota(jnp.int32, (sb, 2 * S5_P), 1) < S5_P
    a_re = [lam_ref[gl][:, 0:2 * S5_P] for gl in range(S5_OCT)]
    a_im = [lam_ref[gl][:, 2 * S5_P:] for gl in range(S5_OCT)]

    def step(i, carry):
        fi = pl.ds(i, sb, stride=nc)
        bj = pl.ds(nc - 1 - i, sb, stride=nc)
        out = []
        for gl in range(S5_OCT):
            h_re, h_im = carry[2 * gl], carry[2 * gl + 1]
            hfre_scr[gl, fi, :] = h_re
            hfim_scr[gl, fi, :] = h_im
            hbre_scr[gl, bj, :] = h_re
            hbim_scr[gl, bj, :] = h_im
            s_re = jnp.where(fwd_lane, sre_scr[gl, fi, :], sre_scr[gl, bj, :])
            s_im = jnp.where(fwd_lane, sim_scr[gl, fi, :], sim_scr[gl, bj, :])
            out.append(a_re[gl] * h_re - a_im[gl] * h_im + s_re)
            out.append(a_re[gl] * h_im + a_im[gl] * h_re + s_im)
        return tuple(out)

    init = []
    for gl in range(S5_OCT):
        h0 = h0_ref[gl, 0]
        init += [h0[:, 0:2 * S5_P], h0[:, 2 * S5_P:]]
    fin = lax.fori_loop(0, nc, step, tuple(init))

    all_fwd = lax.broadcasted_iota(jnp.int32, (rows, 2 * S5_P), 1) < S5_P
    for gl in range(S5_OCT):
        hfin_ref[gl, 0] = jnp.concatenate([fin[2 * gl], fin[2 * gl + 1]], axis=-1)
        hin = jnp.concatenate([jnp.where(all_fwd, hfre_scr[gl], hbre_scr[gl]),
                               jnp.where(all_fwd, hfim_scr[gl], hbim_scr[gl])], axis=-1).astype(BF16)
        y_scr[gl] += _dot(hin, woff_ref[gl])

    def scatter(i, carry):
        r0 = pl.multiple_of(i * rb, rb)
        for tq in range(tiles):
            by_group = [y_scr[gl, pl.ds(r0, rb), tq * LANES:(tq + 1) * LANES].reshape(tile_shape)
                        for gl in range(S5_OCT)]
            by_time = _unit_transpose(by_group, lane_masks, 2, S5_CH)
            for r, v in enumerate(_unit_transpose(by_time, sub_masks, 1, 1)):
                y_ref[tile_rows(i, r, tq)] = v
        return carry

    lax.fori_loop(0, rows // rb, scatter, 0)


def _s5_mixer(proj, tables, layer, h0, *, n_seq, seq_len):
    rhs, woff, lam_t = tables
    t = S5_T
    nc = seq_len // t
    sb = max(1, min(n_seq, S5_MAX_ROWS // nc))
    assert n_seq % sb == 0
    nbs = n_seq // sb
    rows = sb * nc
    rb = min(rows, 32)
    noct = S5_GROUPS // S5_OCT
    ucol = C_U // LANES
    tab = lambda shape: pl.BlockSpec((S5_OCT,) + shape, lambda j, s: (layer * noct + j, 0, 0))
    state = pl.BlockSpec((S5_OCT, 1, sb, S5_ST), lambda j, s: (j, s, 0, 0))
    rows_scr = lambda dt: pltpu.VMEM((S5_OCT, rows, 2 * S5_P), dt)
    n_tok = n_seq * seq_len
    tok_block = (sb * seq_len // S5_OCT, S5_OCT, LANES)
    y, hfin = pl.pallas_call(
        functools.partial(_s5_kernel, nc=nc, rb=rb),
        grid=(noct, nbs),
        in_specs=[pl.BlockSpec(tok_block, lambda j, s: (s, 0, ucol + j)),
                  tab(rhs.shape[1:]), tab(woff.shape[1:]), tab(lam_t.shape[1:]), state],
        out_specs=[pl.BlockSpec(tok_block, lambda j, s: (s, 0, j)), state],
        out_shape=[jax.ShapeDtypeStruct((n_tok // S5_OCT, S5_OCT, S5_WIDTH), F32),
                   jax.ShapeDtypeStruct((S5_GROUPS, nbs, sb, S5_ST), F32)],
        scratch_shapes=[pltpu.VMEM((S5_OCT, rows, S5_TW), BF16), pltpu.VMEM((S5_OCT, rows, S5_TW), F32),
                        rows_scr(F32), rows_scr(F32), rows_scr(F32), rows_scr(F32), rows_scr(F32), rows_scr(F32)],
        compiler_params=_cparams("parallel", "parallel"),
        name="s5_mixer",
    )(proj.reshape(n_tok // S5_OCT, S5_OCT, D_INP), rhs, woff, lam_t, h0.reshape(S5_GROUPS, nbs, sb, S5_ST))
    return y.reshape(n_tok, S5_WIDTH), hfin.reshape(S5_GROUPS, n_seq, S5_ST)


def _s5_state_in(re, im):
    b = re.shape[0]
    x = jnp.stack([re, im], axis=1)
    return jnp.transpose(x, (3, 0, 1, 2, 4)).reshape(S5_GROUPS, b, S5_ST)


def _s5_state_out(h):
    n_seq = h.shape[1]
    x = h.reshape(S5_GROUPS, n_seq, 2, 2, S5_P)
    x = jnp.transpose(x, (2, 1, 3, 0, 4))
    return x[0], x[1]


def kernel(x_prompt, x_sample, c, cache_k, cache_v, state_ssd, state_s5_re, state_s5_im, c_ctx, w_ada, b_ada, norm_mix_pre, norm_mix_post, norm_ffn_pre, norm_ffn_post, w_in, w_out, ssd_conv_w, ssd_conv_b, ssd_dt_bias, ssd_A_log, ssd_D, ssd_norm, attn_sink, s5_A_re, s5_A_im, s5_log_dt, s5_B_re, s5_B_im, s5_C_re, s5_C_im, s5_D, s5_w_glu, s5_b_glu, w_ffn_in, w_ffn_out):
    nb_ctx, len_ctx, _ = x_prompt.shape
    nb_lat, len_lat, _ = x_sample.shape
    n_ctx = nb_ctx * len_ctx
    n_lat = nb_lat * len_lat
    tm = 512
    mod_rows = -(-(1 + nb_lat) // 8) * 8
    cvecs = jnp.concatenate([c_ctx[None, :], c, jnp.zeros((mod_rows - 1 - nb_lat, D_MODEL), F32)], axis=0)
    mods = _ada(cvecs, w_ada, b_ada)
    mod_geom = ((n_ctx, 0), (len_lat, 1))
    token_sets = tuple(dict(mod_spec=_mod_spec(tm, *g)) for g in mod_geom)
    tm_up = min(FFN_UP_ROWS, n_ctx, len_lat)
    rope_args = (dict(), dict(rope_tables=_rope_tables(len_lat), seq_len=len_lat))

    s5_tab = _s5_tables(s5_A_re, s5_A_im, s5_log_dt, s5_B_re, s5_B_im, s5_C_re, s5_C_im)
    w_in_p = jnp.concatenate(
        [w_in[:, :, O_Q:O_K], w_in[:, :, O_Z:O_XS], w_in[:, :, O_XS:O_BC], w_in[:, :, O_BC:O_DT],
         w_in[:, :, O_K:O_V], w_in[:, :, O_V:O_U], w_in[:, :, O_U:O_END], w_in[:, :, O_DT:O_Q],
         jnp.zeros((N_LAYERS, D_MODEL, DT_PAD - 2 * SSD_HEADS), F32)], axis=2).astype(BF16)
    w_out_b = w_out.astype(BF16)
    w_glu_b = s5_w_glu.astype(BF16)
    w_ffn_in_b = w_ffn_in.astype(BF16)
    w_ffn_out_b = w_ffn_out.astype(BF16)
    zeros_ssd = jnp.zeros((nb_ctx, SSD_STATE, SSD_WIDTH), F32)
    zeros_s5 = jnp.zeros((S5_GROUPS, nb_ctx, S5_ST), F32)

    xs = [x_prompt.reshape(n_ctx, D_MODEL), x_sample.reshape(n_lat, D_MODEL)]
    new_k, new_v, new_ssd, new_s5_re, new_s5_im = [], [], [], [], []
    for l in range(N_LAYERS):
        mod3 = mods[l].reshape(mod_rows, 1, 6 * D_MODEL)
        proj, qv, kt = zip(*[_in_proj(x, mod3, norm_mix_pre[l].reshape(1, -1), w_in_p, layer=l, tm=tm, **ts, **ra)
                             for x, ts, ra in zip(xs, token_sets, rope_args)])

        sink = attn_sink[l]
        ckt = jnp.swapaxes(cache_k[:, l].reshape(nb_lat, -1, KV_WIDTH), 1, 2).astype(BF16)
        cv = cache_v[:, l].reshape(nb_lat, -1, KV_WIDTH).astype(BF16)
        o_att = [_ctx_attention(qv[0], kt[0], sink, n_seq=nb_ctx, seq_len=len_ctx),
                 _lat_attention(qv[1], kt[1], sink, ckt, cv, n_seq=nb_lat, seq_len=len_lat)]

        consts = _ssd_consts(l, ssd_conv_w, ssd_conv_b, ssd_dt_bias, ssd_A_log, ssd_D, ssd_norm)
        y_ctx, hf_ctx, hb_ctx = _ssd_mixer(proj[0], zeros_ssd, zeros_ssd, consts, n_seq=nb_ctx, seq_len=len_ctx)
        y_lat, _, _ = _ssd_mixer(proj[1], _ssd_state_in(state_ssd[:, l, 0]), _ssd_state_in(state_ssd[:, l, 1]),
                                 consts, n_seq=nb_lat, seq_len=len_lat)
        y_ssd = [y_ctx, y_lat]

        s5_ctx, s5_fin = _s5_mixer(proj[0], s5_tab, l, zeros_s5, n_seq=nb_ctx, seq_len=len_ctx)
        s5_lat, _ = _s5_mixer(proj[1], s5_tab, l, _s5_state_in(state_s5_re[:, l], state_s5_im[:, l]),
                              n_seq=nb_lat, seq_len=len_lat)
        y_s5 = [s5_ctx, s5_lat]

        xs = [_out_proj(x, mod3, norm_mix_post[l].reshape(1, -1), y_ssd[i], o_att[i], y_s5[i], proj[i],
                        s5_D[l].reshape(1, -1), w_glu_b, s5_b_glu[l].reshape(1, -1), w_out_b,
                        layer=l, tm=tm, **ts)
              for i, (x, ts) in enumerate(zip(xs, token_sets))]
        xs = [_ffn(x, mod3, norm_ffn_pre[l].reshape(1, -1), norm_ffn_post[l].reshape(1, -1),
                   w_ffn_in_b, w_ffn_out_b, layer=l, tm_up=tm_up, tf=FFN_UP_COLS, tm_down=tm,
                   mod_spec_up=_mod_spec(tm_up, *g), mod_spec_down=_mod_spec(tm, *g))
              for x, g in zip(xs, mod_geom)]

        new_k.append(proj[0][:, C_K:C_K + KV_WIDTH].reshape(nb_ctx, len_ctx, ATT_KV, HEAD_DIM))
        new_v.append(proj[0][:, C_V:C_V + KV_WIDTH].reshape(nb_ctx, len_ctx, ATT_KV, HEAD_DIM))
        new_ssd.append(jnp.stack([_ssd_state_out(hf_ctx), _ssd_state_out(hb_ctx)], axis=1))
        re, im = _s5_state_out(s5_fin)
        new_s5_re.append(re)
        new_s5_im.append(im)

    y_prompt = xs[0].reshape(nb_ctx, len_ctx, D_MODEL)
    y_sample = xs[1].reshape(nb_lat, len_lat, D_MODEL)
    return (y_prompt, y_sample, jnp.stack(new_k, axis=1), jnp.stack(new_v, axis=1), jnp.stack(new_ssd, axis=1),
            jnp.stack(new_s5_re, axis=1), jnp.stack(new_s5_im, axis=1))
```

```python
import functools
import math

import jax
import jax.numpy as jnp
import numpy as np
from jax import lax
from jax.experimental import pallas as pl
from jax.experimental.pallas import tpu as pltpu

F32 = jnp.float32
BF16 = jnp.bfloat16

D_MODEL = 2048
N_LAYERS = 2
D_FF = 5632
GRID_W = 64
EPS = 1e-6
LANES = 128

SSD_HEADS = 12
SSD_HEAD_DIM = 64
SSD_WIDTH = SSD_HEADS * SSD_HEAD_DIM
SSD_GROUPS = 2
SSD_STATE = 64
SSD_BC = 2 * SSD_GROUPS * SSD_STATE
SSD_CHUNK = 128
SSD_CHUNKS_PER_STEP = 4
SSD_HALO = 8
ATT_HEADS = 12
ATT_KV = 4
ATT_GROUP = ATT_HEADS // ATT_KV
HEAD_DIM = 64
ATT_WIDTH = ATT_HEADS * HEAD_DIM
KV_WIDTH = ATT_KV * HEAD_DIM
ATT_BLOCK = 128
QV_WIDTH = ATT_WIDTH + KV_WIDTH
LOG2E = math.log2(math.e)
Q_SCALE = HEAD_DIM ** -0.5 * LOG2E
ROPE_PER_AXIS = HEAD_DIM // 4
ROPE_BASE = 10000.0
S5_GROUPS = 32
S5_CH = 16
S5_WIDTH = S5_GROUPS * S5_CH
S5_P = 64
S5_T = 32
S5_TW = S5_T * S5_CH
S5_ST = 4 * S5_P
S5_OCT = LANES // S5_CH
S5_MAX_ROWS = 256
MIX_WIDTH = SSD_WIDTH + ATT_WIDTH + S5_WIDTH

O_Z = 0
O_XS = SSD_WIDTH
O_BC = O_XS + SSD_WIDTH
O_DT = O_BC + SSD_BC
O_Q = O_DT + 2 * SSD_HEADS
O_K = O_Q + ATT_WIDTH
O_V = O_K + KV_WIDTH
O_U = O_V + KV_WIDTH
O_END = O_U + S5_WIDTH
C_Q = 0
C_Z = 768
C_XS = 1536
C_BC = 2304
C_K = 2560
C_V = 2816
C_U = 3072
C_DT = 3584
DT_PAD = 128
D_INP = C_DT + DT_PAD

FFN_UP_ROWS = 1024
FFN_UP_COLS = 512

NEG = -1e30
VMEM_LIMIT = 56 * 1024 * 1024


def _cparams(*sem):
    return pltpu.CompilerParams(dimension_semantics=sem, vmem_limit_bytes=VMEM_LIMIT)


def _resident(shape):
    nd = len(shape)
    return pl.BlockSpec(shape, lambda *_: (0,) * nd, pipeline_mode=pl.Buffered(1))


def _layer_resident(shape, layer, block=0):
    rest = (0,) * (len(shape) - 1)
    return pl.BlockSpec((None,) + shape, lambda *_: (layer, block) + rest, pipeline_mode=pl.Buffered(1))


def _rms(x, g):
    return x * lax.rsqrt(jnp.mean(x * x, axis=-1, keepdims=True) + EPS) * g


def _silu(x):
    return x * jax.nn.sigmoid(x)


def _dot(a, b):
    return jnp.dot(a, b, preferred_element_type=F32)


def _dot_wide(a, b):
    a_hi, a_lo = _split_bf16(a, 2)
    b_hi, b_lo = _split_bf16(b, 2)
    return _dot(a_hi, b_hi) + (_dot(a_hi, b_lo) + _dot(a_lo, b_hi))


def _dot_nt(a, b):
    return lax.dot_general(a, b, (((1,), (1,)), ((), ())), preferred_element_type=F32)


def _dot_tn(a, b):
    return lax.dot_general(a, b, (((0,), (0,)), ((), ())), preferred_element_type=F32)


def _split_bf16(x, parts):
    out = []
    for _ in range(parts):
        p = x.astype(BF16)
        out.append(p)
        x = x - p.astype(F32)
    return out


def _dot_sel_rhs(x, sel, parts):
    acc = None
    for p in _split_bf16(x, parts):
        t = _dot(p, sel)
        acc = t if acc is None else acc + t
    return acc


def _dot_sel_lhs(sel, x, parts):
    acc = None
    for p in _split_bf16(x, parts):
        t = _dot(sel, p)
        acc = t if acc is None else acc + t
    return acc


def _mod_spec(tm, rows_per_mod, first_row):
    return pl.BlockSpec((1, 1, 6 * D_MODEL), lambda i, *_: (first_row + (i * tm) // rows_per_mod, 0, 0))


def _ada_kernel(c_ref, w_ref, b_ref, o_ref):
    c = c_ref[...]
    o_ref[0] = _dot(_silu(c).astype(BF16), w_ref[0].astype(BF16)) + b_ref[0]


def _ada(cvecs, w_ada, b_ada):
    tn = 1024
    rows = cvecs.shape[0]
    return pl.pallas_call(
        _ada_kernel,
        grid=(N_LAYERS, 6 * D_MODEL // tn),
        in_specs=[pl.BlockSpec((rows, D_MODEL), lambda l, j: (0, 0)),
                  pl.BlockSpec((1, D_MODEL, tn), lambda l, j: (l, 0, j)),
                  pl.BlockSpec((1, 1, tn), lambda l, j: (l, 0, j))],
        out_specs=pl.BlockSpec((1, rows, tn), lambda l, j: (l, 0, j)),
        out_shape=jax.ShapeDtypeStruct((N_LAYERS, rows, 6 * D_MODEL), F32),
        compiler_params=_cparams("parallel", "parallel"),
        name="ada_mod",
    )(cvecs, w_ada, b_ada.reshape(N_LAYERS, 1, 6 * D_MODEL))


def _rope(x, cos, sin_signed, first_half):
    outs = []
    for j in range(x.shape[1] // LANES):
        xj = x[:, j * LANES:(j + 1) * LANES]
        partner = jnp.where(first_half, pltpu.roll(xj, LANES - HEAD_DIM // 2, axis=1),
                            pltpu.roll(xj, HEAD_DIM // 2, axis=1))
        outs.append(xj * cos + partner * sin_signed)
    return jnp.concatenate(outs, axis=-1)


def _in_proj_kernel(*refs, rope):
    if rope:
        x_ref, mod_ref, g_ref, w_ref, cos_ref, sin_ref, o_ref, qv_ref, kt_ref = refs
    else:
        x_ref, mod_ref, g_ref, w_ref, o_ref, qv_ref, kt_ref = refs
    mod = mod_ref[0]
    sh = mod[:, 0:D_MODEL]
    sc = mod[:, D_MODEL:2 * D_MODEL]
    h = _rms(x_ref[...], g_ref[...]) * (1.0 + sc) + sh
    o_ref[...] = _dot(h.astype(BF16), w_ref[...])
    o_ref[:, C_Z:C_Z + SSD_WIDTH] = _silu(o_ref[:, C_Z:C_Z + SSD_WIDTH])
    q = o_ref[:, C_Q:C_Q + ATT_WIDTH]
    k = o_ref[:, C_K:C_K + KV_WIDTH]
    if rope:
        lane = lax.broadcasted_iota(jnp.int32, (x_ref.shape[0], LANES), 1)
        first_half = (lane % HEAD_DIM) < (HEAD_DIM // 2)
        q = _rope(q, cos_ref[...], sin_ref[...], first_half)
        k = _rope(k, cos_ref[...], sin_ref[...], first_half)
    qv_ref[:, 0:ATT_WIDTH] = (q * Q_SCALE).astype(BF16)
    qv_ref[:, ATT_WIDTH:] = o_ref[:, C_V:C_V + KV_WIDTH].astype(BF16)
    kt_ref[...] = k.T.astype(BF16)


def _in_proj(x, mod3, gamma, w, *, layer, tm, mod_spec, rope_tables=None, seq_len=None):
    n = x.shape[0]
    rope = rope_tables is not None
    tables, table_specs = (), []
    if rope:
        per_seq = seq_len // tm
        tables = tuple(rope_tables)
        table_specs = [pl.BlockSpec((tm, LANES), lambda i: (i % per_seq, 0))] * 2
    return pl.pallas_call(
        functools.partial(_in_proj_kernel, rope=rope),
        grid=(n // tm,),
        in_specs=[pl.BlockSpec((tm, D_MODEL), lambda i: (i, 0)),
                  mod_spec,
                  _resident((1, D_MODEL)),
                  _layer_resident((D_MODEL, D_INP), layer)] + table_specs,
        out_specs=[pl.BlockSpec((tm, D_INP), lambda i: (i, 0)), pl.BlockSpec((tm, QV_WIDTH), lambda i: (i, 0)),
                   pl.BlockSpec((KV_WIDTH, tm), lambda i: (0, i))],
        out_shape=[jax.ShapeDtypeStruct((n, D_INP), F32), jax.ShapeDtypeStruct((n, QV_WIDTH), BF16),
                   jax.ShapeDtypeStruct((KV_WIDTH, n), BF16)],
        compiler_params=_cparams("parallel"),
        name="in_proj",
    )(x, mod3, gamma, w, *tables)


def _gelu_tanh(x):
    return 0.5 * x * (1.0 + jnp.tanh(math.sqrt(2.0 / math.pi) * (x + 0.044715 * (x * x * x))))


def _out_proj_kernel(x_ref, mod_ref, g_ref, yssd_ref, oatt_ref, ys5_ref, u_ref, d_ref, wglu_ref, bglu_ref,
                     w_ref, o_ref, mix_scr):
    y5 = ys5_ref[...] + d_ref[...] * u_ref[...]
    g = _gelu_tanh(y5)
    s5 = g * jax.nn.sigmoid(_dot(g.astype(BF16), wglu_ref[...]) + bglu_ref[...])
    mix_scr[:, 0:SSD_WIDTH] = yssd_ref[...]
    mix_scr[:, SSD_WIDTH:SSD_WIDTH + ATT_WIDTH] = oatt_ref[...]
    mix_scr[:, SSD_WIDTH + ATT_WIDTH:] = s5.astype(BF16)
    gate = mod_ref[0][:, 2 * D_MODEL:3 * D_MODEL]
    o_ref[...] = x_ref[...] + gate * _rms(_dot(mix_scr[...], w_ref[...]), g_ref[...])


def _out_proj(x, mod3, gamma, y_ssd, o_att, y_s5, proj, s5_d, w_glu, b_glu, w_out, *, layer, tm, mod_spec):
    n = x.shape[0]
    row = lambda i: (i, 0)
    return pl.pallas_call(
        _out_proj_kernel,
        grid=(n // tm,),
        in_specs=[pl.BlockSpec((tm, D_MODEL), row),
                  mod_spec,
                  _resident((1, D_MODEL)),
                  pl.BlockSpec((tm, SSD_WIDTH), row),
                  pl.BlockSpec((tm, ATT_WIDTH), row),
                  pl.BlockSpec((tm, S5_WIDTH), row),
                  pl.BlockSpec((tm, S5_WIDTH), lambda i: (i, C_U // S5_WIDTH)),
                  _resident((1, S5_WIDTH)),
                  _layer_resident((S5_WIDTH, S5_WIDTH), layer),
                  _resident((1, S5_WIDTH)),
                  _layer_resident((MIX_WIDTH, D_MODEL), layer)],
        out_specs=pl.BlockSpec((tm, D_MODEL), row),
        out_shape=jax.ShapeDtypeStruct((n, D_MODEL), F32),
        scratch_shapes=[pltpu.VMEM((tm, MIX_WIDTH), BF16)],
        compiler_params=_cparams("parallel"),
        name="out_proj",
    )(x, mod3, gamma, y_ssd, o_att, y_s5, proj, s5_d, w_glu, b_glu, w_out)


def _ffn_up_kernel(x_ref, mod_ref, gpre_ref, wg_ref, wu_ref, act_ref, h_scr):
    j = pl.program_id(1)

    def gated(h):
        return (_silu(_dot(h, wg_ref[...])) * _dot(h, wu_ref[...])).astype(BF16)

    @pl.when(j == 0)
    def _():
        mod = mod_ref[0]
        sh = mod[:, 3 * D_MODEL:4 * D_MODEL]
        sc = mod[:, 4 * D_MODEL:5 * D_MODEL]
        h = (_rms(x_ref[...], gpre_ref[...]) * (1.0 + sc) + sh).astype(BF16)
        h_scr[...] = h
        act_ref[...] = gated(h)

    @pl.when(j > 0)
    def _():
        act_ref[...] = gated(h_scr[...])


def _ffn_down_kernel(x_ref, mod_ref, gpost_ref, act_ref, wo_ref, o_ref):
    gate = mod_ref[0][:, 5 * D_MODEL:6 * D_MODEL]
    o_ref[...] = x_ref[...] + gate * _rms(_dot(act_ref[...], wo_ref[...]), gpost_ref[...])


def _ffn(x, mod3, g_pre, g_post, w_in, w_out, *, layer, tm_up, tf, tm_down, mod_spec_up, mod_spec_down):
    n = x.shape[0]
    nf = D_FF // tf
    act = pl.pallas_call(
        _ffn_up_kernel,
        grid=(n // tm_up, nf),
        in_specs=[pl.BlockSpec((tm_up, D_MODEL), lambda i, j: (i, 0)),
                  mod_spec_up,
                  _resident((1, D_MODEL)),
                  pl.BlockSpec((None, D_MODEL, tf), lambda i, j: (layer, 0, j)),
                  pl.BlockSpec((None, D_MODEL, tf), lambda i, j: (layer, 0, j + nf))],
        out_specs=pl.BlockSpec((tm_up, tf), lambda i, j: (i, j)),
        out_shape=jax.ShapeDtypeStruct((n, D_FF), BF16),
        scratch_shapes=[pltpu.VMEM((tm_up, D_MODEL), BF16)],
        compiler_params=_cparams("parallel", "arbitrary"),
        name="ffn_up",
    )(x, mod3, g_pre, w_in, w_in)
    return pl.pallas_call(
        _ffn_down_kernel,
        grid=(n // tm_down,),
        in_specs=[pl.BlockSpec((tm_down, D_MODEL), lambda i: (i, 0)),
                  mod_spec_down,
                  _resident((1, D_MODEL)),
                  pl.BlockSpec((tm_down, D_FF), lambda i: (i, 0)),
                  _layer_resident((D_FF, D_MODEL), layer)],
        out_specs=pl.BlockSpec((tm_down, D_MODEL), lambda i: (i, 0)),
        out_shape=jax.ShapeDtypeStruct((n, D_MODEL), F32),
        compiler_params=_cparams("parallel"),
        name="ffn_down",
    )(x, mod3, g_post, act, w_out)


def _attend(q, parts, sink_ref):
    rows = q.shape[0]
    srow = lax.broadcasted_iota(jnp.int32, (ATT_GROUP * rows, 1), 0)
    outs = [None] * ATT_HEADS

    def masked_scores(kv):
        heads = range(kv * ATT_GROUP, (kv + 1) * ATT_GROUP)
        q3 = jnp.concatenate([q[:, h * HEAD_DIM:(h + 1) * HEAD_DIM] for h in heads], axis=0)
        scores = []
        for kt, _, mask in parts:
            s = _dot(q3, kt[kv * HEAD_DIM:(kv + 1) * HEAD_DIM, :])
            scores.append(s if mask is None else jnp.where(mask, s, NEG))
        return scores

    def softmax_numerator(kv, scores):
        heads = range(kv * ATT_GROUP, (kv + 1) * ATT_GROUP)
        sink = jnp.full((ATT_GROUP * rows, 1), sink_ref[heads[-1]] * LOG2E, F32)
        for i in range(ATT_GROUP - 2, -1, -1):
            sink = jnp.where(srow < (i + 1) * rows, sink_ref[heads[i]] * LOG2E, sink)
        blocks = [s[:, j:j + LANES] for s in scores for j in range(0, s.shape[1], LANES)]
        m = jnp.maximum(sink, jnp.max(functools.reduce(jnp.maximum, blocks), axis=-1, keepdims=True))
        p = jnp.concatenate([jnp.exp2(s - m).astype(BF16) for s in scores], axis=1)
        return p, jnp.exp2(sink - m)

    def weighted_values(kv, p, sink_term):
        lo = kv * HEAD_DIM
        vlo = (lo // LANES) * LANES
        v_first = lo == vlo
        v128 = jnp.concatenate([v[:, vlo:vlo + LANES] for _, v, _ in parts], axis=0)
        lane = lax.broadcasted_iota(jnp.int32, v128.shape, 1)
        keep = (lane < HEAD_DIM) if v_first else (lane >= HEAD_DIM)
        acc = _dot(p, jnp.where(keep, v128, jnp.ones_like(v128)))
        ocol, dcol = (0, HEAD_DIM) if v_first else (HEAD_DIM, 0)
        o3 = acc[:, ocol:ocol + HEAD_DIM] / (acc[:, dcol:dcol + 1] + sink_term)
        for i in range(ATT_GROUP):
            outs[kv * ATT_GROUP + i] = o3[i * rows:(i + 1) * rows]

    scores = {0: masked_scores(0)}
    probs = {}
    for step in range(ATT_KV + 2):
        if step + 1 < ATT_KV:
            scores[step + 1] = masked_scores(step + 1)
        if step - 1 in probs:
            weighted_values(step - 1, *probs.pop(step - 1))
        if step in scores:
            probs[step] = softmax_numerator(step, scores.pop(step))
    return jnp.concatenate(outs, axis=-1)


def _ctx_attn_kernel(sink_ref, q_ref, k_ref, v_ref, o_ref):
    o_ref[...] = _attend(q_ref[...], [(k_ref[...], v_ref[...], None)], sink_ref).astype(o_ref.dtype)


def _ctx_attention(qv, kt, sink, *, n_seq, seq_len):
    vcol = ATT_WIDTH // KV_WIDTH
    blk = ATT_BLOCK
    nb = seq_len // blk
    return pl.pallas_call(
        _ctx_attn_kernel,
        grid=(n_seq, nb),
        in_specs=[pl.BlockSpec(memory_space=pltpu.SMEM),
                  pl.BlockSpec((blk, ATT_WIDTH), lambda b, n: (b * nb + n, 0)),
                  pl.BlockSpec((KV_WIDTH, seq_len), lambda b, n: (0, b)),
                  pl.BlockSpec((seq_len, KV_WIDTH), lambda b, n: (b, vcol))],
        out_specs=pl.BlockSpec((blk, ATT_WIDTH), lambda b, n: (b * nb + n, 0)),
        out_shape=jax.ShapeDtypeStruct((n_seq * seq_len, ATT_WIDTH), BF16),
        compiler_params=_cparams("parallel", "parallel"),
        name="ctx_attention",
    )(sink, qv, kt, qv)


def _lat_attn_kernel(sink_ref, q_ref, kp_ref, kc_ref, kn_ref, vp_ref, vc_ref, vn_ref, ck_ref, cv_ref, o_ref):
    n = pl.program_id(1)
    nb = pl.num_programs(1)
    blk = ATT_BLOCK
    row = lax.broadcasted_iota(jnp.int32, (ATT_GROUP * blk, blk), 0) % blk
    col = lax.broadcasted_iota(jnp.int32, (ATT_GROUP * blk, blk), 1)
    mask_prev = jnp.logical_and(col >= row, n > 0)
    mask_next = jnp.logical_and(col <= row, n < nb - 1)
    parts = [(ck_ref[0], cv_ref[0], None),
             (kp_ref[...], vp_ref[...], mask_prev),
             (kc_ref[...], vc_ref[...], None),
             (kn_ref[...], vn_ref[...], mask_next)]
    o_ref[...] = _attend(q_ref[...], parts, sink_ref).astype(o_ref.dtype)


def _lat_attention(qv, kt, sink, ckt, cv, *, n_seq, seq_len):
    blk = ATT_BLOCK
    nb = seq_len // blk

    def cur(b, n):
        return b * nb + n

    def prev(b, n):
        return b * nb + jnp.maximum(n - 1, 0)

    def nxt(b, n):
        return b * nb + jnp.minimum(n + 1, nb - 1)

    vcol = ATT_WIDTH // KV_WIDTH
    return pl.pallas_call(
        _lat_attn_kernel,
        grid=(n_seq, nb),
        in_specs=[pl.BlockSpec(memory_space=pltpu.SMEM),
                  pl.BlockSpec((blk, ATT_WIDTH), lambda b, n: (cur(b, n), 0)),
                  pl.BlockSpec((KV_WIDTH, blk), lambda b, n: (0, prev(b, n))),
                  pl.BlockSpec((KV_WIDTH, blk), lambda b, n: (0, cur(b, n))),
                  pl.BlockSpec((KV_WIDTH, blk), lambda b, n: (0, nxt(b, n))),
                  pl.BlockSpec((blk, KV_WIDTH), lambda b, n: (prev(b, n), vcol)),
                  pl.BlockSpec((blk, KV_WIDTH), lambda b, n: (cur(b, n), vcol)),
                  pl.BlockSpec((blk, KV_WIDTH), lambda b, n: (nxt(b, n), vcol)),
                  pl.BlockSpec((1,) + ckt.shape[1:], lambda b, n: (b, 0, 0)),
                  pl.BlockSpec((1,) + cv.shape[1:], lambda b, n: (b, 0, 0))],
        out_specs=pl.BlockSpec((blk, ATT_WIDTH), lambda b, n: (cur(b, n), 0)),
        out_shape=jax.ShapeDtypeStruct((n_seq * seq_len, ATT_WIDTH), BF16),
        compiler_params=_cparams("parallel", "parallel"),
        name="lat_attention",
    )(sink, qv, kt, kt, kt, qv, qv, qv, ckt, cv)


def _rope_tables(seq_len):
    rows = seq_len // GRID_W
    row = np.repeat(np.arange(rows, dtype=np.float32), GRID_W)
    col = np.tile(np.arange(GRID_W, dtype=np.float32), rows)
    inv = np.float32(ROPE_BASE) ** (-np.arange(ROPE_PER_AXIS, dtype=np.float32) / np.float32(ROPE_PER_AXIS))
    ang = np.concatenate([row[:, None] * inv, col[:, None] * inv], axis=-1)
    cos, sin = np.cos(ang), np.sin(ang)
    cos128 = np.tile(cos, (1, 4))
    sin128 = np.tile(np.concatenate([-sin, sin], axis=-1), (1, 2))
    return jnp.asarray(cos128, F32), jnp.asarray(sin128, F32)


def _conv_silu(scr, prev_ref, cur_ref, next_ref, w_ref, b_ref, at_start, at_end):
    n = cur_ref.shape[0]
    h = SSD_HALO
    scr[0:h, :] = jnp.where(at_start, 0.0, prev_ref[...])
    scr[h:h + n, :] = cur_ref[...]
    scr[h + n:h + n + h, :] = jnp.where(at_end, 0.0, next_ref[...])
    return _silu(scr[h - 1:h - 1 + n, :] * w_ref[0:1, :] + scr[h:h + n, :] * w_ref[1:2, :]
                 + scr[h + 1:h + 1 + n, :] * w_ref[2:3, :] + b_ref[...])


def _softplus(x):
    return jnp.maximum(x, 0.0) + jnp.log1p(jnp.exp(-jnp.abs(x)))


def _ssd_bwd_kernel(xsp_ref, xsc_ref, xsn_ref, bcp_ref, bcc_ref, bcn_ref, dt_ref, h0_ref,
                    cwx_ref, cbx_ref, cwb_ref, cbb_ref, dtb_ref, alog_ref, tri_ref, exp_ref,
                    hstart_ref, hfin_ref, xbc_ref, xs_scr, bc_scr, h_scr):
    i = pl.program_id(1)
    q = SSD_CHUNK
    half = SSD_WIDTH // SSD_GROUPS
    sel_b = exp_ref[1]
    neg_a = -LOG2E * jnp.exp(alog_ref[...])
    at_start = i == pl.num_programs(1) - 1
    at_end = i == 0
    xbc_ref[:, 0:SSD_WIDTH] = _conv_silu(xs_scr, xsp_ref, xsc_ref, xsn_ref, cwx_ref, cbx_ref, at_start, at_end)
    xbc_ref[:, SSD_WIDTH:] = _conv_silu(bc_scr, bcp_ref, bcc_ref, bcn_ref, cwb_ref, cbb_ref, at_start, at_end)

    @pl.when(i == 0)
    def _():
        h_scr[...] = h0_ref[0]

    hb = h_scr[...]
    for j in reversed(range(xsc_ref.shape[0] // q)):
        rows = slice(j * q, (j + 1) * q)
        hstart_ref[0, j] = hb
        bm = xbc_ref[rows, SSD_WIDTH:SSD_WIDTH + SSD_GROUPS * SSD_STATE].astype(BF16)
        dtv = _softplus(dt_ref[rows, :] + dtb_ref[...])
        rev = _dot_sel_lhs(tri_ref[1], dtv * neg_a, 3)
        dt_w = _dot_sel_rhs(dtv, sel_b, 2)
        to_end = _dot_sel_rhs(jnp.exp2(rev[0:1, :] - rev), sel_b, 2)
        decay = _dot_sel_rhs(jnp.exp2(rev[0:SSD_HALO, :]), sel_b, 2)[0:1, :]
        xw = (xbc_ref[rows, 0:SSD_WIDTH] * dt_w * to_end).astype(BF16)
        upd = jnp.concatenate(
            [_dot_tn(bm[:, g * SSD_STATE:(g + 1) * SSD_STATE], xw[:, g * half:(g + 1) * half])
             for g in range(SSD_GROUPS)], axis=-1)
        hb = hb * decay + upd
    h_scr[...] = hb

    @pl.when(i == pl.num_programs(1) - 1)
    def _():
        hfin_ref[0] = hb.T


def _ssd_intra(xs, bc, dt_raw, dtb_ref, alog_ref, tri_ref, exp_ref, dvec_ref):
    q = SSD_CHUNK
    nh = SSD_HEADS
    gs = SSD_GROUPS * SSD_STATE
    bm = bc[:, 0:gs].astype(BF16)
    cm = bc[:, gs:2 * gs].astype(BF16)
    xs_bf = xs.astype(BF16)

    dtv = _softplus(dt_raw + dtb_ref[...])
    dta = dtv * (-LOG2E * jnp.exp(alog_ref[...]))
    lower, upper = tri_ref[0], tri_ref[1]
    cum = _dot_sel_lhs(lower, dta, 3)
    rev = _dot_sel_lhs(upper, dta, 3)
    dt_t = dtv.T
    dta_t = dta.T
    cum_t = _dot_sel_rhs(dta_t, upper, 3)
    rev_t = _dot_sel_rhs(dta_t, lower, 3)

    sel_f, sel_b = exp_ref[0], exp_ref[1]
    dtf_w = _dot_sel_rhs(dtv, sel_f, 2)
    ecum_w = _dot_sel_rhs(jnp.exp2(cum), sel_f, 2)
    erev_w = _dot_sel_rhs(jnp.exp2(rev), sel_b, 2)
    toend_w = _dot_sel_rhs(jnp.exp2(cum[q - 1:q, :] - cum), sel_f, 2)

    row = lax.broadcasted_iota(jnp.int32, (q, q), 0)
    col = lax.broadcasted_iota(jnp.int32, (q, q), 1)
    below = row > col
    log_dt_t = jnp.log2(dt_t)
    src_f = log_dt_t - cum_t
    src_b = log_dt_t - rev_t
    gmat = [_dot_nt(cm[:, g * SSD_STATE:(g + 1) * SSD_STATE], bm[:, g * SSD_STATE:(g + 1) * SSD_STATE])
            for g in range(SSD_GROUPS)]
    per_group = nh // SSD_GROUPS
    half = SSD_WIDTH // SSD_GROUPS
    ys = []
    for h in range(nh):
        e = jnp.where(below, cum[:, h:h + 1] + src_f[h:h + 1, :],
                      rev[:, nh + h:nh + h + 1] + src_b[nh + h:nh + h + 1, :])
        m = (gmat[h // per_group] * jnp.exp2(e)).astype(BF16)
        ys.append(_dot(m, xs_bf[:, h * SSD_HEAD_DIM:(h + 1) * SSD_HEAD_DIM]))
    cb = cm.astype(F32) * bm.astype(F32)
    g_diag = jnp.concatenate(
        [jnp.broadcast_to(jnp.sum(cb[:, g * SSD_STATE:(g + 1) * SSD_STATE], axis=-1, keepdims=True), (q, half))
         for g in range(SSD_GROUPS)], axis=-1)
    y = jnp.concatenate(ys, axis=-1) + (g_diag * dtf_w + dvec_ref[...]) * xs
    xw = (xs * dtf_w * toend_w).astype(BF16)
    upd = jnp.concatenate(
        [_dot_tn(bm[:, g * SSD_STATE:(g + 1) * SSD_STATE], xw[:, g * half:(g + 1) * half])
         for g in range(SSD_GROUPS)], axis=-1)
    return y, cm, ecum_w, erev_w, upd


def _ssd_carry(hf, hb_start, gz, intra, nw_ref):
    y, cm, ecum_w, erev_w, upd = intra
    half = SSD_WIDTH // SSD_GROUPS
    hf_bf = hf.astype(BF16)
    hb_bf = hb_start.astype(BF16)
    off_f = jnp.concatenate([_dot(cm[:, g * SSD_STATE:(g + 1) * SSD_STATE], hf_bf[:, g * half:(g + 1) * half])
                             for g in range(SSD_GROUPS)], axis=-1)
    off_b = jnp.concatenate([_dot(cm[:, g * SSD_STATE:(g + 1) * SSD_STATE], hb_bf[:, g * half:(g + 1) * half])
                             for g in range(SSD_GROUPS)], axis=-1)
    y = y + off_f * ecum_w + off_b * erev_w
    y = _rms(y * gz, nw_ref[...])
    return y, hf * ecum_w[SSD_CHUNK - 1:SSD_CHUNK, :] + upd


def _ssd_fwd_kernel(gz_ref, xs_ref, bc_ref, dt_ref, hb_ref, h0_ref,
                    dtb_ref, alog_ref, tri_ref, exp_ref, dvec_ref, nw_ref, y_ref, hfin_ref, h_scr):
    c = pl.program_id(1)
    q = SSD_CHUNK
    cps = xs_ref.shape[0] // q
    rows = [slice(j * q, (j + 1) * q) for j in range(cps)]

    def intra(j):
        return _ssd_intra(xs_ref[rows[j], :], bc_ref[rows[j], :], dt_ref[rows[j], :],
                          dtb_ref, alog_ref, tri_ref, exp_ref, dvec_ref)

    @pl.when(c == 0)
    def _():
        h_scr[...] = h0_ref[0]

    hf = h_scr[...]
    ahead = intra(0)
    for j in range(cps):
        cur = ahead
        if j + 1 < cps:
            ahead = intra(j + 1)
        y, hf = _ssd_carry(hf, hb_ref[0, j], gz_ref[rows[j], :], cur, nw_ref)
        y_ref[rows[j], :] = y.astype(y_ref.dtype)
    h_scr[...] = hf

    @pl.when(c == pl.num_programs(1) - 1)
    def _():
        hfin_ref[0] = hf.T


def _ssd_mixer(proj, h0_f, h0_b, consts, *, n_seq, seq_len):
    q = SSD_CHUNK
    nc = seq_len // q
    cps = max(c for c in range(1, SSD_CHUNKS_PER_STEP + 1) if nc % c == 0)
    ns = nc // cps
    rows = cps * q
    (cwx, cbx, cwb, cbb, dt_bias, a_log, tri, expand, dvec, nw) = consts
    per8 = rows // SSD_HALO
    last8 = proj.shape[0] // SSD_HALO - 1

    def raw_specs(width, col):
        step = lambda b, i: b * ns + ns - 1 - i
        return [pl.BlockSpec((SSD_HALO, width), lambda b, i: (jnp.maximum(step(b, i) * per8 - 1, 0), col)),
                pl.BlockSpec((rows, width), lambda b, i: (step(b, i), col)),
                pl.BlockSpec((SSD_HALO, width), lambda b, i: (jnp.minimum((step(b, i) + 1) * per8, last8), col))]

    def specs(chunk_of):
        def cur(b, i):
            return b * ns + chunk_of(i)

        return [pl.BlockSpec((rows, SSD_WIDTH), lambda b, i: (cur(b, i), 0)),
                pl.BlockSpec((rows, SSD_BC), lambda b, i: (cur(b, i), SSD_WIDTH // SSD_BC)),
                pl.BlockSpec((rows, DT_PAD), lambda b, i: (cur(b, i), C_DT // DT_PAD))], cur

    state_spec = pl.BlockSpec((1, SSD_STATE, SSD_WIDTH), lambda b, i: (b, 0, 0))
    final_spec = pl.BlockSpec((1, SSD_WIDTH, SSD_STATE), lambda b, i: (b, 0, 0))
    const_specs = [_resident(dt_bias.shape), _resident(a_log.shape), _resident(tri.shape),
                   _resident(expand.shape)]
    scratch = [pltpu.VMEM((SSD_STATE, SSD_WIDTH), F32)]

    data_specs, rcur = specs(lambda i: ns - 1 - i)
    hb_start, hb_fin, xbc = pl.pallas_call(
        _ssd_bwd_kernel,
        grid=(n_seq, ns),
        in_specs=(raw_specs(SSD_WIDTH, C_XS // SSD_WIDTH) + raw_specs(SSD_BC, C_BC // SSD_BC)
                  + [data_specs[2], state_spec]
                  + [_resident(cwx.shape), _resident(cbx.shape), _resident(cwb.shape), _resident(cbb.shape)]
                  + const_specs),
        out_specs=[pl.BlockSpec((1, cps, SSD_STATE, SSD_WIDTH), lambda b, i: (b, ns - 1 - i, 0, 0)), final_spec,
                   pl.BlockSpec((rows, SSD_WIDTH + SSD_BC), lambda b, i: (rcur(b, i), 0))],
        out_shape=[jax.ShapeDtypeStruct((n_seq, nc, SSD_STATE, SSD_WIDTH), F32),
                   jax.ShapeDtypeStruct((n_seq, SSD_WIDTH, SSD_STATE), F32),
                   jax.ShapeDtypeStruct((n_seq * seq_len, SSD_WIDTH + SSD_BC), F32)],
        scratch_shapes=[pltpu.VMEM((rows + 2 * SSD_HALO, SSD_WIDTH), F32),
                        pltpu.VMEM((rows + 2 * SSD_HALO, SSD_BC), F32)] + scratch,
        compiler_params=_cparams("parallel", "arbitrary"),
        name="ssd_backward_states",
    )(proj, proj, proj, proj, proj, proj, proj, h0_b, cwx, cbx, cwb, cbb, dt_bias, a_log, tri, expand)

    data_specs, cur = specs(lambda i: i)
    y, hf_fin = pl.pallas_call(
        _ssd_fwd_kernel,
        grid=(n_seq, ns),
        in_specs=([pl.BlockSpec((rows, SSD_WIDTH), lambda b, i: (cur(b, i), C_Z // SSD_WIDTH))] + data_specs
                  + [pl.BlockSpec((1, cps, SSD_STATE, SSD_WIDTH), lambda b, i: (b, i, 0, 0)), state_spec]
                  + const_specs + [_resident(dvec.shape), _resident(nw.shape)]),
        out_specs=[pl.BlockSpec((rows, SSD_WIDTH), lambda b, i: (b * ns + i, 0)), final_spec],
        out_shape=[jax.ShapeDtypeStruct((n_seq * seq_len, SSD_WIDTH), BF16),
                   jax.ShapeDtypeStruct((n_seq, SSD_WIDTH, SSD_STATE), F32)],
        scratch_shapes=scratch,
        compiler_params=_cparams("parallel", "arbitrary"),
        name="ssd_forward",
    )(proj, xbc, xbc, proj, hb_start, h0_f, dt_bias, a_log, tri, expand, dvec, nw)
    return y, hf_fin, hb_fin


def _ssd_consts(l, ssd_conv_w, ssd_conv_b, ssd_dt_bias, ssd_A_log, ssd_D, ssd_norm):
    q = SSD_CHUNK
    cw, cb = ssd_conv_w[l], ssd_conv_b[l]
    pad = DT_PAD - 2 * SSD_HEADS
    dt_bias = jnp.pad(ssd_dt_bias[l].reshape(1, -1), ((0, 0), (0, pad)))
    a_log = jnp.pad(ssd_A_log[l].reshape(1, -1), ((0, 0), (0, pad)))
    r = np.arange(q)
    tri = jnp.asarray(np.stack([r[None, :] <= r[:, None], r[None, :] >= r[:, None]]), BF16)
    head_of_col = np.arange(SSD_WIDTH) // SSD_HEAD_DIM
    j = np.arange(DT_PAD)
    expand = jnp.asarray(np.stack([j[:, None] == head_of_col[None, :],
                                   j[:, None] == head_of_col[None, :] + SSD_HEADS]), BF16)
    dvec = jnp.repeat(ssd_D[l], SSD_HEAD_DIM).reshape(1, SSD_WIDTH)
    return (cw[:, :SSD_WIDTH], cb[:SSD_WIDTH].reshape(1, -1), cw[:, SSD_WIDTH:], cb[SSD_WIDTH:].reshape(1, -1),
            dt_bias, a_log, tri, expand, dvec, ssd_norm[l].reshape(1, -1))


def _ssd_state_in(s):
    b = s.shape[0]
    return jnp.transpose(s, (0, 3, 1, 2)).reshape(b, SSD_STATE, SSD_WIDTH)


def _ssd_state_out(s):
    return s.reshape(s.shape[0], SSD_HEADS, SSD_HEAD_DIM, SSD_STATE)


def _s5_table_kernel(pw_ref, ct_ref, bb_ref, bbt_ref, sel_ref, tile_ref, rhs_ref, woff_ref):
    t = S5_T
    c = S5_CH
    tile = tile_ref[...]
    spread = lambda x, sel: _dot_sel_rhs(x, sel, 2)
    krows, offs, cols = [], [], []
    for d in range(2):
        pw_re, pw_im = pw_ref[0, 2 * d], pw_ref[0, 2 * d + 1]
        c_re = spread(ct_ref[0, :, (2 * d) * c:(2 * d + 1) * c], tile)
        c_im = spread(ct_ref[0, :, (2 * d + 1) * c:(2 * d + 2) * c], tile)
        b_re = spread(bb_ref[0, :, (2 * d) * c:(2 * d + 1) * c], tile)
        b_im = spread(bb_ref[0, :, (2 * d + 1) * c:(2 * d + 2) * c], tile)
        bt_re = bbt_ref[0, (2 * d) * c:(2 * d + 1) * c, :]
        bt_im = bbt_ref[0, (2 * d + 1) * c:(2 * d + 2) * c, :]
        up = (spread(pw_re, sel_ref[0]), spread(pw_im, sel_ref[0]))
        down = (spread(pw_re, sel_ref[1]), spread(pw_im, sel_ref[1]))
        (q_re, q_im), state = ((up, down), (down, up))[d]
        g_re = q_re * c_re - q_im * c_im
        g_im = q_re * c_im + q_im * c_re
        krows.append(_dot_wide(bt_re, g_re) - _dot_wide(bt_im, g_im))
        lb_re, lb_im = pw_re[:, 1:2], pw_im[:, 1:2]
        r_re = q_re * lb_re - q_im * lb_im
        r_im = q_re * lb_im + q_im * lb_re
        offs.append((r_re * c_re - r_im * c_im, -(r_re * c_im + r_im * c_re)))
        s_re, s_im = state
        cols.append(((s_re * b_re - s_im * b_im).T, (s_re * b_im + s_im * b_re).T))
    k_f, k_b = krows
    for s in range(t):
        r = t - 1 - s
        f = k_f if s == 0 else jnp.concatenate([jnp.zeros((c, c * s), F32), k_f[:, :S5_TW - c * s]], axis=1)
        b = k_b if r == 0 else jnp.concatenate([k_b[:, c * r:], jnp.zeros((c, c * r), F32)], axis=1)
        rhs_ref[0, s * c:(s + 1) * c, 0:S5_TW] = (f + b).astype(BF16)
    rhs_ref[0, :, S5_TW:] = jnp.concatenate([cols[0][0], cols[1][0], cols[0][1], cols[1][1]], axis=1).astype(BF16)
    woff_ref[0] = jnp.concatenate([offs[0][0], offs[1][0], offs[0][1], offs[1][1]], axis=0).astype(BF16)


def _s5_tables(s5_A_re, s5_A_im, s5_log_dt, s5_B_re, s5_B_im, s5_C_re, s5_C_im):
    t = S5_T
    n = N_LAYERS * S5_GROUPS
    step = jnp.exp(s5_log_dt)[..., None]
    k = jnp.arange(LANES, dtype=F32)
    keep = k <= t
    kk = jnp.where(keep, k, 0.0)
    mag = jnp.exp(kk * (s5_A_re * step)[..., None])
    ang = kk * (s5_A_im * step)[..., None]
    pw_re = jnp.where(keep, mag * jnp.cos(ang), 0.0)
    pw_im = jnp.where(keep, mag * jnp.sin(ang), 0.0)
    lb_re, lb_im = pw_re[..., 1], pw_im[..., 1]
    den = s5_A_re * s5_A_re + s5_A_im * s5_A_im
    r_re = ((lb_re - 1.0) * s5_A_re + lb_im * s5_A_im) / den
    r_im = (lb_im * s5_A_re - (lb_re - 1.0) * s5_A_im) / den
    b_re, b_im = s5_B_re[:, None], s5_B_im[:, None]
    bb_re = r_re[..., None] * b_re - r_im[..., None] * b_im
    bb_im = r_re[..., None] * b_im + r_im[..., None] * b_re
    c_re = jnp.swapaxes(s5_C_re, -1, -2)
    c_im = jnp.swapaxes(s5_C_im, -1, -2)

    def pack(re, im, axis):
        return jnp.concatenate([re[:, 0], im[:, 0], re[:, 1], im[:, 1]], axis=axis)

    pw = jnp.stack([pw_re[:, 0], pw_im[:, 0], pw_re[:, 1], pw_im[:, 1]], axis=2).reshape(n, 4, S5_P, LANES)
    ct = pack(c_re, c_im, -1).reshape(n, S5_P, 4 * S5_CH)
    bb = pack(bb_re, bb_im, -1)
    bbt = jnp.swapaxes(bb, -1, -2).reshape(n, 4 * S5_CH, S5_P)
    bb = bb.reshape(n, S5_P, 4 * S5_CH)
    lam_t = jnp.concatenate([pw_re[:, 0, :, :, t], pw_re[:, 1, :, :, t], pw_im[:, 0, :, :, t],
                             pw_im[:, 1, :, :, t]], axis=-1).reshape(n, 1, S5_ST)
    jt = np.arange(S5_TW) // S5_CH
    lane_k = np.arange(LANES)[:, None]
    sel = jnp.asarray(np.stack([lane_k == f[None, :] for f in (jt, t - 1 - jt)]), BF16)
    tile = jnp.asarray(np.arange(S5_CH)[:, None] == (np.arange(S5_TW) % S5_CH)[None, :], BF16)
    grp = lambda shape: pl.BlockSpec((1,) + shape, lambda i: (i,) + (0,) * len(shape))
    rhs, woff = pl.pallas_call(
        _s5_table_kernel,
        grid=(n,),
        in_specs=[grp((4, S5_P, LANES)), grp((S5_P, 4 * S5_CH)), grp((S5_P, 4 * S5_CH)), grp((4 * S5_CH, S5_P)),
                  _resident(sel.shape), _resident(tile.shape)],
        out_specs=[grp((S5_TW, S5_TW + S5_ST)), grp((S5_ST, S5_TW))],
        out_shape=[jax.ShapeDtypeStruct((n, S5_TW, S5_TW + S5_ST), BF16),
                   jax.ShapeDtypeStruct((n, S5_ST, S5_TW), BF16)],
        compiler_params=_cparams("parallel"),
        name="s5_tables",
    )(pw, ct, bb, bbt, sel, tile)
    return rhs, woff, lam_t


def _unit_transpose(vs, masks, axis, unit):
    size = vs[0].shape[axis]
    cur = list(vs)
    for d in (1, 2, 4):
        nxt = list(cur)
        for a in range(S5_OCT):
            if a & d:
                continue
            b = a | d
            nxt[a] = jnp.where(masks[d], cur[a], pltpu.roll(cur[b], unit * d, axis=axis))
            nxt[b] = jnp.where(masks[d], pltpu.roll(cur[a], size - unit * d, axis=axis), cur[b])
        cur = nxt
    return cur


def _s5_kernel(u_ref, rhs_ref, woff_ref, lam_ref, h0_ref, y_ref, hfin_ref,
               dst_scr, y_scr, sre_scr, sim_scr, hfre_scr, hfim_scr, hbre_scr, hbim_scr, *, nc, rb):
    t = S5_T
    rows = dst_scr.shape[1]
    sb = rows // nc
    tiles = t // S5_OCT
    nr = rb // S5_OCT
    tile_shape = (nr, S5_OCT, LANES)
    seg = lax.broadcasted_iota(jnp.int32, tile_shape, 2) // S5_CH
    sub = lax.broadcasted_iota(jnp.int32, tile_shape, 1)
    lane_masks = {d: (seg & d) == 0 for d in (1, 2, 4)}
    sub_masks = {d: (sub & d) == 0 for d in (1, 2, 4)}

    def tile_rows(i, r, tq):
        return pl.ds((i * rb + r) * tiles + tq, nr, stride=S5_OCT * tiles)

    def gather(i, carry):
        r0 = pl.multiple_of(i * rb, rb)
        for tq in range(tiles):
            by_chunk = [u_ref[tile_rows(i, r, tq)] for r in range(S5_OCT)]
            by_time = _unit_transpose(by_chunk, sub_masks, 1, 1)
            for gl, v in enumerate(_unit_transpose(by_time, lane_masks, 2, S5_CH)):
                dst_scr[gl, pl.ds(r0, rb), tq * LANES:(tq + 1) * LANES] = v.reshape(rb, LANES).astype(BF16)
        return carry

    lax.fori_loop(0, rows // rb, gather, 0)

    for gl in range(S5_OCT):
        z = _dot(dst_scr[gl], rhs_ref[gl])
        y_scr[gl] = z[:, 0:S5_TW]
        sre_scr[gl] = z[:, S5_TW:S5_TW + 2 * S5_P]
        sim_scr[gl] = z[:, S5_TW + 2 * S5_P:]

    fwd_lane = lax.broadcasted_iota(jnp.int32, (sb, 2 * S5_P), 1) < S5_P
    a_re = [lam_ref[gl][:, 0:2 * S5_P] for gl in range(S5_OCT)]
    a_im = [lam_ref[gl][:, 2 * S5_P:] for gl in range(S5_OCT)]

    def step(i, carry):
        fi = pl.ds(i, sb, stride=nc)
        bj = pl.ds(nc - 1 - i, sb, stride=nc)
        out = []
        for gl in range(S5_OCT):
            h_re, h_im = carry[2 * gl], carry[2 * gl + 1]
            hfre_scr[gl, fi, :] = h_re
            hfim_scr[gl, fi, :] = h_im
            hbre_scr[gl, bj, :] = h_re
            hbim_scr[gl, bj, :] = h_im
            s_re = jnp.where(fwd_lane, sre_scr[gl, fi, :], sre_scr[gl, bj, :])
            s_im = jnp.where(fwd_lane, sim_scr[gl, fi, :], sim_scr[gl, bj, :])
            out.append(a_re[gl] * h_re - a_im[gl] * h_im + s_re)
            out.append(a_re[gl] * h_im + a_im[gl] * h_re + s_im)
        return tuple(out)

    init = []
    for gl in range(S5_OCT):
        h0 = h0_ref[gl, 0]
        init += [h0[:, 0:2 * S5_P], h0[:, 2 * S5_P:]]
    fin = lax.fori_loop(0, nc, step, tuple(init), unroll=2 if nc % 2 == 0 else 1)

    all_fwd = lax.broadcasted_iota(jnp.int32, (rows, 2 * S5_P), 1) < S5_P
    for gl in range(S5_OCT):
        hfin_ref[gl, 0] = jnp.concatenate([fin[2 * gl], fin[2 * gl + 1]], axis=-1)
        hin = jnp.concatenate([jnp.where(all_fwd, hfre_scr[gl], hbre_scr[gl]),
                               jnp.where(all_fwd, hfim_scr[gl], hbim_scr[gl])], axis=-1).astype(BF16)
        y_scr[gl] += _dot(hin, woff_ref[gl])

    def scatter(i, carry):
        r0 = pl.multiple_of(i * rb, rb)
        for tq in range(tiles):
            by_group = [y_scr[gl, pl.ds(r0, rb), tq * LANES:(tq + 1) * LANES].reshape(tile_shape)
                        for gl in range(S5_OCT)]
            by_time = _unit_transpose(by_group, lane_masks, 2, S5_CH)
            for r, v in enumerate(_unit_transpose(by_time, sub_masks, 1, 1)):
                y_ref[tile_rows(i, r, tq)] = v
        return carry

    lax.fori_loop(0, rows // rb, scatter, 0)


def _s5_mixer(proj, tables, layer, h0, *, n_seq, seq_len):
    rhs, woff, lam_t = tables
    t = S5_T
    nc = seq_len // t
    sb = max(1, min(n_seq, S5_MAX_ROWS // nc))
    assert n_seq % sb == 0
    nbs = n_seq // sb
    rows = sb * nc
    rb = min(rows, 32)
    noct = S5_GROUPS // S5_OCT
    ucol = C_U // LANES
    tab = lambda shape: pl.BlockSpec((S5_OCT,) + shape, lambda j, s: (layer * noct + j, 0, 0))
    state = pl.BlockSpec((S5_OCT, 1, sb, S5_ST), lambda j, s: (j, s, 0, 0))
    rows_scr = lambda dt: pltpu.VMEM((S5_OCT, rows, 2 * S5_P), dt)
    n_tok = n_seq * seq_len
    tok_block = (sb * seq_len // S5_OCT, S5_OCT, LANES)
    y, hfin = pl.pallas_call(
        functools.partial(_s5_kernel, nc=nc, rb=rb),
        grid=(noct, nbs),
        in_specs=[pl.BlockSpec(tok_block, lambda j, s: (s, 0, ucol + j)),
                  tab(rhs.shape[1:]), tab(woff.shape[1:]), tab(lam_t.shape[1:]), state],
        out_specs=[pl.BlockSpec(tok_block, lambda j, s: (s, 0, j)), state],
        out_shape=[jax.ShapeDtypeStruct((n_tok // S5_OCT, S5_OCT, S5_WIDTH), F32),
                   jax.ShapeDtypeStruct((S5_GROUPS, nbs, sb, S5_ST), F32)],
        scratch_shapes=[pltpu.VMEM((S5_OCT, rows, S5_TW), BF16), pltpu.VMEM((S5_OCT, rows, S5_TW), F32),
                        rows_scr(F32), rows_scr(F32), rows_scr(F32), rows_scr(F32), rows_scr(F32), rows_scr(F32)],
        compiler_params=_cparams("parallel", "parallel"),
        name="s5_mixer",
    )(proj.reshape(n_tok // S5_OCT, S5_OCT, D_INP), rhs, woff, lam_t, h0.reshape(S5_GROUPS, nbs, sb, S5_ST))
    return y.reshape(n_tok, S5_WIDTH), hfin.reshape(S5_GROUPS, n_seq, S5_ST)


def _s5_state_in(re, im):
    b = re.shape[0]
    x = jnp.stack([re, im], axis=1)
    return jnp.transpose(x, (3, 0, 1, 2, 4)).reshape(S5_GROUPS, b, S5_ST)


def _s5_state_out(h):
    n_seq = h.shape[1]
    x = h.reshape(S5_GROUPS, n_seq, 2, 2, S5_P)
    x = jnp.transpose(x, (2, 1, 3, 0, 4))
    return x[0], x[1]


def kernel(x_prompt, x_sample, c, cache_k, cache_v, state_ssd, state_s5_re, state_s5_im, c_ctx, w_ada, b_ada, norm_mix_pre, norm_mix_post, norm_ffn_pre, norm_ffn_post, w_in, w_out, ssd_conv_w, ssd_conv_b, ssd_dt_bias, ssd_A_log, ssd_D, ssd_norm, attn_sink, s5_A_re, s5_A_im, s5_log_dt, s5_B_re, s5_B_im, s5_C_re, s5_C_im, s5_D, s5_w_glu, s5_b_glu, w_ffn_in, w_ffn_out):
    nb_ctx, len_ctx, _ = x_prompt.shape
    nb_lat, len_lat, _ = x_sample.shape
    n_ctx = nb_ctx * len_ctx
    n_lat = nb_lat * len_lat
    tm = 512
    mod_rows = -(-(1 + nb_lat) // 8) * 8
    cvecs = jnp.concatenate([c_ctx[None, :], c, jnp.zeros((mod_rows - 1 - nb_lat, D_MODEL), F32)], axis=0)
    mods = _ada(cvecs, w_ada, b_ada)
    mod_geom = ((n_ctx, 0), (len_lat, 1))
    token_sets = tuple(dict(mod_spec=_mod_spec(tm, *g)) for g in mod_geom)
    tm_up = min(FFN_UP_ROWS, n_ctx, len_lat)
    rope_args = (dict(), dict(rope_tables=_rope_tables(len_lat), seq_len=len_lat))

    s5_tab = _s5_tables(s5_A_re, s5_A_im, s5_log_dt, s5_B_re, s5_B_im, s5_C_re, s5_C_im)
    w_in_b = w_in.astype(BF16)
    w_in_p = jnp.concatenate(
        [w_in_b[:, :, O_Q:O_K], w_in_b[:, :, O_Z:O_DT], w_in_b[:, :, O_K:O_END], w_in_b[:, :, O_DT:O_Q],
         jnp.zeros((N_LAYERS, D_MODEL, DT_PAD - 2 * SSD_HEADS), BF16)], axis=2)
    w_out_b = w_out.astype(BF16)
    w_glu_b = s5_w_glu.astype(BF16)
    w_ffn_in_b = w_ffn_in.astype(BF16)
    w_ffn_out_b = w_ffn_out.astype(BF16)
    zeros_ssd = jnp.zeros((nb_ctx, SSD_STATE, SSD_WIDTH), F32)
    zeros_s5 = jnp.zeros((S5_GROUPS, nb_ctx, S5_ST), F32)

    xs = [x_prompt.reshape(n_ctx, D_MODEL), x_sample.reshape(n_lat, D_MODEL)]
    new_k, new_v, new_ssd, new_s5_re, new_s5_im = [], [], [], [], []
    for l in range(N_LAYERS):
        mod3 = mods[l].reshape(mod_rows, 1, 6 * D_MODEL)
        proj, qv, kt = zip(*[_in_proj(x, mod3, norm_mix_pre[l].reshape(1, -1), w_in_p, layer=l, tm=tm, **ts, **ra)
                             for x, ts, ra in zip(xs, token_sets, rope_args)])

        sink = attn_sink[l]
        ckt = jnp.swapaxes(cache_k[:, l].reshape(nb_lat, -1, KV_WIDTH), 1, 2).astype(BF16)
        cv = cache_v[:, l].reshape(nb_lat, -1, KV_WIDTH).astype(BF16)
        o_att = [_ctx_attention(qv[0], kt[0], sink, n_seq=nb_ctx, seq_len=len_ctx),
                 _lat_attention(qv[1], kt[1], sink, ckt, cv, n_seq=nb_lat, seq_len=len_lat)]

        consts = _ssd_consts(l, ssd_conv_w, ssd_conv_b, ssd_dt_bias, ssd_A_log, ssd_D, ssd_norm)
        y_ctx, hf_ctx, hb_ctx = _ssd_mixer(proj[0], zeros_ssd, zeros_ssd, consts, n_seq=nb_ctx, seq_len=len_ctx)
        y_lat, _, _ = _ssd_mixer(proj[1], _ssd_state_in(state_ssd[:, l, 0]), _ssd_state_in(state_ssd[:, l, 1]),
                                 consts, n_seq=nb_lat, seq_len=len_lat)
        y_ssd = [y_ctx, y_lat]

        s5_ctx, s5_fin = _s5_mixer(proj[0], s5_tab, l, zeros_s5, n_seq=nb_ctx, seq_len=len_ctx)
        s5_lat, _ = _s5_mixer(proj[1], s5_tab, l, _s5_state_in(state_s5_re[:, l], state_s5_im[:, l]),
                              n_seq=nb_lat, seq_len=len_lat)
        y_s5 = [s5_ctx, s5_lat]

        xs = [_out_proj(x, mod3, norm_mix_post[l].reshape(1, -1), y_ssd[i], o_att[i], y_s5[i], proj[i],
                        s5_D[l].reshape(1, -1), w_glu_b, s5_b_glu[l].reshape(1, -1), w_out_b,
                        layer=l, tm=tm, **ts)
              for i, (x, ts) in enumerate(zip(xs, token_sets))]
        xs = [_ffn(x, mod3, norm_ffn_pre[l].reshape(1, -1), norm_ffn_post[l].reshape(1, -1),
                   w_ffn_in_b, w_ffn_out_b, layer=l, tm_up=tm_up, tf=FFN_UP_COLS, tm_down=tm,
                   mod_spec_up=_mod_spec(tm_up, *g), mod_spec_down=_mod_spec(tm, *g))
              for x, g in zip(xs, mod_geom)]

        new_k.append(proj[0][:, C_K:C_K + KV_WIDTH].reshape(nb_ctx, len_ctx, ATT_KV, HEAD_DIM))
        new_v.append(proj[0][:, C_V:C_V + KV_WIDTH].reshape(nb_ctx, len_ctx, ATT_KV, HEAD_DIM))
        new_ssd.append(jnp.stack([_ssd_state_out(hf_ctx), _ssd_state_out(hb_ctx)], axis=1))
        re, im = _s5_state_out(s5_fin)
        new_s5_re.append(re)
        new_s5_im.append(im)

    y_prompt = xs[0].reshape(nb_ctx, len_ctx, D_MODEL)
    y_sample = xs[1].reshape(nb_lat, len_lat, D_MODEL)
    return (y_prompt, y_sample, jnp.stack(new_k, axis=1), jnp.stack(new_v, axis=1), jnp.stack(new_ssd, axis=1),
            jnp.stack(new_s5_re, axis=1), jnp.stack(new_s5_im, axis=1))
```

```python
import functools
import math

import jax
import jax.numpy as jnp
import numpy as np
from jax import lax
from jax.experimental import pallas as pl
from jax.experimental.pallas import tpu as pltpu

F32 = jnp.float32
BF16 = jnp.bfloat16

D_MODEL = 2048
N_LAYERS = 2
D_FF = 5632
GRID_W = 64
EPS = 1e-6
LANES = 128

SSD_HEADS = 12
SSD_HEAD_DIM = 64
SSD_WIDTH = SSD_HEADS * SSD_HEAD_DIM
SSD_GROUPS = 2
SSD_STATE = 64
SSD_BC = 2 * SSD_GROUPS * SSD_STATE
SSD_CHUNK = 128
SSD_CHUNKS_PER_STEP = 4
SSD_HALO = 8
ATT_HEADS = 12
ATT_KV = 4
ATT_GROUP = ATT_HEADS // ATT_KV
HEAD_DIM = 64
ATT_WIDTH = ATT_HEADS * HEAD_DIM
KV_WIDTH = ATT_KV * HEAD_DIM
ATT_BLOCK = 128
QV_WIDTH = ATT_WIDTH + KV_WIDTH
LOG2E = math.log2(math.e)
Q_SCALE = HEAD_DIM ** -0.5 * LOG2E
ROPE_PER_AXIS = HEAD_DIM // 4
ROPE_BASE = 10000.0
S5_GROUPS = 32
S5_CH = 16
S5_WIDTH = S5_GROUPS * S5_CH
S5_P = 64
S5_T = 32
S5_TW = S5_T * S5_CH
S5_ST = 4 * S5_P
S5_OCT = LANES // S5_CH
S5_MAX_ROWS = 256
MIX_WIDTH = SSD_WIDTH + ATT_WIDTH + S5_WIDTH

O_Z = 0
O_XS = SSD_WIDTH
O_BC = O_XS + SSD_WIDTH
O_DT = O_BC + SSD_BC
O_Q = O_DT + 2 * SSD_HEADS
O_K = O_Q + ATT_WIDTH
O_V = O_K + KV_WIDTH
O_U = O_V + KV_WIDTH
O_END = O_U + S5_WIDTH
C_Q = 0
C_Z = 768
C_XS = 1536
C_BC = 2304
C_K = 2560
C_V = 2816
C_U = 3072
C_DT = 3584
DT_PAD = 128
D_INP = C_DT + DT_PAD

FFN_UP_ROWS = 1024
FFN_UP_COLS = 512

NEG = -1e30
VMEM_LIMIT = 56 * 1024 * 1024


def _cparams(*sem):
    return pltpu.CompilerParams(dimension_semantics=sem, vmem_limit_bytes=VMEM_LIMIT)


def _resident(shape):
    nd = len(shape)
    return pl.BlockSpec(shape, lambda *_: (0,) * nd, pipeline_mode=pl.Buffered(1))


def _layer_resident(shape, layer, block=0):
    rest = (0,) * (len(shape) - 1)
    return pl.BlockSpec((None,) + shape, lambda *_: (layer, block) + rest, pipeline_mode=pl.Buffered(1))


def _rms(x, g):
    return x * lax.rsqrt(jnp.mean(x * x, axis=-1, keepdims=True) + EPS) * g


def _silu(x):
    return x * jax.nn.sigmoid(x)


def _dot(a, b):
    return jnp.dot(a, b, preferred_element_type=F32)


def _dot_wide(a, b):
    a_hi, a_lo = _split_bf16(a, 2)
    b_hi, b_lo = _split_bf16(b, 2)
    return _dot(a_hi, b_hi) + (_dot(a_hi, b_lo) + _dot(a_lo, b_hi))


def _dot_nt(a, b):
    return lax.dot_general(a, b, (((1,), (1,)), ((), ())), preferred_element_type=F32)


def _dot_tn(a, b):
    return lax.dot_general(a, b, (((0,), (0,)), ((), ())), preferred_element_type=F32)


def _split_bf16(x, parts):
    out = []
    for _ in range(parts):
        p = x.astype(BF16)
        out.append(p)
        x = x - p.astype(F32)
    return out


def _dot_sel_rhs(x, sel, parts):
    acc = None
    for p in _split_bf16(x, parts):
        t = _dot(p, sel)
        acc = t if acc is None else acc + t
    return acc


def _dot_sel_lhs(sel, x, parts):
    acc = None
    for p in _split_bf16(x, parts):
        t = _dot(sel, p)
        acc = t if acc is None else acc + t
    return acc


def _mod_spec(tm, rows_per_mod, first_row):
    return pl.BlockSpec((1, 1, 6 * D_MODEL), lambda i, *_: (first_row + (i * tm) // rows_per_mod, 0, 0))


def _ada_kernel(c_ref, w_ref, b_ref, o_ref):
    c = c_ref[...]
    o_ref[0] = _dot(_silu(c).astype(BF16), w_ref[0].astype(BF16)) + b_ref[0]


def _ada(cvecs, w_ada, b_ada):
    tn = 1024
    rows = cvecs.shape[0]
    return pl.pallas_call(
        _ada_kernel,
        grid=(N_LAYERS, 6 * D_MODEL // tn),
        in_specs=[pl.BlockSpec((rows, D_MODEL), lambda l, j: (0, 0)),
                  pl.BlockSpec((1, D_MODEL, tn), lambda l, j: (l, 0, j)),
                  pl.BlockSpec((1, 1, tn), lambda l, j: (l, 0, j))],
        out_specs=pl.BlockSpec((1, rows, tn), lambda l, j: (l, 0, j)),
        out_shape=jax.ShapeDtypeStruct((N_LAYERS, rows, 6 * D_MODEL), F32),
        compiler_params=_cparams("parallel", "parallel"),
        name="ada_mod",
    )(cvecs, w_ada, b_ada.reshape(N_LAYERS, 1, 6 * D_MODEL))


def _rope(x, cos, sin_signed, first_half):
    outs = []
    for j in range(x.shape[1] // LANES):
        xj = x[:, j * LANES:(j + 1) * LANES]
        partner = jnp.where(first_half, pltpu.roll(xj, LANES - HEAD_DIM // 2, axis=1),
                            pltpu.roll(xj, HEAD_DIM // 2, axis=1))
        outs.append(xj * cos + partner * sin_signed)
    return jnp.concatenate(outs, axis=-1)


def _in_proj_kernel(*refs, rope):
    if rope:
        x_ref, mod_ref, g_ref, w_ref, cos_ref, sin_ref, o_ref, qv_ref, kt_ref = refs
    else:
        x_ref, mod_ref, g_ref, w_ref, o_ref, qv_ref, kt_ref = refs
    mod = mod_ref[0]
    sh = mod[:, 0:D_MODEL]
    sc = mod[:, D_MODEL:2 * D_MODEL]
    h = _rms(x_ref[...], g_ref[...]) * (1.0 + sc) + sh
    o_ref[...] = _dot(h.astype(BF16), w_ref[...])
    o_ref[:, C_Z:C_Z + SSD_WIDTH] = _silu(o_ref[:, C_Z:C_Z + SSD_WIDTH])
    q = o_ref[:, C_Q:C_Q + ATT_WIDTH]
    k = o_ref[:, C_K:C_K + KV_WIDTH]
    if rope:
        lane = lax.broadcasted_iota(jnp.int32, (x_ref.shape[0], LANES), 1)
        first_half = (lane % HEAD_DIM) < (HEAD_DIM // 2)
        q = _rope(q, cos_ref[...], sin_ref[...], first_half)
        k = _rope(k, cos_ref[...], sin_ref[...], first_half)
    qv_ref[:, 0:ATT_WIDTH] = (q * Q_SCALE).astype(BF16)
    qv_ref[:, ATT_WIDTH:] = o_ref[:, C_V:C_V + KV_WIDTH].astype(BF16)
    kt_ref[...] = k.T.astype(BF16)


def _in_proj(x, mod3, gamma, w, *, layer, tm, mod_spec, rope_tables=None, seq_len=None):
    n = x.shape[0]
    rope = rope_tables is not None
    tables, table_specs = (), []
    if rope:
        per_seq = seq_len // tm
        tables = tuple(rope_tables)
        table_specs = [pl.BlockSpec((tm, LANES), lambda i: (i % per_seq, 0))] * 2
    return pl.pallas_call(
        functools.partial(_in_proj_kernel, rope=rope),
        grid=(n // tm,),
        in_specs=[pl.BlockSpec((tm, D_MODEL), lambda i: (i, 0)),
                  mod_spec,
                  _resident((1, D_MODEL)),
                  _layer_resident((D_MODEL, D_INP), layer)] + table_specs,
        out_specs=[pl.BlockSpec((tm, D_INP), lambda i: (i, 0)), pl.BlockSpec((tm, QV_WIDTH), lambda i: (i, 0)),
                   pl.BlockSpec((KV_WIDTH, tm), lambda i: (0, i))],
        out_shape=[jax.ShapeDtypeStruct((n, D_INP), F32), jax.ShapeDtypeStruct((n, QV_WIDTH), BF16),
                   jax.ShapeDtypeStruct((KV_WIDTH, n), BF16)],
        compiler_params=_cparams("parallel"),
        name="in_proj",
    )(x, mod3, gamma, w, *tables)


def _gelu_tanh(x):
    return 0.5 * x * (1.0 + jnp.tanh(math.sqrt(2.0 / math.pi) * (x + 0.044715 * (x * x * x))))


def _out_proj_kernel(x_ref, mod_ref, g_ref, yssd_ref, oatt_ref, ys5_ref, u_ref, d_ref, wglu_ref, bglu_ref,
                     w12_ref, w3_ref, o_ref, mix_scr):
    mix_scr[:, 0:SSD_WIDTH] = yssd_ref[...]
    mix_scr[:, SSD_WIDTH:] = oatt_ref[...]
    mix = _dot(mix_scr[...], w12_ref[...])
    y5 = ys5_ref[...] + d_ref[...] * u_ref[...]
    g = _gelu_tanh(y5)
    s5 = g * jax.nn.sigmoid(_dot(g.astype(BF16), wglu_ref[...]) + bglu_ref[...])
    mix = mix + _dot(s5.astype(BF16), w3_ref[...])
    gate = mod_ref[0][:, 2 * D_MODEL:3 * D_MODEL]
    o_ref[...] = x_ref[...] + gate * _rms(mix, g_ref[...])


def _out_proj(x, mod3, gamma, y_ssd, o_att, y_s5, proj, s5_d, w_glu, b_glu, w_out, *, layer, tm, mod_spec):
    n = x.shape[0]
    row = lambda i: (i, 0)
    return pl.pallas_call(
        _out_proj_kernel,
        grid=(n // tm,),
        in_specs=[pl.BlockSpec((tm, D_MODEL), row),
                  mod_spec,
                  _resident((1, D_MODEL)),
                  pl.BlockSpec((tm, SSD_WIDTH), row),
                  pl.BlockSpec((tm, ATT_WIDTH), row),
                  pl.BlockSpec((tm, S5_WIDTH), row),
                  pl.BlockSpec((tm, S5_WIDTH), lambda i: (i, C_U // S5_WIDTH)),
                  _resident((1, S5_WIDTH)),
                  _layer_resident((S5_WIDTH, S5_WIDTH), layer),
                  _resident((1, S5_WIDTH)),
                  _layer_resident((SSD_WIDTH + ATT_WIDTH, D_MODEL), layer, 0),
                  _layer_resident((S5_WIDTH, D_MODEL), layer, (SSD_WIDTH + ATT_WIDTH) // S5_WIDTH)],
        out_specs=pl.BlockSpec((tm, D_MODEL), row),
        out_shape=jax.ShapeDtypeStruct((n, D_MODEL), F32),
        scratch_shapes=[pltpu.VMEM((tm, SSD_WIDTH + ATT_WIDTH), BF16)],
        compiler_params=_cparams("parallel"),
        name="out_proj",
    )(x, mod3, gamma, y_ssd, o_att, y_s5, proj, s5_d, w_glu, b_glu, w_out, w_out)


def _ffn_up_kernel(x_ref, mod_ref, gpre_ref, wg_ref, wu_ref, act_ref, h_scr):
    j = pl.program_id(1)

    def gated(h):
        return (_silu(_dot(h, wg_ref[...])) * _dot(h, wu_ref[...])).astype(BF16)

    @pl.when(j == 0)
    def _():
        mod = mod_ref[0]
        sh = mod[:, 3 * D_MODEL:4 * D_MODEL]
        sc = mod[:, 4 * D_MODEL:5 * D_MODEL]
        h = (_rms(x_ref[...], gpre_ref[...]) * (1.0 + sc) + sh).astype(BF16)
        h_scr[...] = h
        act_ref[...] = gated(h)

    @pl.when(j > 0)
    def _():
        act_ref[...] = gated(h_scr[...])


def _ffn_down_kernel(x_ref, mod_ref, gpost_ref, act_ref, wo_ref, o_ref):
    gate = mod_ref[0][:, 5 * D_MODEL:6 * D_MODEL]
    o_ref[...] = x_ref[...] + gate * _rms(_dot(act_ref[...], wo_ref[...]), gpost_ref[...])


def _ffn(x, mod3, g_pre, g_post, w_in, w_out, *, layer, tm_up, tf, tm_down, mod_spec_up, mod_spec_down):
    n = x.shape[0]
    nf = D_FF // tf
    act = pl.pallas_call(
        _ffn_up_kernel,
        grid=(n // tm_up, nf),
        in_specs=[pl.BlockSpec((tm_up, D_MODEL), lambda i, j: (i, 0)),
                  mod_spec_up,
                  _resident((1, D_MODEL)),
                  pl.BlockSpec((None, D_MODEL, tf), lambda i, j: (layer, 0, j)),
                  pl.BlockSpec((None, D_MODEL, tf), lambda i, j: (layer, 0, j + nf))],
        out_specs=pl.BlockSpec((tm_up, tf), lambda i, j: (i, j)),
        out_shape=jax.ShapeDtypeStruct((n, D_FF), BF16),
        scratch_shapes=[pltpu.VMEM((tm_up, D_MODEL), BF16)],
        compiler_params=_cparams("parallel", "arbitrary"),
        name="ffn_up",
    )(x, mod3, g_pre, w_in, w_in)
    return pl.pallas_call(
        _ffn_down_kernel,
        grid=(n // tm_down,),
        in_specs=[pl.BlockSpec((tm_down, D_MODEL), lambda i: (i, 0)),
                  mod_spec_down,
                  _resident((1, D_MODEL)),
                  pl.BlockSpec((tm_down, D_FF), lambda i: (i, 0)),
                  _layer_resident((D_FF, D_MODEL), layer)],
        out_specs=pl.BlockSpec((tm_down, D_MODEL), lambda i: (i, 0)),
        out_shape=jax.ShapeDtypeStruct((n, D_MODEL), F32),
        compiler_params=_cparams("parallel"),
        name="ffn_down",
    )(x, mod3, g_post, act, w_out)


def _attend(queries, sink_ref):
    rows = queries[0][0].shape[0]
    srow = lax.broadcasted_iota(jnp.int32, (ATT_GROUP * rows, 1), 0)
    outs = [[None] * ATT_HEADS for _ in queries]
    units = [(i, kv) for i in range(len(queries)) for kv in range(ATT_KV)]

    def masked_scores(unit):
        (q, parts), kv = queries[unit[0]], unit[1]
        heads = range(kv * ATT_GROUP, (kv + 1) * ATT_GROUP)
        q3 = jnp.concatenate([q[:, h * HEAD_DIM:(h + 1) * HEAD_DIM] for h in heads], axis=0)
        scores = []
        for kt, _, mask in parts:
            s = _dot(q3, kt[kv * HEAD_DIM:(kv + 1) * HEAD_DIM, :])
            scores.append(s if mask is None else jnp.where(mask, s, NEG))
        return scores

    def softmax_numerator(unit, scores):
        kv = unit[1]
        heads = range(kv * ATT_GROUP, (kv + 1) * ATT_GROUP)
        sink = jnp.full((ATT_GROUP * rows, 1), sink_ref[heads[-1]] * LOG2E, F32)
        for i in range(ATT_GROUP - 2, -1, -1):
            sink = jnp.where(srow < (i + 1) * rows, sink_ref[heads[i]] * LOG2E, sink)
        blocks = [s[:, j:j + LANES] for s in scores for j in range(0, s.shape[1], LANES)]
        m = jnp.maximum(sink, jnp.max(functools.reduce(jnp.maximum, blocks), axis=-1, keepdims=True))
        p = jnp.concatenate([jnp.exp2(s - m).astype(BF16) for s in scores], axis=1)
        return p, jnp.exp2(sink - m)

    def weighted_values(unit, p, sink_term):
        (_, parts), kv = queries[unit[0]], unit[1]
        lo = kv * HEAD_DIM
        vlo = (lo // LANES) * LANES
        v_first = lo == vlo
        v128 = jnp.concatenate([v[:, vlo:vlo + LANES] for _, v, _ in parts], axis=0)
        lane = lax.broadcasted_iota(jnp.int32, v128.shape, 1)
        keep = (lane < HEAD_DIM) if v_first else (lane >= HEAD_DIM)
        acc = _dot(p, jnp.where(keep, v128, jnp.ones_like(v128)))
        ocol, dcol = (0, HEAD_DIM) if v_first else (HEAD_DIM, 0)
        o3 = acc[:, ocol:ocol + HEAD_DIM] / (acc[:, dcol:dcol + 1] + sink_term)
        for i in range(ATT_GROUP):
            outs[unit[0]][kv * ATT_GROUP + i] = o3[i * rows:(i + 1) * rows]

    scores = {0: masked_scores(units[0])}
    probs = {}
    for step in range(len(units) + 2):
        if step + 1 < len(units):
            scores[step + 1] = masked_scores(units[step + 1])
        if step - 1 in probs:
            weighted_values(units[step - 1], *probs.pop(step - 1))
        if step in scores:
            probs[step] = softmax_numerator(units[step], scores.pop(step))
    return [jnp.concatenate(o, axis=-1) for o in outs]


def _ctx_attn_kernel(sink_ref, q_ref, k_ref, v_ref, o_ref):
    blk = ATT_BLOCK
    parts = [(k_ref[...], v_ref[...], None)]
    queries = [(q_ref[r:r + blk, :], parts) for r in range(0, q_ref.shape[0], blk)]
    for r, o in zip(range(0, q_ref.shape[0], blk), _attend(queries, sink_ref)):
        o_ref[r:r + blk, :] = o.astype(o_ref.dtype)


def _ctx_attention(qv, kt, sink, *, n_seq, seq_len):
    vcol = ATT_WIDTH // KV_WIDTH
    blk = ATT_BLOCK
    nb = seq_len // blk
    return pl.pallas_call(
        _ctx_attn_kernel,
        grid=(n_seq, nb),
        in_specs=[pl.BlockSpec(memory_space=pltpu.SMEM),
                  pl.BlockSpec((blk, ATT_WIDTH), lambda b, n: (b * nb + n, 0)),
                  pl.BlockSpec((KV_WIDTH, seq_len), lambda b, n: (0, b)),
                  pl.BlockSpec((seq_len, KV_WIDTH), lambda b, n: (b, vcol))],
        out_specs=pl.BlockSpec((blk, ATT_WIDTH), lambda b, n: (b * nb + n, 0)),
        out_shape=jax.ShapeDtypeStruct((n_seq * seq_len, ATT_WIDTH), BF16),
        compiler_params=_cparams("parallel", "parallel"),
        name="ctx_attention",
    )(sink, qv, kt, qv)


def _lat_attn_kernel(sink_ref, q_ref, kt0_ref, kt1_ref, kt2_ref, kt3_ref, v0_ref, v1_ref, v2_ref, v3_ref,
                     ck_ref, cv_ref, o_ref):
    pair = pl.program_id(1)
    npairs = pl.num_programs(1)
    blk = ATT_BLOCK
    row = lax.broadcasted_iota(jnp.int32, (ATT_GROUP * blk, blk), 0) % blk
    col = lax.broadcasted_iota(jnp.int32, (ATT_GROUP * blk, blk), 1)
    before = col >= row
    after = col <= row
    ctx = (ck_ref[0], cv_ref[0], None)
    blocks = [(kt0_ref[...], v0_ref[...]), (kt1_ref[...], v1_ref[...]), (kt2_ref[...], v2_ref[...]),
              (kt3_ref[...], v3_ref[...])]
    queries = [(q_ref[0:blk, :], [ctx, blocks[0] + (jnp.logical_and(before, pair > 0),),
                                  blocks[1] + (None,), blocks[2] + (after,)]),
               (q_ref[blk:2 * blk, :], [ctx, blocks[1] + (before,), blocks[2] + (None,),
                                        blocks[3] + (jnp.logical_and(after, pair < npairs - 1),)])]
    first, second = _attend(queries, sink_ref)
    o_ref[0:blk, :] = first.astype(o_ref.dtype)
    o_ref[blk:2 * blk, :] = second.astype(o_ref.dtype)


def _lat_attention(qv, kt, sink, ckt, cv, *, n_seq, seq_len):
    blk = ATT_BLOCK
    nb = seq_len // blk
    npairs = nb // 2

    def block(offset):
        return lambda b, n: b * nb + jnp.clip(2 * n + offset, 0, nb - 1)

    vcol = ATT_WIDTH // KV_WIDTH
    kt_specs = [pl.BlockSpec((KV_WIDTH, blk), lambda b, n, f=block(o): (0, f(b, n))) for o in (-1, 0, 1, 2)]
    v_specs = [pl.BlockSpec((blk, KV_WIDTH), lambda b, n, f=block(o): (f(b, n), vcol)) for o in (-1, 0, 1, 2)]
    return pl.pallas_call(
        _lat_attn_kernel,
        grid=(n_seq, npairs),
        in_specs=[pl.BlockSpec(memory_space=pltpu.SMEM),
                  pl.BlockSpec((2 * blk, ATT_WIDTH), lambda b, n: (b * npairs + n, 0))] + kt_specs + v_specs
                 + [pl.BlockSpec((1,) + ckt.shape[1:], lambda b, n: (b, 0, 0)),
                    pl.BlockSpec((1,) + cv.shape[1:], lambda b, n: (b, 0, 0))],
        out_specs=pl.BlockSpec((2 * blk, ATT_WIDTH), lambda b, n: (b * npairs + n, 0)),
        out_shape=jax.ShapeDtypeStruct((n_seq * seq_len, ATT_WIDTH), BF16),
        compiler_params=_cparams("parallel", "parallel"),
        name="lat_attention",
    )(sink, qv, kt, kt, kt, kt, qv, qv, qv, qv, ckt, cv)


def _rope_tables(seq_len):
    rows = seq_len // GRID_W
    row = np.repeat(np.arange(rows, dtype=np.float32), GRID_W)
    col = np.tile(np.arange(GRID_W, dtype=np.float32), rows)
    inv = np.float32(ROPE_BASE) ** (-np.arange(ROPE_PER_AXIS, dtype=np.float32) / np.float32(ROPE_PER_AXIS))
    ang = np.concatenate([row[:, None] * inv, col[:, None] * inv], axis=-1)
    cos, sin = np.cos(ang), np.sin(ang)
    cos128 = np.tile(cos, (1, 4))
    sin128 = np.tile(np.concatenate([-sin, sin], axis=-1), (1, 2))
    return jnp.asarray(cos128, F32), jnp.asarray(sin128, F32)


def _conv_silu(scr, prev_ref, cur_ref, next_ref, w_ref, b_ref, at_start, at_end):
    n = cur_ref.shape[0]
    h = SSD_HALO
    scr[0:h, :] = jnp.where(at_start, 0.0, prev_ref[...])
    scr[h:h + n, :] = cur_ref[...]
    scr[h + n:h + n + h, :] = jnp.where(at_end, 0.0, next_ref[...])
    return _silu(scr[h - 1:h - 1 + n, :] * w_ref[0:1, :] + scr[h:h + n, :] * w_ref[1:2, :]
                 + scr[h + 1:h + 1 + n, :] * w_ref[2:3, :] + b_ref[...])


def _softplus(x):
    return jnp.maximum(x, 0.0) + jnp.log1p(jnp.exp(-jnp.abs(x)))


def _ssd_bwd_kernel(xsp_ref, xsc_ref, xsn_ref, bcp_ref, bcc_ref, bcn_ref, dt_ref, h0_ref,
                    cwx_ref, cbx_ref, cwb_ref, cbb_ref, dtb_ref, alog_ref, tri_ref, exp_ref,
                    hstart_ref, hfin_ref, xbc_ref, xs_scr, bc_scr, h_scr):
    i = pl.program_id(1)
    q = SSD_CHUNK
    half = SSD_WIDTH // SSD_GROUPS
    sel_b = exp_ref[1]
    neg_a = -LOG2E * jnp.exp(alog_ref[...])
    at_start = i == pl.num_programs(1) - 1
    at_end = i == 0
    xbc_ref[:, 0:SSD_WIDTH] = _conv_silu(xs_scr, xsp_ref, xsc_ref, xsn_ref, cwx_ref, cbx_ref, at_start, at_end)
    xbc_ref[:, SSD_WIDTH:] = _conv_silu(bc_scr, bcp_ref, bcc_ref, bcn_ref, cwb_ref, cbb_ref, at_start, at_end)

    @pl.when(i == 0)
    def _():
        h_scr[...] = h0_ref[0]

    hb = h_scr[...]
    for j in reversed(range(xsc_ref.shape[0] // q)):
        rows = slice(j * q, (j + 1) * q)
        hstart_ref[0, j] = hb
        bm = xbc_ref[rows, SSD_WIDTH:SSD_WIDTH + SSD_GROUPS * SSD_STATE].astype(BF16)
        dtv = _softplus(dt_ref[rows, :] + dtb_ref[...])
        rev = _dot_sel_lhs(tri_ref[1], dtv * neg_a, 3)
        dt_w = _dot_sel_rhs(dtv, sel_b, 2)
        to_end = _dot_sel_rhs(jnp.exp2(rev[0:1, :] - rev), sel_b, 2)
        decay = _dot_sel_rhs(jnp.exp2(rev[0:SSD_HALO, :]), sel_b, 2)[0:1, :]
        xw = (xbc_ref[rows, 0:SSD_WIDTH] * dt_w * to_end).astype(BF16)
        upd = jnp.concatenate(
            [_dot_tn(bm[:, g * SSD_STATE:(g + 1) * SSD_STATE], xw[:, g * half:(g + 1) * half])
             for g in range(SSD_GROUPS)], axis=-1)
        hb = hb * decay + upd
    h_scr[...] = hb

    @pl.when(i == pl.num_programs(1) - 1)
    def _():
        hfin_ref[0] = hb.T


def _ssd_intra(xs, bc, dt_raw, dtb_ref, alog_ref, tri_ref, exp_ref, dvec_ref):
    q = SSD_CHUNK
    nh = SSD_HEADS
    gs = SSD_GROUPS * SSD_STATE
    bm = bc[:, 0:gs].astype(BF16)
    cm = bc[:, gs:2 * gs].astype(BF16)
    xs_bf = xs.astype(BF16)

    dtv = _softplus(dt_raw + dtb_ref[...])
    dta = dtv * (-LOG2E * jnp.exp(alog_ref[...]))
    lower, upper = tri_ref[0], tri_ref[1]
    cum = _dot_sel_lhs(lower, dta, 3)
    rev = _dot_sel_lhs(upper, dta, 3)
    dt_t = dtv.T
    dta_t = dta.T
    cum_t = _dot_sel_rhs(dta_t, upper, 3)
    rev_t = _dot_sel_rhs(dta_t, lower, 3)

    sel_f, sel_b = exp_ref[0], exp_ref[1]
    dtf_w = _dot_sel_rhs(dtv, sel_f, 2)
    ecum_w = _dot_sel_rhs(jnp.exp2(cum), sel_f, 2)
    erev_w = _dot_sel_rhs(jnp.exp2(rev), sel_b, 2)
    toend_w = _dot_sel_rhs(jnp.exp2(cum[q - 1:q, :] - cum), sel_f, 2)

    row = lax.broadcasted_iota(jnp.int32, (q, q), 0)
    col = lax.broadcasted_iota(jnp.int32, (q, q), 1)
    below = row > col
    log_dt_t = jnp.log2(dt_t)
    src_f = log_dt_t - cum_t
    src_b = log_dt_t - rev_t
    gmat = [_dot_nt(cm[:, g * SSD_STATE:(g + 1) * SSD_STATE], bm[:, g * SSD_STATE:(g + 1) * SSD_STATE])
            for g in range(SSD_GROUPS)]
    per_group = nh // SSD_GROUPS
    half = SSD_WIDTH // SSD_GROUPS
    ys = []
    for h in range(nh):
        e = jnp.where(below, cum[:, h:h + 1] + src_f[h:h + 1, :],
                      rev[:, nh + h:nh + h + 1] + src_b[nh + h:nh + h + 1, :])
        m = (gmat[h // per_group] * jnp.exp2(e)).astype(BF16)
        ys.append(_dot(m, xs_bf[:, h * SSD_HEAD_DIM:(h + 1) * SSD_HEAD_DIM]))
    cb = cm.astype(F32) * bm.astype(F32)
    g_diag = jnp.concatenate(
        [jnp.broadcast_to(jnp.sum(cb[:, g * SSD_STATE:(g + 1) * SSD_STATE], axis=-1, keepdims=True), (q, half))
         for g in range(SSD_GROUPS)], axis=-1)
    y = jnp.concatenate(ys, axis=-1) + (g_diag * dtf_w + dvec_ref[...]) * xs
    xw = (xs * dtf_w * toend_w).astype(BF16)
    upd = jnp.concatenate(
        [_dot_tn(bm[:, g * SSD_STATE:(g + 1) * SSD_STATE], xw[:, g * half:(g + 1) * half])
         for g in range(SSD_GROUPS)], axis=-1)
    return y, cm, ecum_w, erev_w, upd


def _ssd_carry(hf, hb_start, gz, intra, nw_ref):
    y, cm, ecum_w, erev_w, upd = intra
    half = SSD_WIDTH // SSD_GROUPS
    hf_bf = hf.astype(BF16)
    hb_bf = hb_start.astype(BF16)
    off_f = jnp.concatenate([_dot(cm[:, g * SSD_STATE:(g + 1) * SSD_STATE], hf_bf[:, g * half:(g + 1) * half])
                             for g in range(SSD_GROUPS)], axis=-1)
    off_b = jnp.concatenate([_dot(cm[:, g * SSD_STATE:(g + 1) * SSD_STATE], hb_bf[:, g * half:(g + 1) * half])
                             for g in range(SSD_GROUPS)], axis=-1)
    y = y + off_f * ecum_w + off_b * erev_w
    y = _rms(y * gz, nw_ref[...])
    return y, hf * ecum_w[SSD_CHUNK - 1:SSD_CHUNK, :] + upd


def _ssd_fwd_kernel(gz_ref, xs_ref, bc_ref, dt_ref, hb_ref, h0_ref,
                    dtb_ref, alog_ref, tri_ref, exp_ref, dvec_ref, nw_ref, y_ref, hfin_ref, h_scr):
    c = pl.program_id(1)
    q = SSD_CHUNK
    cps = xs_ref.shape[0] // q
    rows = [slice(j * q, (j + 1) * q) for j in range(cps)]

    def intra(j):
        return _ssd_intra(xs_ref[rows[j], :], bc_ref[rows[j], :], dt_ref[rows[j], :],
                          dtb_ref, alog_ref, tri_ref, exp_ref, dvec_ref)

    @pl.when(c == 0)
    def _():
        h_scr[...] = h0_ref[0]

    hf = h_scr[...]
    ahead = intra(0)
    for j in range(cps):
        cur = ahead
        if j + 1 < cps:
            ahead = intra(j + 1)
        y, hf = _ssd_carry(hf, hb_ref[0, j], gz_ref[rows[j], :], cur, nw_ref)
        y_ref[rows[j], :] = y.astype(y_ref.dtype)
    h_scr[...] = hf

    @pl.when(c == pl.num_programs(1) - 1)
    def _():
        hfin_ref[0] = hf.T


def _ssd_mixer(proj, h0_f, h0_b, consts, *, n_seq, seq_len):
    q = SSD_CHUNK
    nc = seq_len // q
    cps = max(c for c in range(1, SSD_CHUNKS_PER_STEP + 1) if nc % c == 0)
    ns = nc // cps
    rows = cps * q
    (cwx, cbx, cwb, cbb, dt_bias, a_log, tri, expand, dvec, nw) = consts
    per8 = rows // SSD_HALO
    last8 = proj.shape[0] // SSD_HALO - 1

    def raw_specs(width, col):
        step = lambda b, i: b * ns + ns - 1 - i
        return [pl.BlockSpec((SSD_HALO, width), lambda b, i: (jnp.maximum(step(b, i) * per8 - 1, 0), col)),
                pl.BlockSpec((rows, width), lambda b, i: (step(b, i), col)),
                pl.BlockSpec((SSD_HALO, width), lambda b, i: (jnp.minimum((step(b, i) + 1) * per8, last8), col))]

    def specs(chunk_of):
        def cur(b, i):
            return b * ns + chunk_of(i)

        return [pl.BlockSpec((rows, SSD_WIDTH), lambda b, i: (cur(b, i), 0)),
                pl.BlockSpec((rows, SSD_BC), lambda b, i: (cur(b, i), SSD_WIDTH // SSD_BC)),
                pl.BlockSpec((rows, DT_PAD), lambda b, i: (cur(b, i), C_DT // DT_PAD))], cur

    state_spec = pl.BlockSpec((1, SSD_STATE, SSD_WIDTH), lambda b, i: (b, 0, 0))
    final_spec = pl.BlockSpec((1, SSD_WIDTH, SSD_STATE), lambda b, i: (b, 0, 0))
    const_specs = [_resident(dt_bias.shape), _resident(a_log.shape), _resident(tri.shape),
                   _resident(expand.shape)]
    scratch = [pltpu.VMEM((SSD_STATE, SSD_WIDTH), F32)]

    data_specs, rcur = specs(lambda i: ns - 1 - i)
    hb_start, hb_fin, xbc = pl.pallas_call(
        _ssd_bwd_kernel,
        grid=(n_seq, ns),
        in_specs=(raw_specs(SSD_WIDTH, C_XS // SSD_WIDTH) + raw_specs(SSD_BC, C_BC // SSD_BC)
                  + [data_specs[2], state_spec]
                  + [_resident(cwx.shape), _resident(cbx.shape), _resident(cwb.shape), _resident(cbb.shape)]
                  + const_specs),
        out_specs=[pl.BlockSpec((1, cps, SSD_STATE, SSD_WIDTH), lambda b, i: (b, ns - 1 - i, 0, 0)), final_spec,
                   pl.BlockSpec((rows, SSD_WIDTH + SSD_BC), lambda b, i: (rcur(b, i), 0))],
        out_shape=[jax.ShapeDtypeStruct((n_seq, nc, SSD_STATE, SSD_WIDTH), F32),
                   jax.ShapeDtypeStruct((n_seq, SSD_WIDTH, SSD_STATE), F32),
                   jax.ShapeDtypeStruct((n_seq * seq_len, SSD_WIDTH + SSD_BC), F32)],
        scratch_shapes=[pltpu.VMEM((rows + 2 * SSD_HALO, SSD_WIDTH), F32),
                        pltpu.VMEM((rows + 2 * SSD_HALO, SSD_BC), F32)] + scratch,
        compiler_params=_cparams("parallel", "arbitrary"),
        name="ssd_backward_states",
    )(proj, proj, proj, proj, proj, proj, proj, h0_b, cwx, cbx, cwb, cbb, dt_bias, a_log, tri, expand)

    data_specs, cur = specs(lambda i: i)
    y, hf_fin = pl.pallas_call(
        _ssd_fwd_kernel,
        grid=(n_seq, ns),
        in_specs=([pl.BlockSpec((rows, SSD_WIDTH), lambda b, i: (cur(b, i), C_Z // SSD_WIDTH))] + data_specs
                  + [pl.BlockSpec((1, cps, SSD_STATE, SSD_WIDTH), lambda b, i: (b, i, 0, 0)), state_spec]
                  + const_specs + [_resident(dvec.shape), _resident(nw.shape)]),
        out_specs=[pl.BlockSpec((rows, SSD_WIDTH), lambda b, i: (b * ns + i, 0)), final_spec],
        out_shape=[jax.ShapeDtypeStruct((n_seq * seq_len, SSD_WIDTH), BF16),
                   jax.ShapeDtypeStruct((n_seq, SSD_WIDTH, SSD_STATE), F32)],
        scratch_shapes=scratch,
        compiler_params=_cparams("parallel", "arbitrary"),
        name="ssd_forward",
    )(proj, xbc, xbc, proj, hb_start, h0_f, dt_bias, a_log, tri, expand, dvec, nw)
    return y, hf_fin, hb_fin


def _ssd_consts(l, ssd_conv_w, ssd_conv_b, ssd_dt_bias, ssd_A_log, ssd_D, ssd_norm):
    q = SSD_CHUNK
    cw, cb = ssd_conv_w[l], ssd_conv_b[l]
    pad = DT_PAD - 2 * SSD_HEADS
    dt_bias = jnp.pad(ssd_dt_bias[l].reshape(1, -1), ((0, 0), (0, pad)))
    a_log = jnp.pad(ssd_A_log[l].reshape(1, -1), ((0, 0), (0, pad)))
    r = np.arange(q)
    tri = jnp.asarray(np.stack([r[None, :] <= r[:, None], r[None, :] >= r[:, None]]), BF16)
    head_of_col = np.arange(SSD_WIDTH) // SSD_HEAD_DIM
    j = np.arange(DT_PAD)
    expand = jnp.asarray(np.stack([j[:, None] == head_of_col[None, :],
                                   j[:, None] == head_of_col[None, :] + SSD_HEADS]), BF16)
    dvec = jnp.repeat(ssd_D[l], SSD_HEAD_DIM).reshape(1, SSD_WIDTH)
    return (cw[:, :SSD_WIDTH], cb[:SSD_WIDTH].reshape(1, -1), cw[:, SSD_WIDTH:], cb[SSD_WIDTH:].reshape(1, -1),
            dt_bias, a_log, tri, expand, dvec, ssd_norm[l].reshape(1, -1))


def _ssd_state_in(s):
    b = s.shape[0]
    return jnp.transpose(s, (0, 3, 1, 2)).reshape(b, SSD_STATE, SSD_WIDTH)


def _s5_table_kernel(pw_ref, ct_ref, bb_ref, bbt_ref, sel_ref, tile_ref, rhs_ref, woff_ref):
    t = S5_T
    c = S5_CH
    tile = tile_ref[...]
    spread = lambda x, sel: _dot_sel_rhs(x, sel, 2)
    krows, offs, cols = [], [], []
    for d in range(2):
        pw_re, pw_im = pw_ref[0, 2 * d], pw_ref[0, 2 * d + 1]
        c_re = spread(ct_ref[0, :, (2 * d) * c:(2 * d + 1) * c], tile)
        c_im = spread(ct_ref[0, :, (2 * d + 1) * c:(2 * d + 2) * c], tile)
        b_re = spread(bb_ref[0, :, (2 * d) * c:(2 * d + 1) * c], tile)
        b_im = spread(bb_ref[0, :, (2 * d + 1) * c:(2 * d + 2) * c], tile)
        bt_re = bbt_ref[0, (2 * d) * c:(2 * d + 1) * c, :]
        bt_im = bbt_ref[0, (2 * d + 1) * c:(2 * d + 2) * c, :]
        up = (spread(pw_re, sel_ref[0]), spread(pw_im, sel_ref[0]))
        down = (spread(pw_re, sel_ref[1]), spread(pw_im, sel_ref[1]))
        (q_re, q_im), state = ((up, down), (down, up))[d]
        g_re = q_re * c_re - q_im * c_im
        g_im = q_re * c_im + q_im * c_re
        krows.append(_dot_wide(bt_re, g_re) - _dot_wide(bt_im, g_im))
        lb_re, lb_im = pw_re[:, 1:2], pw_im[:, 1:2]
        r_re = q_re * lb_re - q_im * lb_im
        r_im = q_re * lb_im + q_im * lb_re
        offs.append((r_re * c_re - r_im * c_im, -(r_re * c_im + r_im * c_re)))
        s_re, s_im = state
        cols.append(((s_re * b_re - s_im * b_im).T, (s_re * b_im + s_im * b_re).T))
    k_f, k_b = krows
    for s in range(t):
        r = t - 1 - s
        f = k_f if s == 0 else jnp.concatenate([jnp.zeros((c, c * s), F32), k_f[:, :S5_TW - c * s]], axis=1)
        b = k_b if r == 0 else jnp.concatenate([k_b[:, c * r:], jnp.zeros((c, c * r), F32)], axis=1)
        rhs_ref[0, s * c:(s + 1) * c, 0:S5_TW] = (f + b).astype(BF16)
    rhs_ref[0, :, S5_TW:] = jnp.concatenate([cols[0][0], cols[1][0], cols[0][1], cols[1][1]], axis=1).astype(BF16)
    woff_ref[0] = jnp.concatenate([offs[0][0], offs[1][0], offs[0][1], offs[1][1]], axis=0).astype(BF16)


def _s5_tables(s5_A_re, s5_A_im, s5_log_dt, s5_B_re, s5_B_im, s5_C_re, s5_C_im):
    t = S5_T
    n = N_LAYERS * S5_GROUPS
    step = jnp.exp(s5_log_dt)[..., None]
    k = jnp.arange(LANES, dtype=F32)
    keep = k <= t
    kk = jnp.where(keep, k, 0.0)
    mag = jnp.exp(kk * (s5_A_re * step)[..., None])
    ang = kk * (s5_A_im * step)[..., None]
    pw_re = jnp.where(keep, mag * jnp.cos(ang), 0.0)
    pw_im = jnp.where(keep, mag * jnp.sin(ang), 0.0)
    lb_re, lb_im = pw_re[..., 1], pw_im[..., 1]
    den = s5_A_re * s5_A_re + s5_A_im * s5_A_im
    r_re = ((lb_re - 1.0) * s5_A_re + lb_im * s5_A_im) / den
    r_im = (lb_im * s5_A_re - (lb_re - 1.0) * s5_A_im) / den
    b_re, b_im = s5_B_re[:, None], s5_B_im[:, None]
    bb_re = r_re[..., None] * b_re - r_im[..., None] * b_im
    bb_im = r_re[..., None] * b_im + r_im[..., None] * b_re
    c_re = jnp.swapaxes(s5_C_re, -1, -2)
    c_im = jnp.swapaxes(s5_C_im, -1, -2)

    def pack(re, im, axis):
        return jnp.concatenate([re[:, 0], im[:, 0], re[:, 1], im[:, 1]], axis=axis)

    pw = jnp.stack([pw_re[:, 0], pw_im[:, 0], pw_re[:, 1], pw_im[:, 1]], axis=2).reshape(n, 4, S5_P, LANES)
    ct = pack(c_re, c_im, -1).reshape(n, S5_P, 4 * S5_CH)
    bb = pack(bb_re, bb_im, -1)
    bbt = jnp.swapaxes(bb, -1, -2).reshape(n, 4 * S5_CH, S5_P)
    bb = bb.reshape(n, S5_P, 4 * S5_CH)
    lam_t = jnp.concatenate([pw_re[:, 0, :, :, t], pw_re[:, 1, :, :, t], pw_im[:, 0, :, :, t],
                             pw_im[:, 1, :, :, t]], axis=-1).reshape(n, 1, S5_ST)
    jt = np.arange(S5_TW) // S5_CH
    lane_k = np.arange(LANES)[:, None]
    sel = jnp.asarray(np.stack([lane_k == f[None, :] for f in (jt, t - 1 - jt)]), BF16)
    tile = jnp.asarray(np.arange(S5_CH)[:, None] == (np.arange(S5_TW) % S5_CH)[None, :], BF16)
    grp = lambda shape: pl.BlockSpec((1,) + shape, lambda i: (i,) + (0,) * len(shape))
    rhs, woff = pl.pallas_call(
        _s5_table_kernel,
        grid=(n,),
        in_specs=[grp((4, S5_P, LANES)), grp((S5_P, 4 * S5_CH)), grp((S5_P, 4 * S5_CH)), grp((4 * S5_CH, S5_P)),
                  _resident(sel.shape), _resident(tile.shape)],
        out_specs=[grp((S5_TW, S5_TW + S5_ST)), grp((S5_ST, S5_TW))],
        out_shape=[jax.ShapeDtypeStruct((n, S5_TW, S5_TW + S5_ST), BF16),
                   jax.ShapeDtypeStruct((n, S5_ST, S5_TW), BF16)],
        compiler_params=_cparams("parallel"),
        name="s5_tables",
    )(pw, ct, bb, bbt, sel, tile)
    return rhs, woff, lam_t


def _unit_transpose(vs, masks, axis, unit):
    size = vs[0].shape[axis]
    cur = list(vs)
    for d in (1, 2, 4):
        nxt = list(cur)
        for a in range(S5_OCT):
            if a & d:
                continue
            b = a | d
            nxt[a] = jnp.where(masks[d], cur[a], pltpu.roll(cur[b], unit * d, axis=axis))
            nxt[b] = jnp.where(masks[d], pltpu.roll(cur[a], size - unit * d, axis=axis), cur[b])
        cur = nxt
    return cur


def _s5_kernel(u_ref, rhs_ref, woff_ref, lam_ref, h0_ref, y_ref, hfin_ref,
               dst_scr, y_scr, sre_scr, sim_scr, hfre_scr, hfim_scr, hbre_scr, hbim_scr, *, nc, rb):
    t = S5_T
    rows = dst_scr.shape[1]
    sb = rows // nc
    tiles = t // S5_OCT
    nr = rb // S5_OCT
    tile_shape = (nr, S5_OCT, LANES)
    seg = lax.broadcasted_iota(jnp.int32, tile_shape, 2) // S5_CH
    sub = lax.broadcasted_iota(jnp.int32, tile_shape, 1)
    lane_masks = {d: (seg & d) == 0 for d in (1, 2, 4)}
    sub_masks = {d: (sub & d) == 0 for d in (1, 2, 4)}

    def tile_rows(i, r, tq):
        return pl.ds((i * rb + r) * tiles + tq, nr, stride=S5_OCT * tiles)

    def gather(i, carry):
        r0 = pl.multiple_of(i * rb, rb)
        for tq in range(tiles):
            by_chunk = [u_ref[tile_rows(i, r, tq)] for r in range(S5_OCT)]
            by_time = _unit_transpose(by_chunk, sub_masks, 1, 1)
            for gl, v in enumerate(_unit_transpose(by_time, lane_masks, 2, S5_CH)):
                dst_scr[gl, pl.ds(r0, rb), tq * LANES:(tq + 1) * LANES] = v.reshape(rb, LANES).astype(BF16)
        return carry

    lax.fori_loop(0, rows // rb, gather, 0)

    for gl in range(S5_OCT):
        z = _dot(dst_scr[gl], rhs_ref[gl])
        y_scr[gl] = z[:, 0:S5_TW]
        sre_scr[gl] = z[:, S5_TW:S5_TW + 2 * S5_P]
        sim_scr[gl] = z[:, S5_TW + 2 * S5_P:]

    fwd_lane = lax.broadcasted_iota(jnp.int32, (sb, 2 * S5_P), 1) < S5_P
    a_re = [lam_ref[gl][:, 0:2 * S5_P] for gl in range(S5_OCT)]
    a_im = [lam_ref[gl][:, 2 * S5_P:] for gl in range(S5_OCT)]

    def step(i, carry):
        fi = pl.ds(i, sb, stride=nc)
        bj = pl.ds(nc - 1 - i, sb, stride=nc)
        out = []
        for gl in range(S5_OCT):
            h_re, h_im = carry[2 * gl], carry[2 * gl + 1]
            hfre_scr[gl, fi, :] = h_re
            hfim_scr[gl, fi, :] = h_im
            hbre_scr[gl, bj, :] = h_re
            hbim_scr[gl, bj, :] = h_im
            s_re = jnp.where(fwd_lane, sre_scr[gl, fi, :], sre_scr[gl, bj, :])
            s_im = jnp.where(fwd_lane, sim_scr[gl, fi, :], sim_scr[gl, bj, :])
            out.append(a_re[gl] * h_re - a_im[gl] * h_im + s_re)
            out.append(a_re[gl] * h_im + a_im[gl] * h_re + s_im)
        return tuple(out)

    init = []
    for gl in range(S5_OCT):
        h0 = h0_ref[gl, 0]
        init += [h0[:, 0:2 * S5_P], h0[:, 2 * S5_P:]]
    fin = lax.fori_loop(0, nc, step, tuple(init), unroll=2 if nc % 2 == 0 else 1)

    all_fwd = lax.broadcasted_iota(jnp.int32, (rows, 2 * S5_P), 1) < S5_P
    for gl in range(S5_OCT):
        hfin_ref[gl, 0] = jnp.concatenate([fin[2 * gl], fin[2 * gl + 1]], axis=-1)
        hin = jnp.concatenate([jnp.where(all_fwd, hfre_scr[gl], hbre_scr[gl]),
                               jnp.where(all_fwd, hfim_scr[gl], hbim_scr[gl])], axis=-1).astype(BF16)
        y_scr[gl] += _dot(hin, woff_ref[gl])

    def scatter(i, carry):
        r0 = pl.multiple_of(i * rb, rb)
        for tq in range(tiles):
            by_group = [y_scr[gl, pl.ds(r0, rb), tq * LANES:(tq + 1) * LANES].reshape(tile_shape)
                        for gl in range(S5_OCT)]
            by_time = _unit_transpose(by_group, lane_masks, 2, S5_CH)
            for r, v in enumerate(_unit_transpose(by_time, sub_masks, 1, 1)):
                y_ref[tile_rows(i, r, tq)] = v
        return carry

    lax.fori_loop(0, rows // rb, scatter, 0)


def _s5_mixer(proj, tables, layer, h0, *, n_seq, seq_len):
    rhs, woff, lam_t = tables
    t = S5_T
    nc = seq_len // t
    sb = max(1, min(n_seq, S5_MAX_ROWS // nc))
    assert n_seq % sb == 0
    nbs = n_seq // sb
    rows = sb * nc
    rb = min(rows, 32)
    noct = S5_GROUPS // S5_OCT
    ucol = C_U // LANES
    tab = lambda shape: pl.BlockSpec((S5_OCT,) + shape, lambda j, s: (layer * noct + j, 0, 0))
    state = pl.BlockSpec((S5_OCT, 1, sb, S5_ST), lambda j, s: (j, s, 0, 0))
    rows_scr = lambda dt: pltpu.VMEM((S5_OCT, rows, 2 * S5_P), dt)
    n_tok = n_seq * seq_len
    tok_block = (sb * seq_len // S5_OCT, S5_OCT, LANES)
    y, hfin = pl.pallas_call(
        functools.partial(_s5_kernel, nc=nc, rb=rb),
        grid=(noct, nbs),
        in_specs=[pl.BlockSpec(tok_block, lambda j, s: (s, 0, ucol + j)),
                  tab(rhs.shape[1:]), tab(woff.shape[1:]), tab(lam_t.shape[1:]), state],
        out_specs=[pl.BlockSpec(tok_block, lambda j, s: (s, 0, j)), state],
        out_shape=[jax.ShapeDtypeStruct((n_tok // S5_OCT, S5_OCT, S5_WIDTH), F32),
                   jax.ShapeDtypeStruct((S5_GROUPS, nbs, sb, S5_ST), F32)],
        scratch_shapes=[pltpu.VMEM((S5_OCT, rows, S5_TW), BF16), pltpu.VMEM((S5_OCT, rows, S5_TW), F32),
                        rows_scr(F32), rows_scr(F32), rows_scr(F32), rows_scr(F32), rows_scr(F32), rows_scr(F32)],
        compiler_params=_cparams("parallel", "parallel"),
        name="s5_mixer",
    )(proj.reshape(n_tok // S5_OCT, S5_OCT, D_INP), rhs, woff, lam_t, h0.reshape(S5_GROUPS, nbs, sb, S5_ST))
    return y.reshape(n_tok, S5_WIDTH), hfin.reshape(S5_GROUPS, n_seq, S5_ST)


def _s5_state_in(re, im):
    b = re.shape[0]
    x = jnp.stack([re, im], axis=1)
    return jnp.transpose(x, (3, 0, 1, 2, 4)).reshape(S5_GROUPS, b, S5_ST)


def _s5_state_out(h):
    n_seq = h.shape[1]
    x = h.reshape(S5_GROUPS, n_seq, 2, 2, S5_P)
    x = jnp.transpose(x, (2, 1, 3, 0, 4))
    return x[0], x[1]


def kernel(x_prompt, x_sample, c, cache_k, cache_v, state_ssd, state_s5_re, state_s5_im, c_ctx, w_ada, b_ada, norm_mix_pre, norm_mix_post, norm_ffn_pre, norm_ffn_post, w_in, w_out, ssd_conv_w, ssd_conv_b, ssd_dt_bias, ssd_A_log, ssd_D, ssd_norm, attn_sink, s5_A_re, s5_A_im, s5_log_dt, s5_B_re, s5_B_im, s5_C_re, s5_C_im, s5_D, s5_w_glu, s5_b_glu, w_ffn_in, w_ffn_out):
    nb_ctx, len_ctx, _ = x_prompt.shape
    nb_lat, len_lat, _ = x_sample.shape
    n_ctx = nb_ctx * len_ctx
    n_lat = nb_lat * len_lat
    tm = 512
    mod_rows = -(-(1 + nb_lat) // 8) * 8
    cvecs = jnp.concatenate([c_ctx[None, :], c, jnp.zeros((mod_rows - 1 - nb_lat, D_MODEL), F32)], axis=0)
    mods = _ada(cvecs, w_ada, b_ada)
    mod_geom = ((n_ctx, 0), (len_lat, 1))
    token_sets = tuple(dict(mod_spec=_mod_spec(tm, *g)) for g in mod_geom)
    tm_up = min(FFN_UP_ROWS, n_ctx, len_lat)
    rope_args = (dict(), dict(rope_tables=_rope_tables(len_lat), seq_len=len_lat))

    s5_tab = _s5_tables(s5_A_re, s5_A_im, s5_log_dt, s5_B_re, s5_B_im, s5_C_re, s5_C_im)
    w_in_b = w_in.astype(BF16)
    w_in_p = jnp.concatenate(
        [w_in_b[:, :, O_Q:O_K], w_in_b[:, :, O_Z:O_DT], w_in_b[:, :, O_K:O_END], w_in_b[:, :, O_DT:O_Q],
         jnp.zeros((N_LAYERS, D_MODEL, DT_PAD - 2 * SSD_HEADS), BF16)], axis=2)
    w_out_b = w_out.astype(BF16)
    w_glu_b = s5_w_glu.astype(BF16)
    w_ffn_in_b = w_ffn_in.astype(BF16)
    w_ffn_out_b = w_ffn_out.astype(BF16)
    zeros_ssd = jnp.zeros((nb_ctx, SSD_STATE, SSD_WIDTH), F32)
    zeros_s5 = jnp.zeros((S5_GROUPS, nb_ctx, S5_ST), F32)

    xs = [x_prompt.reshape(n_ctx, D_MODEL), x_sample.reshape(n_lat, D_MODEL)]
    new_k, new_v, new_ssd, new_s5_re, new_s5_im = [], [], [], [], []
    for l in range(N_LAYERS):
        mod3 = mods[l].reshape(mod_rows, 1, 6 * D_MODEL)
        proj, qv, kt = zip(*[_in_proj(x, mod3, norm_mix_pre[l].reshape(1, -1), w_in_p, layer=l, tm=tm, **ts, **ra)
                             for x, ts, ra in zip(xs, token_sets, rope_args)])

        sink = attn_sink[l]
        ckt = jnp.swapaxes(cache_k[:, l].reshape(nb_lat, -1, KV_WIDTH), 1, 2).astype(BF16)
        cv = cache_v[:, l].reshape(nb_lat, -1, KV_WIDTH).astype(BF16)
        o_att = [_ctx_attention(qv[0], kt[0], sink, n_seq=nb_ctx, seq_len=len_ctx),
                 _lat_attention(qv[1], kt[1], sink, ckt, cv, n_seq=nb_lat, seq_len=len_lat)]

        consts = _ssd_consts(l, ssd_conv_w, ssd_conv_b, ssd_dt_bias, ssd_A_log, ssd_D, ssd_norm)
        y_ctx, hf_ctx, hb_ctx = _ssd_mixer(proj[0], zeros_ssd, zeros_ssd, consts, n_seq=nb_ctx, seq_len=len_ctx)
        y_lat, _, _ = _ssd_mixer(proj[1], _ssd_state_in(state_ssd[:, l, 0]), _ssd_state_in(state_ssd[:, l, 1]),
                                 consts, n_seq=nb_lat, seq_len=len_lat)
        y_ssd = [y_ctx, y_lat]

        s5_ctx, s5_fin = _s5_mixer(proj[0], s5_tab, l, zeros_s5, n_seq=nb_ctx, seq_len=len_ctx)
        s5_lat, _ = _s5_mixer(proj[1], s5_tab, l, _s5_state_in(state_s5_re[:, l], state_s5_im[:, l]),
                              n_seq=nb_lat, seq_len=len_lat)
        y_s5 = [s5_ctx, s5_lat]

        xs = [_out_proj(x, mod3, norm_mix_post[l].reshape(1, -1), y_ssd[i], o_att[i], y_s5[i], proj[i],
                        s5_D[l].reshape(1, -1), w_glu_b, s5_b_glu[l].reshape(1, -1), w_out_b,
                        layer=l, tm=tm, **ts)
              for i, (x, ts) in enumerate(zip(xs, token_sets))]
        xs = [_ffn(x, mod3, norm_ffn_pre[l].reshape(1, -1), norm_ffn_post[l].reshape(1, -1),
                   w_ffn_in_b, w_ffn_out_b, layer=l, tm_up=tm_up, tf=FFN_UP_COLS, tm_down=tm,
                   mod_spec_up=_mod_spec(tm_up, *g), mod_spec_down=_mod_spec(tm, *g))
              for x, g in zip(xs, mod_geom)]

        new_k.append(proj[0][:, C_K:C_K + KV_WIDTH].reshape(nb_ctx, len_ctx, ATT_KV, HEAD_DIM))
        new_v.append(proj[0][:, C_V:C_V + KV_WIDTH].reshape(nb_ctx, len_ctx, ATT_KV, HEAD_DIM))
        new_ssd += [hf_ctx, hb_ctx]
        re, im = _s5_state_out(s5_fin)
        new_s5_re.append(re)
        new_s5_im.append(im)

    y_prompt = xs[0].reshape(nb_ctx, len_ctx, D_MODEL)
    y_sample = xs[1].reshape(nb_lat, len_lat, D_MODEL)
    new_state_ssd = jnp.stack(new_ssd, axis=1).reshape(nb_ctx, N_LAYERS, 2, SSD_HEADS, SSD_HEAD_DIM, SSD_STATE)
    return (y_prompt, y_sample, jnp.stack(new_k, axis=1), jnp.stack(new_v, axis=1), new_state_ssd,
            jnp.stack(new_s5_re, axis=1), jnp.stack(new_s5_im, axis=1))
```

```python
import functools
import math

import jax
import jax.numpy as jnp
import numpy as np
from jax import lax
from jax.experimental import pallas as pl
from jax.experimental.pallas import tpu as pltpu

F32 = jnp.float32
BF16 = jnp.bfloat16

D_MODEL = 2048
N_LAYERS = 2
D_FF = 5632
GRID_W = 64
EPS = 1e-6
LANES = 128

SSD_HEADS = 12
SSD_HEAD_DIM = 64
SSD_WIDTH = SSD_HEADS * SSD_HEAD_DIM
SSD_GROUPS = 2
SSD_STATE = 64
SSD_BC = 2 * SSD_GROUPS * SSD_STATE
SSD_CHUNK = 128
SSD_CHUNKS_PER_STEP = 8
SSD_HALO = 8
ATT_HEADS = 12
ATT_KV = 4
ATT_GROUP = ATT_HEADS // ATT_KV
HEAD_DIM = 64
ATT_WIDTH = ATT_HEADS * HEAD_DIM
KV_WIDTH = ATT_KV * HEAD_DIM
ATT_BLOCK = 128
LAT_Q_BLOCKS = 4
QV_WIDTH = ATT_WIDTH + KV_WIDTH
LOG2E = math.log2(math.e)
Q_SCALE = HEAD_DIM ** -0.5 * LOG2E
ROPE_PER_AXIS = HEAD_DIM // 4
ROPE_BASE = 10000.0
S5_GROUPS = 32
S5_CH = 16
S5_WIDTH = S5_GROUPS * S5_CH
S5_P = 64
S5_T = 32
S5_TW = S5_T * S5_CH
S5_ST = 4 * S5_P
S5_OCT = LANES // S5_CH
S5_MAX_ROWS = 256
MIX_WIDTH = SSD_WIDTH + ATT_WIDTH + S5_WIDTH

O_Z = 0
O_XS = SSD_WIDTH
O_BC = O_XS + SSD_WIDTH
O_DT = O_BC + SSD_BC
O_Q = O_DT + 2 * SSD_HEADS
O_K = O_Q + ATT_WIDTH
O_V = O_K + KV_WIDTH
O_U = O_V + KV_WIDTH
O_END = O_U + S5_WIDTH
C_Q = 0
C_Z = 768
C_XS = 1536
C_BC = 2304
C_K = 2560
C_V = 2816
C_U = 3072
C_DT = 3584
DT_PAD = 128
D_INP = C_DT + DT_PAD

FFN_UP_ROWS = 1024
FFN_UP_COLS = 512

NEG = -1e30
VMEM_LIMIT = 56 * 1024 * 1024


def _cparams(*sem):
    return pltpu.CompilerParams(dimension_semantics=sem, vmem_limit_bytes=VMEM_LIMIT)


def _resident(shape):
    nd = len(shape)
    return pl.BlockSpec(shape, lambda *_: (0,) * nd, pipeline_mode=pl.Buffered(1))


def _layer_resident(shape, layer, block=0):
    rest = (0,) * (len(shape) - 1)
    return pl.BlockSpec((None,) + shape, lambda *_: (layer, block) + rest, pipeline_mode=pl.Buffered(1))


def _rms(x, g):
    return x * lax.rsqrt(jnp.mean(x * x, axis=-1, keepdims=True) + EPS) * g


def _silu(x):
    return x * jax.nn.sigmoid(x)


def _dot(a, b):
    return jnp.dot(a, b, preferred_element_type=F32)


def _dot_wide(a, b):
    a_hi, a_lo = _split_bf16(a, 2)
    b_hi, b_lo = _split_bf16(b, 2)
    return _dot(a_hi, b_hi) + (_dot(a_hi, b_lo) + _dot(a_lo, b_hi))


def _dot_nt(a, b):
    return lax.dot_general(a, b, (((1,), (1,)), ((), ())), preferred_element_type=F32)


def _dot_tn(a, b):
    return lax.dot_general(a, b, (((0,), (0,)), ((), ())), preferred_element_type=F32)


def _split_bf16(x, parts):
    out = []
    for _ in range(parts):
        p = x.astype(BF16)
        out.append(p)
        x = x - p.astype(F32)
    return out


def _dot_sel_rhs(x, sel, parts):
    acc = None
    for p in _split_bf16(x, parts):
        t = _dot(p, sel)
        acc = t if acc is None else acc + t
    return acc


def _dot_sel_lhs(sel, x, parts):
    acc = None
    for p in _split_bf16(x, parts):
        t = _dot(sel, p)
        acc = t if acc is None else acc + t
    return acc


def _mod_spec(tm, rows_per_mod, first_row):
    return pl.BlockSpec((1, 1, 6 * D_MODEL), lambda i, *_: (first_row + (i * tm) // rows_per_mod, 0, 0))


def _ada_kernel(c_ref, w_ref, b_ref, o_ref):
    c = c_ref[...]
    o_ref[0] = _dot(_silu(c).astype(BF16), w_ref[0].astype(BF16)) + b_ref[0]


def _ada(cvecs, w_ada, b_ada):
    tn = 1024
    rows = cvecs.shape[0]
    return pl.pallas_call(
        _ada_kernel,
        grid=(N_LAYERS, 6 * D_MODEL // tn),
        in_specs=[pl.BlockSpec((rows, D_MODEL), lambda l, j: (0, 0)),
                  pl.BlockSpec((1, D_MODEL, tn), lambda l, j: (l, 0, j)),
                  pl.BlockSpec((1, 1, tn), lambda l, j: (l, 0, j))],
        out_specs=pl.BlockSpec((1, rows, tn), lambda l, j: (l, 0, j)),
        out_shape=jax.ShapeDtypeStruct((N_LAYERS, rows, 6 * D_MODEL), F32),
        compiler_params=_cparams("parallel", "parallel"),
        name="ada_mod",
    )(cvecs, w_ada, b_ada.reshape(N_LAYERS, 1, 6 * D_MODEL))


def _rope(x, cos, sin_signed, first_half):
    outs = []
    for j in range(x.shape[1] // LANES):
        xj = x[:, j * LANES:(j + 1) * LANES]
        partner = jnp.where(first_half, pltpu.roll(xj, LANES - HEAD_DIM // 2, axis=1),
                            pltpu.roll(xj, HEAD_DIM // 2, axis=1))
        outs.append(xj * cos + partner * sin_signed)
    return jnp.concatenate(outs, axis=-1)


def _in_proj_kernel(*refs, rope):
    if rope:
        x_ref, mod_ref, g_ref, w_ref, cos_ref, sin_ref, o_ref, qv_ref, kt_ref = refs
    else:
        x_ref, mod_ref, g_ref, w_ref, o_ref, qv_ref, kt_ref = refs
    mod = mod_ref[0]
    sh = mod[:, 0:D_MODEL]
    sc = mod[:, D_MODEL:2 * D_MODEL]
    h = _rms(x_ref[...], g_ref[...]) * (1.0 + sc) + sh
    o_ref[...] = _dot(h.astype(BF16), w_ref[...])
    o_ref[:, C_Z:C_Z + SSD_WIDTH] = _silu(o_ref[:, C_Z:C_Z + SSD_WIDTH])
    q = o_ref[:, C_Q:C_Q + ATT_WIDTH]
    k = o_ref[:, C_K:C_K + KV_WIDTH]
    if rope:
        lane = lax.broadcasted_iota(jnp.int32, (x_ref.shape[0], LANES), 1)
        first_half = (lane % HEAD_DIM) < (HEAD_DIM // 2)
        q = _rope(q, cos_ref[...], sin_ref[...], first_half)
        k = _rope(k, cos_ref[...], sin_ref[...], first_half)
    qv_ref[:, 0:ATT_WIDTH] = (q * Q_SCALE).astype(BF16)
    qv_ref[:, ATT_WIDTH:] = o_ref[:, C_V:C_V + KV_WIDTH].astype(BF16)
    kt_ref[...] = k.T.astype(BF16)


def _in_proj(x, mod3, gamma, w, *, layer, tm, mod_spec, rope_tables=None, seq_len=None):
    n = x.shape[0]
    rope = rope_tables is not None
    tables, table_specs = (), []
    if rope:
        per_seq = seq_len // tm
        tables = tuple(rope_tables)
        table_specs = [pl.BlockSpec((tm, LANES), lambda i: (i % per_seq, 0))] * 2
    return pl.pallas_call(
        functools.partial(_in_proj_kernel, rope=rope),
        grid=(n // tm,),
        in_specs=[pl.BlockSpec((tm, D_MODEL), lambda i: (i, 0)),
                  mod_spec,
                  _resident((1, D_MODEL)),
                  _layer_resident((D_MODEL, D_INP), layer)] + table_specs,
        out_specs=[pl.BlockSpec((tm, D_INP), lambda i: (i, 0)), pl.BlockSpec((tm, QV_WIDTH), lambda i: (i, 0)),
                   pl.BlockSpec((KV_WIDTH, tm), lambda i: (0, i))],
        out_shape=[jax.ShapeDtypeStruct((n, D_INP), F32), jax.ShapeDtypeStruct((n, QV_WIDTH), BF16),
                   jax.ShapeDtypeStruct((KV_WIDTH, n), BF16)],
        compiler_params=_cparams("parallel"),
        name="in_proj",
    )(x, mod3, gamma, w, *tables)


def _gelu_tanh(x):
    return 0.5 * x * (1.0 + jnp.tanh(math.sqrt(2.0 / math.pi) * (x + 0.044715 * (x * x * x))))


def _out_proj_kernel(x_ref, mod_ref, g_ref, yssd_ref, oatt_ref, ys5_ref, u_ref, d_ref, wglu_ref, bglu_ref,
                     w12_ref, w3_ref, o_ref, mix_scr):
    mix_scr[:, 0:SSD_WIDTH] = yssd_ref[...]
    mix_scr[:, SSD_WIDTH:] = oatt_ref[...]
    mix = _dot(mix_scr[...], w12_ref[...])
    y5 = ys5_ref[...] + d_ref[...] * u_ref[...]
    g = _gelu_tanh(y5)
    s5 = g * jax.nn.sigmoid(_dot(g.astype(BF16), wglu_ref[...]) + bglu_ref[...])
    mix = mix + _dot(s5.astype(BF16), w3_ref[...])
    gate = mod_ref[0][:, 2 * D_MODEL:3 * D_MODEL]
    o_ref[...] = x_ref[...] + gate * _rms(mix, g_ref[...])


def _out_proj(x, mod3, gamma, y_ssd, o_att, y_s5, proj, s5_d, w_glu, b_glu, w_out, *, layer, tm, mod_spec):
    n = x.shape[0]
    row = lambda i: (i, 0)
    return pl.pallas_call(
        _out_proj_kernel,
        grid=(n // tm,),
        in_specs=[pl.BlockSpec((tm, D_MODEL), row),
                  mod_spec,
                  _resident((1, D_MODEL)),
                  pl.BlockSpec((tm, SSD_WIDTH), row),
                  pl.BlockSpec((tm, ATT_WIDTH), row),
                  pl.BlockSpec((tm, S5_WIDTH), row),
                  pl.BlockSpec((tm, S5_WIDTH), lambda i: (i, C_U // S5_WIDTH)),
                  _resident((1, S5_WIDTH)),
                  _layer_resident((S5_WIDTH, S5_WIDTH), layer),
                  _resident((1, S5_WIDTH)),
                  _layer_resident((SSD_WIDTH + ATT_WIDTH, D_MODEL), layer, 0),
                  _layer_resident((S5_WIDTH, D_MODEL), layer, (SSD_WIDTH + ATT_WIDTH) // S5_WIDTH)],
        out_specs=pl.BlockSpec((tm, D_MODEL), row),
        out_shape=jax.ShapeDtypeStruct((n, D_MODEL), F32),
        scratch_shapes=[pltpu.VMEM((tm, SSD_WIDTH + ATT_WIDTH), BF16)],
        compiler_params=_cparams("parallel"),
        name="out_proj",
    )(x, mod3, gamma, y_ssd, o_att, y_s5, proj, s5_d, w_glu, b_glu, w_out, w_out)


def _ffn_up_kernel(x_ref, mod_ref, gpre_ref, wg_ref, wu_ref, act_ref, h_scr):
    j = pl.program_id(1)

    def gated(h):
        return (_silu(_dot(h, wg_ref[...])) * _dot(h, wu_ref[...])).astype(BF16)

    @pl.when(j == 0)
    def _():
        mod = mod_ref[0]
        sh = mod[:, 3 * D_MODEL:4 * D_MODEL]
        sc = mod[:, 4 * D_MODEL:5 * D_MODEL]
        h = (_rms(x_ref[...], gpre_ref[...]) * (1.0 + sc) + sh).astype(BF16)
        h_scr[...] = h
        act_ref[...] = gated(h)

    @pl.when(j > 0)
    def _():
        act_ref[...] = gated(h_scr[...])


def _ffn_down_kernel(x_ref, mod_ref, gpost_ref, act_ref, wo_ref, o_ref):
    gate = mod_ref[0][:, 5 * D_MODEL:6 * D_MODEL]
    o_ref[...] = x_ref[...] + gate * _rms(_dot(act_ref[...], wo_ref[...]), gpost_ref[...])


def _ffn(x, mod3, g_pre, g_post, w_in, w_out, *, layer, tm_up, tf, tm_down, mod_spec_up, mod_spec_down):
    n = x.shape[0]
    nf = D_FF // tf
    act = pl.pallas_call(
        _ffn_up_kernel,
        grid=(n // tm_up, nf),
        in_specs=[pl.BlockSpec((tm_up, D_MODEL), lambda i, j: (i, 0)),
                  mod_spec_up,
                  _resident((1, D_MODEL)),
                  pl.BlockSpec((None, D_MODEL, tf), lambda i, j: (layer, 0, j)),
                  pl.BlockSpec((None, D_MODEL, tf), lambda i, j: (layer, 0, j + nf))],
        out_specs=pl.BlockSpec((tm_up, tf), lambda i, j: (i, j)),
        out_shape=jax.ShapeDtypeStruct((n, D_FF), BF16),
        scratch_shapes=[pltpu.VMEM((tm_up, D_MODEL), BF16)],
        compiler_params=_cparams("parallel", "arbitrary"),
        name="ffn_up",
    )(x, mod3, g_pre, w_in, w_in)
    return pl.pallas_call(
        _ffn_down_kernel,
        grid=(n // tm_down,),
        in_specs=[pl.BlockSpec((tm_down, D_MODEL), lambda i: (i, 0)),
                  mod_spec_down,
                  _resident((1, D_MODEL)),
                  pl.BlockSpec((tm_down, D_FF), lambda i: (i, 0)),
                  _layer_resident((D_FF, D_MODEL), layer)],
        out_specs=pl.BlockSpec((tm_down, D_MODEL), lambda i: (i, 0)),
        out_shape=jax.ShapeDtypeStruct((n, D_MODEL), F32),
        compiler_params=_cparams("parallel"),
        name="ffn_down",
    )(x, mod3, g_post, act, w_out)


def _attend(queries, sink_ref):
    rows = queries[0][0].shape[0]
    srow = lax.broadcasted_iota(jnp.int32, (ATT_GROUP * rows, 1), 0)
    outs = [[None] * ATT_HEADS for _ in queries]
    units = [(i, kv) for i in range(len(queries)) for kv in range(ATT_KV)]

    def masked_scores(unit):
        (q, parts), kv = queries[unit[0]], unit[1]
        heads = range(kv * ATT_GROUP, (kv + 1) * ATT_GROUP)
        q3 = jnp.concatenate([q[:, h * HEAD_DIM:(h + 1) * HEAD_DIM] for h in heads], axis=0)
        scores = []
        for kt, _, mask in parts:
            s = _dot(q3, kt[kv * HEAD_DIM:(kv + 1) * HEAD_DIM, :])
            scores.append(s if mask is None else jnp.where(mask, s, NEG))
        return scores

    def softmax_numerator(unit, scores):
        kv = unit[1]
        heads = range(kv * ATT_GROUP, (kv + 1) * ATT_GROUP)
        sink = jnp.full((ATT_GROUP * rows, 1), sink_ref[heads[-1]] * LOG2E, F32)
        for i in range(ATT_GROUP - 2, -1, -1):
            sink = jnp.where(srow < (i + 1) * rows, sink_ref[heads[i]] * LOG2E, sink)
        blocks = [s[:, j:j + LANES] for s in scores for j in range(0, s.shape[1], LANES)]
        m = jnp.maximum(sink, jnp.max(functools.reduce(jnp.maximum, blocks), axis=-1, keepdims=True))
        p = jnp.concatenate([jnp.exp2(s - m).astype(BF16) for s in scores], axis=1)
        return p, jnp.exp2(sink - m)

    def weighted_values(unit, p, sink_term):
        (_, parts), kv = queries[unit[0]], unit[1]
        lo = kv * HEAD_DIM
        vlo = (lo // LANES) * LANES
        v_first = lo == vlo
        v128 = jnp.concatenate([v[:, vlo:vlo + LANES] for _, v, _ in parts], axis=0)
        lane = lax.broadcasted_iota(jnp.int32, v128.shape, 1)
        keep = (lane < HEAD_DIM) if v_first else (lane >= HEAD_DIM)
        acc = _dot(p, jnp.where(keep, v128, jnp.ones_like(v128)))
        ocol, dcol = (0, HEAD_DIM) if v_first else (HEAD_DIM, 0)
        o3 = acc[:, ocol:ocol + HEAD_DIM] / (acc[:, dcol:dcol + 1] + sink_term)
        for i in range(ATT_GROUP):
            outs[unit[0]][kv * ATT_GROUP + i] = o3[i * rows:(i + 1) * rows]

    scores = {0: masked_scores(units[0])}
    probs = {}
    for step in range(len(units) + 2):
        if step + 1 < len(units):
            scores[step + 1] = masked_scores(units[step + 1])
        if step - 1 in probs:
            weighted_values(units[step - 1], *probs.pop(step - 1))
        if step in scores:
            probs[step] = softmax_numerator(units[step], scores.pop(step))
    return [jnp.concatenate(o, axis=-1) for o in outs]


def _ctx_attn_kernel(sink_ref, q_ref, k_ref, v_ref, o_ref):
    blk = ATT_BLOCK
    parts = [(k_ref[...], v_ref[...], None)]
    queries = [(q_ref[r:r + blk, :], parts) for r in range(0, q_ref.shape[0], blk)]
    for r, o in zip(range(0, q_ref.shape[0], blk), _attend(queries, sink_ref)):
        o_ref[r:r + blk, :] = o.astype(o_ref.dtype)


def _ctx_attention(qv, kt, sink, *, n_seq, seq_len):
    vcol = ATT_WIDTH // KV_WIDTH
    blk = ATT_BLOCK
    nb = seq_len // blk
    return pl.pallas_call(
        _ctx_attn_kernel,
        grid=(n_seq, nb),
        in_specs=[pl.BlockSpec(memory_space=pltpu.SMEM),
                  pl.BlockSpec((blk, ATT_WIDTH), lambda b, n: (b * nb + n, 0)),
                  pl.BlockSpec((KV_WIDTH, seq_len), lambda b, n: (0, b)),
                  pl.BlockSpec((seq_len, KV_WIDTH), lambda b, n: (b, vcol))],
        out_specs=pl.BlockSpec((blk, ATT_WIDTH), lambda b, n: (b * nb + n, 0)),
        out_shape=jax.ShapeDtypeStruct((n_seq * seq_len, ATT_WIDTH), BF16),
        compiler_params=_cparams("parallel", "parallel"),
        name="ctx_attention",
    )(sink, qv, kt, qv)


def _lat_attn_kernel(sink_ref, q_ref, *refs, nq):
    kt_refs, v_refs = refs[0:nq + 2], refs[nq + 2:2 * nq + 4]
    ck_ref, cv_ref, o_ref = refs[2 * nq + 4:]
    step = pl.program_id(1)
    blk = ATT_BLOCK
    row = lax.broadcasted_iota(jnp.int32, (ATT_GROUP * blk, blk), 0) % blk
    col = lax.broadcasted_iota(jnp.int32, (ATT_GROUP * blk, blk), 1)
    before = [col >= row] * nq
    after = [col <= row] * nq
    before[0] = jnp.logical_and(before[0], step > 0)
    after[-1] = jnp.logical_and(after[-1], step < pl.num_programs(1) - 1)
    ctx = (ck_ref[0], cv_ref[0], None)
    blocks = [(kt[...], v[...]) for kt, v in zip(kt_refs, v_refs)]
    queries = [(q_ref[i * blk:(i + 1) * blk, :],
                [ctx, blocks[i] + (before[i],), blocks[i + 1] + (None,), blocks[i + 2] + (after[i],)])
               for i in range(nq)]
    for i, o in enumerate(_attend(queries, sink_ref)):
        o_ref[i * blk:(i + 1) * blk, :] = o.astype(o_ref.dtype)


def _lat_attention(qv, kt, sink, ckt, cv, *, n_seq, seq_len):
    blk = ATT_BLOCK
    nb = seq_len // blk
    nq = max(c for c in range(1, LAT_Q_BLOCKS + 1) if nb % c == 0)
    steps = nb // nq

    def block(offset):
        return lambda b, n: b * nb + jnp.clip(nq * n + offset, 0, nb - 1)

    vcol = ATT_WIDTH // KV_WIDTH
    offsets = range(-1, nq + 1)
    kt_specs = [pl.BlockSpec((KV_WIDTH, blk), lambda b, n, f=block(o): (0, f(b, n))) for o in offsets]
    v_specs = [pl.BlockSpec((blk, KV_WIDTH), lambda b, n, f=block(o): (f(b, n), vcol)) for o in offsets]
    return pl.pallas_call(
        functools.partial(_lat_attn_kernel, nq=nq),
        grid=(n_seq, steps),
        in_specs=[pl.BlockSpec(memory_space=pltpu.SMEM),
                  pl.BlockSpec((nq * blk, ATT_WIDTH), lambda b, n: (b * steps + n, 0))] + kt_specs + v_specs
                 + [pl.BlockSpec((1,) + ckt.shape[1:], lambda b, n: (b, 0, 0)),
                    pl.BlockSpec((1,) + cv.shape[1:], lambda b, n: (b, 0, 0))],
        out_specs=pl.BlockSpec((nq * blk, ATT_WIDTH), lambda b, n: (b * steps + n, 0)),
        out_shape=jax.ShapeDtypeStruct((n_seq * seq_len, ATT_WIDTH), BF16),
        compiler_params=_cparams("parallel", "parallel"),
        name="lat_attention",
    )(sink, qv, *([kt] * (nq + 2)), *([qv] * (nq + 2)), ckt, cv)


def _rope_tables(seq_len):
    rows = seq_len // GRID_W
    row = np.repeat(np.arange(rows, dtype=np.float32), GRID_W)
    col = np.tile(np.arange(GRID_W, dtype=np.float32), rows)
    inv = np.float32(ROPE_BASE) ** (-np.arange(ROPE_PER_AXIS, dtype=np.float32) / np.float32(ROPE_PER_AXIS))
    ang = np.concatenate([row[:, None] * inv, col[:, None] * inv], axis=-1)
    cos, sin = np.cos(ang), np.sin(ang)
    cos128 = np.tile(cos, (1, 4))
    sin128 = np.tile(np.concatenate([-sin, sin], axis=-1), (1, 2))
    return jnp.asarray(cos128, F32), jnp.asarray(sin128, F32)


def _conv_silu(scr, prev_ref, cur_ref, next_ref, w_ref, b_ref, at_start, at_end):
    n = cur_ref.shape[0]
    h = SSD_HALO
    scr[0:h, :] = jnp.where(at_start, 0.0, prev_ref[...])
    scr[h:h + n, :] = cur_ref[...]
    scr[h + n:h + n + h, :] = jnp.where(at_end, 0.0, next_ref[...])
    return _silu(scr[h - 1:h - 1 + n, :] * w_ref[0:1, :] + scr[h:h + n, :] * w_ref[1:2, :]
                 + scr[h + 1:h + 1 + n, :] * w_ref[2:3, :] + b_ref[...])


def _softplus(x):
    return jnp.maximum(x, 0.0) + jnp.log1p(jnp.exp(-jnp.abs(x)))


def _ssd_bwd_kernel(xsp_ref, xsc_ref, xsn_ref, bcp_ref, bcc_ref, bcn_ref, dt_ref, h0_ref,
                    cwx_ref, cbx_ref, cwb_ref, cbb_ref, dtb_ref, alog_ref, tri_ref, exp_ref,
                    hstart_ref, hfin_ref, xbc_ref, xs_scr, bc_scr, h_scr):
    i = pl.program_id(1)
    q = SSD_CHUNK
    half = SSD_WIDTH // SSD_GROUPS
    sel_b = exp_ref[1]
    neg_a = -LOG2E * jnp.exp(alog_ref[...])
    at_start = i == pl.num_programs(1) - 1
    at_end = i == 0
    xbc_ref[:, 0:SSD_WIDTH] = _conv_silu(xs_scr, xsp_ref, xsc_ref, xsn_ref, cwx_ref, cbx_ref, at_start, at_end)
    xbc_ref[:, SSD_WIDTH:] = _conv_silu(bc_scr, bcp_ref, bcc_ref, bcn_ref, cwb_ref, cbb_ref, at_start, at_end)

    @pl.when(i == 0)
    def _():
        h_scr[...] = h0_ref[0]

    hb = h_scr[...]
    for j in reversed(range(xsc_ref.shape[0] // q)):
        rows = slice(j * q, (j + 1) * q)
        hstart_ref[0, j] = hb
        bm = xbc_ref[rows, SSD_WIDTH:SSD_WIDTH + SSD_GROUPS * SSD_STATE].astype(BF16)
        dtv = _softplus(dt_ref[rows, :] + dtb_ref[...])
        rev = _dot_sel_lhs(tri_ref[1], dtv * neg_a, 3)
        dt_w = _dot_sel_rhs(dtv, sel_b, 2)
        to_end = _dot_sel_rhs(jnp.exp2(rev[0:1, :] - rev), sel_b, 2)
        decay = _dot_sel_rhs(jnp.exp2(rev[0:SSD_HALO, :]), sel_b, 2)[0:1, :]
        xw = (xbc_ref[rows, 0:SSD_WIDTH] * dt_w * to_end).astype(BF16)
        upd = jnp.concatenate(
            [_dot_tn(bm[:, g * SSD_STATE:(g + 1) * SSD_STATE], xw[:, g * half:(g + 1) * half])
             for g in range(SSD_GROUPS)], axis=-1)
        hb = hb * decay + upd
    h_scr[...] = hb

    @pl.when(i == pl.num_programs(1) - 1)
    def _():
        hfin_ref[0] = hb.T


def _ssd_intra(xs, bc, dt_raw, dtb_ref, alog_ref, tri_ref, exp_ref, dvec_ref):
    q = SSD_CHUNK
    nh = SSD_HEADS
    gs = SSD_GROUPS * SSD_STATE
    bm = bc[:, 0:gs].astype(BF16)
    cm = bc[:, gs:2 * gs].astype(BF16)
    xs_bf = xs.astype(BF16)

    dtv = _softplus(dt_raw + dtb_ref[...])
    dta = dtv * (-LOG2E * jnp.exp(alog_ref[...]))
    lower, upper = tri_ref[0], tri_ref[1]
    cum = _dot_sel_lhs(lower, dta, 3)
    rev = _dot_sel_lhs(upper, dta, 3)
    dt_t = dtv.T
    dta_t = dta.T
    cum_t = _dot_sel_rhs(dta_t, upper, 3)
    rev_t = _dot_sel_rhs(dta_t, lower, 3)

    sel_f, sel_b = exp_ref[0], exp_ref[1]
    dtf_w = _dot_sel_rhs(dtv, sel_f, 2)
    ecum_w = _dot_sel_rhs(jnp.exp2(cum), sel_f, 2)
    erev_w = _dot_sel_rhs(jnp.exp2(rev), sel_b, 2)
    toend_w = _dot_sel_rhs(jnp.exp2(cum[q - 1:q, :] - cum), sel_f, 2)

    row = lax.broadcasted_iota(jnp.int32, (q, q), 0)
    col = lax.broadcasted_iota(jnp.int32, (q, q), 1)
    below = row > col
    log_dt_t = jnp.log2(dt_t)
    src_f = log_dt_t - cum_t
    src_b = log_dt_t - rev_t
    gmat = [_dot_nt(cm[:, g * SSD_STATE:(g + 1) * SSD_STATE], bm[:, g * SSD_STATE:(g + 1) * SSD_STATE])
            for g in range(SSD_GROUPS)]
    per_group = nh // SSD_GROUPS
    half = SSD_WIDTH // SSD_GROUPS
    ys = []
    for h in range(nh):
        e = jnp.where(below, cum[:, h:h + 1] + src_f[h:h + 1, :],
                      rev[:, nh + h:nh + h + 1] + src_b[nh + h:nh + h + 1, :])
        m = (gmat[h // per_group] * jnp.exp2(e)).astype(BF16)
        ys.append(_dot(m, xs_bf[:, h * SSD_HEAD_DIM:(h + 1) * SSD_HEAD_DIM]))
    cb = cm.astype(F32) * bm.astype(F32)
    g_diag = jnp.concatenate(
        [jnp.broadcast_to(jnp.sum(cb[:, g * SSD_STATE:(g + 1) * SSD_STATE], axis=-1, keepdims=True), (q, half))
         for g in range(SSD_GROUPS)], axis=-1)
    y = jnp.concatenate(ys, axis=-1) + (g_diag * dtf_w + dvec_ref[...]) * xs
    xw = (xs * dtf_w * toend_w).astype(BF16)
    upd = jnp.concatenate(
        [_dot_tn(bm[:, g * SSD_STATE:(g + 1) * SSD_STATE], xw[:, g * half:(g + 1) * half])
         for g in range(SSD_GROUPS)], axis=-1)
    return y, cm, ecum_w, erev_w, upd


def _ssd_carry(hf, hb_start, gz, intra, nw_ref):
    y, cm, ecum_w, erev_w, upd = intra
    half = SSD_WIDTH // SSD_GROUPS
    hf_bf = hf.astype(BF16)
    hb_bf = hb_start.astype(BF16)
    off_f = jnp.concatenate([_dot(cm[:, g * SSD_STATE:(g + 1) * SSD_STATE], hf_bf[:, g * half:(g + 1) * half])
                             for g in range(SSD_GROUPS)], axis=-1)
    off_b = jnp.concatenate([_dot(cm[:, g * SSD_STATE:(g + 1) * SSD_STATE], hb_bf[:, g * half:(g + 1) * half])
                             for g in range(SSD_GROUPS)], axis=-1)
    y = y + off_f * ecum_w + off_b * erev_w
    y = _rms(y * gz, nw_ref[...])
    return y, hf * ecum_w[SSD_CHUNK - 1:SSD_CHUNK, :] + upd


def _ssd_fwd_kernel(gz_ref, xs_ref, bc_ref, dt_ref, hb_ref, h0_ref,
                    dtb_ref, alog_ref, tri_ref, exp_ref, dvec_ref, nw_ref, y_ref, hfin_ref, h_scr):
    c = pl.program_id(1)
    q = SSD_CHUNK
    cps = xs_ref.shape[0] // q
    rows = [slice(j * q, (j + 1) * q) for j in range(cps)]

    def intra(j):
        return _ssd_intra(xs_ref[rows[j], :], bc_ref[rows[j], :], dt_ref[rows[j], :],
                          dtb_ref, alog_ref, tri_ref, exp_ref, dvec_ref)

    @pl.when(c == 0)
    def _():
        h_scr[...] = h0_ref[0]

    hf = h_scr[...]
    ahead = intra(0)
    for j in range(cps):
        cur = ahead
        if j + 1 < cps:
            ahead = intra(j + 1)
        y, hf = _ssd_carry(hf, hb_ref[0, j], gz_ref[rows[j], :], cur, nw_ref)
        y_ref[rows[j], :] = y.astype(y_ref.dtype)
    h_scr[...] = hf

    @pl.when(c == pl.num_programs(1) - 1)
    def _():
        hfin_ref[0] = hf.T


def _ssd_mixer(proj, h0_f, h0_b, consts, *, n_seq, seq_len):
    q = SSD_CHUNK
    nc = seq_len // q
    cps = max(c for c in range(1, SSD_CHUNKS_PER_STEP + 1) if nc % c == 0)
    ns = nc // cps
    rows = cps * q
    (cwx, cbx, cwb, cbb, dt_bias, a_log, tri, expand, dvec, nw) = consts
    per8 = rows // SSD_HALO
    last8 = proj.shape[0] // SSD_HALO - 1

    def raw_specs(width, col):
        step = lambda b, i: b * ns + ns - 1 - i
        return [pl.BlockSpec((SSD_HALO, width), lambda b, i: (jnp.maximum(step(b, i) * per8 - 1, 0), col)),
                pl.BlockSpec((rows, width), lambda b, i: (step(b, i), col)),
                pl.BlockSpec((SSD_HALO, width), lambda b, i: (jnp.minimum((step(b, i) + 1) * per8, last8), col))]

    def specs(chunk_of):
        def cur(b, i):
            return b * ns + chunk_of(i)

        return [pl.BlockSpec((rows, SSD_WIDTH), lambda b, i: (cur(b, i), 0)),
                pl.BlockSpec((rows, SSD_BC), lambda b, i: (cur(b, i), SSD_WIDTH // SSD_BC)),
                pl.BlockSpec((rows, DT_PAD), lambda b, i: (cur(b, i), C_DT // DT_PAD))], cur

    state_spec = pl.BlockSpec((1, SSD_STATE, SSD_WIDTH), lambda b, i: (b, 0, 0))
    final_spec = pl.BlockSpec((1, SSD_WIDTH, SSD_STATE), lambda b, i: (b, 0, 0))
    const_specs = [_resident(dt_bias.shape), _resident(a_log.shape), _resident(tri.shape),
                   _resident(expand.shape)]
    scratch = [pltpu.VMEM((SSD_STATE, SSD_WIDTH), F32)]

    data_specs, rcur = specs(lambda i: ns - 1 - i)
    hb_start, hb_fin, xbc = pl.pallas_call(
        _ssd_bwd_kernel,
        grid=(n_seq, ns),
        in_specs=(raw_specs(SSD_WIDTH, C_XS // SSD_WIDTH) + raw_specs(SSD_BC, C_BC // SSD_BC)
                  + [data_specs[2], state_spec]
                  + [_resident(cwx.shape), _resident(cbx.shape), _resident(cwb.shape), _resident(cbb.shape)]
                  + const_specs),
        out_specs=[pl.BlockSpec((1, cps, SSD_STATE, SSD_WIDTH), lambda b, i: (b, ns - 1 - i, 0, 0)), final_spec,
                   pl.BlockSpec((rows, SSD_WIDTH + SSD_BC), lambda b, i: (rcur(b, i), 0))],
        out_shape=[jax.ShapeDtypeStruct((n_seq, nc, SSD_STATE, SSD_WIDTH), F32),
                   jax.ShapeDtypeStruct((n_seq, SSD_WIDTH, SSD_STATE), F32),
                   jax.ShapeDtypeStruct((n_seq * seq_len, SSD_WIDTH + SSD_BC), F32)],
        scratch_shapes=[pltpu.VMEM((rows + 2 * SSD_HALO, SSD_WIDTH), F32),
                        pltpu.VMEM((rows + 2 * SSD_HALO, SSD_BC), F32)] + scratch,
        compiler_params=_cparams("parallel", "arbitrary"),
        name="ssd_backward_states",
    )(proj, proj, proj, proj, proj, proj, proj, h0_b, cwx, cbx, cwb, cbb, dt_bias, a_log, tri, expand)

    data_specs, cur = specs(lambda i: i)
    y, hf_fin = pl.pallas_call(
        _ssd_fwd_kernel,
        grid=(n_seq, ns),
        in_specs=([pl.BlockSpec((rows, SSD_WIDTH), lambda b, i: (cur(b, i), C_Z // SSD_WIDTH))] + data_specs
                  + [pl.BlockSpec((1, cps, SSD_STATE, SSD_WIDTH), lambda b, i: (b, i, 0, 0)), state_spec]
                  + const_specs + [_resident(dvec.shape), _resident(nw.shape)]),
        out_specs=[pl.BlockSpec((rows, SSD_WIDTH), lambda b, i: (b * ns + i, 0)), final_spec],
        out_shape=[jax.ShapeDtypeStruct((n_seq * seq_len, SSD_WIDTH), BF16),
                   jax.ShapeDtypeStruct((n_seq, SSD_WIDTH, SSD_STATE), F32)],
        scratch_shapes=scratch,
        compiler_params=_cparams("parallel", "arbitrary"),
        name="ssd_forward",
    )(proj, xbc, xbc, proj, hb_start, h0_f, dt_bias, a_log, tri, expand, dvec, nw)
    return y, hf_fin, hb_fin


def _ssd_consts(l, ssd_conv_w, ssd_conv_b, ssd_dt_bias, ssd_A_log, ssd_D, ssd_norm):
    q = SSD_CHUNK
    cw, cb = ssd_conv_w[l], ssd_conv_b[l]
    pad = DT_PAD - 2 * SSD_HEADS
    dt_bias = jnp.pad(ssd_dt_bias[l].reshape(1, -1), ((0, 0), (0, pad)))
    a_log = jnp.pad(ssd_A_log[l].reshape(1, -1), ((0, 0), (0, pad)))
    r = np.arange(q)
    tri = jnp.asarray(np.stack([r[None, :] <= r[:, None], r[None, :] >= r[:, None]]), BF16)
    head_of_col = np.arange(SSD_WIDTH) // SSD_HEAD_DIM
    j = np.arange(DT_PAD)
    expand = jnp.asarray(np.stack([j[:, None] == head_of_col[None, :],
                                   j[:, None] == head_of_col[None, :] + SSD_HEADS]), BF16)
    dvec = jnp.repeat(ssd_D[l], SSD_HEAD_DIM).reshape(1, SSD_WIDTH)
    return (cw[:, :SSD_WIDTH], cb[:SSD_WIDTH].reshape(1, -1), cw[:, SSD_WIDTH:], cb[SSD_WIDTH:].reshape(1, -1),
            dt_bias, a_log, tri, expand, dvec, ssd_norm[l].reshape(1, -1))


def _ssd_state_in(s):
    b = s.shape[0]
    return jnp.transpose(s, (0, 3, 1, 2)).reshape(b, SSD_STATE, SSD_WIDTH)


def _s5_table_kernel(pw_ref, ct_ref, bb_ref, bbt_ref, sel_ref, tile_ref, rhs_ref, woff_ref):
    t = S5_T
    c = S5_CH
    tile = tile_ref[...]
    spread = lambda x, sel: _dot_sel_rhs(x, sel, 2)
    krows, offs, cols = [], [], []
    for d in range(2):
        pw_re, pw_im = pw_ref[0, 2 * d], pw_ref[0, 2 * d + 1]
        c_re = spread(ct_ref[0, :, (2 * d) * c:(2 * d + 1) * c], tile)
        c_im = spread(ct_ref[0, :, (2 * d + 1) * c:(2 * d + 2) * c], tile)
        b_re = spread(bb_ref[0, :, (2 * d) * c:(2 * d + 1) * c], tile)
        b_im = spread(bb_ref[0, :, (2 * d + 1) * c:(2 * d + 2) * c], tile)
        bt_re = bbt_ref[0, (2 * d) * c:(2 * d + 1) * c, :]
        bt_im = bbt_ref[0, (2 * d + 1) * c:(2 * d + 2) * c, :]
        up = (spread(pw_re, sel_ref[0]), spread(pw_im, sel_ref[0]))
        down = (spread(pw_re, sel_ref[1]), spread(pw_im, sel_ref[1]))
        (q_re, q_im), state = ((up, down), (down, up))[d]
        g_re = q_re * c_re - q_im * c_im
        g_im = q_re * c_im + q_im * c_re
        krows.append(_dot_wide(bt_re, g_re) - _dot_wide(bt_im, g_im))
        lb_re, lb_im = pw_re[:, 1:2], pw_im[:, 1:2]
        r_re = q_re * lb_re - q_im * lb_im
        r_im = q_re * lb_im + q_im * lb_re
        offs.append((r_re * c_re - r_im * c_im, -(r_re * c_im + r_im * c_re)))
        s_re, s_im = state
        cols.append(((s_re * b_re - s_im * b_im).T, (s_re * b_im + s_im * b_re).T))
    k_f, k_b = krows
    for s in range(t):
        r = t - 1 - s
        f = k_f if s == 0 else jnp.concatenate([jnp.zeros((c, c * s), F32), k_f[:, :S5_TW - c * s]], axis=1)
        b = k_b if r == 0 else jnp.concatenate([k_b[:, c * r:], jnp.zeros((c, c * r), F32)], axis=1)
        rhs_ref[0, s * c:(s + 1) * c, 0:S5_TW] = (f + b).astype(BF16)
    rhs_ref[0, :, S5_TW:] = jnp.concatenate([cols[0][0], cols[1][0], cols[0][1], cols[1][1]], axis=1).astype(BF16)
    woff_ref[0] = jnp.concatenate([offs[0][0], offs[1][0], offs[0][1], offs[1][1]], axis=0).astype(BF16)


def _s5_tables(s5_A_re, s5_A_im, s5_log_dt, s5_B_re, s5_B_im, s5_C_re, s5_C_im):
    t = S5_T
    n = N_LAYERS * S5_GROUPS
    step = jnp.exp(s5_log_dt)[..., None]
    k = jnp.arange(LANES, dtype=F32)
    keep = k <= t
    kk = jnp.where(keep, k, 0.0)
    mag = jnp.exp(kk * (s5_A_re * step)[..., None])
    ang = kk * (s5_A_im * step)[..., None]
    pw_re = jnp.where(keep, mag * jnp.cos(ang), 0.0)
    pw_im = jnp.where(keep, mag * jnp.sin(ang), 0.0)
    lb_re, lb_im = pw_re[..., 1], pw_im[..., 1]
    den = s5_A_re * s5_A_re + s5_A_im * s5_A_im
    r_re = ((lb_re - 1.0) * s5_A_re + lb_im * s5_A_im) / den
    r_im = (lb_im * s5_A_re - (lb_re - 1.0) * s5_A_im) / den
    b_re, b_im = s5_B_re[:, None], s5_B_im[:, None]
    bb_re = r_re[..., None] * b_re - r_im[..., None] * b_im
    bb_im = r_re[..., None] * b_im + r_im[..., None] * b_re
    c_re = jnp.swapaxes(s5_C_re, -1, -2)
    c_im = jnp.swapaxes(s5_C_im, -1, -2)

    def pack(re, im, axis):
        return jnp.concatenate([re[:, 0], im[:, 0], re[:, 1], im[:, 1]], axis=axis)

    pw = jnp.stack([pw_re[:, 0], pw_im[:, 0], pw_re[:, 1], pw_im[:, 1]], axis=2).reshape(n, 4, S5_P, LANES)
    ct = pack(c_re, c_im, -1).reshape(n, S5_P, 4 * S5_CH)
    bb = pack(bb_re, bb_im, -1)
    bbt = jnp.swapaxes(bb, -1, -2).reshape(n, 4 * S5_CH, S5_P)
    bb = bb.reshape(n, S5_P, 4 * S5_CH)
    lam_t = jnp.concatenate([pw_re[:, 0, :, :, t], pw_re[:, 1, :, :, t], pw_im[:, 0, :, :, t],
                             pw_im[:, 1, :, :, t]], axis=-1).reshape(n, 1, S5_ST)
    jt = np.arange(S5_TW) // S5_CH
    lane_k = np.arange(LANES)[:, None]
    sel = jnp.asarray(np.stack([lane_k == f[None, :] for f in (jt, t - 1 - jt)]), BF16)
    tile = jnp.asarray(np.arange(S5_CH)[:, None] == (np.arange(S5_TW) % S5_CH)[None, :], BF16)
    grp = lambda shape: pl.BlockSpec((1,) + shape, lambda i: (i,) + (0,) * len(shape))
    rhs, woff = pl.pallas_call(
        _s5_table_kernel,
        grid=(n,),
        in_specs=[grp((4, S5_P, LANES)), grp((S5_P, 4 * S5_CH)), grp((S5_P, 4 * S5_CH)), grp((4 * S5_CH, S5_P)),
                  _resident(sel.shape), _resident(tile.shape)],
        out_specs=[grp((S5_TW, S5_TW + S5_ST)), grp((S5_ST, S5_TW))],
        out_shape=[jax.ShapeDtypeStruct((n, S5_TW, S5_TW + S5_ST), BF16),
                   jax.ShapeDtypeStruct((n, S5_ST, S5_TW), BF16)],
        compiler_params=_cparams("parallel"),
        name="s5_tables",
    )(pw, ct, bb, bbt, sel, tile)
    return rhs, woff, lam_t


def _unit_transpose(vs, masks, axis, unit):
    size = vs[0].shape[axis]
    cur = list(vs)
    for d in (1, 2, 4):
        nxt = list(cur)
        for a in range(S5_OCT):
            if a & d:
                continue
            b = a | d
            nxt[a] = jnp.where(masks[d], cur[a], pltpu.roll(cur[b], unit * d, axis=axis))
            nxt[b] = jnp.where(masks[d], pltpu.roll(cur[a], size - unit * d, axis=axis), cur[b])
        cur = nxt
    return cur


def _s5_kernel(u_ref, rhs_ref, woff_ref, lam_ref, h0_ref, y_ref, hfin_ref,
               dst_scr, y_scr, sre_scr, sim_scr, hfre_scr, hfim_scr, hbre_scr, hbim_scr, *, nc, rb):
    t = S5_T
    rows = dst_scr.shape[1]
    sb = rows // nc
    tiles = t // S5_OCT
    nr = rb // S5_OCT
    tile_shape = (nr, S5_OCT, LANES)
    seg = lax.broadcasted_iota(jnp.int32, tile_shape, 2) // S5_CH
    sub = lax.broadcasted_iota(jnp.int32, tile_shape, 1)
    lane_masks = {d: (seg & d) == 0 for d in (1, 2, 4)}
    sub_masks = {d: (sub & d) == 0 for d in (1, 2, 4)}

    def tile_rows(i, r, tq):
        return pl.ds((i * rb + r) * tiles + tq, nr, stride=S5_OCT * tiles)

    def gather(i, carry):
        r0 = pl.multiple_of(i * rb, rb)
        for tq in range(tiles):
            by_chunk = [u_ref[tile_rows(i, r, tq)] for r in range(S5_OCT)]
            by_time = _unit_transpose(by_chunk, sub_masks, 1, 1)
            for gl, v in enumerate(_unit_transpose(by_time, lane_masks, 2, S5_CH)):
                dst_scr[gl, pl.ds(r0, rb), tq * LANES:(tq + 1) * LANES] = v.reshape(rb, LANES).astype(BF16)
        return carry

    lax.fori_loop(0, rows // rb, gather, 0)

    for gl in range(S5_OCT):
        z = _dot(dst_scr[gl], rhs_ref[gl])
        y_scr[gl] = z[:, 0:S5_TW]
        sre_scr[gl] = z[:, S5_TW:S5_TW + 2 * S5_P]
        sim_scr[gl] = z[:, S5_TW + 2 * S5_P:]

    fwd_lane = lax.broadcasted_iota(jnp.int32, (sb, 2 * S5_P), 1) < S5_P
    a_re = [lam_ref[gl][:, 0:2 * S5_P] for gl in range(S5_OCT)]
    a_im = [lam_ref[gl][:, 2 * S5_P:] for gl in range(S5_OCT)]

    def step(i, carry):
        fi = pl.ds(i, sb, stride=nc)
        bj = pl.ds(nc - 1 - i, sb, stride=nc)
        out = []
        for gl in range(S5_OCT):
            h_re, h_im = carry[2 * gl], carry[2 * gl + 1]
            hfre_scr[gl, fi, :] = h_re
            hfim_scr[gl, fi, :] = h_im
            hbre_scr[gl, bj, :] = h_re
            hbim_scr[gl, bj, :] = h_im
            s_re = jnp.where(fwd_lane, sre_scr[gl, fi, :], sre_scr[gl, bj, :])
            s_im = jnp.where(fwd_lane, sim_scr[gl, fi, :], sim_scr[gl, bj, :])
            out.append(a_re[gl] * h_re - a_im[gl] * h_im + s_re)
            out.append(a_re[gl] * h_im + a_im[gl] * h_re + s_im)
        return tuple(out)

    init = []
    for gl in range(S5_OCT):
        h0 = h0_ref[gl, 0]
        init += [h0[:, 0:2 * S5_P], h0[:, 2 * S5_P:]]
    fin = lax.fori_loop(0, nc, step, tuple(init), unroll=2 if nc % 2 == 0 else 1)

    all_fwd = lax.broadcasted_iota(jnp.int32, (rows, 2 * S5_P), 1) < S5_P
    for gl in range(S5_OCT):
        hfin_ref[gl, 0] = jnp.concatenate([fin[2 * gl], fin[2 * gl + 1]], axis=-1)
        hin = jnp.concatenate([jnp.where(all_fwd, hfre_scr[gl], hbre_scr[gl]),
                               jnp.where(all_fwd, hfim_scr[gl], hbim_scr[gl])], axis=-1).astype(BF16)
        y_scr[gl] += _dot(hin, woff_ref[gl])

    def scatter(i, carry):
        r0 = pl.multiple_of(i * rb, rb)
        for tq in range(tiles):
            by_group = [y_scr[gl, pl.ds(r0, rb), tq * LANES:(tq + 1) * LANES].reshape(tile_shape)
                        for gl in range(S5_OCT)]
            by_time = _unit_transpose(by_group, lane_masks, 2, S5_CH)
            for r, v in enumerate(_unit_transpose(by_time, sub_masks, 1, 1)):
                y_ref[tile_rows(i, r, tq)] = v
        return carry

    lax.fori_loop(0, rows // rb, scatter, 0)


def _s5_mixer(proj, tables, layer, h0, *, n_seq, seq_len):
    rhs, woff, lam_t = tables
    t = S5_T
    nc = seq_len // t
    sb = max(1, min(n_seq, S5_MAX_ROWS // nc))
    assert n_seq % sb == 0
    nbs = n_seq // sb
    rows = sb * nc
    rb = min(rows, 32)
    noct = S5_GROUPS // S5_OCT
    ucol = C_U // LANES
    tab = lambda shape: pl.BlockSpec((S5_OCT,) + shape, lambda j, s: (layer * noct + j, 0, 0))
    state = pl.BlockSpec((S5_OCT, 1, sb, S5_ST), lambda j, s: (j, s, 0, 0))
    rows_scr = lambda dt: pltpu.VMEM((S5_OCT, rows, 2 * S5_P), dt)
    n_tok = n_seq * seq_len
    tok_block = (sb * seq_len // S5_OCT, S5_OCT, LANES)
    y, hfin = pl.pallas_call(
        functools.partial(_s5_kernel, nc=nc, rb=rb),
        grid=(noct, nbs),
        in_specs=[pl.BlockSpec(tok_block, lambda j, s: (s, 0, ucol + j)),
                  tab(rhs.shape[1:]), tab(woff.shape[1:]), tab(lam_t.shape[1:]), state],
        out_specs=[pl.BlockSpec(tok_block, lambda j, s: (s, 0, j)), state],
        out_shape=[jax.ShapeDtypeStruct((n_tok // S5_OCT, S5_OCT, S5_WIDTH), F32),
                   jax.ShapeDtypeStruct((S5_GROUPS, nbs, sb, S5_ST), F32)],
        scratch_shapes=[pltpu.VMEM((S5_OCT, rows, S5_TW), BF16), pltpu.VMEM((S5_OCT, rows, S5_TW), F32),
                        rows_scr(F32), rows_scr(F32), rows_scr(F32), rows_scr(F32), rows_scr(F32), rows_scr(F32)],
        compiler_params=_cparams("parallel", "parallel"),
        name="s5_mixer",
    )(proj.reshape(n_tok // S5_OCT, S5_OCT, D_INP), rhs, woff, lam_t, h0.reshape(S5_GROUPS, nbs, sb, S5_ST))
    return y.reshape(n_tok, S5_WIDTH), hfin.reshape(S5_GROUPS, n_seq, S5_ST)


def _s5_state_in(re, im):
    b = re.shape[0]
    x = jnp.stack([re, im], axis=1)
    return jnp.transpose(x, (3, 0, 1, 2, 4)).reshape(S5_GROUPS, b, S5_ST)


def _s5_state_out(h):
    n_seq = h.shape[1]
    x = h.reshape(S5_GROUPS, n_seq, 2, 2, S5_P)
    x = jnp.transpose(x, (2, 1, 3, 0, 4))
    return x[0], x[1]


def kernel(x_prompt, x_sample, c, cache_k, cache_v, state_ssd, state_s5_re, state_s5_im, c_ctx, w_ada, b_ada, norm_mix_pre, norm_mix_post, norm_ffn_pre, norm_ffn_post, w_in, w_out, ssd_conv_w, ssd_conv_b, ssd_dt_bias, ssd_A_log, ssd_D, ssd_norm, attn_sink, s5_A_re, s5_A_im, s5_log_dt, s5_B_re, s5_B_im, s5_C_re, s5_C_im, s5_D, s5_w_glu, s5_b_glu, w_ffn_in, w_ffn_out):
    nb_ctx, len_ctx, _ = x_prompt.shape
    nb_lat, len_lat, _ = x_sample.shape
    n_ctx = nb_ctx * len_ctx
    n_lat = nb_lat * len_lat
    tm = 512
    mod_rows = -(-(1 + nb_lat) // 8) * 8
    cvecs = jnp.concatenate([c_ctx[None, :], c, jnp.zeros((mod_rows - 1 - nb_lat, D_MODEL), F32)], axis=0)
    mods = _ada(cvecs, w_ada, b_ada)
    mod_geom = ((n_ctx, 0), (len_lat, 1))
    token_sets = tuple(dict(mod_spec=_mod_spec(tm, *g)) for g in mod_geom)
    tm_up = min(FFN_UP_ROWS, n_ctx, len_lat)
    rope_args = (dict(), dict(rope_tables=_rope_tables(len_lat), seq_len=len_lat))

    s5_tab = _s5_tables(s5_A_re, s5_A_im, s5_log_dt, s5_B_re, s5_B_im, s5_C_re, s5_C_im)
    w_in_b = w_in.astype(BF16)
    w_in_p = jnp.concatenate(
        [w_in_b[:, :, O_Q:O_K], w_in_b[:, :, O_Z:O_DT], w_in_b[:, :, O_K:O_END], w_in_b[:, :, O_DT:O_Q],
         jnp.zeros((N_LAYERS, D_MODEL, DT_PAD - 2 * SSD_HEADS), BF16)], axis=2)
    w_out_b = w_out.astype(BF16)
    w_glu_b = s5_w_glu.astype(BF16)
    w_ffn_in_b = w_ffn_in.astype(BF16)
    w_ffn_out_b = w_ffn_out.astype(BF16)
    zeros_ssd = jnp.zeros((nb_ctx, SSD_STATE, SSD_WIDTH), F32)
    zeros_s5 = jnp.zeros((S5_GROUPS, nb_ctx, S5_ST), F32)

    xs = [x_prompt.reshape(n_ctx, D_MODEL), x_sample.reshape(n_lat, D_MODEL)]
    new_k, new_v, new_ssd, new_s5_re, new_s5_im = [], [], [], [], []
    for l in range(N_LAYERS):
        mod3 = mods[l].reshape(mod_rows, 1, 6 * D_MODEL)
        proj, qv, kt = zip(*[_in_proj(x, mod3, norm_mix_pre[l].reshape(1, -1), w_in_p, layer=l, tm=tm, **ts, **ra)
                             for x, ts, ra in zip(xs, token_sets, rope_args)])

        sink = attn_sink[l]
        ckt = jnp.swapaxes(cache_k[:, l].reshape(nb_lat, -1, KV_WIDTH), 1, 2).astype(BF16)
        cv = cache_v[:, l].reshape(nb_lat, -1, KV_WIDTH).astype(BF16)
        o_att = [_ctx_attention(qv[0], kt[0], sink, n_seq=nb_ctx, seq_len=len_ctx),
                 _lat_attention(qv[1], kt[1], sink, ckt, cv, n_seq=nb_lat, seq_len=len_lat)]

        consts = _ssd_consts(l, ssd_conv_w, ssd_conv_b, ssd_dt_bias, ssd_A_log, ssd_D, ssd_norm)
        y_ctx, hf_ctx, hb_ctx = _ssd_mixer(proj[0], zeros_ssd, zeros_ssd, consts, n_seq=nb_ctx, seq_len=len_ctx)
        y_lat, _, _ = _ssd_mixer(proj[1], _ssd_state_in(state_ssd[:, l, 0]), _ssd_state_in(state_ssd[:, l, 1]),
                                 consts, n_seq=nb_lat, seq_len=len_lat)
        y_ssd = [y_ctx, y_lat]

        s5_ctx, s5_fin = _s5_mixer(proj[0], s5_tab, l, zeros_s5, n_seq=nb_ctx, seq_len=len_ctx)
        s5_lat, _ = _s5_mixer(proj[1], s5_tab, l, _s5_state_in(state_s5_re[:, l], state_s5_im[:, l]),
                              n_seq=nb_lat, seq_len=len_lat)
        y_s5 = [s5_ctx, s5_lat]

        xs = [_out_proj(x, mod3, norm_mix_post[l].reshape(1, -1), y_ssd[i], o_att[i], y_s5[i], proj[i],
                        s5_D[l].reshape(1, -1), w_glu_b, s5_b_glu[l].reshape(1, -1), w_out_b,
                        layer=l, tm=tm, **ts)
              for i, (x, ts) in enumerate(zip(xs, token_sets))]
        xs = [_ffn(x, mod3, norm_ffn_pre[l].reshape(1, -1), norm_ffn_post[l].reshape(1, -1),
                   w_ffn_in_b, w_ffn_out_b, layer=l, tm_up=tm_up, tf=FFN_UP_COLS, tm_down=tm,
                   mod_spec_up=_mod_spec(tm_up, *g), mod_spec_down=_mod_spec(tm, *g))
              for x, g in zip(xs, mod_geom)]

        new_k.append(proj[0][:, C_K:C_K + KV_WIDTH].reshape(nb_ctx, len_ctx, ATT_KV, HEAD_DIM))
        new_v.append(proj[0][:, C_V:C_V + KV_WIDTH].reshape(nb_ctx, len_ctx, ATT_KV, HEAD_DIM))
        new_ssd += [hf_ctx, hb_ctx]
        re, im = _s5_state_out(s5_fin)
        new_s5_re.append(re)
        new_s5_im.append(im)

    y_prompt = xs[0].reshape(nb_ctx, len_ctx, D_MODEL)
    y_sample = xs[1].reshape(nb_lat, len_lat, D_MODEL)
    new_state_ssd = jnp.stack(new_ssd, axis=1).reshape(nb_ctx, N_LAYERS, 2, SSD_HEADS, SSD_HEAD_DIM, SSD_STATE)
    return (y_prompt, y_sample, jnp.stack(new_k, axis=1), jnp.stack(new_v, axis=1), new_state_ssd,
            jnp.stack(new_s5_re, axis=1), jnp.stack(new_s5_im, axis=1))
```

```python
import functools
import math

import jax
import jax.numpy as jnp
import numpy as np
from jax import lax
from jax.experimental import pallas as pl
from jax.experimental.pallas import tpu as pltpu

F32 = jnp.float32
BF16 = jnp.bfloat16

D_MODEL = 2048
N_LAYERS = 2
D_FF = 5632
GRID_W = 64
EPS = 1e-6
LANES = 128

SSD_HEADS = 12
SSD_HEAD_DIM = 64
SSD_WIDTH = SSD_HEADS * SSD_HEAD_DIM
SSD_GROUPS = 2
SSD_STATE = 64
SSD_BC = 2 * SSD_GROUPS * SSD_STATE
SSD_CHUNK = 128
SSD_CHUNKS_PER_STEP = 8
SSD_HALO = 8
ATT_HEADS = 12
ATT_KV = 4
ATT_GROUP = ATT_HEADS // ATT_KV
HEAD_DIM = 64
ATT_WIDTH = ATT_HEADS * HEAD_DIM
KV_WIDTH = ATT_KV * HEAD_DIM
ATT_BLOCK = 128
LAT_Q_BLOCKS = 4
QV_WIDTH = ATT_WIDTH + KV_WIDTH
LOG2E = math.log2(math.e)
Q_SCALE = HEAD_DIM ** -0.5 * LOG2E
ROPE_PER_AXIS = HEAD_DIM // 4
ROPE_BASE = 10000.0
S5_GROUPS = 32
S5_CH = 16
S5_WIDTH = S5_GROUPS * S5_CH
S5_P = 64
S5_T = 32
S5_TW = S5_T * S5_CH
S5_ST = 4 * S5_P
S5_OCT = LANES // S5_CH
S5_MAX_ROWS = 256
MIX_WIDTH = SSD_WIDTH + ATT_WIDTH + S5_WIDTH

O_Z = 0
O_XS = SSD_WIDTH
O_BC = O_XS + SSD_WIDTH
O_DT = O_BC + SSD_BC
O_Q = O_DT + 2 * SSD_HEADS
O_K = O_Q + ATT_WIDTH
O_V = O_K + KV_WIDTH
O_U = O_V + KV_WIDTH
O_END = O_U + S5_WIDTH
C_Q = 0
C_Z = 768
C_XS = 1536
C_BC = 2304
C_K = 2560
C_V = 2816
C_U = 3072
C_DT = 3584
DT_PAD = 128
D_INP = C_DT + DT_PAD

FFN_UP_ROWS = 1024
FFN_UP_COLS = 512

NEG = -1e30
VMEM_LIMIT = 56 * 1024 * 1024


def _cparams(*sem):
    return pltpu.CompilerParams(dimension_semantics=sem, vmem_limit_bytes=VMEM_LIMIT)


def _resident(shape):
    nd = len(shape)
    return pl.BlockSpec(shape, lambda *_: (0,) * nd, pipeline_mode=pl.Buffered(1))


def _layer_resident(shape, layer, block=0):
    rest = (0,) * (len(shape) - 1)
    return pl.BlockSpec((None,) + shape, lambda *_: (layer, block) + rest, pipeline_mode=pl.Buffered(1))


def _rms(x, g):
    return x * lax.rsqrt(jnp.mean(x * x, axis=-1, keepdims=True) + EPS) * g


def _silu(x):
    return x * jax.nn.sigmoid(x)


def _dot(a, b):
    return jnp.dot(a, b, preferred_element_type=F32)


def _dot_wide(a, b):
    a_hi, a_lo = _split_bf16(a, 2)
    b_hi, b_lo = _split_bf16(b, 2)
    return _dot(a_hi, b_hi) + (_dot(a_hi, b_lo) + _dot(a_lo, b_hi))


def _dot_nt(a, b):
    return lax.dot_general(a, b, (((1,), (1,)), ((), ())), preferred_element_type=F32)


def _dot_tn(a, b):
    return lax.dot_general(a, b, (((0,), (0,)), ((), ())), preferred_element_type=F32)


def _split_bf16(x, parts):
    out = []
    for _ in range(parts):
        p = x.astype(BF16)
        out.append(p)
        x = x - p.astype(F32)
    return out


def _dot_sel_rhs(x, sel, parts):
    acc = None
    for p in _split_bf16(x, parts):
        t = _dot(p, sel)
        acc = t if acc is None else acc + t
    return acc


def _dot_sel_lhs(sel, x, parts):
    acc = None
    for p in _split_bf16(x, parts):
        t = _dot(sel, p)
        acc = t if acc is None else acc + t
    return acc


def _mod_spec(tm, rows_per_mod, first_row):
    return pl.BlockSpec((1, 1, 6 * D_MODEL), lambda i, *_: (first_row + (i * tm) // rows_per_mod, 0, 0))


def _ada_kernel(c_ref, w_ref, b_ref, o_ref):
    c = c_ref[...]
    o_ref[0] = _dot(_silu(c).astype(BF16), w_ref[0].astype(BF16)) + b_ref[0]


def _ada(cvecs, w_ada, b_ada):
    tn = 1024
    rows = cvecs.shape[0]
    return pl.pallas_call(
        _ada_kernel,
        grid=(N_LAYERS, 6 * D_MODEL // tn),
        in_specs=[pl.BlockSpec((rows, D_MODEL), lambda l, j: (0, 0)),
                  pl.BlockSpec((1, D_MODEL, tn), lambda l, j: (l, 0, j)),
                  pl.BlockSpec((1, 1, tn), lambda l, j: (l, 0, j))],
        out_specs=pl.BlockSpec((1, rows, tn), lambda l, j: (l, 0, j)),
        out_shape=jax.ShapeDtypeStruct((N_LAYERS, rows, 6 * D_MODEL), F32),
        compiler_params=_cparams("parallel", "parallel"),
        name="ada_mod",
    )(cvecs, w_ada, b_ada.reshape(N_LAYERS, 1, 6 * D_MODEL))


def _rope(x, cos, sin_signed, first_half):
    outs = []
    for j in range(x.shape[1] // LANES):
        xj = x[:, j * LANES:(j + 1) * LANES]
        partner = jnp.where(first_half, pltpu.roll(xj, LANES - HEAD_DIM // 2, axis=1),
                            pltpu.roll(xj, HEAD_DIM // 2, axis=1))
        outs.append(xj * cos + partner * sin_signed)
    return jnp.concatenate(outs, axis=-1)


def _in_proj_kernel(*refs, rope):
    if rope:
        x_ref, mod_ref, g_ref, w_ref, cos_ref, sin_ref, o_ref, qv_ref, kt_ref = refs
    else:
        x_ref, mod_ref, g_ref, w_ref, o_ref, qv_ref, kt_ref = refs
    mod = mod_ref[0]
    sh = mod[:, 0:D_MODEL]
    sc = mod[:, D_MODEL:2 * D_MODEL]
    h = _rms(x_ref[...], g_ref[...]) * (1.0 + sc) + sh
    o_ref[...] = _dot(h.astype(BF16), w_ref[...])
    o_ref[:, C_Z:C_Z + SSD_WIDTH] = _silu(o_ref[:, C_Z:C_Z + SSD_WIDTH])
    q = o_ref[:, C_Q:C_Q + ATT_WIDTH]
    k = o_ref[:, C_K:C_K + KV_WIDTH]
    if rope:
        lane = lax.broadcasted_iota(jnp.int32, (x_ref.shape[0], LANES), 1)
        first_half = (lane % HEAD_DIM) < (HEAD_DIM // 2)
        q = _rope(q, cos_ref[...], sin_ref[...], first_half)
        k = _rope(k, cos_ref[...], sin_ref[...], first_half)
    qv_ref[:, 0:ATT_WIDTH] = (q * Q_SCALE).astype(BF16)
    qv_ref[:, ATT_WIDTH:] = o_ref[:, C_V:C_V + KV_WIDTH].astype(BF16)
    kt_ref[...] = k.T.astype(BF16)


def _in_proj(x, mod3, gamma, w, *, layer, tm, mod_spec, rope_tables=None, seq_len=None):
    n = x.shape[0]
    rope = rope_tables is not None
    tables, table_specs = (), []
    if rope:
        per_seq = seq_len // tm
        tables = tuple(rope_tables)
        table_specs = [pl.BlockSpec((tm, LANES), lambda i: (i % per_seq, 0))] * 2
    return pl.pallas_call(
        functools.partial(_in_proj_kernel, rope=rope),
        grid=(n // tm,),
        in_specs=[pl.BlockSpec((tm, D_MODEL), lambda i: (i, 0)),
                  mod_spec,
                  _resident((1, D_MODEL)),
                  _layer_resident((D_MODEL, D_INP), layer)] + table_specs,
        out_specs=[pl.BlockSpec((tm, D_INP), lambda i: (i, 0)), pl.BlockSpec((tm, QV_WIDTH), lambda i: (i, 0)),
                   pl.BlockSpec((KV_WIDTH, tm), lambda i: (0, i))],
        out_shape=[jax.ShapeDtypeStruct((n, D_INP), F32), jax.ShapeDtypeStruct((n, QV_WIDTH), BF16),
                   jax.ShapeDtypeStruct((KV_WIDTH, n), BF16)],
        compiler_params=_cparams("parallel"),
        name="in_proj",
    )(x, mod3, gamma, w, *tables)


def _gelu_tanh(x):
    return 0.5 * x * (1.0 + jnp.tanh(math.sqrt(2.0 / math.pi) * (x + 0.044715 * (x * x * x))))


def _out_proj_kernel(x_ref, mod_ref, g_ref, yssd_ref, oatt_ref, ys5_ref, u_ref, d_ref, wglu_ref, bglu_ref,
                     w12_ref, w3_ref, o_ref, mix_scr):
    mix_scr[:, 0:SSD_WIDTH] = yssd_ref[...]
    mix_scr[:, SSD_WIDTH:] = oatt_ref[...]
    mix = _dot(mix_scr[...], w12_ref[...])
    y5 = ys5_ref[...] + d_ref[...] * u_ref[...]
    g = _gelu_tanh(y5)
    s5 = g * jax.nn.sigmoid(_dot(g.astype(BF16), wglu_ref[...]) + bglu_ref[...])
    mix = mix + _dot(s5.astype(BF16), w3_ref[...])
    gate = mod_ref[0][:, 2 * D_MODEL:3 * D_MODEL]
    o_ref[...] = x_ref[...] + gate * _rms(mix, g_ref[...])


def _out_proj(x, mod3, gamma, y_ssd, o_att, y_s5, proj, s5_d, w_glu, b_glu, w_out, *, layer, tm, mod_spec):
    n = x.shape[0]
    row = lambda i: (i, 0)
    return pl.pallas_call(
        _out_proj_kernel,
        grid=(n // tm,),
        in_specs=[pl.BlockSpec((tm, D_MODEL), row),
                  mod_spec,
                  _resident((1, D_MODEL)),
                  pl.BlockSpec((tm, SSD_WIDTH), row),
                  pl.BlockSpec((tm, ATT_WIDTH), row),
                  pl.BlockSpec((tm, S5_WIDTH), row),
                  pl.BlockSpec((tm, S5_WIDTH), lambda i: (i, C_U // S5_WIDTH)),
                  _resident((1, S5_WIDTH)),
                  _layer_resident((S5_WIDTH, S5_WIDTH), layer),
                  _resident((1, S5_WIDTH)),
                  _layer_resident((SSD_WIDTH + ATT_WIDTH, D_MODEL), layer, 0),
                  _layer_resident((S5_WIDTH, D_MODEL), layer, (SSD_WIDTH + ATT_WIDTH) // S5_WIDTH)],
        out_specs=pl.BlockSpec((tm, D_MODEL), row),
        out_shape=jax.ShapeDtypeStruct((n, D_MODEL), F32),
        scratch_shapes=[pltpu.VMEM((tm, SSD_WIDTH + ATT_WIDTH), BF16)],
        compiler_params=_cparams("parallel"),
        name="out_proj",
    )(x, mod3, gamma, y_ssd, o_att, y_s5, proj, s5_d, w_glu, b_glu, w_out, w_out)


def _ffn_up_kernel(x_ref, mod_ref, gpre_ref, wg_ref, wu_ref, act_ref, h_scr):
    j = pl.program_id(1)

    def gated(h):
        return (_silu(_dot(h, wg_ref[...].astype(BF16))) * _dot(h, wu_ref[...].astype(BF16))).astype(BF16)

    @pl.when(j == 0)
    def _():
        mod = mod_ref[0]
        sh = mod[:, 3 * D_MODEL:4 * D_MODEL]
        sc = mod[:, 4 * D_MODEL:5 * D_MODEL]
        h = (_rms(x_ref[...], gpre_ref[...]) * (1.0 + sc) + sh).astype(BF16)
        h_scr[...] = h
        act_ref[...] = gated(h)

    @pl.when(j > 0)
    def _():
        act_ref[...] = gated(h_scr[...])


def _ffn_down_kernel(x_ref, mod_ref, gpost_ref, act_ref, wo_ref, o_ref):
    gate = mod_ref[0][:, 5 * D_MODEL:6 * D_MODEL]
    o_ref[...] = x_ref[...] + gate * _rms(_dot(act_ref[...], wo_ref[...]), gpost_ref[...])


def _ffn(x, mod3, g_pre, g_post, w_in, w_out, *, layer, tm_up, tf, tm_down, mod_spec_up, mod_spec_down):
    n = x.shape[0]
    nf = D_FF // tf
    act = pl.pallas_call(
        _ffn_up_kernel,
        grid=(n // tm_up, nf),
        in_specs=[pl.BlockSpec((tm_up, D_MODEL), lambda i, j: (i, 0)),
                  mod_spec_up,
                  _resident((1, D_MODEL)),
                  pl.BlockSpec((None, D_MODEL, tf), lambda i, j: (layer, 0, j)),
                  pl.BlockSpec((None, D_MODEL, tf), lambda i, j: (layer, 0, j + nf))],
        out_specs=pl.BlockSpec((tm_up, tf), lambda i, j: (i, j)),
        out_shape=jax.ShapeDtypeStruct((n, D_FF), BF16),
        scratch_shapes=[pltpu.VMEM((tm_up, D_MODEL), BF16)],
        compiler_params=_cparams("parallel", "arbitrary"),
        name="ffn_up",
    )(x, mod3, g_pre, w_in, w_in)
    return pl.pallas_call(
        _ffn_down_kernel,
        grid=(n // tm_down,),
        in_specs=[pl.BlockSpec((tm_down, D_MODEL), lambda i: (i, 0)),
                  mod_spec_down,
                  _resident((1, D_MODEL)),
                  pl.BlockSpec((tm_down, D_FF), lambda i: (i, 0)),
                  _layer_resident((D_FF, D_MODEL), layer)],
        out_specs=pl.BlockSpec((tm_down, D_MODEL), lambda i: (i, 0)),
        out_shape=jax.ShapeDtypeStruct((n, D_MODEL), F32),
        compiler_params=_cparams("parallel"),
        name="ffn_down",
    )(x, mod3, g_post, act, w_out)


def _attend(queries, sink_ref):
    rows = queries[0][0].shape[0]
    srow = lax.broadcasted_iota(jnp.int32, (ATT_GROUP * rows, 1), 0)
    outs = [[None] * ATT_HEADS for _ in queries]
    units = [(i, kv) for i in range(len(queries)) for kv in range(ATT_KV)]

    def masked_scores(unit):
        (q, parts), kv = queries[unit[0]], unit[1]
        heads = range(kv * ATT_GROUP, (kv + 1) * ATT_GROUP)
        q3 = jnp.concatenate([q[:, h * HEAD_DIM:(h + 1) * HEAD_DIM] for h in heads], axis=0)
        scores = []
        for kt, _, mask in parts:
            s = _dot(q3, kt[kv * HEAD_DIM:(kv + 1) * HEAD_DIM, :])
            scores.append(s if mask is None else jnp.where(mask, s, NEG))
        return scores

    def softmax_numerator(unit, scores):
        kv = unit[1]
        heads = range(kv * ATT_GROUP, (kv + 1) * ATT_GROUP)
        sink = jnp.full((ATT_GROUP * rows, 1), sink_ref[heads[-1]] * LOG2E, F32)
        for i in range(ATT_GROUP - 2, -1, -1):
            sink = jnp.where(srow < (i + 1) * rows, sink_ref[heads[i]] * LOG2E, sink)
        blocks = [s[:, j:j + LANES] for s in scores for j in range(0, s.shape[1], LANES)]
        m = jnp.maximum(sink, jnp.max(functools.reduce(jnp.maximum, blocks), axis=-1, keepdims=True))
        p = jnp.concatenate([jnp.exp2(s - m).astype(BF16) for s in scores], axis=1)
        return p, jnp.exp2(sink - m)

    def weighted_values(unit, p, sink_term):
        (_, parts), kv = queries[unit[0]], unit[1]
        lo = kv * HEAD_DIM
        vlo = (lo // LANES) * LANES
        v_first = lo == vlo
        v128 = jnp.concatenate([v[:, vlo:vlo + LANES] for _, v, _ in parts], axis=0)
        lane = lax.broadcasted_iota(jnp.int32, v128.shape, 1)
        keep = (lane < HEAD_DIM) if v_first else (lane >= HEAD_DIM)
        acc = _dot(p, jnp.where(keep, v128, jnp.ones_like(v128)))
        ocol, dcol = (0, HEAD_DIM) if v_first else (HEAD_DIM, 0)
        o3 = acc[:, ocol:ocol + HEAD_DIM] / (acc[:, dcol:dcol + 1] + sink_term)
        for i in range(ATT_GROUP):
            outs[unit[0]][kv * ATT_GROUP + i] = o3[i * rows:(i + 1) * rows]

    scores = {0: masked_scores(units[0])}
    probs = {}
    for step in range(len(units) + 2):
        if step + 1 < len(units):
            scores[step + 1] = masked_scores(units[step + 1])
        if step - 1 in probs:
            weighted_values(units[step - 1], *probs.pop(step - 1))
        if step in scores:
            probs[step] = softmax_numerator(units[step], scores.pop(step))
    return [jnp.concatenate(o, axis=-1) for o in outs]


def _ctx_attn_kernel(sink_ref, q_ref, k_ref, v_ref, o_ref):
    blk = ATT_BLOCK
    parts = [(k_ref[...], v_ref[...], None)]
    queries = [(q_ref[r:r + blk, :], parts) for r in range(0, q_ref.shape[0], blk)]
    for r, o in zip(range(0, q_ref.shape[0], blk), _attend(queries, sink_ref)):
        o_ref[r:r + blk, :] = o.astype(o_ref.dtype)


def _ctx_attention(qv, kt, sink, *, n_seq, seq_len):
    vcol = ATT_WIDTH // KV_WIDTH
    blk = ATT_BLOCK
    nb = seq_len // blk
    return pl.pallas_call(
        _ctx_attn_kernel,
        grid=(n_seq, nb),
        in_specs=[pl.BlockSpec(memory_space=pltpu.SMEM),
                  pl.BlockSpec((blk, ATT_WIDTH), lambda b, n: (b * nb + n, 0)),
                  pl.BlockSpec((KV_WIDTH, seq_len), lambda b, n: (0, b)),
                  pl.BlockSpec((seq_len, KV_WIDTH), lambda b, n: (b, vcol))],
        out_specs=pl.BlockSpec((blk, ATT_WIDTH), lambda b, n: (b * nb + n, 0)),
        out_shape=jax.ShapeDtypeStruct((n_seq * seq_len, ATT_WIDTH), BF16),
        compiler_params=_cparams("parallel", "parallel"),
        name="ctx_attention",
    )(sink, qv, kt, qv)


def _lat_attn_kernel(sink_ref, q_ref, *refs, nq):
    kt_refs, v_refs = refs[0:nq + 2], refs[nq + 2:2 * nq + 4]
    ck_ref, cv_ref, o_ref = refs[2 * nq + 4:]
    step = pl.program_id(1)
    blk = ATT_BLOCK
    row = lax.broadcasted_iota(jnp.int32, (ATT_GROUP * blk, blk), 0) % blk
    col = lax.broadcasted_iota(jnp.int32, (ATT_GROUP * blk, blk), 1)
    before = [col >= row] * nq
    after = [col <= row] * nq
    before[0] = jnp.logical_and(before[0], step > 0)
    after[-1] = jnp.logical_and(after[-1], step < pl.num_programs(1) - 1)
    ctx = (ck_ref[0], cv_ref[0], None)
    blocks = [(kt[...], v[...]) for kt, v in zip(kt_refs, v_refs)]
    queries = [(q_ref[i * blk:(i + 1) * blk, :],
                [ctx, blocks[i] + (before[i],), blocks[i + 1] + (None,), blocks[i + 2] + (after[i],)])
               for i in range(nq)]
    for i, o in enumerate(_attend(queries, sink_ref)):
        o_ref[i * blk:(i + 1) * blk, :] = o.astype(o_ref.dtype)


def _lat_attention(qv, kt, sink, ckt, cv, *, n_seq, seq_len):
    blk = ATT_BLOCK
    nb = seq_len // blk
    nq = max(c for c in range(1, LAT_Q_BLOCKS + 1) if nb % c == 0)
    steps = nb // nq

    def block(offset):
        return lambda b, n: b * nb + jnp.clip(nq * n + offset, 0, nb - 1)

    vcol = ATT_WIDTH // KV_WIDTH
    offsets = range(-1, nq + 1)
    kt_specs = [pl.BlockSpec((KV_WIDTH, blk), lambda b, n, f=block(o): (0, f(b, n))) for o in offsets]
    v_specs = [pl.BlockSpec((blk, KV_WIDTH), lambda b, n, f=block(o): (f(b, n), vcol)) for o in offsets]
    return pl.pallas_call(
        functools.partial(_lat_attn_kernel, nq=nq),
        grid=(n_seq, steps),
        in_specs=[pl.BlockSpec(memory_space=pltpu.SMEM),
                  pl.BlockSpec((nq * blk, ATT_WIDTH), lambda b, n: (b * steps + n, 0))] + kt_specs + v_specs
                 + [pl.BlockSpec((1,) + ckt.shape[1:], lambda b, n: (b, 0, 0)),
                    pl.BlockSpec((1,) + cv.shape[1:], lambda b, n: (b, 0, 0))],
        out_specs=pl.BlockSpec((nq * blk, ATT_WIDTH), lambda b, n: (b * steps + n, 0)),
        out_shape=jax.ShapeDtypeStruct((n_seq * seq_len, ATT_WIDTH), BF16),
        compiler_params=_cparams("parallel", "parallel"),
        name="lat_attention",
    )(sink, qv, *([kt] * (nq + 2)), *([qv] * (nq + 2)), ckt, cv)


def _rope_tables(seq_len):
    rows = seq_len // GRID_W
    row = np.repeat(np.arange(rows, dtype=np.float32), GRID_W)
    col = np.tile(np.arange(GRID_W, dtype=np.float32), rows)
    inv = np.float32(ROPE_BASE) ** (-np.arange(ROPE_PER_AXIS, dtype=np.float32) / np.float32(ROPE_PER_AXIS))
    ang = np.concatenate([row[:, None] * inv, col[:, None] * inv], axis=-1)
    cos, sin = np.cos(ang), np.sin(ang)
    cos128 = np.tile(cos, (1, 4))
    sin128 = np.tile(np.concatenate([-sin, sin], axis=-1), (1, 2))
    return jnp.asarray(cos128, F32), jnp.asarray(sin128, F32)


def _conv_silu(scr, prev_ref, cur_ref, next_ref, w_ref, b_ref, at_start, at_end):
    n = cur_ref.shape[0]
    h = SSD_HALO
    scr[0:h, :] = jnp.where(at_start, 0.0, prev_ref[...])
    scr[h:h + n, :] = cur_ref[...]
    scr[h + n:h + n + h, :] = jnp.where(at_end, 0.0, next_ref[...])
    return _silu(scr[h - 1:h - 1 + n, :] * w_ref[0:1, :] + scr[h:h + n, :] * w_ref[1:2, :]
                 + scr[h + 1:h + 1 + n, :] * w_ref[2:3, :] + b_ref[...])


def _softplus(x):
    return jnp.maximum(x, 0.0) + jnp.log1p(jnp.exp(-jnp.abs(x)))


def _ssd_bwd_kernel(xsp_ref, xsc_ref, xsn_ref, bcp_ref, bcc_ref, bcn_ref, dt_ref, h0_ref,
                    cwx_ref, cbx_ref, cwb_ref, cbb_ref, dtb_ref, alog_ref, tri_ref, exp_ref,
                    hstart_ref, hfin_ref, xbc_ref, xs_scr, bc_scr, h_scr):
    i = pl.program_id(1)
    q = SSD_CHUNK
    half = SSD_WIDTH // SSD_GROUPS
    sel_b = exp_ref[1]
    neg_a = -LOG2E * jnp.exp(alog_ref[...])
    at_start = i == pl.num_programs(1) - 1
    at_end = i == 0
    xbc_ref[:, 0:SSD_WIDTH] = _conv_silu(xs_scr, xsp_ref, xsc_ref, xsn_ref, cwx_ref, cbx_ref, at_start, at_end)
    xbc_ref[:, SSD_WIDTH:] = _conv_silu(bc_scr, bcp_ref, bcc_ref, bcn_ref, cwb_ref, cbb_ref, at_start, at_end)

    @pl.when(i == 0)
    def _():
        h_scr[...] = h0_ref[0]

    hb = h_scr[...]
    for j in reversed(range(xsc_ref.shape[0] // q)):
        rows = slice(j * q, (j + 1) * q)
        hstart_ref[0, j] = hb
        bm = xbc_ref[rows, SSD_WIDTH:SSD_WIDTH + SSD_GROUPS * SSD_STATE].astype(BF16)
        dtv = _softplus(dt_ref[rows, :] + dtb_ref[...])
        rev = _dot_sel_lhs(tri_ref[1], dtv * neg_a, 3)
        dt_w = _dot_sel_rhs(dtv, sel_b, 2)
        to_end = _dot_sel_rhs(jnp.exp2(rev[0:1, :] - rev), sel_b, 2)
        decay = _dot_sel_rhs(jnp.exp2(rev[0:SSD_HALO, :]), sel_b, 2)[0:1, :]
        xw = (xbc_ref[rows, 0:SSD_WIDTH] * dt_w * to_end).astype(BF16)
        upd = jnp.concatenate(
            [_dot_tn(bm[:, g * SSD_STATE:(g + 1) * SSD_STATE], xw[:, g * half:(g + 1) * half])
             for g in range(SSD_GROUPS)], axis=-1)
        hb = hb * decay + upd
    h_scr[...] = hb

    @pl.when(i == pl.num_programs(1) - 1)
    def _():
        hfin_ref[0] = hb.T


def _ssd_intra(xs, bc, dt_raw, dtb_ref, alog_ref, tri_ref, exp_ref, dvec_ref):
    q = SSD_CHUNK
    nh = SSD_HEADS
    gs = SSD_GROUPS * SSD_STATE
    bm = bc[:, 0:gs].astype(BF16)
    cm = bc[:, gs:2 * gs].astype(BF16)
    xs_bf = xs.astype(BF16)

    dtv = _softplus(dt_raw + dtb_ref[...])
    dta = dtv * (-LOG2E * jnp.exp(alog_ref[...]))
    lower, upper = tri_ref[0], tri_ref[1]
    cum = _dot_sel_lhs(lower, dta, 3)
    rev = _dot_sel_lhs(upper, dta, 3)
    dt_t = dtv.T
    dta_t = dta.T
    cum_t = _dot_sel_rhs(dta_t, upper, 3)
    rev_t = _dot_sel_rhs(dta_t, lower, 3)

    sel_f, sel_b = exp_ref[0], exp_ref[1]
    dtf_w = _dot_sel_rhs(dtv, sel_f, 2)
    ecum_w = _dot_sel_rhs(jnp.exp2(cum), sel_f, 2)
    erev_w = _dot_sel_rhs(jnp.exp2(rev), sel_b, 2)
    toend_w = _dot_sel_rhs(jnp.exp2(cum[q - 1:q, :] - cum), sel_f, 2)

    row = lax.broadcasted_iota(jnp.int32, (q, q), 0)
    col = lax.broadcasted_iota(jnp.int32, (q, q), 1)
    below = row > col
    log_dt_t = jnp.log2(dt_t)
    src_f = log_dt_t - cum_t
    src_b = log_dt_t - rev_t
    gmat = [_dot_nt(cm[:, g * SSD_STATE:(g + 1) * SSD_STATE], bm[:, g * SSD_STATE:(g + 1) * SSD_STATE])
            for g in range(SSD_GROUPS)]
    per_group = nh // SSD_GROUPS
    half = SSD_WIDTH // SSD_GROUPS
    ys = []
    for h in range(nh):
        e = jnp.where(below, cum[:, h:h + 1] + src_f[h:h + 1, :],
                      rev[:, nh + h:nh + h + 1] + src_b[nh + h:nh + h + 1, :])
        m = (gmat[h // per_group] * jnp.exp2(e)).astype(BF16)
        ys.append(_dot(m, xs_bf[:, h * SSD_HEAD_DIM:(h + 1) * SSD_HEAD_DIM]))
    cb = cm.astype(F32) * bm.astype(F32)
    g_diag = jnp.concatenate(
        [jnp.broadcast_to(jnp.sum(cb[:, g * SSD_STATE:(g + 1) * SSD_STATE], axis=-1, keepdims=True), (q, half))
         for g in range(SSD_GROUPS)], axis=-1)
    y = jnp.concatenate(ys, axis=-1) + (g_diag * dtf_w + dvec_ref[...]) * xs
    xw = (xs * dtf_w * toend_w).astype(BF16)
    upd = jnp.concatenate(
        [_dot_tn(bm[:, g * SSD_STATE:(g + 1) * SSD_STATE], xw[:, g * half:(g + 1) * half])
         for g in range(SSD_GROUPS)], axis=-1)
    return y, cm, ecum_w, erev_w, upd


def _ssd_carry(hf, hb_start, gz, intra, nw_ref):
    y, cm, ecum_w, erev_w, upd = intra
    half = SSD_WIDTH // SSD_GROUPS
    hf_bf = hf.astype(BF16)
    hb_bf = hb_start.astype(BF16)
    off_f = jnp.concatenate([_dot(cm[:, g * SSD_STATE:(g + 1) * SSD_STATE], hf_bf[:, g * half:(g + 1) * half])
                             for g in range(SSD_GROUPS)], axis=-1)
    off_b = jnp.concatenate([_dot(cm[:, g * SSD_STATE:(g + 1) * SSD_STATE], hb_bf[:, g * half:(g + 1) * half])
                             for g in range(SSD_GROUPS)], axis=-1)
    y = y + off_f * ecum_w + off_b * erev_w
    y = _rms(y * gz, nw_ref[...])
    return y, hf * ecum_w[SSD_CHUNK - 1:SSD_CHUNK, :] + upd


def _ssd_fwd_kernel(gz_ref, xs_ref, bc_ref, dt_ref, hb_ref, h0_ref,
                    dtb_ref, alog_ref, tri_ref, exp_ref, dvec_ref, nw_ref, y_ref, hfin_ref, h_scr):
    c = pl.program_id(1)
    q = SSD_CHUNK
    cps = xs_ref.shape[0] // q
    rows = [slice(j * q, (j + 1) * q) for j in range(cps)]

    def intra(j):
        return _ssd_intra(xs_ref[rows[j], :], bc_ref[rows[j], :], dt_ref[rows[j], :],
                          dtb_ref, alog_ref, tri_ref, exp_ref, dvec_ref)

    @pl.when(c == 0)
    def _():
        h_scr[...] = h0_ref[0]

    hf = h_scr[...]
    ahead = intra(0)
    for j in range(cps):
        cur = ahead
        if j + 1 < cps:
            ahead = intra(j + 1)
        y, hf = _ssd_carry(hf, hb_ref[0, j], gz_ref[rows[j], :], cur, nw_ref)
        y_ref[rows[j], :] = y.astype(y_ref.dtype)
    h_scr[...] = hf

    @pl.when(c == pl.num_programs(1) - 1)
    def _():
        hfin_ref[0] = hf.T


def _ssd_mixer(proj, h0_f, h0_b, consts, *, n_seq, seq_len):
    q = SSD_CHUNK
    nc = seq_len // q
    cps = max(c for c in range(1, SSD_CHUNKS_PER_STEP + 1) if nc % c == 0)
    ns = nc // cps
    rows = cps * q
    (cwx, cbx, cwb, cbb, dt_bias, a_log, tri, expand, dvec, nw) = consts
    per8 = rows // SSD_HALO
    last8 = proj.shape[0] // SSD_HALO - 1

    def raw_specs(width, col):
        step = lambda b, i: b * ns + ns - 1 - i
        return [pl.BlockSpec((SSD_HALO, width), lambda b, i: (jnp.maximum(step(b, i) * per8 - 1, 0), col)),
                pl.BlockSpec((rows, width), lambda b, i: (step(b, i), col)),
                pl.BlockSpec((SSD_HALO, width), lambda b, i: (jnp.minimum((step(b, i) + 1) * per8, last8), col))]

    def specs(chunk_of):
        def cur(b, i):
            return b * ns + chunk_of(i)

        return [pl.BlockSpec((rows, SSD_WIDTH), lambda b, i: (cur(b, i), 0)),
                pl.BlockSpec((rows, SSD_BC), lambda b, i: (cur(b, i), SSD_WIDTH // SSD_BC)),
                pl.BlockSpec((rows, DT_PAD), lambda b, i: (cur(b, i), C_DT // DT_PAD))], cur

    state_spec = pl.BlockSpec((1, SSD_STATE, SSD_WIDTH), lambda b, i: (b, 0, 0))
    final_spec = pl.BlockSpec((1, SSD_WIDTH, SSD_STATE), lambda b, i: (b, 0, 0))
    const_specs = [_resident(dt_bias.shape), _resident(a_log.shape), _resident(tri.shape),
                   _resident(expand.shape)]
    scratch = [pltpu.VMEM((SSD_STATE, SSD_WIDTH), F32)]

    data_specs, rcur = specs(lambda i: ns - 1 - i)
    hb_start, hb_fin, xbc = pl.pallas_call(
        _ssd_bwd_kernel,
        grid=(n_seq, ns),
        in_specs=(raw_specs(SSD_WIDTH, C_XS // SSD_WIDTH) + raw_specs(SSD_BC, C_BC // SSD_BC)
                  + [data_specs[2], state_spec]
                  + [_resident(cwx.shape), _resident(cbx.shape), _resident(cwb.shape), _resident(cbb.shape)]
                  + const_specs),
        out_specs=[pl.BlockSpec((1, cps, SSD_STATE, SSD_WIDTH), lambda b, i: (b, ns - 1 - i, 0, 0)), final_spec,
                   pl.BlockSpec((rows, SSD_WIDTH + SSD_BC), lambda b, i: (rcur(b, i), 0))],
        out_shape=[jax.ShapeDtypeStruct((n_seq, nc, SSD_STATE, SSD_WIDTH), F32),
                   jax.ShapeDtypeStruct((n_seq, SSD_WIDTH, SSD_STATE), F32),
                   jax.ShapeDtypeStruct((n_seq * seq_len, SSD_WIDTH + SSD_BC), F32)],
        scratch_shapes=[pltpu.VMEM((rows + 2 * SSD_HALO, SSD_WIDTH), F32),
                        pltpu.VMEM((rows + 2 * SSD_HALO, SSD_BC), F32)] + scratch,
        compiler_params=_cparams("parallel", "arbitrary"),
        name="ssd_backward_states",
    )(proj, proj, proj, proj, proj, proj, proj, h0_b, cwx, cbx, cwb, cbb, dt_bias, a_log, tri, expand)

    data_specs, cur = specs(lambda i: i)
    y, hf_fin = pl.pallas_call(
        _ssd_fwd_kernel,
        grid=(n_seq, ns),
        in_specs=([pl.BlockSpec((rows, SSD_WIDTH), lambda b, i: (cur(b, i), C_Z // SSD_WIDTH))] + data_specs
                  + [pl.BlockSpec((1, cps, SSD_STATE, SSD_WIDTH), lambda b, i: (b, i, 0, 0)), state_spec]
                  + const_specs + [_resident(dvec.shape), _resident(nw.shape)]),
        out_specs=[pl.BlockSpec((rows, SSD_WIDTH), lambda b, i: (b * ns + i, 0)), final_spec],
        out_shape=[jax.ShapeDtypeStruct((n_seq * seq_len, SSD_WIDTH), BF16),
                   jax.ShapeDtypeStruct((n_seq, SSD_WIDTH, SSD_STATE), F32)],
        scratch_shapes=scratch,
        compiler_params=_cparams("parallel", "arbitrary"),
        name="ssd_forward",
    )(proj, xbc, xbc, proj, hb_start, h0_f, dt_bias, a_log, tri, expand, dvec, nw)
    return y, hf_fin, hb_fin


def _ssd_consts(l, ssd_conv_w, ssd_conv_b, ssd_dt_bias, ssd_A_log, ssd_D, ssd_norm):
    q = SSD_CHUNK
    cw, cb = ssd_conv_w[l], ssd_conv_b[l]
    pad = DT_PAD - 2 * SSD_HEADS
    dt_bias = jnp.pad(ssd_dt_bias[l].reshape(1, -1), ((0, 0), (0, pad)))
    a_log = jnp.pad(ssd_A_log[l].reshape(1, -1), ((0, 0), (0, pad)))
    r = np.arange(q)
    tri = jnp.asarray(np.stack([r[None, :] <= r[:, None], r[None, :] >= r[:, None]]), BF16)
    head_of_col = np.arange(SSD_WIDTH) // SSD_HEAD_DIM
    j = np.arange(DT_PAD)
    expand = jnp.asarray(np.stack([j[:, None] == head_of_col[None, :],
                                   j[:, None] == head_of_col[None, :] + SSD_HEADS]), BF16)
    dvec = jnp.repeat(ssd_D[l], SSD_HEAD_DIM).reshape(1, SSD_WIDTH)
    return (cw[:, :SSD_WIDTH], cb[:SSD_WIDTH].reshape(1, -1), cw[:, SSD_WIDTH:], cb[SSD_WIDTH:].reshape(1, -1),
            dt_bias, a_log, tri, expand, dvec, ssd_norm[l].reshape(1, -1))


def _ssd_state_in(s):
    b = s.shape[0]
    return jnp.transpose(s, (0, 3, 1, 2)).reshape(b, SSD_STATE, SSD_WIDTH)


def _s5_table_kernel(pw_ref, ct_ref, bb_ref, bbt_ref, sel_ref, tile_ref, rhs_ref, woff_ref):
    t = S5_T
    c = S5_CH
    tile = tile_ref[...]
    spread = lambda x, sel: _dot_sel_rhs(x, sel, 2)
    krows, offs, cols = [], [], []
    for d in range(2):
        pw_re, pw_im = pw_ref[0, 2 * d], pw_ref[0, 2 * d + 1]
        c_re = spread(ct_ref[0, :, (2 * d) * c:(2 * d + 1) * c], tile)
        c_im = spread(ct_ref[0, :, (2 * d + 1) * c:(2 * d + 2) * c], tile)
        b_re = spread(bb_ref[0, :, (2 * d) * c:(2 * d + 1) * c], tile)
        b_im = spread(bb_ref[0, :, (2 * d + 1) * c:(2 * d + 2) * c], tile)
        bt_re = bbt_ref[0, (2 * d) * c:(2 * d + 1) * c, :]
        bt_im = bbt_ref[0, (2 * d + 1) * c:(2 * d + 2) * c, :]
        up = (spread(pw_re, sel_ref[0]), spread(pw_im, sel_ref[0]))
        down = (spread(pw_re, sel_ref[1]), spread(pw_im, sel_ref[1]))
        (q_re, q_im), state = ((up, down), (down, up))[d]
        g_re = q_re * c_re - q_im * c_im
        g_im = q_re * c_im + q_im * c_re
        krows.append(_dot_wide(bt_re, g_re) - _dot_wide(bt_im, g_im))
        lb_re, lb_im = pw_re[:, 1:2], pw_im[:, 1:2]
        r_re = q_re * lb_re - q_im * lb_im
        r_im = q_re * lb_im + q_im * lb_re
        offs.append((r_re * c_re - r_im * c_im, -(r_re * c_im + r_im * c_re)))
        s_re, s_im = state
        cols.append(((s_re * b_re - s_im * b_im).T, (s_re * b_im + s_im * b_re).T))
    k_f, k_b = krows
    for s in range(t):
        r = t - 1 - s
        f = k_f if s == 0 else jnp.concatenate([jnp.zeros((c, c * s), F32), k_f[:, :S5_TW - c * s]], axis=1)
        b = k_b if r == 0 else jnp.concatenate([k_b[:, c * r:], jnp.zeros((c, c * r), F32)], axis=1)
        rhs_ref[0, s * c:(s + 1) * c, 0:S5_TW] = (f + b).astype(BF16)
    rhs_ref[0, :, S5_TW:] = jnp.concatenate([cols[0][0], cols[1][0], cols[0][1], cols[1][1]], axis=1).astype(BF16)
    woff_ref[0] = jnp.concatenate([offs[0][0], offs[1][0], offs[0][1], offs[1][1]], axis=0).astype(BF16)


def _s5_tables(s5_A_re, s5_A_im, s5_log_dt, s5_B_re, s5_B_im, s5_C_re, s5_C_im):
    t = S5_T
    n = N_LAYERS * S5_GROUPS
    step = jnp.exp(s5_log_dt)[..., None]
    k = jnp.arange(LANES, dtype=F32)
    keep = k <= t
    kk = jnp.where(keep, k, 0.0)
    mag = jnp.exp(kk * (s5_A_re * step)[..., None])
    ang = kk * (s5_A_im * step)[..., None]
    pw_re = jnp.where(keep, mag * jnp.cos(ang), 0.0)
    pw_im = jnp.where(keep, mag * jnp.sin(ang), 0.0)
    lb_re, lb_im = pw_re[..., 1], pw_im[..., 1]
    den = s5_A_re * s5_A_re + s5_A_im * s5_A_im
    r_re = ((lb_re - 1.0) * s5_A_re + lb_im * s5_A_im) / den
    r_im = (lb_im * s5_A_re - (lb_re - 1.0) * s5_A_im) / den
    b_re, b_im = s5_B_re[:, None], s5_B_im[:, None]
    bb_re = r_re[..., None] * b_re - r_im[..., None] * b_im
    bb_im = r_re[..., None] * b_im + r_im[..., None] * b_re
    c_re = jnp.swapaxes(s5_C_re, -1, -2)
    c_im = jnp.swapaxes(s5_C_im, -1, -2)

    def pack(re, im, axis):
        return jnp.concatenate([re[:, 0], im[:, 0], re[:, 1], im[:, 1]], axis=axis)

    pw = jnp.stack([pw_re[:, 0], pw_im[:, 0], pw_re[:, 1], pw_im[:, 1]], axis=2).reshape(n, 4, S5_P, LANES)
    ct = pack(c_re, c_im, -1).reshape(n, S5_P, 4 * S5_CH)
    bb = pack(bb_re, bb_im, -1)
    bbt = jnp.swapaxes(bb, -1, -2).reshape(n, 4 * S5_CH, S5_P)
    bb = bb.reshape(n, S5_P, 4 * S5_CH)
    lam_t = jnp.concatenate([pw_re[:, 0, :, :, t], pw_re[:, 1, :, :, t], pw_im[:, 0, :, :, t],
                             pw_im[:, 1, :, :, t]], axis=-1).reshape(n, 1, S5_ST)
    jt = np.arange(S5_TW) // S5_CH
    lane_k = np.arange(LANES)[:, None]
    sel = jnp.asarray(np.stack([lane_k == f[None, :] for f in (jt, t - 1 - jt)]), BF16)
    tile = jnp.asarray(np.arange(S5_CH)[:, None] == (np.arange(S5_TW) % S5_CH)[None, :], BF16)
    grp = lambda shape: pl.BlockSpec((1,) + shape, lambda i: (i,) + (0,) * len(shape))
    rhs, woff = pl.pallas_call(
        _s5_table_kernel,
        grid=(n,),
        in_specs=[grp((4, S5_P, LANES)), grp((S5_P, 4 * S5_CH)), grp((S5_P, 4 * S5_CH)), grp((4 * S5_CH, S5_P)),
                  _resident(sel.shape), _resident(tile.shape)],
        out_specs=[grp((S5_TW, S5_TW + S5_ST)), grp((S5_ST, S5_TW))],
        out_shape=[jax.ShapeDtypeStruct((n, S5_TW, S5_TW + S5_ST), BF16),
                   jax.ShapeDtypeStruct((n, S5_ST, S5_TW), BF16)],
        compiler_params=_cparams("parallel"),
        name="s5_tables",
    )(pw, ct, bb, bbt, sel, tile)
    return rhs, woff, lam_t


def _unit_transpose(vs, masks, axis, unit):
    size = vs[0].shape[axis]
    cur = list(vs)
    for d in (1, 2, 4):
        nxt = list(cur)
        for a in range(S5_OCT):
            if a & d:
                continue
            b = a | d
            nxt[a] = jnp.where(masks[d], cur[a], pltpu.roll(cur[b], unit * d, axis=axis))
            nxt[b] = jnp.where(masks[d], pltpu.roll(cur[a], size - unit * d, axis=axis), cur[b])
        cur = nxt
    return cur


def _s5_kernel(u_ref, rhs_ref, woff_ref, lam_ref, h0_ref, y_ref, hfin_ref,
               dst_scr, y_scr, sre_scr, sim_scr, hfre_scr, hfim_scr, hbre_scr, hbim_scr, *, nc, rb):
    t = S5_T
    rows = dst_scr.shape[1]
    sb = rows // nc
    tiles = t // S5_OCT
    nr = rb // S5_OCT
    tile_shape = (nr, S5_OCT, LANES)
    seg = lax.broadcasted_iota(jnp.int32, tile_shape, 2) // S5_CH
    sub = lax.broadcasted_iota(jnp.int32, tile_shape, 1)
    lane_masks = {d: (seg & d) == 0 for d in (1, 2, 4)}
    sub_masks = {d: (sub & d) == 0 for d in (1, 2, 4)}

    def tile_rows(i, r, tq):
        return pl.ds((i * rb + r) * tiles + tq, nr, stride=S5_OCT * tiles)

    def gather(i, carry):
        r0 = pl.multiple_of(i * rb, rb)
        for tq in range(tiles):
            by_chunk = [u_ref[tile_rows(i, r, tq)] for r in range(S5_OCT)]
            by_time = _unit_transpose(by_chunk, sub_masks, 1, 1)
            for gl, v in enumerate(_unit_transpose(by_time, lane_masks, 2, S5_CH)):
                dst_scr[gl, pl.ds(r0, rb), tq * LANES:(tq + 1) * LANES] = v.reshape(rb, LANES).astype(BF16)
        return carry

    lax.fori_loop(0, rows // rb, gather, 0)

    for gl in range(S5_OCT):
        z = _dot(dst_scr[gl], rhs_ref[gl])
        y_scr[gl] = z[:, 0:S5_TW]
        sre_scr[gl] = z[:, S5_TW:S5_TW + 2 * S5_P]
        sim_scr[gl] = z[:, S5_TW + 2 * S5_P:]

    fwd_lane = lax.broadcasted_iota(jnp.int32, (sb, 2 * S5_P), 1) < S5_P
    a_re = [lam_ref[gl][:, 0:2 * S5_P] for gl in range(S5_OCT)]
    a_im = [lam_ref[gl][:, 2 * S5_P:] for gl in range(S5_OCT)]

    def step(i, carry):
        fi = pl.ds(i, sb, stride=nc)
        bj = pl.ds(nc - 1 - i, sb, stride=nc)
        out = []
        for gl in range(S5_OCT):
            h_re, h_im = carry[2 * gl], carry[2 * gl + 1]
            hfre_scr[gl, fi, :] = h_re
            hfim_scr[gl, fi, :] = h_im
            hbre_scr[gl, bj, :] = h_re
            hbim_scr[gl, bj, :] = h_im
            s_re = jnp.where(fwd_lane, sre_scr[gl, fi, :], sre_scr[gl, bj, :])
            s_im = jnp.where(fwd_lane, sim_scr[gl, fi, :], sim_scr[gl, bj, :])
            out.append(a_re[gl] * h_re - a_im[gl] * h_im + s_re)
            out.append(a_re[gl] * h_im + a_im[gl] * h_re + s_im)
        return tuple(out)

    init = []
    for gl in range(S5_OCT):
        h0 = h0_ref[gl, 0]
        init += [h0[:, 0:2 * S5_P], h0[:, 2 * S5_P:]]
    fin = lax.fori_loop(0, nc, step, tuple(init), unroll=2 if nc % 2 == 0 else 1)

    all_fwd = lax.broadcasted_iota(jnp.int32, (rows, 2 * S5_P), 1) < S5_P
    for gl in range(S5_OCT):
        hfin_ref[gl, 0] = jnp.concatenate([fin[2 * gl], fin[2 * gl + 1]], axis=-1)
        hin = jnp.concatenate([jnp.where(all_fwd, hfre_scr[gl], hbre_scr[gl]),
                               jnp.where(all_fwd, hfim_scr[gl], hbim_scr[gl])], axis=-1).astype(BF16)
        y_scr[gl] += _dot(hin, woff_ref[gl])

    def scatter(i, carry):
        r0 = pl.multiple_of(i * rb, rb)
        for tq in range(tiles):
            by_group = [y_scr[gl, pl.ds(r0, rb), tq * LANES:(tq + 1) * LANES].reshape(tile_shape)
                        for gl in range(S5_OCT)]
            by_time = _unit_transpose(by_group, lane_masks, 2, S5_CH)
            for r, v in enumerate(_unit_transpose(by_time, sub_masks, 1, 1)):
                y_ref[tile_rows(i, r, tq)] = v
        return carry

    lax.fori_loop(0, rows // rb, scatter, 0)


def _s5_mixer(proj, tables, layer, h0, *, n_seq, seq_len):
    rhs, woff, lam_t = tables
    t = S5_T
    nc = seq_len // t
    sb = max(1, min(n_seq, S5_MAX_ROWS // nc))
    assert n_seq % sb == 0
    nbs = n_seq // sb
    rows = sb * nc
    rb = min(rows, 32)
    noct = S5_GROUPS // S5_OCT
    ucol = C_U // LANES
    tab = lambda shape: pl.BlockSpec((S5_OCT,) + shape, lambda j, s: (layer * noct + j, 0, 0))
    state = pl.BlockSpec((S5_OCT, 1, sb, S5_ST), lambda j, s: (j, s, 0, 0))
    rows_scr = lambda dt: pltpu.VMEM((S5_OCT, rows, 2 * S5_P), dt)
    n_tok = n_seq * seq_len
    tok_block = (sb * seq_len // S5_OCT, S5_OCT, LANES)
    y, hfin = pl.pallas_call(
        functools.partial(_s5_kernel, nc=nc, rb=rb),
        grid=(noct, nbs),
        in_specs=[pl.BlockSpec(tok_block, lambda j, s: (s, 0, ucol + j)),
                  tab(rhs.shape[1:]), tab(woff.shape[1:]), tab(lam_t.shape[1:]), state],
        out_specs=[pl.BlockSpec(tok_block, lambda j, s: (s, 0, j)), state],
        out_shape=[jax.ShapeDtypeStruct((n_tok // S5_OCT, S5_OCT, S5_WIDTH), F32),
                   jax.ShapeDtypeStruct((S5_GROUPS, nbs, sb, S5_ST), F32)],
        scratch_shapes=[pltpu.VMEM((S5_OCT, rows, S5_TW), BF16), pltpu.VMEM((S5_OCT, rows, S5_TW), F32),
                        rows_scr(F32), rows_scr(F32), rows_scr(F32), rows_scr(F32), rows_scr(F32), rows_scr(F32)],
        compiler_params=_cparams("parallel", "parallel"),
        name="s5_mixer",
    )(proj.reshape(n_tok // S5_OCT, S5_OCT, D_INP), rhs, woff, lam_t, h0.reshape(S5_GROUPS, nbs, sb, S5_ST))
    return y.reshape(n_tok, S5_WIDTH), hfin.reshape(S5_GROUPS, n_seq, S5_ST)


def _s5_state_in(re, im):
    b = re.shape[0]
    x = jnp.stack([re, im], axis=1)
    return jnp.transpose(x, (3, 0, 1, 2, 4)).reshape(S5_GROUPS, b, S5_ST)


def _s5_state_out(h):
    n_seq = h.shape[1]
    x = h.reshape(S5_GROUPS, n_seq, 2, 2, S5_P)
    x = jnp.transpose(x, (2, 1, 3, 0, 4))
    return x[0], x[1]


def kernel(x_prompt, x_sample, c, cache_k, cache_v, state_ssd, state_s5_re, state_s5_im, c_ctx, w_ada, b_ada, norm_mix_pre, norm_mix_post, norm_ffn_pre, norm_ffn_post, w_in, w_out, ssd_conv_w, ssd_conv_b, ssd_dt_bias, ssd_A_log, ssd_D, ssd_norm, attn_sink, s5_A_re, s5_A_im, s5_log_dt, s5_B_re, s5_B_im, s5_C_re, s5_C_im, s5_D, s5_w_glu, s5_b_glu, w_ffn_in, w_ffn_out):
    nb_ctx, len_ctx, _ = x_prompt.shape
    nb_lat, len_lat, _ = x_sample.shape
    n_ctx = nb_ctx * len_ctx
    n_lat = nb_lat * len_lat
    tm = 512
    mod_rows = -(-(1 + nb_lat) // 8) * 8
    cvecs = jnp.concatenate([c_ctx[None, :], c, jnp.zeros((mod_rows - 1 - nb_lat, D_MODEL), F32)], axis=0)
    mods = _ada(cvecs, w_ada, b_ada)
    mod_geom = ((n_ctx, 0), (len_lat, 1))
    token_sets = tuple(dict(mod_spec=_mod_spec(tm, *g)) for g in mod_geom)
    tm_up = min(FFN_UP_ROWS, n_ctx, len_lat)
    rope_args = (dict(), dict(rope_tables=_rope_tables(len_lat), seq_len=len_lat))

    s5_tab = _s5_tables(s5_A_re, s5_A_im, s5_log_dt, s5_B_re, s5_B_im, s5_C_re, s5_C_im)
    w_in_b = w_in.astype(BF16)
    w_in_p = jnp.concatenate(
        [w_in_b[:, :, O_Q:O_K], w_in_b[:, :, O_Z:O_DT], w_in_b[:, :, O_K:O_END], w_in_b[:, :, O_DT:O_Q],
         jnp.zeros((N_LAYERS, D_MODEL, DT_PAD - 2 * SSD_HEADS), BF16)], axis=2)
    w_out_b = w_out.astype(BF16)
    w_glu_b = s5_w_glu.astype(BF16)
    w_ffn_out_b = w_ffn_out.astype(BF16)
    zeros_ssd = jnp.zeros((nb_ctx, SSD_STATE, SSD_WIDTH), F32)
    zeros_s5 = jnp.zeros((S5_GROUPS, nb_ctx, S5_ST), F32)

    xs = [x_prompt.reshape(n_ctx, D_MODEL), x_sample.reshape(n_lat, D_MODEL)]
    new_k, new_v, new_ssd, new_s5_re, new_s5_im = [], [], [], [], []
    for l in range(N_LAYERS):
        mod3 = mods[l].reshape(mod_rows, 1, 6 * D_MODEL)
        proj, qv, kt = zip(*[_in_proj(x, mod3, norm_mix_pre[l].reshape(1, -1), w_in_p, layer=l, tm=tm, **ts, **ra)
                             for x, ts, ra in zip(xs, token_sets, rope_args)])

        sink = attn_sink[l]
        ckt = jnp.swapaxes(cache_k[:, l].reshape(nb_lat, -1, KV_WIDTH), 1, 2).astype(BF16)
        cv = cache_v[:, l].reshape(nb_lat, -1, KV_WIDTH).astype(BF16)
        o_att = [_ctx_attention(qv[0], kt[0], sink, n_seq=nb_ctx, seq_len=len_ctx),
                 _lat_attention(qv[1], kt[1], sink, ckt, cv, n_seq=nb_lat, seq_len=len_lat)]

        consts = _ssd_consts(l, ssd_conv_w, ssd_conv_b, ssd_dt_bias, ssd_A_log, ssd_D, ssd_norm)
        y_ctx, hf_ctx, hb_ctx = _ssd_mixer(proj[0], zeros_ssd, zeros_ssd, consts, n_seq=nb_ctx, seq_len=len_ctx)
        y_lat, _, _ = _ssd_mixer(proj[1], _ssd_state_in(state_ssd[:, l, 0]), _ssd_state_in(state_ssd[:, l, 1]),
                                 consts, n_seq=nb_lat, seq_len=len_lat)
        y_ssd = [y_ctx, y_lat]

        s5_ctx, s5_fin = _s5_mixer(proj[0], s5_tab, l, zeros_s5, n_seq=nb_ctx, seq_len=len_ctx)
        s5_lat, _ = _s5_mixer(proj[1], s5_tab, l, _s5_state_in(state_s5_re[:, l], state_s5_im[:, l]),
                              n_seq=nb_lat, seq_len=len_lat)
        y_s5 = [s5_ctx, s5_lat]

        xs = [_out_proj(x, mod3, norm_mix_post[l].reshape(1, -1), y_ssd[i], o_att[i], y_s5[i], proj[i],
                        s5_D[l].reshape(1, -1), w_glu_b, s5_b_glu[l].reshape(1, -1), w_out_b,
                        layer=l, tm=tm, **ts)
              for i, (x, ts) in enumerate(zip(xs, token_sets))]
        xs = [_ffn(x, mod3, norm_ffn_pre[l].reshape(1, -1), norm_ffn_post[l].reshape(1, -1),
                   w_ffn_in, w_ffn_out_b, layer=l, tm_up=tm_up, tf=FFN_UP_COLS, tm_down=tm,
                   mod_spec_up=_mod_spec(tm_up, *g), mod_spec_down=_mod_spec(tm, *g))
              for x, g in zip(xs, mod_geom)]

        new_k.append(proj[0][:, C_K:C_K + KV_WIDTH].reshape(nb_ctx, len_ctx, ATT_KV, HEAD_DIM))
        new_v.append(proj[0][:, C_V:C_V + KV_WIDTH].reshape(nb_ctx, len_ctx, ATT_KV, HEAD_DIM))
        new_ssd += [hf_ctx, hb_ctx]
        re, im = _s5_state_out(s5_fin)
        new_s5_re.append(re)
        new_s5_im.append(im)

    y_prompt = xs[0].reshape(nb_ctx, len_ctx, D_MODEL)
    y_sample = xs[1].reshape(nb_lat, len_lat, D_MODEL)
    new_state_ssd = jnp.stack(new_ssd, axis=1).reshape(nb_ctx, N_LAYERS, 2, SSD_HEADS, SSD_HEAD_DIM, SSD_STATE)
    return (y_prompt, y_sample, jnp.stack(new_k, axis=1), jnp.stack(new_v, axis=1), new_state_ssd,
            jnp.stack(new_s5_re, axis=1), jnp.stack(new_s5_im, axis=1))
```

```python
import functools
import math

import jax
import jax.numpy as jnp
import numpy as np
from jax import lax
from jax.experimental import pallas as pl
from jax.experimental.pallas import tpu as pltpu

F32 = jnp.float32
BF16 = jnp.bfloat16

D_MODEL = 2048
N_LAYERS = 2
D_FF = 5632
GRID_W = 64
EPS = 1e-6
LANES = 128

SSD_HEADS = 12
SSD_HEAD_DIM = 64
SSD_WIDTH = SSD_HEADS * SSD_HEAD_DIM
SSD_GROUPS = 2
SSD_STATE = 64
SSD_BC = 2 * SSD_GROUPS * SSD_STATE
SSD_CHUNK = 128
SSD_CHUNKS_PER_STEP = 8
SSD_HALO = 8
ATT_HEADS = 12
ATT_KV = 4
ATT_GROUP = ATT_HEADS // ATT_KV
HEAD_DIM = 64
ATT_WIDTH = ATT_HEADS * HEAD_DIM
KV_WIDTH = ATT_KV * HEAD_DIM
ATT_BLOCK = 128
LAT_Q_BLOCKS = 4
QV_WIDTH = ATT_WIDTH + KV_WIDTH
LOG2E = math.log2(math.e)
Q_SCALE = HEAD_DIM ** -0.5 * LOG2E
ROPE_PER_AXIS = HEAD_DIM // 4
ROPE_BASE = 10000.0
S5_GROUPS = 32
S5_CH = 16
S5_WIDTH = S5_GROUPS * S5_CH
S5_P = 64
S5_T = 32
S5_TW = S5_T * S5_CH
S5_ST = 4 * S5_P
S5_OCT = LANES // S5_CH
S5_MAX_ROWS = 256
MIX_WIDTH = SSD_WIDTH + ATT_WIDTH + S5_WIDTH

O_Z = 0
O_XS = SSD_WIDTH
O_BC = O_XS + SSD_WIDTH
O_DT = O_BC + SSD_BC
O_Q = O_DT + 2 * SSD_HEADS
O_K = O_Q + ATT_WIDTH
O_V = O_K + KV_WIDTH
O_U = O_V + KV_WIDTH
O_END = O_U + S5_WIDTH
C_Q = 0
C_Z = 768
C_XS = 1536
C_BC = 2304
C_K = 2560
C_V = 2816
C_U = 3072
C_DT = 3584
DT_PAD = 128
D_INP = C_DT + DT_PAD

FFN_UP_ROWS = 1024
FFN_UP_COLS = 512

NEG = -1e30
VMEM_LIMIT = 56 * 1024 * 1024


def _cparams(*sem):
    return pltpu.CompilerParams(dimension_semantics=sem, vmem_limit_bytes=VMEM_LIMIT)


def _resident(shape):
    nd = len(shape)
    return pl.BlockSpec(shape, lambda *_: (0,) * nd, pipeline_mode=pl.Buffered(1))


def _layer_resident(shape, layer, block=0):
    rest = (0,) * (len(shape) - 1)
    return pl.BlockSpec((None,) + shape, lambda *_: (layer, block) + rest, pipeline_mode=pl.Buffered(1))


def _rms(x, g):
    return x * lax.rsqrt(jnp.mean(x * x, axis=-1, keepdims=True) + EPS) * g


def _silu(x):
    return x * jax.nn.sigmoid(x)


def _dot(a, b):
    return jnp.dot(a, b, preferred_element_type=F32)


def _dot_wide(a, b):
    a_hi, a_lo = _split_bf16(a, 2)
    b_hi, b_lo = _split_bf16(b, 2)
    return _dot(a_hi, b_hi) + (_dot(a_hi, b_lo) + _dot(a_lo, b_hi))


def _dot_nt(a, b):
    return lax.dot_general(a, b, (((1,), (1,)), ((), ())), preferred_element_type=F32)


def _dot_tn(a, b):
    return lax.dot_general(a, b, (((0,), (0,)), ((), ())), preferred_element_type=F32)


def _split_bf16(x, parts):
    out = []
    for _ in range(parts):
        p = x.astype(BF16)
        out.append(p)
        x = x - p.astype(F32)
    return out


def _dot_sel_rhs(x, sel, parts):
    acc = None
    for p in _split_bf16(x, parts):
        t = _dot(p, sel)
        acc = t if acc is None else acc + t
    return acc


def _dot_sel_lhs(sel, x, parts):
    acc = None
    for p in _split_bf16(x, parts):
        t = _dot(sel, p)
        acc = t if acc is None else acc + t
    return acc


def _mod_spec(tm, rows_per_mod, first_row):
    return pl.BlockSpec((1, 1, 6 * D_MODEL), lambda i, *_: (first_row + (i * tm) // rows_per_mod, 0, 0))


def _ada_kernel(c_ref, w_ref, b_ref, o_ref):
    c = c_ref[...]
    o_ref[0] = _dot(_silu(c).astype(BF16), w_ref[0].astype(BF16)) + b_ref[0]


def _ada(cvecs, w_ada, b_ada):
    tn = 1024
    rows = cvecs.shape[0]
    return pl.pallas_call(
        _ada_kernel,
        grid=(N_LAYERS, 6 * D_MODEL // tn),
        in_specs=[pl.BlockSpec((rows, D_MODEL), lambda l, j: (0, 0)),
                  pl.BlockSpec((1, D_MODEL, tn), lambda l, j: (l, 0, j)),
                  pl.BlockSpec((1, 1, tn), lambda l, j: (l, 0, j))],
        out_specs=pl.BlockSpec((1, rows, tn), lambda l, j: (l, 0, j)),
        out_shape=jax.ShapeDtypeStruct((N_LAYERS, rows, 6 * D_MODEL), F32),
        compiler_params=_cparams("parallel", "parallel"),
        name="ada_mod",
    )(cvecs, w_ada, b_ada.reshape(N_LAYERS, 1, 6 * D_MODEL))


def _rope(x, cos, sin_signed, first_half):
    outs = []
    for j in range(x.shape[1] // LANES):
        xj = x[:, j * LANES:(j + 1) * LANES]
        partner = jnp.where(first_half, pltpu.roll(xj, LANES - HEAD_DIM // 2, axis=1),
                            pltpu.roll(xj, HEAD_DIM // 2, axis=1))
        outs.append(xj * cos + partner * sin_signed)
    return jnp.concatenate(outs, axis=-1)


def _in_proj_kernel(*refs, rope):
    if rope:
        x_ref, mod_ref, g_ref, w_ref, cos_ref, sin_ref, o_ref, qv_ref, kt_ref = refs
    else:
        x_ref, mod_ref, g_ref, w_ref, o_ref, qv_ref, kt_ref = refs
    mod = mod_ref[0]
    sh = mod[:, 0:D_MODEL]
    sc = mod[:, D_MODEL:2 * D_MODEL]
    h = _rms(x_ref[...], g_ref[...]) * (1.0 + sc) + sh
    o_ref[...] = _dot(h.astype(BF16), w_ref[...])
    o_ref[:, C_Z:C_Z + SSD_WIDTH] = _silu(o_ref[:, C_Z:C_Z + SSD_WIDTH])
    q = o_ref[:, C_Q:C_Q + ATT_WIDTH]
    k = o_ref[:, C_K:C_K + KV_WIDTH]
    if rope:
        lane = lax.broadcasted_iota(jnp.int32, (x_ref.shape[0], LANES), 1)
        first_half = (lane % HEAD_DIM) < (HEAD_DIM // 2)
        q = _rope(q, cos_ref[...], sin_ref[...], first_half)
        k = _rope(k, cos_ref[...], sin_ref[...], first_half)
    qv_ref[:, 0:ATT_WIDTH] = (q * Q_SCALE).astype(BF16)
    qv_ref[:, ATT_WIDTH:] = o_ref[:, C_V:C_V + KV_WIDTH].astype(BF16)
    kt_ref[...] = k.T.astype(BF16)


def _in_proj(x, mod3, gamma, w, *, layer, tm, mod_spec, rope_tables=None, seq_len=None):
    n = x.shape[0]
    rope = rope_tables is not None
    tables, table_specs = (), []
    if rope:
        per_seq = seq_len // tm
        tables = tuple(rope_tables)
        table_specs = [pl.BlockSpec((tm, LANES), lambda i: (i % per_seq, 0))] * 2
    return pl.pallas_call(
        functools.partial(_in_proj_kernel, rope=rope),
        grid=(n // tm,),
        in_specs=[pl.BlockSpec((tm, D_MODEL), lambda i: (i, 0)),
                  mod_spec,
                  _resident((1, D_MODEL)),
                  _layer_resident((D_MODEL, D_INP), layer)] + table_specs,
        out_specs=[pl.BlockSpec((tm, D_INP), lambda i: (i, 0)), pl.BlockSpec((tm, QV_WIDTH), lambda i: (i, 0)),
                   pl.BlockSpec((KV_WIDTH, tm), lambda i: (0, i))],
        out_shape=[jax.ShapeDtypeStruct((n, D_INP), F32), jax.ShapeDtypeStruct((n, QV_WIDTH), BF16),
                   jax.ShapeDtypeStruct((KV_WIDTH, n), BF16)],
        compiler_params=_cparams("parallel"),
        name="in_proj",
    )(x, mod3, gamma, w, *tables)


def _gelu_tanh(x):
    return 0.5 * x * (1.0 + jnp.tanh(math.sqrt(2.0 / math.pi) * (x + 0.044715 * (x * x * x))))


def _out_proj_kernel(x_ref, mod_ref, g_ref, yssd_ref, oatt_ref, ys5_ref, u_ref, d_ref, wglu_ref, bglu_ref,
                     w12_ref, w3_ref, o_ref, mix_scr):
    mix_scr[:, 0:SSD_WIDTH] = yssd_ref[...]
    mix_scr[:, SSD_WIDTH:] = oatt_ref[...]
    mix = _dot(mix_scr[...], w12_ref[...])
    y5 = ys5_ref[...] + d_ref[...] * u_ref[...]
    g = _gelu_tanh(y5)
    s5 = g * jax.nn.sigmoid(_dot(g.astype(BF16), wglu_ref[...]) + bglu_ref[...])
    mix = mix + _dot(s5.astype(BF16), w3_ref[...])
    gate = mod_ref[0][:, 2 * D_MODEL:3 * D_MODEL]
    o_ref[...] = x_ref[...] + gate * _rms(mix, g_ref[...])


def _out_proj(x, mod3, gamma, y_ssd, o_att, y_s5, proj, s5_d, w_glu, b_glu, w_out, *, layer, tm, mod_spec):
    n = x.shape[0]
    row = lambda i: (i, 0)
    return pl.pallas_call(
        _out_proj_kernel,
        grid=(n // tm,),
        in_specs=[pl.BlockSpec((tm, D_MODEL), row),
                  mod_spec,
                  _resident((1, D_MODEL)),
                  pl.BlockSpec((tm, SSD_WIDTH), row),
                  pl.BlockSpec((tm, ATT_WIDTH), row),
                  pl.BlockSpec((tm, S5_WIDTH), row),
                  pl.BlockSpec((tm, S5_WIDTH), lambda i: (i, C_U // S5_WIDTH)),
                  _resident((1, S5_WIDTH)),
                  _layer_resident((S5_WIDTH, S5_WIDTH), layer),
                  _resident((1, S5_WIDTH)),
                  _layer_resident((SSD_WIDTH + ATT_WIDTH, D_MODEL), layer, 0),
                  _layer_resident((S5_WIDTH, D_MODEL), layer, (SSD_WIDTH + ATT_WIDTH) // S5_WIDTH)],
        out_specs=pl.BlockSpec((tm, D_MODEL), row),
        out_shape=jax.ShapeDtypeStruct((n, D_MODEL), F32),
        scratch_shapes=[pltpu.VMEM((tm, SSD_WIDTH + ATT_WIDTH), BF16)],
        compiler_params=_cparams("parallel"),
        name="out_proj",
    )(x, mod3, gamma, y_ssd, o_att, y_s5, proj, s5_d, w_glu, b_glu, w_out, w_out)


def _ffn_up_kernel(x_ref, mod_ref, gpre_ref, wg_ref, wu_ref, act_ref, h_scr):
    j = pl.program_id(1)

    def gated(h):
        return (_silu(_dot(h, wg_ref[...])) * _dot(h, wu_ref[...])).astype(BF16)

    @pl.when(j == 0)
    def _():
        mod = mod_ref[0]
        sh = mod[:, 3 * D_MODEL:4 * D_MODEL]
        sc = mod[:, 4 * D_MODEL:5 * D_MODEL]
        h = (_rms(x_ref[...], gpre_ref[...]) * (1.0 + sc) + sh).astype(BF16)
        h_scr[...] = h
        act_ref[...] = gated(h)

    @pl.when(j > 0)
    def _():
        act_ref[...] = gated(h_scr[...])


def _ffn_down_kernel(x_ref, mod_ref, gpost_ref, act_ref, wo_ref, o_ref):
    gate = mod_ref[0][:, 5 * D_MODEL:6 * D_MODEL]
    o_ref[...] = x_ref[...] + gate * _rms(_dot(act_ref[...], wo_ref[...]), gpost_ref[...])


def _ffn(x, mod3, g_pre, g_post, w_in, w_out, *, layer, tm_up, tf, tm_down, mod_spec_up, mod_spec_down):
    n = x.shape[0]
    nf = D_FF // tf
    act = pl.pallas_call(
        _ffn_up_kernel,
        grid=(n // tm_up, nf),
        in_specs=[pl.BlockSpec((tm_up, D_MODEL), lambda i, j: (i, 0)),
                  mod_spec_up,
                  _resident((1, D_MODEL)),
                  pl.BlockSpec((None, D_MODEL, tf), lambda i, j: (layer, 0, j)),
                  pl.BlockSpec((None, D_MODEL, tf), lambda i, j: (layer, 0, j + nf))],
        out_specs=pl.BlockSpec((tm_up, tf), lambda i, j: (i, j)),
        out_shape=jax.ShapeDtypeStruct((n, D_FF), BF16),
        scratch_shapes=[pltpu.VMEM((tm_up, D_MODEL), BF16)],
        compiler_params=_cparams("parallel", "arbitrary"),
        name="ffn_up",
    )(x, mod3, g_pre, w_in, w_in)
    return pl.pallas_call(
        _ffn_down_kernel,
        grid=(n // tm_down,),
        in_specs=[pl.BlockSpec((tm_down, D_MODEL), lambda i: (i, 0)),
                  mod_spec_down,
                  _resident((1, D_MODEL)),
                  pl.BlockSpec((tm_down, D_FF), lambda i: (i, 0)),
                  _layer_resident((D_FF, D_MODEL), layer)],
        out_specs=pl.BlockSpec((tm_down, D_MODEL), lambda i: (i, 0)),
        out_shape=jax.ShapeDtypeStruct((n, D_MODEL), F32),
        compiler_params=_cparams("parallel"),
        name="ffn_down",
    )(x, mod3, g_post, act, w_out)


def _attend(queries, sink_ref):
    rows = queries[0][0].shape[0]
    srow = lax.broadcasted_iota(jnp.int32, (ATT_GROUP * rows, 1), 0)
    outs = [[None] * ATT_HEADS for _ in queries]
    units = [(i, kv) for i in range(len(queries)) for kv in range(ATT_KV)]

    def masked_scores(unit):
        (q, parts), kv = queries[unit[0]], unit[1]
        heads = range(kv * ATT_GROUP, (kv + 1) * ATT_GROUP)
        q3 = jnp.concatenate([q[:, h * HEAD_DIM:(h + 1) * HEAD_DIM] for h in heads], axis=0)
        scores = []
        for kt, _, mask in parts:
            s = _dot(q3, kt[kv * HEAD_DIM:(kv + 1) * HEAD_DIM, :])
            scores.append(s if mask is None else jnp.where(mask, s, NEG))
        return scores

    def softmax_numerator(unit, scores):
        kv = unit[1]
        heads = range(kv * ATT_GROUP, (kv + 1) * ATT_GROUP)
        sink = jnp.full((ATT_GROUP * rows, 1), sink_ref[heads[-1]] * LOG2E, F32)
        for i in range(ATT_GROUP - 2, -1, -1):
            sink = jnp.where(srow < (i + 1) * rows, sink_ref[heads[i]] * LOG2E, sink)
        blocks = [s[:, j:j + LANES] for s in scores for j in range(0, s.shape[1], LANES)]
        m = jnp.maximum(sink, jnp.max(functools.reduce(jnp.maximum, blocks), axis=-1, keepdims=True))
        p = jnp.concatenate([jnp.exp2(s - m).astype(BF16) for s in scores], axis=1)
        return p, jnp.exp2(sink - m)

    def weighted_values(unit, p, sink_term):
        (_, parts), kv = queries[unit[0]], unit[1]
        lo = kv * HEAD_DIM
        vlo = (lo // LANES) * LANES
        v_first = lo == vlo
        v128 = jnp.concatenate([v[:, vlo:vlo + LANES] for _, v, _ in parts], axis=0)
        lane = lax.broadcasted_iota(jnp.int32, v128.shape, 1)
        keep = (lane < HEAD_DIM) if v_first else (lane >= HEAD_DIM)
        acc = _dot(p, jnp.where(keep, v128, jnp.ones_like(v128)))
        ocol, dcol = (0, HEAD_DIM) if v_first else (HEAD_DIM, 0)
        o3 = acc[:, ocol:ocol + HEAD_DIM] / (acc[:, dcol:dcol + 1] + sink_term)
        for i in range(ATT_GROUP):
            outs[unit[0]][kv * ATT_GROUP + i] = o3[i * rows:(i + 1) * rows]

    scores = {0: masked_scores(units[0])}
    probs = {}
    for step in range(len(units) + 2):
        if step + 1 < len(units):
            scores[step + 1] = masked_scores(units[step + 1])
        if step - 1 in probs:
            weighted_values(units[step - 1], *probs.pop(step - 1))
        if step in scores:
            probs[step] = softmax_numerator(units[step], scores.pop(step))
    return [jnp.concatenate(o, axis=-1) for o in outs]


def _ctx_attn_kernel(sink_ref, q_ref, kt_ref, v_ref, o_ref):
    def scores(h):
        lo = (h // ATT_GROUP) * HEAD_DIM
        return _dot(q_ref[:, h * HEAD_DIM:(h + 1) * HEAD_DIM], kt_ref[lo:lo + HEAD_DIM, :])

    outs = []
    ahead = scores(0)
    for h in range(ATT_HEADS):
        s = ahead
        if h + 1 < ATT_HEADS:
            ahead = scores(h + 1)
        lo = (h // ATT_GROUP) * HEAD_DIM
        sink = sink_ref[h] * LOG2E
        m = jnp.maximum(jnp.max(s, axis=-1, keepdims=True), sink)
        p = jnp.exp2(s - m)
        den = jnp.exp2(sink - m) + jnp.sum(p, axis=-1, keepdims=True)
        outs.append(_dot(p.astype(BF16), v_ref[:, lo:lo + HEAD_DIM]) / den)
    o_ref[...] = jnp.concatenate(outs, axis=-1).astype(o_ref.dtype)


def _ctx_attention(qv, kt, sink, *, n_seq, seq_len):
    vcol = ATT_WIDTH // KV_WIDTH
    return pl.pallas_call(
        _ctx_attn_kernel,
        grid=(n_seq,),
        in_specs=[pl.BlockSpec(memory_space=pltpu.SMEM),
                  pl.BlockSpec((seq_len, ATT_WIDTH), lambda b: (b, 0)),
                  pl.BlockSpec((KV_WIDTH, seq_len), lambda b: (0, b)),
                  pl.BlockSpec((seq_len, KV_WIDTH), lambda b: (b, vcol))],
        out_specs=pl.BlockSpec((seq_len, ATT_WIDTH), lambda b: (b, 0)),
        out_shape=jax.ShapeDtypeStruct((n_seq * seq_len, ATT_WIDTH), BF16),
        compiler_params=_cparams("parallel"),
        name="ctx_attention",
    )(sink, qv, kt, qv)


def _lat_attn_kernel(sink_ref, q_ref, *refs, nq):
    kt_refs, v_refs = refs[0:nq + 2], refs[nq + 2:2 * nq + 4]
    ck_ref, cv_ref, o_ref = refs[2 * nq + 4:]
    step = pl.program_id(1)
    blk = ATT_BLOCK
    row = lax.broadcasted_iota(jnp.int32, (ATT_GROUP * blk, blk), 0) % blk
    col = lax.broadcasted_iota(jnp.int32, (ATT_GROUP * blk, blk), 1)
    before = [col >= row] * nq
    after = [col <= row] * nq
    before[0] = jnp.logical_and(before[0], step > 0)
    after[-1] = jnp.logical_and(after[-1], step < pl.num_programs(1) - 1)
    ctx = (ck_ref[0], cv_ref[0], None)
    blocks = [(kt[...], v[...]) for kt, v in zip(kt_refs, v_refs)]
    queries = [(q_ref[i * blk:(i + 1) * blk, :],
                [ctx, blocks[i] + (before[i],), blocks[i + 1] + (None,), blocks[i + 2] + (after[i],)])
               for i in range(nq)]
    for i, o in enumerate(_attend(queries, sink_ref)):
        o_ref[i * blk:(i + 1) * blk, :] = o.astype(o_ref.dtype)


def _lat_attention(qv, kt, sink, ckt, cv, *, n_seq, seq_len):
    blk = ATT_BLOCK
    nb = seq_len // blk
    nq = max(c for c in range(1, LAT_Q_BLOCKS + 1) if nb % c == 0)
    steps = nb // nq

    def block(offset):
        return lambda b, n: b * nb + jnp.clip(nq * n + offset, 0, nb - 1)

    vcol = ATT_WIDTH // KV_WIDTH
    offsets = range(-1, nq + 1)
    kt_specs = [pl.BlockSpec((KV_WIDTH, blk), lambda b, n, f=block(o): (0, f(b, n))) for o in offsets]
    v_specs = [pl.BlockSpec((blk, KV_WIDTH), lambda b, n, f=block(o): (f(b, n), vcol)) for o in offsets]
    return pl.pallas_call(
        functools.partial(_lat_attn_kernel, nq=nq),
        grid=(n_seq, steps),
        in_specs=[pl.BlockSpec(memory_space=pltpu.SMEM),
                  pl.BlockSpec((nq * blk, ATT_WIDTH), lambda b, n: (b * steps + n, 0))] + kt_specs + v_specs
                 + [pl.BlockSpec((1,) + ckt.shape[1:], lambda b, n: (b, 0, 0)),
                    pl.BlockSpec((1,) + cv.shape[1:], lambda b, n: (b, 0, 0))],
        out_specs=pl.BlockSpec((nq * blk, ATT_WIDTH), lambda b, n: (b * steps + n, 0)),
        out_shape=jax.ShapeDtypeStruct((n_seq * seq_len, ATT_WIDTH), BF16),
        compiler_params=_cparams("parallel", "parallel"),
        name="lat_attention",
    )(sink, qv, *([kt] * (nq + 2)), *([qv] * (nq + 2)), ckt, cv)


def _rope_tables(seq_len):
    rows = seq_len // GRID_W
    row = np.repeat(np.arange(rows, dtype=np.float32), GRID_W)
    col = np.tile(np.arange(GRID_W, dtype=np.float32), rows)
    inv = np.float32(ROPE_BASE) ** (-np.arange(ROPE_PER_AXIS, dtype=np.float32) / np.float32(ROPE_PER_AXIS))
    ang = np.concatenate([row[:, None] * inv, col[:, None] * inv], axis=-1)
    cos, sin = np.cos(ang), np.sin(ang)
    cos128 = np.tile(cos, (1, 4))
    sin128 = np.tile(np.concatenate([-sin, sin], axis=-1), (1, 2))
    return jnp.asarray(cos128, F32), jnp.asarray(sin128, F32)


def _conv_silu(scr, prev_ref, cur_ref, next_ref, w_ref, b_ref, at_start, at_end):
    n = cur_ref.shape[0]
    h = SSD_HALO
    scr[0:h, :] = jnp.where(at_start, 0.0, prev_ref[...])
    scr[h:h + n, :] = cur_ref[...]
    scr[h + n:h + n + h, :] = jnp.where(at_end, 0.0, next_ref[...])
    return _silu(scr[h - 1:h - 1 + n, :] * w_ref[0:1, :] + scr[h:h + n, :] * w_ref[1:2, :]
                 + scr[h + 1:h + 1 + n, :] * w_ref[2:3, :] + b_ref[...])


def _softplus(x):
    return jnp.maximum(x, 0.0) + jnp.log1p(jnp.exp(-jnp.abs(x)))


def _ssd_bwd_kernel(xsp_ref, xsc_ref, xsn_ref, bcp_ref, bcc_ref, bcn_ref, dt_ref, h0_ref,
                    cwx_ref, cbx_ref, cwb_ref, cbb_ref, dtb_ref, alog_ref, tri_ref, exp_ref,
                    hstart_ref, hfin_ref, xbc_ref, xs_scr, bc_scr, h_scr):
    i = pl.program_id(1)
    q = SSD_CHUNK
    half = SSD_WIDTH // SSD_GROUPS
    sel_b = exp_ref[1]
    neg_a = -LOG2E * jnp.exp(alog_ref[...])
    at_start = i == pl.num_programs(1) - 1
    at_end = i == 0
    xbc_ref[:, 0:SSD_WIDTH] = _conv_silu(xs_scr, xsp_ref, xsc_ref, xsn_ref, cwx_ref, cbx_ref, at_start, at_end)
    xbc_ref[:, SSD_WIDTH:] = _conv_silu(bc_scr, bcp_ref, bcc_ref, bcn_ref, cwb_ref, cbb_ref, at_start, at_end)

    @pl.when(i == 0)
    def _():
        h_scr[...] = h0_ref[0]

    hb = h_scr[...]
    for j in reversed(range(xsc_ref.shape[0] // q)):
        rows = slice(j * q, (j + 1) * q)
        hstart_ref[0, j] = hb
        bm = xbc_ref[rows, SSD_WIDTH:SSD_WIDTH + SSD_GROUPS * SSD_STATE].astype(BF16)
        dtv = _softplus(dt_ref[rows, :] + dtb_ref[...])
        rev = _dot_sel_lhs(tri_ref[1], dtv * neg_a, 3)
        dt_w = _dot_sel_rhs(dtv, sel_b, 2)
        to_end = _dot_sel_rhs(jnp.exp2(rev[0:1, :] - rev), sel_b, 2)
        decay = _dot_sel_rhs(jnp.exp2(rev[0:SSD_HALO, :]), sel_b, 2)[0:1, :]
        xw = (xbc_ref[rows, 0:SSD_WIDTH] * dt_w * to_end).astype(BF16)
        upd = jnp.concatenate(
            [_dot_tn(bm[:, g * SSD_STATE:(g + 1) * SSD_STATE], xw[:, g * half:(g + 1) * half])
             for g in range(SSD_GROUPS)], axis=-1)
        hb = hb * decay + upd
    h_scr[...] = hb

    @pl.when(i == pl.num_programs(1) - 1)
    def _():
        hfin_ref[0] = hb.T


def _ssd_intra(xs, bc, dt_raw, dtb_ref, alog_ref, tri_ref, exp_ref, dvec_ref):
    q = SSD_CHUNK
    nh = SSD_HEADS
    gs = SSD_GROUPS * SSD_STATE
    bm = bc[:, 0:gs].astype(BF16)
    cm = bc[:, gs:2 * gs].astype(BF16)
    xs_bf = xs.astype(BF16)

    dtv = _softplus(dt_raw + dtb_ref[...])
    dta = dtv * (-LOG2E * jnp.exp(alog_ref[...]))
    lower, upper = tri_ref[0], tri_ref[1]
    cum = _dot_sel_lhs(lower, dta, 3)
    rev = _dot_sel_lhs(upper, dta, 3)
    dt_t = dtv.T
    dta_t = dta.T
    cum_t = _dot_sel_rhs(dta_t, upper, 3)
    rev_t = _dot_sel_rhs(dta_t, lower, 3)

    sel_f, sel_b = exp_ref[0], exp_ref[1]
    dtf_w = _dot_sel_rhs(dtv, sel_f, 2)
    ecum_w = _dot_sel_rhs(jnp.exp2(cum), sel_f, 2)
    erev_w = _dot_sel_rhs(jnp.exp2(rev), sel_b, 2)
    toend_w = _dot_sel_rhs(jnp.exp2(cum[q - 1:q, :] - cum), sel_f, 2)

    row = lax.broadcasted_iota(jnp.int32, (q, q), 0)
    col = lax.broadcasted_iota(jnp.int32, (q, q), 1)
    below = row > col
    log_dt_t = jnp.log2(dt_t)
    src_f = log_dt_t - cum_t
    src_b = log_dt_t - rev_t
    gmat = [_dot_nt(cm[:, g * SSD_STATE:(g + 1) * SSD_STATE], bm[:, g * SSD_STATE:(g + 1) * SSD_STATE])
            for g in range(SSD_GROUPS)]
    per_group = nh // SSD_GROUPS
    half = SSD_WIDTH // SSD_GROUPS
    ys = []
    for h in range(nh):
        e = jnp.where(below, cum[:, h:h + 1] + src_f[h:h + 1, :],
                      rev[:, nh + h:nh + h + 1] + src_b[nh + h:nh + h + 1, :])
        m = (gmat[h // per_group] * jnp.exp2(e)).astype(BF16)
        ys.append(_dot(m, xs_bf[:, h * SSD_HEAD_DIM:(h + 1) * SSD_HEAD_DIM]))
    cb = cm.astype(F32) * bm.astype(F32)
    g_diag = jnp.concatenate(
        [jnp.broadcast_to(jnp.sum(cb[:, g * SSD_STATE:(g + 1) * SSD_STATE], axis=-1, keepdims=True), (q, half))
         for g in range(SSD_GROUPS)], axis=-1)
    y = jnp.concatenate(ys, axis=-1) + (g_diag * dtf_w + dvec_ref[...]) * xs
    xw = (xs * dtf_w * toend_w).astype(BF16)
    upd = jnp.concatenate(
        [_dot_tn(bm[:, g * SSD_STATE:(g + 1) * SSD_STATE], xw[:, g * half:(g + 1) * half])
         for g in range(SSD_GROUPS)], axis=-1)
    return y, cm, ecum_w, erev_w, upd


def _ssd_carry(hf, hb_start, gz, intra, nw_ref):
    y, cm, ecum_w, erev_w, upd = intra
    half = SSD_WIDTH // SSD_GROUPS
    hf_bf = hf.astype(BF16)
    hb_bf = hb_start.astype(BF16)
    off_f = jnp.concatenate([_dot(cm[:, g * SSD_STATE:(g + 1) * SSD_STATE], hf_bf[:, g * half:(g + 1) * half])
                             for g in range(SSD_GROUPS)], axis=-1)
    off_b = jnp.concatenate([_dot(cm[:, g * SSD_STATE:(g + 1) * SSD_STATE], hb_bf[:, g * half:(g + 1) * half])
                             for g in range(SSD_GROUPS)], axis=-1)
    y = y + off_f * ecum_w + off_b * erev_w
    y = _rms(y * gz, nw_ref[...])
    return y, hf * ecum_w[SSD_CHUNK - 1:SSD_CHUNK, :] + upd


def _ssd_fwd_kernel(gz_ref, xs_ref, bc_ref, dt_ref, hb_ref, h0_ref,
                    dtb_ref, alog_ref, tri_ref, exp_ref, dvec_ref, nw_ref, y_ref, hfin_ref, h_scr):
    c = pl.program_id(1)
    q = SSD_CHUNK
    cps = xs_ref.shape[0] // q
    rows = [slice(j * q, (j + 1) * q) for j in range(cps)]

    def intra(j):
        return _ssd_intra(xs_ref[rows[j], :], bc_ref[rows[j], :], dt_ref[rows[j], :],
                          dtb_ref, alog_ref, tri_ref, exp_ref, dvec_ref)

    @pl.when(c == 0)
    def _():
        h_scr[...] = h0_ref[0]

    hf = h_scr[...]
    ahead = intra(0)
    for j in range(cps):
        cur = ahead
        if j + 1 < cps:
            ahead = intra(j + 1)
        y, hf = _ssd_carry(hf, hb_ref[0, j], gz_ref[rows[j], :], cur, nw_ref)
        y_ref[rows[j], :] = y.astype(y_ref.dtype)
    h_scr[...] = hf

    @pl.when(c == pl.num_programs(1) - 1)
    def _():
        hfin_ref[0] = hf.T


def _ssd_mixer(proj, h0_f, h0_b, consts, *, n_seq, seq_len):
    q = SSD_CHUNK
    nc = seq_len // q
    cps = max(c for c in range(1, SSD_CHUNKS_PER_STEP + 1) if nc % c == 0)
    ns = nc // cps
    rows = cps * q
    (cwx, cbx, cwb, cbb, dt_bias, a_log, tri, expand, dvec, nw) = consts
    per8 = rows // SSD_HALO
    last8 = proj.shape[0] // SSD_HALO - 1

    def raw_specs(width, col):
        step = lambda b, i: b * ns + ns - 1 - i
        return [pl.BlockSpec((SSD_HALO, width), lambda b, i: (jnp.maximum(step(b, i) * per8 - 1, 0), col)),
                pl.BlockSpec((rows, width), lambda b, i: (step(b, i), col)),
                pl.BlockSpec((SSD_HALO, width), lambda b, i: (jnp.minimum((step(b, i) + 1) * per8, last8), col))]

    def specs(chunk_of):
        def cur(b, i):
            return b * ns + chunk_of(i)

        return [pl.BlockSpec((rows, SSD_WIDTH), lambda b, i: (cur(b, i), 0)),
                pl.BlockSpec((rows, SSD_BC), lambda b, i: (cur(b, i), SSD_WIDTH // SSD_BC)),
                pl.BlockSpec((rows, DT_PAD), lambda b, i: (cur(b, i), C_DT // DT_PAD))], cur

    state_spec = pl.BlockSpec((1, SSD_STATE, SSD_WIDTH), lambda b, i: (b, 0, 0))
    final_spec = pl.BlockSpec((1, SSD_WIDTH, SSD_STATE), lambda b, i: (b, 0, 0))
    const_specs = [_resident(dt_bias.shape), _resident(a_log.shape), _resident(tri.shape),
                   _resident(expand.shape)]
    scratch = [pltpu.VMEM((SSD_STATE, SSD_WIDTH), F32)]

    data_specs, rcur = specs(lambda i: ns - 1 - i)
    hb_start, hb_fin, xbc = pl.pallas_call(
        _ssd_bwd_kernel,
        grid=(n_seq, ns),
        in_specs=(raw_specs(SSD_WIDTH, C_XS // SSD_WIDTH) + raw_specs(SSD_BC, C_BC // SSD_BC)
                  + [data_specs[2], state_spec]
                  + [_resident(cwx.shape), _resident(cbx.shape), _resident(cwb.shape), _resident(cbb.shape)]
                  + const_specs),
        out_specs=[pl.BlockSpec((1, cps, SSD_STATE, SSD_WIDTH), lambda b, i: (b, ns - 1 - i, 0, 0)), final_spec,
                   pl.BlockSpec((rows, SSD_WIDTH + SSD_BC), lambda b, i: (rcur(b, i), 0))],
        out_shape=[jax.ShapeDtypeStruct((n_seq, nc, SSD_STATE, SSD_WIDTH), F32),
                   jax.ShapeDtypeStruct((n_seq, SSD_WIDTH, SSD_STATE), F32),
                   jax.ShapeDtypeStruct((n_seq * seq_len, SSD_WIDTH + SSD_BC), F32)],
        scratch_shapes=[pltpu.VMEM((rows + 2 * SSD_HALO, SSD_WIDTH), F32),
                        pltpu.VMEM((rows + 2 * SSD_HALO, SSD_BC), F32)] + scratch,
        compiler_params=_cparams("parallel", "arbitrary"),
        name="ssd_backward_states",
    )(proj, proj, proj, proj, proj, proj, proj, h0_b, cwx, cbx, cwb, cbb, dt_bias, a_log, tri, expand)

    data_specs, cur = specs(lambda i: i)
    y, hf_fin = pl.pallas_call(
        _ssd_fwd_kernel,
        grid=(n_seq, ns),
        in_specs=([pl.BlockSpec((rows, SSD_WIDTH), lambda b, i: (cur(b, i), C_Z // SSD_WIDTH))] + data_specs
                  + [pl.BlockSpec((1, cps, SSD_STATE, SSD_WIDTH), lambda b, i: (b, i, 0, 0)), state_spec]
                  + const_specs + [_resident(dvec.shape), _resident(nw.shape)]),
        out_specs=[pl.BlockSpec((rows, SSD_WIDTH), lambda b, i: (b * ns + i, 0)), final_spec],
        out_shape=[jax.ShapeDtypeStruct((n_seq * seq_len, SSD_WIDTH), BF16),
                   jax.ShapeDtypeStruct((n_seq, SSD_WIDTH, SSD_STATE), F32)],
        scratch_shapes=scratch,
        compiler_params=_cparams("parallel", "arbitrary"),
        name="ssd_forward",
    )(proj, xbc, xbc, proj, hb_start, h0_f, dt_bias, a_log, tri, expand, dvec, nw)
    return y, hf_fin, hb_fin


def _ssd_consts(l, ssd_conv_w, ssd_conv_b, ssd_dt_bias, ssd_A_log, ssd_D, ssd_norm):
    q = SSD_CHUNK
    cw, cb = ssd_conv_w[l], ssd_conv_b[l]
    pad = DT_PAD - 2 * SSD_HEADS
    dt_bias = jnp.pad(ssd_dt_bias[l].reshape(1, -1), ((0, 0), (0, pad)))
    a_log = jnp.pad(ssd_A_log[l].reshape(1, -1), ((0, 0), (0, pad)))
    r = np.arange(q)
    tri = jnp.asarray(np.stack([r[None, :] <= r[:, None], r[None, :] >= r[:, None]]), BF16)
    head_of_col = np.arange(SSD_WIDTH) // SSD_HEAD_DIM
    j = np.arange(DT_PAD)
    expand = jnp.asarray(np.stack([j[:, None] == head_of_col[None, :],
                                   j[:, None] == head_of_col[None, :] + SSD_HEADS]), BF16)
    dvec = jnp.repeat(ssd_D[l], SSD_HEAD_DIM).reshape(1, SSD_WIDTH)
    return (cw[:, :SSD_WIDTH], cb[:SSD_WIDTH].reshape(1, -1), cw[:, SSD_WIDTH:], cb[SSD_WIDTH:].reshape(1, -1),
            dt_bias, a_log, tri, expand, dvec, ssd_norm[l].reshape(1, -1))


def _ssd_state_in(s):
    b = s.shape[0]
    return jnp.transpose(s, (0, 3, 1, 2)).reshape(b, SSD_STATE, SSD_WIDTH)


def _s5_table_kernel(pw_ref, ct_ref, bb_ref, bbt_ref, sel_ref, tile_ref, rhs_ref, woff_ref):
    t = S5_T
    c = S5_CH
    tile = tile_ref[...]
    spread = lambda x, sel: _dot_sel_rhs(x, sel, 2)
    krows, offs, cols = [], [], []
    for d in range(2):
        pw_re, pw_im = pw_ref[0, 2 * d], pw_ref[0, 2 * d + 1]
        c_re = spread(ct_ref[0, :, (2 * d) * c:(2 * d + 1) * c], tile)
        c_im = spread(ct_ref[0, :, (2 * d + 1) * c:(2 * d + 2) * c], tile)
        b_re = spread(bb_ref[0, :, (2 * d) * c:(2 * d + 1) * c], tile)
        b_im = spread(bb_ref[0, :, (2 * d + 1) * c:(2 * d + 2) * c], tile)
        bt_re = bbt_ref[0, (2 * d) * c:(2 * d + 1) * c, :]
        bt_im = bbt_ref[0, (2 * d + 1) * c:(2 * d + 2) * c, :]
        up = (spread(pw_re, sel_ref[0]), spread(pw_im, sel_ref[0]))
        down = (spread(pw_re, sel_ref[1]), spread(pw_im, sel_ref[1]))
        (q_re, q_im), state = ((up, down), (down, up))[d]
        g_re = q_re * c_re - q_im * c_im
        g_im = q_re * c_im + q_im * c_re
        krows.append(_dot_wide(bt_re, g_re) - _dot_wide(bt_im, g_im))
        lb_re, lb_im = pw_re[:, 1:2], pw_im[:, 1:2]
        r_re = q_re * lb_re - q_im * lb_im
        r_im = q_re * lb_im + q_im * lb_re
        offs.append((r_re * c_re - r_im * c_im, -(r_re * c_im + r_im * c_re)))
        s_re, s_im = state
        cols.append(((s_re * b_re - s_im * b_im).T, (s_re * b_im + s_im * b_re).T))
    k_f, k_b = krows
    for s in range(t):
        r = t - 1 - s
        f = k_f if s == 0 else jnp.concatenate([jnp.zeros((c, c * s), F32), k_f[:, :S5_TW - c * s]], axis=1)
        b = k_b if r == 0 else jnp.concatenate([k_b[:, c * r:], jnp.zeros((c, c * r), F32)], axis=1)
        rhs_ref[0, s * c:(s + 1) * c, 0:S5_TW] = (f + b).astype(BF16)
    rhs_ref[0, :, S5_TW:] = jnp.concatenate([cols[0][0], cols[1][0], cols[0][1], cols[1][1]], axis=1).astype(BF16)
    woff_ref[0] = jnp.concatenate([offs[0][0], offs[1][0], offs[0][1], offs[1][1]], axis=0).astype(BF16)


def _s5_tables(s5_A_re, s5_A_im, s5_log_dt, s5_B_re, s5_B_im, s5_C_re, s5_C_im):
    t = S5_T
    n = N_LAYERS * S5_GROUPS
    step = jnp.exp(s5_log_dt)[..., None]
    k = jnp.arange(LANES, dtype=F32)
    keep = k <= t
    kk = jnp.where(keep, k, 0.0)
    mag = jnp.exp(kk * (s5_A_re * step)[..., None])
    ang = kk * (s5_A_im * step)[..., None]
    pw_re = jnp.where(keep, mag * jnp.cos(ang), 0.0)
    pw_im = jnp.where(keep, mag * jnp.sin(ang), 0.0)
    lb_re, lb_im = pw_re[..., 1], pw_im[..., 1]
    den = s5_A_re * s5_A_re + s5_A_im * s5_A_im
    r_re = ((lb_re - 1.0) * s5_A_re + lb_im * s5_A_im) / den
    r_im = (lb_im * s5_A_re - (lb_re - 1.0) * s5_A_im) / den
    b_re, b_im = s5_B_re[:, None], s5_B_im[:, None]
    bb_re = r_re[..., None] * b_re - r_im[..., None] * b_im
    bb_im = r_re[..., None] * b_im + r_im[..., None] * b_re
    c_re = jnp.swapaxes(s5_C_re, -1, -2)
    c_im = jnp.swapaxes(s5_C_im, -1, -2)

    def pack(re, im, axis):
        return jnp.concatenate([re[:, 0], im[:, 0], re[:, 1], im[:, 1]], axis=axis)

    pw = jnp.stack([pw_re[:, 0], pw_im[:, 0], pw_re[:, 1], pw_im[:, 1]], axis=2).reshape(n, 4, S5_P, LANES)
    ct = pack(c_re, c_im, -1).reshape(n, S5_P, 4 * S5_CH)
    bb = pack(bb_re, bb_im, -1)
    bbt = jnp.swapaxes(bb, -1, -2).reshape(n, 4 * S5_CH, S5_P)
    bb = bb.reshape(n, S5_P, 4 * S5_CH)
    lam_t = jnp.concatenate([pw_re[:, 0, :, :, t], pw_re[:, 1, :, :, t], pw_im[:, 0, :, :, t],
                             pw_im[:, 1, :, :, t]], axis=-1).reshape(n, 1, S5_ST)
    jt = np.arange(S5_TW) // S5_CH
    lane_k = np.arange(LANES)[:, None]
    sel = jnp.asarray(np.stack([lane_k == f[None, :] for f in (jt, t - 1 - jt)]), BF16)
    tile = jnp.asarray(np.arange(S5_CH)[:, None] == (np.arange(S5_TW) % S5_CH)[None, :], BF16)
    grp = lambda shape: pl.BlockSpec((1,) + shape, lambda i: (i,) + (0,) * len(shape))
    rhs, woff = pl.pallas_call(
        _s5_table_kernel,
        grid=(n,),
        in_specs=[grp((4, S5_P, LANES)), grp((S5_P, 4 * S5_CH)), grp((S5_P, 4 * S5_CH)), grp((4 * S5_CH, S5_P)),
                  _resident(sel.shape), _resident(tile.shape)],
        out_specs=[grp((S5_TW, S5_TW + S5_ST)), grp((S5_ST, S5_TW))],
        out_shape=[jax.ShapeDtypeStruct((n, S5_TW, S5_TW + S5_ST), BF16),
                   jax.ShapeDtypeStruct((n, S5_ST, S5_TW), BF16)],
        compiler_params=_cparams("parallel"),
        name="s5_tables",
    )(pw, ct, bb, bbt, sel, tile)
    return rhs, woff, lam_t


def _unit_transpose(vs, masks, axis, unit):
    size = vs[0].shape[axis]
    cur = list(vs)
    for d in (1, 2, 4):
        nxt = list(cur)
        for a in range(S5_OCT):
            if a & d:
                continue
            b = a | d
            nxt[a] = jnp.where(masks[d], cur[a], pltpu.roll(cur[b], unit * d, axis=axis))
            nxt[b] = jnp.where(masks[d], pltpu.roll(cur[a], size - unit * d, axis=axis), cur[b])
        cur = nxt
    return cur


def _s5_kernel(u_ref, rhs_ref, woff_ref, lam_ref, h0_ref, y_ref, hfin_ref,
               dst_scr, y_scr, sre_scr, sim_scr, hfre_scr, hfim_scr, hbre_scr, hbim_scr, *, nc, rb):
    t = S5_T
    rows = dst_scr.shape[1]
    sb = rows // nc
    tiles = t // S5_OCT
    nr = rb // S5_OCT
    tile_shape = (nr, S5_OCT, LANES)
    seg = lax.broadcasted_iota(jnp.int32, tile_shape, 2) // S5_CH
    sub = lax.broadcasted_iota(jnp.int32, tile_shape, 1)
    lane_masks = {d: (seg & d) == 0 for d in (1, 2, 4)}
    sub_masks = {d: (sub & d) == 0 for d in (1, 2, 4)}

    def tile_rows(i, r, tq):
        return pl.ds((i * rb + r) * tiles + tq, nr, stride=S5_OCT * tiles)

    def gather(i, carry):
        r0 = pl.multiple_of(i * rb, rb)
        for tq in range(tiles):
            by_chunk = [u_ref[tile_rows(i, r, tq)] for r in range(S5_OCT)]
            by_time = _unit_transpose(by_chunk, sub_masks, 1, 1)
            for gl, v in enumerate(_unit_transpose(by_time, lane_masks, 2, S5_CH)):
                dst_scr[gl, pl.ds(r0, rb), tq * LANES:(tq + 1) * LANES] = v.reshape(rb, LANES).astype(BF16)
        return carry

    lax.fori_loop(0, rows // rb, gather, 0)

    for gl in range(S5_OCT):
        z = _dot(dst_scr[gl], rhs_ref[gl])
        y_scr[gl] = z[:, 0:S5_TW]
        sre_scr[gl] = z[:, S5_TW:S5_TW + 2 * S5_P]
        sim_scr[gl] = z[:, S5_TW + 2 * S5_P:]

    fwd_lane = lax.broadcasted_iota(jnp.int32, (sb, 2 * S5_P), 1) < S5_P
    a_re = [lam_ref[gl][:, 0:2 * S5_P] for gl in range(S5_OCT)]
    a_im = [lam_ref[gl][:, 2 * S5_P:] for gl in range(S5_OCT)]

    def step(i, carry):
        fi = pl.ds(i, sb, stride=nc)
        bj = pl.ds(nc - 1 - i, sb, stride=nc)
        out = []
        for gl in range(S5_OCT):
            h_re, h_im = carry[2 * gl], carry[2 * gl + 1]
            hfre_scr[gl, fi, :] = h_re
            hfim_scr[gl, fi, :] = h_im
            hbre_scr[gl, bj, :] = h_re
            hbim_scr[gl, bj, :] = h_im
            s_re = jnp.where(fwd_lane, sre_scr[gl, fi, :], sre_scr[gl, bj, :])
            s_im = jnp.where(fwd_lane, sim_scr[gl, fi, :], sim_scr[gl, bj, :])
            out.append(a_re[gl] * h_re - a_im[gl] * h_im + s_re)
            out.append(a_re[gl] * h_im + a_im[gl] * h_re + s_im)
        return tuple(out)

    init = []
    for gl in range(S5_OCT):
        h0 = h0_ref[gl, 0]
        init += [h0[:, 0:2 * S5_P], h0[:, 2 * S5_P:]]
    fin = lax.fori_loop(0, nc, step, tuple(init), unroll=2 if nc % 2 == 0 else 1)

    all_fwd = lax.broadcasted_iota(jnp.int32, (rows, 2 * S5_P), 1) < S5_P
    for gl in range(S5_OCT):
        hfin_ref[gl, 0] = jnp.concatenate([fin[2 * gl], fin[2 * gl + 1]], axis=-1)
        hin = jnp.concatenate([jnp.where(all_fwd, hfre_scr[gl], hbre_scr[gl]),
                               jnp.where(all_fwd, hfim_scr[gl], hbim_scr[gl])], axis=-1).astype(BF16)
        y_scr[gl] += _dot(hin, woff_ref[gl])

    def scatter(i, carry):
        r0 = pl.multiple_of(i * rb, rb)
        for tq in range(tiles):
            by_group = [y_scr[gl, pl.ds(r0, rb), tq * LANES:(tq + 1) * LANES].reshape(tile_shape)
                        for gl in range(S5_OCT)]
            by_time = _unit_transpose(by_group, lane_masks, 2, S5_CH)
            for r, v in enumerate(_unit_transpose(by_time, sub_masks, 1, 1)):
                y_ref[tile_rows(i, r, tq)] = v
        return carry

    lax.fori_loop(0, rows // rb, scatter, 0)


def _s5_mixer(proj, tables, layer, h0, *, n_seq, seq_len):
    rhs, woff, lam_t = tables
    t = S5_T
    nc = seq_len // t
    sb = max(1, min(n_seq, S5_MAX_ROWS // nc))
    assert n_seq % sb == 0
    nbs = n_seq // sb
    rows = sb * nc
    rb = min(rows, 32)
    noct = S5_GROUPS // S5_OCT
    ucol = C_U // LANES
    tab = lambda shape: pl.BlockSpec((S5_OCT,) + shape, lambda j, s: (layer * noct + j, 0, 0))
    state = pl.BlockSpec((S5_OCT, 1, sb, S5_ST), lambda j, s: (j, s, 0, 0))
    rows_scr = lambda dt: pltpu.VMEM((S5_OCT, rows, 2 * S5_P), dt)
    n_tok = n_seq * seq_len
    tok_block = (sb * seq_len // S5_OCT, S5_OCT, LANES)
    y, hfin = pl.pallas_call(
        functools.partial(_s5_kernel, nc=nc, rb=rb),
        grid=(noct, nbs),
        in_specs=[pl.BlockSpec(tok_block, lambda j, s: (s, 0, ucol + j)),
                  tab(rhs.shape[1:]), tab(woff.shape[1:]), tab(lam_t.shape[1:]), state],
        out_specs=[pl.BlockSpec(tok_block, lambda j, s: (s, 0, j)), state],
        out_shape=[jax.ShapeDtypeStruct((n_tok // S5_OCT, S5_OCT, S5_WIDTH), F32),
                   jax.ShapeDtypeStruct((S5_GROUPS, nbs, sb, S5_ST), F32)],
        scratch_shapes=[pltpu.VMEM((S5_OCT, rows, S5_TW), BF16), pltpu.VMEM((S5_OCT, rows, S5_TW), F32),
                        rows_scr(F32), rows_scr(F32), rows_scr(F32), rows_scr(F32), rows_scr(F32), rows_scr(F32)],
        compiler_params=_cparams("parallel", "parallel"),
        name="s5_mixer",
    )(proj.reshape(n_tok // S5_OCT, S5_OCT, D_INP), rhs, woff, lam_t, h0.reshape(S5_GROUPS, nbs, sb, S5_ST))
    return y.reshape(n_tok, S5_WIDTH), hfin.reshape(S5_GROUPS, n_seq, S5_ST)


def _s5_state_in(re, im):
    b = re.shape[0]
    x = jnp.stack([re, im], axis=1)
    return jnp.transpose(x, (3, 0, 1, 2, 4)).reshape(S5_GROUPS, b, S5_ST)


def _s5_state_out(h):
    n_seq = h.shape[1]
    x = h.reshape(S5_GROUPS, n_seq, 2, 2, S5_P)
    x = jnp.transpose(x, (2, 1, 3, 0, 4))
    return x[0], x[1]


def kernel(x_prompt, x_sample, c, cache_k, cache_v, state_ssd, state_s5_re, state_s5_im, c_ctx, w_ada, b_ada, norm_mix_pre, norm_mix_post, norm_ffn_pre, norm_ffn_post, w_in, w_out, ssd_conv_w, ssd_conv_b, ssd_dt_bias, ssd_A_log, ssd_D, ssd_norm, attn_sink, s5_A_re, s5_A_im, s5_log_dt, s5_B_re, s5_B_im, s5_C_re, s5_C_im, s5_D, s5_w_glu, s5_b_glu, w_ffn_in, w_ffn_out):
    nb_ctx, len_ctx, _ = x_prompt.shape
    nb_lat, len_lat, _ = x_sample.shape
    n_ctx = nb_ctx * len_ctx
    n_lat = nb_lat * len_lat
    tm = 512
    mod_rows = -(-(1 + nb_lat) // 8) * 8
    cvecs = jnp.concatenate([c_ctx[None, :], c, jnp.zeros((mod_rows - 1 - nb_lat, D_MODEL), F32)], axis=0)
    mods = _ada(cvecs, w_ada, b_ada)
    mod_geom = ((n_ctx, 0), (len_lat, 1))
    token_sets = tuple(dict(mod_spec=_mod_spec(tm, *g)) for g in mod_geom)
    tm_up = min(FFN_UP_ROWS, n_ctx, len_lat)
    rope_args = (dict(), dict(rope_tables=_rope_tables(len_lat), seq_len=len_lat))

    s5_tab = _s5_tables(s5_A_re, s5_A_im, s5_log_dt, s5_B_re, s5_B_im, s5_C_re, s5_C_im)
    w_in_b = w_in.astype(BF16)
    w_in_p = jnp.concatenate(
        [w_in_b[:, :, O_Q:O_K], w_in_b[:, :, O_Z:O_DT], w_in_b[:, :, O_K:O_END], w_in_b[:, :, O_DT:O_Q],
         jnp.zeros((N_LAYERS, D_MODEL, DT_PAD - 2 * SSD_HEADS), BF16)], axis=2)
    w_out_b = w_out.astype(BF16)
    w_glu_b = s5_w_glu.astype(BF16)
    w_ffn_in_b = w_ffn_in.astype(BF16)
    w_ffn_out_b = w_ffn_out.astype(BF16)
    zeros_ssd = jnp.zeros((nb_ctx, SSD_STATE, SSD_WIDTH), F32)
    zeros_s5 = jnp.zeros((S5_GROUPS, nb_ctx, S5_ST), F32)

    xs = [x_prompt.reshape(n_ctx, D_MODEL), x_sample.reshape(n_lat, D_MODEL)]
    new_k, new_v, new_ssd, new_s5_re, new_s5_im = [], [], [], [], []
    for l in range(N_LAYERS):
        mod3 = mods[l].reshape(mod_rows, 1, 6 * D_MODEL)
        proj, qv, kt = zip(*[_in_proj(x, mod3, norm_mix_pre[l].reshape(1, -1), w_in_p, layer=l, tm=tm, **ts, **ra)
                             for x, ts, ra in zip(xs, token_sets, rope_args)])

        sink = attn_sink[l]
        ckt = jnp.swapaxes(cache_k[:, l].reshape(nb_lat, -1, KV_WIDTH), 1, 2).astype(BF16)
        cv = cache_v[:, l].reshape(nb_lat, -1, KV_WIDTH).astype(BF16)
        o_att = [_ctx_attention(qv[0], kt[0], sink, n_seq=nb_ctx, seq_len=len_ctx),
                 _lat_attention(qv[1], kt[1], sink, ckt, cv, n_seq=nb_lat, seq_len=len_lat)]

        consts = _ssd_consts(l, ssd_conv_w, ssd_conv_b, ssd_dt_bias, ssd_A_log, ssd_D, ssd_norm)
        y_ctx, hf_ctx, hb_ctx = _ssd_mixer(proj[0], zeros_ssd, zeros_ssd, consts, n_seq=nb_ctx, seq_len=len_ctx)
        y_lat, _, _ = _ssd_mixer(proj[1], _ssd_state_in(state_ssd[:, l, 0]), _ssd_state_in(state_ssd[:, l, 1]),
                                 consts, n_seq=nb_lat, seq_len=len_lat)
        y_ssd = [y_ctx, y_lat]

        s5_ctx, s5_fin = _s5_mixer(proj[0], s5_tab, l, zeros_s5, n_seq=nb_ctx, seq_len=len_ctx)
        s5_lat, _ = _s5_mixer(proj[1], s5_tab, l, _s5_state_in(state_s5_re[:, l], state_s5_im[:, l]),
                              n_seq=nb_lat, seq_len=len_lat)
        y_s5 = [s5_ctx, s5_lat]

        xs = [_out_proj(x, mod3, norm_mix_post[l].reshape(1, -1), y_ssd[i], o_att[i], y_s5[i], proj[i],
                        s5_D[l].reshape(1, -1), w_glu_b, s5_b_glu[l].reshape(1, -1), w_out_b,
                        layer=l, tm=tm, **ts)
              for i, (x, ts) in enumerate(zip(xs, token_sets))]
        xs = [_ffn(x, mod3, norm_ffn_pre[l].reshape(1, -1), norm_ffn_post[l].reshape(1, -1),
                   w_ffn_in_b, w_ffn_out_b, layer=l, tm_up=tm_up, tf=FFN_UP_COLS, tm_down=tm,
                   mod_spec_up=_mod_spec(tm_up, *g), mod_spec_down=_mod_spec(tm, *g))
              for x, g in zip(xs, mod_geom)]

        new_k.append(proj[0][:, C_K:C_K + KV_WIDTH].reshape(nb_ctx, len_ctx, ATT_KV, HEAD_DIM))
        new_v.append(proj[0][:, C_V:C_V + KV_WIDTH].reshape(nb_ctx, len_ctx, ATT_KV, HEAD_DIM))
        new_ssd += [hf_ctx, hb_ctx]
        re, im = _s5_state_out(s5_fin)
        new_s5_re.append(re)
        new_s5_im.append(im)

    y_prompt = xs[0].reshape(nb_ctx, len_ctx, D_MODEL)
    y_sample = xs[1].reshape(nb_lat, len_lat, D_MODEL)
    new_state_ssd = jnp.stack(new_ssd, axis=1).reshape(nb_ctx, N_LAYERS, 2, SSD_HEADS, SSD_HEAD_DIM, SSD_STATE)
    return (y_prompt, y_sample, jnp.stack(new_k, axis=1), jnp.stack(new_v, axis=1), new_state_ssd,
            jnp.stack(new_s5_re, axis=1), jnp.stack(new_s5_im, axis=1))
```

```python
import functools
import math

import jax
import jax.numpy as jnp
import numpy as np
from jax import lax
from jax.experimental import pallas as pl
from jax.experimental.pallas import tpu as pltpu

F32 = jnp.float32
BF16 = jnp.bfloat16

D_MODEL = 2048
N_LAYERS = 2
D_FF = 5632
GRID_W = 64
EPS = 1e-6
LANES = 128

SSD_HEADS = 12
SSD_HEAD_DIM = 64
SSD_WIDTH = SSD_HEADS * SSD_HEAD_DIM
SSD_GROUPS = 2
SSD_STATE = 64
SSD_BC = 2 * SSD_GROUPS * SSD_STATE
SSD_CHUNK = 128
SSD_CHUNKS_PER_STEP = 16
SSD_HALO = 8
ATT_HEADS = 12
ATT_KV = 4
ATT_GROUP = ATT_HEADS // ATT_KV
HEAD_DIM = 64
ATT_WIDTH = ATT_HEADS * HEAD_DIM
KV_WIDTH = ATT_KV * HEAD_DIM
ATT_BLOCK = 128
LAT_Q_BLOCKS = 8
QV_WIDTH = ATT_WIDTH + KV_WIDTH
LOG2E = math.log2(math.e)
Q_SCALE = HEAD_DIM ** -0.5 * LOG2E
ROPE_PER_AXIS = HEAD_DIM // 4
ROPE_BASE = 10000.0
S5_GROUPS = 32
S5_CH = 16
S5_WIDTH = S5_GROUPS * S5_CH
S5_P = 64
S5_T = 32
S5_TW = S5_T * S5_CH
S5_ST = 4 * S5_P
S5_OCT = LANES // S5_CH
S5_MAX_ROWS = 256
MIX_WIDTH = SSD_WIDTH + ATT_WIDTH + S5_WIDTH

O_Z = 0
O_XS = SSD_WIDTH
O_BC = O_XS + SSD_WIDTH
O_DT = O_BC + SSD_BC
O_Q = O_DT + 2 * SSD_HEADS
O_K = O_Q + ATT_WIDTH
O_V = O_K + KV_WIDTH
O_U = O_V + KV_WIDTH
O_END = O_U + S5_WIDTH
C_Q = 0
C_Z = 768
C_XS = 1536
C_BC = 2304
C_K = 2560
C_V = 2816
C_U = 3072
C_DT = 3584
DT_PAD = 128
D_INP = C_DT + DT_PAD

FFN_UP_ROWS = 1024
FFN_UP_COLS = 512

NEG = -1e30
VMEM_LIMIT = 56 * 1024 * 1024


def _cparams(*sem):
    return pltpu.CompilerParams(dimension_semantics=sem, vmem_limit_bytes=VMEM_LIMIT)


def _resident(shape):
    nd = len(shape)
    return pl.BlockSpec(shape, lambda *_: (0,) * nd, pipeline_mode=pl.Buffered(1))


def _layer_resident(shape, layer, block=0):
    rest = (0,) * (len(shape) - 1)
    return pl.BlockSpec((None,) + shape, lambda *_: (layer, block) + rest, pipeline_mode=pl.Buffered(1))


def _rms(x, g):
    return x * lax.rsqrt(jnp.mean(x * x, axis=-1, keepdims=True) + EPS) * g


def _silu(x):
    return x * jax.nn.sigmoid(x)


def _dot(a, b):
    return jnp.dot(a, b, preferred_element_type=F32)


def _dot_wide(a, b):
    a_hi, a_lo = _split_bf16(a, 2)
    b_hi, b_lo = _split_bf16(b, 2)
    return _dot(a_hi, b_hi) + (_dot(a_hi, b_lo) + _dot(a_lo, b_hi))


def _dot_nt(a, b):
    return lax.dot_general(a, b, (((1,), (1,)), ((), ())), preferred_element_type=F32)


def _dot_tn(a, b):
    return lax.dot_general(a, b, (((0,), (0,)), ((), ())), preferred_element_type=F32)


def _split_bf16(x, parts):
    out = []
    for _ in range(parts):
        p = x.astype(BF16)
        out.append(p)
        x = x - p.astype(F32)
    return out


def _dot_sel_rhs(x, sel, parts):
    acc = None
    for p in _split_bf16(x, parts):
        t = _dot(p, sel)
        acc = t if acc is None else acc + t
    return acc


def _dot_sel_lhs(sel, x, parts):
    acc = None
    for p in _split_bf16(x, parts):
        t = _dot(sel, p)
        acc = t if acc is None else acc + t
    return acc


def _mod_spec(tm, rows_per_mod, first_row):
    return pl.BlockSpec((1, 1, 6 * D_MODEL), lambda i, *_: (first_row + (i * tm) // rows_per_mod, 0, 0))


def _ada_kernel(c_ref, w_ref, b_ref, o_ref):
    c = c_ref[...]
    o_ref[0] = _dot(_silu(c).astype(BF16), w_ref[0].astype(BF16)) + b_ref[0]


def _ada(cvecs, w_ada, b_ada):
    tn = 1024
    rows = cvecs.shape[0]
    return pl.pallas_call(
        _ada_kernel,
        grid=(N_LAYERS, 6 * D_MODEL // tn),
        in_specs=[pl.BlockSpec((rows, D_MODEL), lambda l, j: (0, 0)),
                  pl.BlockSpec((1, D_MODEL, tn), lambda l, j: (l, 0, j)),
                  pl.BlockSpec((1, 1, tn), lambda l, j: (l, 0, j))],
        out_specs=pl.BlockSpec((1, rows, tn), lambda l, j: (l, 0, j)),
        out_shape=jax.ShapeDtypeStruct((N_LAYERS, rows, 6 * D_MODEL), F32),
        compiler_params=_cparams("parallel", "parallel"),
        name="ada_mod",
    )(cvecs, w_ada, b_ada.reshape(N_LAYERS, 1, 6 * D_MODEL))


def _rope(x, cos, sin_signed, first_half):
    outs = []
    for j in range(x.shape[1] // LANES):
        xj = x[:, j * LANES:(j + 1) * LANES]
        partner = jnp.where(first_half, pltpu.roll(xj, LANES - HEAD_DIM // 2, axis=1),
                            pltpu.roll(xj, HEAD_DIM // 2, axis=1))
        outs.append(xj * cos + partner * sin_signed)
    return jnp.concatenate(outs, axis=-1)


def _in_proj_kernel(*refs, rope):
    if rope:
        x_ref, mod_ref, g_ref, w_ref, cos_ref, sin_ref, o_ref, qv_ref, kt_ref = refs
    else:
        x_ref, mod_ref, g_ref, w_ref, o_ref, qv_ref, kt_ref = refs
    mod = mod_ref[0]
    sh = mod[:, 0:D_MODEL]
    sc = mod[:, D_MODEL:2 * D_MODEL]
    h = _rms(x_ref[...], g_ref[...]) * (1.0 + sc) + sh
    o_ref[...] = _dot(h.astype(BF16), w_ref[...])
    o_ref[:, C_Z:C_Z + SSD_WIDTH] = _silu(o_ref[:, C_Z:C_Z + SSD_WIDTH])
    q = o_ref[:, C_Q:C_Q + ATT_WIDTH]
    k = o_ref[:, C_K:C_K + KV_WIDTH]
    if rope:
        lane = lax.broadcasted_iota(jnp.int32, (x_ref.shape[0], LANES), 1)
        first_half = (lane % HEAD_DIM) < (HEAD_DIM // 2)
        q = _rope(q, cos_ref[...], sin_ref[...], first_half)
        k = _rope(k, cos_ref[...], sin_ref[...], first_half)
    qv_ref[:, 0:ATT_WIDTH] = (q * Q_SCALE).astype(BF16)
    qv_ref[:, ATT_WIDTH:] = o_ref[:, C_V:C_V + KV_WIDTH].astype(BF16)
    kt_ref[...] = k.T.astype(BF16)


def _in_proj(x, mod3, gamma, w, *, layer, tm, mod_spec, rope_tables=None, seq_len=None):
    n = x.shape[0]
    rope = rope_tables is not None
    tables, table_specs = (), []
    if rope:
        per_seq = seq_len // tm
        tables = tuple(rope_tables)
        table_specs = [pl.BlockSpec((tm, LANES), lambda i: (i % per_seq, 0))] * 2
    return pl.pallas_call(
        functools.partial(_in_proj_kernel, rope=rope),
        grid=(n // tm,),
        in_specs=[pl.BlockSpec((tm, D_MODEL), lambda i: (i, 0)),
                  mod_spec,
                  _resident((1, D_MODEL)),
                  _layer_resident((D_MODEL, D_INP), layer)] + table_specs,
        out_specs=[pl.BlockSpec((tm, D_INP), lambda i: (i, 0)), pl.BlockSpec((tm, QV_WIDTH), lambda i: (i, 0)),
                   pl.BlockSpec((KV_WIDTH, tm), lambda i: (0, i))],
        out_shape=[jax.ShapeDtypeStruct((n, D_INP), F32), jax.ShapeDtypeStruct((n, QV_WIDTH), BF16),
                   jax.ShapeDtypeStruct((KV_WIDTH, n), BF16)],
        compiler_params=_cparams("parallel"),
        name="in_proj",
    )(x, mod3, gamma, w, *tables)


def _gelu_tanh(x):
    return 0.5 * x * (1.0 + jnp.tanh(math.sqrt(2.0 / math.pi) * (x + 0.044715 * (x * x * x))))


def _out_proj_kernel(x_ref, mod_ref, g_ref, yssd_ref, oatt_ref, ys5_ref, u_ref, d_ref, wglu_ref, bglu_ref,
                     w12_ref, w3_ref, o_ref, mix_scr):
    mix_scr[:, 0:SSD_WIDTH] = yssd_ref[...]
    mix_scr[:, SSD_WIDTH:] = oatt_ref[...]
    mix = _dot(mix_scr[...], w12_ref[...])
    y5 = ys5_ref[...] + d_ref[...] * u_ref[...]
    g = _gelu_tanh(y5)
    s5 = g * jax.nn.sigmoid(_dot(g.astype(BF16), wglu_ref[...]) + bglu_ref[...])
    mix = mix + _dot(s5.astype(BF16), w3_ref[...])
    gate = mod_ref[0][:, 2 * D_MODEL:3 * D_MODEL]
    o_ref[...] = x_ref[...] + gate * _rms(mix, g_ref[...])


def _out_proj(x, mod3, gamma, y_ssd, o_att, y_s5, proj, s5_d, w_glu, b_glu, w_out, *, layer, tm, mod_spec):
    n = x.shape[0]
    row = lambda i: (i, 0)
    return pl.pallas_call(
        _out_proj_kernel,
        grid=(n // tm,),
        in_specs=[pl.BlockSpec((tm, D_MODEL), row),
                  mod_spec,
                  _resident((1, D_MODEL)),
                  pl.BlockSpec((tm, SSD_WIDTH), row),
                  pl.BlockSpec((tm, ATT_WIDTH), row),
                  pl.BlockSpec((tm, S5_WIDTH), row),
                  pl.BlockSpec((tm, S5_WIDTH), lambda i: (i, C_U // S5_WIDTH)),
                  _resident((1, S5_WIDTH)),
                  _layer_resident((S5_WIDTH, S5_WIDTH), layer),
                  _resident((1, S5_WIDTH)),
                  _layer_resident((SSD_WIDTH + ATT_WIDTH, D_MODEL), layer, 0),
                  _layer_resident((S5_WIDTH, D_MODEL), layer, (SSD_WIDTH + ATT_WIDTH) // S5_WIDTH)],
        out_specs=pl.BlockSpec((tm, D_MODEL), row),
        out_shape=jax.ShapeDtypeStruct((n, D_MODEL), F32),
        scratch_shapes=[pltpu.VMEM((tm, SSD_WIDTH + ATT_WIDTH), BF16)],
        compiler_params=_cparams("parallel"),
        name="out_proj",
    )(x, mod3, gamma, y_ssd, o_att, y_s5, proj, s5_d, w_glu, b_glu, w_out, w_out)


def _ffn_up_kernel(x_ref, mod_ref, gpre_ref, wg_ref, wu_ref, act_ref, h_scr):
    j = pl.program_id(1)

    def gated(h):
        return (_silu(_dot(h, wg_ref[...])) * _dot(h, wu_ref[...])).astype(BF16)

    @pl.when(j == 0)
    def _():
        mod = mod_ref[0]
        sh = mod[:, 3 * D_MODEL:4 * D_MODEL]
        sc = mod[:, 4 * D_MODEL:5 * D_MODEL]
        h = (_rms(x_ref[...], gpre_ref[...]) * (1.0 + sc) + sh).astype(BF16)
        h_scr[...] = h
        act_ref[...] = gated(h)

    @pl.when(j > 0)
    def _():
        act_ref[...] = gated(h_scr[...])


def _ffn_down_kernel(x_ref, mod_ref, gpost_ref, act_ref, wo_ref, o_ref):
    gate = mod_ref[0][:, 5 * D_MODEL:6 * D_MODEL]
    o_ref[...] = x_ref[...] + gate * _rms(_dot(act_ref[...], wo_ref[...]), gpost_ref[...])


def _ffn(x, mod3, g_pre, g_post, w_in, w_out, *, layer, tm_up, tf, tm_down, mod_spec_up, mod_spec_down):
    n = x.shape[0]
    nf = D_FF // tf
    act = pl.pallas_call(
        _ffn_up_kernel,
        grid=(n // tm_up, nf),
        in_specs=[pl.BlockSpec((tm_up, D_MODEL), lambda i, j: (i, 0)),
                  mod_spec_up,
                  _resident((1, D_MODEL)),
                  pl.BlockSpec((None, D_MODEL, tf), lambda i, j: (layer, 0, j)),
                  pl.BlockSpec((None, D_MODEL, tf), lambda i, j: (layer, 0, j + nf))],
        out_specs=pl.BlockSpec((tm_up, tf), lambda i, j: (i, j)),
        out_shape=jax.ShapeDtypeStruct((n, D_FF), BF16),
        scratch_shapes=[pltpu.VMEM((tm_up, D_MODEL), BF16)],
        compiler_params=_cparams("parallel", "arbitrary"),
        name="ffn_up",
    )(x, mod3, g_pre, w_in, w_in)
    return pl.pallas_call(
        _ffn_down_kernel,
        grid=(n // tm_down,),
        in_specs=[pl.BlockSpec((tm_down, D_MODEL), lambda i: (i, 0)),
                  mod_spec_down,
                  _resident((1, D_MODEL)),
                  pl.BlockSpec((tm_down, D_FF), lambda i: (i, 0)),
                  _layer_resident((D_FF, D_MODEL), layer)],
        out_specs=pl.BlockSpec((tm_down, D_MODEL), lambda i: (i, 0)),
        out_shape=jax.ShapeDtypeStruct((n, D_MODEL), F32),
        compiler_params=_cparams("parallel"),
        name="ffn_down",
    )(x, mod3, g_post, act, w_out)


def _attend(queries, sink_ref):
    rows = queries[0][0].shape[0]
    srow = lax.broadcasted_iota(jnp.int32, (ATT_GROUP * rows, 1), 0)
    outs = [[None] * ATT_HEADS for _ in queries]
    units = [(i, kv) for i in range(len(queries)) for kv in range(ATT_KV)]

    def masked_scores(unit):
        (q, parts), kv = queries[unit[0]], unit[1]
        heads = range(kv * ATT_GROUP, (kv + 1) * ATT_GROUP)
        q3 = jnp.concatenate([q[:, h * HEAD_DIM:(h + 1) * HEAD_DIM] for h in heads], axis=0)
        scores = []
        for kt, _, mask in parts:
            s = _dot(q3, kt[kv * HEAD_DIM:(kv + 1) * HEAD_DIM, :])
            scores.append(s if mask is None else jnp.where(mask, s, NEG))
        return scores

    def softmax_numerator(unit, scores):
        kv = unit[1]
        heads = range(kv * ATT_GROUP, (kv + 1) * ATT_GROUP)
        sink = jnp.full((ATT_GROUP * rows, 1), sink_ref[heads[-1]] * LOG2E, F32)
        for i in range(ATT_GROUP - 2, -1, -1):
            sink = jnp.where(srow < (i + 1) * rows, sink_ref[heads[i]] * LOG2E, sink)
        blocks = [s[:, j:j + LANES] for s in scores for j in range(0, s.shape[1], LANES)]
        m = jnp.maximum(sink, jnp.max(functools.reduce(jnp.maximum, blocks), axis=-1, keepdims=True))
        p = jnp.concatenate([jnp.exp2(s - m).astype(BF16) for s in scores], axis=1)
        return p, jnp.exp2(sink - m)

    def weighted_values(unit, p, sink_term):
        (_, parts), kv = queries[unit[0]], unit[1]
        lo = kv * HEAD_DIM
        vlo = (lo // LANES) * LANES
        v_first = lo == vlo
        v128 = jnp.concatenate([v[:, vlo:vlo + LANES] for _, v, _ in parts], axis=0)
        lane = lax.broadcasted_iota(jnp.int32, v128.shape, 1)
        keep = (lane < HEAD_DIM) if v_first else (lane >= HEAD_DIM)
        acc = _dot(p, jnp.where(keep, v128, jnp.ones_like(v128)))
        ocol, dcol = (0, HEAD_DIM) if v_first else (HEAD_DIM, 0)
        o3 = acc[:, ocol:ocol + HEAD_DIM] / (acc[:, dcol:dcol + 1] + sink_term)
        for i in range(ATT_GROUP):
            outs[unit[0]][kv * ATT_GROUP + i] = o3[i * rows:(i + 1) * rows]

    scores = {0: masked_scores(units[0])}
    probs = {}
    for step in range(len(units) + 2):
        if step + 1 < len(units):
            scores[step + 1] = masked_scores(units[step + 1])
        if step - 1 in probs:
            weighted_values(units[step - 1], *probs.pop(step - 1))
        if step in scores:
            probs[step] = softmax_numerator(units[step], scores.pop(step))
    return [jnp.concatenate(o, axis=-1) for o in outs]


def _ctx_attn_kernel(sink_ref, q_ref, kt_ref, v_ref, o_ref):
    def scores(h):
        lo = (h // ATT_GROUP) * HEAD_DIM
        return _dot(q_ref[:, h * HEAD_DIM:(h + 1) * HEAD_DIM], kt_ref[lo:lo + HEAD_DIM, :])

    outs = []
    ahead = scores(0)
    for h in range(ATT_HEADS):
        s = ahead
        if h + 1 < ATT_HEADS:
            ahead = scores(h + 1)
        lo = (h // ATT_GROUP) * HEAD_DIM
        sink = sink_ref[h] * LOG2E
        m = jnp.maximum(jnp.max(s, axis=-1, keepdims=True), sink)
        p = jnp.exp2(s - m)
        den = jnp.exp2(sink - m) + jnp.sum(p, axis=-1, keepdims=True)
        outs.append(_dot(p.astype(BF16), v_ref[:, lo:lo + HEAD_DIM]) / den)
    o_ref[...] = jnp.concatenate(outs, axis=-1).astype(o_ref.dtype)


def _ctx_attention(qv, kt, sink, *, n_seq, seq_len):
    vcol = ATT_WIDTH // KV_WIDTH
    return pl.pallas_call(
        _ctx_attn_kernel,
        grid=(n_seq,),
        in_specs=[pl.BlockSpec(memory_space=pltpu.SMEM),
                  pl.BlockSpec((seq_len, ATT_WIDTH), lambda b: (b, 0)),
                  pl.BlockSpec((KV_WIDTH, seq_len), lambda b: (0, b)),
                  pl.BlockSpec((seq_len, KV_WIDTH), lambda b: (b, vcol))],
        out_specs=pl.BlockSpec((seq_len, ATT_WIDTH), lambda b: (b, 0)),
        out_shape=jax.ShapeDtypeStruct((n_seq * seq_len, ATT_WIDTH), BF16),
        compiler_params=_cparams("parallel"),
        name="ctx_attention",
    )(sink, qv, kt, qv)


def _lat_attn_kernel(sink_ref, q_ref, *refs, nq):
    kt_refs, v_refs = refs[0:nq + 2], refs[nq + 2:2 * nq + 4]
    ck_ref, cv_ref, o_ref = refs[2 * nq + 4:]
    step = pl.program_id(1)
    blk = ATT_BLOCK
    row = lax.broadcasted_iota(jnp.int32, (ATT_GROUP * blk, blk), 0) % blk
    col = lax.broadcasted_iota(jnp.int32, (ATT_GROUP * blk, blk), 1)
    before = [col >= row] * nq
    after = [col <= row] * nq
    before[0] = jnp.logical_and(before[0], step > 0)
    after[-1] = jnp.logical_and(after[-1], step < pl.num_programs(1) - 1)
    ctx = (ck_ref[0], cv_ref[0], None)
    blocks = [(kt[...], v[...]) for kt, v in zip(kt_refs, v_refs)]
    queries = [(q_ref[i * blk:(i + 1) * blk, :],
                [ctx, blocks[i] + (before[i],), blocks[i + 1] + (None,), blocks[i + 2] + (after[i],)])
               for i in range(nq)]
    for i, o in enumerate(_attend(queries, sink_ref)):
        o_ref[i * blk:(i + 1) * blk, :] = o.astype(o_ref.dtype)


def _lat_attention(qv, kt, sink, ckt, cv, *, n_seq, seq_len):
    blk = ATT_BLOCK
    nb = seq_len // blk
    nq = max(c for c in range(1, LAT_Q_BLOCKS + 1) if nb % c == 0)
    steps = nb // nq

    def block(offset):
        return lambda b, n: b * nb + jnp.clip(nq * n + offset, 0, nb - 1)

    vcol = ATT_WIDTH // KV_WIDTH
    offsets = range(-1, nq + 1)
    kt_specs = [pl.BlockSpec((KV_WIDTH, blk), lambda b, n, f=block(o): (0, f(b, n))) for o in offsets]
    v_specs = [pl.BlockSpec((blk, KV_WIDTH), lambda b, n, f=block(o): (f(b, n), vcol)) for o in offsets]
    return pl.pallas_call(
        functools.partial(_lat_attn_kernel, nq=nq),
        grid=(n_seq, steps),
        in_specs=[pl.BlockSpec(memory_space=pltpu.SMEM),
                  pl.BlockSpec((nq * blk, ATT_WIDTH), lambda b, n: (b * steps + n, 0))] + kt_specs + v_specs
                 + [pl.BlockSpec((1,) + ckt.shape[1:], lambda b, n: (b, 0, 0)),
                    pl.BlockSpec((1,) + cv.shape[1:], lambda b, n: (b, 0, 0))],
        out_specs=pl.BlockSpec((nq * blk, ATT_WIDTH), lambda b, n: (b * steps + n, 0)),
        out_shape=jax.ShapeDtypeStruct((n_seq * seq_len, ATT_WIDTH), BF16),
        compiler_params=_cparams("parallel", "parallel"),
        name="lat_attention",
    )(sink, qv, *([kt] * (nq + 2)), *([qv] * (nq + 2)), ckt, cv)


def _rope_tables(seq_len):
    rows = seq_len // GRID_W
    row = np.repeat(np.arange(rows, dtype=np.float32), GRID_W)
    col = np.tile(np.arange(GRID_W, dtype=np.float32), rows)
    inv = np.float32(ROPE_BASE) ** (-np.arange(ROPE_PER_AXIS, dtype=np.float32) / np.float32(ROPE_PER_AXIS))
    ang = np.concatenate([row[:, None] * inv, col[:, None] * inv], axis=-1)
    cos, sin = np.cos(ang), np.sin(ang)
    cos128 = np.tile(cos, (1, 4))
    sin128 = np.tile(np.concatenate([-sin, sin], axis=-1), (1, 2))
    return jnp.asarray(cos128, F32), jnp.asarray(sin128, F32)


def _conv_silu(scr, prev_ref, cur_ref, next_ref, w_ref, b_ref, at_start, at_end):
    n = cur_ref.shape[0]
    h = SSD_HALO
    scr[0:h, :] = jnp.where(at_start, 0.0, prev_ref[...])
    scr[h:h + n, :] = cur_ref[...]
    scr[h + n:h + n + h, :] = jnp.where(at_end, 0.0, next_ref[...])
    return _silu(scr[h - 1:h - 1 + n, :] * w_ref[0:1, :] + scr[h:h + n, :] * w_ref[1:2, :]
                 + scr[h + 1:h + 1 + n, :] * w_ref[2:3, :] + b_ref[...])


def _softplus(x):
    return jnp.maximum(x, 0.0) + jnp.log1p(jnp.exp(-jnp.abs(x)))


def _ssd_bwd_kernel(xsp_ref, xsc_ref, xsn_ref, bcp_ref, bcc_ref, bcn_ref, dt_ref, h0_ref,
                    cwx_ref, cbx_ref, cwb_ref, cbb_ref, dtb_ref, alog_ref, tri_ref, exp_ref,
                    hstart_ref, hfin_ref, xbc_ref, xs_scr, bc_scr, h_scr):
    i = pl.program_id(1)
    q = SSD_CHUNK
    half = SSD_WIDTH // SSD_GROUPS
    sel_b = exp_ref[1]
    neg_a = -LOG2E * jnp.exp(alog_ref[...])
    at_start = i == pl.num_programs(1) - 1
    at_end = i == 0
    xbc_ref[:, 0:SSD_WIDTH] = _conv_silu(xs_scr, xsp_ref, xsc_ref, xsn_ref, cwx_ref, cbx_ref, at_start, at_end)
    xbc_ref[:, SSD_WIDTH:] = _conv_silu(bc_scr, bcp_ref, bcc_ref, bcn_ref, cwb_ref, cbb_ref, at_start, at_end)

    @pl.when(i == 0)
    def _():
        h_scr[...] = h0_ref[0]

    hb = h_scr[...]
    for j in reversed(range(xsc_ref.shape[0] // q)):
        rows = slice(j * q, (j + 1) * q)
        hstart_ref[0, j] = hb
        bm = xbc_ref[rows, SSD_WIDTH:SSD_WIDTH + SSD_GROUPS * SSD_STATE].astype(BF16)
        dtv = _softplus(dt_ref[rows, :] + dtb_ref[...])
        rev = _dot_sel_lhs(tri_ref[1], dtv * neg_a, 3)
        dt_w = _dot_sel_rhs(dtv, sel_b, 2)
        to_end = _dot_sel_rhs(jnp.exp2(rev[0:1, :] - rev), sel_b, 2)
        decay = _dot_sel_rhs(jnp.exp2(rev[0:SSD_HALO, :]), sel_b, 2)[0:1, :]
        xw = (xbc_ref[rows, 0:SSD_WIDTH] * dt_w * to_end).astype(BF16)
        upd = jnp.concatenate(
            [_dot_tn(bm[:, g * SSD_STATE:(g + 1) * SSD_STATE], xw[:, g * half:(g + 1) * half])
             for g in range(SSD_GROUPS)], axis=-1)
        hb = hb * decay + upd
    h_scr[...] = hb

    @pl.when(i == pl.num_programs(1) - 1)
    def _():
        hfin_ref[0] = hb.T


def _ssd_intra(xs, bc, dt_raw, dtb_ref, alog_ref, tri_ref, exp_ref, dvec_ref):
    q = SSD_CHUNK
    nh = SSD_HEADS
    gs = SSD_GROUPS * SSD_STATE
    bm = bc[:, 0:gs].astype(BF16)
    cm = bc[:, gs:2 * gs].astype(BF16)
    xs_bf = xs.astype(BF16)

    dtv = _softplus(dt_raw + dtb_ref[...])
    dta = dtv * (-LOG2E * jnp.exp(alog_ref[...]))
    lower, upper = tri_ref[0], tri_ref[1]
    cum = _dot_sel_lhs(lower, dta, 3)
    rev = _dot_sel_lhs(upper, dta, 3)
    dt_t = dtv.T
    dta_t = dta.T
    cum_t = _dot_sel_rhs(dta_t, upper, 3)
    rev_t = _dot_sel_rhs(dta_t, lower, 3)

    sel_f, sel_b = exp_ref[0], exp_ref[1]
    dtf_w = _dot_sel_rhs(dtv, sel_f, 2)
    ecum_w = _dot_sel_rhs(jnp.exp2(cum), sel_f, 2)
    erev_w = _dot_sel_rhs(jnp.exp2(rev), sel_b, 2)
    toend_w = _dot_sel_rhs(jnp.exp2(cum[q - 1:q, :] - cum), sel_f, 2)

    row = lax.broadcasted_iota(jnp.int32, (q, q), 0)
    col = lax.broadcasted_iota(jnp.int32, (q, q), 1)
    below = row > col
    log_dt_t = jnp.log2(dt_t)
    src_f = log_dt_t - cum_t
    src_b = log_dt_t - rev_t
    gmat = [_dot_nt(cm[:, g * SSD_STATE:(g + 1) * SSD_STATE], bm[:, g * SSD_STATE:(g + 1) * SSD_STATE])
            for g in range(SSD_GROUPS)]
    per_group = nh // SSD_GROUPS
    half = SSD_WIDTH // SSD_GROUPS
    ys = []
    for h in range(nh):
        e = jnp.where(below, cum[:, h:h + 1] + src_f[h:h + 1, :],
                      rev[:, nh + h:nh + h + 1] + src_b[nh + h:nh + h + 1, :])
        m = (gmat[h // per_group] * jnp.exp2(e)).astype(BF16)
        ys.append(_dot(m, xs_bf[:, h * SSD_HEAD_DIM:(h + 1) * SSD_HEAD_DIM]))
    cb = cm.astype(F32) * bm.astype(F32)
    g_diag = jnp.concatenate(
        [jnp.broadcast_to(jnp.sum(cb[:, g * SSD_STATE:(g + 1) * SSD_STATE], axis=-1, keepdims=True), (q, half))
         for g in range(SSD_GROUPS)], axis=-1)
    y = jnp.concatenate(ys, axis=-1) + (g_diag * dtf_w + dvec_ref[...]) * xs
    xw = (xs * dtf_w * toend_w).astype(BF16)
    upd = jnp.concatenate(
        [_dot_tn(bm[:, g * SSD_STATE:(g + 1) * SSD_STATE], xw[:, g * half:(g + 1) * half])
         for g in range(SSD_GROUPS)], axis=-1)
    return y, cm, ecum_w, erev_w, upd


def _ssd_carry(hf, hb_start, gz, intra, nw_ref):
    y, cm, ecum_w, erev_w, upd = intra
    half = SSD_WIDTH // SSD_GROUPS
    hf_bf = hf.astype(BF16)
    hb_bf = hb_start.astype(BF16)
    off_f = jnp.concatenate([_dot(cm[:, g * SSD_STATE:(g + 1) * SSD_STATE], hf_bf[:, g * half:(g + 1) * half])
                             for g in range(SSD_GROUPS)], axis=-1)
    off_b = jnp.concatenate([_dot(cm[:, g * SSD_STATE:(g + 1) * SSD_STATE], hb_bf[:, g * half:(g + 1) * half])
                             for g in range(SSD_GROUPS)], axis=-1)
    y = y + off_f * ecum_w + off_b * erev_w
    y = _rms(y * gz, nw_ref[...])
    return y, hf * ecum_w[SSD_CHUNK - 1:SSD_CHUNK, :] + upd


def _ssd_fwd_kernel(gz_ref, xs_ref, bc_ref, dt_ref, hb_ref, h0_ref,
                    dtb_ref, alog_ref, tri_ref, exp_ref, dvec_ref, nw_ref, y_ref, hfin_ref, h_scr):
    c = pl.program_id(1)
    q = SSD_CHUNK
    cps = xs_ref.shape[0] // q
    rows = [slice(j * q, (j + 1) * q) for j in range(cps)]

    def intra(j):
        return _ssd_intra(xs_ref[rows[j], :], bc_ref[rows[j], :], dt_ref[rows[j], :],
                          dtb_ref, alog_ref, tri_ref, exp_ref, dvec_ref)

    @pl.when(c == 0)
    def _():
        h_scr[...] = h0_ref[0]

    hf = h_scr[...]
    ahead = intra(0)
    for j in range(cps):
        cur = ahead
        if j + 1 < cps:
            ahead = intra(j + 1)
        y, hf = _ssd_carry(hf, hb_ref[0, j], gz_ref[rows[j], :], cur, nw_ref)
        y_ref[rows[j], :] = y.astype(y_ref.dtype)
    h_scr[...] = hf

    @pl.when(c == pl.num_programs(1) - 1)
    def _():
        hfin_ref[0] = hf.T


def _ssd_mixer(proj, h0_f, h0_b, consts, *, n_seq, seq_len):
    q = SSD_CHUNK
    nc = seq_len // q
    cps = max(c for c in range(1, SSD_CHUNKS_PER_STEP + 1) if nc % c == 0)
    ns = nc // cps
    rows = cps * q
    (cwx, cbx, cwb, cbb, dt_bias, a_log, tri, expand, dvec, nw) = consts
    per8 = rows // SSD_HALO
    last8 = proj.shape[0] // SSD_HALO - 1

    def raw_specs(width, col):
        step = lambda b, i: b * ns + ns - 1 - i
        return [pl.BlockSpec((SSD_HALO, width), lambda b, i: (jnp.maximum(step(b, i) * per8 - 1, 0), col)),
                pl.BlockSpec((rows, width), lambda b, i: (step(b, i), col)),
                pl.BlockSpec((SSD_HALO, width), lambda b, i: (jnp.minimum((step(b, i) + 1) * per8, last8), col))]

    def specs(chunk_of):
        def cur(b, i):
            return b * ns + chunk_of(i)

        return [pl.BlockSpec((rows, SSD_WIDTH), lambda b, i: (cur(b, i), 0)),
                pl.BlockSpec((rows, SSD_BC), lambda b, i: (cur(b, i), SSD_WIDTH // SSD_BC)),
                pl.BlockSpec((rows, DT_PAD), lambda b, i: (cur(b, i), C_DT // DT_PAD))], cur

    state_spec = pl.BlockSpec((1, SSD_STATE, SSD_WIDTH), lambda b, i: (b, 0, 0))
    final_spec = pl.BlockSpec((1, SSD_WIDTH, SSD_STATE), lambda b, i: (b, 0, 0))
    const_specs = [_resident(dt_bias.shape), _resident(a_log.shape), _resident(tri.shape),
                   _resident(expand.shape)]
    scratch = [pltpu.VMEM((SSD_STATE, SSD_WIDTH), F32)]

    data_specs, rcur = specs(lambda i: ns - 1 - i)
    hb_start, hb_fin, xbc = pl.pallas_call(
        _ssd_bwd_kernel,
        grid=(n_seq, ns),
        in_specs=(raw_specs(SSD_WIDTH, C_XS // SSD_WIDTH) + raw_specs(SSD_BC, C_BC // SSD_BC)
                  + [data_specs[2], state_spec]
                  + [_resident(cwx.shape), _resident(cbx.shape), _resident(cwb.shape), _resident(cbb.shape)]
                  + const_specs),
        out_specs=[pl.BlockSpec((1, cps, SSD_STATE, SSD_WIDTH), lambda b, i: (b, ns - 1 - i, 0, 0)), final_spec,
                   pl.BlockSpec((rows, SSD_WIDTH + SSD_BC), lambda b, i: (rcur(b, i), 0))],
        out_shape=[jax.ShapeDtypeStruct((n_seq, nc, SSD_STATE, SSD_WIDTH), F32),
                   jax.ShapeDtypeStruct((n_seq, SSD_WIDTH, SSD_STATE), F32),
                   jax.ShapeDtypeStruct((n_seq * seq_len, SSD_WIDTH + SSD_BC), F32)],
        scratch_shapes=[pltpu.VMEM((rows + 2 * SSD_HALO, SSD_WIDTH), F32),
                        pltpu.VMEM((rows + 2 * SSD_HALO, SSD_BC), F32)] + scratch,
        compiler_params=_cparams("parallel", "arbitrary"),
        name="ssd_backward_states",
    )(proj, proj, proj, proj, proj, proj, proj, h0_b, cwx, cbx, cwb, cbb, dt_bias, a_log, tri, expand)

    data_specs, cur = specs(lambda i: i)
    y, hf_fin = pl.pallas_call(
        _ssd_fwd_kernel,
        grid=(n_seq, ns),
        in_specs=([pl.BlockSpec((rows, SSD_WIDTH), lambda b, i: (cur(b, i), C_Z // SSD_WIDTH))] + data_specs
                  + [pl.BlockSpec((1, cps, SSD_STATE, SSD_WIDTH), lambda b, i: (b, i, 0, 0)), state_spec]
                  + const_specs + [_resident(dvec.shape), _resident(nw.shape)]),
        out_specs=[pl.BlockSpec((rows, SSD_WIDTH), lambda b, i: (b * ns + i, 0)), final_spec],
        out_shape=[jax.ShapeDtypeStruct((n_seq * seq_len, SSD_WIDTH), BF16),
                   jax.ShapeDtypeStruct((n_seq, SSD_WIDTH, SSD_STATE), F32)],
        scratch_shapes=scratch,
        compiler_params=_cparams("parallel", "arbitrary"),
        name="ssd_forward",
    )(proj, xbc, xbc, proj, hb_start, h0_f, dt_bias, a_log, tri, expand, dvec, nw)
    return y, hf_fin, hb_fin


def _ssd_consts(l, ssd_conv_w, ssd_conv_b, ssd_dt_bias, ssd_A_log, ssd_D, ssd_norm):
    q = SSD_CHUNK
    cw, cb = ssd_conv_w[l], ssd_conv_b[l]
    pad = DT_PAD - 2 * SSD_HEADS
    dt_bias = jnp.pad(ssd_dt_bias[l].reshape(1, -1), ((0, 0), (0, pad)))
    a_log = jnp.pad(ssd_A_log[l].reshape(1, -1), ((0, 0), (0, pad)))
    r = np.arange(q)
    tri = jnp.asarray(np.stack([r[None, :] <= r[:, None], r[None, :] >= r[:, None]]), BF16)
    head_of_col = np.arange(SSD_WIDTH) // SSD_HEAD_DIM
    j = np.arange(DT_PAD)
    expand = jnp.asarray(np.stack([j[:, None] == head_of_col[None, :],
                                   j[:, None] == head_of_col[None, :] + SSD_HEADS]), BF16)
    dvec = jnp.repeat(ssd_D[l], SSD_HEAD_DIM).reshape(1, SSD_WIDTH)
    return (cw[:, :SSD_WIDTH], cb[:SSD_WIDTH].reshape(1, -1), cw[:, SSD_WIDTH:], cb[SSD_WIDTH:].reshape(1, -1),
            dt_bias, a_log, tri, expand, dvec, ssd_norm[l].reshape(1, -1))


def _ssd_state_in(s):
    b = s.shape[0]
    return jnp.transpose(s, (0, 3, 1, 2)).reshape(b, SSD_STATE, SSD_WIDTH)


def _s5_table_kernel(pw_ref, ct_ref, bb_ref, bbt_ref, sel_ref, tile_ref, rhs_ref, woff_ref):
    t = S5_T
    c = S5_CH
    tile = tile_ref[...]
    spread = lambda x, sel: _dot_sel_rhs(x, sel, 2)
    krows, offs, cols = [], [], []
    for d in range(2):
        pw_re, pw_im = pw_ref[0, 2 * d], pw_ref[0, 2 * d + 1]
        c_re = spread(ct_ref[0, :, (2 * d) * c:(2 * d + 1) * c], tile)
        c_im = spread(ct_ref[0, :, (2 * d + 1) * c:(2 * d + 2) * c], tile)
        b_re = spread(bb_ref[0, :, (2 * d) * c:(2 * d + 1) * c], tile)
        b_im = spread(bb_ref[0, :, (2 * d + 1) * c:(2 * d + 2) * c], tile)
        bt_re = bbt_ref[0, (2 * d) * c:(2 * d + 1) * c, :]
        bt_im = bbt_ref[0, (2 * d + 1) * c:(2 * d + 2) * c, :]
        up = (spread(pw_re, sel_ref[0]), spread(pw_im, sel_ref[0]))
        down = (spread(pw_re, sel_ref[1]), spread(pw_im, sel_ref[1]))
        (q_re, q_im), state = ((up, down), (down, up))[d]
        g_re = q_re * c_re - q_im * c_im
        g_im = q_re * c_im + q_im * c_re
        krows.append(_dot_wide(bt_re, g_re) - _dot_wide(bt_im, g_im))
        lb_re, lb_im = pw_re[:, 1:2], pw_im[:, 1:2]
        r_re = q_re * lb_re - q_im * lb_im
        r_im = q_re * lb_im + q_im * lb_re
        offs.append((r_re * c_re - r_im * c_im, -(r_re * c_im + r_im * c_re)))
        s_re, s_im = state
        cols.append(((s_re * b_re - s_im * b_im).T, (s_re * b_im + s_im * b_re).T))
    k_f, k_b = krows
    for s in range(t):
        r = t - 1 - s
        f = k_f if s == 0 else jnp.concatenate([jnp.zeros((c, c * s), F32), k_f[:, :S5_TW - c * s]], axis=1)
        b = k_b if r == 0 else jnp.concatenate([k_b[:, c * r:], jnp.zeros((c, c * r), F32)], axis=1)
        rhs_ref[0, s * c:(s + 1) * c, 0:S5_TW] = (f + b).astype(BF16)
    rhs_ref[0, :, S5_TW:] = jnp.concatenate([cols[0][0], cols[1][0], cols[0][1], cols[1][1]], axis=1).astype(BF16)
    woff_ref[0] = jnp.concatenate([offs[0][0], offs[1][0], offs[0][1], offs[1][1]], axis=0).astype(BF16)


def _s5_tables(s5_A_re, s5_A_im, s5_log_dt, s5_B_re, s5_B_im, s5_C_re, s5_C_im):
    t = S5_T
    n = N_LAYERS * S5_GROUPS
    step = jnp.exp(s5_log_dt)[..., None]
    k = jnp.arange(LANES, dtype=F32)
    keep = k <= t
    kk = jnp.where(keep, k, 0.0)
    mag = jnp.exp(kk * (s5_A_re * step)[..., None])
    ang = kk * (s5_A_im * step)[..., None]
    pw_re = jnp.where(keep, mag * jnp.cos(ang), 0.0)
    pw_im = jnp.where(keep, mag * jnp.sin(ang), 0.0)
    lb_re, lb_im = pw_re[..., 1], pw_im[..., 1]
    den = s5_A_re * s5_A_re + s5_A_im * s5_A_im
    r_re = ((lb_re - 1.0) * s5_A_re + lb_im * s5_A_im) / den
    r_im = (lb_im * s5_A_re - (lb_re - 1.0) * s5_A_im) / den
    b_re, b_im = s5_B_re[:, None], s5_B_im[:, None]
    bb_re = r_re[..., None] * b_re - r_im[..., None] * b_im
    bb_im = r_re[..., None] * b_im + r_im[..., None] * b_re
    c_re = jnp.swapaxes(s5_C_re, -1, -2)
    c_im = jnp.swapaxes(s5_C_im, -1, -2)

    def pack(re, im, axis):
        return jnp.concatenate([re[:, 0], im[:, 0], re[:, 1], im[:, 1]], axis=axis)

    pw = jnp.stack([pw_re[:, 0], pw_im[:, 0], pw_re[:, 1], pw_im[:, 1]], axis=2).reshape(n, 4, S5_P, LANES)
    ct = pack(c_re, c_im, -1).reshape(n, S5_P, 4 * S5_CH)
    bb = pack(bb_re, bb_im, -1)
    bbt = jnp.swapaxes(bb, -1, -2).reshape(n, 4 * S5_CH, S5_P)
    bb = bb.reshape(n, S5_P, 4 * S5_CH)
    lam_t = jnp.concatenate([pw_re[:, 0, :, :, t], pw_re[:, 1, :, :, t], pw_im[:, 0, :, :, t],
                             pw_im[:, 1, :, :, t]], axis=-1).reshape(n, 1, S5_ST)
    jt = np.arange(S5_TW) // S5_CH
    lane_k = np.arange(LANES)[:, None]
    sel = jnp.asarray(np.stack([lane_k == f[None, :] for f in (jt, t - 1 - jt)]), BF16)
    tile = jnp.asarray(np.arange(S5_CH)[:, None] == (np.arange(S5_TW) % S5_CH)[None, :], BF16)
    grp = lambda shape: pl.BlockSpec((1,) + shape, lambda i: (i,) + (0,) * len(shape))
    rhs, woff = pl.pallas_call(
        _s5_table_kernel,
        grid=(n,),
        in_specs=[grp((4, S5_P, LANES)), grp((S5_P, 4 * S5_CH)), grp((S5_P, 4 * S5_CH)), grp((4 * S5_CH, S5_P)),
                  _resident(sel.shape), _resident(tile.shape)],
        out_specs=[grp((S5_TW, S5_TW + S5_ST)), grp((S5_ST, S5_TW))],
        out_shape=[jax.ShapeDtypeStruct((n, S5_TW, S5_TW + S5_ST), BF16),
                   jax.ShapeDtypeStruct((n, S5_ST, S5_TW), BF16)],
        compiler_params=_cparams("parallel"),
        name="s5_tables",
    )(pw, ct, bb, bbt, sel, tile)
    return rhs, woff, lam_t


def _unit_transpose(vs, masks, axis, unit):
    size = vs[0].shape[axis]
    cur = list(vs)
    for d in (1, 2, 4):
        nxt = list(cur)
        for a in range(S5_OCT):
            if a & d:
                continue
            b = a | d
            nxt[a] = jnp.where(masks[d], cur[a], pltpu.roll(cur[b], unit * d, axis=axis))
            nxt[b] = jnp.where(masks[d], pltpu.roll(cur[a], size - unit * d, axis=axis), cur[b])
        cur = nxt
    return cur


def _s5_kernel(u_ref, rhs_ref, woff_ref, lam_ref, h0_ref, y_ref, hfin_ref,
               dst_scr, y_scr, sre_scr, sim_scr, hfre_scr, hfim_scr, hbre_scr, hbim_scr, *, nc, rb):
    t = S5_T
    rows = dst_scr.shape[1]
    sb = rows // nc
    tiles = t // S5_OCT
    nr = rb // S5_OCT
    tile_shape = (nr, S5_OCT, LANES)
    seg = lax.broadcasted_iota(jnp.int32, tile_shape, 2) // S5_CH
    sub = lax.broadcasted_iota(jnp.int32, tile_shape, 1)
    lane_masks = {d: (seg & d) == 0 for d in (1, 2, 4)}
    sub_masks = {d: (sub & d) == 0 for d in (1, 2, 4)}

    def tile_rows(i, r, tq):
        return pl.ds((i * rb + r) * tiles + tq, nr, stride=S5_OCT * tiles)

    def gather(i, carry):
        r0 = pl.multiple_of(i * rb, rb)
        for tq in range(tiles):
            by_chunk = [u_ref[tile_rows(i, r, tq)] for r in range(S5_OCT)]
            by_time = _unit_transpose(by_chunk, sub_masks, 1, 1)
            for gl, v in enumerate(_unit_transpose(by_time, lane_masks, 2, S5_CH)):
                dst_scr[gl, pl.ds(r0, rb), tq * LANES:(tq + 1) * LANES] = v.reshape(rb, LANES).astype(BF16)
        return carry

    lax.fori_loop(0, rows // rb, gather, 0)

    for gl in range(S5_OCT):
        z = _dot(dst_scr[gl], rhs_ref[gl])
        y_scr[gl] = z[:, 0:S5_TW]
        sre_scr[gl] = z[:, S5_TW:S5_TW + 2 * S5_P]
        sim_scr[gl] = z[:, S5_TW + 2 * S5_P:]

    fwd_lane = lax.broadcasted_iota(jnp.int32, (sb, 2 * S5_P), 1) < S5_P
    a_re = [lam_ref[gl][:, 0:2 * S5_P] for gl in range(S5_OCT)]
    a_im = [lam_ref[gl][:, 2 * S5_P:] for gl in range(S5_OCT)]

    def step(i, carry):
        fi = pl.ds(i, sb, stride=nc)
        bj = pl.ds(nc - 1 - i, sb, stride=nc)
        out = []
        for gl in range(S5_OCT):
            h_re, h_im = carry[2 * gl], carry[2 * gl + 1]
            hfre_scr[gl, fi, :] = h_re
            hfim_scr[gl, fi, :] = h_im
            hbre_scr[gl, bj, :] = h_re
            hbim_scr[gl, bj, :] = h_im
            s_re = jnp.where(fwd_lane, sre_scr[gl, fi, :], sre_scr[gl, bj, :])
            s_im = jnp.where(fwd_lane, sim_scr[gl, fi, :], sim_scr[gl, bj, :])
            out.append(a_re[gl] * h_re - a_im[gl] * h_im + s_re)
            out.append(a_re[gl] * h_im + a_im[gl] * h_re + s_im)
        return tuple(out)

    init = []
    for gl in range(S5_OCT):
        h0 = h0_ref[gl, 0]
        init += [h0[:, 0:2 * S5_P], h0[:, 2 * S5_P:]]
    fin = lax.fori_loop(0, nc, step, tuple(init), unroll=2 if nc % 2 == 0 else 1)

    all_fwd = lax.broadcasted_iota(jnp.int32, (rows, 2 * S5_P), 1) < S5_P
    for gl in range(S5_OCT):
        hfin_ref[gl, 0] = jnp.concatenate([fin[2 * gl], fin[2 * gl + 1]], axis=-1)
        hin = jnp.concatenate([jnp.where(all_fwd, hfre_scr[gl], hbre_scr[gl]),
                               jnp.where(all_fwd, hfim_scr[gl], hbim_scr[gl])], axis=-1).astype(BF16)
        y_scr[gl] += _dot(hin, woff_ref[gl])

    def scatter(i, carry):
        r0 = pl.multiple_of(i * rb, rb)
        for tq in range(tiles):
            by_group = [y_scr[gl, pl.ds(r0, rb), tq * LANES:(tq + 1) * LANES].reshape(tile_shape)
                        for gl in range(S5_OCT)]
            by_time = _unit_transpose(by_group, lane_masks, 2, S5_CH)
            for r, v in enumerate(_unit_transpose(by_time, sub_masks, 1, 1)):
                y_ref[tile_rows(i, r, tq)] = v
        return carry

    lax.fori_loop(0, rows // rb, scatter, 0)


def _s5_mixer(proj, tables, layer, h0, *, n_seq, seq_len):
    rhs, woff, lam_t = tables
    t = S5_T
    nc = seq_len // t
    sb = max(1, min(n_seq, S5_MAX_ROWS // nc))
    assert n_seq % sb == 0
    nbs = n_seq // sb
    rows = sb * nc
    rb = min(rows, 32)
    noct = S5_GROUPS // S5_OCT
    ucol = C_U // LANES
    tab = lambda shape: pl.BlockSpec((S5_OCT,) + shape, lambda j, s: (layer * noct + j, 0, 0))
    state = pl.BlockSpec((S5_OCT, 1, sb, S5_ST), lambda j, s: (j, s, 0, 0))
    rows_scr = lambda dt: pltpu.VMEM((S5_OCT, rows, 2 * S5_P), dt)
    n_tok = n_seq * seq_len
    tok_block = (sb * seq_len // S5_OCT, S5_OCT, LANES)
    y, hfin = pl.pallas_call(
        functools.partial(_s5_kernel, nc=nc, rb=rb),
        grid=(noct, nbs),
        in_specs=[pl.BlockSpec(tok_block, lambda j, s: (s, 0, ucol + j)),
                  tab(rhs.shape[1:]), tab(woff.shape[1:]), tab(lam_t.shape[1:]), state],
        out_specs=[pl.BlockSpec(tok_block, lambda j, s: (s, 0, j)), state],
        out_shape=[jax.ShapeDtypeStruct((n_tok // S5_OCT, S5_OCT, S5_WIDTH), F32),
                   jax.ShapeDtypeStruct((S5_GROUPS, nbs, sb, S5_ST), F32)],
        scratch_shapes=[pltpu.VMEM((S5_OCT, rows, S5_TW), BF16), pltpu.VMEM((S5_OCT, rows, S5_TW), F32),
                        rows_scr(F32), rows_scr(F32), rows_scr(F32), rows_scr(F32), rows_scr(F32), rows_scr(F32)],
        compiler_params=_cparams("parallel", "parallel"),
        name="s5_mixer",
    )(proj.reshape(n_tok // S5_OCT, S5_OCT, D_INP), rhs, woff, lam_t, h0.reshape(S5_GROUPS, nbs, sb, S5_ST))
    return y.reshape(n_tok, S5_WIDTH), hfin.reshape(S5_GROUPS, n_seq, S5_ST)


def _s5_state_in(re, im):
    b = re.shape[0]
    x = jnp.stack([re, im], axis=1)
    return jnp.transpose(x, (3, 0, 1, 2, 4)).reshape(S5_GROUPS, b, S5_ST)


def _s5_state_out(h):
    n_seq = h.shape[1]
    x = h.reshape(S5_GROUPS, n_seq, 2, 2, S5_P)
    x = jnp.transpose(x, (2, 1, 3, 0, 4))
    return x[0], x[1]


def kernel(x_prompt, x_sample, c, cache_k, cache_v, state_ssd, state_s5_re, state_s5_im, c_ctx, w_ada, b_ada, norm_mix_pre, norm_mix_post, norm_ffn_pre, norm_ffn_post, w_in, w_out, ssd_conv_w, ssd_conv_b, ssd_dt_bias, ssd_A_log, ssd_D, ssd_norm, attn_sink, s5_A_re, s5_A_im, s5_log_dt, s5_B_re, s5_B_im, s5_C_re, s5_C_im, s5_D, s5_w_glu, s5_b_glu, w_ffn_in, w_ffn_out):
    nb_ctx, len_ctx, _ = x_prompt.shape
    nb_lat, len_lat, _ = x_sample.shape
    n_ctx = nb_ctx * len_ctx
    n_lat = nb_lat * len_lat
    tm = 512
    mod_rows = -(-(1 + nb_lat) // 8) * 8
    cvecs = jnp.concatenate([c_ctx[None, :], c, jnp.zeros((mod_rows - 1 - nb_lat, D_MODEL), F32)], axis=0)
    mods = _ada(cvecs, w_ada, b_ada)
    mod_geom = ((n_ctx, 0), (len_lat, 1))
    token_sets = tuple(dict(mod_spec=_mod_spec(tm, *g)) for g in mod_geom)
    tm_up = min(FFN_UP_ROWS, n_ctx, len_lat)
    rope_args = (dict(), dict(rope_tables=_rope_tables(len_lat), seq_len=len_lat))

    s5_tab = _s5_tables(s5_A_re, s5_A_im, s5_log_dt, s5_B_re, s5_B_im, s5_C_re, s5_C_im)
    w_in_b = w_in.astype(BF16)
    w_in_p = jnp.concatenate(
        [w_in_b[:, :, O_Q:O_K], w_in_b[:, :, O_Z:O_DT], w_in_b[:, :, O_K:O_END], w_in_b[:, :, O_DT:O_Q],
         jnp.zeros((N_LAYERS, D_MODEL, DT_PAD - 2 * SSD_HEADS), BF16)], axis=2)
    w_out_b = w_out.astype(BF16)
    w_glu_b = s5_w_glu.astype(BF16)
    w_ffn_in_b = w_ffn_in.astype(BF16)
    w_ffn_out_b = w_ffn_out.astype(BF16)
    zeros_ssd = jnp.zeros((nb_ctx, SSD_STATE, SSD_WIDTH), F32)
    zeros_s5 = jnp.zeros((S5_GROUPS, nb_ctx, S5_ST), F32)

    xs = [x_prompt.reshape(n_ctx, D_MODEL), x_sample.reshape(n_lat, D_MODEL)]
    new_k, new_v, new_ssd, new_s5_re, new_s5_im = [], [], [], [], []
    for l in range(N_LAYERS):
        mod3 = mods[l].reshape(mod_rows, 1, 6 * D_MODEL)
        proj, qv, kt = zip(*[_in_proj(x, mod3, norm_mix_pre[l].reshape(1, -1), w_in_p, layer=l, tm=tm, **ts, **ra)
                             for x, ts, ra in zip(xs, token_sets, rope_args)])

        sink = attn_sink[l]
        ckt = jnp.swapaxes(cache_k[:, l].reshape(nb_lat, -1, KV_WIDTH), 1, 2).astype(BF16)
        cv = cache_v[:, l].reshape(nb_lat, -1, KV_WIDTH).astype(BF16)
        o_att = [_ctx_attention(qv[0], kt[0], sink, n_seq=nb_ctx, seq_len=len_ctx),
                 _lat_attention(qv[1], kt[1], sink, ckt, cv, n_seq=nb_lat, seq_len=len_lat)]

        consts = _ssd_consts(l, ssd_conv_w, ssd_conv_b, ssd_dt_bias, ssd_A_log, ssd_D, ssd_norm)
        y_ctx, hf_ctx, hb_ctx = _ssd_mixer(proj[0], zeros_ssd, zeros_ssd, consts, n_seq=nb_ctx, seq_len=len_ctx)
        y_lat, _, _ = _ssd_mixer(proj[1], _ssd_state_in(state_ssd[:, l, 0]), _ssd_state_in(state_ssd[:, l, 1]),
                                 consts, n_seq=nb_lat, seq_len=len_lat)
        y_ssd = [y_ctx, y_lat]

        s5_ctx, s5_fin = _s5_mixer(proj[0], s5_tab, l, zeros_s5, n_seq=nb_ctx, seq_len=len_ctx)
        s5_lat, _ = _s5_mixer(proj[1], s5_tab, l, _s5_state_in(state_s5_re[:, l], state_s5_im[:, l]),
                              n_seq=nb_lat, seq_len=len_lat)
        y_s5 = [s5_ctx, s5_lat]

        xs = [_out_proj(x, mod3, norm_mix_post[l].reshape(1, -1), y_ssd[i], o_att[i], y_s5[i], proj[i],
                        s5_D[l].reshape(1, -1), w_glu_b, s5_b_glu[l].reshape(1, -1), w_out_b,
                        layer=l, tm=tm, **ts)
              for i, (x, ts) in enumerate(zip(xs, token_sets))]
        xs = [_ffn(x, mod3, norm_ffn_pre[l].reshape(1, -1), norm_ffn_post[l].reshape(1, -1),
                   w_ffn_in_b, w_ffn_out_b, layer=l, tm_up=tm_up, tf=FFN_UP_COLS, tm_down=tm,
                   mod_spec_up=_mod_spec(tm_up, *g), mod_spec_down=_mod_spec(tm, *g))
              for x, g in zip(xs, mod_geom)]

        new_k.append(proj[0][:, C_K:C_K + KV_WIDTH].reshape(nb_ctx, len_ctx, ATT_KV, HEAD_DIM))
        new_v.append(proj[0][:, C_V:C_V + KV_WIDTH].reshape(nb_ctx, len_ctx, ATT_KV, HEAD_DIM))
        new_ssd += [hf_ctx, hb_ctx]
        re, im = _s5_state_out(s5_fin)
        new_s5_re.append(re)
        new_s5_im.append(im)

    y_prompt = xs[0].reshape(nb_ctx, len_ctx, D_MODEL)
    y_sample = xs[1].reshape(nb_lat, len_lat, D_MODEL)
    new_state_ssd = jnp.stack(new_ssd, axis=1).reshape(nb_ctx, N_LAYERS, 2, SSD_HEADS, SSD_HEAD_DIM, SSD_STATE)
    return (y_prompt, y_sample, jnp.stack(new_k, axis=1), jnp.stack(new_v, axis=1), new_state_ssd,
            jnp.stack(new_s5_re, axis=1), jnp.stack(new_s5_im, axis=1))
```

```python
import functools
import math

import jax
import jax.numpy as jnp
import numpy as np
from jax import lax
from jax.experimental import pallas as pl
from jax.experimental.pallas import tpu as pltpu

F32 = jnp.float32
BF16 = jnp.bfloat16

D_MODEL = 2048
N_LAYERS = 2
D_FF = 5632
GRID_W = 64
EPS = 1e-6
LANES = 128

SSD_HEADS = 12
SSD_HEAD_DIM = 64
SSD_WIDTH = SSD_HEADS * SSD_HEAD_DIM
SSD_GROUPS = 2
SSD_STATE = 64
SSD_BC = 2 * SSD_GROUPS * SSD_STATE
SSD_CHUNK = 128
SSD_CHUNKS_PER_STEP = 8
SSD_HALO = 8
ATT_HEADS = 12
ATT_KV = 4
ATT_GROUP = ATT_HEADS // ATT_KV
HEAD_DIM = 64
ATT_WIDTH = ATT_HEADS * HEAD_DIM
KV_WIDTH = ATT_KV * HEAD_DIM
ATT_BLOCK = 128
LAT_Q_BLOCKS = 4
QV_WIDTH = ATT_WIDTH + KV_WIDTH
LOG2E = math.log2(math.e)
Q_SCALE = HEAD_DIM ** -0.5 * LOG2E
ROPE_PER_AXIS = HEAD_DIM // 4
ROPE_BASE = 10000.0
S5_GROUPS = 32
S5_CH = 16
S5_WIDTH = S5_GROUPS * S5_CH
S5_P = 64
S5_T = 32
S5_TW = S5_T * S5_CH
S5_ST = 4 * S5_P
S5_OCT = LANES // S5_CH
S5_MAX_ROWS = 256
MIX_WIDTH = SSD_WIDTH + ATT_WIDTH + S5_WIDTH

O_Z = 0
O_XS = SSD_WIDTH
O_BC = O_XS + SSD_WIDTH
O_DT = O_BC + SSD_BC
O_Q = O_DT + 2 * SSD_HEADS
O_K = O_Q + ATT_WIDTH
O_V = O_K + KV_WIDTH
O_U = O_V + KV_WIDTH
O_END = O_U + S5_WIDTH
C_Q = 0
C_Z = 768
C_XS = 1536
C_BC = 2304
C_K = 2560
C_V = 2816
C_U = 3072
C_DT = 3584
DT_PAD = 128
D_INP = C_DT + DT_PAD

FFN_UP_ROWS = 1024
FFN_UP_COLS = 512

NEG = -1e30
VMEM_LIMIT = 56 * 1024 * 1024


def _cparams(*sem):
    return pltpu.CompilerParams(dimension_semantics=sem, vmem_limit_bytes=VMEM_LIMIT)


def _resident(shape):
    nd = len(shape)
    return pl.BlockSpec(shape, lambda *_: (0,) * nd, pipeline_mode=pl.Buffered(1))


def _layer_resident(shape, layer, block=0):
    rest = (0,) * (len(shape) - 1)
    return pl.BlockSpec((None,) + shape, lambda *_: (layer, block) + rest, pipeline_mode=pl.Buffered(1))


def _rms(x, g):
    return x * lax.rsqrt(jnp.mean(x * x, axis=-1, keepdims=True) + EPS) * g


def _silu(x):
    return x * jax.nn.sigmoid(x)


def _dot(a, b):
    return jnp.dot(a, b, preferred_element_type=F32)


def _dot_wide(a, b):
    a_hi, a_lo = _split_bf16(a, 2)
    b_hi, b_lo = _split_bf16(b, 2)
    return _dot(a_hi, b_hi) + (_dot(a_hi, b_lo) + _dot(a_lo, b_hi))


def _dot_nt(a, b):
    return lax.dot_general(a, b, (((1,), (1,)), ((), ())), preferred_element_type=F32)


def _dot_tn(a, b):
    return lax.dot_general(a, b, (((0,), (0,)), ((), ())), preferred_element_type=F32)


def _split_bf16(x, parts):
    out = []
    for _ in range(parts):
        p = x.astype(BF16)
        out.append(p)
        x = x - p.astype(F32)
    return out


def _dot_sel_rhs(x, sel, parts):
    acc = None
    for p in _split_bf16(x, parts):
        t = _dot(p, sel)
        acc = t if acc is None else acc + t
    return acc


def _dot_sel_lhs(sel, x, parts):
    acc = None
    for p in _split_bf16(x, parts):
        t = _dot(sel, p)
        acc = t if acc is None else acc + t
    return acc


def _mod_spec(tm, rows_per_mod, first_row):
    return pl.BlockSpec((1, 1, 6 * D_MODEL), lambda i, *_: (first_row + (i * tm) // rows_per_mod, 0, 0))


def _ada_kernel(c_ref, w_ref, b_ref, o_ref):
    c = c_ref[...]
    o_ref[0] = _dot(_silu(c).astype(BF16), w_ref[0].astype(BF16)) + b_ref[0]


def _ada(cvecs, w_ada, b_ada):
    tn = 1024
    rows = cvecs.shape[0]
    return pl.pallas_call(
        _ada_kernel,
        grid=(N_LAYERS, 6 * D_MODEL // tn),
        in_specs=[pl.BlockSpec((rows, D_MODEL), lambda l, j: (0, 0)),
                  pl.BlockSpec((1, D_MODEL, tn), lambda l, j: (l, 0, j)),
                  pl.BlockSpec((1, 1, tn), lambda l, j: (l, 0, j))],
        out_specs=pl.BlockSpec((1, rows, tn), lambda l, j: (l, 0, j)),
        out_shape=jax.ShapeDtypeStruct((N_LAYERS, rows, 6 * D_MODEL), F32),
        compiler_params=_cparams("parallel", "parallel"),
        name="ada_mod",
    )(cvecs, w_ada, b_ada.reshape(N_LAYERS, 1, 6 * D_MODEL))


def _permute_w_in_kernel(w_ref, o_ref):
    w = w_ref[0]
    pad = jnp.zeros((w.shape[0], DT_PAD - 2 * SSD_HEADS), F32)
    o_ref[0] = jnp.concatenate([w[:, O_Q:O_K], w[:, O_Z:O_DT], w[:, O_K:O_END], w[:, O_DT:O_Q], pad],
                               axis=1).astype(BF16)


def _permute_w_in(w_in, rows=256):
    return pl.pallas_call(
        _permute_w_in_kernel,
        grid=(N_LAYERS, D_MODEL // rows),
        in_specs=[pl.BlockSpec((1, rows, O_END), lambda l, i: (l, i, 0))],
        out_specs=pl.BlockSpec((1, rows, D_INP), lambda l, i: (l, i, 0)),
        out_shape=jax.ShapeDtypeStruct((N_LAYERS, D_MODEL, D_INP), BF16),
        compiler_params=_cparams("parallel", "parallel"),
        name="permute_w_in",
    )(w_in)

def _rope(x, cos, sin_signed, first_half):
    outs = []
    for j in range(x.shape[1] // LANES):
        xj = x[:, j * LANES:(j + 1) * LANES]
        partner = jnp.where(first_half, pltpu.roll(xj, LANES - HEAD_DIM // 2, axis=1),
                            pltpu.roll(xj, HEAD_DIM // 2, axis=1))
        outs.append(xj * cos + partner * sin_signed)
    return jnp.concatenate(outs, axis=-1)


def _in_proj_kernel(*refs, rope):
    if rope:
        x_ref, mod_ref, g_ref, w_ref, cos_ref, sin_ref, o_ref, qv_ref, kt_ref = refs
    else:
        x_ref, mod_ref, g_ref, w_ref, o_ref, qv_ref, kt_ref = refs
    mod = mod_ref[0]
    sh = mod[:, 0:D_MODEL]
    sc = mod[:, D_MODEL:2 * D_MODEL]
    h = _rms(x_ref[...], g_ref[...]) * (1.0 + sc) + sh
    o_ref[...] = _dot(h.astype(BF16), w_ref[...])
    o_ref[:, C_Z:C_Z + SSD_WIDTH] = _silu(o_ref[:, C_Z:C_Z + SSD_WIDTH])
    q = o_ref[:, C_Q:C_Q + ATT_WIDTH]
    k = o_ref[:, C_K:C_K + KV_WIDTH]
    if rope:
        lane = lax.broadcasted_iota(jnp.int32, (x_ref.shape[0], LANES), 1)
        first_half = (lane % HEAD_DIM) < (HEAD_DIM // 2)
        q = _rope(q, cos_ref[...], sin_ref[...], first_half)
        k = _rope(k, cos_ref[...], sin_ref[...], first_half)
    qv_ref[:, 0:ATT_WIDTH] = (q * Q_SCALE).astype(BF16)
    qv_ref[:, ATT_WIDTH:] = o_ref[:, C_V:C_V + KV_WIDTH].astype(BF16)
    kt_ref[...] = k.T.astype(BF16)


def _in_proj(x, mod3, gamma, w, *, layer, tm, mod_spec, rope_tables=None, seq_len=None):
    n = x.shape[0]
    rope = rope_tables is not None
    tables, table_specs = (), []
    if rope:
        per_seq = seq_len // tm
        tables = tuple(rope_tables)
        table_specs = [pl.BlockSpec((tm, LANES), lambda i: (i % per_seq, 0))] * 2
    return pl.pallas_call(
        functools.partial(_in_proj_kernel, rope=rope),
        grid=(n // tm,),
        in_specs=[pl.BlockSpec((tm, D_MODEL), lambda i: (i, 0)),
                  mod_spec,
                  _resident((1, D_MODEL)),
                  _layer_resident((D_MODEL, D_INP), layer)] + table_specs,
        out_specs=[pl.BlockSpec((tm, D_INP), lambda i: (i, 0)), pl.BlockSpec((tm, QV_WIDTH), lambda i: (i, 0)),
                   pl.BlockSpec((KV_WIDTH, tm), lambda i: (0, i))],
        out_shape=[jax.ShapeDtypeStruct((n, D_INP), F32), jax.ShapeDtypeStruct((n, QV_WIDTH), BF16),
                   jax.ShapeDtypeStruct((KV_WIDTH, n), BF16)],
        compiler_params=_cparams("parallel"),
        name="in_proj",
    )(x, mod3, gamma, w, *tables)


def _gelu_tanh(x):
    return 0.5 * x * (1.0 + jnp.tanh(math.sqrt(2.0 / math.pi) * (x + 0.044715 * (x * x * x))))


def _out_proj_kernel(x_ref, mod_ref, g_ref, yssd_ref, oatt_ref, ys5_ref, u_ref, d_ref, wglu_ref, bglu_ref,
                     w12_ref, w3_ref, o_ref, mix_scr):
    mix_scr[:, 0:SSD_WIDTH] = yssd_ref[...]
    mix_scr[:, SSD_WIDTH:] = oatt_ref[...]
    mix = _dot(mix_scr[...], w12_ref[...])
    y5 = ys5_ref[...] + d_ref[...] * u_ref[...]
    g = _gelu_tanh(y5)
    s5 = g * jax.nn.sigmoid(_dot(g.astype(BF16), wglu_ref[...]) + bglu_ref[...])
    mix = mix + _dot(s5.astype(BF16), w3_ref[...])
    gate = mod_ref[0][:, 2 * D_MODEL:3 * D_MODEL]
    o_ref[...] = x_ref[...] + gate * _rms(mix, g_ref[...])


def _out_proj(x, mod3, gamma, y_ssd, o_att, y_s5, proj, s5_d, w_glu, b_glu, w_out, *, layer, tm, mod_spec):
    n = x.shape[0]
    row = lambda i: (i, 0)
    return pl.pallas_call(
        _out_proj_kernel,
        grid=(n // tm,),
        in_specs=[pl.BlockSpec((tm, D_MODEL), row),
                  mod_spec,
                  _resident((1, D_MODEL)),
                  pl.BlockSpec((tm, SSD_WIDTH), row),
                  pl.BlockSpec((tm, ATT_WIDTH), row),
                  pl.BlockSpec((tm, S5_WIDTH), row),
                  pl.BlockSpec((tm, S5_WIDTH), lambda i: (i, C_U // S5_WIDTH)),
                  _resident((1, S5_WIDTH)),
                  _layer_resident((S5_WIDTH, S5_WIDTH), layer),
                  _resident((1, S5_WIDTH)),
                  _layer_resident((SSD_WIDTH + ATT_WIDTH, D_MODEL), layer, 0),
                  _layer_resident((S5_WIDTH, D_MODEL), layer, (SSD_WIDTH + ATT_WIDTH) // S5_WIDTH)],
        out_specs=pl.BlockSpec((tm, D_MODEL), row),
        out_shape=jax.ShapeDtypeStruct((n, D_MODEL), F32),
        scratch_shapes=[pltpu.VMEM((tm, SSD_WIDTH + ATT_WIDTH), BF16)],
        compiler_params=_cparams("parallel"),
        name="out_proj",
    )(x, mod3, gamma, y_ssd, o_att, y_s5, proj, s5_d, w_glu, b_glu, w_out, w_out)


def _ffn_up_kernel(x_ref, mod_ref, gpre_ref, wg_ref, wu_ref, act_ref, h_scr):
    j = pl.program_id(1)

    def gated(h):
        return (_silu(_dot(h, wg_ref[...])) * _dot(h, wu_ref[...])).astype(BF16)

    @pl.when(j == 0)
    def _():
        mod = mod_ref[0]
        sh = mod[:, 3 * D_MODEL:4 * D_MODEL]
        sc = mod[:, 4 * D_MODEL:5 * D_MODEL]
        h = (_rms(x_ref[...], gpre_ref[...]) * (1.0 + sc) + sh).astype(BF16)
        h_scr[...] = h
        act_ref[...] = gated(h)

    @pl.when(j > 0)
    def _():
        act_ref[...] = gated(h_scr[...])


def _ffn_down_kernel(x_ref, mod_ref, gpost_ref, act_ref, wo_ref, o_ref):
    gate = mod_ref[0][:, 5 * D_MODEL:6 * D_MODEL]
    o_ref[...] = x_ref[...] + gate * _rms(_dot(act_ref[...], wo_ref[...]), gpost_ref[...])


def _ffn(x, mod3, g_pre, g_post, w_in, w_out, *, layer, tm_up, tf, tm_down, mod_spec_up, mod_spec_down):
    n = x.shape[0]
    nf = D_FF // tf
    act = pl.pallas_call(
        _ffn_up_kernel,
        grid=(n // tm_up, nf),
        in_specs=[pl.BlockSpec((tm_up, D_MODEL), lambda i, j: (i, 0)),
                  mod_spec_up,
                  _resident((1, D_MODEL)),
                  pl.BlockSpec((None, D_MODEL, tf), lambda i, j: (layer, 0, j)),
                  pl.BlockSpec((None, D_MODEL, tf), lambda i, j: (layer, 0, j + nf))],
        out_specs=pl.BlockSpec((tm_up, tf), lambda i, j: (i, j)),
        out_shape=jax.ShapeDtypeStruct((n, D_FF), BF16),
        scratch_shapes=[pltpu.VMEM((tm_up, D_MODEL), BF16)],
        compiler_params=_cparams("parallel", "arbitrary"),
        name="ffn_up",
    )(x, mod3, g_pre, w_in, w_in)
    return pl.pallas_call(
        _ffn_down_kernel,
        grid=(n // tm_down,),
        in_specs=[pl.BlockSpec((tm_down, D_MODEL), lambda i: (i, 0)),
                  mod_spec_down,
                  _resident((1, D_MODEL)),
                  pl.BlockSpec((tm_down, D_FF), lambda i: (i, 0)),
                  _layer_resident((D_FF, D_MODEL), layer)],
        out_specs=pl.BlockSpec((tm_down, D_MODEL), lambda i: (i, 0)),
        out_shape=jax.ShapeDtypeStruct((n, D_MODEL), F32),
        compiler_params=_cparams("parallel"),
        name="ffn_down",
    )(x, mod3, g_post, act, w_out)


def _attend(queries, sink_ref):
    rows = queries[0][0].shape[0]
    srow = lax.broadcasted_iota(jnp.int32, (ATT_GROUP * rows, 1), 0)
    outs = [[None] * ATT_HEADS for _ in queries]
    units = [(i, kv) for i in range(len(queries)) for kv in range(ATT_KV)]

    def masked_scores(unit):
        (q, parts), kv = queries[unit[0]], unit[1]
        heads = range(kv * ATT_GROUP, (kv + 1) * ATT_GROUP)
        q3 = jnp.concatenate([q[:, h * HEAD_DIM:(h + 1) * HEAD_DIM] for h in heads], axis=0)
        scores = []
        for kt, _, mask in parts:
            s = _dot(q3, kt[kv * HEAD_DIM:(kv + 1) * HEAD_DIM, :])
            scores.append(s if mask is None else jnp.where(mask, s, NEG))
        return scores

    def softmax_numerator(unit, scores):
        kv = unit[1]
        heads = range(kv * ATT_GROUP, (kv + 1) * ATT_GROUP)
        sink = jnp.full((ATT_GROUP * rows, 1), sink_ref[heads[-1]] * LOG2E, F32)
        for i in range(ATT_GROUP - 2, -1, -1):
            sink = jnp.where(srow < (i + 1) * rows, sink_ref[heads[i]] * LOG2E, sink)
        blocks = [s[:, j:j + LANES] for s in scores for j in range(0, s.shape[1], LANES)]
        m = jnp.maximum(sink, jnp.max(functools.reduce(jnp.maximum, blocks), axis=-1, keepdims=True))
        p = jnp.concatenate([jnp.exp2(s - m).astype(BF16) for s in scores], axis=1)
        return p, jnp.exp2(sink - m)

    def weighted_values(unit, p, sink_term):
        (_, parts), kv = queries[unit[0]], unit[1]
        lo = kv * HEAD_DIM
        vlo = (lo // LANES) * LANES
        v_first = lo == vlo
        v128 = jnp.concatenate([v[:, vlo:vlo + LANES] for _, v, _ in parts], axis=0)
        lane = lax.broadcasted_iota(jnp.int32, v128.shape, 1)
        keep = (lane < HEAD_DIM) if v_first else (lane >= HEAD_DIM)
        acc = _dot(p, jnp.where(keep, v128, jnp.ones_like(v128)))
        ocol, dcol = (0, HEAD_DIM) if v_first else (HEAD_DIM, 0)
        o3 = acc[:, ocol:ocol + HEAD_DIM] / (acc[:, dcol:dcol + 1] + sink_term)
        for i in range(ATT_GROUP):
            outs[unit[0]][kv * ATT_GROUP + i] = o3[i * rows:(i + 1) * rows]

    scores = {0: masked_scores(units[0])}
    probs = {}
    for step in range(len(units) + 2):
        if step + 1 < len(units):
            scores[step + 1] = masked_scores(units[step + 1])
        if step - 1 in probs:
            weighted_values(units[step - 1], *probs.pop(step - 1))
        if step in scores:
            probs[step] = softmax_numerator(units[step], scores.pop(step))
    return [jnp.concatenate(o, axis=-1) for o in outs]


def _ctx_attn_kernel(sink_ref, q_ref, kt_ref, v_ref, o_ref):
    def scores(h):
        lo = (h // ATT_GROUP) * HEAD_DIM
        return _dot(q_ref[:, h * HEAD_DIM:(h + 1) * HEAD_DIM], kt_ref[lo:lo + HEAD_DIM, :])

    outs = []
    ahead = scores(0)
    for h in range(ATT_HEADS):
        s = ahead
        if h + 1 < ATT_HEADS:
            ahead = scores(h + 1)
        lo = (h // ATT_GROUP) * HEAD_DIM
        sink = sink_ref[h] * LOG2E
        m = jnp.maximum(jnp.max(s, axis=-1, keepdims=True), sink)
        p = jnp.exp2(s - m)
        den = jnp.exp2(sink - m) + jnp.sum(p, axis=-1, keepdims=True)
        outs.append(_dot(p.astype(BF16), v_ref[:, lo:lo + HEAD_DIM]) / den)
    o_ref[...] = jnp.concatenate(outs, axis=-1).astype(o_ref.dtype)


def _ctx_attention(qv, kt, sink, *, n_seq, seq_len):
    vcol = ATT_WIDTH // KV_WIDTH
    return pl.pallas_call(
        _ctx_attn_kernel,
        grid=(n_seq,),
        in_specs=[pl.BlockSpec(memory_space=pltpu.SMEM),
                  pl.BlockSpec((seq_len, ATT_WIDTH), lambda b: (b, 0)),
                  pl.BlockSpec((KV_WIDTH, seq_len), lambda b: (0, b)),
                  pl.BlockSpec((seq_len, KV_WIDTH), lambda b: (b, vcol))],
        out_specs=pl.BlockSpec((seq_len, ATT_WIDTH), lambda b: (b, 0)),
        out_shape=jax.ShapeDtypeStruct((n_seq * seq_len, ATT_WIDTH), BF16),
        compiler_params=_cparams("parallel"),
        name="ctx_attention",
    )(sink, qv, kt, qv)


def _lat_attn_kernel(sink_ref, q_ref, *refs, nq):
    kt_refs, v_refs = refs[0:nq + 2], refs[nq + 2:2 * nq + 4]
    ck_ref, cv_ref, o_ref = refs[2 * nq + 4:]
    step = pl.program_id(1)
    blk = ATT_BLOCK
    row = lax.broadcasted_iota(jnp.int32, (ATT_GROUP * blk, blk), 0) % blk
    col = lax.broadcasted_iota(jnp.int32, (ATT_GROUP * blk, blk), 1)
    before = [col >= row] * nq
    after = [col <= row] * nq
    before[0] = jnp.logical_and(before[0], step > 0)
    after[-1] = jnp.logical_and(after[-1], step < pl.num_programs(1) - 1)
    ctx = (ck_ref[0], cv_ref[0], None)
    blocks = [(kt[...], v[...]) for kt, v in zip(kt_refs, v_refs)]
    queries = [(q_ref[i * blk:(i + 1) * blk, :],
                [ctx, blocks[i] + (before[i],), blocks[i + 1] + (None,), blocks[i + 2] + (after[i],)])
               for i in range(nq)]
    for i, o in enumerate(_attend(queries, sink_ref)):
        o_ref[i * blk:(i + 1) * blk, :] = o.astype(o_ref.dtype)


def _lat_attention(qv, kt, sink, ckt, cv, *, n_seq, seq_len):
    blk = ATT_BLOCK
    nb = seq_len // blk
    nq = max(c for c in range(1, LAT_Q_BLOCKS + 1) if nb % c == 0)
    steps = nb // nq

    def block(offset):
        return lambda b, n: b * nb + jnp.clip(nq * n + offset, 0, nb - 1)

    vcol = ATT_WIDTH // KV_WIDTH
    offsets = range(-1, nq + 1)
    kt_specs = [pl.BlockSpec((KV_WIDTH, blk), lambda b, n, f=block(o): (0, f(b, n))) for o in offsets]
    v_specs = [pl.BlockSpec((blk, KV_WIDTH), lambda b, n, f=block(o): (f(b, n), vcol)) for o in offsets]
    return pl.pallas_call(
        functools.partial(_lat_attn_kernel, nq=nq),
        grid=(n_seq, steps),
        in_specs=[pl.BlockSpec(memory_space=pltpu.SMEM),
                  pl.BlockSpec((nq * blk, ATT_WIDTH), lambda b, n: (b * steps + n, 0))] + kt_specs + v_specs
                 + [pl.BlockSpec((1,) + ckt.shape[1:], lambda b, n: (b, 0, 0)),
                    pl.BlockSpec((1,) + cv.shape[1:], lambda b, n: (b, 0, 0))],
        out_specs=pl.BlockSpec((nq * blk, ATT_WIDTH), lambda b, n: (b * steps + n, 0)),
        out_shape=jax.ShapeDtypeStruct((n_seq * seq_len, ATT_WIDTH), BF16),
        compiler_params=_cparams("parallel", "parallel"),
        name="lat_attention",
    )(sink, qv, *([kt] * (nq + 2)), *([qv] * (nq + 2)), ckt, cv)


def _rope_tables(seq_len):
    rows = seq_len // GRID_W
    row = np.repeat(np.arange(rows, dtype=np.float32), GRID_W)
    col = np.tile(np.arange(GRID_W, dtype=np.float32), rows)
    inv = np.float32(ROPE_BASE) ** (-np.arange(ROPE_PER_AXIS, dtype=np.float32) / np.float32(ROPE_PER_AXIS))
    ang = np.concatenate([row[:, None] * inv, col[:, None] * inv], axis=-1)
    cos, sin = np.cos(ang), np.sin(ang)
    cos128 = np.tile(cos, (1, 4))
    sin128 = np.tile(np.concatenate([-sin, sin], axis=-1), (1, 2))
    return jnp.asarray(cos128, F32), jnp.asarray(sin128, F32)


def _conv_silu(scr, prev_ref, cur_ref, next_ref, w_ref, b_ref, at_start, at_end):
    n = cur_ref.shape[0]
    h = SSD_HALO
    scr[0:h, :] = jnp.where(at_start, 0.0, prev_ref[...])
    scr[h:h + n, :] = cur_ref[...]
    scr[h + n:h + n + h, :] = jnp.where(at_end, 0.0, next_ref[...])
    return _silu(scr[h - 1:h - 1 + n, :] * w_ref[0:1, :] + scr[h:h + n, :] * w_ref[1:2, :]
                 + scr[h + 1:h + 1 + n, :] * w_ref[2:3, :] + b_ref[...])


def _softplus(x):
    return jnp.maximum(x, 0.0) + jnp.log1p(jnp.exp(-jnp.abs(x)))


def _ssd_bwd_kernel(xsp_ref, xsc_ref, xsn_ref, bcp_ref, bcc_ref, bcn_ref, dt_ref, h0_ref,
                    cwx_ref, cbx_ref, cwb_ref, cbb_ref, dtb_ref, alog_ref, tri_ref, exp_ref,
                    hstart_ref, hfin_ref, xbc_ref, xs_scr, bc_scr, h_scr):
    i = pl.program_id(1)
    q = SSD_CHUNK
    half = SSD_WIDTH // SSD_GROUPS
    sel_b = exp_ref[1]
    neg_a = -LOG2E * jnp.exp(alog_ref[...])
    at_start = i == pl.num_programs(1) - 1
    at_end = i == 0
    xbc_ref[:, 0:SSD_WIDTH] = _conv_silu(xs_scr, xsp_ref, xsc_ref, xsn_ref, cwx_ref, cbx_ref, at_start, at_end)
    xbc_ref[:, SSD_WIDTH:] = _conv_silu(bc_scr, bcp_ref, bcc_ref, bcn_ref, cwb_ref, cbb_ref, at_start, at_end)

    @pl.when(i == 0)
    def _():
        h_scr[...] = h0_ref[0]

    hb = h_scr[...]
    for j in reversed(range(xsc_ref.shape[0] // q)):
        rows = slice(j * q, (j + 1) * q)
        hstart_ref[0, j] = hb
        bm = xbc_ref[rows, SSD_WIDTH:SSD_WIDTH + SSD_GROUPS * SSD_STATE].astype(BF16)
        dtv = _softplus(dt_ref[rows, :] + dtb_ref[...])
        rev = _dot_sel_lhs(tri_ref[1], dtv * neg_a, 3)
        dt_w = _dot_sel_rhs(dtv, sel_b, 2)
        to_end = _dot_sel_rhs(jnp.exp2(rev[0:1, :] - rev), sel_b, 2)
        decay = _dot_sel_rhs(jnp.exp2(rev[0:SSD_HALO, :]), sel_b, 2)[0:1, :]
        xw = (xbc_ref[rows, 0:SSD_WIDTH] * dt_w * to_end).astype(BF16)
        upd = jnp.concatenate(
            [_dot_tn(bm[:, g * SSD_STATE:(g + 1) * SSD_STATE], xw[:, g * half:(g + 1) * half])
             for g in range(SSD_GROUPS)], axis=-1)
        hb = hb * decay + upd
    h_scr[...] = hb

    @pl.when(i == pl.num_programs(1) - 1)
    def _():
        hfin_ref[0] = hb.T


def _ssd_intra(xs, bc, dt_raw, dtb_ref, alog_ref, tri_ref, exp_ref, dvec_ref):
    q = SSD_CHUNK
    nh = SSD_HEADS
    gs = SSD_GROUPS * SSD_STATE
    bm = bc[:, 0:gs].astype(BF16)
    cm = bc[:, gs:2 * gs].astype(BF16)
    xs_bf = xs.astype(BF16)

    dtv = _softplus(dt_raw + dtb_ref[...])
    dta = dtv * (-LOG2E * jnp.exp(alog_ref[...]))
    lower, upper = tri_ref[0], tri_ref[1]
    cum = _dot_sel_lhs(lower, dta, 3)
    rev = _dot_sel_lhs(upper, dta, 3)
    dt_t = dtv.T
    dta_t = dta.T
    cum_t = _dot_sel_rhs(dta_t, upper, 3)
    rev_t = _dot_sel_rhs(dta_t, lower, 3)

    sel_f, sel_b = exp_ref[0], exp_ref[1]
    dtf_w = _dot_sel_rhs(dtv, sel_f, 2)
    ecum_w = _dot_sel_rhs(jnp.exp2(cum), sel_f, 2)
    erev_w = _dot_sel_rhs(jnp.exp2(rev), sel_b, 2)
    toend_w = _dot_sel_rhs(jnp.exp2(cum[q - 1:q, :] - cum), sel_f, 2)

    row = lax.broadcasted_iota(jnp.int32, (q, q), 0)
    col = lax.broadcasted_iota(jnp.int32, (q, q), 1)
    below = row > col
    log_dt_t = jnp.log2(dt_t)
    src_f = log_dt_t - cum_t
    src_b = log_dt_t - rev_t
    gmat = [_dot_nt(cm[:, g * SSD_STATE:(g + 1) * SSD_STATE], bm[:, g * SSD_STATE:(g + 1) * SSD_STATE])
            for g in range(SSD_GROUPS)]
    per_group = nh // SSD_GROUPS
    half = SSD_WIDTH // SSD_GROUPS
    ys = []
    for h in range(nh):
        e = jnp.where(below, cum[:, h:h + 1] + src_f[h:h + 1, :],
                      rev[:, nh + h:nh + h + 1] + src_b[nh + h:nh + h + 1, :])
        m = (gmat[h // per_group] * jnp.exp2(e)).astype(BF16)
        ys.append(_dot(m, xs_bf[:, h * SSD_HEAD_DIM:(h + 1) * SSD_HEAD_DIM]))
    cb = cm.astype(F32) * bm.astype(F32)
    g_diag = jnp.concatenate(
        [jnp.broadcast_to(jnp.sum(cb[:, g * SSD_STATE:(g + 1) * SSD_STATE], axis=-1, keepdims=True), (q, half))
         for g in range(SSD_GROUPS)], axis=-1)
    y = jnp.concatenate(ys, axis=-1) + (g_diag * dtf_w + dvec_ref[...]) * xs
    xw = (xs * dtf_w * toend_w).astype(BF16)
    upd = jnp.concatenate(
        [_dot_tn(bm[:, g * SSD_STATE:(g + 1) * SSD_STATE], xw[:, g * half:(g + 1) * half])
         for g in range(SSD_GROUPS)], axis=-1)
    return y, cm, ecum_w, erev_w, upd


def _ssd_carry(hf, hb_start, gz, intra, nw_ref):
    y, cm, ecum_w, erev_w, upd = intra
    half = SSD_WIDTH // SSD_GROUPS
    hf_bf = hf.astype(BF16)
    hb_bf = hb_start.astype(BF16)
    off_f = jnp.concatenate([_dot(cm[:, g * SSD_STATE:(g + 1) * SSD_STATE], hf_bf[:, g * half:(g + 1) * half])
                             for g in range(SSD_GROUPS)], axis=-1)
    off_b = jnp.concatenate([_dot(cm[:, g * SSD_STATE:(g + 1) * SSD_STATE], hb_bf[:, g * half:(g + 1) * half])
                             for g in range(SSD_GROUPS)], axis=-1)
    y = y + off_f * ecum_w + off_b * erev_w
    y = _rms(y * gz, nw_ref[...])
    return y, hf * ecum_w[SSD_CHUNK - 1:SSD_CHUNK, :] + upd


def _ssd_fwd_kernel(gz_ref, xs_ref, bc_ref, dt_ref, hb_ref, h0_ref,
                    dtb_ref, alog_ref, tri_ref, exp_ref, dvec_ref, nw_ref, y_ref, hfin_ref, h_scr):
    c = pl.program_id(1)
    q = SSD_CHUNK
    cps = xs_ref.shape[0] // q
    rows = [slice(j * q, (j + 1) * q) for j in range(cps)]

    def intra(j):
        return _ssd_intra(xs_ref[rows[j], :], bc_ref[rows[j], :], dt_ref[rows[j], :],
                          dtb_ref, alog_ref, tri_ref, exp_ref, dvec_ref)

    @pl.when(c == 0)
    def _():
        h_scr[...] = h0_ref[0]

    hf = h_scr[...]
    ahead = intra(0)
    for j in range(cps):
        cur = ahead
        if j + 1 < cps:
            ahead = intra(j + 1)
        y, hf = _ssd_carry(hf, hb_ref[0, j], gz_ref[rows[j], :], cur, nw_ref)
        y_ref[rows[j], :] = y.astype(y_ref.dtype)
    h_scr[...] = hf

    @pl.when(c == pl.num_programs(1) - 1)
    def _():
        hfin_ref[0] = hf.T


def _ssd_mixer(proj, h0_f, h0_b, consts, *, n_seq, seq_len):
    q = SSD_CHUNK
    nc = seq_len // q
    cps = max(c for c in range(1, SSD_CHUNKS_PER_STEP + 1) if nc % c == 0)
    ns = nc // cps
    rows = cps * q
    (cwx, cbx, cwb, cbb, dt_bias, a_log, tri, expand, dvec, nw) = consts
    per8 = rows // SSD_HALO
    last8 = proj.shape[0] // SSD_HALO - 1

    def raw_specs(width, col):
        step = lambda b, i: b * ns + ns - 1 - i
        return [pl.BlockSpec((SSD_HALO, width), lambda b, i: (jnp.maximum(step(b, i) * per8 - 1, 0), col)),
                pl.BlockSpec((rows, width), lambda b, i: (step(b, i), col)),
                pl.BlockSpec((SSD_HALO, width), lambda b, i: (jnp.minimum((step(b, i) + 1) * per8, last8), col))]

    def specs(chunk_of):
        def cur(b, i):
            return b * ns + chunk_of(i)

        return [pl.BlockSpec((rows, SSD_WIDTH), lambda b, i: (cur(b, i), 0)),
                pl.BlockSpec((rows, SSD_BC), lambda b, i: (cur(b, i), SSD_WIDTH // SSD_BC)),
                pl.BlockSpec((rows, DT_PAD), lambda b, i: (cur(b, i), C_DT // DT_PAD))], cur

    state_spec = pl.BlockSpec((1, SSD_STATE, SSD_WIDTH), lambda b, i: (b, 0, 0))
    final_spec = pl.BlockSpec((1, SSD_WIDTH, SSD_STATE), lambda b, i: (b, 0, 0))
    const_specs = [_resident(dt_bias.shape), _resident(a_log.shape), _resident(tri.shape),
                   _resident(expand.shape)]
    scratch = [pltpu.VMEM((SSD_STATE, SSD_WIDTH), F32)]

    data_specs, rcur = specs(lambda i: ns - 1 - i)
    hb_start, hb_fin, xbc = pl.pallas_call(
        _ssd_bwd_kernel,
        grid=(n_seq, ns),
        in_specs=(raw_specs(SSD_WIDTH, C_XS // SSD_WIDTH) + raw_specs(SSD_BC, C_BC // SSD_BC)
                  + [data_specs[2], state_spec]
                  + [_resident(cwx.shape), _resident(cbx.shape), _resident(cwb.shape), _resident(cbb.shape)]
                  + const_specs),
        out_specs=[pl.BlockSpec((1, cps, SSD_STATE, SSD_WIDTH), lambda b, i: (b, ns - 1 - i, 0, 0)), final_spec,
                   pl.BlockSpec((rows, SSD_WIDTH + SSD_BC), lambda b, i: (rcur(b, i), 0))],
        out_shape=[jax.ShapeDtypeStruct((n_seq, nc, SSD_STATE, SSD_WIDTH), F32),
                   jax.ShapeDtypeStruct((n_seq, SSD_WIDTH, SSD_STATE), F32),
                   jax.ShapeDtypeStruct((n_seq * seq_len, SSD_WIDTH + SSD_BC), F32)],
        scratch_shapes=[pltpu.VMEM((rows + 2 * SSD_HALO, SSD_WIDTH), F32),
                        pltpu.VMEM((rows + 2 * SSD_HALO, SSD_BC), F32)] + scratch,
        compiler_params=_cparams("parallel", "arbitrary"),
        name="ssd_backward_states",
    )(proj, proj, proj, proj, proj, proj, proj, h0_b, cwx, cbx, cwb, cbb, dt_bias, a_log, tri, expand)

    data_specs, cur = specs(lambda i: i)
    y, hf_fin = pl.pallas_call(
        _ssd_fwd_kernel,
        grid=(n_seq, ns),
        in_specs=([pl.BlockSpec((rows, SSD_WIDTH), lambda b, i: (cur(b, i), C_Z // SSD_WIDTH))] + data_specs
                  + [pl.BlockSpec((1, cps, SSD_STATE, SSD_WIDTH), lambda b, i: (b, i, 0, 0)), state_spec]
                  + const_specs + [_resident(dvec.shape), _resident(nw.shape)]),
        out_specs=[pl.BlockSpec((rows, SSD_WIDTH), lambda b, i: (b * ns + i, 0)), final_spec],
        out_shape=[jax.ShapeDtypeStruct((n_seq * seq_len, SSD_WIDTH), BF16),
                   jax.ShapeDtypeStruct((n_seq, SSD_WIDTH, SSD_STATE), F32)],
        scratch_shapes=scratch,
        compiler_params=_cparams("parallel", "arbitrary"),
        name="ssd_forward",
    )(proj, xbc, xbc, proj, hb_start, h0_f, dt_bias, a_log, tri, expand, dvec, nw)
    return y, hf_fin, hb_fin


def _ssd_consts(l, ssd_conv_w, ssd_conv_b, ssd_dt_bias, ssd_A_log, ssd_D, ssd_norm):
    q = SSD_CHUNK
    cw, cb = ssd_conv_w[l], ssd_conv_b[l]
    pad = DT_PAD - 2 * SSD_HEADS
    dt_bias = jnp.pad(ssd_dt_bias[l].reshape(1, -1), ((0, 0), (0, pad)))
    a_log = jnp.pad(ssd_A_log[l].reshape(1, -1), ((0, 0), (0, pad)))
    r = np.arange(q)
    tri = jnp.asarray(np.stack([r[None, :] <= r[:, None], r[None, :] >= r[:, None]]), BF16)
    head_of_col = np.arange(SSD_WIDTH) // SSD_HEAD_DIM
    j = np.arange(DT_PAD)
    expand = jnp.asarray(np.stack([j[:, None] == head_of_col[None, :],
                                   j[:, None] == head_of_col[None, :] + SSD_HEADS]), BF16)
    dvec = jnp.repeat(ssd_D[l], SSD_HEAD_DIM).reshape(1, SSD_WIDTH)
    return (cw[:, :SSD_WIDTH], cb[:SSD_WIDTH].reshape(1, -1), cw[:, SSD_WIDTH:], cb[SSD_WIDTH:].reshape(1, -1),
            dt_bias, a_log, tri, expand, dvec, ssd_norm[l].reshape(1, -1))


def _ssd_state_in(s):
    b = s.shape[0]
    return jnp.transpose(s, (0, 3, 1, 2)).reshape(b, SSD_STATE, SSD_WIDTH)


def _s5_table_kernel(pw_ref, ct_ref, bb_ref, bbt_ref, sel_ref, tile_ref, rhs_ref, woff_ref):
    t = S5_T
    c = S5_CH
    tile = tile_ref[...]
    spread = lambda x, sel: _dot_sel_rhs(x, sel, 2)
    krows, offs, cols = [], [], []
    for d in range(2):
        pw_re, pw_im = pw_ref[0, 2 * d], pw_ref[0, 2 * d + 1]
        c_re = spread(ct_ref[0, :, (2 * d) * c:(2 * d + 1) * c], tile)
        c_im = spread(ct_ref[0, :, (2 * d + 1) * c:(2 * d + 2) * c], tile)
        b_re = spread(bb_ref[0, :, (2 * d) * c:(2 * d + 1) * c], tile)
        b_im = spread(bb_ref[0, :, (2 * d + 1) * c:(2 * d + 2) * c], tile)
        bt_re = bbt_ref[0, (2 * d) * c:(2 * d + 1) * c, :]
        bt_im = bbt_ref[0, (2 * d + 1) * c:(2 * d + 2) * c, :]
        up = (spread(pw_re, sel_ref[0]), spread(pw_im, sel_ref[0]))
        down = (spread(pw_re, sel_ref[1]), spread(pw_im, sel_ref[1]))
        (q_re, q_im), state = ((up, down), (down, up))[d]
        g_re = q_re * c_re - q_im * c_im
        g_im = q_re * c_im + q_im * c_re
        krows.append(_dot_wide(bt_re, g_re) - _dot_wide(bt_im, g_im))
        lb_re, lb_im = pw_re[:, 1:2], pw_im[:, 1:2]
        r_re = q_re * lb_re - q_im * lb_im
        r_im = q_re * lb_im + q_im * lb_re
        offs.append((r_re * c_re - r_im * c_im, -(r_re * c_im + r_im * c_re)))
        s_re, s_im = state
        cols.append(((s_re * b_re - s_im * b_im).T, (s_re * b_im + s_im * b_re).T))
    k_f, k_b = krows
    for s in range(t):
        r = t - 1 - s
        f = k_f if s == 0 else jnp.concatenate([jnp.zeros((c, c * s), F32), k_f[:, :S5_TW - c * s]], axis=1)
        b = k_b if r == 0 else jnp.concatenate([k_b[:, c * r:], jnp.zeros((c, c * r), F32)], axis=1)
        rhs_ref[0, s * c:(s + 1) * c, 0:S5_TW] = (f + b).astype(BF16)
    rhs_ref[0, :, S5_TW:] = jnp.concatenate([cols[0][0], cols[1][0], cols[0][1], cols[1][1]], axis=1).astype(BF16)
    woff_ref[0] = jnp.concatenate([offs[0][0], offs[1][0], offs[0][1], offs[1][1]], axis=0).astype(BF16)


def _s5_tables(s5_A_re, s5_A_im, s5_log_dt, s5_B_re, s5_B_im, s5_C_re, s5_C_im):
    t = S5_T
    n = N_LAYERS * S5_GROUPS
    step = jnp.exp(s5_log_dt)[..., None]
    k = jnp.arange(LANES, dtype=F32)
    keep = k <= t
    kk = jnp.where(keep, k, 0.0)
    mag = jnp.exp(kk * (s5_A_re * step)[..., None])
    ang = kk * (s5_A_im * step)[..., None]
    pw_re = jnp.where(keep, mag * jnp.cos(ang), 0.0)
    pw_im = jnp.where(keep, mag * jnp.sin(ang), 0.0)
    lb_re, lb_im = pw_re[..., 1], pw_im[..., 1]
    den = s5_A_re * s5_A_re + s5_A_im * s5_A_im
    r_re = ((lb_re - 1.0) * s5_A_re + lb_im * s5_A_im) / den
    r_im = (lb_im * s5_A_re - (lb_re - 1.0) * s5_A_im) / den
    b_re, b_im = s5_B_re[:, None], s5_B_im[:, None]
    bb_re = r_re[..., None] * b_re - r_im[..., None] * b_im
    bb_im = r_re[..., None] * b_im + r_im[..., None] * b_re
    c_re = jnp.swapaxes(s5_C_re, -1, -2)
    c_im = jnp.swapaxes(s5_C_im, -1, -2)

    def pack(re, im, axis):
        return jnp.concatenate([re[:, 0], im[:, 0], re[:, 1], im[:, 1]], axis=axis)

    pw = jnp.stack([pw_re[:, 0], pw_im[:, 0], pw_re[:, 1], pw_im[:, 1]], axis=2).reshape(n, 4, S5_P, LANES)
    ct = pack(c_re, c_im, -1).reshape(n, S5_P, 4 * S5_CH)
    bb = pack(bb_re, bb_im, -1)
    bbt = jnp.swapaxes(bb, -1, -2).reshape(n, 4 * S5_CH, S5_P)
    bb = bb.reshape(n, S5_P, 4 * S5_CH)
    lam_t = jnp.concatenate([pw_re[:, 0, :, :, t], pw_re[:, 1, :, :, t], pw_im[:, 0, :, :, t],
                             pw_im[:, 1, :, :, t]], axis=-1).reshape(n, 1, S5_ST)
    jt = np.arange(S5_TW) // S5_CH
    lane_k = np.arange(LANES)[:, None]
    sel = jnp.asarray(np.stack([lane_k == f[None, :] for f in (jt, t - 1 - jt)]), BF16)
    tile = jnp.asarray(np.arange(S5_CH)[:, None] == (np.arange(S5_TW) % S5_CH)[None, :], BF16)
    grp = lambda shape: pl.BlockSpec((1,) + shape, lambda i: (i,) + (0,) * len(shape))
    rhs, woff = pl.pallas_call(
        _s5_table_kernel,
        grid=(n,),
        in_specs=[grp((4, S5_P, LANES)), grp((S5_P, 4 * S5_CH)), grp((S5_P, 4 * S5_CH)), grp((4 * S5_CH, S5_P)),
                  _resident(sel.shape), _resident(tile.shape)],
        out_specs=[grp((S5_TW, S5_TW + S5_ST)), grp((S5_ST, S5_TW))],
        out_shape=[jax.ShapeDtypeStruct((n, S5_TW, S5_TW + S5_ST), BF16),
                   jax.ShapeDtypeStruct((n, S5_ST, S5_TW), BF16)],
        compiler_params=_cparams("parallel"),
        name="s5_tables",
    )(pw, ct, bb, bbt, sel, tile)
    return rhs, woff, lam_t


def _unit_transpose(vs, masks, axis, unit):
    size = vs[0].shape[axis]
    cur = list(vs)
    for d in (1, 2, 4):
        nxt = list(cur)
        for a in range(S5_OCT):
            if a & d:
                continue
            b = a | d
            nxt[a] = jnp.where(masks[d], cur[a], pltpu.roll(cur[b], unit * d, axis=axis))
            nxt[b] = jnp.where(masks[d], pltpu.roll(cur[a], size - unit * d, axis=axis), cur[b])
        cur = nxt
    return cur


def _s5_kernel(u_ref, rhs_ref, woff_ref, lam_ref, h0_ref, y_ref, hfin_ref,
               dst_scr, y_scr, sre_scr, sim_scr, hfre_scr, hfim_scr, hbre_scr, hbim_scr, *, nc, rb):
    t = S5_T
    rows = dst_scr.shape[1]
    sb = rows // nc
    tiles = t // S5_OCT
    nr = rb // S5_OCT
    tile_shape = (nr, S5_OCT, LANES)
    seg = lax.broadcasted_iota(jnp.int32, tile_shape, 2) // S5_CH
    sub = lax.broadcasted_iota(jnp.int32, tile_shape, 1)
    lane_masks = {d: (seg & d) == 0 for d in (1, 2, 4)}
    sub_masks = {d: (sub & d) == 0 for d in (1, 2, 4)}

    def tile_rows(i, r, tq):
        return pl.ds((i * rb + r) * tiles + tq, nr, stride=S5_OCT * tiles)

    def gather(i, carry):
        r0 = pl.multiple_of(i * rb, rb)
        for tq in range(tiles):
            by_chunk = [u_ref[tile_rows(i, r, tq)] for r in range(S5_OCT)]
            by_time = _unit_transpose(by_chunk, sub_masks, 1, 1)
            for gl, v in enumerate(_unit_transpose(by_time, lane_masks, 2, S5_CH)):
                dst_scr[gl, pl.ds(r0, rb), tq * LANES:(tq + 1) * LANES] = v.reshape(rb, LANES).astype(BF16)
        return carry

    lax.fori_loop(0, rows // rb, gather, 0)

    for gl in range(S5_OCT):
        z = _dot(dst_scr[gl], rhs_ref[gl])
        y_scr[gl] = z[:, 0:S5_TW]
        sre_scr[gl] = z[:, S5_TW:S5_TW + 2 * S5_P]
        sim_scr[gl] = z[:, S5_TW + 2 * S5_P:]

    fwd_lane = lax.broadcasted_iota(jnp.int32, (sb, 2 * S5_P), 1) < S5_P
    a_re = [lam_ref[gl][:, 0:2 * S5_P] for gl in range(S5_OCT)]
    a_im = [lam_ref[gl][:, 2 * S5_P:] for gl in range(S5_OCT)]

    def step(i, carry):
        fi = pl.ds(i, sb, stride=nc)
        bj = pl.ds(nc - 1 - i, sb, stride=nc)
        out = []
        for gl in range(S5_OCT):
            h_re, h_im = carry[2 * gl], carry[2 * gl + 1]
            hfre_scr[gl, fi, :] = h_re
            hfim_scr[gl, fi, :] = h_im
            hbre_scr[gl, bj, :] = h_re
            hbim_scr[gl, bj, :] = h_im
            s_re = jnp.where(fwd_lane, sre_scr[gl, fi, :], sre_scr[gl, bj, :])
            s_im = jnp.where(fwd_lane, sim_scr[gl, fi, :], sim_scr[gl, bj, :])
            out.append(a_re[gl] * h_re - a_im[gl] * h_im + s_re)
            out.append(a_re[gl] * h_im + a_im[gl] * h_re + s_im)
        return tuple(out)

    init = []
    for gl in range(S5_OCT):
        h0 = h0_ref[gl, 0]
        init += [h0[:, 0:2 * S5_P], h0[:, 2 * S5_P:]]
    fin = lax.fori_loop(0, nc, step, tuple(init), unroll=2 if nc % 2 == 0 else 1)

    all_fwd = lax.broadcasted_iota(jnp.int32, (rows, 2 * S5_P), 1) < S5_P
    for gl in range(S5_OCT):
        hfin_ref[gl, 0] = jnp.concatenate([fin[2 * gl], fin[2 * gl + 1]], axis=-1)
        hin = jnp.concatenate([jnp.where(all_fwd, hfre_scr[gl], hbre_scr[gl]),
                               jnp.where(all_fwd, hfim_scr[gl], hbim_scr[gl])], axis=-1).astype(BF16)
        y_scr[gl] += _dot(hin, woff_ref[gl])

    def scatter(i, carry):
        r0 = pl.multiple_of(i * rb, rb)
        for tq in range(tiles):
            by_group = [y_scr[gl, pl.ds(r0, rb), tq * LANES:(tq + 1) * LANES].reshape(tile_shape)
                        for gl in range(S5_OCT)]
            by_time = _unit_transpose(by_group, lane_masks, 2, S5_CH)
            for r, v in enumerate(_unit_transpose(by_time, sub_masks, 1, 1)):
                y_ref[tile_rows(i, r, tq)] = v
        return carry

    lax.fori_loop(0, rows // rb, scatter, 0)


def _s5_mixer(proj, tables, layer, h0, *, n_seq, seq_len):
    rhs, woff, lam_t = tables
    t = S5_T
    nc = seq_len // t
    sb = max(1, min(n_seq, S5_MAX_ROWS // nc))
    assert n_seq % sb == 0
    nbs = n_seq // sb
    rows = sb * nc
    rb = min(rows, 32)
    noct = S5_GROUPS // S5_OCT
    ucol = C_U // LANES
    tab = lambda shape: pl.BlockSpec((S5_OCT,) + shape, lambda j, s: (layer * noct + j, 0, 0))
    state = pl.BlockSpec((S5_OCT, 1, sb, S5_ST), lambda j, s: (j, s, 0, 0))
    rows_scr = lambda dt: pltpu.VMEM((S5_OCT, rows, 2 * S5_P), dt)
    n_tok = n_seq * seq_len
    tok_block = (sb * seq_len // S5_OCT, S5_OCT, LANES)
    y, hfin = pl.pallas_call(
        functools.partial(_s5_kernel, nc=nc, rb=rb),
        grid=(noct, nbs),
        in_specs=[pl.BlockSpec(tok_block, lambda j, s: (s, 0, ucol + j)),
                  tab(rhs.shape[1:]), tab(woff.shape[1:]), tab(lam_t.shape[1:]), state],
        out_specs=[pl.BlockSpec(tok_block, lambda j, s: (s, 0, j)), state],
        out_shape=[jax.ShapeDtypeStruct((n_tok // S5_OCT, S5_OCT, S5_WIDTH), F32),
                   jax.ShapeDtypeStruct((S5_GROUPS, nbs, sb, S5_ST), F32)],
        scratch_shapes=[pltpu.VMEM((S5_OCT, rows, S5_TW), BF16), pltpu.VMEM((S5_OCT, rows, S5_TW), F32),
                        rows_scr(F32), rows_scr(F32), rows_scr(F32), rows_scr(F32), rows_scr(F32), rows_scr(F32)],
        compiler_params=_cparams("parallel", "parallel"),
        name="s5_mixer",
    )(proj.reshape(n_tok // S5_OCT, S5_OCT, D_INP), rhs, woff, lam_t, h0.reshape(S5_GROUPS, nbs, sb, S5_ST))
    return y.reshape(n_tok, S5_WIDTH), hfin.reshape(S5_GROUPS, n_seq, S5_ST)


def _s5_state_in(re, im):
    b = re.shape[0]
    x = jnp.stack([re, im], axis=1)
    return jnp.transpose(x, (3, 0, 1, 2, 4)).reshape(S5_GROUPS, b, S5_ST)


def _s5_state_out(h):
    n_seq = h.shape[1]
    x = h.reshape(S5_GROUPS, n_seq, 2, 2, S5_P)
    x = jnp.transpose(x, (2, 1, 3, 0, 4))
    return x[0], x[1]


def kernel(x_prompt, x_sample, c, cache_k, cache_v, state_ssd, state_s5_re, state_s5_im, c_ctx, w_ada, b_ada, norm_mix_pre, norm_mix_post, norm_ffn_pre, norm_ffn_post, w_in, w_out, ssd_conv_w, ssd_conv_b, ssd_dt_bias, ssd_A_log, ssd_D, ssd_norm, attn_sink, s5_A_re, s5_A_im, s5_log_dt, s5_B_re, s5_B_im, s5_C_re, s5_C_im, s5_D, s5_w_glu, s5_b_glu, w_ffn_in, w_ffn_out):
    nb_ctx, len_ctx, _ = x_prompt.shape
    nb_lat, len_lat, _ = x_sample.shape
    n_ctx = nb_ctx * len_ctx
    n_lat = nb_lat * len_lat
    tm = 512
    mod_rows = -(-(1 + nb_lat) // 8) * 8
    cvecs = jnp.concatenate([c_ctx[None, :], c, jnp.zeros((mod_rows - 1 - nb_lat, D_MODEL), F32)], axis=0)
    mods = _ada(cvecs, w_ada, b_ada)
    mod_geom = ((n_ctx, 0), (len_lat, 1))
    token_sets = tuple(dict(mod_spec=_mod_spec(tm, *g)) for g in mod_geom)
    tm_up = min(FFN_UP_ROWS, n_ctx, len_lat)
    rope_args = (dict(), dict(rope_tables=_rope_tables(len_lat), seq_len=len_lat))

    s5_tab = _s5_tables(s5_A_re, s5_A_im, s5_log_dt, s5_B_re, s5_B_im, s5_C_re, s5_C_im)
    w_in_p = _permute_w_in(w_in)
    w_out_b = w_out.astype(BF16)
    w_glu_b = s5_w_glu.astype(BF16)
    w_ffn_in_b = w_ffn_in.astype(BF16)
    w_ffn_out_b = w_ffn_out.astype(BF16)
    zeros_ssd = jnp.zeros((nb_ctx, SSD_STATE, SSD_WIDTH), F32)
    zeros_s5 = jnp.zeros((S5_GROUPS, nb_ctx, S5_ST), F32)

    xs = [x_prompt.reshape(n_ctx, D_MODEL), x_sample.reshape(n_lat, D_MODEL)]
    new_k, new_v, new_ssd, new_s5_re, new_s5_im = [], [], [], [], []
    for l in range(N_LAYERS):
        mod3 = mods[l].reshape(mod_rows, 1, 6 * D_MODEL)
        proj, qv, kt = zip(*[_in_proj(x, mod3, norm_mix_pre[l].reshape(1, -1), w_in_p, layer=l, tm=tm, **ts, **ra)
                             for x, ts, ra in zip(xs, token_sets, rope_args)])

        sink = attn_sink[l]
        ckt = jnp.swapaxes(cache_k[:, l].reshape(nb_lat, -1, KV_WIDTH), 1, 2).astype(BF16)
        cv = cache_v[:, l].reshape(nb_lat, -1, KV_WIDTH).astype(BF16)
        o_att = [_ctx_attention(qv[0], kt[0], sink, n_seq=nb_ctx, seq_len=len_ctx),
                 _lat_attention(qv[1], kt[1], sink, ckt, cv, n_seq=nb_lat, seq_len=len_lat)]

        consts = _ssd_consts(l, ssd_conv_w, ssd_conv_b, ssd_dt_bias, ssd_A_log, ssd_D, ssd_norm)
        y_ctx, hf_ctx, hb_ctx = _ssd_mixer(proj[0], zeros_ssd, zeros_ssd, consts, n_seq=nb_ctx, seq_len=len_ctx)
        y_lat, _, _ = _ssd_mixer(proj[1], _ssd_state_in(state_ssd[:, l, 0]), _ssd_state_in(state_ssd[:, l, 1]),
                                 consts, n_seq=nb_lat, seq_len=len_lat)
        y_ssd = [y_ctx, y_lat]

        s5_ctx, s5_fin = _s5_mixer(proj[0], s5_tab, l, zeros_s5, n_seq=nb_ctx, seq_len=len_ctx)
        s5_lat, _ = _s5_mixer(proj[1], s5_tab, l, _s5_state_in(state_s5_re[:, l], state_s5_im[:, l]),
                              n_seq=nb_lat, seq_len=len_lat)
        y_s5 = [s5_ctx, s5_lat]

        xs = [_out_proj(x, mod3, norm_mix_post[l].reshape(1, -1), y_ssd[i], o_att[i], y_s5[i], proj[i],
                        s5_D[l].reshape(1, -1), w_glu_b, s5_b_glu[l].reshape(1, -1), w_out_b,
                        layer=l, tm=tm, **ts)
              for i, (x, ts) in enumerate(zip(xs, token_sets))]
        xs = [_ffn(x, mod3, norm_ffn_pre[l].reshape(1, -1), norm_ffn_post[l].reshape(1, -1),
                   w_ffn_in_b, w_ffn_out_b, layer=l, tm_up=tm_up, tf=FFN_UP_COLS, tm_down=tm,
                   mod_spec_up=_mod_spec(tm_up, *g), mod_spec_down=_mod_spec(tm, *g))
              for x, g in zip(xs, mod_geom)]

        new_k.append(proj[0][:, C_K:C_K + KV_WIDTH].reshape(nb_ctx, len_ctx, ATT_KV, HEAD_DIM))
        new_v.append(proj[0][:, C_V:C_V + KV_WIDTH].reshape(nb_ctx, len_ctx, ATT_KV, HEAD_DIM))
        new_ssd += [hf_ctx, hb_ctx]
        re, im = _s5_state_out(s5_fin)
        new_s5_re.append(re)
        new_s5_im.append(im)

    y_prompt = xs[0].reshape(nb_ctx, len_ctx, D_MODEL)
    y_sample = xs[1].reshape(nb_lat, len_lat, D_MODEL)
    new_state_ssd = jnp.stack(new_ssd, axis=1).reshape(nb_ctx, N_LAYERS, 2, SSD_HEADS, SSD_HEAD_DIM, SSD_STATE)
    return (y_prompt, y_sample, jnp.stack(new_k, axis=1), jnp.stack(new_v, axis=1), new_state_ssd,
            jnp.stack(new_s5_re, axis=1), jnp.stack(new_s5_im, axis=1))
```
